```python
import jax
import jax.numpy as jnp
from jax import lax
import numpy as np

D_MODEL = 2048
BATCH = 2
SEQ = 4096
DEPTH = 1
DEC_BATCH = 32
DEC_SEQ = 4
PAST_LEN = 8192
PAGE_SIZE = 128

N_HEADS_A = 16
HEAD_DIM_A = 64
KV_GROUPS_A = 2
CMP_BLOCK = 32
CMP_STRIDE = 16
CMP_HIDDEN = 256
SLC_BLOCK = 64
N_SELECT = 16
WINDOW = 512
N_HEADS_B = 8
HEAD_DIM_B = 128
IDX_HEADS = 16
IDX_DIM = 64
DSA_TOPK = 256
MEM_TOKENS = 256
MEM_HEADS = 4
MEM_HEAD_DIM = 128
D_FF = 5632
CONV_WIDTH = 3
ROPE_THETA = 500000.0
ROT_FRACTION = 4
Q_BLOCK = 128
EPS = 1e-6

HPG_A = N_HEADS_A // KV_GROUPS_A
KV_A = KV_GROUPS_A * HEAD_DIM_A
CMP_PER_SLC = SLC_BLOCK // CMP_STRIDE
DSA_ROW = 2 * HEAD_DIM_B + IDX_DIM
IN_SIZES = (N_HEADS_A * HEAD_DIM_A, 6 * KV_A, 3 * N_HEADS_A, N_HEADS_B * HEAD_DIM_B, 2 * HEAD_DIM_B,
            IDX_HEADS * IDX_DIM, IDX_DIM, IDX_HEADS, 2 * D_MODEL)
IN_OFFSETS = tuple(sum(IN_SIZES[:i + 1]) for i in range(len(IN_SIZES) - 1))
D_IN = sum(IN_SIZES)

kernel_name = 'nsa_dsa_gated_hybrid_step'


def rmsnorm(x, g):
    xf = x.astype(jnp.float32)
    y = xf * lax.rsqrt(jnp.mean(xf * xf, axis=-1, keepdims=True) + EPS)
    return (y * g.astype(jnp.float32)).astype(x.dtype)


def rope_partial(x, pos):
    d = x.shape[-1]
    rot = d // ROT_FRACTION
    half = rot // 2
    inv = ROPE_THETA ** (-jnp.arange(half, dtype=jnp.float32) / half)
    ang = pos.astype(jnp.float32)[:, None] * inv[None, :]
    ang = ang.reshape((ang.shape[0],) + (1,) * (x.ndim - 3) + (half,))
    cos = jnp.cos(ang).astype(x.dtype)
    sin = jnp.sin(ang).astype(x.dtype)
    x1 = x[..., :half]
    x2 = x[..., half:rot]
    return jnp.concatenate([x1 * cos - x2 * sin, x2 * cos + x1 * sin, x[..., rot:]], axis=-1)


def masked_softmax(s, mask):
    s = jnp.where(mask, s, -jnp.inf)
    m = jnp.max(s, axis=-1, keepdims=True)
    m = jnp.where(jnp.isfinite(m), m, 0.0)
    e = jnp.exp(s - m)
    return e / jnp.maximum(jnp.sum(e, axis=-1, keepdims=True), 1e-30)


def pad_rows(a, n):
    return jnp.pad(a, [(0, 0), (0, n - a.shape[1])] + [(0, 0)] * (a.ndim - 2))


def gather_pages(pool, page_table):
    pages = pool[page_table]
    return pages.reshape((pages.shape[0], pages.shape[1] * pages.shape[2]) + pages.shape[3:])


def compress(raw, pe, w1, b1, w2):
    b, n, g, d = raw.shape
    nc = -(-n // CMP_STRIDE)
    ratio = CMP_BLOCK // CMP_STRIDE
    chunks = pad_rows(raw, (nc + ratio - 1) * CMP_STRIDE).reshape(b, nc + ratio - 1, CMP_STRIDE, g, d)
    blocks = jnp.concatenate([chunks[:, j:j + nc] for j in range(ratio)], axis=2) + pe[:, None, :]
    flat = jnp.swapaxes(blocks, 2, 3).reshape(b, nc, g, CMP_BLOCK * d)
    return jax.nn.gelu(flat @ w1 + b1) @ w2


def project(xn, w_in, pos):
    b, t, _ = xn.shape
    qa, kva, ga, qb, kvb, qi, ki, wi, gm = jnp.split(xn @ w_in, IN_OFFSETS, axis=-1)
    qa = qa.reshape(b, t, N_HEADS_A, HEAD_DIM_A)
    kva = kva.reshape(b, t, 6, KV_GROUPS_A, HEAD_DIM_A)
    kvb = kvb.reshape(b, t, 2, HEAD_DIM_B)
    gm = jax.nn.sigmoid(gm.reshape(b, t, 2, D_MODEL))
    return {
        'q_cmp': qa, 'q_rot': rope_partial(qa, pos),
        'k_cmp': kva[:, :, 0], 'v_cmp': kva[:, :, 1],
        'k_slc': rope_partial(kva[:, :, 2], pos), 'v_slc': kva[:, :, 3],
        'k_win': rope_partial(kva[:, :, 4], pos), 'v_win': kva[:, :, 5],
        'g_nsa': jax.nn.sigmoid(ga.reshape(b, t, 3, N_HEADS_A)),
        'q_b': rope_partial(qb.reshape(b, t, N_HEADS_B, HEAD_DIM_B), pos),
        'k_b': rope_partial(kvb[:, :, 0], pos), 'v_b': kvb[:, :, 1],
        'q_idx': rope_partial(qi.reshape(b, t, IDX_HEADS, IDX_DIM), pos),
        'k_idx': rope_partial(ki, pos), 'w_idx': wi,
        'g_a': gm[:, :, 0], 'g_b': gm[:, :, 1],
    }


def nsa_global(q_c, q_r, qpos, kc, vc, ks, vs):
    b, t, h, d = q_c.shape
    g = kc.shape[2]
    hpg = h // g
    nc = kc.shape[1]
    scale = d ** -0.5
    s = jnp.einsum('btghd,bngd->btghn', q_c.reshape(b, t, g, hpg, d), kc).astype(jnp.float32) * scale
    blk_last = jnp.arange(nc, dtype=jnp.int32) * CMP_STRIDE + (CMP_BLOCK - 1)
    p = masked_softmax(s, (blk_last[None, :] <= qpos[:, None])[None, :, None, None, :])
    o_cmp = jnp.einsum('btghn,bngd->btghd', p.astype(vc.dtype), vc)
    ns = ks.shape[1] // SLC_BLOCK
    imp = jnp.pad(p.sum(axis=3), ((0, 0), (0, 0), (0, 0), (0, ns * CMP_PER_SLC - nc)))
    imp = imp.reshape(b, t, g, ns, CMP_PER_SLC)
    score = imp.sum(-1) + jnp.pad(imp[..., -1], ((0, 0), (0, 0), (0, 0), (1, 0)))[..., :ns]
    blk = jnp.arange(ns, dtype=jnp.int32)[None, :]
    cur = (qpos // SLC_BLOCK)[:, None]
    visible = blk * SLC_BLOCK <= qpos[:, None]
    forced = (blk == 0) | (blk == cur) | (blk == cur - 1)
    score = jnp.where(forced[None, :, None, :], jnp.inf, jnp.where(visible[None, :, None, :], score, -jnp.inf))
    _, sel = lax.top_k(score, min(N_SELECT, ns))
    tok = (sel[..., None] * SLC_BLOCK + jnp.arange(SLC_BLOCK, dtype=jnp.int32)).reshape(b, t, g, -1)
    bi = jnp.arange(b)[:, None, None, None]
    gi = jnp.arange(g)[None, None, :, None]
    kg = ks[bi, tok, gi]
    vg = vs[bi, tok, gi]
    s2 = jnp.einsum('btghd,btgmd->btghm', q_r.reshape(b, t, g, hpg, d), kg).astype(jnp.float32) * scale
    p2 = masked_softmax(s2, (tok <= qpos[None, :, None, None])[:, :, :, None, :])
    o_slc = jnp.einsum('btghm,btgmd->btghd', p2.astype(vg.dtype), vg)
    return o_cmp.reshape(b, t, h, d), o_slc.reshape(b, t, h, d)


def window_attend(q, k, v, qpos, kpos):
    b, t, h, d = q.shape
    g = k.shape[2]
    s = jnp.einsum('btghd,bkgd->btghk', q.reshape(b, t, g, h // g, d), k).astype(jnp.float32) * d ** -0.5
    dist = qpos[:, None] - kpos[None, :]
    mask = (dist >= 0) & (dist < WINDOW) & (kpos >= 0)[None, :]
    p = masked_softmax(s, mask[None, :, None, None, :])
    return jnp.einsum('btghk,bkgd->btghd', p.astype(v.dtype), v).reshape(b, t, h, d)


def dsa_attend(q, q_idx, w_idx, qpos, k, v, k_idx, n_keep):
    b, t, h, d = q.shape
    n = k.shape[1]
    dots = jnp.einsum('bthd,bsd->bths', q_idx, k_idx).astype(jnp.float32) * IDX_DIM ** -0.5
    score = jnp.einsum('bth,bths->bts', w_idx.astype(jnp.float32) * IDX_HEADS ** -0.5, jax.nn.relu(dots))
    vis = jnp.arange(n, dtype=jnp.int32)[None, :] <= qpos[:, None]
    _, idx = lax.top_k(jnp.where(vis[None], score, -jnp.inf), n_keep)
    bi = jnp.arange(b)[:, None, None]
    kg = k[bi, idx]
    vg = v[bi, idx]
    s = jnp.einsum('bthd,btkd->bthk', q, kg).astype(jnp.float32) * d ** -0.5
    p = masked_softmax(s, (idx <= qpos[None, :, None])[:, :, None, :])
    return jnp.einsum('bthk,btkd->bthd', p.astype(vg.dtype), vg)


def nsa_combine(g, o_cmp, o_slc, o_win):
    return g[:, :, 0, :, None] * o_cmp + g[:, :, 1, :, None] * o_slc + g[:, :, 2, :, None] * o_win


def merge_branches(o_nsa, o_dsa, g_a, g_b, w_oa, w_ob, w_o):
    b, t = o_nsa.shape[:2]
    ya = o_nsa.reshape(b, t, -1) @ w_oa
    yb = o_dsa.reshape(b, t, -1) @ w_ob
    return (g_a * ya + g_b * yb) @ w_o


def mixer_prompt(xn, w_in, cmp_pe, cmp_w1, cmp_b1, cmp_w2, w_oa, w_ob, w_o):
    b, s, _ = xn.shape
    pos = jnp.arange(s, dtype=jnp.int32)
    p = project(xn, w_in, pos)
    kc = compress(p['k_cmp'], cmp_pe[0], cmp_w1[0], cmp_b1[0], cmp_w2[0])
    vc = compress(p['v_cmp'], cmp_pe[1], cmp_w1[1], cmp_b1[1], cmp_w2[1])
    n_rows = -(-s // SLC_BLOCK) * SLC_BLOCK
    ks = pad_rows(p['k_slc'], n_rows)
    vs = pad_rows(p['v_slc'], n_rows)
    nb = s // Q_BLOCK
    nback = -(-WINDOW // Q_BLOCK)

    def band(a):
        ap = jnp.pad(a, [(0, 0), (nback * Q_BLOCK, 0)] + [(0, 0)] * (a.ndim - 2))
        ap = ap.reshape((b, nb + nback, Q_BLOCK) + a.shape[2:])
        return jnp.moveaxis(jnp.concatenate([ap[:, j:j + nb] for j in range(nback + 1)], axis=2), 1, 0)

    kpos_p = jnp.arange(-nback * Q_BLOCK, s, dtype=jnp.int32).reshape(nb + nback, Q_BLOCK)
    kpos_band = jnp.concatenate([kpos_p[j:j + nb] for j in range(nback + 1)], axis=1)
    n_keep = min(DSA_TOPK, s // 4)

    def blocks(a):
        return jnp.moveaxis(a.reshape((b, nb, Q_BLOCK) + a.shape[2:]), 1, 0)

    def block_fn(xs):
        qc, qr, g, qb, qi, wi, kw, vw, qpos, kpos = xs
        o_cmp, o_slc = nsa_global(qc, qr, qpos, kc, vc, ks, vs)
        o_win = window_attend(qr, kw, vw, qpos, kpos)
        o_dsa = dsa_attend(qb, qi, wi, qpos, p['k_b'], p['v_b'], p['k_idx'], n_keep)
        return nsa_combine(g, o_cmp, o_slc, o_win), o_dsa

    xs = (blocks(p['q_cmp']), blocks(p['q_rot']), blocks(p['g_nsa']), blocks(p['q_b']), blocks(p['q_idx']),
          blocks(p['w_idx']), band(p['k_win']), band(p['v_win']), pos.reshape(nb, Q_BLOCK), kpos_band)
    o_nsa, o_dsa = lax.map(block_fn, xs)

    def unblock(a):
        return jnp.moveaxis(a, 0, 1).reshape((b, s) + a.shape[3:])

    y = merge_branches(unblock(o_nsa), unblock(o_dsa), p['g_a'], p['g_b'], w_oa, w_ob, w_o)
    nsa_rows = jnp.stack([p['k_cmp'], p['v_cmp'], p['k_slc'], p['v_slc']], axis=2)
    win_state = jnp.stack([p['k_win'], p['v_win']], axis=2)[:, -min(WINDOW, s):]
    dsa_rows = jnp.concatenate([p['k_b'], p['v_b'], p['k_idx']], axis=-1)
    return y, nsa_rows, win_state, dsa_rows


def mixer_sample(xn, cache_nsa, win_buf, cache_dsa, page_table, w_in, cmp_pe, cmp_w1, cmp_b1, cmp_w2,
                 w_oa, w_ob, w_o):
    b, t, _ = xn.shape
    pos = PAST_LEN + jnp.arange(t, dtype=jnp.int32)
    p = project(xn, w_in, pos)
    nsa_rows = jnp.stack([p['k_cmp'], p['v_cmp'], p['k_slc'], p['v_slc']], axis=2)
    full = jnp.concatenate([gather_pages(cache_nsa, page_table), nsa_rows], axis=1)
    n_keys = full.shape[1]
    kc = compress(full[:, :, 0], cmp_pe[0], cmp_w1[0], cmp_b1[0], cmp_w2[0])
    vc = compress(full[:, :, 1], cmp_pe[1], cmp_w1[1], cmp_b1[1], cmp_w2[1])
    n_rows = -(-n_keys // SLC_BLOCK) * SLC_BLOCK
    o_cmp, o_slc = nsa_global(p['q_cmp'], p['q_rot'], pos, kc, vc,
                              pad_rows(full[:, :, 2], n_rows), pad_rows(full[:, :, 3], n_rows))
    w_len = win_buf.shape[1]
    win_all = jnp.concatenate([win_buf, jnp.stack([p['k_win'], p['v_win']], axis=2)], axis=1)
    kpos = jnp.concatenate([PAST_LEN - w_len + jnp.arange(w_len, dtype=jnp.int32), pos])
    o_win = window_attend(p['q_rot'], win_all[:, :, 0], win_all[:, :, 1], pos, kpos)
    dsa_rows = jnp.concatenate([p['k_b'], p['v_b'], p['k_idx']], axis=-1)
    dfull = jnp.concatenate([gather_pages(cache_dsa, page_table), dsa_rows], axis=1)
    o_dsa = dsa_attend(p['q_b'], p['q_idx'], p['w_idx'], pos, dfull[..., :HEAD_DIM_B],
                       dfull[..., HEAD_DIM_B:2 * HEAD_DIM_B], dfull[..., 2 * HEAD_DIM_B:], min(DSA_TOPK, n_keys // 4))
    y = merge_branches(nsa_combine(p['g_nsa'], o_cmp, o_slc, o_win), o_dsa, p['g_a'], p['g_b'], w_oa, w_ob, w_o)
    return y, nsa_rows, win_all[:, -w_len:], dsa_rows


def mem_kv(mem, g, w_kv):
    b, m, _ = mem.shape
    return (rmsnorm(mem, g) @ w_kv).reshape(b, m, 2, MEM_HEADS, MEM_HEAD_DIM)


def mem_attend(hn, kv, w_q, w_o):
    b, t, _ = hn.shape
    q = (hn @ w_q).reshape(b, t, MEM_HEADS, MEM_HEAD_DIM)
    s = jnp.einsum('bthd,bmhd->bthm', q, kv[:, :, 0]).astype(jnp.float32) * MEM_HEAD_DIM ** -0.5
    p = jax.nn.softmax(s, axis=-1).astype(kv.dtype)
    o = jnp.einsum('bthm,bmhd->bthd', p, kv[:, :, 1])
    return o.reshape(b, t, -1) @ w_o


def conv_ffn(hn, prev, w_up, conv_w, conv_b, w_down):
    t = hn.shape[1]
    ext = jnp.concatenate([prev, hn @ w_up], axis=1)
    c = conv_b + sum(ext[:, j:j + t] * conv_w[j] for j in range(CONV_WIDTH))
    gate, up = jnp.split(c, 2, axis=-1)
    return (jax.nn.silu(gate) * up) @ w_down, ext[:, t:]


def setup_inputs(seed: int = 0) -> dict:
    key = jax.random.key(seed)
    ks = jax.random.split(key, 27)
    n_pages = PAST_LEN // PAGE_SIZE
    used = DEC_BATCH * n_pages
    pool = (5 * used + 3) // 4
    w_len = min(WINDOW, PAST_LEN)

    def nrm(k, shape, scale=1.0):
        return jax.random.normal(k, shape, jnp.float32) * scale

    return {
        'x_prompt': nrm(ks[0], (BATCH, SEQ, D_MODEL)),
        'x_sample': nrm(ks[1], (DEC_BATCH, DEC_SEQ, D_MODEL)),
        'mem_prompt': nrm(ks[2], (BATCH, MEM_TOKENS, D_MODEL)),
        'cache_nsa_kv': nrm(ks[3], (DEPTH, pool, PAGE_SIZE, 4, KV_GROUPS_A, HEAD_DIM_A)),
        'state_nsa_win': nrm(ks[4], (DEPTH, DEC_BATCH, w_len, 2, KV_GROUPS_A, HEAD_DIM_A)),
        'cache_dsa_kv': nrm(ks[5], (DEPTH, pool, PAGE_SIZE, DSA_ROW)),
        'cache_mem_kv': nrm(ks[6], (DEPTH, DEC_BATCH, MEM_TOKENS, 2, MEM_HEADS, MEM_HEAD_DIM)),
        'state_conv': nrm(ks[7], (DEPTH, DEC_BATCH, CONV_WIDTH - 1, 2 * D_FF)),
        'page_table': jax.random.permutation(ks[8], pool)[:used].reshape(DEC_BATCH, n_pages).astype(jnp.int32),
        'norm_g': 1.0 + nrm(ks[9], (DEPTH, 4, D_MODEL), 0.02),
        'w_in': nrm(ks[10], (DEPTH, D_MODEL, D_IN), D_MODEL ** -0.5),
        'cmp_pe': nrm(ks[11], (DEPTH, 2, CMP_BLOCK, HEAD_DIM_A), 0.1),
        'cmp_w1': nrm(ks[12], (DEPTH, 2, CMP_BLOCK * HEAD_DIM_A, CMP_HIDDEN), (CMP_BLOCK * HEAD_DIM_A) ** -0.5),
        'cmp_b1': nrm(ks[13], (DEPTH, 2, CMP_HIDDEN), 0.01),
        'cmp_w2': nrm(ks[14], (DEPTH, 2, CMP_HIDDEN, HEAD_DIM_A), CMP_HIDDEN ** -0.5),
        'w_out_a': nrm(ks[15], (DEPTH, N_HEADS_A * HEAD_DIM_A, D_MODEL), (N_HEADS_A * HEAD_DIM_A) ** -0.5),
        'w_out_b': nrm(ks[16], (DEPTH, N_HEADS_B * HEAD_DIM_B, D_MODEL), (N_HEADS_B * HEAD_DIM_B) ** -0.5),
        'w_out': nrm(ks[17], (DEPTH, D_MODEL, D_MODEL), D_MODEL ** -0.5),
        'w_mem_q': nrm(ks[18], (DEPTH, D_MODEL, MEM_HEADS * MEM_HEAD_DIM), D_MODEL ** -0.5),
        'w_mem_kv': nrm(ks[19], (DEPTH, D_MODEL, 2 * MEM_HEADS * MEM_HEAD_DIM), D_MODEL ** -0.5),
        'w_mem_out': nrm(ks[20], (DEPTH, MEM_HEADS * MEM_HEAD_DIM, D_MODEL), (MEM_HEADS * MEM_HEAD_DIM) ** -0.5),
        'w_up': nrm(ks[21], (DEPTH, D_MODEL, 2 * D_FF), D_MODEL ** -0.5),
        'conv_w': nrm(ks[22], (DEPTH, CONV_WIDTH, 2 * D_FF), CONV_WIDTH ** -0.5),
        'conv_b': nrm(ks[23], (DEPTH, 2 * D_FF), 0.01),
        'w_down': nrm(ks[24], (DEPTH, D_FF, D_MODEL), D_FF ** -0.5),
        'final_g': 1.0 + nrm(ks[25], (D_MODEL,), 0.02),
    }


def reference(x_prompt, x_sample, mem_prompt, cache_nsa_kv, state_nsa_win, cache_dsa_kv, cache_mem_kv, state_conv,
              page_table, norm_g, w_in, cmp_pe, cmp_w1, cmp_b1, cmp_w2, w_out_a, w_out_b, w_out, w_mem_q, w_mem_kv,
              w_mem_out, w_up, conv_w, conv_b, w_down, final_g):
    xp, xs = x_prompt, x_sample
    nsa_p, nsa_s, win_p, win_s, dsa_p, dsa_s, mem_p, conv_p, conv_s = [], [], [], [], [], [], [], [], []
    for l in range(DEPTH):
        yp, a, bwin, c = mixer_prompt(rmsnorm(xp, norm_g[l, 0]), w_in[l], cmp_pe[l], cmp_w1[l], cmp_b1[l], cmp_w2[l],
                                      w_out_a[l], w_out_b[l], w_out[l])
        nsa_p.append(a); win_p.append(bwin); dsa_p.append(c)
        ys, a, bwin, c = mixer_sample(rmsnorm(xs, norm_g[l, 0]), cache_nsa_kv[l], state_nsa_win[l], cache_dsa_kv[l],
                                      page_table, w_in[l], cmp_pe[l], cmp_w1[l], cmp_b1[l], cmp_w2[l],
                                      w_out_a[l], w_out_b[l], w_out[l])
        nsa_s.append(a); win_s.append(bwin); dsa_s.append(c)
        xp = xp + yp
        xs = xs + ys
        kv_p = mem_kv(mem_prompt, norm_g[l, 2], w_mem_kv[l])
        mem_p.append(kv_p)
        xp = xp + mem_attend(rmsnorm(xp, norm_g[l, 1]), kv_p, w_mem_q[l], w_mem_out[l])
        xs = xs + mem_attend(rmsnorm(xs, norm_g[l, 1]), cache_mem_kv[l], w_mem_q[l], w_mem_out[l])
        fp, cp = conv_ffn(rmsnorm(xp, norm_g[l, 3]), jnp.zeros((xp.shape[0], CONV_WIDTH - 1, 2 * D_FF), xp.dtype),
                          w_up[l], conv_w[l], conv_b[l], w_down[l])
        fs, cs = conv_ffn(rmsnorm(xs, norm_g[l, 3]), state_conv[l], w_up[l], conv_w[l], conv_b[l], w_down[l])
        conv_p.append(cp); conv_s.append(cs)
        xp = xp + fp
        xs = xs + fs
    y_prompt = rmsnorm(xp, final_g)
    y_sample = rmsnorm(xs, final_g)
    return (y_prompt, y_sample, jnp.stack(nsa_p), jnp.stack(nsa_s), jnp.stack(win_p), jnp.stack(win_s),
            jnp.stack(dsa_p), jnp.stack(dsa_s), jnp.stack(mem_p), jnp.stack(conv_p), jnp.stack(conv_s))
```

```python
import functools

import jax
import jax.numpy as jnp
from jax import lax
from jax.experimental import pallas as pl
from jax.experimental.pallas import tpu as pltpu

N_HEADS_A = 16
HEAD_DIM_A = 64
KV_GROUPS_A = 2
CMP_BLOCK = 32
CMP_STRIDE = 16
SLC_BLOCK = 64
N_SELECT = 16
WINDOW = 512
N_HEADS_B = 8
HEAD_DIM_B = 128
IDX_HEADS = 16
IDX_DIM = 64
DSA_TOPK = 256
MEM_HEADS = 4
MEM_HEAD_DIM = 128
CONV_WIDTH = 3
ROPE_THETA = 500000.0
ROT_FRACTION = 4
Q_BLOCK = 128
EPS = 1e-6
KV_A = KV_GROUPS_A * HEAD_DIM_A
CMP_PER_SLC = SLC_BLOCK // CMP_STRIDE

LANE = 128
VMEM_LIMIT = 48 * 1024 * 1024


def _tile(n, cap):
    if n <= cap:
        return n
    best = None
    for t in range(LANE, cap + 1, LANE):
        if n % t == 0:
            best = t
    assert best is not None, (n, cap)
    return best


def _mm_kernel(x_ref, w_ref, o_ref, acc_ref):
    @pl.when(pl.program_id(2) == 0)
    def _():
        acc_ref[...] = jnp.zeros_like(acc_ref)

    acc_ref[...] += jnp.dot(x_ref[...], w_ref[...], preferred_element_type=jnp.float32)

    @pl.when(pl.program_id(2) == pl.num_programs(2) - 1)
    def _():
        o_ref[...] = acc_ref[...].astype(o_ref.dtype)


def matmul(x, w, out_dtype=jnp.float32):
    lead = x.shape[:-1]
    k = x.shape[-1]
    n = w.shape[-1]
    x2 = x.reshape(-1, k).astype(jnp.bfloat16)
    w2 = w.astype(jnp.bfloat16)
    m0 = x2.shape[0]
    tm = 512 if m0 >= 512 else -(-m0 // 16) * 16
    m = -(-m0 // tm) * tm
    if m != m0:
        x2 = jnp.pad(x2, ((0, m - m0), (0, 0)))
    tn = _tile(n, 1024)
    tk = _tile(k, 2048)
    out = pl.pallas_call(
        _mm_kernel,
        grid=(m // tm, n // tn, k // tk),
        in_specs=[pl.BlockSpec((tm, tk), lambda i, j, l: (i, l)),
                  pl.BlockSpec((tk, tn), lambda i, j, l: (l, j))],
        out_specs=pl.BlockSpec((tm, tn), lambda i, j, l: (i, j)),
        out_shape=jax.ShapeDtypeStruct((m, n), out_dtype),
        scratch_shapes=[pltpu.VMEM((tm, tn), jnp.float32)],
        compiler_params=pltpu.CompilerParams(
            dimension_semantics=("parallel", "parallel", "arbitrary"),
            vmem_limit_bytes=VMEM_LIMIT),
    )(x2, w2)
    return out[:m0].reshape(lead + (n,))


def rmsnorm(x, g):
    xf = x.astype(jnp.float32)
    y = xf * lax.rsqrt(jnp.mean(xf * xf, axis=-1, keepdims=True) + EPS)
    return (y * g.astype(jnp.float32)).astype(x.dtype)


def rope_partial(x, pos):
    d = x.shape[-1]
    rot = d // ROT_FRACTION
    half = rot // 2
    inv = ROPE_THETA ** (-jnp.arange(half, dtype=jnp.float32) / half)
    ang = pos.astype(jnp.float32)[:, None] * inv[None, :]
    ang = ang.reshape((ang.shape[0],) + (1,) * (x.ndim - 3) + (half,))
    cos = jnp.cos(ang).astype(x.dtype)
    sin = jnp.sin(ang).astype(x.dtype)
    x1 = x[..., :half]
    x2 = x[..., half:rot]
    return jnp.concatenate([x1 * cos - x2 * sin, x2 * cos + x1 * sin, x[..., rot:]], axis=-1)


def masked_softmax(s, mask):
    s = jnp.where(mask, s, -jnp.inf)
    m = jnp.max(s, axis=-1, keepdims=True)
    m = jnp.where(jnp.isfinite(m), m, 0.0)
    e = jnp.exp(s - m)
    return e / jnp.maximum(jnp.sum(e, axis=-1, keepdims=True), 1e-30)


def pad_rows(a, n):
    return jnp.pad(a, [(0, 0), (0, n - a.shape[1])] + [(0, 0)] * (a.ndim - 2))


def gather_pages(pool, page_table):
    pages = pool[page_table]
    return pages.reshape((pages.shape[0], pages.shape[1] * pages.shape[2]) + pages.shape[3:])


def compress(raw, pe, w1, b1, w2):
    b, n, g, d = raw.shape
    nc = -(-n // CMP_STRIDE)
    ratio = CMP_BLOCK // CMP_STRIDE
    chunks = pad_rows(raw, (nc + ratio - 1) * CMP_STRIDE).reshape(b, nc + ratio - 1, CMP_STRIDE, g, d)
    blocks = jnp.concatenate([chunks[:, j:j + nc] for j in range(ratio)], axis=2) + pe[:, None, :]
    flat = jnp.swapaxes(blocks, 2, 3).reshape(b, nc, g, CMP_BLOCK * d)
    return matmul(jax.nn.gelu(matmul(flat, w1) + b1), w2)


def project(xn, w_in, pos, sizes):
    b, t, d_model = xn.shape
    offsets = tuple(sum(sizes[:i + 1]) for i in range(len(sizes) - 1))
    qa, kva, ga, qb, kvb, qi, ki, wi, gm = jnp.split(matmul(xn, w_in), offsets, axis=-1)
    qa = qa.reshape(b, t, N_HEADS_A, HEAD_DIM_A)
    kva = kva.reshape(b, t, 6, KV_GROUPS_A, HEAD_DIM_A)
    kvb = kvb.reshape(b, t, 2, HEAD_DIM_B)
    gm = jax.nn.sigmoid(gm.reshape(b, t, 2, d_model))
    return {
        'q_cmp': qa, 'q_rot': rope_partial(qa, pos),
        'k_cmp': kva[:, :, 0], 'v_cmp': kva[:, :, 1],
        'k_slc': rope_partial(kva[:, :, 2], pos), 'v_slc': kva[:, :, 3],
        'k_win': rope_partial(kva[:, :, 4], pos), 'v_win': kva[:, :, 5],
        'g_nsa': jax.nn.sigmoid(ga.reshape(b, t, 3, N_HEADS_A)),
        'q_b': rope_partial(qb.reshape(b, t, N_HEADS_B, HEAD_DIM_B), pos),
        'k_b': rope_partial(kvb[:, :, 0], pos), 'v_b': kvb[:, :, 1],
        'q_idx': rope_partial(qi.reshape(b, t, IDX_HEADS, IDX_DIM), pos),
        'k_idx': rope_partial(ki, pos), 'w_idx': wi,
        'g_a': gm[:, :, 0], 'g_b': gm[:, :, 1],
    }


def nsa_global(q_c, q_r, qpos, kc, vc, ks, vs):
    b, t, h, d = q_c.shape
    g = kc.shape[2]
    hpg = h // g
    nc = kc.shape[1]
    scale = d ** -0.5
    s = jnp.einsum('btghd,bngd->btghn', q_c.reshape(b, t, g, hpg, d), kc).astype(jnp.float32) * scale
    blk_last = jnp.arange(nc, dtype=jnp.int32) * CMP_STRIDE + (CMP_BLOCK - 1)
    p = masked_softmax(s, (blk_last[None, :] <= qpos[:, None])[None, :, None, None, :])
    o_cmp = jnp.einsum('btghn,bngd->btghd', p.astype(vc.dtype), vc)
    ns = ks.shape[1] // SLC_BLOCK
    imp = jnp.pad(p.sum(axis=3), ((0, 0), (0, 0), (0, 0), (0, ns * CMP_PER_SLC - nc)))
    imp = imp.reshape(b, t, g, ns, CMP_PER_SLC)
    score = imp.sum(-1) + jnp.pad(imp[..., -1], ((0, 0), (0, 0), (0, 0), (1, 0)))[..., :ns]
    blk = jnp.arange(ns, dtype=jnp.int32)[None, :]
    cur = (qpos // SLC_BLOCK)[:, None]
    visible = blk * SLC_BLOCK <= qpos[:, None]
    forced = (blk == 0) | (blk == cur) | (blk == cur - 1)
    score = jnp.where(forced[None, :, None, :], jnp.inf, jnp.where(visible[None, :, None, :], score, -jnp.inf))
    _, sel = lax.top_k(score, min(N_SELECT, ns))
    tok = (sel[..., None] * SLC_BLOCK + jnp.arange(SLC_BLOCK, dtype=jnp.int32)).reshape(b, t, g, -1)
    bi = jnp.arange(b)[:, None, None, None]
    gi = jnp.arange(g)[None, None, :, None]
    kg = ks[bi, tok, gi]
    vg = vs[bi, tok, gi]
    s2 = jnp.einsum('btghd,btgmd->btghm', q_r.reshape(b, t, g, hpg, d), kg).astype(jnp.float32) * scale
    p2 = masked_softmax(s2, (tok <= qpos[None, :, None, None])[:, :, :, None, :])
    o_slc = jnp.einsum('btghm,btgmd->btghd', p2.astype(vg.dtype), vg)
    return o_cmp.reshape(b, t, h, d), o_slc.reshape(b, t, h, d)


def window_attend(q, k, v, qpos, kpos):
    b, t, h, d = q.shape
    g = k.shape[2]
    s = jnp.einsum('btghd,bkgd->btghk', q.reshape(b, t, g, h // g, d), k).astype(jnp.float32) * d ** -0.5
    dist = qpos[:, None] - kpos[None, :]
    mask = (dist >= 0) & (dist < WINDOW) & (kpos >= 0)[None, :]
    p = masked_softmax(s, mask[None, :, None, None, :])
    return jnp.einsum('btghk,bkgd->btghd', p.astype(v.dtype), v).reshape(b, t, h, d)


def dsa_attend(q, q_idx, w_idx, qpos, k, v, k_idx, n_keep):
    b, t, h, d = q.shape
    n = k.shape[1]
    dots = jnp.einsum('bthd,bsd->bths', q_idx, k_idx).astype(jnp.float32) * IDX_DIM ** -0.5
    score = jnp.einsum('bth,bths->bts', w_idx.astype(jnp.float32) * IDX_HEADS ** -0.5, jax.nn.relu(dots))
    vis = jnp.arange(n, dtype=jnp.int32)[None, :] <= qpos[:, None]
    _, idx = lax.top_k(jnp.where(vis[None], score, -jnp.inf), n_keep)
    bi = jnp.arange(b)[:, None, None]
    kg = k[bi, idx]
    vg = v[bi, idx]
    s = jnp.einsum('bthd,btkd->bthk', q, kg).astype(jnp.float32) * d ** -0.5
    p = masked_softmax(s, (idx <= qpos[None, :, None])[:, :, None, :])
    return jnp.einsum('bthk,btkd->bthd', p.astype(vg.dtype), vg)


def nsa_combine(g, o_cmp, o_slc, o_win):
    return g[:, :, 0, :, None] * o_cmp + g[:, :, 1, :, None] * o_slc + g[:, :, 2, :, None] * o_win


def merge_branches(o_nsa, o_dsa, g_a, g_b, w_oa, w_ob, w_o):
    b, t = o_nsa.shape[:2]
    ya = matmul(o_nsa.reshape(b, t, -1), w_oa)
    yb = matmul(o_dsa.reshape(b, t, -1), w_ob)
    return matmul(g_a * ya + g_b * yb, w_o)


def mixer_prompt(xn, w_in, cmp_pe, cmp_w1, cmp_b1, cmp_w2, w_oa, w_ob, w_o, sizes):
    b, s, _ = xn.shape
    pos = jnp.arange(s, dtype=jnp.int32)
    p = project(xn, w_in, pos, sizes)
    kc = compress(p['k_cmp'], cmp_pe[0], cmp_w1[0], cmp_b1[0], cmp_w2[0])
    vc = compress(p['v_cmp'], cmp_pe[1], cmp_w1[1], cmp_b1[1], cmp_w2[1])
    n_rows = -(-s // SLC_BLOCK) * SLC_BLOCK
    ks = pad_rows(p['k_slc'], n_rows)
    vs = pad_rows(p['v_slc'], n_rows)
    nb = s // Q_BLOCK
    nback = -(-WINDOW // Q_BLOCK)

    def band(a):
        ap = jnp.pad(a, [(0, 0), (nback * Q_BLOCK, 0)] + [(0, 0)] * (a.ndim - 2))
        ap = ap.reshape((b, nb + nback, Q_BLOCK) + a.shape[2:])
        return jnp.moveaxis(jnp.concatenate([ap[:, j:j + nb] for j in range(nback + 1)], axis=2), 1, 0)

    kpos_p = jnp.arange(-nback * Q_BLOCK, s, dtype=jnp.int32).reshape(nb + nback, Q_BLOCK)
    kpos_band = jnp.concatenate([kpos_p[j:j + nb] for j in range(nback + 1)], axis=1)
    n_keep = min(DSA_TOPK, s // 4)

    def blocks(a):
        return jnp.moveaxis(a.reshape((b, nb, Q_BLOCK) + a.shape[2:]), 1, 0)

    def block_fn(xs):
        qc, qr, g, qb, qi, wi, kw, vw, qpos, kpos = xs
        o_cmp, o_slc = nsa_global(qc, qr, qpos, kc, vc, ks, vs)
        o_win = window_attend(qr, kw, vw, qpos, kpos)
        o_dsa = dsa_attend(qb, qi, wi, qpos, p['k_b'], p['v_b'], p['k_idx'], n_keep)
        return nsa_combine(g, o_cmp, o_slc, o_win), o_dsa

    xs = (blocks(p['q_cmp']), blocks(p['q_rot']), blocks(p['g_nsa']), blocks(p['q_b']), blocks(p['q_idx']),
          blocks(p['w_idx']), band(p['k_win']), band(p['v_win']), pos.reshape(nb, Q_BLOCK), kpos_band)
    o_nsa, o_dsa = lax.map(block_fn, xs)

    def unblock(a):
        return jnp.moveaxis(a, 0, 1).reshape((b, s) + a.shape[3:])

    y = merge_branches(unblock(o_nsa), unblock(o_dsa), p['g_a'], p['g_b'], w_oa, w_ob, w_o)
    nsa_rows = jnp.stack([p['k_cmp'], p['v_cmp'], p['k_slc'], p['v_slc']], axis=2)
    win_state = jnp.stack([p['k_win'], p['v_win']], axis=2)[:, -min(WINDOW, s):]
    dsa_rows = jnp.concatenate([p['k_b'], p['v_b'], p['k_idx']], axis=-1)
    return y, nsa_rows, win_state, dsa_rows


def mixer_sample(xn, cache_nsa, win_buf, cache_dsa, page_table, w_in, cmp_pe, cmp_w1, cmp_b1, cmp_w2,
                 w_oa, w_ob, w_o, sizes, past_len):
    b, t, _ = xn.shape
    pos = past_len + jnp.arange(t, dtype=jnp.int32)
    p = project(xn, w_in, pos, sizes)
    nsa_rows = jnp.stack([p['k_cmp'], p['v_cmp'], p['k_slc'], p['v_slc']], axis=2)
    full = jnp.concatenate([gather_pages(cache_nsa, page_table), nsa_rows], axis=1)
    n_keys = full.shape[1]
    kc = compress(full[:, :, 0], cmp_pe[0], cmp_w1[0], cmp_b1[0], cmp_w2[0])
    vc = compress(full[:, :, 1], cmp_pe[1], cmp_w1[1], cmp_b1[1], cmp_w2[1])
    n_rows = -(-n_keys // SLC_BLOCK) * SLC_BLOCK
    o_cmp, o_slc = nsa_global(p['q_cmp'], p['q_rot'], pos, kc, vc,
                              pad_rows(full[:, :, 2], n_rows), pad_rows(full[:, :, 3], n_rows))
    w_len = win_buf.shape[1]
    win_all = jnp.concatenate([win_buf, jnp.stack([p['k_win'], p['v_win']], axis=2)], axis=1)
    kpos = jnp.concatenate([past_len - w_len + jnp.arange(w_len, dtype=jnp.int32), pos])
    o_win = window_attend(p['q_rot'], win_all[:, :, 0], win_all[:, :, 1], pos, kpos)
    dsa_rows = jnp.concatenate([p['k_b'], p['v_b'], p['k_idx']], axis=-1)
    dfull = jnp.concatenate([gather_pages(cache_dsa, page_table), dsa_rows], axis=1)
    o_dsa = dsa_attend(p['q_b'], p['q_idx'], p['w_idx'], pos, dfull[..., :HEAD_DIM_B],
                       dfull[..., HEAD_DIM_B:2 * HEAD_DIM_B], dfull[..., 2 * HEAD_DIM_B:], min(DSA_TOPK, n_keys // 4))
    y = merge_branches(nsa_combine(p['g_nsa'], o_cmp, o_slc, o_win), o_dsa, p['g_a'], p['g_b'], w_oa, w_ob, w_o)
    return y, nsa_rows, win_all[:, -w_len:], dsa_rows


def mem_kv(mem, g, w_kv):
    b, m, _ = mem.shape
    return matmul(rmsnorm(mem, g), w_kv).reshape(b, m, 2, MEM_HEADS, MEM_HEAD_DIM)


def mem_attend(hn, kv, w_q, w_o):
    b, t, _ = hn.shape
    q = matmul(hn, w_q).reshape(b, t, MEM_HEADS, MEM_HEAD_DIM)
    s = jnp.einsum('bthd,bmhd->bthm', q, kv[:, :, 0]).astype(jnp.float32) * MEM_HEAD_DIM ** -0.5
    p = jax.nn.softmax(s, axis=-1).astype(kv.dtype)
    o = jnp.einsum('bthm,bmhd->bthd', p, kv[:, :, 1])
    return matmul(o.reshape(b, t, -1), w_o)


def conv_ffn(hn, prev, w_up, conv_w, conv_b, w_down):
    t = hn.shape[1]
    ext = jnp.concatenate([prev, matmul(hn, w_up)], axis=1)
    c = conv_b + sum(ext[:, j:j + t] * conv_w[j] for j in range(CONV_WIDTH))
    gate, up = jnp.split(c, 2, axis=-1)
    return matmul(jax.nn.silu(gate) * up, w_down), ext[:, t:]


def kernel(x_prompt, x_sample, mem_prompt, cache_nsa_kv, state_nsa_win, cache_dsa_kv, cache_mem_kv, state_conv,
           page_table, norm_g, w_in, cmp_pe, cmp_w1, cmp_b1, cmp_w2, w_out_a, w_out_b, w_out, w_mem_q, w_mem_kv,
           w_mem_out, w_up, conv_w, conv_b, w_down, final_g):
    depth = w_in.shape[0]
    d_model = x_prompt.shape[-1]
    d_ff = w_down.shape[1]
    past_len = page_table.shape[1] * cache_nsa_kv.shape[2]
    sizes = (N_HEADS_A * HEAD_DIM_A, 6 * KV_A, 3 * N_HEADS_A, N_HEADS_B * HEAD_DIM_B, 2 * HEAD_DIM_B,
             IDX_HEADS * IDX_DIM, IDX_DIM, IDX_HEADS, 2 * d_model)
    xp, xs = x_prompt, x_sample
    nsa_p, nsa_s, win_p, win_s, dsa_p, dsa_s, mem_p, conv_p, conv_s = [], [], [], [], [], [], [], [], []
    for l in range(depth):
        yp, a, bwin, c = mixer_prompt(rmsnorm(xp, norm_g[l, 0]), w_in[l], cmp_pe[l], cmp_w1[l], cmp_b1[l], cmp_w2[l],
                                      w_out_a[l], w_out_b[l], w_out[l], sizes)
        nsa_p.append(a); win_p.append(bwin); dsa_p.append(c)
        ys, a, bwin, c = mixer_sample(rmsnorm(xs, norm_g[l, 0]), cache_nsa_kv[l], state_nsa_win[l], cache_dsa_kv[l],
                                      page_table, w_in[l], cmp_pe[l], cmp_w1[l], cmp_b1[l], cmp_w2[l],
                                      w_out_a[l], w_out_b[l], w_out[l], sizes, past_len)
        nsa_s.append(a); win_s.append(bwin); dsa_s.append(c)
        xp = xp + yp
        xs = xs + ys
        kv_p = mem_kv(mem_prompt, norm_g[l, 2], w_mem_kv[l])
        mem_p.append(kv_p)
        xp = xp + mem_attend(rmsnorm(xp, norm_g[l, 1]), kv_p, w_mem_q[l], w_mem_out[l])
        xs = xs + mem_attend(rmsnorm(xs, norm_g[l, 1]), cache_mem_kv[l], w_mem_q[l], w_mem_out[l])
        fp, cp = conv_ffn(rmsnorm(xp, norm_g[l, 3]), jnp.zeros((xp.shape[0], CONV_WIDTH - 1, 2 * d_ff), xp.dtype),
                          w_up[l], conv_w[l], conv_b[l], w_down[l])
        fs, cs = conv_ffn(rmsnorm(xs, norm_g[l, 3]), state_conv[l], w_up[l], conv_w[l], conv_b[l], w_down[l])
        conv_p.append(cp); conv_s.append(cs)
        xp = xp + fp
        xs = xs + fs
    y_prompt = rmsnorm(xp, final_g)
    y_sample = rmsnorm(xs, final_g)
    return (y_prompt, y_sample, jnp.stack(nsa_p), jnp.stack(nsa_s), jnp.stack(win_p), jnp.stack(win_s),
            jnp.stack(dsa_p), jnp.stack(dsa_s), jnp.stack(mem_p), jnp.stack(conv_p), jnp.stack(conv_s))
```

```python
import functools

import jax
import jax.numpy as jnp
from jax import lax
from jax.experimental import pallas as pl
from jax.experimental.pallas import tpu as pltpu

N_HEADS_A = 16
HEAD_DIM_A = 64
KV_GROUPS_A = 2
CMP_BLOCK = 32
CMP_STRIDE = 16
SLC_BLOCK = 64
N_SELECT = 16
WINDOW = 512
N_HEADS_B = 8
HEAD_DIM_B = 128
IDX_HEADS = 16
IDX_DIM = 64
DSA_TOPK = 256
MEM_HEADS = 4
MEM_HEAD_DIM = 128
CONV_WIDTH = 3
ROPE_THETA = 500000.0
ROT_FRACTION = 4
Q_BLOCK = 128
EPS = 1e-6
KV_A = KV_GROUPS_A * HEAD_DIM_A
CMP_PER_SLC = SLC_BLOCK // CMP_STRIDE

LANE = 128
VMEM_LIMIT = 48 * 1024 * 1024


def _tile(n, cap):
    if n <= cap:
        return n
    best = None
    for t in range(LANE, cap + 1, LANE):
        if n % t == 0:
            best = t
    assert best is not None, (n, cap)
    return best


def _mm_kernel(x_ref, w_ref, o_ref, acc_ref):
    @pl.when(pl.program_id(2) == 0)
    def _():
        acc_ref[...] = jnp.zeros_like(acc_ref)

    acc_ref[...] += jnp.dot(x_ref[...], w_ref[...], preferred_element_type=jnp.float32)

    @pl.when(pl.program_id(2) == pl.num_programs(2) - 1)
    def _():
        o_ref[...] = acc_ref[...].astype(o_ref.dtype)


def matmul(x, w, out_dtype=jnp.float32):
    lead = x.shape[:-1]
    k = x.shape[-1]
    n = w.shape[-1]
    x2 = x.reshape(-1, k).astype(MXU_DTYPE)
    w2 = w.astype(MXU_DTYPE)
    m0 = x2.shape[0]
    tm = 512 if m0 >= 512 else -(-m0 // 16) * 16
    m = -(-m0 // tm) * tm
    if m != m0:
        x2 = jnp.pad(x2, ((0, m - m0), (0, 0)))
    tn = _tile(n, 1024)
    tk = _tile(k, 2048)
    out = pl.pallas_call(
        _mm_kernel,
        grid=(m // tm, n // tn, k // tk),
        in_specs=[pl.BlockSpec((tm, tk), lambda i, j, l: (i, l)),
                  pl.BlockSpec((tk, tn), lambda i, j, l: (l, j))],
        out_specs=pl.BlockSpec((tm, tn), lambda i, j, l: (i, j)),
        out_shape=jax.ShapeDtypeStruct((m, n), out_dtype),
        scratch_shapes=[pltpu.VMEM((tm, tn), jnp.float32)],
        compiler_params=pltpu.CompilerParams(
            dimension_semantics=("parallel", "parallel", "arbitrary"),
            vmem_limit_bytes=VMEM_LIMIT),
    )(x2, w2)
    return out[:m0].reshape(lead + (n,))


NEG_INF = float('-inf')
POS_INF = float('inf')
MXU_DTYPE = jnp.bfloat16
HPG_A = N_HEADS_A // KV_GROUPS_A
TQ = 128
KEY_CHUNK = 512
BISECT_ITERS = 32
SLC_SHIFT = SLC_BLOCK.bit_length() - 1
CMP_PER_SLC_SHIFT = CMP_PER_SLC.bit_length() - 1


def _dot_nt(a, b):
    return lax.dot_general(a, b, (((1,), (1,)), ((), ())), preferred_element_type=jnp.float32)


def _iota(shape, dim):
    return lax.broadcasted_iota(jnp.int32, shape, dim)


def _flash_init(rows, d):
    return (jnp.full((rows, 1), NEG_INF, jnp.float32), jnp.zeros((rows, 1), jnp.float32),
            jnp.zeros((rows, d), jnp.float32))


def _flash_step(carry, q, k, v, madd, nh, scale=None):
    m, l, acc = carry
    s = _dot_nt(q, k)
    if scale is not None:
        s = s * scale
    r, kb = s.shape
    s = (s.reshape(nh, r // nh, kb) + madd[None]).reshape(r, kb)
    m_new = jnp.maximum(m, jnp.max(s, axis=1, keepdims=True))
    m_safe = jnp.where(m_new == NEG_INF, 0.0, m_new)
    p = jnp.exp(s - m_safe)
    alpha = jnp.exp(m - m_safe)
    l = alpha * l + jnp.sum(p, axis=1, keepdims=True)
    acc = alpha * acc + jnp.dot(p.astype(v.dtype), v, preferred_element_type=jnp.float32)
    return m_new, l, acc


def _flash_finish(carry):
    _, l, acc = carry
    return acc / jnp.maximum(l, 1e-30)


def _split_dot(x, m01):
    hi = x.astype(jnp.bfloat16)
    r1 = x - hi.astype(jnp.float32)
    mid = r1.astype(jnp.bfloat16)
    lo = (r1 - mid.astype(jnp.float32)).astype(jnp.bfloat16)
    dot = functools.partial(jnp.dot, preferred_element_type=jnp.float32)
    return dot(hi, m01) + dot(mid, m01) + dot(lo, m01)


def _nsa_prompt_kernel(qc_ref, qr_ref, gate_ref, kc_ref, vc_ref, ks_ref, vs_ref, kw_ref, vw_ref, o_ref, mask_ref,
                       *, tq, kb, seq):
    i = pl.program_id(2)
    nh, d = HPG_A, HEAD_DIM_A
    rows = nh * tq
    nc = kc_ref.shape[2]
    ns = seq // SLC_BLOCK
    n_sel = min(N_SELECT, ns)
    t0 = i * tq
    nj = (t0 + tq + kb - 1) // kb
    qc = qc_ref[0].reshape(rows, d)
    qr = qr_ref[0].reshape(rows, d)
    tpos = t0 + _iota((tq, 1), 0)

    blk_last = _iota((1, nc), 1) * CMP_STRIDE + (CMP_BLOCK - 1)
    madd_c = jnp.where(blk_last <= tpos, 0.0, NEG_INF)
    s = _dot_nt(qc, kc_ref[0, 0]).reshape(nh, tq, nc) + madd_c[None]
    m = jnp.max(s, axis=2, keepdims=True)
    m = jnp.where(m == NEG_INF, 0.0, m)
    e = jnp.exp(s - m)
    p = e / jnp.maximum(jnp.sum(e, axis=2, keepdims=True), 1e-30)
    o_cmp = jnp.dot(p.reshape(rows, nc).astype(vc_ref.dtype), vc_ref[0, 0], preferred_element_type=jnp.float32)

    imp = jnp.sum(p, axis=0)
    c_id = _iota((nc, ns), 0)
    m_id = _iota((nc, ns), 1)
    overlap = (jnp.right_shift(c_id, CMP_PER_SLC_SHIFT) == m_id) | (c_id == m_id * CMP_PER_SLC - 1)
    score = _split_dot(imp, overlap.astype(jnp.bfloat16))
    blk = _iota((1, ns), 1)
    cur = jnp.right_shift(tpos, SLC_SHIFT)
    forced = (blk == 0) | (blk == cur) | (blk == cur - 1)
    sc = jnp.where(forced, POS_INF, jnp.where(blk * SLC_BLOCK <= tpos, score, NEG_INF))
    rank = jnp.zeros((tq, ns), jnp.float32)
    for mp in range(ns):
        col = sc[:, mp:mp + 1]
        beats = (col > sc) | ((col == sc) & (blk > mp))
        rank = rank + jnp.where(beats, 1.0, 0.0)
    sel = jnp.where(rank < n_sel, 1.0, 0.0).astype(jnp.bfloat16)

    def make_mask(j, _):
        kpos = j * kb + _iota((1, kb), 1)
        expand = (jnp.right_shift(j * kb + _iota((ns, kb), 1), SLC_SHIFT) == _iota((ns, kb), 0)).astype(jnp.bfloat16)
        hit = jnp.dot(sel, expand, preferred_element_type=jnp.float32)
        mask_ref[j] = jnp.where((hit > 0.5) & (kpos <= tpos), 0.0, NEG_INF)
        return 0

    lax.fori_loop(0, nj, make_mask, 0)

    def slc_body(j, carry):
        off = pl.multiple_of(j * kb, kb)
        return _flash_step(carry, qr, ks_ref[0, 0, pl.ds(off, kb), :], vs_ref[0, 0, pl.ds(off, kb), :],
                           mask_ref[j], nh)

    o_slc = _flash_finish(lax.fori_loop(0, nj, slc_body, _flash_init(rows, d)))

    def win_body(jb, carry):
        off = pl.multiple_of(jb * tq, tq)
        dist = tpos - (off + _iota((1, tq), 1))
        madd = jnp.where((dist >= 0) & (dist < WINDOW), 0.0, NEG_INF)
        return _flash_step(carry, qr, kw_ref[0, 0, pl.ds(off, tq), :], vw_ref[0, 0, pl.ds(off, tq), :], madd, nh)

    nback = -(-WINDOW // tq)
    o_win = _flash_finish(lax.fori_loop(jnp.maximum(i - nback, 0), i + 1, win_body, _flash_init(rows, d)))

    gates = gate_ref[0, 0]
    outs = []
    for h in range(nh):
        rs = slice(h * tq, (h + 1) * tq)
        outs.append(gates[:, h:h + 1] * o_cmp[rs] + gates[:, nh + h:nh + h + 1] * o_slc[rs]
                    + gates[:, 2 * nh + h:2 * nh + h + 1] * o_win[rs])
    o_ref[0] = jnp.concatenate(outs, axis=1).astype(o_ref.dtype)


def nsa_prompt_attention(p, kc, vc):
    b, s, h, d = p['q_cmp'].shape
    g = KV_GROUPS_A
    tq, kb = TQ, min(KEY_CHUNK, s)
    scale = d ** -0.5

    def heads_first(a):
        return jnp.swapaxes(a, 1, 2).astype(MXU_DTYPE)

    qc = heads_first(p['q_cmp'] * scale)
    qr = heads_first(p['q_rot'] * scale)
    gates = p['g_nsa'].reshape(b, s, 3, g, HPG_A).transpose(0, 3, 1, 2, 4).reshape(b, g, s, 3 * HPG_A)
    kvs = [heads_first(a) for a in (kc, vc, p['k_slc'], p['v_slc'], p['k_win'], p['v_win'])]
    nc = kc.shape[1]
    q_spec = pl.BlockSpec((1, HPG_A, tq, d), lambda bi, gi, i: (bi, gi, i, 0))
    c_spec = pl.BlockSpec((1, 1, nc, d), lambda bi, gi, i: (bi, gi, 0, 0))
    k_spec = pl.BlockSpec((1, 1, s, d), lambda bi, gi, i: (bi, gi, 0, 0))
    return pl.pallas_call(
        functools.partial(_nsa_prompt_kernel, tq=tq, kb=kb, seq=s),
        grid=(b, g, s // tq),
        in_specs=[q_spec, q_spec, pl.BlockSpec((1, 1, tq, 3 * HPG_A), lambda bi, gi, i: (bi, gi, i, 0)),
                  c_spec, c_spec, k_spec, k_spec, k_spec, k_spec],
        out_specs=pl.BlockSpec((1, tq, HPG_A * d), lambda bi, gi, i: (bi, i, gi)),
        out_shape=jax.ShapeDtypeStruct((b, s, h * d), MXU_DTYPE),
        scratch_shapes=[pltpu.VMEM((s // kb, tq, kb), jnp.float32)],
        compiler_params=pltpu.CompilerParams(
            dimension_semantics=("parallel", "parallel", "arbitrary"), vmem_limit_bytes=VMEM_LIMIT),
        name="nsa_prompt_attention",
    )(qc, qr, gates, *kvs)


def _dsa_prompt_kernel(qb_ref, qi_ref, wi_ref, kb_ref, vb_ref, ki_ref, o_ref, score_ref, *, tq, kb, n_keep):
    i = pl.program_id(1)
    nh, d = N_HEADS_B, HEAD_DIM_B
    t0 = i * tq
    nj = (t0 + tq + kb - 1) // kb
    tpos = t0 + _iota((tq, 1), 0)
    w = wi_ref[0]

    def idx_body(j, carry):
        lo, hi = carry
        off = pl.multiple_of(j * kb, kb)
        kidx = ki_ref[0, pl.ds(off, kb), :]
        acc = jnp.zeros((tq, kb), jnp.float32)
        for h in range(IDX_HEADS):
            acc = acc + w[:, h:h + 1] * jnp.maximum(_dot_nt(qi_ref[0, h], kidx), 0.0)
        vis = (off + _iota((1, kb), 1)) <= tpos
        score_ref[j] = jnp.where(vis, acc, NEG_INF)
        lo = jnp.minimum(lo, jnp.min(jnp.where(vis, acc, POS_INF), axis=1, keepdims=True))
        hi = jnp.maximum(hi, jnp.max(jnp.where(vis, acc, NEG_INF), axis=1, keepdims=True))
        return lo, hi

    lo, hi = lax.fori_loop(0, nj, idx_body, (jnp.full((tq, 1), POS_INF, jnp.float32),
                                             jnp.full((tq, 1), NEG_INF, jnp.float32)))

    def bisect(_, carry):
        lo, hi = carry
        mid = 0.5 * (lo + hi)

        def count(j, c):
            return c + jnp.sum(jnp.where(score_ref[j] >= mid, 1.0, 0.0), axis=1, keepdims=True)

        ge = lax.fori_loop(0, nj, count, jnp.zeros((tq, 1), jnp.float32)) >= float(n_keep)
        return jnp.where(ge, mid, lo), jnp.where(ge, hi, mid)

    thr, _ = lax.fori_loop(0, BISECT_ITERS, bisect, (lo, hi))

    q = jnp.concatenate([qb_ref[0, :, h * d:(h + 1) * d] for h in range(nh)], axis=0)

    def att_body(j, carry):
        off = pl.multiple_of(j * kb, kb)
        madd = jnp.where(score_ref[j] >= thr, 0.0, NEG_INF)
        return _flash_step(carry, q, kb_ref[0, pl.ds(off, kb), :], vb_ref[0, pl.ds(off, kb), :], madd, nh,
                           scale=d ** -0.5)

    o = _flash_finish(lax.fori_loop(0, nj, att_body, _flash_init(nh * tq, d)))
    o_ref[0] = jnp.concatenate([o[h * tq:(h + 1) * tq] for h in range(nh)], axis=1).astype(o_ref.dtype)


def dsa_prompt_attention(p, n_keep):
    b, s, nh, d = p['q_b'].shape
    tq, kb = TQ, min(KEY_CHUNK, s)
    qb = p['q_b'].reshape(b, s, nh * d).astype(MXU_DTYPE)
    qi = jnp.swapaxes(p['q_idx'] * IDX_DIM ** -0.5, 1, 2).astype(MXU_DTYPE)
    wi = p['w_idx'].astype(jnp.float32) * IDX_HEADS ** -0.5
    kv_spec = lambda width: pl.BlockSpec((1, s, width), lambda bi, i: (bi, 0, 0))
    return pl.pallas_call(
        functools.partial(_dsa_prompt_kernel, tq=tq, kb=kb, n_keep=n_keep),
        grid=(b, s // tq),
        in_specs=[pl.BlockSpec((1, tq, nh * d), lambda bi, i: (bi, i, 0)),
                  pl.BlockSpec((1, IDX_HEADS, tq, IDX_DIM), lambda bi, i: (bi, 0, i, 0)),
                  pl.BlockSpec((1, tq, IDX_HEADS), lambda bi, i: (bi, i, 0)),
                  kv_spec(d), kv_spec(d), kv_spec(IDX_DIM)],
        out_specs=pl.BlockSpec((1, tq, nh * d), lambda bi, i: (bi, i, 0)),
        out_shape=jax.ShapeDtypeStruct((b, s, nh * d), MXU_DTYPE),
        scratch_shapes=[pltpu.VMEM((s // kb, tq, kb), jnp.float32)],
        compiler_params=pltpu.CompilerParams(
            dimension_semantics=("parallel", "arbitrary"), vmem_limit_bytes=VMEM_LIMIT),
        name="dsa_prompt_attention",
    )(qb, qi, wi, p['k_b'].astype(MXU_DTYPE), p['v_b'].astype(MXU_DTYPE), p['k_idx'].astype(MXU_DTYPE))


def rmsnorm(x, g):
    xf = x.astype(jnp.float32)
    y = xf * lax.rsqrt(jnp.mean(xf * xf, axis=-1, keepdims=True) + EPS)
    return (y * g.astype(jnp.float32)).astype(x.dtype)


def rope_partial(x, pos):
    d = x.shape[-1]
    rot = d // ROT_FRACTION
    half = rot // 2
    inv = ROPE_THETA ** (-jnp.arange(half, dtype=jnp.float32) / half)
    ang = pos.astype(jnp.float32)[:, None] * inv[None, :]
    ang = ang.reshape((ang.shape[0],) + (1,) * (x.ndim - 3) + (half,))
    cos = jnp.cos(ang).astype(x.dtype)
    sin = jnp.sin(ang).astype(x.dtype)
    x1 = x[..., :half]
    x2 = x[..., half:rot]
    return jnp.concatenate([x1 * cos - x2 * sin, x2 * cos + x1 * sin, x[..., rot:]], axis=-1)


def masked_softmax(s, mask):
    s = jnp.where(mask, s, -jnp.inf)
    m = jnp.max(s, axis=-1, keepdims=True)
    m = jnp.where(jnp.isfinite(m), m, 0.0)
    e = jnp.exp(s - m)
    return e / jnp.maximum(jnp.sum(e, axis=-1, keepdims=True), 1e-30)


def pad_rows(a, n):
    return jnp.pad(a, [(0, 0), (0, n - a.shape[1])] + [(0, 0)] * (a.ndim - 2))


def gather_pages(pool, page_table):
    pages = pool[page_table]
    return pages.reshape((pages.shape[0], pages.shape[1] * pages.shape[2]) + pages.shape[3:])


def compress(raw, pe, w1, b1, w2):
    b, n, g, d = raw.shape
    nc = -(-n // CMP_STRIDE)
    ratio = CMP_BLOCK // CMP_STRIDE
    chunks = pad_rows(raw, (nc + ratio - 1) * CMP_STRIDE).reshape(b, nc + ratio - 1, CMP_STRIDE, g, d)
    blocks = jnp.concatenate([chunks[:, j:j + nc] for j in range(ratio)], axis=2) + pe[:, None, :]
    flat = jnp.swapaxes(blocks, 2, 3).reshape(b, nc, g, CMP_BLOCK * d)
    return matmul(jax.nn.gelu(matmul(flat, w1) + b1), w2)


def project(xn, w_in, pos, sizes):
    b, t, d_model = xn.shape
    offsets = tuple(sum(sizes[:i + 1]) for i in range(len(sizes) - 1))
    qa, kva, ga, qb, kvb, qi, ki, wi, gm = jnp.split(matmul(xn, w_in), offsets, axis=-1)
    qa = qa.reshape(b, t, N_HEADS_A, HEAD_DIM_A)
    kva = kva.reshape(b, t, 6, KV_GROUPS_A, HEAD_DIM_A)
    kvb = kvb.reshape(b, t, 2, HEAD_DIM_B)
    gm = jax.nn.sigmoid(gm.reshape(b, t, 2, d_model))
    return {
        'q_cmp': qa, 'q_rot': rope_partial(qa, pos),
        'k_cmp': kva[:, :, 0], 'v_cmp': kva[:, :, 1],
        'k_slc': rope_partial(kva[:, :, 2], pos), 'v_slc': kva[:, :, 3],
        'k_win': rope_partial(kva[:, :, 4], pos), 'v_win': kva[:, :, 5],
        'g_nsa': jax.nn.sigmoid(ga.reshape(b, t, 3, N_HEADS_A)),
        'q_b': rope_partial(qb.reshape(b, t, N_HEADS_B, HEAD_DIM_B), pos),
        'k_b': rope_partial(kvb[:, :, 0], pos), 'v_b': kvb[:, :, 1],
        'q_idx': rope_partial(qi.reshape(b, t, IDX_HEADS, IDX_DIM), pos),
        'k_idx': rope_partial(ki, pos), 'w_idx': wi,
        'g_a': gm[:, :, 0], 'g_b': gm[:, :, 1],
    }


def nsa_global(q_c, q_r, qpos, kc, vc, ks, vs):
    b, t, h, d = q_c.shape
    g = kc.shape[2]
    hpg = h // g
    nc = kc.shape[1]
    scale = d ** -0.5
    s = jnp.einsum('btghd,bngd->btghn', q_c.reshape(b, t, g, hpg, d), kc).astype(jnp.float32) * scale
    blk_last = jnp.arange(nc, dtype=jnp.int32) * CMP_STRIDE + (CMP_BLOCK - 1)
    p = masked_softmax(s, (blk_last[None, :] <= qpos[:, None])[None, :, None, None, :])
    o_cmp = jnp.einsum('btghn,bngd->btghd', p.astype(vc.dtype), vc)
    ns = ks.shape[1] // SLC_BLOCK
    imp = jnp.pad(p.sum(axis=3), ((0, 0), (0, 0), (0, 0), (0, ns * CMP_PER_SLC - nc)))
    imp = imp.reshape(b, t, g, ns, CMP_PER_SLC)
    score = imp.sum(-1) + jnp.pad(imp[..., -1], ((0, 0), (0, 0), (0, 0), (1, 0)))[..., :ns]
    blk = jnp.arange(ns, dtype=jnp.int32)[None, :]
    cur = (qpos // SLC_BLOCK)[:, None]
    visible = blk * SLC_BLOCK <= qpos[:, None]
    forced = (blk == 0) | (blk == cur) | (blk == cur - 1)
    score = jnp.where(forced[None, :, None, :], jnp.inf, jnp.where(visible[None, :, None, :], score, -jnp.inf))
    _, sel = lax.top_k(score, min(N_SELECT, ns))
    tok = (sel[..., None] * SLC_BLOCK + jnp.arange(SLC_BLOCK, dtype=jnp.int32)).reshape(b, t, g, -1)
    bi = jnp.arange(b)[:, None, None, None]
    gi = jnp.arange(g)[None, None, :, None]
    kg = ks[bi, tok, gi]
    vg = vs[bi, tok, gi]
    s2 = jnp.einsum('btghd,btgmd->btghm', q_r.reshape(b, t, g, hpg, d), kg).astype(jnp.float32) * scale
    p2 = masked_softmax(s2, (tok <= qpos[None, :, None, None])[:, :, :, None, :])
    o_slc = jnp.einsum('btghm,btgmd->btghd', p2.astype(vg.dtype), vg)
    return o_cmp.reshape(b, t, h, d), o_slc.reshape(b, t, h, d)


def window_attend(q, k, v, qpos, kpos):
    b, t, h, d = q.shape
    g = k.shape[2]
    s = jnp.einsum('btghd,bkgd->btghk', q.reshape(b, t, g, h // g, d), k).astype(jnp.float32) * d ** -0.5
    dist = qpos[:, None] - kpos[None, :]
    mask = (dist >= 0) & (dist < WINDOW) & (kpos >= 0)[None, :]
    p = masked_softmax(s, mask[None, :, None, None, :])
    return jnp.einsum('btghk,bkgd->btghd', p.astype(v.dtype), v).reshape(b, t, h, d)


def dsa_attend(q, q_idx, w_idx, qpos, k, v, k_idx, n_keep):
    b, t, h, d = q.shape
    n = k.shape[1]
    dots = jnp.einsum('bthd,bsd->bths', q_idx, k_idx).astype(jnp.float32) * IDX_DIM ** -0.5
    score = jnp.einsum('bth,bths->bts', w_idx.astype(jnp.float32) * IDX_HEADS ** -0.5, jax.nn.relu(dots))
    vis = jnp.arange(n, dtype=jnp.int32)[None, :] <= qpos[:, None]
    _, idx = lax.top_k(jnp.where(vis[None], score, -jnp.inf), n_keep)
    bi = jnp.arange(b)[:, None, None]
    kg = k[bi, idx]
    vg = v[bi, idx]
    s = jnp.einsum('bthd,btkd->bthk', q, kg).astype(jnp.float32) * d ** -0.5
    p = masked_softmax(s, (idx <= qpos[None, :, None])[:, :, None, :])
    return jnp.einsum('bthk,btkd->bthd', p.astype(vg.dtype), vg)


def nsa_combine(g, o_cmp, o_slc, o_win):
    return g[:, :, 0, :, None] * o_cmp + g[:, :, 1, :, None] * o_slc + g[:, :, 2, :, None] * o_win


def merge_branches(o_nsa, o_dsa, g_a, g_b, w_oa, w_ob, w_o):
    b, t = o_nsa.shape[:2]
    ya = matmul(o_nsa.reshape(b, t, -1), w_oa)
    yb = matmul(o_dsa.reshape(b, t, -1), w_ob)
    return matmul(g_a * ya + g_b * yb, w_o)


def mixer_prompt(xn, w_in, cmp_pe, cmp_w1, cmp_b1, cmp_w2, w_oa, w_ob, w_o, sizes):
    b, s, _ = xn.shape
    pos = jnp.arange(s, dtype=jnp.int32)
    p = project(xn, w_in, pos, sizes)
    kc = compress(p['k_cmp'], cmp_pe[0], cmp_w1[0], cmp_b1[0], cmp_w2[0])
    vc = compress(p['v_cmp'], cmp_pe[1], cmp_w1[1], cmp_b1[1], cmp_w2[1])
    n_rows = -(-s // SLC_BLOCK) * SLC_BLOCK
    ks = pad_rows(p['k_slc'], n_rows)
    vs = pad_rows(p['v_slc'], n_rows)
    del ks, vs
    o_nsa = nsa_prompt_attention(p, kc, vc)
    o_dsa = dsa_prompt_attention(p, min(DSA_TOPK, s // 4))
    y = merge_branches(o_nsa, o_dsa, p['g_a'], p['g_b'], w_oa, w_ob, w_o)
    nsa_rows = jnp.stack([p['k_cmp'], p['v_cmp'], p['k_slc'], p['v_slc']], axis=2)
    win_state = jnp.stack([p['k_win'], p['v_win']], axis=2)[:, -min(WINDOW, s):]
    dsa_rows = jnp.concatenate([p['k_b'], p['v_b'], p['k_idx']], axis=-1)
    return y, nsa_rows, win_state, dsa_rows


def mixer_sample(xn, cache_nsa, win_buf, cache_dsa, page_table, w_in, cmp_pe, cmp_w1, cmp_b1, cmp_w2,
                 w_oa, w_ob, w_o, sizes, past_len):
    b, t, _ = xn.shape
    pos = past_len + jnp.arange(t, dtype=jnp.int32)
    p = project(xn, w_in, pos, sizes)
    nsa_rows = jnp.stack([p['k_cmp'], p['v_cmp'], p['k_slc'], p['v_slc']], axis=2)
    full = jnp.concatenate([gather_pages(cache_nsa, page_table), nsa_rows], axis=1)
    n_keys = full.shape[1]
    kc = compress(full[:, :, 0], cmp_pe[0], cmp_w1[0], cmp_b1[0], cmp_w2[0])
    vc = compress(full[:, :, 1], cmp_pe[1], cmp_w1[1], cmp_b1[1], cmp_w2[1])
    n_rows = -(-n_keys // SLC_BLOCK) * SLC_BLOCK
    o_cmp, o_slc = nsa_global(p['q_cmp'], p['q_rot'], pos, kc, vc,
                              pad_rows(full[:, :, 2], n_rows), pad_rows(full[:, :, 3], n_rows))
    w_len = win_buf.shape[1]
    win_all = jnp.concatenate([win_buf, jnp.stack([p['k_win'], p['v_win']], axis=2)], axis=1)
    kpos = jnp.concatenate([past_len - w_len + jnp.arange(w_len, dtype=jnp.int32), pos])
    o_win = window_attend(p['q_rot'], win_all[:, :, 0], win_all[:, :, 1], pos, kpos)
    dsa_rows = jnp.concatenate([p['k_b'], p['v_b'], p['k_idx']], axis=-1)
    dfull = jnp.concatenate([gather_pages(cache_dsa, page_table), dsa_rows], axis=1)
    o_dsa = dsa_attend(p['q_b'], p['q_idx'], p['w_idx'], pos, dfull[..., :HEAD_DIM_B],
                       dfull[..., HEAD_DIM_B:2 * HEAD_DIM_B], dfull[..., 2 * HEAD_DIM_B:], min(DSA_TOPK, n_keys // 4))
    y = merge_branches(nsa_combine(p['g_nsa'], o_cmp, o_slc, o_win), o_dsa, p['g_a'], p['g_b'], w_oa, w_ob, w_o)
    return y, nsa_rows, win_all[:, -w_len:], dsa_rows


def mem_kv(mem, g, w_kv):
    b, m, _ = mem.shape
    return matmul(rmsnorm(mem, g), w_kv).reshape(b, m, 2, MEM_HEADS, MEM_HEAD_DIM)


def mem_attend(hn, kv, w_q, w_o):
    b, t, _ = hn.shape
    q = matmul(hn, w_q).reshape(b, t, MEM_HEADS, MEM_HEAD_DIM)
    s = jnp.einsum('bthd,bmhd->bthm', q, kv[:, :, 0]).astype(jnp.float32) * MEM_HEAD_DIM ** -0.5
    p = jax.nn.softmax(s, axis=-1).astype(kv.dtype)
    o = jnp.einsum('bthm,bmhd->bthd', p, kv[:, :, 1])
    return matmul(o.reshape(b, t, -1), w_o)


def conv_ffn(hn, prev, w_up, conv_w, conv_b, w_down):
    t = hn.shape[1]
    ext = jnp.concatenate([prev, matmul(hn, w_up)], axis=1)
    c = conv_b + sum(ext[:, j:j + t] * conv_w[j] for j in range(CONV_WIDTH))
    gate, up = jnp.split(c, 2, axis=-1)
    return matmul(jax.nn.silu(gate) * up, w_down), ext[:, t:]


def kernel(x_prompt, x_sample, mem_prompt, cache_nsa_kv, state_nsa_win, cache_dsa_kv, cache_mem_kv, state_conv,
           page_table, norm_g, w_in, cmp_pe, cmp_w1, cmp_b1, cmp_w2, w_out_a, w_out_b, w_out, w_mem_q, w_mem_kv,
           w_mem_out, w_up, conv_w, conv_b, w_down, final_g):
    depth = w_in.shape[0]
    d_model = x_prompt.shape[-1]
    d_ff = w_down.shape[1]
    past_len = page_table.shape[1] * cache_nsa_kv.shape[2]
    sizes = (N_HEADS_A * HEAD_DIM_A, 6 * KV_A, 3 * N_HEADS_A, N_HEADS_B * HEAD_DIM_B, 2 * HEAD_DIM_B,
             IDX_HEADS * IDX_DIM, IDX_DIM, IDX_HEADS, 2 * d_model)
    xp, xs = x_prompt, x_sample
    nsa_p, nsa_s, win_p, win_s, dsa_p, dsa_s, mem_p, conv_p, conv_s = [], [], [], [], [], [], [], [], []
    for l in range(depth):
        yp, a, bwin, c = mixer_prompt(rmsnorm(xp, norm_g[l, 0]), w_in[l], cmp_pe[l], cmp_w1[l], cmp_b1[l], cmp_w2[l],
                                      w_out_a[l], w_out_b[l], w_out[l], sizes)
        nsa_p.append(a); win_p.append(bwin); dsa_p.append(c)
        ys, a, bwin, c = mixer_sample(rmsnorm(xs, norm_g[l, 0]), cache_nsa_kv[l], state_nsa_win[l], cache_dsa_kv[l],
                                      page_table, w_in[l], cmp_pe[l], cmp_w1[l], cmp_b1[l], cmp_w2[l],
                                      w_out_a[l], w_out_b[l], w_out[l], sizes, past_len)
        nsa_s.append(a); win_s.append(bwin); dsa_s.append(c)
        xp = xp + yp
        xs = xs + ys
        kv_p = mem_kv(mem_prompt, norm_g[l, 2], w_mem_kv[l])
        mem_p.append(kv_p)
        xp = xp + mem_attend(rmsnorm(xp, norm_g[l, 1]), kv_p, w_mem_q[l], w_mem_out[l])
        xs = xs + mem_attend(rmsnorm(xs, norm_g[l, 1]), cache_mem_kv[l], w_mem_q[l], w_mem_out[l])
        fp, cp = conv_ffn(rmsnorm(xp, norm_g[l, 3]), jnp.zeros((xp.shape[0], CONV_WIDTH - 1, 2 * d_ff), xp.dtype),
                          w_up[l], conv_w[l], conv_b[l], w_down[l])
        fs, cs = conv_ffn(rmsnorm(xs, norm_g[l, 3]), state_conv[l], w_up[l], conv_w[l], conv_b[l], w_down[l])
        conv_p.append(cp); conv_s.append(cs)
        xp = xp + fp
        xs = xs + fs
    y_prompt = rmsnorm(xp, final_g)
    y_sample = rmsnorm(xs, final_g)
    return (y_prompt, y_sample, jnp.stack(nsa_p), jnp.stack(nsa_s), jnp.stack(win_p), jnp.stack(win_s),
            jnp.stack(dsa_p), jnp.stack(dsa_s), jnp.stack(mem_p), jnp.stack(conv_p), jnp.stack(conv_s))
```

```python
import functools

import jax
import jax.numpy as jnp
from jax import lax
from jax.experimental import pallas as pl
from jax.experimental.pallas import tpu as pltpu

N_HEADS_A = 16
HEAD_DIM_A = 64
KV_GROUPS_A = 2
CMP_BLOCK = 32
CMP_STRIDE = 16
SLC_BLOCK = 64
N_SELECT = 16
WINDOW = 512
N_HEADS_B = 8
HEAD_DIM_B = 128
IDX_HEADS = 16
IDX_DIM = 64
DSA_TOPK = 256
MEM_HEADS = 4
MEM_HEAD_DIM = 128
CONV_WIDTH = 3
ROPE_THETA = 500000.0
ROT_FRACTION = 4
EPS = 1e-6
KV_A = KV_GROUPS_A * HEAD_DIM_A
CMP_PER_SLC = SLC_BLOCK // CMP_STRIDE
HPG_A = N_HEADS_A // KV_GROUPS_A
DSA_ROW = 2 * HEAD_DIM_B + IDX_DIM

LANE = 128
SUBLANE = 8
BF16_ROWS = 16
VMEM_LIMIT = 48 * 1024 * 1024

NEG_INF = float('-inf')
POS_INF = float('inf')
MXU_DTYPE = jnp.bfloat16
TQ = 128
TQ_STEP = BF16_ROWS
KEY_CHUNK = 512
WIN_CHUNK = 128
PAGES_PER_STEP = 4
BISECT_ITERS = 32
SLC_SHIFT = SLC_BLOCK.bit_length() - 1
CMP_PER_SLC_SHIFT = CMP_PER_SLC.bit_length() - 1


def _round_up(n, m):
    return -(-n // m) * m


def _tile(n, cap):
    if n <= cap:
        return n
    best = None
    for t in range(LANE, cap + 1, LANE):
        if n % t == 0:
            best = t
    assert best is not None, (n, cap)
    return best


def _mm_kernel(x_ref, w_ref, o_ref, acc_ref):
    @pl.when(pl.program_id(2) == 0)
    def _():
        acc_ref[...] = jnp.zeros_like(acc_ref)

    acc_ref[...] += jnp.dot(x_ref[...], w_ref[...], preferred_element_type=jnp.float32)

    @pl.when(pl.program_id(2) == pl.num_programs(2) - 1)
    def _():
        o_ref[...] = acc_ref[...].astype(o_ref.dtype)


def matmul(x, w, out_dtype=jnp.float32):
    lead = x.shape[:-1]
    k = x.shape[-1]
    n = w.shape[-1]
    x2 = x.reshape(-1, k).astype(MXU_DTYPE)
    w2 = w.astype(MXU_DTYPE)
    m0 = x2.shape[0]
    tm = 512 if m0 >= 512 else _round_up(m0, BF16_ROWS)
    m = _round_up(m0, tm)
    if m != m0:
        x2 = jnp.pad(x2, ((0, m - m0), (0, 0)))
    tn = _tile(n, 1024)
    tk = _tile(k, 2048)
    out = pl.pallas_call(
        _mm_kernel,
        grid=(m // tm, n // tn, k // tk),
        in_specs=[pl.BlockSpec((tm, tk), lambda i, j, l: (i, l)),
                  pl.BlockSpec((tk, tn), lambda i, j, l: (l, j))],
        out_specs=pl.BlockSpec((tm, tn), lambda i, j, l: (i, j)),
        out_shape=jax.ShapeDtypeStruct((m, n), out_dtype),
        scratch_shapes=[pltpu.VMEM((tm, tn), jnp.float32)],
        compiler_params=pltpu.CompilerParams(
            dimension_semantics=("parallel", "parallel", "arbitrary"),
            vmem_limit_bytes=VMEM_LIMIT),
        name="matmul",
    )(x2, w2)
    return out[:m0].reshape(lead + (n,))


def _dot_nt(a, b):
    return lax.dot_general(a, b, (((1,), (1,)), ((), ())), preferred_element_type=jnp.float32)


def _iota(shape, dim):
    return lax.broadcasted_iota(jnp.int32, shape, dim)


def _flash_init(rows, d):
    return (jnp.full((rows, 1), NEG_INF, jnp.float32), jnp.zeros((rows, 1), jnp.float32),
            jnp.zeros((rows, d), jnp.float32))


def _flash_step(carry, q, k, v, madd, nh, scale=None):
    m, l, acc = carry
    s = _dot_nt(q, k)
    if scale is not None:
        s = s * scale
    r, kb = s.shape
    s = (s.reshape(nh, r // nh, kb) + madd[None]).reshape(r, kb)
    m_new = jnp.maximum(m, jnp.max(s, axis=1, keepdims=True))
    m_safe = jnp.where(m_new == NEG_INF, 0.0, m_new)
    p = jnp.exp(s - m_safe)
    alpha = jnp.exp(m - m_safe)
    l = alpha * l + jnp.sum(p, axis=1, keepdims=True)
    acc = alpha * acc + jnp.dot(p.astype(v.dtype), v, preferred_element_type=jnp.float32)
    return m_new, l, acc


def _flash_finish(carry):
    _, l, acc = carry
    return acc / jnp.maximum(l, 1e-30)


def _split_dot(x, m01):
    hi = x.astype(jnp.bfloat16)
    r1 = x - hi.astype(jnp.float32)
    mid = r1.astype(jnp.bfloat16)
    lo = (r1 - mid.astype(jnp.float32)).astype(jnp.bfloat16)
    dot = functools.partial(jnp.dot, preferred_element_type=jnp.float32)
    return dot(hi, m01) + dot(mid, m01) + dot(lo, m01)


def _nsa_kernel(qc_ref, qr_ref, gate_ref, kc_ref, vc_ref, ks_ref, vs_ref, kw_ref, vw_ref, o_ref, mask_ref,
                *, tq, kb, qpos0, wbase, ns, n_sel):
    i = pl.program_id(2)
    nh, d = HPG_A, HEAD_DIM_A
    rows = nh * tq
    nc = kc_ref.shape[2]
    nsp = ns if ns <= LANE else _round_up(ns, LANE)
    t0 = qpos0 + i * tq
    nj = (t0 + tq - 1) // kb + 1
    qc = qc_ref[0].reshape(rows, d)
    qr = qr_ref[0].reshape(rows, d)
    tpos = t0 + _iota((tq, 1), 0)

    blk_last = _iota((1, nc), 1) * CMP_STRIDE + (CMP_BLOCK - 1)
    madd_c = jnp.where(blk_last <= tpos, 0.0, NEG_INF)
    s = _dot_nt(qc, kc_ref[0, 0]).reshape(nh, tq, nc) + madd_c[None]
    m = jnp.max(s, axis=2, keepdims=True)
    m = jnp.where(m == NEG_INF, 0.0, m)
    e = jnp.exp(s - m)
    p = e / jnp.maximum(jnp.sum(e, axis=2, keepdims=True), 1e-30)
    o_cmp = jnp.dot(p.reshape(rows, nc).astype(vc_ref.dtype), vc_ref[0, 0], preferred_element_type=jnp.float32)

    imp = jnp.sum(p, axis=0)
    c_id = _iota((nc, nsp), 0)
    m_id = _iota((nc, nsp), 1)
    overlap = (jnp.right_shift(c_id, CMP_PER_SLC_SHIFT) == m_id) | (c_id == m_id * CMP_PER_SLC - 1)
    score = _split_dot(imp, overlap.astype(jnp.bfloat16))
    blk = _iota((1, nsp), 1)
    cur = jnp.right_shift(tpos, SLC_SHIFT)
    forced = (blk == 0) | (blk == cur) | (blk == cur - 1)
    sc = jnp.where(forced, POS_INF, jnp.where(blk * SLC_BLOCK <= tpos, score, NEG_INF))
    rank = jnp.zeros((tq, nsp), jnp.float32)
    for mp in range(ns):
        col = sc[:, mp:mp + 1]
        beats = (col > sc) | ((col == sc) & (blk > mp))
        rank = rank + jnp.where(beats, 1.0, 0.0)
    sel = jnp.where(rank < n_sel, 1.0, 0.0).astype(jnp.bfloat16)

    def make_mask(j, _):
        kpos = j * kb + _iota((1, kb), 1)
        expand = (jnp.right_shift(j * kb + _iota((nsp, kb), 1), SLC_SHIFT) == _iota((nsp, kb), 0))
        hit = jnp.dot(sel, expand.astype(jnp.bfloat16), preferred_element_type=jnp.float32)
        mask_ref[j] = jnp.where((hit > 0.5) & (kpos <= tpos), 0.0, NEG_INF)
        return 0

    lax.fori_loop(0, nj, make_mask, 0)

    def slc_body(j, carry):
        off = pl.multiple_of(j * kb, kb)
        return _flash_step(carry, qr, ks_ref[0, 0, pl.ds(off, kb), :], vs_ref[0, 0, pl.ds(off, kb), :],
                           mask_ref[j], nh)

    o_slc = _flash_finish(lax.fori_loop(0, nj, slc_body, _flash_init(rows, d)))

    wb = WIN_CHUNK

    def win_body(jb, carry):
        off = pl.multiple_of(jb * wb, wb)
        dist = tpos - (wbase + off + _iota((1, wb), 1))
        madd = jnp.where((dist >= 0) & (dist < WINDOW), 0.0, NEG_INF)
        return _flash_step(carry, qr, kw_ref[0, 0, pl.ds(off, wb), :], vw_ref[0, 0, pl.ds(off, wb), :], madd, nh)

    jb_lo = jnp.maximum(t0 - (WINDOW - 1) - wbase, 0) // wb
    jb_hi = (t0 + tq - 1 - wbase) // wb + 1
    o_win = _flash_finish(lax.fori_loop(jb_lo, jb_hi, win_body, _flash_init(rows, d)))

    gates = gate_ref[0, 0]
    outs = []
    for h in range(nh):
        rs = slice(h * tq, (h + 1) * tq)
        outs.append(gates[:, h:h + 1] * o_cmp[rs] + gates[:, nh + h:nh + h + 1] * o_slc[rs]
                    + gates[:, 2 * nh + h:2 * nh + h + 1] * o_win[rs])
    o_ref[0] = jnp.concatenate(outs, axis=1).astype(o_ref.dtype)


def nsa_attention(qc, qr, gates, kc, vc, ks, vs, kw, vw, *, tq, qpos0, wbase, n_keys):
    b, h, t, d = qc.shape
    g = KV_GROUPS_A
    l = ks.shape[2]
    lw = kw.shape[2]
    nc = kc.shape[2]
    kb = min(KEY_CHUNK, l)
    assert l % kb == 0 and t % tq == 0 and lw % WIN_CHUNK == 0
    assert (qpos0 + t - 1) // kb + 1 <= l // kb and (qpos0 + t - 1 - wbase) // WIN_CHUNK + 1 <= lw // WIN_CHUNK
    ns = l // SLC_BLOCK
    n_sel = min(N_SELECT, -(-n_keys // SLC_BLOCK))
    q_spec = pl.BlockSpec((1, HPG_A, tq, d), lambda bi, gi, i: (bi, gi, i, 0))
    c_spec = pl.BlockSpec((1, 1, nc, d), lambda bi, gi, i: (bi, gi, 0, 0))
    k_spec = pl.BlockSpec((1, 1, l, d), lambda bi, gi, i: (bi, gi, 0, 0))
    w_spec = pl.BlockSpec((1, 1, lw, d), lambda bi, gi, i: (bi, gi, 0, 0))
    return pl.pallas_call(
        functools.partial(_nsa_kernel, tq=tq, kb=kb, qpos0=qpos0, wbase=wbase, ns=ns, n_sel=n_sel),
        grid=(b, g, t // tq),
        in_specs=[q_spec, q_spec, pl.BlockSpec((1, 1, tq, 3 * HPG_A), lambda bi, gi, i: (bi, gi, i, 0)),
                  c_spec, c_spec, k_spec, k_spec, w_spec, w_spec],
        out_specs=pl.BlockSpec((1, tq, HPG_A * d), lambda bi, gi, i: (bi, i, gi)),
        out_shape=jax.ShapeDtypeStruct((b, t, h * d), MXU_DTYPE),
        scratch_shapes=[pltpu.VMEM((l // kb, tq, kb), jnp.float32)],
        compiler_params=pltpu.CompilerParams(
            dimension_semantics=("parallel", "parallel", "arbitrary"), vmem_limit_bytes=VMEM_LIMIT),
        name="nsa_attention",
    )(qc, qr, gates, kc, vc, ks, vs, kw, vw)


def _dsa_kernel(qb_ref, qi_ref, wi_ref, kb_ref, vb_ref, ki_ref, o_ref, score_ref, *, tq, kb, qpos0, n_keep):
    i = pl.program_id(1)
    nh, d = N_HEADS_B, HEAD_DIM_B
    t0 = qpos0 + i * tq
    nj = (t0 + tq - 1) // kb + 1
    tpos = t0 + _iota((tq, 1), 0)
    w = wi_ref[0]

    def idx_body(j, carry):
        lo, hi = carry
        off = pl.multiple_of(j * kb, kb)
        kidx = ki_ref[0, pl.ds(off, kb), :]
        acc = jnp.zeros((tq, kb), jnp.float32)
        for h in range(IDX_HEADS):
            acc = acc + w[:, h:h + 1] * jnp.maximum(_dot_nt(qi_ref[0, h], kidx), 0.0)
        vis = (off + _iota((1, kb), 1)) <= tpos
        score_ref[j] = jnp.where(vis, acc, NEG_INF)
        lo = jnp.minimum(lo, jnp.min(jnp.where(vis, acc, POS_INF), axis=1, keepdims=True))
        hi = jnp.maximum(hi, jnp.max(jnp.where(vis, acc, NEG_INF), axis=1, keepdims=True))
        return lo, hi

    lo, hi = lax.fori_loop(0, nj, idx_body, (jnp.full((tq, 1), POS_INF, jnp.float32),
                                             jnp.full((tq, 1), NEG_INF, jnp.float32)))

    def bisect(_, carry):
        lo, hi = carry
        mid = 0.5 * (lo + hi)

        def count(j, c):
            return c + jnp.sum(jnp.where(score_ref[j] >= mid, 1.0, 0.0), axis=1, keepdims=True)

        ge = lax.fori_loop(0, nj, count, jnp.zeros((tq, 1), jnp.float32)) >= float(n_keep)
        return jnp.where(ge, mid, lo), jnp.where(ge, hi, mid)

    thr, _ = lax.fori_loop(0, BISECT_ITERS, bisect, (lo, hi))

    q = jnp.concatenate([qb_ref[0, :, h * d:(h + 1) * d] for h in range(nh)], axis=0)

    def att_body(j, carry):
        off = pl.multiple_of(j * kb, kb)
        madd = jnp.where(score_ref[j] >= thr, 0.0, NEG_INF)
        return _flash_step(carry, q, kb_ref[0, pl.ds(off, kb), :], vb_ref[0, pl.ds(off, kb), :], madd, nh,
                           scale=d ** -0.5)

    o = _flash_finish(lax.fori_loop(0, nj, att_body, _flash_init(nh * tq, d)))
    o_ref[0] = jnp.concatenate([o[h * tq:(h + 1) * tq] for h in range(nh)], axis=1).astype(o_ref.dtype)


def dsa_attention(qb, qi, wi, kb_, vb, ki, *, tq, qpos0, n_keep):
    b, t, hd = qb.shape
    l = kb_.shape[1]
    kb = min(KEY_CHUNK, l)
    assert l % kb == 0 and t % tq == 0 and (qpos0 + t - 1) // kb + 1 <= l // kb
    kv_spec = lambda width: pl.BlockSpec((1, l, width), lambda bi, i: (bi, 0, 0))
    return pl.pallas_call(
        functools.partial(_dsa_kernel, tq=tq, kb=kb, qpos0=qpos0, n_keep=n_keep),
        grid=(b, t // tq),
        in_specs=[pl.BlockSpec((1, tq, hd), lambda bi, i: (bi, i, 0)),
                  pl.BlockSpec((1, IDX_HEADS, tq, IDX_DIM), lambda bi, i: (bi, 0, i, 0)),
                  pl.BlockSpec((1, tq, IDX_HEADS), lambda bi, i: (bi, i, 0)),
                  kv_spec(HEAD_DIM_B), kv_spec(HEAD_DIM_B), kv_spec(IDX_DIM)],
        out_specs=pl.BlockSpec((1, tq, hd), lambda bi, i: (bi, i, 0)),
        out_shape=jax.ShapeDtypeStruct((b, t, hd), MXU_DTYPE),
        scratch_shapes=[pltpu.VMEM((l // kb, tq, kb), jnp.float32)],
        compiler_params=pltpu.CompilerParams(
            dimension_semantics=("parallel", "arbitrary"), vmem_limit_bytes=VMEM_LIMIT),
        name="dsa_attention",
    )(qb, qi, wi, kb_, vb, ki)


def _page_maps(n_pages, pp):
    n_steps = n_pages // pp

    def page_map(r):
        return lambda b, s, pt: (pt[b * n_pages + jnp.minimum(s, n_steps - 1) * pp + r], 0, 0)

    return n_steps, page_map


def _nsa_gather_kernel(pt_ref, *refs, pp, n_steps, rows):
    del pt_ref
    pages, (tail_ref, pe_ref), outs = refs[:pp], refs[pp:pp + 2], refs[pp + 2:]
    is_tail = pl.program_id(1) == n_steps
    for r in range(pp):
        rs = slice(r * rows, (r + 1) * rows)
        x = jnp.where(is_tail, tail_ref[0, rs, :], pages[r][0])
        k_cmp, v_cmp = x[:, :KV_A], x[:, KV_A:2 * KV_A]
        outs[0][0, rs, :] = (k_cmp + pe_ref[0]).astype(outs[0].dtype)
        outs[1][0, rs, :] = (k_cmp + pe_ref[1]).astype(outs[1].dtype)
        outs[2][0, rs, :] = (v_cmp + pe_ref[2]).astype(outs[2].dtype)
        outs[3][0, rs, :] = (v_cmp + pe_ref[3]).astype(outs[3].dtype)
        outs[4][0, rs, :] = x[:, 2 * KV_A:3 * KV_A].astype(outs[4].dtype)
        outs[5][0, rs, :] = x[:, 3 * KV_A:].astype(outs[5].dtype)


def nsa_gather(cache, page_table, tail, pe_tiles):
    db, n_pages = page_table.shape
    rows, width = cache.shape[1:]
    pp = PAGES_PER_STEP
    n_steps, page_map = _page_maps(n_pages, pp)
    l = (n_steps + 1) * pp * rows
    out_spec = pl.BlockSpec((1, pp * rows, KV_A), lambda b, s, pt: (b, s, 0))
    return pl.pallas_call(
        functools.partial(_nsa_gather_kernel, pp=pp, n_steps=n_steps, rows=rows),
        grid_spec=pltpu.PrefetchScalarGridSpec(
            num_scalar_prefetch=1,
            grid=(db, n_steps + 1),
            in_specs=[pl.BlockSpec((1, rows, width), page_map(r)) for r in range(pp)]
            + [pl.BlockSpec((1, pp * rows, width), lambda b, s, pt: (b, 0, 0)),
               pl.BlockSpec((4, rows, KV_A), lambda b, s, pt: (0, 0, 0))],
            out_specs=[out_spec] * 6),
        out_shape=[jax.ShapeDtypeStruct((db, l, KV_A), MXU_DTYPE)] * 6,
        compiler_params=pltpu.CompilerParams(
            dimension_semantics=("parallel", "arbitrary"), vmem_limit_bytes=VMEM_LIMIT),
        name="nsa_gather",
    )(page_table.reshape(-1), *([cache] * pp), tail, pe_tiles)


def _dsa_gather_kernel(pt_ref, *refs, pp, n_steps, rows):
    del pt_ref
    pages, tail_ref, (k_ref, v_ref, i_ref) = refs[:pp], refs[pp], refs[pp + 1:]
    is_tail = pl.program_id(1) == n_steps
    d = HEAD_DIM_B
    for r in range(pp):
        rs = slice(r * rows, (r + 1) * rows)
        x = jnp.where(is_tail, tail_ref[0, rs, :], pages[r][0])
        k_ref[0, rs, :] = x[:, :d].astype(k_ref.dtype)
        v_ref[0, rs, :] = x[:, d:2 * d].astype(v_ref.dtype)
        i_ref[0, rs, :] = x[:, 2 * d:].astype(i_ref.dtype)


def dsa_gather(cache, page_table, tail):
    db, n_pages = page_table.shape
    rows, width = cache.shape[1:]
    pp = PAGES_PER_STEP
    n_steps, page_map = _page_maps(n_pages, pp)
    l = (n_steps + 1) * pp * rows
    out_spec = lambda w: pl.BlockSpec((1, pp * rows, w), lambda b, s, pt: (b, s, 0))
    widths = (HEAD_DIM_B, HEAD_DIM_B, IDX_DIM)
    return pl.pallas_call(
        functools.partial(_dsa_gather_kernel, pp=pp, n_steps=n_steps, rows=rows),
        grid_spec=pltpu.PrefetchScalarGridSpec(
            num_scalar_prefetch=1,
            grid=(db, n_steps + 1),
            in_specs=[pl.BlockSpec((1, rows, width), page_map(r)) for r in range(pp)]
            + [pl.BlockSpec((1, pp * rows, width), lambda b, s, pt: (b, 0, 0))],
            out_specs=[out_spec(w) for w in widths]),
        out_shape=[jax.ShapeDtypeStruct((db, l, w), MXU_DTYPE) for w in widths],
        compiler_params=pltpu.CompilerParams(
            dimension_semantics=("parallel", "arbitrary"), vmem_limit_bytes=VMEM_LIMIT),
        name="dsa_gather",
    )(page_table.reshape(-1), *([cache] * pp), tail)


def _compress_kernel(zt_ref, zb_ref, w1t_ref, w1b_ref, b1_ref, w2_ref, o_ref, ab_ref, *, ncp):
    dot = functools.partial(jnp.dot, preferred_element_type=jnp.float32)
    at = dot(zt_ref[0, 0], w1t_ref[0])
    ab_ref[...] = dot(zb_ref[0, 0], w1b_ref[0])
    h = jax.nn.gelu(at[:ncp] + ab_ref[pl.ds(1, ncp), :] + b1_ref[0])
    o_ref[0, 0] = dot(h.astype(w2_ref.dtype), w2_ref[0]).astype(o_ref.dtype)


def compress(zt, zb, w1, b1, w2, ncp):
    _, nb, ch, kdim = zt.shape
    hid = w1.shape[-1]
    d = w2.shape[-1]
    assert ch >= ncp + 1 and ncp % SUBLANE == 0
    w1 = w1.astype(MXU_DTYPE)
    z_spec = pl.BlockSpec((1, 1, ch, kdim), lambda s, n: (s, n, 0, 0))
    return pl.pallas_call(
        functools.partial(_compress_kernel, ncp=ncp),
        grid=(2, nb),
        in_specs=[z_spec, z_spec,
                  pl.BlockSpec((1, kdim, hid), lambda s, n: (s, 0, 0)),
                  pl.BlockSpec((1, kdim, hid), lambda s, n: (s, 1, 0)),
                  pl.BlockSpec((1, 1, hid), lambda s, n: (s, 0, 0)),
                  pl.BlockSpec((1, hid, d), lambda s, n: (s, 0, 0))],
        out_specs=pl.BlockSpec((1, 1, ncp, d), lambda s, n: (s, n, 0, 0)),
        out_shape=jax.ShapeDtypeStruct((2, nb, ncp, d), MXU_DTYPE),
        scratch_shapes=[pltpu.VMEM((ch, hid), jnp.float32)],
        compiler_params=pltpu.CompilerParams(
            dimension_semantics=("parallel", "parallel"), vmem_limit_bytes=VMEM_LIMIT),
        name="compress",
    )(zt, zb, w1, w1, b1.reshape(2, 1, hid).astype(jnp.float32), w2.astype(MXU_DTYPE))


def _chunk_flat(z):
    b, l, _ = z.shape
    z = z.reshape(b, l // CMP_STRIDE, CMP_STRIDE, KV_GROUPS_A, HEAD_DIM_A)
    return z.transpose(0, 3, 1, 2, 4).reshape(b * KV_GROUPS_A, l // CMP_STRIDE, CMP_STRIDE * HEAD_DIM_A)


def _group_major(a):
    b, l, _ = a.shape
    return a.reshape(b, l, KV_GROUPS_A, HEAD_DIM_A).transpose(0, 2, 1, 3)


def _pe_tiles(cmp_pe, rows):
    tiles = [jnp.tile(cmp_pe[s, hh * CMP_STRIDE:(hh + 1) * CMP_STRIDE], (rows // CMP_STRIDE, KV_GROUPS_A))
             for s in range(2) for hh in range(2)]
    return jnp.stack(tiles).astype(jnp.float32)


def block_summaries(z4, cmp_w1, cmp_b1, cmp_w2, n_keys):
    b = z4[0].shape[0]
    nc = -(-n_keys // CMP_STRIDE)
    ncp = _round_up(nc, SUBLANE)
    zt = jnp.stack([_chunk_flat(z4[0]), _chunk_flat(z4[2])])
    zb = jnp.stack([_chunk_flat(z4[1]), _chunk_flat(z4[3])])
    out = compress(zt, zb, cmp_w1, cmp_b1, cmp_w2, ncp).reshape(2, b, KV_GROUPS_A, ncp, HEAD_DIM_A)
    ncl = ncp if ncp <= LANE else _round_up(ncp, LANE)
    out = jnp.pad(out, ((0, 0),) * 3 + ((0, ncl - ncp), (0, 0)))
    return out[0], out[1]


def rmsnorm(x, g):
    xf = x.astype(jnp.float32)
    y = xf * lax.rsqrt(jnp.mean(xf * xf, axis=-1, keepdims=True) + EPS)
    return (y * g.astype(jnp.float32)).astype(x.dtype)


def rope_partial(x, pos):
    d = x.shape[-1]
    rot = d // ROT_FRACTION
    half = rot // 2
    inv = ROPE_THETA ** (-jnp.arange(half, dtype=jnp.float32) / half)
    ang = pos.astype(jnp.float32)[:, None] * inv[None, :]
    ang = ang.reshape((ang.shape[0],) + (1,) * (x.ndim - 3) + (half,))
    cos = jnp.cos(ang).astype(x.dtype)
    sin = jnp.sin(ang).astype(x.dtype)
    x1 = x[..., :half]
    x2 = x[..., half:rot]
    return jnp.concatenate([x1 * cos - x2 * sin, x2 * cos + x1 * sin, x[..., rot:]], axis=-1)


def pad_rows(a, n):
    return jnp.pad(a, [(0, 0), (0, n - a.shape[1])] + [(0, 0)] * (a.ndim - 2))


def project(xn, w_in, pos, sizes):
    b, t, d_model = xn.shape
    offsets = tuple(sum(sizes[:i + 1]) for i in range(len(sizes) - 1))
    qa, kva, ga, qb, kvb, qi, ki, wi, gm = jnp.split(matmul(xn, w_in), offsets, axis=-1)
    qa = qa.reshape(b, t, N_HEADS_A, HEAD_DIM_A)
    kva = kva.reshape(b, t, 6, KV_GROUPS_A, HEAD_DIM_A)
    kvb = kvb.reshape(b, t, 2, HEAD_DIM_B)
    gm = jax.nn.sigmoid(gm.reshape(b, t, 2, d_model))
    return {
        'q_cmp': qa, 'q_rot': rope_partial(qa, pos),
        'k_cmp': kva[:, :, 0], 'v_cmp': kva[:, :, 1],
        'k_slc': rope_partial(kva[:, :, 2], pos), 'v_slc': kva[:, :, 3],
        'k_win': rope_partial(kva[:, :, 4], pos), 'v_win': kva[:, :, 5],
        'g_nsa': jax.nn.sigmoid(ga.reshape(b, t, 3, N_HEADS_A)),
        'q_b': rope_partial(qb.reshape(b, t, N_HEADS_B, HEAD_DIM_B), pos),
        'k_b': rope_partial(kvb[:, :, 0], pos), 'v_b': kvb[:, :, 1],
        'q_idx': rope_partial(qi.reshape(b, t, IDX_HEADS, IDX_DIM), pos),
        'k_idx': rope_partial(ki, pos), 'w_idx': wi,
        'g_a': gm[:, :, 0], 'g_b': gm[:, :, 1],
    }


def _query_operands(p, t_pad):
    b, t = p['q_cmp'].shape[:2]

    def heads_first(a, scale):
        return pad_rows(a * scale, t_pad).swapaxes(1, 2).astype(MXU_DTYPE)

    qc = heads_first(p['q_cmp'], HEAD_DIM_A ** -0.5)
    qr = heads_first(p['q_rot'], HEAD_DIM_A ** -0.5)
    gates = p['g_nsa'].reshape(b, t, 3, KV_GROUPS_A, HPG_A).transpose(0, 3, 1, 2, 4)
    gates = pad_rows(gates.reshape(b, KV_GROUPS_A, t, 3 * HPG_A).swapaxes(1, 2), t_pad).swapaxes(1, 2)
    qb = pad_rows(p['q_b'].reshape(b, t, N_HEADS_B * HEAD_DIM_B), t_pad).astype(MXU_DTYPE)
    qi = heads_first(p['q_idx'], IDX_DIM ** -0.5)
    wi = pad_rows(p['w_idx'].astype(jnp.float32) * IDX_HEADS ** -0.5, t_pad)
    return qc, qr, gates, qb, qi, wi


def merge_branches(o_nsa, o_dsa, g_a, g_b, w_oa, w_ob, w_o):
    ya = matmul(o_nsa, w_oa)
    yb = matmul(o_dsa, w_ob)
    return matmul(g_a * ya + g_b * yb, w_o)


def mixer_prompt(xn, w_in, cmp_pe, cmp_w1, cmp_b1, cmp_w2, w_oa, w_ob, w_o, sizes):
    b, s, _ = xn.shape
    pos = jnp.arange(s, dtype=jnp.int32)
    p = project(xn, w_in, pos, sizes)
    rows2 = lambda a: a.reshape(b, s, KV_A)
    pe4 = _pe_tiles(cmp_pe, s + LANE)
    kz, vz = pad_rows(rows2(p['k_cmp']), s + LANE), pad_rows(rows2(p['v_cmp']), s + LANE)
    z4 = [(kz + pe4[0]).astype(MXU_DTYPE), (kz + pe4[1]).astype(MXU_DTYPE),
          (vz + pe4[2]).astype(MXU_DTYPE), (vz + pe4[3]).astype(MXU_DTYPE)]
    kc, vc = block_summaries(z4, cmp_w1, cmp_b1, cmp_w2, s)
    qc, qr, gates, qb, qi, wi = _query_operands(p, s)
    gm = lambda a: jnp.swapaxes(a, 1, 2).astype(MXU_DTYPE)
    o_nsa = nsa_attention(qc, qr, gates, kc, vc, gm(p['k_slc']), gm(p['v_slc']), gm(p['k_win']), gm(p['v_win']),
                          tq=min(TQ, s), qpos0=0, wbase=0, n_keys=s)
    o_dsa = dsa_attention(qb, qi, wi, p['k_b'].astype(MXU_DTYPE), p['v_b'].astype(MXU_DTYPE),
                          p['k_idx'].astype(MXU_DTYPE), tq=min(TQ, s), qpos0=0, n_keep=min(DSA_TOPK, s // 4))
    y = merge_branches(o_nsa, o_dsa, p['g_a'], p['g_b'], w_oa, w_ob, w_o)
    nsa_rows = jnp.stack([p['k_cmp'], p['v_cmp'], p['k_slc'], p['v_slc']], axis=2)
    win_state = jnp.stack([p['k_win'], p['v_win']], axis=2)[:, -min(WINDOW, s):]
    dsa_rows = jnp.concatenate([p['k_b'], p['v_b'], p['k_idx']], axis=-1)
    return y, nsa_rows, win_state, dsa_rows


def mixer_sample(xn, cache_nsa, win_buf, cache_dsa, page_table, w_in, cmp_pe, cmp_w1, cmp_b1, cmp_w2,
                 w_oa, w_ob, w_o, sizes):
    b, t, _ = xn.shape
    page = cache_nsa.shape[1]
    past_len = page_table.shape[1] * page
    n_keys = past_len + t
    tail_rows = PAGES_PER_STEP * page
    pos = past_len + jnp.arange(t, dtype=jnp.int32)
    p = project(xn, w_in, pos, sizes)
    nsa_rows = jnp.stack([p['k_cmp'], p['v_cmp'], p['k_slc'], p['v_slc']], axis=2)
    dsa_rows = jnp.concatenate([p['k_b'], p['v_b'], p['k_idx']], axis=-1)
    qc, qr, gates, qb, qi, wi = _query_operands(p, TQ_STEP)

    z4_ks_vs = nsa_gather(cache_nsa.reshape(cache_nsa.shape[0], page, 4 * KV_A), page_table,
                          pad_rows(nsa_rows.reshape(b, t, 4 * KV_A), tail_rows), _pe_tiles(cmp_pe, page))
    kc, vc = block_summaries(z4_ks_vs[:4], cmp_w1, cmp_b1, cmp_w2, n_keys)
    w_len = win_buf.shape[1]
    win_all = jnp.concatenate([win_buf, jnp.stack([p['k_win'], p['v_win']], axis=2)], axis=1)
    win_pad = pad_rows(win_all, w_len + WIN_CHUNK).astype(MXU_DTYPE)
    kw, vw = win_pad[:, :, 0].swapaxes(1, 2), win_pad[:, :, 1].swapaxes(1, 2)
    o_nsa = nsa_attention(qc, qr, gates, kc, vc, _group_major(z4_ks_vs[4]), _group_major(z4_ks_vs[5]), kw, vw,
                          tq=TQ_STEP, qpos0=past_len, wbase=past_len - w_len, n_keys=n_keys)

    kb_, vb, ki = dsa_gather(cache_dsa, page_table, pad_rows(dsa_rows, tail_rows))
    o_dsa = dsa_attention(qb, qi, wi, kb_, vb, ki, tq=TQ_STEP, qpos0=past_len, n_keep=min(DSA_TOPK, n_keys // 4))
    y = merge_branches(o_nsa[:, :t], o_dsa[:, :t], p['g_a'], p['g_b'], w_oa, w_ob, w_o)
    return y, nsa_rows, win_all[:, -w_len:], dsa_rows


def mem_kv(mem, g, w_kv):
    b, m, _ = mem.shape
    return matmul(rmsnorm(mem, g), w_kv).reshape(b, m, 2, MEM_HEADS, MEM_HEAD_DIM)


def mem_attend(hn, kv, w_q, w_o):
    b, t, _ = hn.shape
    q = matmul(hn, w_q).reshape(b, t, MEM_HEADS, MEM_HEAD_DIM)
    s = jnp.einsum('bthd,bmhd->bthm', q, kv[:, :, 0]).astype(jnp.float32) * MEM_HEAD_DIM ** -0.5
    p = jax.nn.softmax(s, axis=-1).astype(kv.dtype)
    o = jnp.einsum('bthm,bmhd->bthd', p, kv[:, :, 1])
    return matmul(o.reshape(b, t, -1), w_o)


def conv_ffn(hn, prev, w_up, conv_w, conv_b, w_down):
    t = hn.shape[1]
    ext = jnp.concatenate([prev, matmul(hn, w_up)], axis=1)
    c = conv_b + sum(ext[:, j:j + t] * conv_w[j] for j in range(CONV_WIDTH))
    gate, up = jnp.split(c, 2, axis=-1)
    return matmul(jax.nn.silu(gate) * up, w_down), ext[:, t:]


def kernel(x_prompt, x_sample, mem_prompt, cache_nsa_kv, state_nsa_win, cache_dsa_kv, cache_mem_kv, state_conv,
           page_table, norm_g, w_in, cmp_pe, cmp_w1, cmp_b1, cmp_w2, w_out_a, w_out_b, w_out, w_mem_q, w_mem_kv,
           w_mem_out, w_up, conv_w, conv_b, w_down, final_g):
    depth = w_in.shape[0]
    d_model = x_prompt.shape[-1]
    d_ff = w_down.shape[1]
    sizes = (N_HEADS_A * HEAD_DIM_A, 6 * KV_A, 3 * N_HEADS_A, N_HEADS_B * HEAD_DIM_B, 2 * HEAD_DIM_B,
             IDX_HEADS * IDX_DIM, IDX_DIM, IDX_HEADS, 2 * d_model)
    xp, xs = x_prompt, x_sample
    nsa_p, nsa_s, win_p, win_s, dsa_p, dsa_s, mem_p, conv_p, conv_s = [], [], [], [], [], [], [], [], []
    for l in range(depth):
        yp, a, bwin, c = mixer_prompt(rmsnorm(xp, norm_g[l, 0]), w_in[l], cmp_pe[l], cmp_w1[l], cmp_b1[l], cmp_w2[l],
                                      w_out_a[l], w_out_b[l], w_out[l], sizes)
        nsa_p.append(a); win_p.append(bwin); dsa_p.append(c)
        ys, a, bwin, c = mixer_sample(rmsnorm(xs, norm_g[l, 0]), cache_nsa_kv[l], state_nsa_win[l], cache_dsa_kv[l],
                                      page_table, w_in[l], cmp_pe[l], cmp_w1[l], cmp_b1[l], cmp_w2[l],
                                      w_out_a[l], w_out_b[l], w_out[l], sizes)
        nsa_s.append(a); win_s.append(bwin); dsa_s.append(c)
        xp = xp + yp
        xs = xs + ys
        kv_p = mem_kv(mem_prompt, norm_g[l, 2], w_mem_kv[l])
        mem_p.append(kv_p)
        xp = xp + mem_attend(rmsnorm(xp, norm_g[l, 1]), kv_p, w_mem_q[l], w_mem_out[l])
        xs = xs + mem_attend(rmsnorm(xs, norm_g[l, 1]), cache_mem_kv[l], w_mem_q[l], w_mem_out[l])
        fp, cp = conv_ffn(rmsnorm(xp, norm_g[l, 3]), jnp.zeros((xp.shape[0], CONV_WIDTH - 1, 2 * d_ff), xp.dtype),
                          w_up[l], conv_w[l], conv_b[l], w_down[l])
        fs, cs = conv_ffn(rmsnorm(xs, norm_g[l, 3]), state_conv[l], w_up[l], conv_w[l], conv_b[l], w_down[l])
        conv_p.append(cp); conv_s.append(cs)
        xp = xp + fp
        xs = xs + fs
    y_prompt = rmsnorm(xp, final_g)
    y_sample = rmsnorm(xs, final_g)
    return (y_prompt, y_sample, jnp.stack(nsa_p), jnp.stack(nsa_s), jnp.stack(win_p), jnp.stack(win_s),
            jnp.stack(dsa_p), jnp.stack(dsa_s), jnp.stack(mem_p), jnp.stack(conv_p), jnp.stack(conv_s))
```

```python
import functools

import jax
import jax.numpy as jnp
from jax import lax
from jax.experimental import pallas as pl
from jax.experimental.pallas import tpu as pltpu

N_HEADS_A = 16
HEAD_DIM_A = 64
KV_GROUPS_A = 2
CMP_BLOCK = 32
CMP_STRIDE = 16
SLC_BLOCK = 64
N_SELECT = 16
WINDOW = 512
N_HEADS_B = 8
HEAD_DIM_B = 128
IDX_HEADS = 16
IDX_DIM = 64
DSA_TOPK = 256
MEM_HEADS = 4
MEM_HEAD_DIM = 128
CONV_WIDTH = 3
ROPE_THETA = 500000.0
ROT_FRACTION = 4
EPS = 1e-6
KV_A = KV_GROUPS_A * HEAD_DIM_A
CMP_PER_SLC = SLC_BLOCK // CMP_STRIDE
HPG_A = N_HEADS_A // KV_GROUPS_A
DSA_ROW = 2 * HEAD_DIM_B + IDX_DIM
NSA_SECTIONS = 4

LANE = 128
SUBLANE = 8
BF16_ROWS = 16
VMEM_LIMIT = 48 * 1024 * 1024

NEG_INF = float('-inf')
POS_INF = float('inf')
MXU_DTYPE = jnp.bfloat16
TQ = 128
TQ_STEP = BF16_ROWS
KEY_CHUNK = 512
WIN_CHUNK = 128
PAGES_PER_STEP = 4
BISECT_ITERS = 32
SLC_SHIFT = SLC_BLOCK.bit_length() - 1
CMP_PER_SLC_SHIFT = CMP_PER_SLC.bit_length() - 1


def _round_up(n, m):
    return -(-n // m) * m


def _tile(n, cap):
    if n <= cap:
        return n
    best = None
    for t in range(LANE, cap + 1, LANE):
        if n % t == 0:
            best = t
    assert best is not None, (n, cap)
    return best


def _mm_kernel(x_ref, w_ref, o_ref, acc_ref):
    @pl.when(pl.program_id(2) == 0)
    def _():
        acc_ref[...] = jnp.zeros_like(acc_ref)

    acc_ref[...] += jnp.dot(x_ref[...], w_ref[...], preferred_element_type=jnp.float32)

    @pl.when(pl.program_id(2) == pl.num_programs(2) - 1)
    def _():
        o_ref[...] = acc_ref[...].astype(o_ref.dtype)


def matmul(x, w, out_dtype=jnp.float32):
    lead = x.shape[:-1]
    k = x.shape[-1]
    n = w.shape[-1]
    x2 = x.reshape(-1, k).astype(MXU_DTYPE)
    w2 = w.astype(MXU_DTYPE)
    m0 = x2.shape[0]
    tm = 512 if m0 >= 512 else _round_up(m0, BF16_ROWS)
    m = _round_up(m0, tm)
    if m != m0:
        x2 = jnp.pad(x2, ((0, m - m0), (0, 0)))
    tn = _tile(n, 1024)
    tk = _tile(k, 2048)
    out = pl.pallas_call(
        _mm_kernel,
        grid=(m // tm, n // tn, k // tk),
        in_specs=[pl.BlockSpec((tm, tk), lambda i, j, l: (i, l)),
                  pl.BlockSpec((tk, tn), lambda i, j, l: (l, j))],
        out_specs=pl.BlockSpec((tm, tn), lambda i, j, l: (i, j)),
        out_shape=jax.ShapeDtypeStruct((m, n), out_dtype),
        scratch_shapes=[pltpu.VMEM((tm, tn), jnp.float32)],
        compiler_params=pltpu.CompilerParams(
            dimension_semantics=("parallel", "parallel", "arbitrary"),
            vmem_limit_bytes=VMEM_LIMIT),
        name="matmul",
    )(x2, w2)
    return out[:m0].reshape(lead + (n,))


def _dot_nt(a, b):
    return lax.dot_general(a, b, (((1,), (1,)), ((), ())), preferred_element_type=jnp.float32)


def _iota(shape, dim):
    return lax.broadcasted_iota(jnp.int32, shape, dim)


def _flash_init(rows, d):
    return (jnp.full((rows, 1), NEG_INF, jnp.float32), jnp.zeros((rows, 1), jnp.float32),
            jnp.zeros((rows, d), jnp.float32))


def _flash_step(carry, q, k, v, madd, nh, scale=None):
    m, l, acc = carry
    s = _dot_nt(q, k)
    if scale is not None:
        s = s * scale
    r, kb = s.shape
    s = (s.reshape(nh, r // nh, kb) + madd[None]).reshape(r, kb)
    m_new = jnp.maximum(m, jnp.max(s, axis=1, keepdims=True))
    m_safe = jnp.where(m_new == NEG_INF, 0.0, m_new)
    p = jnp.exp(s - m_safe)
    alpha = jnp.exp(m - m_safe)
    l = alpha * l + jnp.sum(p, axis=1, keepdims=True)
    acc = alpha * acc + jnp.dot(p.astype(v.dtype), v, preferred_element_type=jnp.float32)
    return m_new, l, acc


def _flash_finish(carry):
    _, l, acc = carry
    return acc / jnp.maximum(l, 1e-30)


def _split_dot(x, m01):
    hi = x.astype(jnp.bfloat16)
    r1 = x - hi.astype(jnp.float32)
    mid = r1.astype(jnp.bfloat16)
    lo = (r1 - mid.astype(jnp.float32)).astype(jnp.bfloat16)
    dot = functools.partial(jnp.dot, preferred_element_type=jnp.float32)
    return dot(hi, m01) + dot(mid, m01) + dot(lo, m01)


def _nsa_group(g, i, qc_ref, qr_ref, gate_ref, kc_ref, vc_ref, ks_ref, vs_ref, kw_ref, vw_ref, mask_ref,
               *, tq, kb, qpos0, wbase, ns, n_sel):
    nh, d = HPG_A, HEAD_DIM_A
    rows = nh * tq
    nc = kc_ref.shape[3]
    nsp = ns if ns <= LANE else _round_up(ns, LANE)
    t0 = qpos0 + i * tq
    nj = (t0 + tq - 1) // kb + 1
    cols = slice(g * d, (g + 1) * d)

    def stack_heads(q_ref):
        return jnp.concatenate([q_ref[0, :, (g * nh + h) * d:(g * nh + h + 1) * d] for h in range(nh)], axis=0)

    qc = stack_heads(qc_ref)
    qr = stack_heads(qr_ref)
    tpos = t0 + _iota((tq, 1), 0)

    blk_last = _iota((1, nc), 1) * CMP_STRIDE + (CMP_BLOCK - 1)
    madd_c = jnp.where(blk_last <= tpos, 0.0, NEG_INF)
    s = _dot_nt(qc, kc_ref[0, 0, g]).reshape(nh, tq, nc) + madd_c[None]
    m = jnp.max(s, axis=2, keepdims=True)
    m = jnp.where(m == NEG_INF, 0.0, m)
    e = jnp.exp(s - m)
    p = e / jnp.maximum(jnp.sum(e, axis=2, keepdims=True), 1e-30)
    o_cmp = jnp.dot(p.reshape(rows, nc).astype(vc_ref.dtype), vc_ref[0, 0, g], preferred_element_type=jnp.float32)

    imp = jnp.sum(p, axis=0)
    c_id = _iota((nc, nsp), 0)
    m_id = _iota((nc, nsp), 1)
    overlap = (jnp.right_shift(c_id, CMP_PER_SLC_SHIFT) == m_id) | (c_id == m_id * CMP_PER_SLC - 1)
    score = _split_dot(imp, overlap.astype(jnp.bfloat16))
    blk = _iota((1, nsp), 1)
    cur = jnp.right_shift(tpos, SLC_SHIFT)
    forced = (blk == 0) | (blk == cur) | (blk == cur - 1)
    sc = jnp.where(forced, POS_INF, jnp.where(blk * SLC_BLOCK <= tpos, score, NEG_INF))
    rank = jnp.zeros((tq, nsp), jnp.float32)
    for mp in range(ns):
        col = sc[:, mp:mp + 1]
        beats = (col > sc) | ((col == sc) & (blk > mp))
        rank = rank + jnp.where(beats, 1.0, 0.0)
    sel = jnp.where(rank < n_sel, 1.0, 0.0).astype(jnp.bfloat16)

    def make_mask(j, _):
        kpos = j * kb + _iota((1, kb), 1)
        expand = (jnp.right_shift(j * kb + _iota((nsp, kb), 1), SLC_SHIFT) == _iota((nsp, kb), 0))
        hit = jnp.dot(sel, expand.astype(jnp.bfloat16), preferred_element_type=jnp.float32)
        mask_ref[j] = jnp.where((hit > 0.5) & (kpos <= tpos), 0.0, NEG_INF)
        return 0

    lax.fori_loop(0, nj, make_mask, 0)

    def slc_body(j, carry):
        off = pl.multiple_of(j * kb, kb)
        return _flash_step(carry, qr, ks_ref[0, pl.ds(off, kb), cols], vs_ref[0, pl.ds(off, kb), cols],
                           mask_ref[j], nh)

    o_slc = _flash_finish(lax.fori_loop(0, nj, slc_body, _flash_init(rows, d)))

    wb = WIN_CHUNK

    def win_body(jb, carry):
        off = pl.multiple_of(jb * wb, wb)
        dist = tpos - (wbase + off + _iota((1, wb), 1))
        madd = jnp.where((dist >= 0) & (dist < WINDOW), 0.0, NEG_INF)
        return _flash_step(carry, qr, kw_ref[0, pl.ds(off, wb), cols], vw_ref[0, pl.ds(off, wb), cols], madd, nh)

    jb_lo = jnp.maximum(t0 - (WINDOW - 1) - wbase, 0) // wb
    jb_hi = (t0 + tq - 1 - wbase) // wb + 1
    o_win = _flash_finish(lax.fori_loop(jb_lo, jb_hi, win_body, _flash_init(rows, d)))

    gates = gate_ref[0]
    outs = []
    for h in range(nh):
        rs = slice(h * tq, (h + 1) * tq)
        c = g * nh + h
        outs.append(gates[:, c:c + 1] * o_cmp[rs] + gates[:, N_HEADS_A + c:N_HEADS_A + c + 1] * o_slc[rs]
                    + gates[:, 2 * N_HEADS_A + c:2 * N_HEADS_A + c + 1] * o_win[rs])
    return outs


def _nsa_kernel(qc_ref, qr_ref, gate_ref, kc_ref, vc_ref, ks_ref, vs_ref, kw_ref, vw_ref, o_ref, mask_ref, **kw):
    i = pl.program_id(1)
    outs = []
    for g in range(KV_GROUPS_A):
        outs += _nsa_group(g, i, qc_ref, qr_ref, gate_ref, kc_ref, vc_ref, ks_ref, vs_ref, kw_ref, vw_ref,
                           mask_ref, **kw)
    o_ref[0] = jnp.concatenate(outs, axis=1).astype(o_ref.dtype)


def nsa_attention(qc, qr, gates, kvc, ks, vs, kw, vw, *, tq, qpos0, wbase, n_keys):
    b, t, hd = qc.shape
    l = ks.shape[1]
    lw = kw.shape[1]
    nc, d = kvc.shape[3:]
    kb = min(KEY_CHUNK, l) if tq >= TQ else l
    assert l % kb == 0 and t % tq == 0 and lw % WIN_CHUNK == 0
    assert (qpos0 + t - 1) // kb + 1 <= l // kb and (qpos0 + t - 1 - wbase) // WIN_CHUNK + 1 <= lw // WIN_CHUNK
    ns = l // SLC_BLOCK
    n_sel = min(N_SELECT, -(-n_keys // SLC_BLOCK))
    q_spec = pl.BlockSpec((1, tq, hd), lambda bi, i: (bi, i, 0))
    kc_spec = pl.BlockSpec((1, 1, KV_GROUPS_A, nc, d), lambda bi, i: (0, bi, 0, 0, 0))
    vc_spec = pl.BlockSpec((1, 1, KV_GROUPS_A, nc, d), lambda bi, i: (1, bi, 0, 0, 0))
    k_spec = pl.BlockSpec((1, l, KV_A), lambda bi, i: (bi, 0, 0))
    w_spec = pl.BlockSpec((1, lw, KV_A), lambda bi, i: (bi, 0, 0))
    return pl.pallas_call(
        functools.partial(_nsa_kernel, tq=tq, kb=kb, qpos0=qpos0, wbase=wbase, ns=ns, n_sel=n_sel),
        grid=(b, t // tq),
        in_specs=[q_spec, q_spec, pl.BlockSpec((1, tq, 3 * N_HEADS_A), lambda bi, i: (bi, i, 0)),
                  kc_spec, vc_spec, k_spec, k_spec, w_spec, w_spec],
        out_specs=q_spec,
        out_shape=jax.ShapeDtypeStruct((b, t, hd), MXU_DTYPE),
        scratch_shapes=[pltpu.VMEM((l // kb, tq, kb), jnp.float32)],
        compiler_params=pltpu.CompilerParams(
            dimension_semantics=("parallel", "arbitrary"), vmem_limit_bytes=VMEM_LIMIT),
        name="nsa_attention",
    )(qc, qr, gates, kvc, kvc, ks, vs, kw, vw)


def _dsa_kernel(qb_ref, qi_ref, wi_ref, kb_ref, vb_ref, ki_ref, o_ref, score_ref, *, tq, kb, qpos0, n_keep):
    i = pl.program_id(1)
    nh, d = N_HEADS_B, HEAD_DIM_B
    t0 = qpos0 + i * tq
    nj = (t0 + tq - 1) // kb + 1
    tpos = t0 + _iota((tq, 1), 0)
    w = wi_ref[0]

    def idx_body(j, carry):
        lo, hi = carry
        off = pl.multiple_of(j * kb, kb)
        kidx = ki_ref[0, pl.ds(off, kb), :]
        acc = jnp.zeros((tq, kb), jnp.float32)
        for h in range(IDX_HEADS):
            qh = qi_ref[0, :, h * IDX_DIM:(h + 1) * IDX_DIM]
            acc = acc + w[:, h:h + 1] * jnp.maximum(_dot_nt(qh, kidx), 0.0)
        vis = (off + _iota((1, kb), 1)) <= tpos
        score_ref[j] = jnp.where(vis, acc, NEG_INF)
        lo = jnp.minimum(lo, jnp.min(jnp.where(vis, acc, POS_INF), axis=1, keepdims=True))
        hi = jnp.maximum(hi, jnp.max(jnp.where(vis, acc, NEG_INF), axis=1, keepdims=True))
        return lo, hi

    lo, hi = lax.fori_loop(0, nj, idx_body, (jnp.full((tq, 1), POS_INF, jnp.float32),
                                             jnp.full((tq, 1), NEG_INF, jnp.float32)))

    def bisect(_, carry):
        lo, hi = carry
        mid = 0.5 * (lo + hi)

        def count(j, c):
            return c + jnp.sum(jnp.where(score_ref[j] >= mid, 1.0, 0.0), axis=1, keepdims=True)

        ge = lax.fori_loop(0, nj, count, jnp.zeros((tq, 1), jnp.float32)) >= float(n_keep)
        return jnp.where(ge, mid, lo), jnp.where(ge, hi, mid)

    thr, _ = lax.fori_loop(0, BISECT_ITERS, bisect, (lo, hi))

    q = jnp.concatenate([qb_ref[0, :, h * d:(h + 1) * d] for h in range(nh)], axis=0)

    def att_body(j, carry):
        off = pl.multiple_of(j * kb, kb)
        madd = jnp.where(score_ref[j] >= thr, 0.0, NEG_INF)
        return _flash_step(carry, q, kb_ref[0, pl.ds(off, kb), :], vb_ref[0, pl.ds(off, kb), :], madd, nh,
                           scale=d ** -0.5)

    o = _flash_finish(lax.fori_loop(0, nj, att_body, _flash_init(nh * tq, d)))
    o_ref[0] = jnp.concatenate([o[h * tq:(h + 1) * tq] for h in range(nh)], axis=1).astype(o_ref.dtype)


def dsa_attention(qb, qi, wi, kb_, vb, ki, *, tq, qpos0, n_keep):
    b, t, hd = qb.shape
    l = kb_.shape[1]
    kb = min(KEY_CHUNK, l) if tq >= TQ else l
    assert l % kb == 0 and t % tq == 0 and (qpos0 + t - 1) // kb + 1 <= l // kb
    kv_spec = lambda width: pl.BlockSpec((1, l, width), lambda bi, i: (bi, 0, 0))
    return pl.pallas_call(
        functools.partial(_dsa_kernel, tq=tq, kb=kb, qpos0=qpos0, n_keep=n_keep),
        grid=(b, t // tq),
        in_specs=[pl.BlockSpec((1, tq, hd), lambda bi, i: (bi, i, 0)),
                  pl.BlockSpec((1, tq, IDX_HEADS * IDX_DIM), lambda bi, i: (bi, i, 0)),
                  pl.BlockSpec((1, tq, IDX_HEADS), lambda bi, i: (bi, i, 0)),
                  kv_spec(HEAD_DIM_B), kv_spec(HEAD_DIM_B), kv_spec(IDX_DIM)],
        out_specs=pl.BlockSpec((1, tq, hd), lambda bi, i: (bi, i, 0)),
        out_shape=jax.ShapeDtypeStruct((b, t, hd), MXU_DTYPE),
        scratch_shapes=[pltpu.VMEM((l // kb, tq, kb), jnp.float32)],
        compiler_params=pltpu.CompilerParams(
            dimension_semantics=("parallel", "arbitrary"), vmem_limit_bytes=VMEM_LIMIT),
        name="dsa_attention",
    )(qb, qi, wi, kb_, vb, ki)


def _page_maps(n_pages, pp):
    n_steps = n_pages // pp

    def page_map(r):
        return lambda b, s, pt: (pt[b * n_pages + jnp.minimum(s, n_steps - 1) * pp + r], 0, 0)

    return n_steps, page_map


def _nsa_gather_kernel(pt_ref, *refs, pp, n_steps, rows):
    del pt_ref
    pages, (tail_ref, pe_ref), (zt_ref, zb_ref, ks_ref, vs_ref) = refs[:pp], refs[pp:pp + 2], refs[pp + 2:]
    is_tail = pl.program_id(1) == n_steps
    cpp = rows // CMP_STRIDE
    d = HEAD_DIM_A
    first_half = _iota((cpp, KV_A), 1) < d
    z = [[[] for _ in range(KV_GROUPS_A)] for _ in range(2)]

    def strided(r, first, n, step):
        return jnp.where(is_tail, tail_ref[0, pl.ds(r * NSA_SECTIONS * rows + first, n, stride=step), :],
                         pages[r][0, pl.ds(first, n, stride=step), :])

    for r in range(pp):
        rs = slice(r * rows, (r + 1) * rows)
        ks_ref[0, rs, :] = strided(r, 2, rows, NSA_SECTIONS).astype(ks_ref.dtype)
        vs_ref[0, rs, :] = strided(r, 3, rows, NSA_SECTIONS).astype(vs_ref.dtype)
        pieces = [[[] for _ in range(KV_GROUPS_A)] for _ in range(2)]
        for j in range(0, CMP_STRIDE, 2):
            for sec in range(2):
                a, b = [strided(r, NSA_SECTIONS * jj + sec, cpp, NSA_SECTIONS * CMP_STRIDE) for jj in (j, j + 1)]
                pieces[sec][0].append(jnp.where(first_half, a, pltpu.roll(b, d, 1)))
                pieces[sec][1].append(jnp.where(first_half, pltpu.roll(a, d, 1), b))
        for sec in range(2):
            for g in range(KV_GROUPS_A):
                z[sec][g].append(jnp.concatenate(pieces[sec][g], axis=1))
    for sec in range(2):
        for g in range(KV_GROUPS_A):
            zf = jnp.concatenate(z[sec][g], axis=0)
            zt_ref[sec, 0, g] = (zf + pe_ref[sec, 0]).astype(zt_ref.dtype)
            zb_ref[sec, 0, g] = (zf + pe_ref[sec, 1]).astype(zb_ref.dtype)


def nsa_gather(cache, page_table, tail, cmp_pe):
    db, n_pages = page_table.shape
    width = cache.shape[2]
    rows = cache.shape[1] // NSA_SECTIONS
    pp = PAGES_PER_STEP
    n_steps, page_map = _page_maps(n_pages, pp)
    l = (n_steps + 1) * pp * rows
    cps = pp * rows // CMP_STRIDE
    flat = CMP_STRIDE * HEAD_DIM_A
    pe = cmp_pe.reshape(2, 2, 1, flat).astype(jnp.float32)
    z_spec = pl.BlockSpec((2, 1, KV_GROUPS_A, cps, flat), lambda b, s, pt: (0, b, 0, s, 0))
    r_spec = pl.BlockSpec((1, pp * rows, KV_A), lambda b, s, pt: (b, s, 0))
    z_shape = jax.ShapeDtypeStruct((2, db, KV_GROUPS_A, l // CMP_STRIDE, flat), MXU_DTYPE)
    r_shape = jax.ShapeDtypeStruct((db, l, KV_A), MXU_DTYPE)
    return pl.pallas_call(
        functools.partial(_nsa_gather_kernel, pp=pp, n_steps=n_steps, rows=rows),
        grid_spec=pltpu.PrefetchScalarGridSpec(
            num_scalar_prefetch=1,
            grid=(db, n_steps + 1),
            in_specs=[pl.BlockSpec((1, NSA_SECTIONS * rows, width), page_map(r)) for r in range(pp)]
            + [pl.BlockSpec((1, pp * NSA_SECTIONS * rows, width), lambda b, s, pt: (b, 0, 0)),
               pl.BlockSpec((2, 2, 1, flat), lambda b, s, pt: (0, 0, 0, 0))],
            out_specs=[z_spec, z_spec, r_spec, r_spec]),
        out_shape=[z_shape, z_shape, r_shape, r_shape],
        compiler_params=pltpu.CompilerParams(
            dimension_semantics=("parallel", "arbitrary"), vmem_limit_bytes=VMEM_LIMIT),
        name="nsa_gather",
    )(page_table.reshape(-1), *([cache] * pp), tail, pe)


def _dsa_gather_kernel(pt_ref, *refs, pp, n_steps, rows):
    del pt_ref
    pages, tail_ref, (k_ref, v_ref, i_ref) = refs[:pp], refs[pp], refs[pp + 1:]
    is_tail = pl.program_id(1) == n_steps
    d = HEAD_DIM_B
    for r in range(pp):
        rs = slice(r * rows, (r + 1) * rows)
        x = jnp.where(is_tail, tail_ref[0, rs, :], pages[r][0])
        k_ref[0, rs, :] = x[:, :d].astype(k_ref.dtype)
        v_ref[0, rs, :] = x[:, d:2 * d].astype(v_ref.dtype)
        i_ref[0, rs, :] = x[:, 2 * d:].astype(i_ref.dtype)


def dsa_gather(cache, page_table, tail):
    db, n_pages = page_table.shape
    rows, width = cache.shape[1:]
    pp = PAGES_PER_STEP
    n_steps, page_map = _page_maps(n_pages, pp)
    l = (n_steps + 1) * pp * rows
    out_spec = lambda w: pl.BlockSpec((1, pp * rows, w), lambda b, s, pt: (b, s, 0))
    widths = (HEAD_DIM_B, HEAD_DIM_B, IDX_DIM)
    return pl.pallas_call(
        functools.partial(_dsa_gather_kernel, pp=pp, n_steps=n_steps, rows=rows),
        grid_spec=pltpu.PrefetchScalarGridSpec(
            num_scalar_prefetch=1,
            grid=(db, n_steps + 1),
            in_specs=[pl.BlockSpec((1, rows, width), page_map(r)) for r in range(pp)]
            + [pl.BlockSpec((1, pp * rows, width), lambda b, s, pt: (b, 0, 0))],
            out_specs=[out_spec(w) for w in widths]),
        out_shape=[jax.ShapeDtypeStruct((db, l, w), MXU_DTYPE) for w in widths],
        compiler_params=pltpu.CompilerParams(
            dimension_semantics=("parallel", "arbitrary"), vmem_limit_bytes=VMEM_LIMIT),
        name="dsa_gather",
    )(page_table.reshape(-1), *([cache] * pp), tail)


def _compress_kernel(zt_ref, zb_ref, w1t_ref, w1b_ref, b1_ref, w2_ref, o_ref, ab_ref, *, ncp):
    dot = functools.partial(jnp.dot, preferred_element_type=jnp.float32)
    at = dot(zt_ref[0, 0], w1t_ref[0])
    ab_ref[...] = dot(zb_ref[0, 0], w1b_ref[0])
    h = jax.nn.gelu(at[:ncp] + ab_ref[pl.ds(1, ncp), :] + b1_ref[0])
    o_ref[0, 0, :ncp] = dot(h.astype(w2_ref.dtype), w2_ref[0]).astype(o_ref.dtype)
    if o_ref.shape[2] > ncp:
        o_ref[0, 0, ncp:] = jnp.zeros((o_ref.shape[2] - ncp, o_ref.shape[3]), o_ref.dtype)


def compress(zt, zb, w1, b1, w2, n_keys):
    ncp = _round_up(-(-n_keys // CMP_STRIDE), BF16_ROWS)
    ncl = ncp if ncp <= LANE else _round_up(ncp, LANE)
    _, nb, ch, kdim = zt.shape
    hid = w1.shape[-1]
    d = w2.shape[-1]
    assert ch >= ncp + 1
    w1 = w1.astype(MXU_DTYPE)
    z_spec = pl.BlockSpec((1, 1, ch, kdim), lambda s, n: (s, n, 0, 0))
    return pl.pallas_call(
        functools.partial(_compress_kernel, ncp=ncp),
        grid=(2, nb),
        in_specs=[z_spec, z_spec,
                  pl.BlockSpec((1, kdim, hid), lambda s, n: (s, 0, 0)),
                  pl.BlockSpec((1, kdim, hid), lambda s, n: (s, 1, 0)),
                  pl.BlockSpec((1, 1, hid), lambda s, n: (s, 0, 0)),
                  pl.BlockSpec((1, hid, d), lambda s, n: (s, 0, 0))],
        out_specs=pl.BlockSpec((1, 1, ncl, d), lambda s, n: (s, n, 0, 0)),
        out_shape=jax.ShapeDtypeStruct((2, nb, ncl, d), MXU_DTYPE),
        scratch_shapes=[pltpu.VMEM((ch, hid), jnp.float32)],
        compiler_params=pltpu.CompilerParams(
            dimension_semantics=("parallel", "parallel"), vmem_limit_bytes=VMEM_LIMIT),
        name="compress",
    )(zt, zb, w1, w1, b1.reshape(2, 1, hid).astype(jnp.float32), w2.astype(MXU_DTYPE))


def _chunk_flat(z):
    b, l, _ = z.shape
    z = z.reshape(b, l // CMP_STRIDE, CMP_STRIDE, KV_GROUPS_A, HEAD_DIM_A)
    return z.transpose(0, 3, 1, 2, 4).reshape(b * KV_GROUPS_A, l // CMP_STRIDE, CMP_STRIDE * HEAD_DIM_A)


def rmsnorm(x, g):
    xf = x.astype(jnp.float32)
    y = xf * lax.rsqrt(jnp.mean(xf * xf, axis=-1, keepdims=True) + EPS)
    return (y * g.astype(jnp.float32)).astype(x.dtype)


def rope_partial(x, pos):
    d = x.shape[-1]
    rot = d // ROT_FRACTION
    half = rot // 2
    inv = ROPE_THETA ** (-jnp.arange(half, dtype=jnp.float32) / half)
    ang = pos.astype(jnp.float32)[:, None] * inv[None, :]
    ang = ang.reshape((ang.shape[0],) + (1,) * (x.ndim - 3) + (half,))
    cos = jnp.cos(ang).astype(x.dtype)
    sin = jnp.sin(ang).astype(x.dtype)
    x1 = x[..., :half]
    x2 = x[..., half:rot]
    return jnp.concatenate([x1 * cos - x2 * sin, x2 * cos + x1 * sin, x[..., rot:]], axis=-1)


def pad_rows(a, n):
    return jnp.pad(a, [(0, 0), (0, n - a.shape[1])] + [(0, 0)] * (a.ndim - 2))


def project(xn, w_in, pos, sizes):
    b, t, d_model = xn.shape
    offsets = tuple(sum(sizes[:i + 1]) for i in range(len(sizes) - 1))
    qa, kva, ga, qb, kvb, qi, ki, wi, gm = jnp.split(matmul(xn, w_in), offsets, axis=-1)
    qa = qa.reshape(b, t, N_HEADS_A, HEAD_DIM_A)
    kva = kva.reshape(b, t, 6, KV_GROUPS_A, HEAD_DIM_A)
    kvb = kvb.reshape(b, t, 2, HEAD_DIM_B)
    gm = jax.nn.sigmoid(gm.reshape(b, t, 2, d_model))
    return {
        'q_cmp': qa, 'q_rot': rope_partial(qa, pos),
        'k_cmp': kva[:, :, 0], 'v_cmp': kva[:, :, 1],
        'k_slc': rope_partial(kva[:, :, 2], pos), 'v_slc': kva[:, :, 3],
        'k_win': rope_partial(kva[:, :, 4], pos), 'v_win': kva[:, :, 5],
        'g_nsa': jax.nn.sigmoid(ga.reshape(b, t, 3, N_HEADS_A)),
        'q_b': rope_partial(qb.reshape(b, t, N_HEADS_B, HEAD_DIM_B), pos),
        'k_b': rope_partial(kvb[:, :, 0], pos), 'v_b': kvb[:, :, 1],
        'q_idx': rope_partial(qi.reshape(b, t, IDX_HEADS, IDX_DIM), pos),
        'k_idx': rope_partial(ki, pos), 'w_idx': wi,
        'g_a': gm[:, :, 0], 'g_b': gm[:, :, 1],
    }


def _query_operands(p, t_pad):
    b, t = p['q_cmp'].shape[:2]

    def rows(a, scale=1.0):
        return pad_rows((a * scale).reshape(b, t, -1), t_pad)

    qc = rows(p['q_cmp'], HEAD_DIM_A ** -0.5).astype(MXU_DTYPE)
    qr = rows(p['q_rot'], HEAD_DIM_A ** -0.5).astype(MXU_DTYPE)
    gates = rows(p['g_nsa'])
    qb = rows(p['q_b']).astype(MXU_DTYPE)
    qi = rows(p['q_idx'], IDX_DIM ** -0.5).astype(MXU_DTYPE)
    wi = rows(p['w_idx'].astype(jnp.float32), IDX_HEADS ** -0.5)
    return qc, qr, gates, qb, qi, wi


def merge_branches(o_nsa, o_dsa, g_a, g_b, w_oa, w_ob, w_o):
    ya = matmul(o_nsa, w_oa)
    yb = matmul(o_dsa, w_ob)
    return matmul(g_a * ya + g_b * yb, w_o)


def mixer_prompt(xn, w_in, cmp_pe, cmp_w1, cmp_b1, cmp_w2, w_oa, w_ob, w_o, sizes):
    b, s, _ = xn.shape
    pos = jnp.arange(s, dtype=jnp.int32)
    p = project(xn, w_in, pos, sizes)
    rows2 = lambda a: a.reshape(b, s, -1)
    cast = lambda a: rows2(a).astype(MXU_DTYPE)
    pe = jnp.tile(cmp_pe.reshape(2, 2, CMP_STRIDE, 1, HEAD_DIM_A), (1, 1, (s + LANE) // CMP_STRIDE, KV_GROUPS_A, 1))
    pe = pe.reshape(2, 2, 1, s + LANE, KV_A)
    raw = jnp.stack([pad_rows(rows2(p['k_cmp']), s + LANE), pad_rows(rows2(p['v_cmp']), s + LANE)])
    z = (raw[:, None] + pe).astype(MXU_DTYPE)
    zt = jnp.stack([_chunk_flat(z[0, 0]), _chunk_flat(z[1, 0])])
    zb = jnp.stack([_chunk_flat(z[0, 1]), _chunk_flat(z[1, 1])])
    kvc = compress(zt, zb, cmp_w1, cmp_b1, cmp_w2, s)
    kvc = kvc.reshape(2, b, KV_GROUPS_A, kvc.shape[2], HEAD_DIM_A)
    qc, qr, gates, qb, qi, wi = _query_operands(p, s)
    o_nsa = nsa_attention(qc, qr, gates, kvc, cast(p['k_slc']), cast(p['v_slc']), cast(p['k_win']),
                          cast(p['v_win']), tq=min(TQ, s), qpos0=0, wbase=0, n_keys=s)
    o_dsa = dsa_attention(qb, qi, wi, cast(p['k_b']), cast(p['v_b']), cast(p['k_idx']),
                          tq=min(TQ, s), qpos0=0, n_keep=min(DSA_TOPK, s // 4))
    y = merge_branches(o_nsa, o_dsa, p['g_a'], p['g_b'], w_oa, w_ob, w_o)
    nsa_rows = jnp.stack([p['k_cmp'], p['v_cmp'], p['k_slc'], p['v_slc']], axis=2)
    win_state = jnp.stack([p['k_win'], p['v_win']], axis=2)[:, -min(WINDOW, s):]
    dsa_rows = jnp.concatenate([p['k_b'], p['v_b'], p['k_idx']], axis=-1)
    return y, nsa_rows, win_state, dsa_rows


def mixer_sample(xn, cache_nsa, win_buf, cache_dsa, page_table, w_in, cmp_pe, cmp_w1, cmp_b1, cmp_w2,
                 w_oa, w_ob, w_o, sizes):
    b, t, _ = xn.shape
    page = cache_nsa.shape[1]
    past_len = page_table.shape[1] * page
    n_keys = past_len + t
    tail_rows = PAGES_PER_STEP * page
    pos = past_len + jnp.arange(t, dtype=jnp.int32)
    p = project(xn, w_in, pos, sizes)
    nsa_rows = jnp.stack([p['k_cmp'], p['v_cmp'], p['k_slc'], p['v_slc']], axis=2)
    dsa_rows = jnp.concatenate([p['k_b'], p['v_b'], p['k_idx']], axis=-1)
    qc, qr, gates, qb, qi, wi = _query_operands(p, TQ_STEP)

    zt, zb, ks, vs = nsa_gather(cache_nsa.reshape(cache_nsa.shape[0], page * NSA_SECTIONS, KV_A), page_table,
                                pad_rows(nsa_rows.reshape(b, t * NSA_SECTIONS, KV_A), tail_rows * NSA_SECTIONS),
                                cmp_pe)
    merge_bg = lambda a: a.reshape((2, b * KV_GROUPS_A) + a.shape[3:])
    kvc = compress(merge_bg(zt), merge_bg(zb), cmp_w1, cmp_b1, cmp_w2, n_keys)
    kvc = kvc.reshape(2, b, KV_GROUPS_A, kvc.shape[2], HEAD_DIM_A)
    w_len = win_buf.shape[1]
    win_all = jnp.concatenate([win_buf, jnp.stack([p['k_win'], p['v_win']], axis=2)], axis=1)
    win_pad = pad_rows(win_all, w_len + WIN_CHUNK).astype(MXU_DTYPE)
    kw = win_pad[:, :, 0].reshape(b, w_len + WIN_CHUNK, KV_A)
    vw = win_pad[:, :, 1].reshape(b, w_len + WIN_CHUNK, KV_A)
    o_nsa = nsa_attention(qc, qr, gates, kvc, ks, vs, kw, vw,
                          tq=TQ_STEP, qpos0=past_len, wbase=past_len - w_len, n_keys=n_keys)

    kb_, vb, ki = dsa_gather(cache_dsa, page_table, pad_rows(dsa_rows, tail_rows))
    o_dsa = dsa_attention(qb, qi, wi, kb_, vb, ki, tq=TQ_STEP, qpos0=past_len, n_keep=min(DSA_TOPK, n_keys // 4))
    y = merge_branches(o_nsa[:, :t], o_dsa[:, :t], p['g_a'], p['g_b'], w_oa, w_ob, w_o)
    return y, nsa_rows, win_all[:, -w_len:], dsa_rows


def mem_kv(mem, g, w_kv):
    b, m, _ = mem.shape
    return matmul(rmsnorm(mem, g), w_kv).reshape(b, m, 2, MEM_HEADS, MEM_HEAD_DIM)


def mem_attend(hn, kv, w_q, w_o):
    b, t, _ = hn.shape
    q = matmul(hn, w_q).reshape(b, t, MEM_HEADS, MEM_HEAD_DIM)
    s = jnp.einsum('bthd,bmhd->bthm', q, kv[:, :, 0]).astype(jnp.float32) * MEM_HEAD_DIM ** -0.5
    p = jax.nn.softmax(s, axis=-1).astype(kv.dtype)
    o = jnp.einsum('bthm,bmhd->bthd', p, kv[:, :, 1])
    return matmul(o.reshape(b, t, -1), w_o)


def conv_ffn(hn, prev, w_up, conv_w, conv_b, w_down):
    t = hn.shape[1]
    ext = jnp.concatenate([prev, matmul(hn, w_up)], axis=1)
    c = conv_b + sum(ext[:, j:j + t] * conv_w[j] for j in range(CONV_WIDTH))
    gate, up = jnp.split(c, 2, axis=-1)
    return matmul(jax.nn.silu(gate) * up, w_down), ext[:, t:]


def kernel(x_prompt, x_sample, mem_prompt, cache_nsa_kv, state_nsa_win, cache_dsa_kv, cache_mem_kv, state_conv,
           page_table, norm_g, w_in, cmp_pe, cmp_w1, cmp_b1, cmp_w2, w_out_a, w_out_b, w_out, w_mem_q, w_mem_kv,
           w_mem_out, w_up, conv_w, conv_b, w_down, final_g):
    depth = w_in.shape[0]
    d_model = x_prompt.shape[-1]
    d_ff = w_down.shape[1]
    sizes = (N_HEADS_A * HEAD_DIM_A, 6 * KV_A, 3 * N_HEADS_A, N_HEADS_B * HEAD_DIM_B, 2 * HEAD_DIM_B,
             IDX_HEADS * IDX_DIM, IDX_DIM, IDX_HEADS, 2 * d_model)
    xp, xs = x_prompt, x_sample
    nsa_p, nsa_s, win_p, win_s, dsa_p, dsa_s, mem_p, conv_p, conv_s = [], [], [], [], [], [], [], [], []
    for l in range(depth):
        yp, a, bwin, c = mixer_prompt(rmsnorm(xp, norm_g[l, 0]), w_in[l], cmp_pe[l], cmp_w1[l], cmp_b1[l], cmp_w2[l],
                                      w_out_a[l], w_out_b[l], w_out[l], sizes)
        nsa_p.append(a); win_p.append(bwin); dsa_p.append(c)
        ys, a, bwin, c = mixer_sample(rmsnorm(xs, norm_g[l, 0]), cache_nsa_kv[l], state_nsa_win[l], cache_dsa_kv[l],
                                      page_table, w_in[l], cmp_pe[l], cmp_w1[l], cmp_b1[l], cmp_w2[l],
                                      w_out_a[l], w_out_b[l], w_out[l], sizes)
        nsa_s.append(a); win_s.append(bwin); dsa_s.append(c)
        xp = xp + yp
        xs = xs + ys
        kv_p = mem_kv(mem_prompt, norm_g[l, 2], w_mem_kv[l])
        mem_p.append(kv_p)
        xp = xp + mem_attend(rmsnorm(xp, norm_g[l, 1]), kv_p, w_mem_q[l], w_mem_out[l])
        xs = xs + mem_attend(rmsnorm(xs, norm_g[l, 1]), cache_mem_kv[l], w_mem_q[l], w_mem_out[l])
        fp, cp = conv_ffn(rmsnorm(xp, norm_g[l, 3]), jnp.zeros((xp.shape[0], CONV_WIDTH - 1, 2 * d_ff), xp.dtype),
                          w_up[l], conv_w[l], conv_b[l], w_down[l])
        fs, cs = conv_ffn(rmsnorm(xs, norm_g[l, 3]), state_conv[l], w_up[l], conv_w[l], conv_b[l], w_down[l])
        conv_p.append(cp); conv_s.append(cs)
        xp = xp + fp
        xs = xs + fs
    y_prompt = rmsnorm(xp, final_g)
    y_sample = rmsnorm(xs, final_g)
    return (y_prompt, y_sample, jnp.stack(nsa_p), jnp.stack(nsa_s), jnp.stack(win_p), jnp.stack(win_s),
            jnp.stack(dsa_p), jnp.stack(dsa_s), jnp.stack(mem_p), jnp.stack(conv_p), jnp.stack(conv_s))
```

```python
import functools

import jax
import jax.numpy as jnp
from jax import lax
from jax.experimental import pallas as pl
from jax.experimental.pallas import tpu as pltpu

N_HEADS_A = 16
HEAD_DIM_A = 64
KV_GROUPS_A = 2
CMP_BLOCK = 32
CMP_STRIDE = 16
SLC_BLOCK = 64
N_SELECT = 16
WINDOW = 512
N_HEADS_B = 8
HEAD_DIM_B = 128
IDX_HEADS = 16
IDX_DIM = 64
DSA_TOPK = 256
MEM_HEADS = 4
MEM_HEAD_DIM = 128
CONV_WIDTH = 3
ROPE_THETA = 500000.0
ROT_FRACTION = 4
EPS = 1e-6
KV_A = KV_GROUPS_A * HEAD_DIM_A
CMP_PER_SLC = SLC_BLOCK // CMP_STRIDE
HPG_A = N_HEADS_A // KV_GROUPS_A
DSA_ROW = 2 * HEAD_DIM_B + IDX_DIM
NSA_SECTIONS = 4

LANE = 128
SUBLANE = 8
BF16_ROWS = 16
VMEM_LIMIT = 48 * 1024 * 1024

NEG_INF = float('-inf')
POS_INF = float('inf')
MXU_DTYPE = jnp.bfloat16
TQ = 128
TQ_STEP = BF16_ROWS
KEY_CHUNK = 512
WIN_CHUNK = 128
PAGES_PER_STEP = 4
BISECT_ITERS = 40
SLC_SHIFT = SLC_BLOCK.bit_length() - 1
CMP_PER_SLC_SHIFT = CMP_PER_SLC.bit_length() - 1


def _round_up(n, m):
    return -(-n // m) * m


def _tile(n, cap):
    if n <= cap:
        return n
    best = None
    for t in range(LANE, cap + 1, LANE):
        if n % t == 0:
            best = t
    assert best is not None, (n, cap)
    return best


def _mm_kernel(x_ref, w_ref, o_ref, acc_ref):
    @pl.when(pl.program_id(2) == 0)
    def _():
        acc_ref[...] = jnp.zeros_like(acc_ref)

    acc_ref[...] += jnp.dot(x_ref[...], w_ref[...], preferred_element_type=jnp.float32)

    @pl.when(pl.program_id(2) == pl.num_programs(2) - 1)
    def _():
        o_ref[...] = acc_ref[...].astype(o_ref.dtype)


def matmul(x, w, out_dtype=jnp.float32):
    lead = x.shape[:-1]
    k = x.shape[-1]
    n = w.shape[-1]
    x2 = x.reshape(-1, k).astype(MXU_DTYPE)
    w2 = w.astype(MXU_DTYPE)
    m0 = x2.shape[0]
    tm = 512 if m0 >= 512 else _round_up(m0, BF16_ROWS)
    m = _round_up(m0, tm)
    if m != m0:
        x2 = jnp.pad(x2, ((0, m - m0), (0, 0)))
    tn = _tile(n, 1024)
    tk = _tile(k, 2048)
    out = pl.pallas_call(
        _mm_kernel,
        grid=(m // tm, n // tn, k // tk),
        in_specs=[pl.BlockSpec((tm, tk), lambda i, j, l: (i, l)),
                  pl.BlockSpec((tk, tn), lambda i, j, l: (l, j))],
        out_specs=pl.BlockSpec((tm, tn), lambda i, j, l: (i, j)),
        out_shape=jax.ShapeDtypeStruct((m, n), out_dtype),
        scratch_shapes=[pltpu.VMEM((tm, tn), jnp.float32)],
        compiler_params=pltpu.CompilerParams(
            dimension_semantics=("parallel", "parallel", "arbitrary"),
            vmem_limit_bytes=VMEM_LIMIT),
        name="matmul",
    )(x2, w2)
    return out[:m0].reshape(lead + (n,))


def _dot_nt(a, b):
    return lax.dot_general(a, b, (((1,), (1,)), ((), ())), preferred_element_type=jnp.float32)


def _iota(shape, dim):
    return lax.broadcasted_iota(jnp.int32, shape, dim)


def _flash_init(rows, d):
    return (jnp.full((rows, 1), NEG_INF, jnp.float32), jnp.zeros((rows, 1), jnp.float32),
            jnp.zeros((rows, d), jnp.float32))


def _flash_step(carry, q, k, v, madd, nh, scale=None):
    m, l, acc = carry
    s = _dot_nt(q, k)
    if scale is not None:
        s = s * scale
    r, kb = s.shape
    s = (s.reshape(nh, r // nh, kb) + madd[None]).reshape(r, kb)
    m_new = jnp.maximum(m, jnp.max(s, axis=1, keepdims=True))
    m_safe = jnp.where(m_new == NEG_INF, 0.0, m_new)
    p = jnp.exp(s - m_safe)
    alpha = jnp.exp(m - m_safe)
    l = alpha * l + jnp.sum(p, axis=1, keepdims=True)
    acc = alpha * acc + jnp.dot(p.astype(v.dtype), v, preferred_element_type=jnp.float32)
    return m_new, l, acc


def _flash_finish(carry):
    _, l, acc = carry
    return acc / jnp.maximum(l, 1e-30)


def _split_dot(x, m01):
    hi = x.astype(jnp.bfloat16)
    r1 = x - hi.astype(jnp.float32)
    mid = r1.astype(jnp.bfloat16)
    lo = (r1 - mid.astype(jnp.float32)).astype(jnp.bfloat16)
    dot = functools.partial(jnp.dot, preferred_element_type=jnp.float32)
    return dot(hi, m01) + dot(mid, m01) + dot(lo, m01)


def _nsa_group(g, i, qc_ref, qr_ref, gate_ref, kc_ref, vc_ref, ks_ref, vs_ref, kw_ref, vw_ref, mask_ref,
               *, tq, kb, qpos0, wbase, ns, n_sel):
    nh, d = HPG_A, HEAD_DIM_A
    rows = nh * tq
    nc = kc_ref.shape[3]
    nsp = ns if ns <= LANE else _round_up(ns, LANE)
    t0 = qpos0 + i * tq
    nj = (t0 + tq - 1) // kb + 1
    cols = slice(g * d, (g + 1) * d)

    def stack_heads(q_ref):
        return jnp.concatenate([q_ref[0, :, (g * nh + h) * d:(g * nh + h + 1) * d] for h in range(nh)], axis=0)

    qc = stack_heads(qc_ref)
    qr = stack_heads(qr_ref)
    tpos = t0 + _iota((tq, 1), 0)

    blk_last = _iota((1, nc), 1) * CMP_STRIDE + (CMP_BLOCK - 1)
    madd_c = jnp.where(blk_last <= tpos, 0.0, NEG_INF)
    s = _dot_nt(qc, kc_ref[0, 0, g]).reshape(nh, tq, nc) + madd_c[None]
    m = jnp.max(s, axis=2, keepdims=True)
    m = jnp.where(m == NEG_INF, 0.0, m)
    e = jnp.exp(s - m)
    p = e / jnp.maximum(jnp.sum(e, axis=2, keepdims=True), 1e-30)
    o_cmp = jnp.dot(p.reshape(rows, nc).astype(vc_ref.dtype), vc_ref[0, 0, g], preferred_element_type=jnp.float32)

    imp = jnp.sum(p, axis=0)
    c_id = _iota((nc, nsp), 0)
    m_id = _iota((nc, nsp), 1)
    overlap = (jnp.right_shift(c_id, CMP_PER_SLC_SHIFT) == m_id) | (c_id == m_id * CMP_PER_SLC - 1)
    score = _split_dot(imp, overlap.astype(jnp.bfloat16))
    blk = _iota((1, nsp), 1)
    cur = jnp.right_shift(tpos, SLC_SHIFT)
    forced = (blk == 0) | (blk == cur) | (blk == cur - 1)
    sc = jnp.where(forced, POS_INF, jnp.where(blk * SLC_BLOCK <= tpos, score, NEG_INF))
    rank = jnp.zeros((tq, nsp), jnp.float32)
    for mp in range(ns):
        col = sc[:, mp:mp + 1]
        beats = (col > sc) | ((col == sc) & (blk > mp))
        rank = rank + jnp.where(beats, 1.0, 0.0)
    sel = jnp.where(rank < n_sel, 1.0, 0.0).astype(jnp.bfloat16)

    def make_mask(j, _):
        kpos = j * kb + _iota((1, kb), 1)
        expand = (jnp.right_shift(j * kb + _iota((nsp, kb), 1), SLC_SHIFT) == _iota((nsp, kb), 0))
        hit = jnp.dot(sel, expand.astype(jnp.bfloat16), preferred_element_type=jnp.float32)
        mask_ref[j] = jnp.where((hit > 0.5) & (kpos <= tpos), 0.0, NEG_INF)
        return 0

    lax.fori_loop(0, nj, make_mask, 0)

    def slc_body(j, carry):
        off = pl.multiple_of(j * kb, kb)
        return _flash_step(carry, qr, ks_ref[0, pl.ds(off, kb), cols], vs_ref[0, pl.ds(off, kb), cols],
                           mask_ref[j], nh)

    o_slc = _flash_finish(lax.fori_loop(0, nj, slc_body, _flash_init(rows, d)))

    wk = min(_round_up(WINDOW + tq, WIN_CHUNK), kw_ref.shape[1])
    first = jnp.maximum(t0 - (WINDOW - 1) - wbase, 0) // WIN_CHUNK * WIN_CHUNK
    off = pl.multiple_of(jnp.minimum(first, kw_ref.shape[1] - wk), WIN_CHUNK)
    dist = tpos - (wbase + off + _iota((1, wk), 1))
    madd_w = jnp.where((dist >= 0) & (dist < WINDOW), 0.0, NEG_INF)
    o_win = _flash_finish(_flash_step(_flash_init(rows, d), qr, kw_ref[0, pl.ds(off, wk), cols],
                                      vw_ref[0, pl.ds(off, wk), cols], madd_w, nh))

    gates = gate_ref[0]
    outs = []
    for h in range(nh):
        rs = slice(h * tq, (h + 1) * tq)
        c = g * nh + h
        outs.append(gates[:, c:c + 1] * o_cmp[rs] + gates[:, N_HEADS_A + c:N_HEADS_A + c + 1] * o_slc[rs]
                    + gates[:, 2 * N_HEADS_A + c:2 * N_HEADS_A + c + 1] * o_win[rs])
    return outs


def _nsa_kernel(qc_ref, qr_ref, gate_ref, kc_ref, vc_ref, ks_ref, vs_ref, kw_ref, vw_ref, o_ref, mask_ref, **kw):
    i = pl.program_id(1)
    outs = []
    for g in range(KV_GROUPS_A):
        outs += _nsa_group(g, i, qc_ref, qr_ref, gate_ref, kc_ref, vc_ref, ks_ref, vs_ref, kw_ref, vw_ref,
                           mask_ref, **kw)
    o_ref[0] = jnp.concatenate(outs, axis=1).astype(o_ref.dtype)


def nsa_attention(qc, qr, gates, kvc, ks, vs, kw, vw, *, tq, qpos0, wbase, n_keys):
    b, t, hd = qc.shape
    l = ks.shape[1]
    lw = kw.shape[1]
    nc, d = kvc.shape[3:]
    kb = min(KEY_CHUNK, l) if tq >= TQ else l
    assert l % kb == 0 and t % tq == 0 and lw % WIN_CHUNK == 0
    assert (qpos0 + t - 1) // kb + 1 <= l // kb and (qpos0 + t - 1 - wbase) // WIN_CHUNK + 1 <= lw // WIN_CHUNK
    assert (qpos0 - wbase) % WIN_CHUNK == 0 and WIN_CHUNK % tq == 0 and tq > 1
    ns = l // SLC_BLOCK
    n_sel = min(N_SELECT, -(-n_keys // SLC_BLOCK))
    q_spec = pl.BlockSpec((1, tq, hd), lambda bi, i: (bi, i, 0))
    kc_spec = pl.BlockSpec((1, 1, KV_GROUPS_A, nc, d), lambda bi, i: (0, bi, 0, 0, 0))
    vc_spec = pl.BlockSpec((1, 1, KV_GROUPS_A, nc, d), lambda bi, i: (1, bi, 0, 0, 0))
    k_spec = pl.BlockSpec((1, l, KV_A), lambda bi, i: (bi, 0, 0))
    w_spec = pl.BlockSpec((1, lw, KV_A), lambda bi, i: (bi, 0, 0))
    return pl.pallas_call(
        functools.partial(_nsa_kernel, tq=tq, kb=kb, qpos0=qpos0, wbase=wbase, ns=ns, n_sel=n_sel),
        grid=(b, t // tq),
        in_specs=[q_spec, q_spec, pl.BlockSpec((1, tq, 3 * N_HEADS_A), lambda bi, i: (bi, i, 0)),
                  kc_spec, vc_spec, k_spec, k_spec, w_spec, w_spec],
        out_specs=q_spec,
        out_shape=jax.ShapeDtypeStruct((b, t, hd), MXU_DTYPE),
        scratch_shapes=[pltpu.VMEM((l // kb, tq, kb), jnp.float32)],
        compiler_params=pltpu.CompilerParams(
            dimension_semantics=("parallel", "arbitrary"), vmem_limit_bytes=VMEM_LIMIT),
        name="nsa_attention",
    )(qc, qr, gates, kvc, kvc, ks, vs, kw, vw)


def _dsa_kernel(qb_ref, qi_ref, wi_ref, kb_ref, vb_ref, ki_ref, o_ref, score_ref, *, tq, kb, qpos0, n_keep):
    i = pl.program_id(1)
    nh, d = N_HEADS_B, HEAD_DIM_B
    t0 = qpos0 + i * tq
    nj = (t0 + tq - 1) // kb + 1
    tpos = t0 + _iota((tq, 1), 0)
    w = wi_ref[0]

    def idx_body(j, carry):
        lo, hi = carry
        off = pl.multiple_of(j * kb, kb)
        kidx = ki_ref[0, pl.ds(off, kb), :]
        acc = jnp.zeros((tq, kb), jnp.float32)
        for h in range(IDX_HEADS):
            qh = qi_ref[0, :, h * IDX_DIM:(h + 1) * IDX_DIM]
            acc = acc + w[:, h:h + 1] * jnp.maximum(_dot_nt(qh, kidx), 0.0)
        vis = (off + _iota((1, kb), 1)) <= tpos
        score_ref[j] = jnp.where(vis, acc, NEG_INF)
        lo = jnp.minimum(lo, jnp.min(jnp.where(vis, acc, POS_INF), axis=1, keepdims=True))
        hi = jnp.maximum(hi, jnp.max(jnp.where(vis, acc, NEG_INF), axis=1, keepdims=True))
        return lo, hi

    lo, hi = lax.fori_loop(0, nj, idx_body, (jnp.full((tq, 1), POS_INF, jnp.float32),
                                             jnp.full((tq, 1), NEG_INF, jnp.float32)))

    k = float(n_keep)
    n_vis = (tpos + 1).astype(jnp.float32)

    def unsettled(cnt_lo):
        return jnp.max(jnp.where((cnt_lo != k) & (n_vis > k), 1.0, 0.0)) > 0.0

    def bisect(state):
        it, lo, hi, cnt_lo = state
        mid = 0.5 * (lo + hi)
        mid_b = jnp.broadcast_to(mid, (tq, LANE))

        def count(j, acc):
            sc = score_ref[j]
            for c in range(kb // LANE):
                acc = acc + jnp.where(sc[:, c * LANE:(c + 1) * LANE] >= mid_b, 1.0, 0.0)
            return acc

        cnt = jnp.sum(lax.fori_loop(0, nj, count, jnp.zeros((tq, LANE), jnp.float32)), axis=1, keepdims=True)
        ge = cnt >= k
        return it + 1, jnp.where(ge, mid, lo), jnp.where(ge, hi, mid), jnp.where(ge, cnt, cnt_lo)

    _, thr, _, _ = lax.while_loop(lambda st: (st[0] < BISECT_ITERS) & unsettled(st[3]), bisect,
                                  (jnp.int32(0), lo, hi, n_vis))

    q = jnp.concatenate([qb_ref[0, :, h * d:(h + 1) * d] for h in range(nh)], axis=0)

    def att_body(j, carry):
        off = pl.multiple_of(j * kb, kb)
        madd = jnp.where(score_ref[j] >= thr, 0.0, NEG_INF)
        return _flash_step(carry, q, kb_ref[0, pl.ds(off, kb), :], vb_ref[0, pl.ds(off, kb), :], madd, nh,
                           scale=d ** -0.5)

    o = _flash_finish(lax.fori_loop(0, nj, att_body, _flash_init(nh * tq, d)))
    o_ref[0] = jnp.concatenate([o[h * tq:(h + 1) * tq] for h in range(nh)], axis=1).astype(o_ref.dtype)


def dsa_attention(qb, qi, wi, kb_, vb, ki, *, tq, qpos0, n_keep):
    b, t, hd = qb.shape
    l = kb_.shape[1]
    kb = min(KEY_CHUNK, l) if tq >= TQ else l
    assert l % kb == 0 and t % tq == 0 and (qpos0 + t - 1) // kb + 1 <= l // kb
    kv_spec = lambda width: pl.BlockSpec((1, l, width), lambda bi, i: (bi, 0, 0))
    return pl.pallas_call(
        functools.partial(_dsa_kernel, tq=tq, kb=kb, qpos0=qpos0, n_keep=n_keep),
        grid=(b, t // tq),
        in_specs=[pl.BlockSpec((1, tq, hd), lambda bi, i: (bi, i, 0)),
                  pl.BlockSpec((1, tq, IDX_HEADS * IDX_DIM), lambda bi, i: (bi, i, 0)),
                  pl.BlockSpec((1, tq, IDX_HEADS), lambda bi, i: (bi, i, 0)),
                  kv_spec(HEAD_DIM_B), kv_spec(HEAD_DIM_B), kv_spec(IDX_DIM)],
        out_specs=pl.BlockSpec((1, tq, hd), lambda bi, i: (bi, i, 0)),
        out_shape=jax.ShapeDtypeStruct((b, t, hd), MXU_DTYPE),
        scratch_shapes=[pltpu.VMEM((l // kb, tq, kb), jnp.float32)],
        compiler_params=pltpu.CompilerParams(
            dimension_semantics=("parallel", "arbitrary"), vmem_limit_bytes=VMEM_LIMIT),
        name="dsa_attention",
    )(qb, qi, wi, kb_, vb, ki)


def _page_maps(n_pages, pp):
    n_steps = n_pages // pp

    def page_map(r):
        return lambda b, s, pt: (pt[b * n_pages + jnp.minimum(s, n_steps - 1) * pp + r], 0, 0)

    return n_steps, page_map


def _nsa_gather_kernel(pt_ref, *refs, pp, n_steps, rows):
    del pt_ref
    pages, (tail_ref, pe_ref), (zt_ref, zb_ref, ks_ref, vs_ref) = refs[:pp], refs[pp:pp + 2], refs[pp + 2:]
    is_tail = pl.program_id(1) == n_steps
    cpp = rows // CMP_STRIDE
    d = HEAD_DIM_A
    first_half = _iota((cpp, KV_A), 1) < d
    z = [[[] for _ in range(KV_GROUPS_A)] for _ in range(2)]

    def strided(r, first, n, step):
        return jnp.where(is_tail, tail_ref[0, pl.ds(r * NSA_SECTIONS * rows + first, n, stride=step), :],
                         pages[r][0, pl.ds(first, n, stride=step), :])

    for r in range(pp):
        rs = slice(r * rows, (r + 1) * rows)
        ks_ref[0, rs, :] = strided(r, 2, rows, NSA_SECTIONS).astype(ks_ref.dtype)
        vs_ref[0, rs, :] = strided(r, 3, rows, NSA_SECTIONS).astype(vs_ref.dtype)
        pieces = [[[] for _ in range(KV_GROUPS_A)] for _ in range(2)]
        for j in range(0, CMP_STRIDE, 2):
            for sec in range(2):
                a, b = [strided(r, NSA_SECTIONS * jj + sec, cpp, NSA_SECTIONS * CMP_STRIDE) for jj in (j, j + 1)]
                pieces[sec][0].append(jnp.where(first_half, a, pltpu.roll(b, d, 1)))
                pieces[sec][1].append(jnp.where(first_half, pltpu.roll(a, d, 1), b))
        for sec in range(2):
            for g in range(KV_GROUPS_A):
                z[sec][g].append(jnp.concatenate(pieces[sec][g], axis=1))
    for sec in range(2):
        for g in range(KV_GROUPS_A):
            zf = jnp.concatenate(z[sec][g], axis=0)
            zt_ref[sec, 0, g] = (zf + pe_ref[sec, 0]).astype(zt_ref.dtype)
            zb_ref[sec, 0, g] = (zf + pe_ref[sec, 1]).astype(zb_ref.dtype)


def nsa_gather(cache, page_table, tail, cmp_pe):
    db, n_pages = page_table.shape
    width = cache.shape[2]
    rows = cache.shape[1] // NSA_SECTIONS
    pp = PAGES_PER_STEP
    n_steps, page_map = _page_maps(n_pages, pp)
    l = (n_steps + 1) * pp * rows
    cps = pp * rows // CMP_STRIDE
    flat = CMP_STRIDE * HEAD_DIM_A
    pe = cmp_pe.reshape(2, 2, 1, flat).astype(jnp.float32)
    z_spec = pl.BlockSpec((2, 1, KV_GROUPS_A, cps, flat), lambda b, s, pt: (0, b, 0, s, 0))
    r_spec = pl.BlockSpec((1, pp * rows, KV_A), lambda b, s, pt: (b, s, 0))
    z_shape = jax.ShapeDtypeStruct((2, db, KV_GROUPS_A, l // CMP_STRIDE, flat), MXU_DTYPE)
    r_shape = jax.ShapeDtypeStruct((db, l, KV_A), MXU_DTYPE)
    return pl.pallas_call(
        functools.partial(_nsa_gather_kernel, pp=pp, n_steps=n_steps, rows=rows),
        grid_spec=pltpu.PrefetchScalarGridSpec(
            num_scalar_prefetch=1,
            grid=(db, n_steps + 1),
            in_specs=[pl.BlockSpec((1, NSA_SECTIONS * rows, width), page_map(r)) for r in range(pp)]
            + [pl.BlockSpec((1, pp * NSA_SECTIONS * rows, width), lambda b, s, pt: (b, 0, 0)),
               pl.BlockSpec((2, 2, 1, flat), lambda b, s, pt: (0, 0, 0, 0))],
            out_specs=[z_spec, z_spec, r_spec, r_spec]),
        out_shape=[z_shape, z_shape, r_shape, r_shape],
        compiler_params=pltpu.CompilerParams(
            dimension_semantics=("parallel", "arbitrary"), vmem_limit_bytes=VMEM_LIMIT),
        name="nsa_gather",
    )(page_table.reshape(-1), *([cache] * pp), tail, pe)


def _dsa_gather_kernel(pt_ref, *refs, pp, n_steps, rows):
    del pt_ref
    pages, tail_ref, (k_ref, v_ref, i_ref) = refs[:pp], refs[pp], refs[pp + 1:]
    is_tail = pl.program_id(1) == n_steps
    d = HEAD_DIM_B
    for r in range(pp):
        rs = slice(r * rows, (r + 1) * rows)
        x = jnp.where(is_tail, tail_ref[0, rs, :], pages[r][0])
        k_ref[0, rs, :] = x[:, :d].astype(k_ref.dtype)
        v_ref[0, rs, :] = x[:, d:2 * d].astype(v_ref.dtype)
        i_ref[0, rs, :] = x[:, 2 * d:].astype(i_ref.dtype)


def dsa_gather(cache, page_table, tail):
    db, n_pages = page_table.shape
    rows, width = cache.shape[1:]
    pp = PAGES_PER_STEP
    n_steps, page_map = _page_maps(n_pages, pp)
    l = (n_steps + 1) * pp * rows
    out_spec = lambda w: pl.BlockSpec((1, pp * rows, w), lambda b, s, pt: (b, s, 0))
    widths = (HEAD_DIM_B, HEAD_DIM_B, IDX_DIM)
    return pl.pallas_call(
        functools.partial(_dsa_gather_kernel, pp=pp, n_steps=n_steps, rows=rows),
        grid_spec=pltpu.PrefetchScalarGridSpec(
            num_scalar_prefetch=1,
            grid=(db, n_steps + 1),
            in_specs=[pl.BlockSpec((1, rows, width), page_map(r)) for r in range(pp)]
            + [pl.BlockSpec((1, pp * rows, width), lambda b, s, pt: (b, 0, 0))],
            out_specs=[out_spec(w) for w in widths]),
        out_shape=[jax.ShapeDtypeStruct((db, l, w), MXU_DTYPE) for w in widths],
        compiler_params=pltpu.CompilerParams(
            dimension_semantics=("parallel", "arbitrary"), vmem_limit_bytes=VMEM_LIMIT),
        name="dsa_gather",
    )(page_table.reshape(-1), *([cache] * pp), tail)


def _compress_kernel(zt_ref, zb_ref, w1t_ref, w1b_ref, b1_ref, w2_ref, o_ref, ab_ref, *, ncp):
    dot = functools.partial(jnp.dot, preferred_element_type=jnp.float32)
    at = dot(zt_ref[0, 0], w1t_ref[0])
    ab_ref[...] = dot(zb_ref[0, 0], w1b_ref[0])
    h = jax.nn.gelu(at[:ncp] + ab_ref[pl.ds(1, ncp), :] + b1_ref[0])
    o_ref[0, 0, :ncp] = dot(h.astype(w2_ref.dtype), w2_ref[0]).astype(o_ref.dtype)
    if o_ref.shape[2] > ncp:
        o_ref[0, 0, ncp:] = jnp.zeros((o_ref.shape[2] - ncp, o_ref.shape[3]), o_ref.dtype)


def compress(zt, zb, w1, b1, w2, n_keys):
    ncp = _round_up(-(-n_keys // CMP_STRIDE), BF16_ROWS)
    ncl = ncp if ncp <= LANE else _round_up(ncp, LANE)
    _, nb, ch, kdim = zt.shape
    hid = w1.shape[-1]
    d = w2.shape[-1]
    assert ch >= ncp + 1
    w1 = w1.astype(MXU_DTYPE)
    z_spec = pl.BlockSpec((1, 1, ch, kdim), lambda s, n: (s, n, 0, 0))
    return pl.pallas_call(
        functools.partial(_compress_kernel, ncp=ncp),
        grid=(2, nb),
        in_specs=[z_spec, z_spec,
                  pl.BlockSpec((1, kdim, hid), lambda s, n: (s, 0, 0)),
                  pl.BlockSpec((1, kdim, hid), lambda s, n: (s, 1, 0)),
                  pl.BlockSpec((1, 1, hid), lambda s, n: (s, 0, 0)),
                  pl.BlockSpec((1, hid, d), lambda s, n: (s, 0, 0))],
        out_specs=pl.BlockSpec((1, 1, ncl, d), lambda s, n: (s, n, 0, 0)),
        out_shape=jax.ShapeDtypeStruct((2, nb, ncl, d), MXU_DTYPE),
        scratch_shapes=[pltpu.VMEM((ch, hid), jnp.float32)],
        compiler_params=pltpu.CompilerParams(
            dimension_semantics=("parallel", "parallel"), vmem_limit_bytes=VMEM_LIMIT),
        name="compress",
    )(zt, zb, w1, w1, b1.reshape(2, 1, hid).astype(jnp.float32), w2.astype(MXU_DTYPE))


def _chunk_flat(z):
    b, l, _ = z.shape
    z = z.reshape(b, l // CMP_STRIDE, CMP_STRIDE, KV_GROUPS_A, HEAD_DIM_A)
    return z.transpose(0, 3, 1, 2, 4).reshape(b * KV_GROUPS_A, l // CMP_STRIDE, CMP_STRIDE * HEAD_DIM_A)


def rmsnorm(x, g):
    xf = x.astype(jnp.float32)
    y = xf * lax.rsqrt(jnp.mean(xf * xf, axis=-1, keepdims=True) + EPS)
    return (y * g.astype(jnp.float32)).astype(x.dtype)


def rope_partial(x, pos):
    d = x.shape[-1]
    rot = d // ROT_FRACTION
    half = rot // 2
    inv = ROPE_THETA ** (-jnp.arange(half, dtype=jnp.float32) / half)
    ang = pos.astype(jnp.float32)[:, None] * inv[None, :]
    ang = ang.reshape((ang.shape[0],) + (1,) * (x.ndim - 3) + (half,))
    cos = jnp.cos(ang).astype(x.dtype)
    sin = jnp.sin(ang).astype(x.dtype)
    x1 = x[..., :half]
    x2 = x[..., half:rot]
    return jnp.concatenate([x1 * cos - x2 * sin, x2 * cos + x1 * sin, x[..., rot:]], axis=-1)


def pad_rows(a, n):
    return jnp.pad(a, [(0, 0), (0, n - a.shape[1])] + [(0, 0)] * (a.ndim - 2))


def project(xn, w_in, pos, sizes):
    b, t, d_model = xn.shape
    offsets = tuple(sum(sizes[:i + 1]) for i in range(len(sizes) - 1))
    qa, kva, ga, qb, kvb, qi, ki, wi, gm = jnp.split(matmul(xn, w_in), offsets, axis=-1)
    qa = qa.reshape(b, t, N_HEADS_A, HEAD_DIM_A)
    kva = kva.reshape(b, t, 6, KV_GROUPS_A, HEAD_DIM_A)
    kvb = kvb.reshape(b, t, 2, HEAD_DIM_B)
    gm = jax.nn.sigmoid(gm.reshape(b, t, 2, d_model))
    return {
        'q_cmp': qa, 'q_rot': rope_partial(qa, pos),
        'k_cmp': kva[:, :, 0], 'v_cmp': kva[:, :, 1],
        'k_slc': rope_partial(kva[:, :, 2], pos), 'v_slc': kva[:, :, 3],
        'k_win': rope_partial(kva[:, :, 4], pos), 'v_win': kva[:, :, 5],
        'g_nsa': jax.nn.sigmoid(ga.reshape(b, t, 3, N_HEADS_A)),
        'q_b': rope_partial(qb.reshape(b, t, N_HEADS_B, HEAD_DIM_B), pos),
        'k_b': rope_partial(kvb[:, :, 0], pos), 'v_b': kvb[:, :, 1],
        'q_idx': rope_partial(qi.reshape(b, t, IDX_HEADS, IDX_DIM), pos),
        'k_idx': rope_partial(ki, pos), 'w_idx': wi,
        'g_a': gm[:, :, 0], 'g_b': gm[:, :, 1],
    }


def _query_operands(p, t_pad):
    b, t = p['q_cmp'].shape[:2]

    def rows(a, scale=1.0):
        return pad_rows((a * scale).reshape(b, t, -1), t_pad)

    qc = rows(p['q_cmp'], HEAD_DIM_A ** -0.5).astype(MXU_DTYPE)
    qr = rows(p['q_rot'], HEAD_DIM_A ** -0.5).astype(MXU_DTYPE)
    gates = rows(p['g_nsa'])
    qb = rows(p['q_b']).astype(MXU_DTYPE)
    qi = rows(p['q_idx'], IDX_DIM ** -0.5).astype(MXU_DTYPE)
    wi = rows(p['w_idx'].astype(jnp.float32), IDX_HEADS ** -0.5)
    return qc, qr, gates, qb, qi, wi


def merge_branches(o_nsa, o_dsa, g_a, g_b, w_oa, w_ob, w_o):
    ya = matmul(o_nsa, w_oa)
    yb = matmul(o_dsa, w_ob)
    return matmul(g_a * ya + g_b * yb, w_o)


def mixer_prompt(xn, w_in, cmp_pe, cmp_w1, cmp_b1, cmp_w2, w_oa, w_ob, w_o, sizes):
    b, s, _ = xn.shape
    pos = jnp.arange(s, dtype=jnp.int32)
    p = project(xn, w_in, pos, sizes)
    rows2 = lambda a: a.reshape(b, s, -1)
    cast = lambda a: rows2(a).astype(MXU_DTYPE)
    pe = jnp.tile(cmp_pe.reshape(2, 2, CMP_STRIDE, 1, HEAD_DIM_A), (1, 1, (s + LANE) // CMP_STRIDE, KV_GROUPS_A, 1))
    pe = pe.reshape(2, 2, 1, s + LANE, KV_A)
    raw = jnp.stack([pad_rows(rows2(p['k_cmp']), s + LANE), pad_rows(rows2(p['v_cmp']), s + LANE)])
    z = (raw[:, None] + pe).astype(MXU_DTYPE)
    zt = jnp.stack([_chunk_flat(z[0, 0]), _chunk_flat(z[1, 0])])
    zb = jnp.stack([_chunk_flat(z[0, 1]), _chunk_flat(z[1, 1])])
    kvc = compress(zt, zb, cmp_w1, cmp_b1, cmp_w2, s)
    kvc = kvc.reshape(2, b, KV_GROUPS_A, kvc.shape[2], HEAD_DIM_A)
    qc, qr, gates, qb, qi, wi = _query_operands(p, s)
    o_nsa = nsa_attention(qc, qr, gates, kvc, cast(p['k_slc']), cast(p['v_slc']), cast(p['k_win']),
                          cast(p['v_win']), tq=min(TQ, s), qpos0=0, wbase=0, n_keys=s)
    o_dsa = dsa_attention(qb, qi, wi, cast(p['k_b']), cast(p['v_b']), cast(p['k_idx']),
                          tq=min(TQ, s), qpos0=0, n_keep=min(DSA_TOPK, s // 4))
    y = merge_branches(o_nsa, o_dsa, p['g_a'], p['g_b'], w_oa, w_ob, w_o)
    nsa_rows = jnp.stack([p['k_cmp'], p['v_cmp'], p['k_slc'], p['v_slc']], axis=2)
    win_state = jnp.stack([p['k_win'], p['v_win']], axis=2)[:, -min(WINDOW, s):]
    dsa_rows = jnp.concatenate([p['k_b'], p['v_b'], p['k_idx']], axis=-1)
    return y, nsa_rows, win_state, dsa_rows


def mixer_sample(xn, cache_nsa, win_buf, cache_dsa, page_table, w_in, cmp_pe, cmp_w1, cmp_b1, cmp_w2,
                 w_oa, w_ob, w_o, sizes):
    b, t, _ = xn.shape
    page = cache_nsa.shape[1]
    past_len = page_table.shape[1] * page
    n_keys = past_len + t
    tail_rows = PAGES_PER_STEP * page
    pos = past_len + jnp.arange(t, dtype=jnp.int32)
    p = project(xn, w_in, pos, sizes)
    nsa_rows = jnp.stack([p['k_cmp'], p['v_cmp'], p['k_slc'], p['v_slc']], axis=2)
    dsa_rows = jnp.concatenate([p['k_b'], p['v_b'], p['k_idx']], axis=-1)
    qc, qr, gates, qb, qi, wi = _query_operands(p, TQ_STEP)

    zt, zb, ks, vs = nsa_gather(cache_nsa.reshape(cache_nsa.shape[0], page * NSA_SECTIONS, KV_A), page_table,
                                pad_rows(nsa_rows.reshape(b, t * NSA_SECTIONS, KV_A), tail_rows * NSA_SECTIONS),
                                cmp_pe)
    merge_bg = lambda a: a.reshape((2, b * KV_GROUPS_A) + a.shape[3:])
    kvc = compress(merge_bg(zt), merge_bg(zb), cmp_w1, cmp_b1, cmp_w2, n_keys)
    kvc = kvc.reshape(2, b, KV_GROUPS_A, kvc.shape[2], HEAD_DIM_A)
    w_len = win_buf.shape[1]
    win_all = jnp.concatenate([win_buf, jnp.stack([p['k_win'], p['v_win']], axis=2)], axis=1)
    win_pad = pad_rows(win_all, w_len + WIN_CHUNK).astype(MXU_DTYPE)
    kw = win_pad[:, :, 0].reshape(b, w_len + WIN_CHUNK, KV_A)
    vw = win_pad[:, :, 1].reshape(b, w_len + WIN_CHUNK, KV_A)
    o_nsa = nsa_attention(qc, qr, gates, kvc, ks, vs, kw, vw,
                          tq=TQ_STEP, qpos0=past_len, wbase=past_len - w_len, n_keys=n_keys)

    kb_, vb, ki = dsa_gather(cache_dsa, page_table, pad_rows(dsa_rows, tail_rows))
    o_dsa = dsa_attention(qb, qi, wi, kb_, vb, ki, tq=TQ_STEP, qpos0=past_len, n_keep=min(DSA_TOPK, n_keys // 4))
    y = merge_branches(o_nsa[:, :t], o_dsa[:, :t], p['g_a'], p['g_b'], w_oa, w_ob, w_o)
    return y, nsa_rows, win_all[:, -w_len:], dsa_rows


def mem_kv(mem, g, w_kv):
    b, m, _ = mem.shape
    return matmul(rmsnorm(mem, g), w_kv).reshape(b, m, 2, MEM_HEADS, MEM_HEAD_DIM)


def mem_attend(hn, kv, w_q, w_o):
    b, t, _ = hn.shape
    q = matmul(hn, w_q).reshape(b, t, MEM_HEADS, MEM_HEAD_DIM)
    s = jnp.einsum('bthd,bmhd->bthm', q, kv[:, :, 0]).astype(jnp.float32) * MEM_HEAD_DIM ** -0.5
    p = jax.nn.softmax(s, axis=-1).astype(kv.dtype)
    o = jnp.einsum('bthm,bmhd->bthd', p, kv[:, :, 1])
    return matmul(o.reshape(b, t, -1), w_o)


def conv_ffn(hn, prev, w_up, conv_w, conv_b, w_down):
    t = hn.shape[1]
    ext = jnp.concatenate([prev, matmul(hn, w_up)], axis=1)
    c = conv_b + sum(ext[:, j:j + t] * conv_w[j] for j in range(CONV_WIDTH))
    gate, up = jnp.split(c, 2, axis=-1)
    return matmul(jax.nn.silu(gate) * up, w_down), ext[:, t:]


def kernel(x_prompt, x_sample, mem_prompt, cache_nsa_kv, state_nsa_win, cache_dsa_kv, cache_mem_kv, state_conv,
           page_table, norm_g, w_in, cmp_pe, cmp_w1, cmp_b1, cmp_w2, w_out_a, w_out_b, w_out, w_mem_q, w_mem_kv,
           w_mem_out, w_up, conv_w, conv_b, w_down, final_g):
    depth = w_in.shape[0]
    d_model = x_prompt.shape[-1]
    d_ff = w_down.shape[1]
    sizes = (N_HEADS_A * HEAD_DIM_A, 6 * KV_A, 3 * N_HEADS_A, N_HEADS_B * HEAD_DIM_B, 2 * HEAD_DIM_B,
             IDX_HEADS * IDX_DIM, IDX_DIM, IDX_HEADS, 2 * d_model)
    xp, xs = x_prompt, x_sample
    nsa_p, nsa_s, win_p, win_s, dsa_p, dsa_s, mem_p, conv_p, conv_s = [], [], [], [], [], [], [], [], []
    for l in range(depth):
        yp, a, bwin, c = mixer_prompt(rmsnorm(xp, norm_g[l, 0]), w_in[l], cmp_pe[l], cmp_w1[l], cmp_b1[l], cmp_w2[l],
                                      w_out_a[l], w_out_b[l], w_out[l], sizes)
        nsa_p.append(a); win_p.append(bwin); dsa_p.append(c)
        ys, a, bwin, c = mixer_sample(rmsnorm(xs, norm_g[l, 0]), cache_nsa_kv[l], state_nsa_win[l], cache_dsa_kv[l],
                                      page_table, w_in[l], cmp_pe[l], cmp_w1[l], cmp_b1[l], cmp_w2[l],
                                      w_out_a[l], w_out_b[l], w_out[l], sizes)
        nsa_s.append(a); win_s.append(bwin); dsa_s.append(c)
        xp = xp + yp
        xs = xs + ys
        kv_p = mem_kv(mem_prompt, norm_g[l, 2], w_mem_kv[l])
        mem_p.append(kv_p)
        xp = xp + mem_attend(rmsnorm(xp, norm_g[l, 1]), kv_p, w_mem_q[l], w_mem_out[l])
        xs = xs + mem_attend(rmsnorm(xs, norm_g[l, 1]), cache_mem_kv[l], w_mem_q[l], w_mem_out[l])
        fp, cp = conv_ffn(rmsnorm(xp, norm_g[l, 3]), jnp.zeros((xp.shape[0], CONV_WIDTH - 1, 2 * d_ff), xp.dtype),
                          w_up[l], conv_w[l], conv_b[l], w_down[l])
        fs, cs = conv_ffn(rmsnorm(xs, norm_g[l, 3]), state_conv[l], w_up[l], conv_w[l], conv_b[l], w_down[l])
        conv_p.append(cp); conv_s.append(cs)
        xp = xp + fp
        xs = xs + fs
    y_prompt = rmsnorm(xp, final_g)
    y_sample = rmsnorm(xs, final_g)
    return (y_prompt, y_sample, jnp.stack(nsa_p), jnp.stack(nsa_s), jnp.stack(win_p), jnp.stack(win_s),
            jnp.stack(dsa_p), jnp.stack(dsa_s), jnp.stack(mem_p), jnp.stack(conv_p), jnp.stack(conv_s))
```

```python
import functools

import jax
import jax.numpy as jnp
from jax import lax
from jax.experimental import pallas as pl
from jax.experimental.pallas import tpu as pltpu

N_HEADS_A = 16
HEAD_DIM_A = 64
KV_GROUPS_A = 2
CMP_BLOCK = 32
CMP_STRIDE = 16
SLC_BLOCK = 64
N_SELECT = 16
WINDOW = 512
N_HEADS_B = 8
HEAD_DIM_B = 128
IDX_HEADS = 16
IDX_DIM = 64
DSA_TOPK = 256
MEM_HEADS = 4
MEM_HEAD_DIM = 128
CONV_WIDTH = 3
ROPE_THETA = 500000.0
ROT_FRACTION = 4
EPS = 1e-6
KV_A = KV_GROUPS_A * HEAD_DIM_A
CMP_PER_SLC = SLC_BLOCK // CMP_STRIDE
HPG_A = N_HEADS_A // KV_GROUPS_A
DSA_ROW = 2 * HEAD_DIM_B + IDX_DIM
NSA_SECTIONS = 4

LANE = 128
SUBLANE = 8
BF16_ROWS = 16
VMEM_LIMIT = 48 * 1024 * 1024

NEG_INF = float('-inf')
POS_INF = float('inf')
MXU_DTYPE = jnp.bfloat16
TQ = 128
TQ_STEP = BF16_ROWS
KEY_CHUNK = 512
WIN_CHUNK = 128
PAGES_PER_STEP = 4
BISECT_ITERS = 40
SLC_SHIFT = SLC_BLOCK.bit_length() - 1
CMP_PER_SLC_SHIFT = CMP_PER_SLC.bit_length() - 1


def _round_up(n, m):
    return -(-n // m) * m


def _tile(n, cap):
    if n <= cap:
        return n
    best = None
    for t in range(LANE, cap + 1, LANE):
        if n % t == 0:
            best = t
    assert best is not None, (n, cap)
    return best


def _mm_kernel(x_ref, w_ref, o_ref, acc_ref):
    @pl.when(pl.program_id(2) == 0)
    def _():
        acc_ref[...] = jnp.zeros_like(acc_ref)

    acc_ref[...] += jnp.dot(x_ref[...], w_ref[...], preferred_element_type=jnp.float32)

    @pl.when(pl.program_id(2) == pl.num_programs(2) - 1)
    def _():
        o_ref[...] = acc_ref[...].astype(o_ref.dtype)


def matmul(x, w, out_dtype=jnp.float32):
    lead = x.shape[:-1]
    k = x.shape[-1]
    n = w.shape[-1]
    x2 = x.reshape(-1, k).astype(MXU_DTYPE)
    w2 = w.astype(MXU_DTYPE)
    m0 = x2.shape[0]
    tm = 512 if m0 >= 512 else _round_up(m0, BF16_ROWS)
    m = _round_up(m0, tm)
    if m != m0:
        x2 = jnp.pad(x2, ((0, m - m0), (0, 0)))
    tn = _tile(n, 1024)
    tk = _tile(k, 2048)
    out = pl.pallas_call(
        _mm_kernel,
        grid=(m // tm, n // tn, k // tk),
        in_specs=[pl.BlockSpec((tm, tk), lambda i, j, l: (i, l)),
                  pl.BlockSpec((tk, tn), lambda i, j, l: (l, j))],
        out_specs=pl.BlockSpec((tm, tn), lambda i, j, l: (i, j)),
        out_shape=jax.ShapeDtypeStruct((m, n), out_dtype),
        scratch_shapes=[pltpu.VMEM((tm, tn), jnp.float32)],
        compiler_params=pltpu.CompilerParams(
            dimension_semantics=("parallel", "parallel", "arbitrary"),
            vmem_limit_bytes=VMEM_LIMIT),
        name="matmul",
    )(x2, w2)
    return out[:m0].reshape(lead + (n,))


def _dot_nt(a, b):
    return lax.dot_general(a, b, (((1,), (1,)), ((), ())), preferred_element_type=jnp.float32)


def _iota(shape, dim):
    return lax.broadcasted_iota(jnp.int32, shape, dim)


def _flash_init(rows, d):
    return (jnp.full((rows, 1), NEG_INF, jnp.float32), jnp.zeros((rows, 1), jnp.float32),
            jnp.zeros((rows, d), jnp.float32))


def _flash_step(carry, q, k, v, madd, nh, scale=None, kv_t=False):
    m, l, acc = carry
    s = jnp.dot(q, k, preferred_element_type=jnp.float32) if kv_t else _dot_nt(q, k)
    if scale is not None:
        s = s * scale
    r, kb = s.shape
    s = (s.reshape(nh, r // nh, kb) + madd[None]).reshape(r, kb)
    m_new = jnp.maximum(m, jnp.max(s, axis=1, keepdims=True))
    m_safe = jnp.where(m_new == NEG_INF, 0.0, m_new)
    p = jnp.exp(s - m_safe)
    alpha = jnp.exp(m - m_safe)
    l = alpha * l + jnp.sum(p, axis=1, keepdims=True)
    pv = _dot_nt(p.astype(v.dtype), v) if kv_t else jnp.dot(p.astype(v.dtype), v, preferred_element_type=jnp.float32)
    return m_new, l, alpha * acc + pv


def _flash_finish(carry):
    _, l, acc = carry
    return acc / jnp.maximum(l, 1e-30)


def _split_dot(x, m01):
    hi = x.astype(jnp.bfloat16)
    r1 = x - hi.astype(jnp.float32)
    mid = r1.astype(jnp.bfloat16)
    lo = (r1 - mid.astype(jnp.float32)).astype(jnp.bfloat16)
    dot = functools.partial(jnp.dot, preferred_element_type=jnp.float32)
    return dot(hi, m01) + dot(mid, m01) + dot(lo, m01)


def _nsa_group(g, i, qc_ref, qr_ref, gate_ref, kc_ref, vc_ref, ks_ref, vs_ref, kw_ref, vw_ref, mask_ref,
               *, tq, kb, qpos0, wbase, ns, n_sel, kv_t):
    nh, d = HPG_A, HEAD_DIM_A
    rows = nh * tq
    nc = kc_ref.shape[3]
    nsp = ns if ns <= LANE else _round_up(ns, LANE)
    t0 = qpos0 + i * tq
    nj = (t0 + tq - 1) // kb + 1
    cols = slice(g * d, (g + 1) * d)

    def stack_heads(q_ref):
        return jnp.concatenate([q_ref[0, :, (g * nh + h) * d:(g * nh + h + 1) * d] for h in range(nh)], axis=0)

    qc = stack_heads(qc_ref)
    qr = stack_heads(qr_ref)
    tpos = t0 + _iota((tq, 1), 0)

    blk_last = _iota((1, nc), 1) * CMP_STRIDE + (CMP_BLOCK - 1)
    madd_c = jnp.where(blk_last <= tpos, 0.0, NEG_INF)
    s = _dot_nt(qc, kc_ref[0, 0, g]).reshape(nh, tq, nc) + madd_c[None]
    m = jnp.max(s, axis=2, keepdims=True)
    m = jnp.where(m == NEG_INF, 0.0, m)
    e = jnp.exp(s - m)
    p = e / jnp.maximum(jnp.sum(e, axis=2, keepdims=True), 1e-30)
    o_cmp = jnp.dot(p.reshape(rows, nc).astype(vc_ref.dtype), vc_ref[0, 0, g], preferred_element_type=jnp.float32)

    imp = jnp.sum(p, axis=0)
    c_id = _iota((nc, nsp), 0)
    m_id = _iota((nc, nsp), 1)
    overlap = (jnp.right_shift(c_id, CMP_PER_SLC_SHIFT) == m_id) | (c_id == m_id * CMP_PER_SLC - 1)
    score = _split_dot(imp, overlap.astype(jnp.bfloat16))
    blk = _iota((1, nsp), 1)
    cur = jnp.right_shift(tpos, SLC_SHIFT)
    forced = (blk == 0) | (blk == cur) | (blk == cur - 1)
    sc = jnp.where(forced, POS_INF, jnp.where(blk * SLC_BLOCK <= tpos, score, NEG_INF))
    rank = jnp.zeros((tq, nsp), jnp.float32)
    for mp in range(ns):
        col = sc[:, mp:mp + 1]
        beats = (col > sc) | ((col == sc) & (blk > mp))
        rank = rank + jnp.where(beats, 1.0, 0.0)
    sel = jnp.where(rank < n_sel, 1.0, 0.0).astype(jnp.bfloat16)

    def make_mask(j, _):
        kpos = j * kb + _iota((1, kb), 1)
        expand = (jnp.right_shift(j * kb + _iota((nsp, kb), 1), SLC_SHIFT) == _iota((nsp, kb), 0))
        hit = jnp.dot(sel, expand.astype(jnp.bfloat16), preferred_element_type=jnp.float32)
        mask_ref[j] = jnp.where((hit > 0.5) & (kpos <= tpos), 0.0, NEG_INF)
        return 0

    lax.fori_loop(0, nj, make_mask, 0)

    def slc_body(j, carry):
        off = pl.multiple_of(j * kb, kb)
        return _flash_step(carry, qr, ks_ref[0, pl.ds(off, kb), cols], vs_ref[0, pl.ds(off, kb), cols],
                           mask_ref[j], nh)

    if kv_t:
        o_slc = _flash_finish(_flash_step(_flash_init(rows, d), qr, ks_ref[0, cols, :], vs_ref[0, cols, :],
                                          mask_ref[0], nh, kv_t=True))
    else:
        o_slc = _flash_finish(lax.fori_loop(0, nj, slc_body, _flash_init(rows, d)))

    wk = min(_round_up(WINDOW + tq, WIN_CHUNK), kw_ref.shape[1])
    first = jnp.maximum(t0 - (WINDOW - 1) - wbase, 0) // WIN_CHUNK * WIN_CHUNK
    off = pl.multiple_of(jnp.minimum(first, kw_ref.shape[1] - wk), WIN_CHUNK)
    dist = tpos - (wbase + off + _iota((1, wk), 1))
    madd_w = jnp.where((dist >= 0) & (dist < WINDOW), 0.0, NEG_INF)
    o_win = _flash_finish(_flash_step(_flash_init(rows, d), qr, kw_ref[0, pl.ds(off, wk), cols],
                                      vw_ref[0, pl.ds(off, wk), cols], madd_w, nh))

    gates = gate_ref[0]
    outs = []
    for h in range(nh):
        rs = slice(h * tq, (h + 1) * tq)
        c = g * nh + h
        outs.append(gates[:, c:c + 1] * o_cmp[rs] + gates[:, N_HEADS_A + c:N_HEADS_A + c + 1] * o_slc[rs]
                    + gates[:, 2 * N_HEADS_A + c:2 * N_HEADS_A + c + 1] * o_win[rs])
    return outs


def _nsa_kernel(qc_ref, qr_ref, gate_ref, kc_ref, vc_ref, ks_ref, vs_ref, kw_ref, vw_ref, o_ref, mask_ref, **kw):
    i = pl.program_id(1)
    outs = []
    for g in range(KV_GROUPS_A):
        outs += _nsa_group(g, i, qc_ref, qr_ref, gate_ref, kc_ref, vc_ref, ks_ref, vs_ref, kw_ref, vw_ref,
                           mask_ref, **kw)
    o_ref[0] = jnp.concatenate(outs, axis=1).astype(o_ref.dtype)


def nsa_attention(qc, qr, gates, kvc, ks, vs, kw, vw, *, tq, qpos0, wbase, n_keys, kv_t=False):
    b, t, hd = qc.shape
    l = ks.shape[2] if kv_t else ks.shape[1]
    lw = kw.shape[1]
    nc, d = kvc.shape[3:]
    kb = min(KEY_CHUNK, l) if tq >= TQ else l
    assert l % kb == 0 and t % tq == 0 and lw % WIN_CHUNK == 0 and (kb == l or not kv_t)
    assert (qpos0 + t - 1) // kb + 1 <= l // kb and (qpos0 + t - 1 - wbase) // WIN_CHUNK + 1 <= lw // WIN_CHUNK
    assert (qpos0 - wbase) % WIN_CHUNK == 0 and WIN_CHUNK % tq == 0 and tq > 1
    ns = l // SLC_BLOCK
    n_sel = min(N_SELECT, -(-n_keys // SLC_BLOCK))
    q_spec = pl.BlockSpec((1, tq, hd), lambda bi, i: (bi, i, 0))
    kc_spec = pl.BlockSpec((1, 1, KV_GROUPS_A, nc, d), lambda bi, i: (0, bi, 0, 0, 0))
    vc_spec = pl.BlockSpec((1, 1, KV_GROUPS_A, nc, d), lambda bi, i: (1, bi, 0, 0, 0))
    k_spec = pl.BlockSpec((1, KV_A, l) if kv_t else (1, l, KV_A), lambda bi, i: (bi, 0, 0))
    w_spec = pl.BlockSpec((1, lw, KV_A), lambda bi, i: (bi, 0, 0))
    return pl.pallas_call(
        functools.partial(_nsa_kernel, tq=tq, kb=kb, qpos0=qpos0, wbase=wbase, ns=ns, n_sel=n_sel, kv_t=kv_t),
        grid=(b, t // tq),
        in_specs=[q_spec, q_spec, pl.BlockSpec((1, tq, 3 * N_HEADS_A), lambda bi, i: (bi, i, 0)),
                  kc_spec, vc_spec, k_spec, k_spec, w_spec, w_spec],
        out_specs=q_spec,
        out_shape=jax.ShapeDtypeStruct((b, t, hd), MXU_DTYPE),
        scratch_shapes=[pltpu.VMEM((l // kb, tq, kb), jnp.float32)],
        compiler_params=pltpu.CompilerParams(
            dimension_semantics=("parallel", "arbitrary"), vmem_limit_bytes=VMEM_LIMIT),
        name="nsa_attention",
    )(qc, qr, gates, kvc, kvc, ks, vs, kw, vw)


def _dsa_kernel(qb_ref, qi_ref, wi_ref, kb_ref, vb_ref, ki_ref, o_ref, score_ref, *, tq, kb, qpos0, n_keep, kv_t):
    i = pl.program_id(1)
    nh, d = N_HEADS_B, HEAD_DIM_B
    t0 = qpos0 + i * tq
    nj = (t0 + tq - 1) // kb + 1
    tpos = t0 + _iota((tq, 1), 0)
    w = wi_ref[0]

    def idx_body(j, carry):
        lo, hi = carry
        off = pl.multiple_of(j * kb, kb)
        kidx = ki_ref[0] if kv_t else ki_ref[0, pl.ds(off, kb), :]
        acc = jnp.zeros((tq, kb), jnp.float32)
        for h in range(IDX_HEADS):
            qh = qi_ref[0, :, h * IDX_DIM:(h + 1) * IDX_DIM]
            dots = jnp.dot(qh, kidx, preferred_element_type=jnp.float32) if kv_t else _dot_nt(qh, kidx)
            acc = acc + w[:, h:h + 1] * jnp.maximum(dots, 0.0)
        vis = (off + _iota((1, kb), 1)) <= tpos
        score_ref[j] = jnp.where(vis, acc, NEG_INF)
        lo = jnp.minimum(lo, jnp.min(jnp.where(vis, acc, POS_INF), axis=1, keepdims=True))
        hi = jnp.maximum(hi, jnp.max(jnp.where(vis, acc, NEG_INF), axis=1, keepdims=True))
        return lo, hi

    lo, hi = lax.fori_loop(0, nj, idx_body, (jnp.full((tq, 1), POS_INF, jnp.float32),
                                             jnp.full((tq, 1), NEG_INF, jnp.float32)))

    k = float(n_keep)
    n_vis = (tpos + 1).astype(jnp.float32)

    def unsettled(cnt_lo):
        return jnp.max(jnp.where((cnt_lo != k) & (n_vis > k), 1.0, 0.0)) > 0.0

    def bisect(state):
        it, lo, hi, cnt_lo = state
        mid = 0.5 * (lo + hi)
        mid_b = jnp.broadcast_to(mid, (tq, LANE))

        def count(j, acc):
            sc = score_ref[j]
            for c in range(kb // LANE):
                acc = acc + jnp.where(sc[:, c * LANE:(c + 1) * LANE] >= mid_b, 1.0, 0.0)
            return acc

        cnt = jnp.sum(lax.fori_loop(0, nj, count, jnp.zeros((tq, LANE), jnp.float32)), axis=1, keepdims=True)
        ge = cnt >= k
        return it + 1, jnp.where(ge, mid, lo), jnp.where(ge, hi, mid), jnp.where(ge, cnt, cnt_lo)

    _, thr, _, _ = lax.while_loop(lambda st: (st[0] < BISECT_ITERS) & unsettled(st[3]), bisect,
                                  (jnp.int32(0), lo, hi, n_vis))

    q = jnp.concatenate([qb_ref[0, :, h * d:(h + 1) * d] for h in range(nh)], axis=0)

    def att_body(j, carry):
        off = pl.multiple_of(j * kb, kb)
        madd = jnp.where(score_ref[j] >= thr, 0.0, NEG_INF)
        if kv_t:
            return _flash_step(carry, q, kb_ref[0], vb_ref[0], madd, nh, scale=d ** -0.5, kv_t=True)
        return _flash_step(carry, q, kb_ref[0, pl.ds(off, kb), :], vb_ref[0, pl.ds(off, kb), :], madd, nh,
                           scale=d ** -0.5)

    o = _flash_finish(lax.fori_loop(0, nj, att_body, _flash_init(nh * tq, d)))
    o_ref[0] = jnp.concatenate([o[h * tq:(h + 1) * tq] for h in range(nh)], axis=1).astype(o_ref.dtype)


def dsa_attention(qb, qi, wi, kb_, vb, ki, *, tq, qpos0, n_keep, kv_t=False):
    b, t, hd = qb.shape
    l = kb_.shape[2] if kv_t else kb_.shape[1]
    kb = min(KEY_CHUNK, l) if tq >= TQ else l
    assert l % kb == 0 and t % tq == 0 and (qpos0 + t - 1) // kb + 1 <= l // kb and (kb == l or not kv_t)
    kv_spec = lambda width: pl.BlockSpec((1, width, l) if kv_t else (1, l, width), lambda bi, i: (bi, 0, 0))
    return pl.pallas_call(
        functools.partial(_dsa_kernel, tq=tq, kb=kb, qpos0=qpos0, n_keep=n_keep, kv_t=kv_t),
        grid=(b, t // tq),
        in_specs=[pl.BlockSpec((1, tq, hd), lambda bi, i: (bi, i, 0)),
                  pl.BlockSpec((1, tq, IDX_HEADS * IDX_DIM), lambda bi, i: (bi, i, 0)),
                  pl.BlockSpec((1, tq, IDX_HEADS), lambda bi, i: (bi, i, 0)),
                  kv_spec(HEAD_DIM_B), kv_spec(HEAD_DIM_B), kv_spec(IDX_DIM)],
        out_specs=pl.BlockSpec((1, tq, hd), lambda bi, i: (bi, i, 0)),
        out_shape=jax.ShapeDtypeStruct((b, t, hd), MXU_DTYPE),
        scratch_shapes=[pltpu.VMEM((l // kb, tq, kb), jnp.float32)],
        compiler_params=pltpu.CompilerParams(
            dimension_semantics=("parallel", "arbitrary"), vmem_limit_bytes=VMEM_LIMIT),
        name="dsa_attention",
    )(qb, qi, wi, kb_, vb, ki)


def _page_maps(n_pages, pp):
    n_steps = n_pages // pp

    def page_map(r):
        return lambda b, s, pt: (pt[b * n_pages + jnp.minimum(s, n_steps - 1) * pp + r], 0, 0)

    return n_steps, page_map


def _nsa_gather_kernel(pt_ref, *refs, pp, n_steps, rows):
    del pt_ref
    pages, (tail_tok_ref, tail_t_ref, pe_ref) = refs[:pp], refs[pp:pp + 3]
    zt_ref, zb_ref, ks_ref, vs_ref, tok_ref = refs[pp + 3:]
    is_tail = pl.program_id(1) == n_steps
    cpp = rows // CMP_STRIDE
    d = HEAD_DIM_A
    first_half = _iota((cpp, KV_A), 1) < d
    z = [[[] for _ in range(KV_GROUPS_A)] for _ in range(2)]
    for r in range(pp):
        cs = slice(r * rows, (r + 1) * rows)
        ks_ref[0, :, cs] = jnp.where(is_tail, tail_t_ref[0, :KV_A, cs],
                                     pages[r][0, 2 * KV_A:3 * KV_A, :]).astype(ks_ref.dtype)
        vs_ref[0, :, cs] = jnp.where(is_tail, tail_t_ref[0, KV_A:, cs],
                                     pages[r][0, 3 * KV_A:, :]).astype(vs_ref.dtype)
        pieces = [[[] for _ in range(KV_GROUPS_A)] for _ in range(2)]
        for sec in range(2):
            tok_ref[sec, cs, :] = jnp.where(is_tail, tail_tok_ref[0, sec, cs, :],
                                            pages[r][0, sec * KV_A:(sec + 1) * KV_A, :].T)
            for j in range(0, CMP_STRIDE, 2):
                a, b = [tok_ref[sec, pl.ds(r * rows + jj, cpp, stride=CMP_STRIDE), :] for jj in (j, j + 1)]
                pieces[sec][0].append(jnp.where(first_half, a, pltpu.roll(b, d, 1)))
                pieces[sec][1].append(jnp.where(first_half, pltpu.roll(a, d, 1), b))
        for sec in range(2):
            for g in range(KV_GROUPS_A):
                z[sec][g].append(jnp.concatenate(pieces[sec][g], axis=1))
    for sec in range(2):
        for g in range(KV_GROUPS_A):
            zf = jnp.concatenate(z[sec][g], axis=0)
            zt_ref[sec, 0, g] = (zf + pe_ref[sec, 0]).astype(zt_ref.dtype)
            zb_ref[sec, 0, g] = (zf + pe_ref[sec, 1]).astype(zb_ref.dtype)


def nsa_gather(cache_t, page_table, new_rows, cmp_pe):
    db, n_pages = page_table.shape
    width, rows = cache_t.shape[1:]
    pp = PAGES_PER_STEP
    n_steps, page_map = _page_maps(n_pages, pp)
    l = (n_steps + 1) * pp * rows
    cps = pp * rows // CMP_STRIDE
    flat = CMP_STRIDE * HEAD_DIM_A
    pe = cmp_pe.reshape(2, 2, 1, flat).astype(jnp.float32)
    tail = pad_rows(new_rows, pp * rows)
    tail_tok = tail[:, :, :2 * KV_A].reshape(db, pp * rows, 2, KV_A).swapaxes(1, 2)
    tail_t = tail[:, :, 2 * KV_A:].swapaxes(1, 2)
    z_spec = pl.BlockSpec((2, 1, KV_GROUPS_A, cps, flat), lambda b, s, pt: (0, b, 0, s, 0))
    r_spec = pl.BlockSpec((1, KV_A, pp * rows), lambda b, s, pt: (b, 0, s))
    z_shape = jax.ShapeDtypeStruct((2, db, KV_GROUPS_A, l // CMP_STRIDE, flat), MXU_DTYPE)
    r_shape = jax.ShapeDtypeStruct((db, KV_A, l), MXU_DTYPE)
    return pl.pallas_call(
        functools.partial(_nsa_gather_kernel, pp=pp, n_steps=n_steps, rows=rows),
        grid_spec=pltpu.PrefetchScalarGridSpec(
            num_scalar_prefetch=1,
            grid=(db, n_steps + 1),
            in_specs=[pl.BlockSpec((1, width, rows), page_map(r)) for r in range(pp)]
            + [pl.BlockSpec((1, 2, pp * rows, KV_A), lambda b, s, pt: (b, 0, 0, 0)),
               pl.BlockSpec((1, 2 * KV_A, pp * rows), lambda b, s, pt: (b, 0, 0)),
               pl.BlockSpec((2, 2, 1, flat), lambda b, s, pt: (0, 0, 0, 0))],
            out_specs=[z_spec, z_spec, r_spec, r_spec],
            scratch_shapes=[pltpu.VMEM((2, pp * rows, KV_A), jnp.float32)]),
        out_shape=[z_shape, z_shape, r_shape, r_shape],
        compiler_params=pltpu.CompilerParams(
            dimension_semantics=("parallel", "arbitrary"), vmem_limit_bytes=VMEM_LIMIT),
        name="nsa_gather",
    )(page_table.reshape(-1), *([cache_t] * pp), tail_tok, tail_t, pe)


def _dsa_gather_kernel(pt_ref, *refs, pp, n_steps, rows):
    del pt_ref
    pages, tail_ref, (k_ref, v_ref, i_ref) = refs[:pp], refs[pp], refs[pp + 1:]
    is_tail = pl.program_id(1) == n_steps
    d = HEAD_DIM_B
    for r in range(pp):
        cs = slice(r * rows, (r + 1) * rows)
        x = jnp.where(is_tail, tail_ref[0, :, cs], pages[r][0])
        k_ref[0, :, cs] = x[:d].astype(k_ref.dtype)
        v_ref[0, :, cs] = x[d:2 * d].astype(v_ref.dtype)
        i_ref[0, :, cs] = x[2 * d:].astype(i_ref.dtype)


def dsa_gather(cache_t, page_table, new_rows):
    db, n_pages = page_table.shape
    width, rows = cache_t.shape[1:]
    pp = PAGES_PER_STEP
    n_steps, page_map = _page_maps(n_pages, pp)
    l = (n_steps + 1) * pp * rows
    tail_t = pad_rows(new_rows, pp * rows).swapaxes(1, 2)
    out_spec = lambda w: pl.BlockSpec((1, w, pp * rows), lambda b, s, pt: (b, 0, s))
    widths = (HEAD_DIM_B, HEAD_DIM_B, IDX_DIM)
    return pl.pallas_call(
        functools.partial(_dsa_gather_kernel, pp=pp, n_steps=n_steps, rows=rows),
        grid_spec=pltpu.PrefetchScalarGridSpec(
            num_scalar_prefetch=1,
            grid=(db, n_steps + 1),
            in_specs=[pl.BlockSpec((1, width, rows), page_map(r)) for r in range(pp)]
            + [pl.BlockSpec((1, width, pp * rows), lambda b, s, pt: (b, 0, 0))],
            out_specs=[out_spec(w) for w in widths]),
        out_shape=[jax.ShapeDtypeStruct((db, w, l), MXU_DTYPE) for w in widths],
        compiler_params=pltpu.CompilerParams(
            dimension_semantics=("parallel", "arbitrary"), vmem_limit_bytes=VMEM_LIMIT),
        name="dsa_gather",
    )(page_table.reshape(-1), *([cache_t] * pp), tail_t)


def _compress_kernel(zt_ref, zb_ref, w1t_ref, w1b_ref, b1_ref, w2_ref, o_ref, ab_ref, *, ncp):
    dot = functools.partial(jnp.dot, preferred_element_type=jnp.float32)
    at = dot(zt_ref[0, 0], w1t_ref[0])
    ab_ref[...] = dot(zb_ref[0, 0], w1b_ref[0])
    h = jax.nn.gelu(at[:ncp] + ab_ref[pl.ds(1, ncp), :] + b1_ref[0])
    o_ref[0, 0, :ncp] = dot(h.astype(w2_ref.dtype), w2_ref[0]).astype(o_ref.dtype)
    if o_ref.shape[2] > ncp:
        o_ref[0, 0, ncp:] = jnp.zeros((o_ref.shape[2] - ncp, o_ref.shape[3]), o_ref.dtype)


def compress(zt, zb, w1, b1, w2, n_keys):
    ncp = _round_up(-(-n_keys // CMP_STRIDE), BF16_ROWS)
    ncl = ncp if ncp <= LANE else _round_up(ncp, LANE)
    _, nb, ch, kdim = zt.shape
    hid = w1.shape[-1]
    d = w2.shape[-1]
    assert ch >= ncp + 1
    w1 = w1.astype(MXU_DTYPE)
    z_spec = pl.BlockSpec((1, 1, ch, kdim), lambda s, n: (s, n, 0, 0))
    return pl.pallas_call(
        functools.partial(_compress_kernel, ncp=ncp),
        grid=(2, nb),
        in_specs=[z_spec, z_spec,
                  pl.BlockSpec((1, kdim, hid), lambda s, n: (s, 0, 0)),
                  pl.BlockSpec((1, kdim, hid), lambda s, n: (s, 1, 0)),
                  pl.BlockSpec((1, 1, hid), lambda s, n: (s, 0, 0)),
                  pl.BlockSpec((1, hid, d), lambda s, n: (s, 0, 0))],
        out_specs=pl.BlockSpec((1, 1, ncl, d), lambda s, n: (s, n, 0, 0)),
        out_shape=jax.ShapeDtypeStruct((2, nb, ncl, d), MXU_DTYPE),
        scratch_shapes=[pltpu.VMEM((ch, hid), jnp.float32)],
        compiler_params=pltpu.CompilerParams(
            dimension_semantics=("parallel", "parallel"), vmem_limit_bytes=VMEM_LIMIT),
        name="compress",
    )(zt, zb, w1, w1, b1.reshape(2, 1, hid).astype(jnp.float32), w2.astype(MXU_DTYPE))


def _chunk_flat(z):
    b, l, _ = z.shape
    z = z.reshape(b, l // CMP_STRIDE, CMP_STRIDE, KV_GROUPS_A, HEAD_DIM_A)
    return z.transpose(0, 3, 1, 2, 4).reshape(b * KV_GROUPS_A, l // CMP_STRIDE, CMP_STRIDE * HEAD_DIM_A)


def rmsnorm(x, g):
    xf = x.astype(jnp.float32)
    y = xf * lax.rsqrt(jnp.mean(xf * xf, axis=-1, keepdims=True) + EPS)
    return (y * g.astype(jnp.float32)).astype(x.dtype)


def rope_partial(x, pos):
    d = x.shape[-1]
    rot = d // ROT_FRACTION
    half = rot // 2
    inv = ROPE_THETA ** (-jnp.arange(half, dtype=jnp.float32) / half)
    ang = pos.astype(jnp.float32)[:, None] * inv[None, :]
    ang = ang.reshape((ang.shape[0],) + (1,) * (x.ndim - 3) + (half,))
    cos = jnp.cos(ang).astype(x.dtype)
    sin = jnp.sin(ang).astype(x.dtype)
    x1 = x[..., :half]
    x2 = x[..., half:rot]
    return jnp.concatenate([x1 * cos - x2 * sin, x2 * cos + x1 * sin, x[..., rot:]], axis=-1)


def pad_rows(a, n):
    return jnp.pad(a, [(0, 0), (0, n - a.shape[1])] + [(0, 0)] * (a.ndim - 2))


def project(xn, w_in, pos, sizes):
    b, t, d_model = xn.shape
    offsets = tuple(sum(sizes[:i + 1]) for i in range(len(sizes) - 1))
    qa, kva, ga, qb, kvb, qi, ki, wi, gm = jnp.split(matmul(xn, w_in), offsets, axis=-1)
    qa = qa.reshape(b, t, N_HEADS_A, HEAD_DIM_A)
    kva = kva.reshape(b, t, 6, KV_GROUPS_A, HEAD_DIM_A)
    kvb = kvb.reshape(b, t, 2, HEAD_DIM_B)
    gm = jax.nn.sigmoid(gm.reshape(b, t, 2, d_model))
    return {
        'q_cmp': qa, 'q_rot': rope_partial(qa, pos),
        'k_cmp': kva[:, :, 0], 'v_cmp': kva[:, :, 1],
        'k_slc': rope_partial(kva[:, :, 2], pos), 'v_slc': kva[:, :, 3],
        'k_win': rope_partial(kva[:, :, 4], pos), 'v_win': kva[:, :, 5],
        'g_nsa': jax.nn.sigmoid(ga.reshape(b, t, 3, N_HEADS_A)),
        'q_b': rope_partial(qb.reshape(b, t, N_HEADS_B, HEAD_DIM_B), pos),
        'k_b': rope_partial(kvb[:, :, 0], pos), 'v_b': kvb[:, :, 1],
        'q_idx': rope_partial(qi.reshape(b, t, IDX_HEADS, IDX_DIM), pos),
        'k_idx': rope_partial(ki, pos), 'w_idx': wi,
        'g_a': gm[:, :, 0], 'g_b': gm[:, :, 1],
    }


def _query_operands(p, t_pad):
    b, t = p['q_cmp'].shape[:2]

    def rows(a, scale=1.0):
        return pad_rows((a * scale).reshape(b, t, -1), t_pad)

    qc = rows(p['q_cmp'], HEAD_DIM_A ** -0.5).astype(MXU_DTYPE)
    qr = rows(p['q_rot'], HEAD_DIM_A ** -0.5).astype(MXU_DTYPE)
    gates = rows(p['g_nsa'])
    qb = rows(p['q_b']).astype(MXU_DTYPE)
    qi = rows(p['q_idx'], IDX_DIM ** -0.5).astype(MXU_DTYPE)
    wi = rows(p['w_idx'].astype(jnp.float32), IDX_HEADS ** -0.5)
    return qc, qr, gates, qb, qi, wi


def merge_branches(o_nsa, o_dsa, g_a, g_b, w_oa, w_ob, w_o):
    ya = matmul(o_nsa, w_oa)
    yb = matmul(o_dsa, w_ob)
    return matmul(g_a * ya + g_b * yb, w_o)


def mixer_prompt(xn, w_in, cmp_pe, cmp_w1, cmp_b1, cmp_w2, w_oa, w_ob, w_o, sizes):
    b, s, _ = xn.shape
    pos = jnp.arange(s, dtype=jnp.int32)
    p = project(xn, w_in, pos, sizes)
    rows2 = lambda a: a.reshape(b, s, -1)
    cast = lambda a: rows2(a).astype(MXU_DTYPE)
    pe = jnp.tile(cmp_pe.reshape(2, 2, CMP_STRIDE, 1, HEAD_DIM_A), (1, 1, (s + LANE) // CMP_STRIDE, KV_GROUPS_A, 1))
    pe = pe.reshape(2, 2, 1, s + LANE, KV_A)
    raw = jnp.stack([pad_rows(rows2(p['k_cmp']), s + LANE), pad_rows(rows2(p['v_cmp']), s + LANE)])
    z = (raw[:, None] + pe).astype(MXU_DTYPE)
    zt = jnp.stack([_chunk_flat(z[0, 0]), _chunk_flat(z[1, 0])])
    zb = jnp.stack([_chunk_flat(z[0, 1]), _chunk_flat(z[1, 1])])
    kvc = compress(zt, zb, cmp_w1, cmp_b1, cmp_w2, s)
    kvc = kvc.reshape(2, b, KV_GROUPS_A, kvc.shape[2], HEAD_DIM_A)
    qc, qr, gates, qb, qi, wi = _query_operands(p, s)
    o_nsa = nsa_attention(qc, qr, gates, kvc, cast(p['k_slc']), cast(p['v_slc']), cast(p['k_win']),
                          cast(p['v_win']), tq=min(TQ, s), qpos0=0, wbase=0, n_keys=s)
    o_dsa = dsa_attention(qb, qi, wi, cast(p['k_b']), cast(p['v_b']), cast(p['k_idx']),
                          tq=min(TQ, s), qpos0=0, n_keep=min(DSA_TOPK, s // 4))
    y = merge_branches(o_nsa, o_dsa, p['g_a'], p['g_b'], w_oa, w_ob, w_o)
    nsa_rows = jnp.stack([p['k_cmp'], p['v_cmp'], p['k_slc'], p['v_slc']], axis=2)
    win_state = jnp.stack([p['k_win'], p['v_win']], axis=2)[:, -min(WINDOW, s):]
    dsa_rows = jnp.concatenate([p['k_b'], p['v_b'], p['k_idx']], axis=-1)
    return y, nsa_rows, win_state, dsa_rows


def mixer_sample(xn, cache_nsa, win_buf, cache_dsa, page_table, w_in, cmp_pe, cmp_w1, cmp_b1, cmp_w2,
                 w_oa, w_ob, w_o, sizes):
    b, t, _ = xn.shape
    page = cache_nsa.shape[1]
    past_len = page_table.shape[1] * page
    n_keys = past_len + t
    pos = past_len + jnp.arange(t, dtype=jnp.int32)
    p = project(xn, w_in, pos, sizes)
    nsa_rows = jnp.stack([p['k_cmp'], p['v_cmp'], p['k_slc'], p['v_slc']], axis=2)
    dsa_rows = jnp.concatenate([p['k_b'], p['v_b'], p['k_idx']], axis=-1)
    qc, qr, gates, qb, qi, wi = _query_operands(p, TQ_STEP)

    cache_nsa_t = jnp.transpose(cache_nsa, (0, 2, 3, 4, 1)).reshape(cache_nsa.shape[0], NSA_SECTIONS * KV_A, page)
    zt, zb, ks, vs = nsa_gather(cache_nsa_t, page_table, nsa_rows.reshape(b, t, NSA_SECTIONS * KV_A), cmp_pe)
    merge_bg = lambda a: a.reshape((2, b * KV_GROUPS_A) + a.shape[3:])
    kvc = compress(merge_bg(zt), merge_bg(zb), cmp_w1, cmp_b1, cmp_w2, n_keys)
    kvc = kvc.reshape(2, b, KV_GROUPS_A, kvc.shape[2], HEAD_DIM_A)
    w_len = win_buf.shape[1]
    win_all = jnp.concatenate([win_buf, jnp.stack([p['k_win'], p['v_win']], axis=2)], axis=1)
    win_pad = pad_rows(win_all, w_len + WIN_CHUNK).astype(MXU_DTYPE)
    kw = win_pad[:, :, 0].reshape(b, w_len + WIN_CHUNK, KV_A)
    vw = win_pad[:, :, 1].reshape(b, w_len + WIN_CHUNK, KV_A)
    o_nsa = nsa_attention(qc, qr, gates, kvc, ks, vs, kw, vw,
                          tq=TQ_STEP, qpos0=past_len, wbase=past_len - w_len, n_keys=n_keys, kv_t=True)

    kb_, vb, ki = dsa_gather(jnp.swapaxes(cache_dsa, 1, 2), page_table, dsa_rows)
    o_dsa = dsa_attention(qb, qi, wi, kb_, vb, ki, tq=TQ_STEP, qpos0=past_len, n_keep=min(DSA_TOPK, n_keys // 4),
                          kv_t=True)
    y = merge_branches(o_nsa[:, :t], o_dsa[:, :t], p['g_a'], p['g_b'], w_oa, w_ob, w_o)
    return y, nsa_rows, win_all[:, -w_len:], dsa_rows


def mem_kv(mem, g, w_kv):
    b, m, _ = mem.shape
    return matmul(rmsnorm(mem, g), w_kv).reshape(b, m, 2, MEM_HEADS, MEM_HEAD_DIM)


def mem_attend(hn, kv, w_q, w_o):
    b, t, _ = hn.shape
    q = matmul(hn, w_q).reshape(b, t, MEM_HEADS, MEM_HEAD_DIM)
    s = jnp.einsum('bthd,bmhd->bthm', q, kv[:, :, 0]).astype(jnp.float32) * MEM_HEAD_DIM ** -0.5
    p = jax.nn.softmax(s, axis=-1).astype(kv.dtype)
    o = jnp.einsum('bthm,bmhd->bthd', p, kv[:, :, 1])
    return matmul(o.reshape(b, t, -1), w_o)


def conv_ffn(hn, prev, w_up, conv_w, conv_b, w_down):
    t = hn.shape[1]
    ext = jnp.concatenate([prev, matmul(hn, w_up)], axis=1)
    c = conv_b + sum(ext[:, j:j + t] * conv_w[j] for j in range(CONV_WIDTH))
    gate, up = jnp.split(c, 2, axis=-1)
    return matmul(jax.nn.silu(gate) * up, w_down), ext[:, t:]


def kernel(x_prompt, x_sample, mem_prompt, cache_nsa_kv, state_nsa_win, cache_dsa_kv, cache_mem_kv, state_conv,
           page_table, norm_g, w_in, cmp_pe, cmp_w1, cmp_b1, cmp_w2, w_out_a, w_out_b, w_out, w_mem_q, w_mem_kv,
           w_mem_out, w_up, conv_w, conv_b, w_down, final_g):
    depth = w_in.shape[0]
    d_model = x_prompt.shape[-1]
    d_ff = w_down.shape[1]
    sizes = (N_HEADS_A * HEAD_DIM_A, 6 * KV_A, 3 * N_HEADS_A, N_HEADS_B * HEAD_DIM_B, 2 * HEAD_DIM_B,
             IDX_HEADS * IDX_DIM, IDX_DIM, IDX_HEADS, 2 * d_model)
    xp, xs = x_prompt, x_sample
    nsa_p, nsa_s, win_p, win_s, dsa_p, dsa_s, mem_p, conv_p, conv_s = [], [], [], [], [], [], [], [], []
    for l in range(depth):
        yp, a, bwin, c = mixer_prompt(rmsnorm(xp, norm_g[l, 0]), w_in[l], cmp_pe[l], cmp_w1[l], cmp_b1[l], cmp_w2[l],
                                      w_out_a[l], w_out_b[l], w_out[l], sizes)
        nsa_p.append(a); win_p.append(bwin); dsa_p.append(c)
        ys, a, bwin, c = mixer_sample(rmsnorm(xs, norm_g[l, 0]), cache_nsa_kv[l], state_nsa_win[l], cache_dsa_kv[l],
                                      page_table, w_in[l], cmp_pe[l], cmp_w1[l], cmp_b1[l], cmp_w2[l],
                                      w_out_a[l], w_out_b[l], w_out[l], sizes)
        nsa_s.append(a); win_s.append(bwin); dsa_s.append(c)
        xp = xp + yp
        xs = xs + ys
        kv_p = mem_kv(mem_prompt, norm_g[l, 2], w_mem_kv[l])
        mem_p.append(kv_p)
        xp = xp + mem_attend(rmsnorm(xp, norm_g[l, 1]), kv_p, w_mem_q[l], w_mem_out[l])
        xs = xs + mem_attend(rmsnorm(xs, norm_g[l, 1]), cache_mem_kv[l], w_mem_q[l], w_mem_out[l])
        fp, cp = conv_ffn(rmsnorm(xp, norm_g[l, 3]), jnp.zeros((xp.shape[0], CONV_WIDTH - 1, 2 * d_ff), xp.dtype),
                          w_up[l], conv_w[l], conv_b[l], w_down[l])
        fs, cs = conv_ffn(rmsnorm(xs, norm_g[l, 3]), state_conv[l], w_up[l], conv_w[l], conv_b[l], w_down[l])
        conv_p.append(cp); conv_s.append(cs)
        xp = xp + fp
        xs = xs + fs
    y_prompt = rmsnorm(xp, final_g)
    y_sample = rmsnorm(xs, final_g)
    return (y_prompt, y_sample, jnp.stack(nsa_p), jnp.stack(nsa_s), jnp.stack(win_p), jnp.stack(win_s),
            jnp.stack(dsa_p), jnp.stack(dsa_s), jnp.stack(mem_p), jnp.stack(conv_p), jnp.stack(conv_s))
```

```python
import functools

import jax
import jax.numpy as jnp
from jax import lax
from jax.experimental import pallas as pl
from jax.experimental.pallas import tpu as pltpu

N_HEADS_A = 16
HEAD_DIM_A = 64
KV_GROUPS_A = 2
CMP_BLOCK = 32
CMP_STRIDE = 16
SLC_BLOCK = 64
N_SELECT = 16
WINDOW = 512
N_HEADS_B = 8
HEAD_DIM_B = 128
IDX_HEADS = 16
IDX_DIM = 64
DSA_TOPK = 256
MEM_HEADS = 4
MEM_HEAD_DIM = 128
CONV_WIDTH = 3
ROPE_THETA = 500000.0
ROT_FRACTION = 4
EPS = 1e-6
KV_A = KV_GROUPS_A * HEAD_DIM_A
CMP_PER_SLC = SLC_BLOCK // CMP_STRIDE
HPG_A = N_HEADS_A // KV_GROUPS_A
DSA_ROW = 2 * HEAD_DIM_B + IDX_DIM
NSA_SECTIONS = 4

LANE = 128
SUBLANE = 8
BF16_ROWS = 16
VMEM_LIMIT = 48 * 1024 * 1024

NEG_INF = float('-inf')
POS_INF = float('inf')
MXU_DTYPE = jnp.bfloat16
TQ = 128
TQ_STEP = BF16_ROWS
KEY_CHUNK = 512
WIN_CHUNK = 128
PAGES_PER_STEP = 4
BISECT_ITERS = 40
SLC_SHIFT = SLC_BLOCK.bit_length() - 1
CMP_PER_SLC_SHIFT = CMP_PER_SLC.bit_length() - 1


def _round_up(n, m):
    return -(-n // m) * m


def _tile(n, cap):
    if n <= cap:
        return n
    best = None
    for t in range(LANE, cap + 1, LANE):
        if n % t == 0:
            best = t
    assert best is not None, (n, cap)
    return best


def _mm_kernel(x_ref, w_ref, o_ref, acc_ref):
    @pl.when(pl.program_id(2) == 0)
    def _():
        acc_ref[...] = jnp.zeros_like(acc_ref)

    acc_ref[...] += jnp.dot(x_ref[...], w_ref[...], preferred_element_type=jnp.float32)

    @pl.when(pl.program_id(2) == pl.num_programs(2) - 1)
    def _():
        o_ref[...] = acc_ref[...].astype(o_ref.dtype)


def matmul(x, w, out_dtype=jnp.float32):
    lead = x.shape[:-1]
    k = x.shape[-1]
    n = w.shape[-1]
    x2 = x.reshape(-1, k).astype(MXU_DTYPE)
    w2 = w.astype(MXU_DTYPE)
    m0 = x2.shape[0]
    tm = 512 if m0 >= 512 else _round_up(m0, BF16_ROWS)
    m = _round_up(m0, tm)
    if m != m0:
        x2 = jnp.pad(x2, ((0, m - m0), (0, 0)))
    tn = _tile(n, 1024)
    tk = _tile(k, 2048)
    out = pl.pallas_call(
        _mm_kernel,
        grid=(m // tm, n // tn, k // tk),
        in_specs=[pl.BlockSpec((tm, tk), lambda i, j, l: (i, l)),
                  pl.BlockSpec((tk, tn), lambda i, j, l: (l, j))],
        out_specs=pl.BlockSpec((tm, tn), lambda i, j, l: (i, j)),
        out_shape=jax.ShapeDtypeStruct((m, n), out_dtype),
        scratch_shapes=[pltpu.VMEM((tm, tn), jnp.float32)],
        compiler_params=pltpu.CompilerParams(
            dimension_semantics=("parallel", "parallel", "arbitrary"),
            vmem_limit_bytes=VMEM_LIMIT),
        name="matmul",
    )(x2, w2)
    return out[:m0].reshape(lead + (n,))


def _rms(x, g):
    return x * lax.rsqrt(jnp.mean(x * x, axis=-1, keepdims=True) + EPS) * g


def _norm_mm_kernel(x_ref, g_ref, w_ref, o_ref, xn_ref):
    @pl.when(pl.program_id(1) == 0)
    def _():
        xn_ref[...] = _rms(x_ref[...], g_ref[...]).astype(xn_ref.dtype)

    o_ref[...] = jnp.dot(xn_ref[...], w_ref[...], preferred_element_type=jnp.float32).astype(o_ref.dtype)


def norm_matmul(x, g, w):
    lead = x.shape[:-1]
    k = x.shape[-1]
    n = w.shape[-1]
    x2 = x.reshape(-1, k)
    m = x2.shape[0]
    tm = 512 if m % 512 == 0 else m
    tn = _tile(n, 1024)
    out = pl.pallas_call(
        _norm_mm_kernel,
        grid=(m // tm, n // tn),
        in_specs=[pl.BlockSpec((tm, k), lambda i, j: (i, 0)),
                  pl.BlockSpec((1, k), lambda i, j: (0, 0)),
                  pl.BlockSpec((k, tn), lambda i, j: (0, j))],
        out_specs=pl.BlockSpec((tm, tn), lambda i, j: (i, j)),
        out_shape=jax.ShapeDtypeStruct((m, n), jnp.float32),
        scratch_shapes=[pltpu.VMEM((tm, k), MXU_DTYPE)],
        compiler_params=pltpu.CompilerParams(
            dimension_semantics=("parallel", "arbitrary"), vmem_limit_bytes=VMEM_LIMIT),
        name="norm_matmul",
    )(x2, g.reshape(1, k).astype(jnp.float32), w.astype(MXU_DTYPE))
    return out.reshape(lead + (n,))


def _gated_merge_kernel(oa_ref, ob_ref, wa_ref, wb_ref, ga_ref, gb_ref, o_ref):
    dot = functools.partial(jnp.dot, preferred_element_type=jnp.float32)
    ya = dot(oa_ref[...], wa_ref[...])
    yb = dot(ob_ref[...], wb_ref[...])
    o_ref[...] = (jax.nn.sigmoid(ga_ref[...]) * ya + jax.nn.sigmoid(gb_ref[...]) * yb).astype(o_ref.dtype)


def gated_merge(o_nsa, o_dsa, gm, w_oa, w_ob):
    m, ka = o_nsa.shape
    kb = o_dsa.shape[1]
    n = w_oa.shape[1]
    tm = 512 if m % 512 == 0 else m
    tn = _tile(n, 1024)
    nb = n // tn
    return pl.pallas_call(
        _gated_merge_kernel,
        grid=(m // tm, nb),
        in_specs=[pl.BlockSpec((tm, ka), lambda i, j: (i, 0)),
                  pl.BlockSpec((tm, kb), lambda i, j: (i, 0)),
                  pl.BlockSpec((ka, tn), lambda i, j: (0, j)),
                  pl.BlockSpec((kb, tn), lambda i, j: (0, j)),
                  pl.BlockSpec((tm, tn), lambda i, j: (i, j)),
                  pl.BlockSpec((tm, tn), lambda i, j: (i, j + nb))],
        out_specs=pl.BlockSpec((tm, tn), lambda i, j: (i, j)),
        out_shape=jax.ShapeDtypeStruct((m, n), MXU_DTYPE),
        compiler_params=pltpu.CompilerParams(
            dimension_semantics=("parallel", "parallel"), vmem_limit_bytes=VMEM_LIMIT),
        name="gated_merge",
    )(o_nsa, o_dsa, w_oa.astype(MXU_DTYPE), w_ob.astype(MXU_DTYPE), gm, gm)


def _out_proj_kernel(z_ref, w_ref, x_ref, o_ref):
    o_ref[...] = x_ref[...] + jnp.dot(z_ref[...], w_ref[...], preferred_element_type=jnp.float32)


def out_proj_residual(z, w, x):
    m, k = z.shape
    n = w.shape[1]
    tm = 256 if m % 256 == 0 else m
    return pl.pallas_call(
        _out_proj_kernel,
        grid=(m // tm,),
        in_specs=[pl.BlockSpec((tm, k), lambda i: (i, 0)),
                  pl.BlockSpec((k, n), lambda i: (0, 0)),
                  pl.BlockSpec((tm, n), lambda i: (i, 0))],
        out_specs=pl.BlockSpec((tm, n), lambda i: (i, 0)),
        out_shape=jax.ShapeDtypeStruct((m, n), jnp.float32),
        compiler_params=pltpu.CompilerParams(dimension_semantics=("parallel",), vmem_limit_bytes=VMEM_LIMIT),
        name="out_proj_residual",
    )(z, w.astype(MXU_DTYPE), x)


def _mem_block_kernel(x_ref, g1_ref, g2_ref, wq_ref, kv_ref, wo_ref, x_out_ref, xn_out_ref):
    d = MEM_HEAD_DIM
    hd = MEM_HEADS * d
    x = x_ref[0]
    xn = _rms(x, g1_ref[...]).astype(wq_ref.dtype)
    q = jnp.dot(xn, wq_ref[...], preferred_element_type=jnp.float32).astype(wq_ref.dtype)
    outs = []
    for h in range(MEM_HEADS):
        k = kv_ref[0, :, h * d:(h + 1) * d].astype(wq_ref.dtype)
        v = kv_ref[0, :, hd + h * d:hd + (h + 1) * d].astype(wq_ref.dtype)
        s = _dot_nt(q[:, h * d:(h + 1) * d], k) * d ** -0.5
        e = jnp.exp(s - jnp.max(s, axis=1, keepdims=True))
        p = e / jnp.sum(e, axis=1, keepdims=True)
        outs.append(jnp.dot(p.astype(v.dtype), v, preferred_element_type=jnp.float32))
    o = jnp.concatenate(outs, axis=1).astype(wo_ref.dtype)
    x2 = x + jnp.dot(o, wo_ref[...], preferred_element_type=jnp.float32)
    x_out_ref[0] = x2
    xn_out_ref[0] = _rms(x2, g2_ref[...]).astype(xn_out_ref.dtype)


def mem_block(x, g1, g2, w_q, kv, w_o):
    b, t, dm = x.shape
    mt, kvw = kv.shape[1:]
    hd = w_q.shape[1]
    tm = 256 if t % 256 == 0 else t
    row_spec = pl.BlockSpec((1, tm, dm), lambda bi, i: (bi, i, 0))
    g_spec = pl.BlockSpec((1, dm), lambda bi, i: (0, 0))
    return pl.pallas_call(
        _mem_block_kernel,
        grid=(b, t // tm),
        in_specs=[row_spec, g_spec, g_spec,
                  pl.BlockSpec((dm, hd), lambda bi, i: (0, 0)),
                  pl.BlockSpec((1, mt, kvw), lambda bi, i: (bi, 0, 0)),
                  pl.BlockSpec((hd, dm), lambda bi, i: (0, 0))],
        out_specs=[row_spec, row_spec],
        out_shape=[jax.ShapeDtypeStruct((b, t, dm), jnp.float32), jax.ShapeDtypeStruct((b, t, dm), MXU_DTYPE)],
        compiler_params=pltpu.CompilerParams(
            dimension_semantics=("parallel", "parallel"), vmem_limit_bytes=VMEM_LIMIT),
        name="mem_block",
    )(x, g1.reshape(1, dm).astype(jnp.float32), g2.reshape(1, dm).astype(jnp.float32), w_q.astype(MXU_DTYPE),
      kv, w_o.astype(MXU_DTYPE))


def _ffn_up_kernel(x_ref, halo_ref, wg_ref, wu_ref, cwg_ref, cwu_ref, cbg_ref, cbu_ref, pg_ref, pu_ref,
                   h_ref, sg_ref, su_ref, *, tm, t_real):
    i = pl.program_id(2)
    dot = functools.partial(jnp.dot, preferred_element_type=jnp.float32)
    row = _iota((tm, 1), 0)
    last = (t_real - 1) // tm
    r_last = (t_real - 1) % tm

    def branch(w_ref, cw_ref, cb_ref, p_ref, s_ref):
        u = dot(x_ref[0], w_ref[...])
        uh = dot(halo_ref[0], w_ref[...])
        n_halo = uh.shape[0]
        p0 = jnp.where(i == 0, p_ref[0, 0:1, :], uh[n_halo - 2:n_halo - 1])
        p1 = jnp.where(i == 0, p_ref[0, 1:2, :], uh[n_halo - 1:n_halo])
        u1 = jnp.where(row == 0, p1, pltpu.roll(u, 1, 0))
        u2 = jnp.where(row == 0, p0, jnp.where(row == 1, p1, pltpu.roll(u, 2, 0)))

        @pl.when(i == last)
        def _():
            s_ref[0] = u[r_last - 1:r_last + 1]

        return cb_ref[...] + u2 * cw_ref[0:1, :] + u1 * cw_ref[1:2, :] + u * cw_ref[2:3, :]

    gate = branch(wg_ref, cwg_ref, cbg_ref, pg_ref, sg_ref)
    up = branch(wu_ref, cwu_ref, cbu_ref, pu_ref, su_ref)
    h_ref[0] = (jax.nn.silu(gate) * up).astype(h_ref.dtype)


def ffn_up(xn, prev, w_up, conv_w, conv_b, t_real):
    b, t, dm = xn.shape
    f2 = w_up.shape[1]
    f = f2 // 2
    tm = 512 if t % 512 == 0 else t
    tn = _tile(f, 512)
    nf = f // tn
    halo = min(BF16_ROWS, tm)
    hpt = tm // halo
    assert t_real >= 2 and (t_real - 1) % tm >= 1
    w_up = w_up.astype(MXU_DTYPE)
    conv_b = conv_b.reshape(1, f2)
    col = lambda off: (lambda bi, j, i: (0, j + off))
    st = lambda off: (lambda bi, j, i: (bi, 0, j + off))
    specs = [pl.BlockSpec((1, tm, dm), lambda bi, j, i: (bi, i, 0)),
             pl.BlockSpec((1, halo, dm), lambda bi, j, i: (bi, jnp.maximum(i * hpt - 1, 0), 0)),
             pl.BlockSpec((dm, tn), col(0)), pl.BlockSpec((dm, tn), col(nf)),
             pl.BlockSpec((CONV_WIDTH, tn), col(0)), pl.BlockSpec((CONV_WIDTH, tn), col(nf)),
             pl.BlockSpec((1, tn), col(0)), pl.BlockSpec((1, tn), col(nf)),
             pl.BlockSpec((1, 2, tn), st(0)), pl.BlockSpec((1, 2, tn), st(nf))]
    h, sg, su = pl.pallas_call(
        functools.partial(_ffn_up_kernel, tm=tm, t_real=t_real),
        grid=(b, nf, t // tm),
        in_specs=specs,
        out_specs=[pl.BlockSpec((1, tm, tn), lambda bi, j, i: (bi, i, j)),
                   pl.BlockSpec((1, 2, tn), st(0)), pl.BlockSpec((1, 2, tn), st(0))],
        out_shape=[jax.ShapeDtypeStruct((b, t, f), MXU_DTYPE), jax.ShapeDtypeStruct((b, 2, f), jnp.float32),
                   jax.ShapeDtypeStruct((b, 2, f), jnp.float32)],
        compiler_params=pltpu.CompilerParams(
            dimension_semantics=("parallel", "parallel", "arbitrary"), vmem_limit_bytes=VMEM_LIMIT),
        name="ffn_up",
    )(xn, xn, w_up, w_up, conv_w, conv_w, conv_b, conv_b, prev, prev)
    return h, jnp.concatenate([sg, su], axis=-1)


def _ffn_down_kernel(h_ref, w_ref, x_ref, g_ref, o_ref, acc_ref):
    @pl.when(pl.program_id(1) == 0)
    def _():
        acc_ref[...] = x_ref[...]

    acc_ref[...] += jnp.dot(h_ref[...], w_ref[...], preferred_element_type=jnp.float32)

    @pl.when(pl.program_id(1) == pl.num_programs(1) - 1)
    def _():
        o_ref[...] = _rms(acc_ref[...], g_ref[...])


def ffn_down_norm(h, w_down, x, g):
    m, f = h.shape
    dm = w_down.shape[1]
    tm = 512 if m % 512 == 0 else m
    tk = _tile(f, 2048)
    return pl.pallas_call(
        _ffn_down_kernel,
        grid=(m // tm, f // tk),
        in_specs=[pl.BlockSpec((tm, tk), lambda i, l: (i, l)),
                  pl.BlockSpec((tk, dm), lambda i, l: (l, 0)),
                  pl.BlockSpec((tm, dm), lambda i, l: (i, 0)),
                  pl.BlockSpec((1, dm), lambda i, l: (0, 0))],
        out_specs=pl.BlockSpec((tm, dm), lambda i, l: (i, 0)),
        out_shape=jax.ShapeDtypeStruct((m, dm), jnp.float32),
        scratch_shapes=[pltpu.VMEM((tm, dm), jnp.float32)],
        compiler_params=pltpu.CompilerParams(
            dimension_semantics=("parallel", "arbitrary"), vmem_limit_bytes=VMEM_LIMIT),
        name="ffn_down_norm",
    )(h, w_down.astype(MXU_DTYPE), x, g.reshape(1, dm).astype(jnp.float32))


def dense_tail(x, o_nsa, o_dsa, gm, kv_mem, prev_u, t_real, g_mem, g_ffn, g_final, w_oa, w_ob, w_o, w_mq, w_mo,
               w_up, conv_w, conv_b, w_down):
    b, t, dm = x.shape
    rows = lambda a: a.reshape(b * t, a.shape[-1])
    z = gated_merge(rows(o_nsa), rows(o_dsa), rows(gm), w_oa, w_ob)
    x1 = out_proj_residual(z, w_o, rows(x)).reshape(b, t, dm)
    x2, xn2 = mem_block(x1, g_mem, g_ffn, w_mq, kv_mem, w_mo)
    h, state = ffn_up(xn2, prev_u, w_up, conv_w, conv_b, t_real)
    y = ffn_down_norm(rows(h), w_down, rows(x2), g_final)
    return y.reshape(b, t, dm), state


def _dot_nt(a, b):
    return lax.dot_general(a, b, (((1,), (1,)), ((), ())), preferred_element_type=jnp.float32)


def _iota(shape, dim):
    return lax.broadcasted_iota(jnp.int32, shape, dim)


def _flash_init(rows, d):
    return (jnp.full((rows, 1), NEG_INF, jnp.float32), jnp.zeros((rows, 1), jnp.float32),
            jnp.zeros((rows, d), jnp.float32))


def _flash_step(carry, q, k, v, madd, nh, scale=None, kv_t=False):
    m, l, acc = carry
    s = jnp.dot(q, k, preferred_element_type=jnp.float32) if kv_t else _dot_nt(q, k)
    if scale is not None:
        s = s * scale
    r, kb = s.shape
    s = (s.reshape(nh, r // nh, kb) + madd[None]).reshape(r, kb)
    m_new = jnp.maximum(m, jnp.max(s, axis=1, keepdims=True))
    m_safe = jnp.where(m_new == NEG_INF, 0.0, m_new)
    p = jnp.exp(s - m_safe)
    alpha = jnp.exp(m - m_safe)
    l = alpha * l + jnp.sum(p, axis=1, keepdims=True)
    pv = _dot_nt(p.astype(v.dtype), v) if kv_t else jnp.dot(p.astype(v.dtype), v, preferred_element_type=jnp.float32)
    return m_new, l, alpha * acc + pv


def _flash_finish(carry):
    _, l, acc = carry
    return acc / jnp.maximum(l, 1e-30)


def _split_dot(x, m01):
    hi = x.astype(jnp.bfloat16)
    r1 = x - hi.astype(jnp.float32)
    mid = r1.astype(jnp.bfloat16)
    lo = (r1 - mid.astype(jnp.float32)).astype(jnp.bfloat16)
    dot = functools.partial(jnp.dot, preferred_element_type=jnp.float32)
    return dot(hi, m01) + dot(mid, m01) + dot(lo, m01)


def _nsa_group(g, i, qc_ref, qr_ref, gate_ref, kc_ref, vc_ref, ks_ref, vs_ref, kw_ref, vw_ref, mask_ref,
               *, tq, kb, qpos0, wbase, ns, n_sel, kv_t):
    nh, d = HPG_A, HEAD_DIM_A
    rows = nh * tq
    nc = kc_ref.shape[3]
    nsp = ns if ns <= LANE else _round_up(ns, LANE)
    t0 = qpos0 + i * tq
    nj = (t0 + tq - 1) // kb + 1
    cols = slice(g * d, (g + 1) * d)

    def stack_heads(q_ref):
        return jnp.concatenate([q_ref[0, :, (g * nh + h) * d:(g * nh + h + 1) * d] for h in range(nh)], axis=0)

    qc = stack_heads(qc_ref)
    qr = stack_heads(qr_ref)
    tpos = t0 + _iota((tq, 1), 0)

    blk_last = _iota((1, nc), 1) * CMP_STRIDE + (CMP_BLOCK - 1)
    madd_c = jnp.where(blk_last <= tpos, 0.0, NEG_INF)
    s = _dot_nt(qc, kc_ref[0, 0, g]).reshape(nh, tq, nc) + madd_c[None]
    m = jnp.max(s, axis=2, keepdims=True)
    m = jnp.where(m == NEG_INF, 0.0, m)
    e = jnp.exp(s - m)
    p = e / jnp.maximum(jnp.sum(e, axis=2, keepdims=True), 1e-30)
    o_cmp = jnp.dot(p.reshape(rows, nc).astype(vc_ref.dtype), vc_ref[0, 0, g], preferred_element_type=jnp.float32)

    imp = jnp.sum(p, axis=0)
    c_id = _iota((nc, nsp), 0)
    m_id = _iota((nc, nsp), 1)
    overlap = (jnp.right_shift(c_id, CMP_PER_SLC_SHIFT) == m_id) | (c_id == m_id * CMP_PER_SLC - 1)
    score = _split_dot(imp, overlap.astype(jnp.bfloat16))
    blk = _iota((1, nsp), 1)
    cur = jnp.right_shift(tpos, SLC_SHIFT)
    forced = (blk == 0) | (blk == cur) | (blk == cur - 1)
    sc = jnp.where(forced, POS_INF, jnp.where(blk * SLC_BLOCK <= tpos, score, NEG_INF))
    rank = jnp.zeros((tq, nsp), jnp.float32)
    for mp in range(ns):
        col = sc[:, mp:mp + 1]
        beats = (col > sc) | ((col == sc) & (blk > mp))
        rank = rank + jnp.where(beats, 1.0, 0.0)
    sel = jnp.where(rank < n_sel, 1.0, 0.0).astype(jnp.bfloat16)

    def make_mask(j, _):
        kpos = j * kb + _iota((1, kb), 1)
        expand = (jnp.right_shift(j * kb + _iota((nsp, kb), 1), SLC_SHIFT) == _iota((nsp, kb), 0))
        hit = jnp.dot(sel, expand.astype(jnp.bfloat16), preferred_element_type=jnp.float32)
        mask_ref[j] = jnp.where((hit > 0.5) & (kpos <= tpos), 0.0, NEG_INF)
        return 0

    lax.fori_loop(0, nj, make_mask, 0)

    def slc_body(j, carry):
        off = pl.multiple_of(j * kb, kb)
        return _flash_step(carry, qr, ks_ref[0, pl.ds(off, kb), cols], vs_ref[0, pl.ds(off, kb), cols],
                           mask_ref[j], nh)

    if kv_t:
        o_slc = _flash_finish(_flash_step(_flash_init(rows, d), qr, ks_ref[0, cols, :], vs_ref[0, cols, :],
                                          mask_ref[0], nh, kv_t=True))
    else:
        o_slc = _flash_finish(lax.fori_loop(0, nj, slc_body, _flash_init(rows, d)))

    wk = min(_round_up(WINDOW + tq, WIN_CHUNK), kw_ref.shape[1])
    first = jnp.maximum(t0 - (WINDOW - 1) - wbase, 0) // WIN_CHUNK * WIN_CHUNK
    off = pl.multiple_of(jnp.minimum(first, kw_ref.shape[1] - wk), WIN_CHUNK)
    dist = tpos - (wbase + off + _iota((1, wk), 1))
    madd_w = jnp.where((dist >= 0) & (dist < WINDOW), 0.0, NEG_INF)
    o_win = _flash_finish(_flash_step(_flash_init(rows, d), qr, kw_ref[0, pl.ds(off, wk), cols],
                                      vw_ref[0, pl.ds(off, wk), cols], madd_w, nh))

    gates = gate_ref[0]
    outs = []
    for h in range(nh):
        rs = slice(h * tq, (h + 1) * tq)
        c = g * nh + h
        outs.append(gates[:, c:c + 1] * o_cmp[rs] + gates[:, N_HEADS_A + c:N_HEADS_A + c + 1] * o_slc[rs]
                    + gates[:, 2 * N_HEADS_A + c:2 * N_HEADS_A + c + 1] * o_win[rs])
    return outs


def _nsa_kernel(qc_ref, qr_ref, gate_ref, kc_ref, vc_ref, ks_ref, vs_ref, kw_ref, vw_ref, o_ref, mask_ref, **kw):
    i = pl.program_id(1)
    outs = []
    for g in range(KV_GROUPS_A):
        outs += _nsa_group(g, i, qc_ref, qr_ref, gate_ref, kc_ref, vc_ref, ks_ref, vs_ref, kw_ref, vw_ref,
                           mask_ref, **kw)
    o_ref[0] = jnp.concatenate(outs, axis=1).astype(o_ref.dtype)


def nsa_attention(qc, qr, gates, kvc, ks, vs, kw, vw, *, tq, qpos0, wbase, n_keys, kv_t=False):
    b, t, hd = qc.shape
    l = ks.shape[2] if kv_t else ks.shape[1]
    lw = kw.shape[1]
    nc, d = kvc.shape[3:]
    kb = min(KEY_CHUNK, l) if tq >= TQ else l
    assert l % kb == 0 and t % tq == 0 and lw % WIN_CHUNK == 0 and (kb == l or not kv_t)
    assert (qpos0 + t - 1) // kb + 1 <= l // kb and (qpos0 + t - 1 - wbase) // WIN_CHUNK + 1 <= lw // WIN_CHUNK
    assert (qpos0 - wbase) % WIN_CHUNK == 0 and WIN_CHUNK % tq == 0 and tq > 1
    ns = l // SLC_BLOCK
    n_sel = min(N_SELECT, -(-n_keys // SLC_BLOCK))
    q_spec = pl.BlockSpec((1, tq, hd), lambda bi, i: (bi, i, 0))
    kc_spec = pl.BlockSpec((1, 1, KV_GROUPS_A, nc, d), lambda bi, i: (0, bi, 0, 0, 0))
    vc_spec = pl.BlockSpec((1, 1, KV_GROUPS_A, nc, d), lambda bi, i: (1, bi, 0, 0, 0))
    k_spec = pl.BlockSpec((1, KV_A, l) if kv_t else (1, l, KV_A), lambda bi, i: (bi, 0, 0))
    w_spec = pl.BlockSpec((1, lw, KV_A), lambda bi, i: (bi, 0, 0))
    return pl.pallas_call(
        functools.partial(_nsa_kernel, tq=tq, kb=kb, qpos0=qpos0, wbase=wbase, ns=ns, n_sel=n_sel, kv_t=kv_t),
        grid=(b, t // tq),
        in_specs=[q_spec, q_spec, pl.BlockSpec((1, tq, 3 * N_HEADS_A), lambda bi, i: (bi, i, 0)),
                  kc_spec, vc_spec, k_spec, k_spec, w_spec, w_spec],
        out_specs=q_spec,
        out_shape=jax.ShapeDtypeStruct((b, t, hd), MXU_DTYPE),
        scratch_shapes=[pltpu.VMEM((l // kb, tq, kb), jnp.float32)],
        compiler_params=pltpu.CompilerParams(
            dimension_semantics=("parallel", "arbitrary"), vmem_limit_bytes=VMEM_LIMIT),
        name="nsa_attention",
    )(qc, qr, gates, kvc, kvc, ks, vs, kw, vw)


def _dsa_kernel(qb_ref, qi_ref, wi_ref, kb_ref, vb_ref, ki_ref, o_ref, score_ref, *, tq, kb, qpos0, n_keep, kv_t):
    i = pl.program_id(1)
    nh, d = N_HEADS_B, HEAD_DIM_B
    t0 = qpos0 + i * tq
    nj = (t0 + tq - 1) // kb + 1
    tpos = t0 + _iota((tq, 1), 0)
    w = wi_ref[0]

    def idx_body(j, carry):
        lo, hi = carry
        off = pl.multiple_of(j * kb, kb)
        kidx = ki_ref[0] if kv_t else ki_ref[0, pl.ds(off, kb), :]
        acc = jnp.zeros((tq, kb), jnp.float32)
        for h in range(IDX_HEADS):
            qh = qi_ref[0, :, h * IDX_DIM:(h + 1) * IDX_DIM]
            dots = jnp.dot(qh, kidx, preferred_element_type=jnp.float32) if kv_t else _dot_nt(qh, kidx)
            acc = acc + w[:, h:h + 1] * jnp.maximum(dots, 0.0)
        vis = (off + _iota((1, kb), 1)) <= tpos
        score_ref[j] = jnp.where(vis, acc, NEG_INF)
        lo = jnp.minimum(lo, jnp.min(jnp.where(vis, acc, POS_INF), axis=1, keepdims=True))
        hi = jnp.maximum(hi, jnp.max(jnp.where(vis, acc, NEG_INF), axis=1, keepdims=True))
        return lo, hi

    lo, hi = lax.fori_loop(0, nj, idx_body, (jnp.full((tq, 1), POS_INF, jnp.float32),
                                             jnp.full((tq, 1), NEG_INF, jnp.float32)))

    k = float(n_keep)
    n_vis = (tpos + 1).astype(jnp.float32)

    def unsettled(cnt_lo):
        return jnp.max(jnp.where((cnt_lo != k) & (n_vis > k), 1.0, 0.0)) > 0.0

    def bisect(state):
        it, lo, hi, cnt_lo = state
        mid = 0.5 * (lo + hi)
        mid_b = jnp.broadcast_to(mid, (tq, LANE))

        def count(j, acc):
            sc = score_ref[j]
            for c in range(kb // LANE):
                acc = acc + jnp.where(sc[:, c * LANE:(c + 1) * LANE] >= mid_b, 1.0, 0.0)
            return acc

        cnt = jnp.sum(lax.fori_loop(0, nj, count, jnp.zeros((tq, LANE), jnp.float32)), axis=1, keepdims=True)
        ge = cnt >= k
        return it + 1, jnp.where(ge, mid, lo), jnp.where(ge, hi, mid), jnp.where(ge, cnt, cnt_lo)

    _, thr, _, _ = lax.while_loop(lambda st: (st[0] < BISECT_ITERS) & unsettled(st[3]), bisect,
                                  (jnp.int32(0), lo, hi, n_vis))

    q = jnp.concatenate([qb_ref[0, :, h * d:(h + 1) * d] for h in range(nh)], axis=0)

    def att_body(j, carry):
        off = pl.multiple_of(j * kb, kb)
        madd = jnp.where(score_ref[j] >= thr, 0.0, NEG_INF)
        if kv_t:
            return _flash_step(carry, q, kb_ref[0], vb_ref[0], madd, nh, scale=d ** -0.5, kv_t=True)
        return _flash_step(carry, q, kb_ref[0, pl.ds(off, kb), :], vb_ref[0, pl.ds(off, kb), :], madd, nh,
                           scale=d ** -0.5)

    o = _flash_finish(lax.fori_loop(0, nj, att_body, _flash_init(nh * tq, d)))
    o_ref[0] = jnp.concatenate([o[h * tq:(h + 1) * tq] for h in range(nh)], axis=1).astype(o_ref.dtype)


def dsa_attention(qb, qi, wi, kb_, vb, ki, *, tq, qpos0, n_keep, kv_t=False):
    b, t, hd = qb.shape
    l = kb_.shape[2] if kv_t else kb_.shape[1]
    kb = min(KEY_CHUNK, l) if tq >= TQ else l
    assert l % kb == 0 and t % tq == 0 and (qpos0 + t - 1) // kb + 1 <= l // kb and (kb == l or not kv_t)
    kv_spec = lambda width: pl.BlockSpec((1, width, l) if kv_t else (1, l, width), lambda bi, i: (bi, 0, 0))
    return pl.pallas_call(
        functools.partial(_dsa_kernel, tq=tq, kb=kb, qpos0=qpos0, n_keep=n_keep, kv_t=kv_t),
        grid=(b, t // tq),
        in_specs=[pl.BlockSpec((1, tq, hd), lambda bi, i: (bi, i, 0)),
                  pl.BlockSpec((1, tq, IDX_HEADS * IDX_DIM), lambda bi, i: (bi, i, 0)),
                  pl.BlockSpec((1, tq, IDX_HEADS), lambda bi, i: (bi, i, 0)),
                  kv_spec(HEAD_DIM_B), kv_spec(HEAD_DIM_B), kv_spec(IDX_DIM)],
        out_specs=pl.BlockSpec((1, tq, hd), lambda bi, i: (bi, i, 0)),
        out_shape=jax.ShapeDtypeStruct((b, t, hd), MXU_DTYPE),
        scratch_shapes=[pltpu.VMEM((l // kb, tq, kb), jnp.float32)],
        compiler_params=pltpu.CompilerParams(
            dimension_semantics=("parallel", "arbitrary"), vmem_limit_bytes=VMEM_LIMIT),
        name="dsa_attention",
    )(qb, qi, wi, kb_, vb, ki)


def _page_maps(n_pages, pp):
    n_steps = n_pages // pp

    def page_map(r):
        return lambda b, s, pt: (pt[b * n_pages + jnp.minimum(s, n_steps - 1) * pp + r], 0, 0)

    return n_steps, page_map


def _nsa_gather_kernel(pt_ref, *refs, pp, n_steps, rows):
    del pt_ref
    pages, (tail_tok_ref, tail_t_ref, pe_ref) = refs[:pp], refs[pp:pp + 3]
    zt_ref, zb_ref, ks_ref, vs_ref, tok_ref = refs[pp + 3:]
    is_tail = pl.program_id(1) == n_steps
    cpp = rows // CMP_STRIDE
    d = HEAD_DIM_A
    first_half = _iota((cpp, KV_A), 1) < d
    z = [[[] for _ in range(KV_GROUPS_A)] for _ in range(2)]
    for r in range(pp):
        cs = slice(r * rows, (r + 1) * rows)
        ks_ref[0, :, cs] = jnp.where(is_tail, tail_t_ref[0, :KV_A, cs],
                                     pages[r][0, 2 * KV_A:3 * KV_A, :]).astype(ks_ref.dtype)
        vs_ref[0, :, cs] = jnp.where(is_tail, tail_t_ref[0, KV_A:, cs],
                                     pages[r][0, 3 * KV_A:, :]).astype(vs_ref.dtype)
        pieces = [[[] for _ in range(KV_GROUPS_A)] for _ in range(2)]
        for sec in range(2):
            tok_ref[sec, cs, :] = jnp.where(is_tail, tail_tok_ref[0, sec, cs, :],
                                            pages[r][0, sec * KV_A:(sec + 1) * KV_A, :].T)
            for j in range(0, CMP_STRIDE, 2):
                a, b = [tok_ref[sec, pl.ds(r * rows + jj, cpp, stride=CMP_STRIDE), :] for jj in (j, j + 1)]
                pieces[sec][0].append(jnp.where(first_half, a, pltpu.roll(b, d, 1)))
                pieces[sec][1].append(jnp.where(first_half, pltpu.roll(a, d, 1), b))
        for sec in range(2):
            for g in range(KV_GROUPS_A):
                z[sec][g].append(jnp.concatenate(pieces[sec][g], axis=1))
    for sec in range(2):
        for g in range(KV_GROUPS_A):
            zf = jnp.concatenate(z[sec][g], axis=0)
            zt_ref[sec, 0, g] = (zf + pe_ref[sec, 0]).astype(zt_ref.dtype)
            zb_ref[sec, 0, g] = (zf + pe_ref[sec, 1]).astype(zb_ref.dtype)


def nsa_gather(cache_t, page_table, new_rows, cmp_pe):
    db, n_pages = page_table.shape
    width, rows = cache_t.shape[1:]
    pp = PAGES_PER_STEP
    n_steps, page_map = _page_maps(n_pages, pp)
    l = (n_steps + 1) * pp * rows
    cps = pp * rows // CMP_STRIDE
    flat = CMP_STRIDE * HEAD_DIM_A
    pe = cmp_pe.reshape(2, 2, 1, flat).astype(jnp.float32)
    tail = pad_rows(new_rows, pp * rows)
    tail_tok = tail[:, :, :2 * KV_A].reshape(db, pp * rows, 2, KV_A).swapaxes(1, 2)
    tail_t = tail[:, :, 2 * KV_A:].swapaxes(1, 2)
    z_spec = pl.BlockSpec((2, 1, KV_GROUPS_A, cps, flat), lambda b, s, pt: (0, b, 0, s, 0))
    r_spec = pl.BlockSpec((1, KV_A, pp * rows), lambda b, s, pt: (b, 0, s))
    z_shape = jax.ShapeDtypeStruct((2, db, KV_GROUPS_A, l // CMP_STRIDE, flat), MXU_DTYPE)
    r_shape = jax.ShapeDtypeStruct((db, KV_A, l), MXU_DTYPE)
    return pl.pallas_call(
        functools.partial(_nsa_gather_kernel, pp=pp, n_steps=n_steps, rows=rows),
        grid_spec=pltpu.PrefetchScalarGridSpec(
            num_scalar_prefetch=1,
            grid=(db, n_steps + 1),
            in_specs=[pl.BlockSpec((1, width, rows), page_map(r)) for r in range(pp)]
            + [pl.BlockSpec((1, 2, pp * rows, KV_A), lambda b, s, pt: (b, 0, 0, 0)),
               pl.BlockSpec((1, 2 * KV_A, pp * rows), lambda b, s, pt: (b, 0, 0)),
               pl.BlockSpec((2, 2, 1, flat), lambda b, s, pt: (0, 0, 0, 0))],
            out_specs=[z_spec, z_spec, r_spec, r_spec],
            scratch_shapes=[pltpu.VMEM((2, pp * rows, KV_A), jnp.float32)]),
        out_shape=[z_shape, z_shape, r_shape, r_shape],
        compiler_params=pltpu.CompilerParams(
            dimension_semantics=("parallel", "arbitrary"), vmem_limit_bytes=VMEM_LIMIT),
        name="nsa_gather",
    )(page_table.reshape(-1), *([cache_t] * pp), tail_tok, tail_t, pe)


def _dsa_gather_kernel(pt_ref, *refs, pp, n_steps, rows):
    del pt_ref
    pages, tail_ref, (k_ref, v_ref, i_ref) = refs[:pp], refs[pp], refs[pp + 1:]
    is_tail = pl.program_id(1) == n_steps
    d = HEAD_DIM_B
    for r in range(pp):
        cs = slice(r * rows, (r + 1) * rows)
        x = jnp.where(is_tail, tail_ref[0, :, cs], pages[r][0])
        k_ref[0, :, cs] = x[:d].astype(k_ref.dtype)
        v_ref[0, :, cs] = x[d:2 * d].astype(v_ref.dtype)
        i_ref[0, :, cs] = x[2 * d:].astype(i_ref.dtype)


def dsa_gather(cache_t, page_table, new_rows):
    db, n_pages = page_table.shape
    width, rows = cache_t.shape[1:]
    pp = PAGES_PER_STEP
    n_steps, page_map = _page_maps(n_pages, pp)
    l = (n_steps + 1) * pp * rows
    tail_t = pad_rows(new_rows, pp * rows).swapaxes(1, 2)
    out_spec = lambda w: pl.BlockSpec((1, w, pp * rows), lambda b, s, pt: (b, 0, s))
    widths = (HEAD_DIM_B, HEAD_DIM_B, IDX_DIM)
    return pl.pallas_call(
        functools.partial(_dsa_gather_kernel, pp=pp, n_steps=n_steps, rows=rows),
        grid_spec=pltpu.PrefetchScalarGridSpec(
            num_scalar_prefetch=1,
            grid=(db, n_steps + 1),
            in_specs=[pl.BlockSpec((1, width, rows), page_map(r)) for r in range(pp)]
            + [pl.BlockSpec((1, width, pp * rows), lambda b, s, pt: (b, 0, 0))],
            out_specs=[out_spec(w) for w in widths]),
        out_shape=[jax.ShapeDtypeStruct((db, w, l), MXU_DTYPE) for w in widths],
        compiler_params=pltpu.CompilerParams(
            dimension_semantics=("parallel", "arbitrary"), vmem_limit_bytes=VMEM_LIMIT),
        name="dsa_gather",
    )(page_table.reshape(-1), *([cache_t] * pp), tail_t)


def _compress_kernel(zt_ref, zb_ref, w1t_ref, w1b_ref, b1_ref, w2_ref, o_ref, ab_ref, *, ncp):
    dot = functools.partial(jnp.dot, preferred_element_type=jnp.float32)
    at = dot(zt_ref[0, 0], w1t_ref[0])
    ab_ref[...] = dot(zb_ref[0, 0], w1b_ref[0])
    h = jax.nn.gelu(at[:ncp] + ab_ref[pl.ds(1, ncp), :] + b1_ref[0])
    o_ref[0, 0, :ncp] = dot(h.astype(w2_ref.dtype), w2_ref[0]).astype(o_ref.dtype)
    if o_ref.shape[2] > ncp:
        o_ref[0, 0, ncp:] = jnp.zeros((o_ref.shape[2] - ncp, o_ref.shape[3]), o_ref.dtype)


def compress(zt, zb, w1, b1, w2, n_keys):
    ncp = _round_up(-(-n_keys // CMP_STRIDE), BF16_ROWS)
    ncl = ncp if ncp <= LANE else _round_up(ncp, LANE)
    _, nb, ch, kdim = zt.shape
    hid = w1.shape[-1]
    d = w2.shape[-1]
    assert ch >= ncp + 1
    w1 = w1.astype(MXU_DTYPE)
    z_spec = pl.BlockSpec((1, 1, ch, kdim), lambda s, n: (s, n, 0, 0))
    return pl.pallas_call(
        functools.partial(_compress_kernel, ncp=ncp),
        grid=(2, nb),
        in_specs=[z_spec, z_spec,
                  pl.BlockSpec((1, kdim, hid), lambda s, n: (s, 0, 0)),
                  pl.BlockSpec((1, kdim, hid), lambda s, n: (s, 1, 0)),
                  pl.BlockSpec((1, 1, hid), lambda s, n: (s, 0, 0)),
                  pl.BlockSpec((1, hid, d), lambda s, n: (s, 0, 0))],
        out_specs=pl.BlockSpec((1, 1, ncl, d), lambda s, n: (s, n, 0, 0)),
        out_shape=jax.ShapeDtypeStruct((2, nb, ncl, d), MXU_DTYPE),
        scratch_shapes=[pltpu.VMEM((ch, hid), jnp.float32)],
        compiler_params=pltpu.CompilerParams(
            dimension_semantics=("parallel", "parallel"), vmem_limit_bytes=VMEM_LIMIT),
        name="compress",
    )(zt, zb, w1, w1, b1.reshape(2, 1, hid).astype(jnp.float32), w2.astype(MXU_DTYPE))


def _chunk_flat(z):
    b, l, _ = z.shape
    z = z.reshape(b, l // CMP_STRIDE, CMP_STRIDE, KV_GROUPS_A, HEAD_DIM_A)
    return z.transpose(0, 3, 1, 2, 4).reshape(b * KV_GROUPS_A, l // CMP_STRIDE, CMP_STRIDE * HEAD_DIM_A)


def rmsnorm(x, g):
    xf = x.astype(jnp.float32)
    y = xf * lax.rsqrt(jnp.mean(xf * xf, axis=-1, keepdims=True) + EPS)
    return (y * g.astype(jnp.float32)).astype(x.dtype)


def rope_partial(x, pos):
    d = x.shape[-1]
    rot = d // ROT_FRACTION
    half = rot // 2
    inv = ROPE_THETA ** (-jnp.arange(half, dtype=jnp.float32) / half)
    ang = pos.astype(jnp.float32)[:, None] * inv[None, :]
    ang = ang.reshape((ang.shape[0],) + (1,) * (x.ndim - 3) + (half,))
    cos = jnp.cos(ang).astype(x.dtype)
    sin = jnp.sin(ang).astype(x.dtype)
    x1 = x[..., :half]
    x2 = x[..., half:rot]
    return jnp.concatenate([x1 * cos - x2 * sin, x2 * cos + x1 * sin, x[..., rot:]], axis=-1)


def pad_rows(a, n):
    return jnp.pad(a, [(0, 0), (0, n - a.shape[1])] + [(0, 0)] * (a.ndim - 2))


def project(x, g, w_in, pos, sizes):
    b, t, d_model = x.shape
    offsets = tuple(sum(sizes[:i + 1]) for i in range(len(sizes) - 1))
    qa, kva, ga, qb, kvb, qi, ki, wi, gm = jnp.split(norm_matmul(x, g, w_in), offsets, axis=-1)
    qa = qa.reshape(b, t, N_HEADS_A, HEAD_DIM_A)
    kva = kva.reshape(b, t, 6, KV_GROUPS_A, HEAD_DIM_A)
    kvb = kvb.reshape(b, t, 2, HEAD_DIM_B)
    return {
        'q_cmp': qa, 'q_rot': rope_partial(qa, pos),
        'k_cmp': kva[:, :, 0], 'v_cmp': kva[:, :, 1],
        'k_slc': rope_partial(kva[:, :, 2], pos), 'v_slc': kva[:, :, 3],
        'k_win': rope_partial(kva[:, :, 4], pos), 'v_win': kva[:, :, 5],
        'g_nsa': jax.nn.sigmoid(ga.reshape(b, t, 3, N_HEADS_A)),
        'q_b': rope_partial(qb.reshape(b, t, N_HEADS_B, HEAD_DIM_B), pos),
        'k_b': rope_partial(kvb[:, :, 0], pos), 'v_b': kvb[:, :, 1],
        'q_idx': rope_partial(qi.reshape(b, t, IDX_HEADS, IDX_DIM), pos),
        'k_idx': rope_partial(ki, pos), 'w_idx': wi,
        'gm': gm,
    }


def _query_operands(p, t_pad):
    b, t = p['q_cmp'].shape[:2]

    def rows(a, scale=1.0):
        return pad_rows((a * scale).reshape(b, t, -1), t_pad)

    qc = rows(p['q_cmp'], HEAD_DIM_A ** -0.5).astype(MXU_DTYPE)
    qr = rows(p['q_rot'], HEAD_DIM_A ** -0.5).astype(MXU_DTYPE)
    gates = rows(p['g_nsa'])
    qb = rows(p['q_b']).astype(MXU_DTYPE)
    qi = rows(p['q_idx'], IDX_DIM ** -0.5).astype(MXU_DTYPE)
    wi = rows(p['w_idx'].astype(jnp.float32), IDX_HEADS ** -0.5)
    return qc, qr, gates, qb, qi, wi


def mixer_prompt(x, g, w_in, cmp_pe, cmp_w1, cmp_b1, cmp_w2, sizes):
    b, s, _ = x.shape
    pos = jnp.arange(s, dtype=jnp.int32)
    p = project(x, g, w_in, pos, sizes)
    rows2 = lambda a: a.reshape(b, s, -1)
    cast = lambda a: rows2(a).astype(MXU_DTYPE)
    pe = jnp.tile(cmp_pe.reshape(2, 2, CMP_STRIDE, 1, HEAD_DIM_A), (1, 1, (s + LANE) // CMP_STRIDE, KV_GROUPS_A, 1))
    pe = pe.reshape(2, 2, 1, s + LANE, KV_A)
    raw = jnp.stack([pad_rows(rows2(p['k_cmp']), s + LANE), pad_rows(rows2(p['v_cmp']), s + LANE)])
    z = (raw[:, None] + pe).astype(MXU_DTYPE)
    zt = jnp.stack([_chunk_flat(z[0, 0]), _chunk_flat(z[1, 0])])
    zb = jnp.stack([_chunk_flat(z[0, 1]), _chunk_flat(z[1, 1])])
    kvc = compress(zt, zb, cmp_w1, cmp_b1, cmp_w2, s)
    kvc = kvc.reshape(2, b, KV_GROUPS_A, kvc.shape[2], HEAD_DIM_A)
    qc, qr, gates, qb, qi, wi = _query_operands(p, s)
    o_nsa = nsa_attention(qc, qr, gates, kvc, cast(p['k_slc']), cast(p['v_slc']), cast(p['k_win']),
                          cast(p['v_win']), tq=min(TQ, s), qpos0=0, wbase=0, n_keys=s)
    o_dsa = dsa_attention(qb, qi, wi, cast(p['k_b']), cast(p['v_b']), cast(p['k_idx']),
                          tq=min(TQ, s), qpos0=0, n_keep=min(DSA_TOPK, s // 4))
    nsa_rows = jnp.stack([p['k_cmp'], p['v_cmp'], p['k_slc'], p['v_slc']], axis=2)
    win_state = jnp.stack([p['k_win'], p['v_win']], axis=2)[:, -min(WINDOW, s):]
    dsa_rows = jnp.concatenate([p['k_b'], p['v_b'], p['k_idx']], axis=-1)
    return (o_nsa, o_dsa, p['gm']), nsa_rows, win_state, dsa_rows


def mixer_sample(x, g, cache_nsa, win_buf, cache_dsa, page_table, w_in, cmp_pe, cmp_w1, cmp_b1, cmp_w2, sizes):
    b, t, _ = x.shape
    page = cache_nsa.shape[1]
    past_len = page_table.shape[1] * page
    n_keys = past_len + t
    pos = past_len + jnp.arange(t, dtype=jnp.int32)
    p = project(x, g, w_in, pos, sizes)
    nsa_rows = jnp.stack([p['k_cmp'], p['v_cmp'], p['k_slc'], p['v_slc']], axis=2)
    dsa_rows = jnp.concatenate([p['k_b'], p['v_b'], p['k_idx']], axis=-1)
    qc, qr, gates, qb, qi, wi = _query_operands(p, TQ_STEP)

    cache_nsa_t = jnp.transpose(cache_nsa, (0, 2, 3, 4, 1)).reshape(cache_nsa.shape[0], NSA_SECTIONS * KV_A, page)
    zt, zb, ks, vs = nsa_gather(cache_nsa_t, page_table, nsa_rows.reshape(b, t, NSA_SECTIONS * KV_A), cmp_pe)
    merge_bg = lambda a: a.reshape((2, b * KV_GROUPS_A) + a.shape[3:])
    kvc = compress(merge_bg(zt), merge_bg(zb), cmp_w1, cmp_b1, cmp_w2, n_keys)
    kvc = kvc.reshape(2, b, KV_GROUPS_A, kvc.shape[2], HEAD_DIM_A)
    w_len = win_buf.shape[1]
    win_all = jnp.concatenate([win_buf, jnp.stack([p['k_win'], p['v_win']], axis=2)], axis=1)
    win_pad = pad_rows(win_all, w_len + WIN_CHUNK).astype(MXU_DTYPE)
    kw = win_pad[:, :, 0].reshape(b, w_len + WIN_CHUNK, KV_A)
    vw = win_pad[:, :, 1].reshape(b, w_len + WIN_CHUNK, KV_A)
    o_nsa = nsa_attention(qc, qr, gates, kvc, ks, vs, kw, vw,
                          tq=TQ_STEP, qpos0=past_len, wbase=past_len - w_len, n_keys=n_keys, kv_t=True)

    kb_, vb, ki = dsa_gather(jnp.swapaxes(cache_dsa, 1, 2), page_table, dsa_rows)
    o_dsa = dsa_attention(qb, qi, wi, kb_, vb, ki, tq=TQ_STEP, qpos0=past_len, n_keep=min(DSA_TOPK, n_keys // 4),
                          kv_t=True)
    return (o_nsa, o_dsa, pad_rows(p['gm'], TQ_STEP)), nsa_rows, win_all[:, -w_len:], dsa_rows


def kernel(x_prompt, x_sample, mem_prompt, cache_nsa_kv, state_nsa_win, cache_dsa_kv, cache_mem_kv, state_conv,
           page_table, norm_g, w_in, cmp_pe, cmp_w1, cmp_b1, cmp_w2, w_out_a, w_out_b, w_out, w_mem_q, w_mem_kv,
           w_mem_out, w_up, conv_w, conv_b, w_down, final_g):
    depth = w_in.shape[0]
    d_model = x_prompt.shape[-1]
    d_ff = w_down.shape[1]
    assert CONV_WIDTH == 3
    sizes = (N_HEADS_A * HEAD_DIM_A, 6 * KV_A, 3 * N_HEADS_A, N_HEADS_B * HEAD_DIM_B, 2 * HEAD_DIM_B,
             IDX_HEADS * IDX_DIM, IDX_DIM, IDX_HEADS, 2 * d_model)
    xp, xs = x_prompt, pad_rows(x_sample, TQ_STEP)
    t_step = x_sample.shape[1]
    nsa_p, nsa_s, win_p, win_s, dsa_p, dsa_s, mem_p, conv_p, conv_s = [], [], [], [], [], [], [], [], []
    for l in range(depth):
        assert l == depth - 1, "the fused FFN epilogue applies the final norm"
        branches_p, a, bwin, c = mixer_prompt(xp, norm_g[l, 0], w_in[l], cmp_pe[l], cmp_w1[l], cmp_b1[l], cmp_w2[l],
                                              sizes)
        nsa_p.append(a); win_p.append(bwin); dsa_p.append(c)
        branches_s, a, bwin, c = mixer_sample(xs[:, :t_step], norm_g[l, 0], cache_nsa_kv[l], state_nsa_win[l],
                                              cache_dsa_kv[l], page_table, w_in[l], cmp_pe[l], cmp_w1[l], cmp_b1[l],
                                              cmp_w2[l], sizes)
        nsa_s.append(a); win_s.append(bwin); dsa_s.append(c)
        kv_p = norm_matmul(mem_prompt, norm_g[l, 2], w_mem_kv[l])
        mem_p.append(kv_p.reshape(kv_p.shape[:2] + (2, MEM_HEADS, MEM_HEAD_DIM)))
        kv_s = cache_mem_kv[l].reshape(cache_mem_kv.shape[1:3] + (-1,))
        weights = (norm_g[l, 1], norm_g[l, 3], final_g, w_out_a[l], w_out_b[l], w_out[l], w_mem_q[l], w_mem_out[l],
                   w_up[l], conv_w[l], conv_b[l], w_down[l])
        xp, cp = dense_tail(xp, *branches_p, kv_p, jnp.zeros((xp.shape[0], CONV_WIDTH - 1, 2 * d_ff), xp.dtype),
                            xp.shape[1], *weights)
        xs, cs = dense_tail(xs, *branches_s, kv_s, state_conv[l], t_step, *weights)
        conv_p.append(cp); conv_s.append(cs)
    y_prompt, y_sample = xp, xs[:, :t_step]
    return (y_prompt, y_sample, jnp.stack(nsa_p), jnp.stack(nsa_s), jnp.stack(win_p), jnp.stack(win_s),
            jnp.stack(dsa_p), jnp.stack(dsa_s), jnp.stack(mem_p), jnp.stack(conv_p), jnp.stack(conv_s))
```

```python
import functools

import jax
import jax.numpy as jnp
from jax import lax
from jax.experimental import pallas as pl
from jax.experimental.pallas import tpu as pltpu

N_HEADS_A = 16
HEAD_DIM_A = 64
KV_GROUPS_A = 2
CMP_BLOCK = 32
CMP_STRIDE = 16
SLC_BLOCK = 64
N_SELECT = 16
WINDOW = 512
N_HEADS_B = 8
HEAD_DIM_B = 128
IDX_HEADS = 16
IDX_DIM = 64
DSA_TOPK = 256
MEM_HEADS = 4
MEM_HEAD_DIM = 128
CONV_WIDTH = 3
ROPE_THETA = 500000.0
ROT_FRACTION = 4
EPS = 1e-6
KV_A = KV_GROUPS_A * HEAD_DIM_A
CMP_PER_SLC = SLC_BLOCK // CMP_STRIDE
HPG_A = N_HEADS_A // KV_GROUPS_A
DSA_ROW = 2 * HEAD_DIM_B + IDX_DIM
NSA_SECTIONS = 4

LANE = 128
SUBLANE = 8
BF16_ROWS = 16
VMEM_LIMIT = 48 * 1024 * 1024

NEG_INF = float('-inf')
POS_INF = float('inf')
MXU_DTYPE = jnp.bfloat16
TQ = 128
TQ_STEP = BF16_ROWS
KEY_CHUNK = 512
WIN_CHUNK = 128
PAGES_PER_STEP = 4
BISECT_ITERS = 40
SLC_SHIFT = SLC_BLOCK.bit_length() - 1
CMP_PER_SLC_SHIFT = CMP_PER_SLC.bit_length() - 1


def _round_up(n, m):
    return -(-n // m) * m


def _tile(n, cap):
    if n <= cap:
        return n
    best = None
    for t in range(LANE, cap + 1, LANE):
        if n % t == 0:
            best = t
    assert best is not None, (n, cap)
    return best


def _mm_kernel(x_ref, w_ref, o_ref, acc_ref):
    @pl.when(pl.program_id(2) == 0)
    def _():
        acc_ref[...] = jnp.zeros_like(acc_ref)

    acc_ref[...] += jnp.dot(x_ref[...], w_ref[...], preferred_element_type=jnp.float32)

    @pl.when(pl.program_id(2) == pl.num_programs(2) - 1)
    def _():
        o_ref[...] = acc_ref[...].astype(o_ref.dtype)


def matmul(x, w, out_dtype=jnp.float32):
    lead = x.shape[:-1]
    k = x.shape[-1]
    n = w.shape[-1]
    x2 = x.reshape(-1, k).astype(MXU_DTYPE)
    w2 = w.astype(MXU_DTYPE)
    m0 = x2.shape[0]
    tm = 512 if m0 >= 512 else _round_up(m0, BF16_ROWS)
    m = _round_up(m0, tm)
    if m != m0:
        x2 = jnp.pad(x2, ((0, m - m0), (0, 0)))
    tn = _tile(n, 1024)
    tk = _tile(k, 2048)
    out = pl.pallas_call(
        _mm_kernel,
        grid=(m // tm, n // tn, k // tk),
        in_specs=[pl.BlockSpec((tm, tk), lambda i, j, l: (i, l)),
                  pl.BlockSpec((tk, tn), lambda i, j, l: (l, j))],
        out_specs=pl.BlockSpec((tm, tn), lambda i, j, l: (i, j)),
        out_shape=jax.ShapeDtypeStruct((m, n), out_dtype),
        scratch_shapes=[pltpu.VMEM((tm, tn), jnp.float32)],
        compiler_params=pltpu.CompilerParams(
            dimension_semantics=("parallel", "parallel", "arbitrary"),
            vmem_limit_bytes=VMEM_LIMIT),
        name="matmul",
    )(x2, w2)
    return out[:m0].reshape(lead + (n,))


def _rms(x, g):
    return x * lax.rsqrt(jnp.mean(x * x, axis=-1, keepdims=True) + EPS) * g


def _norm_mm_kernel(x_ref, g_ref, w_ref, o_ref, xn_ref):
    @pl.when(pl.program_id(1) == 0)
    def _():
        xn_ref[...] = _rms(x_ref[...], g_ref[...]).astype(xn_ref.dtype)

    o_ref[...] = jnp.dot(xn_ref[...], w_ref[...], preferred_element_type=jnp.float32).astype(o_ref.dtype)


def norm_matmul(x, g, w):
    lead = x.shape[:-1]
    k = x.shape[-1]
    n = w.shape[-1]
    x2 = x.reshape(-1, k)
    m = x2.shape[0]
    tm = 512 if m % 512 == 0 else m
    tn = _tile(n, 1536)
    out = pl.pallas_call(
        _norm_mm_kernel,
        grid=(m // tm, n // tn),
        in_specs=[pl.BlockSpec((tm, k), lambda i, j: (i, 0)),
                  pl.BlockSpec((1, k), lambda i, j: (0, 0)),
                  pl.BlockSpec((k, tn), lambda i, j: (0, j))],
        out_specs=pl.BlockSpec((tm, tn), lambda i, j: (i, j)),
        out_shape=jax.ShapeDtypeStruct((m, n), jnp.float32),
        scratch_shapes=[pltpu.VMEM((tm, k), MXU_DTYPE)],
        compiler_params=pltpu.CompilerParams(
            dimension_semantics=("parallel", "arbitrary"), vmem_limit_bytes=VMEM_LIMIT),
        name="norm_matmul",
    )(x2, g.reshape(1, k).astype(jnp.float32), w.astype(MXU_DTYPE))
    return out.reshape(lead + (n,))


def _gated_merge_kernel(oa_ref, ob_ref, wa_ref, wb_ref, ga_ref, gb_ref, o_ref):
    dot = functools.partial(jnp.dot, preferred_element_type=jnp.float32)
    ya = dot(oa_ref[...], wa_ref[...])
    yb = dot(ob_ref[...], wb_ref[...])
    o_ref[...] = (jax.nn.sigmoid(ga_ref[...]) * ya + jax.nn.sigmoid(gb_ref[...]) * yb).astype(o_ref.dtype)


def gated_merge(o_nsa, o_dsa, gm, w_oa, w_ob):
    m, ka = o_nsa.shape
    kb = o_dsa.shape[1]
    n = w_oa.shape[1]
    tm = 512 if m % 512 == 0 else m
    tn = _tile(n, 1024)
    nb = n // tn
    return pl.pallas_call(
        _gated_merge_kernel,
        grid=(m // tm, nb),
        in_specs=[pl.BlockSpec((tm, ka), lambda i, j: (i, 0)),
                  pl.BlockSpec((tm, kb), lambda i, j: (i, 0)),
                  pl.BlockSpec((ka, tn), lambda i, j: (0, j)),
                  pl.BlockSpec((kb, tn), lambda i, j: (0, j)),
                  pl.BlockSpec((tm, tn), lambda i, j: (i, j)),
                  pl.BlockSpec((tm, tn), lambda i, j: (i, j + nb))],
        out_specs=pl.BlockSpec((tm, tn), lambda i, j: (i, j)),
        out_shape=jax.ShapeDtypeStruct((m, n), MXU_DTYPE),
        compiler_params=pltpu.CompilerParams(
            dimension_semantics=("parallel", "parallel"), vmem_limit_bytes=VMEM_LIMIT),
        name="gated_merge",
    )(o_nsa, o_dsa, w_oa.astype(MXU_DTYPE), w_ob.astype(MXU_DTYPE), gm, gm)


def _out_proj_kernel(z_ref, w_ref, x_ref, o_ref):
    o_ref[...] = x_ref[...] + jnp.dot(z_ref[...], w_ref[...], preferred_element_type=jnp.float32)


def out_proj_residual(z, w, x):
    m, k = z.shape
    n = w.shape[1]
    tm = 256 if m % 256 == 0 else m
    return pl.pallas_call(
        _out_proj_kernel,
        grid=(m // tm,),
        in_specs=[pl.BlockSpec((tm, k), lambda i: (i, 0)),
                  pl.BlockSpec((k, n), lambda i: (0, 0)),
                  pl.BlockSpec((tm, n), lambda i: (i, 0))],
        out_specs=pl.BlockSpec((tm, n), lambda i: (i, 0)),
        out_shape=jax.ShapeDtypeStruct((m, n), jnp.float32),
        compiler_params=pltpu.CompilerParams(dimension_semantics=("parallel",), vmem_limit_bytes=VMEM_LIMIT),
        name="out_proj_residual",
    )(z, w.astype(MXU_DTYPE), x)


def _mem_block_kernel(x_ref, g1_ref, g2_ref, wq_ref, kv_ref, wo_ref, x_out_ref, xn_out_ref):
    d = MEM_HEAD_DIM
    hd = MEM_HEADS * d
    x = x_ref[0]
    xn = _rms(x, g1_ref[...]).astype(wq_ref.dtype)
    q = jnp.dot(xn, wq_ref[...], preferred_element_type=jnp.float32).astype(wq_ref.dtype)
    outs = []
    for h in range(MEM_HEADS):
        k = kv_ref[0, :, h * d:(h + 1) * d].astype(wq_ref.dtype)
        v = kv_ref[0, :, hd + h * d:hd + (h + 1) * d].astype(wq_ref.dtype)
        s = _dot_nt(q[:, h * d:(h + 1) * d], k) * d ** -0.5
        e = jnp.exp(s - jnp.max(s, axis=1, keepdims=True))
        p = e / jnp.sum(e, axis=1, keepdims=True)
        outs.append(jnp.dot(p.astype(v.dtype), v, preferred_element_type=jnp.float32))
    o = jnp.concatenate(outs, axis=1).astype(wo_ref.dtype)
    x2 = x + jnp.dot(o, wo_ref[...], preferred_element_type=jnp.float32)
    x_out_ref[0] = x2
    xn_out_ref[0] = _rms(x2, g2_ref[...]).astype(xn_out_ref.dtype)


def mem_block(x, g1, g2, w_q, kv, w_o):
    b, t, dm = x.shape
    mt, kvw = kv.shape[1:]
    hd = w_q.shape[1]
    tm = 256 if t % 256 == 0 else t
    row_spec = pl.BlockSpec((1, tm, dm), lambda bi, i: (bi, i, 0))
    g_spec = pl.BlockSpec((1, dm), lambda bi, i: (0, 0))
    return pl.pallas_call(
        _mem_block_kernel,
        grid=(b, t // tm),
        in_specs=[row_spec, g_spec, g_spec,
                  pl.BlockSpec((dm, hd), lambda bi, i: (0, 0)),
                  pl.BlockSpec((1, mt, kvw), lambda bi, i: (bi, 0, 0)),
                  pl.BlockSpec((hd, dm), lambda bi, i: (0, 0))],
        out_specs=[row_spec, row_spec],
        out_shape=[jax.ShapeDtypeStruct((b, t, dm), jnp.float32), jax.ShapeDtypeStruct((b, t, dm), MXU_DTYPE)],
        compiler_params=pltpu.CompilerParams(
            dimension_semantics=("parallel", "parallel"), vmem_limit_bytes=VMEM_LIMIT),
        name="mem_block",
    )(x, g1.reshape(1, dm).astype(jnp.float32), g2.reshape(1, dm).astype(jnp.float32), w_q.astype(MXU_DTYPE),
      kv, w_o.astype(MXU_DTYPE))


def _ffn_up_kernel(x_ref, halo_ref, wg_ref, wu_ref, cwg_ref, cwu_ref, cbg_ref, cbu_ref, pg_ref, pu_ref,
                   h_ref, sg_ref, su_ref, *, tm, t_real):
    i = pl.program_id(2)
    dot = functools.partial(jnp.dot, preferred_element_type=jnp.float32)
    row = _iota((tm, 1), 0)
    last = (t_real - 1) // tm
    r_last = (t_real - 1) % tm

    def branch(w_ref, cw_ref, cb_ref, p_ref, s_ref):
        u = dot(x_ref[0], w_ref[...])
        uh = dot(halo_ref[0], w_ref[...])
        n_halo = uh.shape[0]
        p0 = jnp.where(i == 0, p_ref[0, 0:1, :], uh[n_halo - 2:n_halo - 1])
        p1 = jnp.where(i == 0, p_ref[0, 1:2, :], uh[n_halo - 1:n_halo])
        u1 = jnp.where(row == 0, p1, pltpu.roll(u, 1, 0))
        u2 = jnp.where(row == 0, p0, jnp.where(row == 1, p1, pltpu.roll(u, 2, 0)))

        @pl.when(i == last)
        def _():
            s_ref[0] = u[r_last - 1:r_last + 1]

        return cb_ref[...] + u2 * cw_ref[0:1, :] + u1 * cw_ref[1:2, :] + u * cw_ref[2:3, :]

    gate = branch(wg_ref, cwg_ref, cbg_ref, pg_ref, sg_ref)
    up = branch(wu_ref, cwu_ref, cbu_ref, pu_ref, su_ref)
    h_ref[0] = (jax.nn.silu(gate) * up).astype(h_ref.dtype)


def ffn_up(xn, prev, w_up, conv_w, conv_b, t_real):
    b, t, dm = xn.shape
    f2 = w_up.shape[1]
    f = f2 // 2
    tm = 512 if t % 512 == 0 else t
    tn = _tile(f, 512)
    nf = f // tn
    halo = min(BF16_ROWS, tm)
    hpt = tm // halo
    assert t_real >= 2 and (t_real - 1) % tm >= 1
    w_up = w_up.astype(MXU_DTYPE)
    conv_b = conv_b.reshape(1, f2)
    col = lambda off: (lambda j, bi, i: (0, j + off))
    st = lambda off: (lambda j, bi, i: (bi, 0, j + off))
    specs = [pl.BlockSpec((1, tm, dm), lambda j, bi, i: (bi, i, 0)),
             pl.BlockSpec((1, halo, dm), lambda j, bi, i: (bi, jnp.maximum(i * hpt - 1, 0), 0)),
             pl.BlockSpec((dm, tn), col(0)), pl.BlockSpec((dm, tn), col(nf)),
             pl.BlockSpec((CONV_WIDTH, tn), col(0)), pl.BlockSpec((CONV_WIDTH, tn), col(nf)),
             pl.BlockSpec((1, tn), col(0)), pl.BlockSpec((1, tn), col(nf)),
             pl.BlockSpec((1, 2, tn), st(0)), pl.BlockSpec((1, 2, tn), st(nf))]
    h, sg, su = pl.pallas_call(
        functools.partial(_ffn_up_kernel, tm=tm, t_real=t_real),
        grid=(nf, b, t // tm),
        in_specs=specs,
        out_specs=[pl.BlockSpec((1, tm, tn), lambda j, bi, i: (bi, i, j)),
                   pl.BlockSpec((1, 2, tn), st(0)), pl.BlockSpec((1, 2, tn), st(0))],
        out_shape=[jax.ShapeDtypeStruct((b, t, f), MXU_DTYPE), jax.ShapeDtypeStruct((b, 2, f), jnp.float32),
                   jax.ShapeDtypeStruct((b, 2, f), jnp.float32)],
        compiler_params=pltpu.CompilerParams(
            dimension_semantics=("parallel", "parallel", "arbitrary"), vmem_limit_bytes=VMEM_LIMIT),
        name="ffn_up",
    )(xn, xn, w_up, w_up, conv_w, conv_w, conv_b, conv_b, prev, prev)
    return h, jnp.concatenate([sg, su], axis=-1)


def _ffn_down_kernel(h_ref, w_ref, x_ref, g_ref, o_ref, acc_ref):
    @pl.when(pl.program_id(1) == 0)
    def _():
        acc_ref[...] = x_ref[...]

    acc_ref[...] += jnp.dot(h_ref[...], w_ref[...], preferred_element_type=jnp.float32)

    @pl.when(pl.program_id(1) == pl.num_programs(1) - 1)
    def _():
        o_ref[...] = _rms(acc_ref[...], g_ref[...])


def ffn_down_norm(h, w_down, x, g):
    m, f = h.shape
    dm = w_down.shape[1]
    tm = 512 if m % 512 == 0 else m
    tk = _tile(f, 2048)
    return pl.pallas_call(
        _ffn_down_kernel,
        grid=(m // tm, f // tk),
        in_specs=[pl.BlockSpec((tm, tk), lambda i, l: (i, l)),
                  pl.BlockSpec((tk, dm), lambda i, l: (l, 0)),
                  pl.BlockSpec((tm, dm), lambda i, l: (i, 0)),
                  pl.BlockSpec((1, dm), lambda i, l: (0, 0))],
        out_specs=pl.BlockSpec((tm, dm), lambda i, l: (i, 0)),
        out_shape=jax.ShapeDtypeStruct((m, dm), jnp.float32),
        scratch_shapes=[pltpu.VMEM((tm, dm), jnp.float32)],
        compiler_params=pltpu.CompilerParams(
            dimension_semantics=("parallel", "arbitrary"), vmem_limit_bytes=VMEM_LIMIT),
        name="ffn_down_norm",
    )(h, w_down.astype(MXU_DTYPE), x, g.reshape(1, dm).astype(jnp.float32))


def dense_tail(x, o_nsa, o_dsa, gm, kv_mem, prev_u, t_real, g_mem, g_ffn, g_final, w_oa, w_ob, w_o, w_mq, w_mo,
               w_up, conv_w, conv_b, w_down):
    b, t, dm = x.shape
    rows = lambda a: a.reshape(b * t, a.shape[-1])
    z = gated_merge(rows(o_nsa), rows(o_dsa), rows(gm), w_oa, w_ob)
    x1 = out_proj_residual(z, w_o, rows(x)).reshape(b, t, dm)
    x2, xn2 = mem_block(x1, g_mem, g_ffn, w_mq, kv_mem, w_mo)
    h, state = ffn_up(xn2, prev_u, w_up, conv_w, conv_b, t_real)
    y = ffn_down_norm(rows(h), w_down, rows(x2), g_final)
    return y.reshape(b, t, dm), state


def _dot_nt(a, b):
    return lax.dot_general(a, b, (((1,), (1,)), ((), ())), preferred_element_type=jnp.float32)


def _iota(shape, dim):
    return lax.broadcasted_iota(jnp.int32, shape, dim)


def _flash_init(rows, d):
    return (jnp.full((rows, 1), NEG_INF, jnp.float32), jnp.zeros((rows, 1), jnp.float32),
            jnp.zeros((rows, d), jnp.float32))


def _flash_step(carry, q, k, v, madd, nh, scale=None, kv_t=False):
    m, l, acc = carry
    s = jnp.dot(q, k, preferred_element_type=jnp.float32) if kv_t else _dot_nt(q, k)
    if scale is not None:
        s = s * scale
    r, kb = s.shape
    s = (s.reshape(nh, r // nh, kb) + madd[None]).reshape(r, kb)
    m_new = jnp.maximum(m, jnp.max(s, axis=1, keepdims=True))
    m_safe = jnp.where(m_new == NEG_INF, 0.0, m_new)
    p = jnp.exp(s - m_safe)
    alpha = jnp.exp(m - m_safe)
    l = alpha * l + jnp.sum(p, axis=1, keepdims=True)
    pv = _dot_nt(p.astype(v.dtype), v) if kv_t else jnp.dot(p.astype(v.dtype), v, preferred_element_type=jnp.float32)
    return m_new, l, alpha * acc + pv


def _flash_finish(carry):
    _, l, acc = carry
    return acc / jnp.maximum(l, 1e-30)


def _split_dot(x, m01):
    hi = x.astype(jnp.bfloat16)
    r1 = x - hi.astype(jnp.float32)
    mid = r1.astype(jnp.bfloat16)
    lo = (r1 - mid.astype(jnp.float32)).astype(jnp.bfloat16)
    dot = functools.partial(jnp.dot, preferred_element_type=jnp.float32)
    return dot(hi, m01) + dot(mid, m01) + dot(lo, m01)


def _nsa_group(g, i, qc_ref, qr_ref, gate_ref, kc_ref, vc_ref, ks_ref, vs_ref, kw_ref, vw_ref, mask_ref,
               *, tq, kb, qpos0, wbase, ns, n_sel, kv_t):
    nh, d = HPG_A, HEAD_DIM_A
    rows = nh * tq
    nc = kc_ref.shape[3]
    nsp = ns if ns <= LANE else _round_up(ns, LANE)
    t0 = qpos0 + i * tq
    nj = (t0 + tq - 1) // kb + 1
    cols = slice(g * d, (g + 1) * d)

    def stack_heads(q_ref):
        return jnp.concatenate([q_ref[0, :, (g * nh + h) * d:(g * nh + h + 1) * d] for h in range(nh)], axis=0)

    qc = stack_heads(qc_ref)
    qr = stack_heads(qr_ref)
    tpos = t0 + _iota((tq, 1), 0)

    blk_last = _iota((1, nc), 1) * CMP_STRIDE + (CMP_BLOCK - 1)
    madd_c = jnp.where(blk_last <= tpos, 0.0, NEG_INF)
    s = _dot_nt(qc, kc_ref[0, 0, g]).reshape(nh, tq, nc) + madd_c[None]
    m = jnp.max(s, axis=2, keepdims=True)
    m = jnp.where(m == NEG_INF, 0.0, m)
    e = jnp.exp(s - m)
    p = e / jnp.maximum(jnp.sum(e, axis=2, keepdims=True), 1e-30)
    o_cmp = jnp.dot(p.reshape(rows, nc).astype(vc_ref.dtype), vc_ref[0, 0, g], preferred_element_type=jnp.float32)

    imp = jnp.sum(p, axis=0)
    c_id = _iota((nc, nsp), 0)
    m_id = _iota((nc, nsp), 1)
    overlap = (jnp.right_shift(c_id, CMP_PER_SLC_SHIFT) == m_id) | (c_id == m_id * CMP_PER_SLC - 1)
    score = _split_dot(imp, overlap.astype(jnp.bfloat16))
    blk = _iota((1, nsp), 1)
    cur = jnp.right_shift(tpos, SLC_SHIFT)
    forced = (blk == 0) | (blk == cur) | (blk == cur - 1)
    sc = jnp.where(forced, POS_INF, jnp.where(blk * SLC_BLOCK <= tpos, score, NEG_INF))
    rank = jnp.zeros((tq, nsp), jnp.float32)
    for mp in range(ns):
        col = sc[:, mp:mp + 1]
        beats = (col > sc) | ((col == sc) & (blk > mp))
        rank = rank + jnp.where(beats, 1.0, 0.0)
    sel = jnp.where(rank < n_sel, 1.0, 0.0).astype(jnp.bfloat16)

    def make_mask(j, _):
        kpos = j * kb + _iota((1, kb), 1)
        expand = (jnp.right_shift(j * kb + _iota((nsp, kb), 1), SLC_SHIFT) == _iota((nsp, kb), 0))
        hit = jnp.dot(sel, expand.astype(jnp.bfloat16), preferred_element_type=jnp.float32)
        mask_ref[j] = jnp.where((hit > 0.5) & (kpos <= tpos), 0.0, NEG_INF)
        return 0

    lax.fori_loop(0, nj, make_mask, 0)

    def slc_body(j, carry):
        off = pl.multiple_of(j * kb, kb)
        return _flash_step(carry, qr, ks_ref[0, pl.ds(off, kb), cols], vs_ref[0, pl.ds(off, kb), cols],
                           mask_ref[j], nh)

    if kv_t:
        o_slc = _flash_finish(_flash_step(_flash_init(rows, d), qr, ks_ref[0, cols, :], vs_ref[0, cols, :],
                                          mask_ref[0], nh, kv_t=True))
    else:
        o_slc = _flash_finish(lax.fori_loop(0, nj, slc_body, _flash_init(rows, d)))

    wk = min(_round_up(WINDOW + tq, WIN_CHUNK), kw_ref.shape[1])
    first = jnp.maximum(t0 - (WINDOW - 1) - wbase, 0) // WIN_CHUNK * WIN_CHUNK
    off = pl.multiple_of(jnp.minimum(first, kw_ref.shape[1] - wk), WIN_CHUNK)
    dist = tpos - (wbase + off + _iota((1, wk), 1))
    madd_w = jnp.where((dist >= 0) & (dist < WINDOW), 0.0, NEG_INF)
    o_win = _flash_finish(_flash_step(_flash_init(rows, d), qr, kw_ref[0, pl.ds(off, wk), cols],
                                      vw_ref[0, pl.ds(off, wk), cols], madd_w, nh))

    gates = gate_ref[0]
    outs = []
    for h in range(nh):
        rs = slice(h * tq, (h + 1) * tq)
        c = g * nh + h
        outs.append(gates[:, c:c + 1] * o_cmp[rs] + gates[:, N_HEADS_A + c:N_HEADS_A + c + 1] * o_slc[rs]
                    + gates[:, 2 * N_HEADS_A + c:2 * N_HEADS_A + c + 1] * o_win[rs])
    return outs


def _nsa_kernel(qc_ref, qr_ref, gate_ref, kc_ref, vc_ref, ks_ref, vs_ref, kw_ref, vw_ref, o_ref, mask_ref, **kw):
    i = pl.program_id(1)
    outs = []
    for g in range(KV_GROUPS_A):
        outs += _nsa_group(g, i, qc_ref, qr_ref, gate_ref, kc_ref, vc_ref, ks_ref, vs_ref, kw_ref, vw_ref,
                           mask_ref, **kw)
    o_ref[0] = jnp.concatenate(outs, axis=1).astype(o_ref.dtype)


def nsa_attention(qc, qr, gates, kvc, ks, vs, kw, vw, *, tq, qpos0, wbase, n_keys, kv_t=False):
    b, t, hd = qc.shape
    l = ks.shape[2] if kv_t else ks.shape[1]
    lw = kw.shape[1]
    nc, d = kvc.shape[3:]
    kb = min(KEY_CHUNK, l) if tq >= TQ else l
    assert l % kb == 0 and t % tq == 0 and lw % WIN_CHUNK == 0 and (kb == l or not kv_t)
    assert (qpos0 + t - 1) // kb + 1 <= l // kb and (qpos0 + t - 1 - wbase) // WIN_CHUNK + 1 <= lw // WIN_CHUNK
    assert (qpos0 - wbase) % WIN_CHUNK == 0 and WIN_CHUNK % tq == 0 and tq > 1
    ns = l // SLC_BLOCK
    n_sel = min(N_SELECT, -(-n_keys // SLC_BLOCK))
    q_spec = pl.BlockSpec((1, tq, hd), lambda bi, i: (bi, i, 0))
    kc_spec = pl.BlockSpec((1, 1, KV_GROUPS_A, nc, d), lambda bi, i: (0, bi, 0, 0, 0))
    vc_spec = pl.BlockSpec((1, 1, KV_GROUPS_A, nc, d), lambda bi, i: (1, bi, 0, 0, 0))
    k_spec = pl.BlockSpec((1, KV_A, l) if kv_t else (1, l, KV_A), lambda bi, i: (bi, 0, 0))
    w_spec = pl.BlockSpec((1, lw, KV_A), lambda bi, i: (bi, 0, 0))
    return pl.pallas_call(
        functools.partial(_nsa_kernel, tq=tq, kb=kb, qpos0=qpos0, wbase=wbase, ns=ns, n_sel=n_sel, kv_t=kv_t),
        grid=(b, t // tq),
        in_specs=[q_spec, q_spec, pl.BlockSpec((1, tq, 3 * N_HEADS_A), lambda bi, i: (bi, i, 0)),
                  kc_spec, vc_spec, k_spec, k_spec, w_spec, w_spec],
        out_specs=q_spec,
        out_shape=jax.ShapeDtypeStruct((b, t, hd), MXU_DTYPE),
        scratch_shapes=[pltpu.VMEM((l // kb, tq, kb), jnp.float32)],
        compiler_params=pltpu.CompilerParams(
            dimension_semantics=("parallel", "arbitrary"), vmem_limit_bytes=VMEM_LIMIT),
        name="nsa_attention",
    )(qc, qr, gates, kvc, kvc, ks, vs, kw, vw)


def _dsa_kernel(qb_ref, qi_ref, wi_ref, kb_ref, vb_ref, ki_ref, o_ref, score_ref, *, tq, kb, qpos0, n_keep, kv_t):
    i = pl.program_id(1)
    nh, d = N_HEADS_B, HEAD_DIM_B
    t0 = qpos0 + i * tq
    nj = (t0 + tq - 1) // kb + 1
    tpos = t0 + _iota((tq, 1), 0)
    w = wi_ref[0]

    def idx_body(j, carry):
        lo, hi = carry
        off = pl.multiple_of(j * kb, kb)
        kidx = ki_ref[0] if kv_t else ki_ref[0, pl.ds(off, kb), :]
        acc = jnp.zeros((tq, kb), jnp.float32)
        for h in range(IDX_HEADS):
            qh = qi_ref[0, :, h * IDX_DIM:(h + 1) * IDX_DIM]
            dots = jnp.dot(qh, kidx, preferred_element_type=jnp.float32) if kv_t else _dot_nt(qh, kidx)
            acc = acc + w[:, h:h + 1] * jnp.maximum(dots, 0.0)
        vis = (off + _iota((1, kb), 1)) <= tpos
        score_ref[j] = jnp.where(vis, acc, NEG_INF)
        lo = jnp.minimum(lo, jnp.min(jnp.where(vis, acc, POS_INF), axis=1, keepdims=True))
        hi = jnp.maximum(hi, jnp.max(jnp.where(vis, acc, NEG_INF), axis=1, keepdims=True))
        return lo, hi

    lo, hi = lax.fori_loop(0, nj, idx_body, (jnp.full((tq, 1), POS_INF, jnp.float32),
                                             jnp.full((tq, 1), NEG_INF, jnp.float32)))

    k = float(n_keep)
    n_vis = (tpos + 1).astype(jnp.float32)

    def unsettled(cnt_lo):
        return jnp.max(jnp.where((cnt_lo != k) & (n_vis > k), 1.0, 0.0)) > 0.0

    def bisect(state):
        it, lo, hi, cnt_lo = state
        mid = 0.5 * (lo + hi)
        mid_b = jnp.broadcast_to(mid, (tq, LANE))

        def count(j, acc):
            sc = score_ref[j]
            for c in range(kb // LANE):
                acc = acc + jnp.where(sc[:, c * LANE:(c + 1) * LANE] >= mid_b, 1.0, 0.0)
            return acc

        cnt = jnp.sum(lax.fori_loop(0, nj, count, jnp.zeros((tq, LANE), jnp.float32)), axis=1, keepdims=True)
        ge = cnt >= k
        return it + 1, jnp.where(ge, mid, lo), jnp.where(ge, hi, mid), jnp.where(ge, cnt, cnt_lo)

    _, thr, _, _ = lax.while_loop(lambda st: (st[0] < BISECT_ITERS) & unsettled(st[3]), bisect,
                                  (jnp.int32(0), lo, hi, n_vis))

    q = jnp.concatenate([qb_ref[0, :, h * d:(h + 1) * d] for h in range(nh)], axis=0)

    def att_body(j, carry):
        off = pl.multiple_of(j * kb, kb)
        madd = jnp.where(score_ref[j] >= thr, 0.0, NEG_INF)
        if kv_t:
            return _flash_step(carry, q, kb_ref[0], vb_ref[0], madd, nh, scale=d ** -0.5, kv_t=True)
        return _flash_step(carry, q, kb_ref[0, pl.ds(off, kb), :], vb_ref[0, pl.ds(off, kb), :], madd, nh,
                           scale=d ** -0.5)

    o = _flash_finish(lax.fori_loop(0, nj, att_body, _flash_init(nh * tq, d)))
    o_ref[0] = jnp.concatenate([o[h * tq:(h + 1) * tq] for h in range(nh)], axis=1).astype(o_ref.dtype)


def dsa_attention(qb, qi, wi, kb_, vb, ki, *, tq, qpos0, n_keep, kv_t=False):
    b, t, hd = qb.shape
    l = kb_.shape[2] if kv_t else kb_.shape[1]
    kb = min(KEY_CHUNK, l) if tq >= TQ else l
    assert l % kb == 0 and t % tq == 0 and (qpos0 + t - 1) // kb + 1 <= l // kb and (kb == l or not kv_t)
    kv_spec = lambda width: pl.BlockSpec((1, width, l) if kv_t else (1, l, width), lambda bi, i: (bi, 0, 0))
    return pl.pallas_call(
        functools.partial(_dsa_kernel, tq=tq, kb=kb, qpos0=qpos0, n_keep=n_keep, kv_t=kv_t),
        grid=(b, t // tq),
        in_specs=[pl.BlockSpec((1, tq, hd), lambda bi, i: (bi, i, 0)),
                  pl.BlockSpec((1, tq, IDX_HEADS * IDX_DIM), lambda bi, i: (bi, i, 0)),
                  pl.BlockSpec((1, tq, IDX_HEADS), lambda bi, i: (bi, i, 0)),
                  kv_spec(HEAD_DIM_B), kv_spec(HEAD_DIM_B), kv_spec(IDX_DIM)],
        out_specs=pl.BlockSpec((1, tq, hd), lambda bi, i: (bi, i, 0)),
        out_shape=jax.ShapeDtypeStruct((b, t, hd), MXU_DTYPE),
        scratch_shapes=[pltpu.VMEM((l // kb, tq, kb), jnp.float32)],
        compiler_params=pltpu.CompilerParams(
            dimension_semantics=("parallel", "arbitrary"), vmem_limit_bytes=VMEM_LIMIT),
        name="dsa_attention",
    )(qb, qi, wi, kb_, vb, ki)


def _page_maps(n_pages, pp):
    n_steps = n_pages // pp

    def page_map(r):
        return lambda b, s, pt: (pt[b * n_pages + jnp.minimum(s, n_steps - 1) * pp + r], 0, 0)

    return n_steps, page_map


def _chunk_rows(tok_ref, sec, row0, n):
    d = HEAD_DIM_A
    first_half = _iota((n, KV_A), 1) < d
    pieces = [[] for _ in range(KV_GROUPS_A)]
    for j in range(0, CMP_STRIDE, 2):
        a, b = [tok_ref[sec, pl.ds(row0 + jj, n, stride=CMP_STRIDE), :] for jj in (j, j + 1)]
        pieces[0].append(jnp.where(first_half, a, pltpu.roll(b, d, 1)))
        pieces[1].append(jnp.where(first_half, pltpu.roll(a, d, 1), b))
    return [jnp.concatenate(p, axis=1) for p in pieces]


def _nsa_gather_kernel(pt_ref, *refs, pp, n_steps, rows):
    del pt_ref
    pages, (tail_tok_ref, tail_t_ref, pe_ref) = refs[:pp], refs[pp:pp + 3]
    zt_ref, zb_ref, ks_ref, vs_ref, tok_ref = refs[pp + 3:]
    is_tail = pl.program_id(1) == n_steps
    z = [[[] for _ in range(KV_GROUPS_A)] for _ in range(2)]
    for r in range(pp):
        cs = slice(r * rows, (r + 1) * rows)
        ks_ref[0, :, cs] = jnp.where(is_tail, tail_t_ref[0, :KV_A, cs],
                                     pages[r][0, 2 * KV_A:3 * KV_A, :]).astype(ks_ref.dtype)
        vs_ref[0, :, cs] = jnp.where(is_tail, tail_t_ref[0, KV_A:, cs],
                                     pages[r][0, 3 * KV_A:, :]).astype(vs_ref.dtype)
        for sec in range(2):
            tok_ref[sec, cs, :] = jnp.where(is_tail, tail_tok_ref[0, sec, cs, :],
                                            pages[r][0, sec * KV_A:(sec + 1) * KV_A, :].T)
            for g, zg in enumerate(_chunk_rows(tok_ref, sec, r * rows, rows // CMP_STRIDE)):
                z[sec][g].append(zg)
    for sec in range(2):
        for g in range(KV_GROUPS_A):
            zf = jnp.concatenate(z[sec][g], axis=0)
            zt_ref[sec, 0, g] = (zf + pe_ref[sec, 0]).astype(zt_ref.dtype)
            zb_ref[sec, 0, g] = (zf + pe_ref[sec, 1]).astype(zb_ref.dtype)


def nsa_gather(cache_t, page_table, new_rows, cmp_pe):
    db, n_pages = page_table.shape
    width, rows = cache_t.shape[1:]
    pp = PAGES_PER_STEP
    n_steps, page_map = _page_maps(n_pages, pp)
    l = (n_steps + 1) * pp * rows
    cps = pp * rows // CMP_STRIDE
    flat = CMP_STRIDE * HEAD_DIM_A
    pe = cmp_pe.reshape(2, 2, 1, flat).astype(jnp.float32)
    tail = pad_rows(new_rows, pp * rows)
    tail_tok = tail[:, :, :2 * KV_A].reshape(db, pp * rows, 2, KV_A).swapaxes(1, 2)
    tail_t = tail[:, :, 2 * KV_A:].swapaxes(1, 2)
    z_spec = pl.BlockSpec((2, 1, KV_GROUPS_A, cps, flat), lambda b, s, pt: (0, b, 0, s, 0))
    r_spec = pl.BlockSpec((1, KV_A, pp * rows), lambda b, s, pt: (b, 0, s))
    z_shape = jax.ShapeDtypeStruct((2, db, KV_GROUPS_A, l // CMP_STRIDE, flat), MXU_DTYPE)
    r_shape = jax.ShapeDtypeStruct((db, KV_A, l), MXU_DTYPE)
    return pl.pallas_call(
        functools.partial(_nsa_gather_kernel, pp=pp, n_steps=n_steps, rows=rows),
        grid_spec=pltpu.PrefetchScalarGridSpec(
            num_scalar_prefetch=1,
            grid=(db, n_steps + 1),
            in_specs=[pl.BlockSpec((1, width, rows), page_map(r)) for r in range(pp)]
            + [pl.BlockSpec((1, 2, pp * rows, KV_A), lambda b, s, pt: (b, 0, 0, 0)),
               pl.BlockSpec((1, 2 * KV_A, pp * rows), lambda b, s, pt: (b, 0, 0)),
               pl.BlockSpec((2, 2, 1, flat), lambda b, s, pt: (0, 0, 0, 0))],
            out_specs=[z_spec, z_spec, r_spec, r_spec],
            scratch_shapes=[pltpu.VMEM((2, pp * rows, KV_A), jnp.float32)]),
        out_shape=[z_shape, z_shape, r_shape, r_shape],
        compiler_params=pltpu.CompilerParams(
            dimension_semantics=("parallel", "arbitrary"), vmem_limit_bytes=VMEM_LIMIT),
        name="nsa_gather",
    )(page_table.reshape(-1), *([cache_t] * pp), tail_tok, tail_t, pe)


def _dsa_gather_kernel(pt_ref, *refs, pp, n_steps, rows):
    del pt_ref
    pages, tail_ref, (k_ref, v_ref, i_ref) = refs[:pp], refs[pp], refs[pp + 1:]
    is_tail = pl.program_id(1) == n_steps
    d = HEAD_DIM_B
    for r in range(pp):
        cs = slice(r * rows, (r + 1) * rows)
        x = jnp.where(is_tail, tail_ref[0, :, cs], pages[r][0])
        k_ref[0, :, cs] = x[:d].astype(k_ref.dtype)
        v_ref[0, :, cs] = x[d:2 * d].astype(v_ref.dtype)
        i_ref[0, :, cs] = x[2 * d:].astype(i_ref.dtype)


def dsa_gather(cache_t, page_table, new_rows):
    db, n_pages = page_table.shape
    width, rows = cache_t.shape[1:]
    pp = PAGES_PER_STEP
    n_steps, page_map = _page_maps(n_pages, pp)
    l = (n_steps + 1) * pp * rows
    tail_t = pad_rows(new_rows, pp * rows).swapaxes(1, 2)
    out_spec = lambda w: pl.BlockSpec((1, w, pp * rows), lambda b, s, pt: (b, 0, s))
    widths = (HEAD_DIM_B, HEAD_DIM_B, IDX_DIM)
    return pl.pallas_call(
        functools.partial(_dsa_gather_kernel, pp=pp, n_steps=n_steps, rows=rows),
        grid_spec=pltpu.PrefetchScalarGridSpec(
            num_scalar_prefetch=1,
            grid=(db, n_steps + 1),
            in_specs=[pl.BlockSpec((1, width, rows), page_map(r)) for r in range(pp)]
            + [pl.BlockSpec((1, width, pp * rows), lambda b, s, pt: (b, 0, 0))],
            out_specs=[out_spec(w) for w in widths]),
        out_shape=[jax.ShapeDtypeStruct((db, w, l), MXU_DTYPE) for w in widths],
        compiler_params=pltpu.CompilerParams(
            dimension_semantics=("parallel", "arbitrary"), vmem_limit_bytes=VMEM_LIMIT),
        name="dsa_gather",
    )(page_table.reshape(-1), *([cache_t] * pp), tail_t)


def _compress_kernel(zt_ref, zb_ref, pe_ref, w1t_ref, w1b_ref, b1_ref, w2_ref, o_ref, ab_ref, *, ncp):
    dot = functools.partial(jnp.dot, preferred_element_type=jnp.float32)
    ch = zb_ref.shape[2]
    at = dot(zt_ref[0, 0], w1t_ref[0])
    ab_ref[:ch] = dot(zb_ref[0, 0], w1b_ref[0])
    pe_rows = jnp.broadcast_to(pe_ref[0, 1], (SUBLANE, pe_ref.shape[3])).astype(zb_ref.dtype)
    ab_ref[ch:] = dot(pe_rows, w1b_ref[0])
    h = jax.nn.gelu(at[:ncp] + ab_ref[pl.ds(1, ncp), :] + b1_ref[0])
    o_ref[0, 0, :ncp] = dot(h.astype(w2_ref.dtype), w2_ref[0]).astype(o_ref.dtype)
    if o_ref.shape[2] > ncp:
        o_ref[0, 0, ncp:] = jnp.zeros((o_ref.shape[2] - ncp, o_ref.shape[3]), o_ref.dtype)


def compress(zt, zb, cmp_pe, w1, b1, w2, n_keys):
    ncp = _round_up(-(-n_keys // CMP_STRIDE), BF16_ROWS)
    ncl = ncp if ncp <= LANE else _round_up(ncp, LANE)
    _, nb, ch, kdim = zt.shape
    hid = w1.shape[-1]
    d = w2.shape[-1]
    assert ch + SUBLANE >= ncp + 1 and ch % SUBLANE == 0
    w1 = w1.astype(MXU_DTYPE)
    pe = cmp_pe.reshape(2, 2, 1, kdim).astype(jnp.float32)
    z_spec = pl.BlockSpec((1, 1, ch, kdim), lambda s, n: (s, n, 0, 0))
    return pl.pallas_call(
        functools.partial(_compress_kernel, ncp=ncp),
        grid=(2, nb),
        in_specs=[z_spec, z_spec,
                  pl.BlockSpec((1, 2, 1, kdim), lambda s, n: (s, 0, 0, 0)),
                  pl.BlockSpec((1, kdim, hid), lambda s, n: (s, 0, 0)),
                  pl.BlockSpec((1, kdim, hid), lambda s, n: (s, 1, 0)),
                  pl.BlockSpec((1, 1, hid), lambda s, n: (s, 0, 0)),
                  pl.BlockSpec((1, hid, d), lambda s, n: (s, 0, 0))],
        out_specs=pl.BlockSpec((1, 1, ncl, d), lambda s, n: (s, n, 0, 0)),
        out_shape=jax.ShapeDtypeStruct((2, nb, ncl, d), MXU_DTYPE),
        scratch_shapes=[pltpu.VMEM((ch + SUBLANE, hid), jnp.float32)],
        compiler_params=pltpu.CompilerParams(
            dimension_semantics=("parallel", "parallel"), vmem_limit_bytes=VMEM_LIMIT),
        name="compress",
    )(zt, zb, pe, w1, w1, b1.reshape(2, 1, hid).astype(jnp.float32), w2.astype(MXU_DTYPE))


def pad_rows(a, n):
    return jnp.pad(a, [(0, 0), (0, n - a.shape[1])] + [(0, 0)] * (a.ndim - 2))


def _rope_tables(pos, d):
    half = d // ROT_FRACTION // 2
    inv = ROPE_THETA ** (-jnp.arange(half, dtype=jnp.float32) / half)
    ang = pos.astype(jnp.float32)[:, None] * inv[None, :]
    lane = jnp.arange(LANE) % d
    cos = jnp.cos(ang)[:, lane % half]
    sin = jnp.sin(ang)[:, lane % half]
    one, zero = jnp.ones_like(cos), jnp.zeros_like(cos)
    c = jnp.where(lane < 2 * half, cos, one)
    sa = jnp.where((lane >= half) & (lane < 2 * half), sin, zero)
    sb = jnp.where(lane < half, -sin, zero)
    return jnp.stack([c, sa, sb])


def _rope(x, t_ref, half):
    c, sa, sb = t_ref[0], t_ref[1], t_ref[2]
    outs = []
    for j in range(x.shape[1] // LANE):
        xs = x[:, j * LANE:(j + 1) * LANE]
        outs.append(xs * c + pltpu.roll(xs, half, 1) * sa + pltpu.roll(xs, LANE - half, 1) * sb)
    return outs[0] if len(outs) == 1 else jnp.concatenate(outs, axis=1)


_QA = N_HEADS_A * HEAD_DIM_A
_KVA = 6 * KV_A
_QB = N_HEADS_B * HEAD_DIM_B
_KVB = 2 * HEAD_DIM_B
_QI = IDX_HEADS * IDX_DIM
_MISC = IDX_DIM + IDX_HEADS + 3 * N_HEADS_A
PREP_WIDTH = _QA + _KVA + _QB + _KVB + _QI + _MISC
assert _MISC == LANE


def _prep_kernel(x_ref, t64_ref, t128_ref, pe_ref, qc_ref, qr_ref, qb_ref, qi_ref, nsa_ref, win_ref, dsa_ref,
                 ks_ref, vs_ref, kw_ref, vw_ref, kb_ref, vb_ref, ki_ref, wi_ref, gate_ref, *z_refs, tm):
    x = x_ref[...]
    o = 0
    qa = x[:, o:o + _QA]; o += _QA
    kva = x[:, o:o + _KVA]; o += _KVA
    qb = x[:, o:o + _QB]; o += _QB
    kvb = x[:, o:o + _KVB]; o += _KVB
    qi = x[:, o:o + _QI]; o += _QI
    misc = x[:, o:o + _MISC]
    h64, h128 = HEAD_DIM_A // ROT_FRACTION // 2, HEAD_DIM_B // ROT_FRACTION // 2
    sec = lambda i: kva[:, i * KV_A:(i + 1) * KV_A]
    mx = lambda a: a.astype(qc_ref.dtype)
    qc_ref[...] = mx(qa * HEAD_DIM_A ** -0.5)
    qr_ref[...] = mx(_rope(qa, t64_ref, h64) * HEAD_DIM_A ** -0.5)
    k_slc, k_win = _rope(sec(2), t64_ref, h64), _rope(sec(4), t64_ref, h64)
    nsa_ref[:, :2 * KV_A] = kva[:, :2 * KV_A]
    nsa_ref[:, 2 * KV_A:3 * KV_A] = k_slc
    nsa_ref[:, 3 * KV_A:] = sec(3)
    win_ref[:, :KV_A] = k_win
    win_ref[:, KV_A:] = sec(5)
    ks_ref[...], vs_ref[...], kw_ref[...], vw_ref[...] = mx(k_slc), mx(sec(3)), mx(k_win), mx(sec(5))
    qb_ref[...] = mx(_rope(qb, t128_ref, h128))
    k_b, v_b = _rope(kvb[:, :HEAD_DIM_B], t128_ref, h128), kvb[:, HEAD_DIM_B:]
    k_idx = _rope(misc, t64_ref, h64)[:, :IDX_DIM]
    dsa_ref[:, :HEAD_DIM_B] = k_b
    dsa_ref[:, HEAD_DIM_B:2 * HEAD_DIM_B] = v_b
    dsa_ref[:, 2 * HEAD_DIM_B:] = k_idx
    kb_ref[...], vb_ref[...], ki_ref[...] = mx(k_b), mx(v_b), mx(k_idx)
    qi_ref[...] = mx(_rope(qi, t64_ref, h64) * IDX_DIM ** -0.5)
    wi_ref[...] = misc[:, IDX_DIM:IDX_DIM + IDX_HEADS] * IDX_HEADS ** -0.5
    gate_ref[...] = jax.nn.sigmoid(misc[:, IDX_DIM + IDX_HEADS:])
    if z_refs:
        zt_ref, zb_ref, tok_ref = z_refs
        for s_ in range(2):
            tok_ref[s_] = sec(s_)
            for g, zg in enumerate(_chunk_rows(tok_ref, s_, 0, tm // CMP_STRIDE)):
                zt_ref[s_, 0, g] = (zg + pe_ref[s_, 0]).astype(zt_ref.dtype)
                zb_ref[s_, 0, g] = (zg + pe_ref[s_, 1]).astype(zb_ref.dtype)


def prep(proj, pos, cmp_pe, with_chunks):
    b, t, width = proj.shape
    assert width == PREP_WIDTH
    m = b * t
    tm = 256 if t % 256 == 0 else m
    assert m % tm == 0 and t % tm in (0, t)
    nt = max(t // tm, 1)
    flat = CMP_STRIDE * HEAD_DIM_A
    names = ['qc', 'qr', 'qb', 'qi', 'nsa_rows', 'win_rows', 'dsa_rows', 'ks', 'vs', 'kw', 'vw', 'kb', 'vb', 'ki',
             'wi', 'gates']
    widths = [_QA, _QA, _QB, _QI, NSA_SECTIONS * KV_A, 2 * KV_A, DSA_ROW, KV_A, KV_A, KV_A, KV_A, HEAD_DIM_B,
              HEAD_DIM_B, IDX_DIM, IDX_HEADS, 3 * N_HEADS_A]
    dtypes = [MXU_DTYPE] * 4 + [jnp.float32] * 3 + [MXU_DTYPE] * 7 + [jnp.float32] * 2
    row = lambda w: pl.BlockSpec((tm, w), lambda i: (i, 0))
    out_specs = [row(w) for w in widths]
    out_shape = [jax.ShapeDtypeStruct((m, w), dt) for w, dt in zip(widths, dtypes)]
    scratch = []
    if with_chunks:
        assert tm % CMP_STRIDE == 0 and t % tm == 0
        z_spec = pl.BlockSpec((2, 1, KV_GROUPS_A, tm // CMP_STRIDE, flat), lambda i: (0, i // nt, 0, i % nt, 0))
        z_shape = jax.ShapeDtypeStruct((2, b, KV_GROUPS_A, t // CMP_STRIDE, flat), MXU_DTYPE)
        out_specs += [z_spec, z_spec]
        out_shape += [z_shape, z_shape]
        names += ['zt', 'zb']
        scratch = [pltpu.VMEM((2, tm, KV_A), jnp.float32)]
    t_spec = pl.BlockSpec((3, tm, LANE), lambda i: (0, i, 0))
    outs = pl.pallas_call(
        functools.partial(_prep_kernel, tm=tm),
        grid=(m // tm,),
        in_specs=[row(width), t_spec, t_spec, pl.BlockSpec((2, 2, 1, flat), lambda i: (0, 0, 0, 0))],
        out_specs=out_specs,
        out_shape=out_shape,
        scratch_shapes=scratch,
        compiler_params=pltpu.CompilerParams(dimension_semantics=("parallel",), vmem_limit_bytes=VMEM_LIMIT),
        name="prep",
    )(proj.reshape(m, width), _rope_tables(pos, HEAD_DIM_A), _rope_tables(pos, HEAD_DIM_B),
      cmp_pe.reshape(2, 2, 1, flat).astype(jnp.float32))
    return {n: (o if o.ndim > 2 else o.reshape(b, t, o.shape[-1])) for n, o in zip(names, outs)}


def project(x, g, w_in, sizes):
    starts = [sum(sizes[:i]) for i in range(len(sizes))]
    qa, kva, ga, qb, kvb, qi, ki, wi, gm = [slice(o, o + n) for o, n in zip(starts, sizes)]
    w_bf = w_in.astype(MXU_DTYPE)
    w_prep = jnp.concatenate([w_bf[:, c] for c in (qa, kva, qb, kvb, qi, ki, wi, ga)], axis=1)
    return norm_matmul(x, g, w_prep), norm_matmul(x, g, w_bf[:, gm])


def mixer_prompt(x, g, w_in, cmp_pe, cmp_w1, cmp_b1, cmp_w2, sizes):
    b, s, _ = x.shape
    proj, gm = project(x, g, w_in, sizes)
    p = prep(proj, jnp.tile(jnp.arange(s, dtype=jnp.int32), b), cmp_pe, True)
    merge_bg = lambda a: a.reshape((2, b * KV_GROUPS_A) + a.shape[3:])
    kvc = compress(merge_bg(p['zt']), merge_bg(p['zb']), cmp_pe, cmp_w1, cmp_b1, cmp_w2, s)
    kvc = kvc.reshape(2, b, KV_GROUPS_A, kvc.shape[2], HEAD_DIM_A)
    o_nsa = nsa_attention(p['qc'], p['qr'], p['gates'], kvc, p['ks'], p['vs'], p['kw'], p['vw'],
                          tq=min(TQ, s), qpos0=0, wbase=0, n_keys=s)
    o_dsa = dsa_attention(p['qb'], p['qi'], p['wi'], p['kb'], p['vb'], p['ki'],
                          tq=min(TQ, s), qpos0=0, n_keep=min(DSA_TOPK, s // 4))
    nsa_rows = p['nsa_rows'].reshape(b, s, NSA_SECTIONS, KV_GROUPS_A, HEAD_DIM_A)
    win_state = p['win_rows'].reshape(b, s, 2, KV_GROUPS_A, HEAD_DIM_A)[:, -min(WINDOW, s):]
    return (o_nsa, o_dsa, gm), nsa_rows, win_state, p['dsa_rows']


def mixer_sample(x, g, cache_nsa, win_buf, cache_dsa, page_table, w_in, cmp_pe, cmp_w1, cmp_b1, cmp_w2, sizes):
    b, t, _ = x.shape
    page = cache_nsa.shape[1]
    past_len = page_table.shape[1] * page
    n_keys = past_len + t
    proj, gm = project(x, g, w_in, sizes)
    p = prep(proj, jnp.tile(past_len + jnp.arange(t, dtype=jnp.int32), b), cmp_pe, False)
    pad = lambda a: pad_rows(a, TQ_STEP)

    cache_nsa_t = jnp.transpose(cache_nsa, (0, 2, 3, 4, 1)).reshape(cache_nsa.shape[0], NSA_SECTIONS * KV_A, page)
    zt, zb, ks, vs = nsa_gather(cache_nsa_t, page_table, p['nsa_rows'], cmp_pe)
    merge_bg = lambda a: a.reshape((2, b * KV_GROUPS_A) + a.shape[3:])
    kvc = compress(merge_bg(zt), merge_bg(zb), cmp_pe, cmp_w1, cmp_b1, cmp_w2, n_keys)
    kvc = kvc.reshape(2, b, KV_GROUPS_A, kvc.shape[2], HEAD_DIM_A)
    w_len = win_buf.shape[1]
    win_new = p['win_rows'].reshape(b, t, 2, KV_GROUPS_A, HEAD_DIM_A)
    win_all = jnp.concatenate([win_buf, win_new], axis=1)
    win_pad = pad_rows(win_all, w_len + WIN_CHUNK).astype(MXU_DTYPE)
    kw = win_pad[:, :, 0].reshape(b, w_len + WIN_CHUNK, KV_A)
    vw = win_pad[:, :, 1].reshape(b, w_len + WIN_CHUNK, KV_A)
    o_nsa = nsa_attention(pad(p['qc']), pad(p['qr']), pad(p['gates']), kvc, ks, vs, kw, vw,
                          tq=TQ_STEP, qpos0=past_len, wbase=past_len - w_len, n_keys=n_keys, kv_t=True)

    kb_, vb, ki = dsa_gather(jnp.swapaxes(cache_dsa, 1, 2), page_table, p['dsa_rows'])
    o_dsa = dsa_attention(pad(p['qb']), pad(p['qi']), pad(p['wi']), kb_, vb, ki, tq=TQ_STEP, qpos0=past_len,
                          n_keep=min(DSA_TOPK, n_keys // 4), kv_t=True)
    nsa_rows = p['nsa_rows'].reshape(b, t, NSA_SECTIONS, KV_GROUPS_A, HEAD_DIM_A)
    return (o_nsa, o_dsa, pad(gm)), nsa_rows, win_all[:, -w_len:], p['dsa_rows']


def kernel(x_prompt, x_sample, mem_prompt, cache_nsa_kv, state_nsa_win, cache_dsa_kv, cache_mem_kv, state_conv,
           page_table, norm_g, w_in, cmp_pe, cmp_w1, cmp_b1, cmp_w2, w_out_a, w_out_b, w_out, w_mem_q, w_mem_kv,
           w_mem_out, w_up, conv_w, conv_b, w_down, final_g):
    depth = w_in.shape[0]
    d_model = x_prompt.shape[-1]
    d_ff = w_down.shape[1]
    assert CONV_WIDTH == 3
    sizes = (N_HEADS_A * HEAD_DIM_A, 6 * KV_A, 3 * N_HEADS_A, N_HEADS_B * HEAD_DIM_B, 2 * HEAD_DIM_B,
             IDX_HEADS * IDX_DIM, IDX_DIM, IDX_HEADS, 2 * d_model)
    xp, xs = x_prompt, pad_rows(x_sample, TQ_STEP)
    t_step = x_sample.shape[1]
    nsa_p, nsa_s, win_p, win_s, dsa_p, dsa_s, mem_p, conv_p, conv_s = [], [], [], [], [], [], [], [], []
    for l in range(depth):
        assert l == depth - 1, "the fused FFN epilogue applies the final norm"
        branches_p, a, bwin, c = mixer_prompt(xp, norm_g[l, 0], w_in[l], cmp_pe[l], cmp_w1[l], cmp_b1[l], cmp_w2[l],
                                              sizes)
        nsa_p.append(a); win_p.append(bwin); dsa_p.append(c)
        branches_s, a, bwin, c = mixer_sample(xs[:, :t_step], norm_g[l, 0], cache_nsa_kv[l], state_nsa_win[l],
                                              cache_dsa_kv[l], page_table, w_in[l], cmp_pe[l], cmp_w1[l], cmp_b1[l],
                                              cmp_w2[l], sizes)
        nsa_s.append(a); win_s.append(bwin); dsa_s.append(c)
        kv_p = norm_matmul(mem_prompt, norm_g[l, 2], w_mem_kv[l])
        mem_p.append(kv_p.reshape(kv_p.shape[:2] + (2, MEM_HEADS, MEM_HEAD_DIM)))
        kv_s = cache_mem_kv[l].reshape(cache_mem_kv.shape[1:3] + (-1,))
        weights = (norm_g[l, 1], norm_g[l, 3], final_g, w_out_a[l], w_out_b[l], w_out[l], w_mem_q[l], w_mem_out[l],
                   w_up[l], conv_w[l], conv_b[l], w_down[l])
        xp, cp = dense_tail(xp, *branches_p, kv_p, jnp.zeros((xp.shape[0], CONV_WIDTH - 1, 2 * d_ff), xp.dtype),
                            xp.shape[1], *weights)
        xs, cs = dense_tail(xs, *branches_s, kv_s, state_conv[l], t_step, *weights)
        conv_p.append(cp); conv_s.append(cs)
    y_prompt, y_sample = xp, xs[:, :t_step]
    return (y_prompt, y_sample, jnp.stack(nsa_p), jnp.stack(nsa_s), jnp.stack(win_p), jnp.stack(win_s),
            jnp.stack(dsa_p), jnp.stack(dsa_s), jnp.stack(mem_p), jnp.stack(conv_p), jnp.stack(conv_s))
```

```python
import functools

import jax
import jax.numpy as jnp
from jax import lax
from jax.experimental import pallas as pl
from jax.experimental.pallas import tpu as pltpu

N_HEADS_A = 16
HEAD_DIM_A = 64
KV_GROUPS_A = 2
CMP_BLOCK = 32
CMP_STRIDE = 16
SLC_BLOCK = 64
N_SELECT = 16
WINDOW = 512
N_HEADS_B = 8
HEAD_DIM_B = 128
IDX_HEADS = 16
IDX_DIM = 64
DSA_TOPK = 256
MEM_HEADS = 4
MEM_HEAD_DIM = 128
CONV_WIDTH = 3
ROPE_THETA = 500000.0
ROT_FRACTION = 4
EPS = 1e-6
KV_A = KV_GROUPS_A * HEAD_DIM_A
CMP_PER_SLC = SLC_BLOCK // CMP_STRIDE
HPG_A = N_HEADS_A // KV_GROUPS_A
DSA_ROW = 2 * HEAD_DIM_B + IDX_DIM
NSA_SECTIONS = 4

LANE = 128
SUBLANE = 8
BF16_ROWS = 16
VMEM_LIMIT = 48 * 1024 * 1024

NEG_INF = float('-inf')
POS_INF = float('inf')
MXU_DTYPE = jnp.bfloat16
TQ = 128
TQ_STEP = BF16_ROWS
KEY_CHUNK = 512
WIN_CHUNK = 128
PAGES_PER_STEP = 8
BISECT_ITERS = 40
SLC_SHIFT = SLC_BLOCK.bit_length() - 1
CMP_PER_SLC_SHIFT = CMP_PER_SLC.bit_length() - 1


def _round_up(n, m):
    return -(-n // m) * m


def _tile(n, cap):
    if n <= cap:
        return n
    best = None
    for t in range(LANE, cap + 1, LANE):
        if n % t == 0:
            best = t
    assert best is not None, (n, cap)
    return best


def _mm_kernel(x_ref, w_ref, o_ref, acc_ref):
    @pl.when(pl.program_id(2) == 0)
    def _():
        acc_ref[...] = jnp.zeros_like(acc_ref)

    acc_ref[...] += jnp.dot(x_ref[...], w_ref[...], preferred_element_type=jnp.float32)

    @pl.when(pl.program_id(2) == pl.num_programs(2) - 1)
    def _():
        o_ref[...] = acc_ref[...].astype(o_ref.dtype)


def matmul(x, w, out_dtype=jnp.float32):
    lead = x.shape[:-1]
    k = x.shape[-1]
    n = w.shape[-1]
    x2 = x.reshape(-1, k).astype(MXU_DTYPE)
    w2 = w.astype(MXU_DTYPE)
    m0 = x2.shape[0]
    tm = 512 if m0 >= 512 else _round_up(m0, BF16_ROWS)
    m = _round_up(m0, tm)
    if m != m0:
        x2 = jnp.pad(x2, ((0, m - m0), (0, 0)))
    tn = _tile(n, 1024)
    tk = _tile(k, 2048)
    out = pl.pallas_call(
        _mm_kernel,
        grid=(m // tm, n // tn, k // tk),
        in_specs=[pl.BlockSpec((tm, tk), lambda i, j, l: (i, l)),
                  pl.BlockSpec((tk, tn), lambda i, j, l: (l, j))],
        out_specs=pl.BlockSpec((tm, tn), lambda i, j, l: (i, j)),
        out_shape=jax.ShapeDtypeStruct((m, n), out_dtype),
        scratch_shapes=[pltpu.VMEM((tm, tn), jnp.float32)],
        compiler_params=pltpu.CompilerParams(
            dimension_semantics=("parallel", "parallel", "arbitrary"),
            vmem_limit_bytes=VMEM_LIMIT),
        name="matmul",
    )(x2, w2)
    return out[:m0].reshape(lead + (n,))


def _rms(x, g):
    return x * lax.rsqrt(jnp.mean(x * x, axis=-1, keepdims=True) + EPS) * g


def _norm_mm_kernel(x_ref, g_ref, w_ref, o_ref, xn_ref):
    @pl.when(pl.program_id(1) == 0)
    def _():
        xn_ref[...] = _rms(x_ref[...], g_ref[...]).astype(xn_ref.dtype)

    o_ref[...] = jnp.dot(xn_ref[...], w_ref[...], preferred_element_type=jnp.float32).astype(o_ref.dtype)


def norm_matmul(x, g, w):
    lead = x.shape[:-1]
    k = x.shape[-1]
    n = w.shape[-1]
    x2 = x.reshape(-1, k)
    m = x2.shape[0]
    tm = 512 if m % 512 == 0 else m
    tn = _tile(n, 1536)
    out = pl.pallas_call(
        _norm_mm_kernel,
        grid=(m // tm, n // tn),
        in_specs=[pl.BlockSpec((tm, k), lambda i, j: (i, 0)),
                  pl.BlockSpec((1, k), lambda i, j: (0, 0)),
                  pl.BlockSpec((k, tn), lambda i, j: (0, j))],
        out_specs=pl.BlockSpec((tm, tn), lambda i, j: (i, j)),
        out_shape=jax.ShapeDtypeStruct((m, n), jnp.float32),
        scratch_shapes=[pltpu.VMEM((tm, k), MXU_DTYPE)],
        compiler_params=pltpu.CompilerParams(
            dimension_semantics=("parallel", "arbitrary"), vmem_limit_bytes=VMEM_LIMIT),
        name="norm_matmul",
    )(x2, g.reshape(1, k).astype(jnp.float32), w.astype(MXU_DTYPE))
    return out.reshape(lead + (n,))


def _gated_merge_kernel(oa_ref, ob_ref, wa_ref, wb_ref, ga_ref, gb_ref, o_ref):
    dot = functools.partial(jnp.dot, preferred_element_type=jnp.float32)
    ya = dot(oa_ref[...], wa_ref[...])
    yb = dot(ob_ref[...], wb_ref[...])
    o_ref[...] = (jax.nn.sigmoid(ga_ref[...]) * ya + jax.nn.sigmoid(gb_ref[...]) * yb).astype(o_ref.dtype)


def gated_merge(o_nsa, o_dsa, gm, w_oa, w_ob):
    m, ka = o_nsa.shape
    kb = o_dsa.shape[1]
    n = w_oa.shape[1]
    tm = 512 if m % 512 == 0 else m
    tn = _tile(n, 1024)
    nb = n // tn
    return pl.pallas_call(
        _gated_merge_kernel,
        grid=(m // tm, nb),
        in_specs=[pl.BlockSpec((tm, ka), lambda i, j: (i, 0)),
                  pl.BlockSpec((tm, kb), lambda i, j: (i, 0)),
                  pl.BlockSpec((ka, tn), lambda i, j: (0, j)),
                  pl.BlockSpec((kb, tn), lambda i, j: (0, j)),
                  pl.BlockSpec((tm, tn), lambda i, j: (i, j)),
                  pl.BlockSpec((tm, tn), lambda i, j: (i, j + nb))],
        out_specs=pl.BlockSpec((tm, tn), lambda i, j: (i, j)),
        out_shape=jax.ShapeDtypeStruct((m, n), MXU_DTYPE),
        compiler_params=pltpu.CompilerParams(
            dimension_semantics=("parallel", "parallel"), vmem_limit_bytes=VMEM_LIMIT),
        name="gated_merge",
    )(o_nsa, o_dsa, w_oa.astype(MXU_DTYPE), w_ob.astype(MXU_DTYPE), gm, gm)


def _out_proj_kernel(z_ref, w_ref, x_ref, o_ref):
    o_ref[...] = x_ref[...] + jnp.dot(z_ref[...], w_ref[...], preferred_element_type=jnp.float32)


def out_proj_residual(z, w, x):
    m, k = z.shape
    n = w.shape[1]
    tm = 256 if m % 256 == 0 else m
    return pl.pallas_call(
        _out_proj_kernel,
        grid=(m // tm,),
        in_specs=[pl.BlockSpec((tm, k), lambda i: (i, 0)),
                  pl.BlockSpec((k, n), lambda i: (0, 0)),
                  pl.BlockSpec((tm, n), lambda i: (i, 0))],
        out_specs=pl.BlockSpec((tm, n), lambda i: (i, 0)),
        out_shape=jax.ShapeDtypeStruct((m, n), jnp.float32),
        compiler_params=pltpu.CompilerParams(dimension_semantics=("parallel",), vmem_limit_bytes=VMEM_LIMIT),
        name="out_proj_residual",
    )(z, w.astype(MXU_DTYPE), x)


def _mem_block_kernel(x_ref, g1_ref, g2_ref, wq_ref, kv_ref, wo_ref, x_out_ref, xn_out_ref):
    d = MEM_HEAD_DIM
    hd = MEM_HEADS * d
    x = x_ref[0]
    xn = _rms(x, g1_ref[...]).astype(wq_ref.dtype)
    q = jnp.dot(xn, wq_ref[...], preferred_element_type=jnp.float32).astype(wq_ref.dtype)
    outs = []
    for h in range(MEM_HEADS):
        k = kv_ref[0, :, h * d:(h + 1) * d].astype(wq_ref.dtype)
        v = kv_ref[0, :, hd + h * d:hd + (h + 1) * d].astype(wq_ref.dtype)
        s = _dot_nt(q[:, h * d:(h + 1) * d], k) * d ** -0.5
        e = jnp.exp(s - jnp.max(s, axis=1, keepdims=True))
        p = e / jnp.sum(e, axis=1, keepdims=True)
        outs.append(jnp.dot(p.astype(v.dtype), v, preferred_element_type=jnp.float32))
    o = jnp.concatenate(outs, axis=1).astype(wo_ref.dtype)
    x2 = x + jnp.dot(o, wo_ref[...], preferred_element_type=jnp.float32)
    x_out_ref[0] = x2
    xn_out_ref[0] = _rms(x2, g2_ref[...]).astype(xn_out_ref.dtype)


def mem_block(x, g1, g2, w_q, kv, w_o):
    b, t, dm = x.shape
    mt, kvw = kv.shape[1:]
    hd = w_q.shape[1]
    tm = 256 if t % 256 == 0 else t
    row_spec = pl.BlockSpec((1, tm, dm), lambda bi, i: (bi, i, 0))
    g_spec = pl.BlockSpec((1, dm), lambda bi, i: (0, 0))
    return pl.pallas_call(
        _mem_block_kernel,
        grid=(b, t // tm),
        in_specs=[row_spec, g_spec, g_spec,
                  pl.BlockSpec((dm, hd), lambda bi, i: (0, 0)),
                  pl.BlockSpec((1, mt, kvw), lambda bi, i: (bi, 0, 0)),
                  pl.BlockSpec((hd, dm), lambda bi, i: (0, 0))],
        out_specs=[row_spec, row_spec],
        out_shape=[jax.ShapeDtypeStruct((b, t, dm), jnp.float32), jax.ShapeDtypeStruct((b, t, dm), MXU_DTYPE)],
        compiler_params=pltpu.CompilerParams(
            dimension_semantics=("parallel", "parallel"), vmem_limit_bytes=VMEM_LIMIT),
        name="mem_block",
    )(x, g1.reshape(1, dm).astype(jnp.float32), g2.reshape(1, dm).astype(jnp.float32), w_q.astype(MXU_DTYPE),
      kv, w_o.astype(MXU_DTYPE))


def _ffn_up_kernel(x_ref, halo_ref, wg_ref, wu_ref, cwg_ref, cwu_ref, cbg_ref, cbu_ref, pg_ref, pu_ref,
                   h_ref, sg_ref, su_ref, *, nb, tm, t_real):
    i = pl.program_id(2)
    n_halo = halo_ref.shape[1]
    tn = wg_ref.shape[1]
    rows = nb * tm
    x = x_ref[...].reshape(rows, x_ref.shape[2])
    if nb == 1:
        x = jnp.concatenate([halo_ref[0], x], axis=0)
    row = _iota((nb, tm, 1), 1).reshape(rows, 1)
    last = (t_real - 1) // tm
    r_last = (t_real - 1) % tm
    spread = lambda p: jnp.broadcast_to(p, (nb, tm, tn)).reshape(rows, tn)

    def branch(w_ref, cw_ref, cb_ref, p_ref, s_ref):
        u = jnp.dot(x, w_ref[...], preferred_element_type=jnp.float32)
        p0, p1 = p_ref[:, 0:1, :], p_ref[:, 1:2, :]
        if nb == 1:
            uh, u = u[:n_halo], u[n_halo:]
            p0 = jnp.where(i == 0, p0, uh[n_halo - 2:n_halo - 1][None])
            p1 = jnp.where(i == 0, p1, uh[n_halo - 1:n_halo][None])
        p0, p1 = spread(p0), spread(p1)
        u1 = jnp.where(row == 0, p1, pltpu.roll(u, 1, 0))
        u2 = jnp.where(row == 0, p0, jnp.where(row == 1, p1, pltpu.roll(u, 2, 0)))

        @pl.when(i == last)
        def _():
            s_ref[...] = u.reshape(nb, tm, tn)[:, r_last - 1:r_last + 1, :]

        return cb_ref[...] + u2 * cw_ref[0:1, :] + u1 * cw_ref[1:2, :] + u * cw_ref[2:3, :]

    gate = branch(wg_ref, cwg_ref, cbg_ref, pg_ref, sg_ref)
    up = branch(wu_ref, cwu_ref, cbu_ref, pu_ref, su_ref)
    h_ref[...] = (jax.nn.silu(gate) * up).reshape(nb, tm, tn).astype(h_ref.dtype)


def ffn_up(xn, prev, w_up, conv_w, conv_b, t_real):
    b, t, dm = xn.shape
    f2 = w_up.shape[1]
    f = f2 // 2
    tm = 512 if t % 512 == 0 else t
    nb = 1 if t > tm else max(1, min(b, 512 // tm))
    while b % nb:
        nb -= 1
    tn = _tile(f, 512)
    nf = f // tn
    halo = min(BF16_ROWS, tm)
    hpt = tm // halo
    assert t_real >= 2 and (t_real - 1) % tm >= 1 and tm % SUBLANE == 0
    w_up = w_up.astype(MXU_DTYPE)
    conv_b = conv_b.reshape(1, f2)
    col = lambda off: (lambda j, bi, i: (0, j + off))
    st = lambda off: (lambda j, bi, i: (bi, 0, j + off))
    specs = [pl.BlockSpec((nb, tm, dm), lambda j, bi, i: (bi, i, 0)),
             pl.BlockSpec((1, halo, dm), lambda j, bi, i: (bi * nb, jnp.maximum(i * hpt - 1, 0), 0)),
             pl.BlockSpec((dm, tn), col(0)), pl.BlockSpec((dm, tn), col(nf)),
             pl.BlockSpec((CONV_WIDTH, tn), col(0)), pl.BlockSpec((CONV_WIDTH, tn), col(nf)),
             pl.BlockSpec((1, tn), col(0)), pl.BlockSpec((1, tn), col(nf)),
             pl.BlockSpec((nb, 2, tn), st(0)), pl.BlockSpec((nb, 2, tn), st(nf))]
    h, sg, su = pl.pallas_call(
        functools.partial(_ffn_up_kernel, nb=nb, tm=tm, t_real=t_real),
        grid=(nf, b // nb, t // tm),
        in_specs=specs,
        out_specs=[pl.BlockSpec((nb, tm, tn), lambda j, bi, i: (bi, i, j)),
                   pl.BlockSpec((nb, 2, tn), st(0)), pl.BlockSpec((nb, 2, tn), st(0))],
        out_shape=[jax.ShapeDtypeStruct((b, t, f), MXU_DTYPE), jax.ShapeDtypeStruct((b, 2, f), jnp.float32),
                   jax.ShapeDtypeStruct((b, 2, f), jnp.float32)],
        compiler_params=pltpu.CompilerParams(
            dimension_semantics=("parallel", "parallel", "arbitrary"), vmem_limit_bytes=VMEM_LIMIT),
        name="ffn_up",
    )(xn, xn, w_up, w_up, conv_w, conv_w, conv_b, conv_b, prev, prev)
    return h, jnp.concatenate([sg, su], axis=-1)


def _ffn_down_kernel(h_ref, w_ref, x_ref, g_ref, o_ref, acc_ref):
    @pl.when(pl.program_id(1) == 0)
    def _():
        acc_ref[...] = x_ref[...]

    acc_ref[...] += jnp.dot(h_ref[...], w_ref[...], preferred_element_type=jnp.float32)

    @pl.when(pl.program_id(1) == pl.num_programs(1) - 1)
    def _():
        o_ref[...] = _rms(acc_ref[...], g_ref[...])


def ffn_down_norm(h, w_down, x, g):
    m, f = h.shape
    dm = w_down.shape[1]
    tm = 512 if m % 512 == 0 else m
    tk = _tile(f, 2048)
    return pl.pallas_call(
        _ffn_down_kernel,
        grid=(m // tm, f // tk),
        in_specs=[pl.BlockSpec((tm, tk), lambda i, l: (i, l)),
                  pl.BlockSpec((tk, dm), lambda i, l: (l, 0)),
                  pl.BlockSpec((tm, dm), lambda i, l: (i, 0)),
                  pl.BlockSpec((1, dm), lambda i, l: (0, 0))],
        out_specs=pl.BlockSpec((tm, dm), lambda i, l: (i, 0)),
        out_shape=jax.ShapeDtypeStruct((m, dm), jnp.float32),
        scratch_shapes=[pltpu.VMEM((tm, dm), jnp.float32)],
        compiler_params=pltpu.CompilerParams(
            dimension_semantics=("parallel", "arbitrary"), vmem_limit_bytes=VMEM_LIMIT),
        name="ffn_down_norm",
    )(h, w_down.astype(MXU_DTYPE), x, g.reshape(1, dm).astype(jnp.float32))


def dense_tail(x, o_nsa, o_dsa, gm, kv_mem, prev_u, t_real, g_mem, g_ffn, g_final, w_oa, w_ob, w_o, w_mq, w_mo,
               w_up, conv_w, conv_b, w_down):
    b, t, dm = x.shape
    rows = lambda a: a.reshape(b * t, a.shape[-1])
    z = gated_merge(rows(o_nsa), rows(o_dsa), rows(gm), w_oa, w_ob)
    x1 = out_proj_residual(z, w_o, rows(x)).reshape(b, t, dm)
    x2, xn2 = mem_block(x1, g_mem, g_ffn, w_mq, kv_mem, w_mo)
    h, state = ffn_up(xn2, prev_u, w_up, conv_w, conv_b, t_real)
    y = ffn_down_norm(rows(h), w_down, rows(x2), g_final)
    return y.reshape(b, t, dm), state


def _dot_nt(a, b):
    return lax.dot_general(a, b, (((1,), (1,)), ((), ())), preferred_element_type=jnp.float32)


def _iota(shape, dim):
    return lax.broadcasted_iota(jnp.int32, shape, dim)


def _flash_init(rows, d):
    return (jnp.full((rows, 1), NEG_INF, jnp.float32), jnp.zeros((rows, 1), jnp.float32),
            jnp.zeros((rows, d), jnp.float32))


def _flash_step(carry, q, k, v, madd, nh, scale=None, kv_t=False):
    m, l, acc = carry
    s = jnp.dot(q, k, preferred_element_type=jnp.float32) if kv_t else _dot_nt(q, k)
    if scale is not None:
        s = s * scale
    r, kb = s.shape
    s = (s.reshape(nh, r // nh, kb) + madd[None]).reshape(r, kb)
    m_new = jnp.maximum(m, jnp.max(s, axis=1, keepdims=True))
    m_safe = jnp.where(m_new == NEG_INF, 0.0, m_new)
    p = jnp.exp(s - m_safe)
    alpha = jnp.exp(m - m_safe)
    l = alpha * l + jnp.sum(p, axis=1, keepdims=True)
    pv = _dot_nt(p.astype(v.dtype), v) if kv_t else jnp.dot(p.astype(v.dtype), v, preferred_element_type=jnp.float32)
    return m_new, l, alpha * acc + pv


def _flash_finish(carry):
    _, l, acc = carry
    return acc / jnp.maximum(l, 1e-30)


def _split_dot(x, m01):
    hi = x.astype(jnp.bfloat16)
    r1 = x - hi.astype(jnp.float32)
    mid = r1.astype(jnp.bfloat16)
    lo = (r1 - mid.astype(jnp.float32)).astype(jnp.bfloat16)
    dot = functools.partial(jnp.dot, preferred_element_type=jnp.float32)
    return dot(hi, m01) + dot(mid, m01) + dot(lo, m01)


def _nsa_group(g, i, qc_ref, qr_ref, gate_ref, kc_ref, vc_ref, ks_ref, vs_ref, kw_ref, vw_ref, mask_ref,
               *, tq, kb, qpos0, wbase, ns, n_sel, kv_t):
    nh, d = HPG_A, HEAD_DIM_A
    rows = nh * tq
    nc = kc_ref.shape[3]
    nsp = ns if ns <= LANE else _round_up(ns, LANE)
    t0 = qpos0 + i * tq
    nj = (t0 + tq - 1) // kb + 1
    cols = slice(g * d, (g + 1) * d)

    def stack_heads(q_ref):
        return jnp.concatenate([q_ref[0, :, (g * nh + h) * d:(g * nh + h + 1) * d] for h in range(nh)], axis=0)

    qc = stack_heads(qc_ref)
    qr = stack_heads(qr_ref)
    tpos = t0 + _iota((tq, 1), 0)

    blk_last = _iota((1, nc), 1) * CMP_STRIDE + (CMP_BLOCK - 1)
    madd_c = jnp.where(blk_last <= tpos, 0.0, NEG_INF)
    s = _dot_nt(qc, kc_ref[0, 0, g]).reshape(nh, tq, nc) + madd_c[None]
    m = jnp.max(s, axis=2, keepdims=True)
    m = jnp.where(m == NEG_INF, 0.0, m)
    e = jnp.exp(s - m)
    p = e / jnp.maximum(jnp.sum(e, axis=2, keepdims=True), 1e-30)
    o_cmp = jnp.dot(p.reshape(rows, nc).astype(vc_ref.dtype), vc_ref[0, 0, g], preferred_element_type=jnp.float32)

    imp = jnp.sum(p, axis=0)
    c_id = _iota((nc, nsp), 0)
    m_id = _iota((nc, nsp), 1)
    overlap = (jnp.right_shift(c_id, CMP_PER_SLC_SHIFT) == m_id) | (c_id == m_id * CMP_PER_SLC - 1)
    score = _split_dot(imp, overlap.astype(jnp.bfloat16))
    blk = _iota((1, nsp), 1)
    cur = jnp.right_shift(tpos, SLC_SHIFT)
    forced = (blk == 0) | (blk == cur) | (blk == cur - 1)
    sc = jnp.where(forced, POS_INF, jnp.where(blk * SLC_BLOCK <= tpos, score, NEG_INF))
    rank = jnp.zeros((tq, nsp), jnp.float32)
    for mp in range(ns):
        col = sc[:, mp:mp + 1]
        beats = (col > sc) | ((col == sc) & (blk > mp))
        rank = rank + jnp.where(beats, 1.0, 0.0)
    sel = jnp.where(rank < n_sel, 1.0, 0.0).astype(jnp.bfloat16)

    def make_mask(j, _):
        kpos = j * kb + _iota((1, kb), 1)
        expand = (jnp.right_shift(j * kb + _iota((nsp, kb), 1), SLC_SHIFT) == _iota((nsp, kb), 0))
        hit = jnp.dot(sel, expand.astype(jnp.bfloat16), preferred_element_type=jnp.float32)
        mask_ref[j] = jnp.where((hit > 0.5) & (kpos <= tpos), 0.0, NEG_INF)
        return 0

    lax.fori_loop(0, nj, make_mask, 0)

    def slc_body(j, carry):
        off = pl.multiple_of(j * kb, kb)
        return _flash_step(carry, qr, ks_ref[0, pl.ds(off, kb), cols], vs_ref[0, pl.ds(off, kb), cols],
                           mask_ref[j], nh)

    if kv_t:
        o_slc = _flash_finish(_flash_step(_flash_init(rows, d), qr, ks_ref[0, cols, :], vs_ref[0, cols, :],
                                          mask_ref[0], nh, kv_t=True))
    else:
        o_slc = _flash_finish(lax.fori_loop(0, nj, slc_body, _flash_init(rows, d)))

    wk = min(_round_up(WINDOW + tq, WIN_CHUNK), kw_ref.shape[1])
    first = jnp.maximum(t0 - (WINDOW - 1) - wbase, 0) // WIN_CHUNK * WIN_CHUNK
    off = pl.multiple_of(jnp.minimum(first, kw_ref.shape[1] - wk), WIN_CHUNK)
    dist = tpos - (wbase + off + _iota((1, wk), 1))
    madd_w = jnp.where((dist >= 0) & (dist < WINDOW), 0.0, NEG_INF)
    o_win = _flash_finish(_flash_step(_flash_init(rows, d), qr, kw_ref[0, pl.ds(off, wk), cols],
                                      vw_ref[0, pl.ds(off, wk), cols], madd_w, nh))

    gates = gate_ref[0]
    outs = []
    for h in range(nh):
        rs = slice(h * tq, (h + 1) * tq)
        c = g * nh + h
        outs.append(gates[:, c:c + 1] * o_cmp[rs] + gates[:, N_HEADS_A + c:N_HEADS_A + c + 1] * o_slc[rs]
                    + gates[:, 2 * N_HEADS_A + c:2 * N_HEADS_A + c + 1] * o_win[rs])
    return outs


def _nsa_kernel(qc_ref, qr_ref, gate_ref, kc_ref, vc_ref, ks_ref, vs_ref, kw_ref, vw_ref, o_ref, mask_ref, **kw):
    i = pl.program_id(1)
    outs = []
    for g in range(KV_GROUPS_A):
        outs += _nsa_group(g, i, qc_ref, qr_ref, gate_ref, kc_ref, vc_ref, ks_ref, vs_ref, kw_ref, vw_ref,
                           mask_ref, **kw)
    o_ref[0] = jnp.concatenate(outs, axis=1).astype(o_ref.dtype)


def nsa_attention(qc, qr, gates, kvc, ks, vs, kw, vw, *, tq, qpos0, wbase, n_keys, kv_t=False):
    b, t, hd = qc.shape
    l = ks.shape[2] if kv_t else ks.shape[1]
    lw = kw.shape[1]
    nc, d = kvc.shape[3:]
    kb = min(KEY_CHUNK, l) if tq >= TQ else l
    assert l % kb == 0 and t % tq == 0 and lw % WIN_CHUNK == 0 and (kb == l or not kv_t)
    assert (qpos0 + t - 1) // kb + 1 <= l // kb and (qpos0 + t - 1 - wbase) // WIN_CHUNK + 1 <= lw // WIN_CHUNK
    assert (qpos0 - wbase) % WIN_CHUNK == 0 and WIN_CHUNK % tq == 0 and tq > 1
    ns = l // SLC_BLOCK
    n_sel = min(N_SELECT, -(-n_keys // SLC_BLOCK))
    q_spec = pl.BlockSpec((1, tq, hd), lambda bi, i: (bi, i, 0))
    kc_spec = pl.BlockSpec((1, 1, KV_GROUPS_A, nc, d), lambda bi, i: (0, bi, 0, 0, 0))
    vc_spec = pl.BlockSpec((1, 1, KV_GROUPS_A, nc, d), lambda bi, i: (1, bi, 0, 0, 0))
    k_spec = pl.BlockSpec((1, KV_A, l) if kv_t else (1, l, KV_A), lambda bi, i: (bi, 0, 0))
    w_spec = pl.BlockSpec((1, lw, KV_A), lambda bi, i: (bi, 0, 0))
    return pl.pallas_call(
        functools.partial(_nsa_kernel, tq=tq, kb=kb, qpos0=qpos0, wbase=wbase, ns=ns, n_sel=n_sel, kv_t=kv_t),
        grid=(b, t // tq),
        in_specs=[q_spec, q_spec, pl.BlockSpec((1, tq, 3 * N_HEADS_A), lambda bi, i: (bi, i, 0)),
                  kc_spec, vc_spec, k_spec, k_spec, w_spec, w_spec],
        out_specs=q_spec,
        out_shape=jax.ShapeDtypeStruct((b, t, hd), MXU_DTYPE),
        scratch_shapes=[pltpu.VMEM((l // kb, tq, kb), jnp.float32)],
        compiler_params=pltpu.CompilerParams(
            dimension_semantics=("parallel", "arbitrary"), vmem_limit_bytes=VMEM_LIMIT),
        name="nsa_attention",
    )(qc, qr, gates, kvc, kvc, ks, vs, kw, vw)


def _dsa_kernel(qb_ref, qi_ref, wi_ref, kb_ref, vb_ref, ki_ref, o_ref, score_ref, *, tq, kb, qpos0, n_keep, kv_t):
    i = pl.program_id(1)
    nh, d = N_HEADS_B, HEAD_DIM_B
    t0 = qpos0 + i * tq
    nj = (t0 + tq - 1) // kb + 1
    tpos = t0 + _iota((tq, 1), 0)
    w = wi_ref[0]

    def idx_body(j, carry):
        lo, hi = carry
        off = pl.multiple_of(j * kb, kb)
        kidx = ki_ref[0] if kv_t else ki_ref[0, pl.ds(off, kb), :]
        acc = jnp.zeros((tq, kb), jnp.float32)
        heads = [qi_ref[0, :, h * IDX_DIM:(h + 1) * IDX_DIM] for h in range(IDX_HEADS)]
        if kv_t:
            dots_all = jnp.dot(jnp.concatenate(heads, axis=0), kidx, preferred_element_type=jnp.float32)
        for h in range(IDX_HEADS):
            dots = dots_all[h * tq:(h + 1) * tq] if kv_t else _dot_nt(heads[h], kidx)
            acc = acc + w[:, h:h + 1] * jnp.maximum(dots, 0.0)
        vis = (off + _iota((1, kb), 1)) <= tpos
        score_ref[j] = jnp.where(vis, acc, NEG_INF)
        lo = jnp.minimum(lo, jnp.min(jnp.where(vis, acc, POS_INF), axis=1, keepdims=True))
        hi = jnp.maximum(hi, jnp.max(jnp.where(vis, acc, NEG_INF), axis=1, keepdims=True))
        return lo, hi

    lo, hi = lax.fori_loop(0, nj, idx_body, (jnp.full((tq, 1), POS_INF, jnp.float32),
                                             jnp.full((tq, 1), NEG_INF, jnp.float32)))

    k = float(n_keep)
    n_vis = (tpos + 1).astype(jnp.float32)

    def unsettled(cnt_lo):
        return jnp.max(jnp.where((cnt_lo != k) & (n_vis > k), 1.0, 0.0)) > 0.0

    def bisect(state):
        it, lo, hi, cnt_lo = state
        mid = 0.5 * (lo + hi)
        mid_b = jnp.broadcast_to(mid, (tq, LANE))

        def count(j, acc):
            sc = score_ref[j]
            for c in range(kb // LANE):
                acc = acc + jnp.where(sc[:, c * LANE:(c + 1) * LANE] >= mid_b, 1.0, 0.0)
            return acc

        cnt = jnp.sum(lax.fori_loop(0, nj, count, jnp.zeros((tq, LANE), jnp.float32)), axis=1, keepdims=True)
        ge = cnt >= k
        return it + 1, jnp.where(ge, mid, lo), jnp.where(ge, hi, mid), jnp.where(ge, cnt, cnt_lo)

    _, thr, _, _ = lax.while_loop(lambda st: (st[0] < BISECT_ITERS) & unsettled(st[3]), bisect,
                                  (jnp.int32(0), lo, hi, n_vis))

    q = jnp.concatenate([qb_ref[0, :, h * d:(h + 1) * d] for h in range(nh)], axis=0)

    def att_body(j, carry):
        off = pl.multiple_of(j * kb, kb)
        madd = jnp.where(score_ref[j] >= thr, 0.0, NEG_INF)
        if kv_t:
            return _flash_step(carry, q, kb_ref[0], vb_ref[0], madd, nh, scale=d ** -0.5, kv_t=True)
        return _flash_step(carry, q, kb_ref[0, pl.ds(off, kb), :], vb_ref[0, pl.ds(off, kb), :], madd, nh,
                           scale=d ** -0.5)

    o = _flash_finish(lax.fori_loop(0, nj, att_body, _flash_init(nh * tq, d)))
    o_ref[0] = jnp.concatenate([o[h * tq:(h + 1) * tq] for h in range(nh)], axis=1).astype(o_ref.dtype)


def dsa_attention(qb, qi, wi, kb_, vb, ki, *, tq, qpos0, n_keep, kv_t=False):
    b, t, hd = qb.shape
    l = kb_.shape[2] if kv_t else kb_.shape[1]
    kb = min(KEY_CHUNK, l) if tq >= TQ else l
    assert l % kb == 0 and t % tq == 0 and (qpos0 + t - 1) // kb + 1 <= l // kb and (kb == l or not kv_t)
    kv_spec = lambda width: pl.BlockSpec((1, width, l) if kv_t else (1, l, width), lambda bi, i: (bi, 0, 0))
    return pl.pallas_call(
        functools.partial(_dsa_kernel, tq=tq, kb=kb, qpos0=qpos0, n_keep=n_keep, kv_t=kv_t),
        grid=(b, t // tq),
        in_specs=[pl.BlockSpec((1, tq, hd), lambda bi, i: (bi, i, 0)),
                  pl.BlockSpec((1, tq, IDX_HEADS * IDX_DIM), lambda bi, i: (bi, i, 0)),
                  pl.BlockSpec((1, tq, IDX_HEADS), lambda bi, i: (bi, i, 0)),
                  kv_spec(HEAD_DIM_B), kv_spec(HEAD_DIM_B), kv_spec(IDX_DIM)],
        out_specs=pl.BlockSpec((1, tq, hd), lambda bi, i: (bi, i, 0)),
        out_shape=jax.ShapeDtypeStruct((b, t, hd), MXU_DTYPE),
        scratch_shapes=[pltpu.VMEM((l // kb, tq, kb), jnp.float32)],
        compiler_params=pltpu.CompilerParams(
            dimension_semantics=("parallel", "arbitrary"), vmem_limit_bytes=VMEM_LIMIT),
        name="dsa_attention",
    )(qb, qi, wi, kb_, vb, ki)


def _page_maps(n_pages, pp):
    n_steps = n_pages // pp

    def page_map(r):
        return lambda b, s, pt: (pt[b * n_pages + jnp.minimum(s, n_steps - 1) * pp + r], 0, 0)

    return n_steps, page_map


def _chunk_rows(tok_ref, sec, row0, n):
    d = HEAD_DIM_A
    first_half = _iota((n, KV_A), 1) < d
    pieces = [[] for _ in range(KV_GROUPS_A)]
    for j in range(0, CMP_STRIDE, 2):
        a, b = [tok_ref[sec, pl.ds(row0 + jj, n, stride=CMP_STRIDE), :] for jj in (j, j + 1)]
        pieces[0].append(jnp.where(first_half, a, pltpu.roll(b, d, 1)))
        pieces[1].append(jnp.where(first_half, pltpu.roll(a, d, 1), b))
    return [jnp.concatenate(p, axis=1) for p in pieces]


def _nsa_gather_kernel(pt_ref, *refs, pp, n_steps, rows):
    del pt_ref
    pages, (tail_tok_ref, tail_t_ref, pe_ref) = refs[:pp], refs[pp:pp + 3]
    zt_ref, zb_ref, ks_ref, vs_ref, tok_ref = refs[pp + 3:]
    is_tail = pl.program_id(1) == n_steps
    z = [[[] for _ in range(KV_GROUPS_A)] for _ in range(2)]
    for r in range(pp):
        cs = slice(r * rows, (r + 1) * rows)
        ks_ref[0, :, cs] = jnp.where(is_tail, tail_t_ref[0, :KV_A, cs],
                                     pages[r][0, 2 * KV_A:3 * KV_A, :]).astype(ks_ref.dtype)
        vs_ref[0, :, cs] = jnp.where(is_tail, tail_t_ref[0, KV_A:, cs],
                                     pages[r][0, 3 * KV_A:, :]).astype(vs_ref.dtype)
        for sec in range(2):
            tok_ref[sec, cs, :] = jnp.where(is_tail, tail_tok_ref[0, sec, cs, :],
                                            pages[r][0, sec * KV_A:(sec + 1) * KV_A, :].T)
            for g, zg in enumerate(_chunk_rows(tok_ref, sec, r * rows, rows // CMP_STRIDE)):
                z[sec][g].append(zg)
    for sec in range(2):
        for g in range(KV_GROUPS_A):
            zf = jnp.concatenate(z[sec][g], axis=0)
            zt_ref[sec, 0, g] = (zf + pe_ref[sec, 0]).astype(zt_ref.dtype)
            zb_ref[sec, 0, g] = (zf + pe_ref[sec, 1]).astype(zb_ref.dtype)


def nsa_gather(cache_t, page_table, new_rows, cmp_pe):
    db, n_pages = page_table.shape
    width, rows = cache_t.shape[1:]
    pp = PAGES_PER_STEP
    n_steps, page_map = _page_maps(n_pages, pp)
    l = (n_steps + 1) * pp * rows
    cps = pp * rows // CMP_STRIDE
    flat = CMP_STRIDE * HEAD_DIM_A
    pe = cmp_pe.reshape(2, 2, 1, flat).astype(jnp.float32)
    tail = pad_rows(new_rows, pp * rows)
    tail_tok = tail[:, :, :2 * KV_A].reshape(db, pp * rows, 2, KV_A).swapaxes(1, 2)
    tail_t = tail[:, :, 2 * KV_A:].swapaxes(1, 2)
    z_spec = pl.BlockSpec((2, 1, KV_GROUPS_A, cps, flat), lambda b, s, pt: (0, b, 0, s, 0))
    r_spec = pl.BlockSpec((1, KV_A, pp * rows), lambda b, s, pt: (b, 0, s))
    z_shape = jax.ShapeDtypeStruct((2, db, KV_GROUPS_A, l // CMP_STRIDE, flat), MXU_DTYPE)
    r_shape = jax.ShapeDtypeStruct((db, KV_A, l), MXU_DTYPE)
    return pl.pallas_call(
        functools.partial(_nsa_gather_kernel, pp=pp, n_steps=n_steps, rows=rows),
        grid_spec=pltpu.PrefetchScalarGridSpec(
            num_scalar_prefetch=1,
            grid=(db, n_steps + 1),
            in_specs=[pl.BlockSpec((1, width, rows), page_map(r)) for r in range(pp)]
            + [pl.BlockSpec((1, 2, pp * rows, KV_A), lambda b, s, pt: (b, 0, 0, 0)),
               pl.BlockSpec((1, 2 * KV_A, pp * rows), lambda b, s, pt: (b, 0, 0)),
               pl.BlockSpec((2, 2, 1, flat), lambda b, s, pt: (0, 0, 0, 0))],
            out_specs=[z_spec, z_spec, r_spec, r_spec],
            scratch_shapes=[pltpu.VMEM((2, pp * rows, KV_A), jnp.float32)]),
        out_shape=[z_shape, z_shape, r_shape, r_shape],
        compiler_params=pltpu.CompilerParams(
            dimension_semantics=("parallel", "arbitrary"), vmem_limit_bytes=VMEM_LIMIT),
        name="nsa_gather",
    )(page_table.reshape(-1), *([cache_t] * pp), tail_tok, tail_t, pe)


def _dsa_gather_kernel(pt_ref, *refs, pp, n_steps, rows):
    del pt_ref
    pages, tail_ref, (k_ref, v_ref, i_ref) = refs[:pp], refs[pp], refs[pp + 1:]
    is_tail = pl.program_id(1) == n_steps
    d = HEAD_DIM_B
    for r in range(pp):
        cs = slice(r * rows, (r + 1) * rows)
        x = jnp.where(is_tail, tail_ref[0, :, cs], pages[r][0])
        k_ref[0, :, cs] = x[:d].astype(k_ref.dtype)
        v_ref[0, :, cs] = x[d:2 * d].astype(v_ref.dtype)
        i_ref[0, :, cs] = x[2 * d:].astype(i_ref.dtype)


def dsa_gather(cache_t, page_table, new_rows):
    db, n_pages = page_table.shape
    width, rows = cache_t.shape[1:]
    pp = PAGES_PER_STEP
    n_steps, page_map = _page_maps(n_pages, pp)
    l = (n_steps + 1) * pp * rows
    tail_t = pad_rows(new_rows, pp * rows).swapaxes(1, 2)
    out_spec = lambda w: pl.BlockSpec((1, w, pp * rows), lambda b, s, pt: (b, 0, s))
    widths = (HEAD_DIM_B, HEAD_DIM_B, IDX_DIM)
    return pl.pallas_call(
        functools.partial(_dsa_gather_kernel, pp=pp, n_steps=n_steps, rows=rows),
        grid_spec=pltpu.PrefetchScalarGridSpec(
            num_scalar_prefetch=1,
            grid=(db, n_steps + 1),
            in_specs=[pl.BlockSpec((1, width, rows), page_map(r)) for r in range(pp)]
            + [pl.BlockSpec((1, width, pp * rows), lambda b, s, pt: (b, 0, 0))],
            out_specs=[out_spec(w) for w in widths]),
        out_shape=[jax.ShapeDtypeStruct((db, w, l), MXU_DTYPE) for w in widths],
        compiler_params=pltpu.CompilerParams(
            dimension_semantics=("parallel", "arbitrary"), vmem_limit_bytes=VMEM_LIMIT),
        name="dsa_gather",
    )(page_table.reshape(-1), *([cache_t] * pp), tail_t)


def _compress_kernel(zt_ref, zb_ref, pe_ref, w1t_ref, w1b_ref, b1_ref, w2_ref, o_ref, ab_ref, *, ncp):
    dot = functools.partial(jnp.dot, preferred_element_type=jnp.float32)
    ch = zb_ref.shape[2]
    at = dot(zt_ref[0, 0], w1t_ref[0])
    ab_ref[:ch] = dot(zb_ref[0, 0], w1b_ref[0])
    pe_rows = jnp.broadcast_to(pe_ref[0, 1], (SUBLANE, pe_ref.shape[3])).astype(zb_ref.dtype)
    ab_ref[ch:] = dot(pe_rows, w1b_ref[0])
    h = jax.nn.gelu(at[:ncp] + ab_ref[pl.ds(1, ncp), :] + b1_ref[0])
    o_ref[0, 0, :ncp] = dot(h.astype(w2_ref.dtype), w2_ref[0]).astype(o_ref.dtype)
    if o_ref.shape[2] > ncp:
        o_ref[0, 0, ncp:] = jnp.zeros((o_ref.shape[2] - ncp, o_ref.shape[3]), o_ref.dtype)


def compress(zt, zb, cmp_pe, w1, b1, w2, n_keys):
    ncp = _round_up(-(-n_keys // CMP_STRIDE), BF16_ROWS)
    ncl = ncp if ncp <= LANE else _round_up(ncp, LANE)
    _, nb, ch, kdim = zt.shape
    hid = w1.shape[-1]
    d = w2.shape[-1]
    assert ch + SUBLANE >= ncp + 1 and ch % SUBLANE == 0
    w1 = w1.astype(MXU_DTYPE)
    pe = cmp_pe.reshape(2, 2, 1, kdim).astype(jnp.float32)
    z_spec = pl.BlockSpec((1, 1, ch, kdim), lambda s, n: (s, n, 0, 0))
    return pl.pallas_call(
        functools.partial(_compress_kernel, ncp=ncp),
        grid=(2, nb),
        in_specs=[z_spec, z_spec,
                  pl.BlockSpec((1, 2, 1, kdim), lambda s, n: (s, 0, 0, 0)),
                  pl.BlockSpec((1, kdim, hid), lambda s, n: (s, 0, 0)),
                  pl.BlockSpec((1, kdim, hid), lambda s, n: (s, 1, 0)),
                  pl.BlockSpec((1, 1, hid), lambda s, n: (s, 0, 0)),
                  pl.BlockSpec((1, hid, d), lambda s, n: (s, 0, 0))],
        out_specs=pl.BlockSpec((1, 1, ncl, d), lambda s, n: (s, n, 0, 0)),
        out_shape=jax.ShapeDtypeStruct((2, nb, ncl, d), MXU_DTYPE),
        scratch_shapes=[pltpu.VMEM((ch + SUBLANE, hid), jnp.float32)],
        compiler_params=pltpu.CompilerParams(
            dimension_semantics=("parallel", "parallel"), vmem_limit_bytes=VMEM_LIMIT),
        name="compress",
    )(zt, zb, pe, w1, w1, b1.reshape(2, 1, hid).astype(jnp.float32), w2.astype(MXU_DTYPE))


def pad_rows(a, n):
    return jnp.pad(a, [(0, 0), (0, n - a.shape[1])] + [(0, 0)] * (a.ndim - 2))


def _rope_tables(pos, d):
    half = d // ROT_FRACTION // 2
    inv = ROPE_THETA ** (-jnp.arange(half, dtype=jnp.float32) / half)
    ang = pos.astype(jnp.float32)[:, None] * inv[None, :]
    lane = jnp.arange(LANE) % d
    cos = jnp.cos(ang)[:, lane % half]
    sin = jnp.sin(ang)[:, lane % half]
    one, zero = jnp.ones_like(cos), jnp.zeros_like(cos)
    c = jnp.where(lane < 2 * half, cos, one)
    sa = jnp.where((lane >= half) & (lane < 2 * half), sin, zero)
    sb = jnp.where(lane < half, -sin, zero)
    return jnp.stack([c, sa, sb])


def _rope(x, t_ref, half):
    c, sa, sb = t_ref[0], t_ref[1], t_ref[2]
    outs = []
    for j in range(x.shape[1] // LANE):
        xs = x[:, j * LANE:(j + 1) * LANE]
        outs.append(xs * c + pltpu.roll(xs, half, 1) * sa + pltpu.roll(xs, LANE - half, 1) * sb)
    return outs[0] if len(outs) == 1 else jnp.concatenate(outs, axis=1)


_QA = N_HEADS_A * HEAD_DIM_A
_KVA = 6 * KV_A
_QB = N_HEADS_B * HEAD_DIM_B
_KVB = 2 * HEAD_DIM_B
_QI = IDX_HEADS * IDX_DIM
_MISC = IDX_DIM + IDX_HEADS + 3 * N_HEADS_A
PREP_WIDTH = _QA + _KVA + _QB + _KVB + _QI + _MISC
assert _MISC == LANE


def _prep_kernel(x_ref, t64_ref, t128_ref, pe_ref, qc_ref, qr_ref, qb_ref, qi_ref, nsa_ref, win_ref, dsa_ref,
                 ks_ref, vs_ref, kw_ref, vw_ref, kb_ref, vb_ref, ki_ref, wi_ref, gate_ref, *z_refs, tm):
    x = x_ref[...]
    o = 0
    qa = x[:, o:o + _QA]; o += _QA
    kva = x[:, o:o + _KVA]; o += _KVA
    qb = x[:, o:o + _QB]; o += _QB
    kvb = x[:, o:o + _KVB]; o += _KVB
    qi = x[:, o:o + _QI]; o += _QI
    misc = x[:, o:o + _MISC]
    h64, h128 = HEAD_DIM_A // ROT_FRACTION // 2, HEAD_DIM_B // ROT_FRACTION // 2
    sec = lambda i: kva[:, i * KV_A:(i + 1) * KV_A]
    mx = lambda a: a.astype(qc_ref.dtype)
    qc_ref[...] = mx(qa * HEAD_DIM_A ** -0.5)
    qr_ref[...] = mx(_rope(qa, t64_ref, h64) * HEAD_DIM_A ** -0.5)
    k_slc, k_win = _rope(sec(2), t64_ref, h64), _rope(sec(4), t64_ref, h64)
    nsa_ref[:, :2 * KV_A] = kva[:, :2 * KV_A]
    nsa_ref[:, 2 * KV_A:3 * KV_A] = k_slc
    nsa_ref[:, 3 * KV_A:] = sec(3)
    win_ref[:, :KV_A] = k_win
    win_ref[:, KV_A:] = sec(5)
    ks_ref[...], vs_ref[...], kw_ref[...], vw_ref[...] = mx(k_slc), mx(sec(3)), mx(k_win), mx(sec(5))
    qb_ref[...] = mx(_rope(qb, t128_ref, h128))
    k_b, v_b = _rope(kvb[:, :HEAD_DIM_B], t128_ref, h128), kvb[:, HEAD_DIM_B:]
    k_idx = _rope(misc, t64_ref, h64)[:, :IDX_DIM]
    dsa_ref[:, :HEAD_DIM_B] = k_b
    dsa_ref[:, HEAD_DIM_B:2 * HEAD_DIM_B] = v_b
    dsa_ref[:, 2 * HEAD_DIM_B:] = k_idx
    kb_ref[...], vb_ref[...], ki_ref[...] = mx(k_b), mx(v_b), mx(k_idx)
    qi_ref[...] = mx(_rope(qi, t64_ref, h64) * IDX_DIM ** -0.5)
    wi_ref[...] = misc[:, IDX_DIM:IDX_DIM + IDX_HEADS] * IDX_HEADS ** -0.5
    gate_ref[...] = jax.nn.sigmoid(misc[:, IDX_DIM + IDX_HEADS:])
    if z_refs:
        zt_ref, zb_ref, tok_ref = z_refs
        for s_ in range(2):
            tok_ref[s_] = sec(s_)
            for g, zg in enumerate(_chunk_rows(tok_ref, s_, 0, tm // CMP_STRIDE)):
                zt_ref[s_, 0, g] = (zg + pe_ref[s_, 0]).astype(zt_ref.dtype)
                zb_ref[s_, 0, g] = (zg + pe_ref[s_, 1]).astype(zb_ref.dtype)


def prep(proj, pos, cmp_pe, with_chunks):
    b, t, width = proj.shape
    assert width == PREP_WIDTH
    m = b * t
    tm = 256 if t % 256 == 0 else m
    assert m % tm == 0 and t % tm in (0, t)
    nt = max(t // tm, 1)
    flat = CMP_STRIDE * HEAD_DIM_A
    names = ['qc', 'qr', 'qb', 'qi', 'nsa_rows', 'win_rows', 'dsa_rows', 'ks', 'vs', 'kw', 'vw', 'kb', 'vb', 'ki',
             'wi', 'gates']
    widths = [_QA, _QA, _QB, _QI, NSA_SECTIONS * KV_A, 2 * KV_A, DSA_ROW, KV_A, KV_A, KV_A, KV_A, HEAD_DIM_B,
              HEAD_DIM_B, IDX_DIM, IDX_HEADS, 3 * N_HEADS_A]
    dtypes = [MXU_DTYPE] * 4 + [jnp.float32] * 3 + [MXU_DTYPE] * 7 + [jnp.float32] * 2
    row = lambda w: pl.BlockSpec((tm, w), lambda i: (i, 0))
    out_specs = [row(w) for w in widths]
    out_shape = [jax.ShapeDtypeStruct((m, w), dt) for w, dt in zip(widths, dtypes)]
    scratch = []
    if with_chunks:
        assert tm % CMP_STRIDE == 0 and t % tm == 0
        z_spec = pl.BlockSpec((2, 1, KV_GROUPS_A, tm // CMP_STRIDE, flat), lambda i: (0, i // nt, 0, i % nt, 0))
        z_shape = jax.ShapeDtypeStruct((2, b, KV_GROUPS_A, t // CMP_STRIDE, flat), MXU_DTYPE)
        out_specs += [z_spec, z_spec]
        out_shape += [z_shape, z_shape]
        names += ['zt', 'zb']
        scratch = [pltpu.VMEM((2, tm, KV_A), jnp.float32)]
    t_spec = pl.BlockSpec((3, tm, LANE), lambda i: (0, i, 0))
    outs = pl.pallas_call(
        functools.partial(_prep_kernel, tm=tm),
        grid=(m // tm,),
        in_specs=[row(width), t_spec, t_spec, pl.BlockSpec((2, 2, 1, flat), lambda i: (0, 0, 0, 0))],
        out_specs=out_specs,
        out_shape=out_shape,
        scratch_shapes=scratch,
        compiler_params=pltpu.CompilerParams(dimension_semantics=("parallel",), vmem_limit_bytes=VMEM_LIMIT),
        name="prep",
    )(proj.reshape(m, width), _rope_tables(pos, HEAD_DIM_A), _rope_tables(pos, HEAD_DIM_B),
      cmp_pe.reshape(2, 2, 1, flat).astype(jnp.float32))
    return {n: (o if o.ndim > 2 else o.reshape(b, t, o.shape[-1])) for n, o in zip(names, outs)}


def project(x, g, w_in, sizes):
    starts = [sum(sizes[:i]) for i in range(len(sizes))]
    qa, kva, ga, qb, kvb, qi, ki, wi, gm = [slice(o, o + n) for o, n in zip(starts, sizes)]
    w_bf = w_in.astype(MXU_DTYPE)
    w_prep = jnp.concatenate([w_bf[:, c] for c in (qa, kva, qb, kvb, qi, ki, wi, ga)], axis=1)
    return norm_matmul(x, g, w_prep), norm_matmul(x, g, w_bf[:, gm])


def mixer_prompt(x, g, w_in, cmp_pe, cmp_w1, cmp_b1, cmp_w2, sizes):
    b, s, _ = x.shape
    proj, gm = project(x, g, w_in, sizes)
    p = prep(proj, jnp.tile(jnp.arange(s, dtype=jnp.int32), b), cmp_pe, True)
    merge_bg = lambda a: a.reshape((2, b * KV_GROUPS_A) + a.shape[3:])
    kvc = compress(merge_bg(p['zt']), merge_bg(p['zb']), cmp_pe, cmp_w1, cmp_b1, cmp_w2, s)
    kvc = kvc.reshape(2, b, KV_GROUPS_A, kvc.shape[2], HEAD_DIM_A)
    o_nsa = nsa_attention(p['qc'], p['qr'], p['gates'], kvc, p['ks'], p['vs'], p['kw'], p['vw'],
                          tq=min(TQ, s), qpos0=0, wbase=0, n_keys=s)
    o_dsa = dsa_attention(p['qb'], p['qi'], p['wi'], p['kb'], p['vb'], p['ki'],
                          tq=min(TQ, s), qpos0=0, n_keep=min(DSA_TOPK, s // 4))
    nsa_rows = p['nsa_rows'].reshape(b, s, NSA_SECTIONS, KV_GROUPS_A, HEAD_DIM_A)
    win_state = p['win_rows'].reshape(b, s, 2, KV_GROUPS_A, HEAD_DIM_A)[:, -min(WINDOW, s):]
    return (o_nsa, o_dsa, gm), nsa_rows, win_state, p['dsa_rows']


def mixer_sample(x, g, cache_nsa, win_buf, cache_dsa, page_table, w_in, cmp_pe, cmp_w1, cmp_b1, cmp_w2, sizes):
    b, t, _ = x.shape
    page = cache_nsa.shape[1]
    past_len = page_table.shape[1] * page
    n_keys = past_len + t
    proj, gm = project(x, g, w_in, sizes)
    p = prep(proj, jnp.tile(past_len + jnp.arange(t, dtype=jnp.int32), b), cmp_pe, False)
    pad = lambda a: pad_rows(a, TQ_STEP)

    cache_nsa_t = jnp.transpose(cache_nsa, (0, 2, 3, 4, 1)).reshape(cache_nsa.shape[0], NSA_SECTIONS * KV_A, page)
    zt, zb, ks, vs = nsa_gather(cache_nsa_t, page_table, p['nsa_rows'], cmp_pe)
    merge_bg = lambda a: a.reshape((2, b * KV_GROUPS_A) + a.shape[3:])
    kvc = compress(merge_bg(zt), merge_bg(zb), cmp_pe, cmp_w1, cmp_b1, cmp_w2, n_keys)
    kvc = kvc.reshape(2, b, KV_GROUPS_A, kvc.shape[2], HEAD_DIM_A)
    w_len = win_buf.shape[1]
    win_new = p['win_rows'].reshape(b, t, 2, KV_GROUPS_A, HEAD_DIM_A)
    win_all = jnp.concatenate([win_buf, win_new], axis=1)
    win_pad = pad_rows(win_all, w_len + WIN_CHUNK).astype(MXU_DTYPE)
    kw = win_pad[:, :, 0].reshape(b, w_len + WIN_CHUNK, KV_A)
    vw = win_pad[:, :, 1].reshape(b, w_len + WIN_CHUNK, KV_A)
    o_nsa = nsa_attention(pad(p['qc']), pad(p['qr']), pad(p['gates']), kvc, ks, vs, kw, vw,
                          tq=TQ_STEP, qpos0=past_len, wbase=past_len - w_len, n_keys=n_keys, kv_t=True)

    kb_, vb, ki = dsa_gather(jnp.swapaxes(cache_dsa, 1, 2), page_table, p['dsa_rows'])
    o_dsa = dsa_attention(pad(p['qb']), pad(p['qi']), pad(p['wi']), kb_, vb, ki, tq=TQ_STEP, qpos0=past_len,
                          n_keep=min(DSA_TOPK, n_keys // 4), kv_t=True)
    nsa_rows = p['nsa_rows'].reshape(b, t, NSA_SECTIONS, KV_GROUPS_A, HEAD_DIM_A)
    return (o_nsa, o_dsa, pad(gm)), nsa_rows, win_all[:, -w_len:], p['dsa_rows']


def kernel(x_prompt, x_sample, mem_prompt, cache_nsa_kv, state_nsa_win, cache_dsa_kv, cache_mem_kv, state_conv,
           page_table, norm_g, w_in, cmp_pe, cmp_w1, cmp_b1, cmp_w2, w_out_a, w_out_b, w_out, w_mem_q, w_mem_kv,
           w_mem_out, w_up, conv_w, conv_b, w_down, final_g):
    depth = w_in.shape[0]
    d_model = x_prompt.shape[-1]
    d_ff = w_down.shape[1]
    assert CONV_WIDTH == 3
    sizes = (N_HEADS_A * HEAD_DIM_A, 6 * KV_A, 3 * N_HEADS_A, N_HEADS_B * HEAD_DIM_B, 2 * HEAD_DIM_B,
             IDX_HEADS * IDX_DIM, IDX_DIM, IDX_HEADS, 2 * d_model)
    xp, xs = x_prompt, pad_rows(x_sample, TQ_STEP)
    t_step = x_sample.shape[1]
    nsa_p, nsa_s, win_p, win_s, dsa_p, dsa_s, mem_p, conv_p, conv_s = [], [], [], [], [], [], [], [], []
    for l in range(depth):
        assert l == depth - 1, "the fused FFN epilogue applies the final norm"
        branches_p, a, bwin, c = mixer_prompt(xp, norm_g[l, 0], w_in[l], cmp_pe[l], cmp_w1[l], cmp_b1[l], cmp_w2[l],
                                              sizes)
        nsa_p.append(a); win_p.append(bwin); dsa_p.append(c)
        branches_s, a, bwin, c = mixer_sample(xs[:, :t_step], norm_g[l, 0], cache_nsa_kv[l], state_nsa_win[l],
                                              cache_dsa_kv[l], page_table, w_in[l], cmp_pe[l], cmp_w1[l], cmp_b1[l],
                                              cmp_w2[l], sizes)
        nsa_s.append(a); win_s.append(bwin); dsa_s.append(c)
        kv_p = norm_matmul(mem_prompt, norm_g[l, 2], w_mem_kv[l])
        mem_p.append(kv_p.reshape(kv_p.shape[:2] + (2, MEM_HEADS, MEM_HEAD_DIM)))
        kv_s = cache_mem_kv[l].reshape(cache_mem_kv.shape[1:3] + (-1,))
        weights = (norm_g[l, 1], norm_g[l, 3], final_g, w_out_a[l], w_out_b[l], w_out[l], w_mem_q[l], w_mem_out[l],
                   w_up[l], conv_w[l], conv_b[l], w_down[l])
        xp, cp = dense_tail(xp, *branches_p, kv_p, jnp.zeros((xp.shape[0], CONV_WIDTH - 1, 2 * d_ff), xp.dtype),
                            xp.shape[1], *weights)
        xs, cs = dense_tail(xs, *branches_s, kv_s, state_conv[l], t_step, *weights)
        conv_p.append(cp); conv_s.append(cs)
    y_prompt, y_sample = xp, xs[:, :t_step]
    return (y_prompt, y_sample, jnp.stack(nsa_p), jnp.stack(nsa_s), jnp.stack(win_p), jnp.stack(win_s),
            jnp.stack(dsa_p), jnp.stack(dsa_s), jnp.stack(mem_p), jnp.stack(conv_p), jnp.stack(conv_s))
```

```python
import functools

import jax
import jax.numpy as jnp
from jax import lax
from jax.experimental import pallas as pl
from jax.experimental.pallas import tpu as pltpu

N_HEADS_A = 16
HEAD_DIM_A = 64
KV_GROUPS_A = 2
CMP_BLOCK = 32
CMP_STRIDE = 16
SLC_BLOCK = 64
N_SELECT = 16
WINDOW = 512
N_HEADS_B = 8
HEAD_DIM_B = 128
IDX_HEADS = 16
IDX_DIM = 64
DSA_TOPK = 256
MEM_HEADS = 4
MEM_HEAD_DIM = 128
CONV_WIDTH = 3
ROPE_THETA = 500000.0
ROT_FRACTION = 4
EPS = 1e-6
KV_A = KV_GROUPS_A * HEAD_DIM_A
CMP_PER_SLC = SLC_BLOCK // CMP_STRIDE
HPG_A = N_HEADS_A // KV_GROUPS_A
DSA_ROW = 2 * HEAD_DIM_B + IDX_DIM
NSA_SECTIONS = 4

LANE = 128
SUBLANE = 8
BF16_ROWS = 16
VMEM_LIMIT = 48 * 1024 * 1024

NEG_INF = float('-inf')
POS_INF = float('inf')
MXU_DTYPE = jnp.bfloat16
TQ = 128
TQ_STEP = BF16_ROWS
KEY_CHUNK = 512
WIN_CHUNK = 128
PAGES_PER_STEP = 8
BISECT_ITERS = 40
SLC_SHIFT = SLC_BLOCK.bit_length() - 1
CMP_PER_SLC_SHIFT = CMP_PER_SLC.bit_length() - 1


def _round_up(n, m):
    return -(-n // m) * m


def _tile(n, cap):
    if n <= cap:
        return n
    best = None
    for t in range(LANE, cap + 1, LANE):
        if n % t == 0:
            best = t
    assert best is not None, (n, cap)
    return best


def _mm_kernel(x_ref, w_ref, o_ref, acc_ref):
    @pl.when(pl.program_id(2) == 0)
    def _():
        acc_ref[...] = jnp.zeros_like(acc_ref)

    acc_ref[...] += jnp.dot(x_ref[...], w_ref[...], preferred_element_type=jnp.float32)

    @pl.when(pl.program_id(2) == pl.num_programs(2) - 1)
    def _():
        o_ref[...] = acc_ref[...].astype(o_ref.dtype)


def matmul(x, w, out_dtype=jnp.float32):
    lead = x.shape[:-1]
    k = x.shape[-1]
    n = w.shape[-1]
    x2 = x.reshape(-1, k).astype(MXU_DTYPE)
    w2 = w.astype(MXU_DTYPE)
    m0 = x2.shape[0]
    tm = 512 if m0 >= 512 else _round_up(m0, BF16_ROWS)
    m = _round_up(m0, tm)
    if m != m0:
        x2 = jnp.pad(x2, ((0, m - m0), (0, 0)))
    tn = _tile(n, 1024)
    tk = _tile(k, 2048)
    out = pl.pallas_call(
        _mm_kernel,
        grid=(m // tm, n // tn, k // tk),
        in_specs=[pl.BlockSpec((tm, tk), lambda i, j, l: (i, l)),
                  pl.BlockSpec((tk, tn), lambda i, j, l: (l, j))],
        out_specs=pl.BlockSpec((tm, tn), lambda i, j, l: (i, j)),
        out_shape=jax.ShapeDtypeStruct((m, n), out_dtype),
        scratch_shapes=[pltpu.VMEM((tm, tn), jnp.float32)],
        compiler_params=pltpu.CompilerParams(
            dimension_semantics=("parallel", "parallel", "arbitrary"),
            vmem_limit_bytes=VMEM_LIMIT),
        name="matmul",
    )(x2, w2)
    return out[:m0].reshape(lead + (n,))


def _rms(x, g):
    return x * lax.rsqrt(jnp.mean(x * x, axis=-1, keepdims=True) + EPS) * g


def _norm_mm_kernel(x_ref, g_ref, w_ref, o_ref, xn_ref):
    @pl.when(pl.program_id(1) == 0)
    def _():
        xn_ref[...] = _rms(x_ref[...], g_ref[...]).astype(xn_ref.dtype)

    o_ref[...] = jnp.dot(xn_ref[...], w_ref[...], preferred_element_type=jnp.float32).astype(o_ref.dtype)


def norm_matmul(x, g, w):
    lead = x.shape[:-1]
    k = x.shape[-1]
    n = w.shape[-1]
    x2 = x.reshape(-1, k)
    m = x2.shape[0]
    tm = 512 if m % 512 == 0 else m
    tn = _tile(n, 1536)
    out = pl.pallas_call(
        _norm_mm_kernel,
        grid=(m // tm, n // tn),
        in_specs=[pl.BlockSpec((tm, k), lambda i, j: (i, 0)),
                  pl.BlockSpec((1, k), lambda i, j: (0, 0)),
                  pl.BlockSpec((k, tn), lambda i, j: (0, j))],
        out_specs=pl.BlockSpec((tm, tn), lambda i, j: (i, j)),
        out_shape=jax.ShapeDtypeStruct((m, n), jnp.float32),
        scratch_shapes=[pltpu.VMEM((tm, k), MXU_DTYPE)],
        compiler_params=pltpu.CompilerParams(
            dimension_semantics=("parallel", "arbitrary"), vmem_limit_bytes=VMEM_LIMIT),
        name="norm_matmul",
    )(x2, g.reshape(1, k).astype(jnp.float32), w.astype(MXU_DTYPE))
    return out.reshape(lead + (n,))


def _gated_merge_kernel(oa_ref, ob_ref, wa_ref, wb_ref, ga_ref, gb_ref, o_ref):
    dot = functools.partial(jnp.dot, preferred_element_type=jnp.float32)
    ya = dot(oa_ref[...], wa_ref[...])
    yb = dot(ob_ref[...], wb_ref[...])
    o_ref[...] = (jax.nn.sigmoid(ga_ref[...]) * ya + jax.nn.sigmoid(gb_ref[...]) * yb).astype(o_ref.dtype)


def gated_merge(o_nsa, o_dsa, gm, w_oa, w_ob):
    m, ka = o_nsa.shape
    kb = o_dsa.shape[1]
    n = w_oa.shape[1]
    tm = 512 if m % 512 == 0 else m
    tn = _tile(n, 1024)
    nb = n // tn
    return pl.pallas_call(
        _gated_merge_kernel,
        grid=(m // tm, nb),
        in_specs=[pl.BlockSpec((tm, ka), lambda i, j: (i, 0)),
                  pl.BlockSpec((tm, kb), lambda i, j: (i, 0)),
                  pl.BlockSpec((ka, tn), lambda i, j: (0, j)),
                  pl.BlockSpec((kb, tn), lambda i, j: (0, j)),
                  pl.BlockSpec((tm, tn), lambda i, j: (i, j)),
                  pl.BlockSpec((tm, tn), lambda i, j: (i, j + nb))],
        out_specs=pl.BlockSpec((tm, tn), lambda i, j: (i, j)),
        out_shape=jax.ShapeDtypeStruct((m, n), MXU_DTYPE),
        compiler_params=pltpu.CompilerParams(
            dimension_semantics=("parallel", "parallel"), vmem_limit_bytes=VMEM_LIMIT),
        name="gated_merge",
    )(o_nsa, o_dsa, w_oa.astype(MXU_DTYPE), w_ob.astype(MXU_DTYPE), gm, gm)


def _out_proj_kernel(z_ref, w_ref, x_ref, o_ref):
    o_ref[...] = x_ref[...] + jnp.dot(z_ref[...], w_ref[...], preferred_element_type=jnp.float32)


def out_proj_residual(z, w, x):
    m, k = z.shape
    n = w.shape[1]
    tm = 256 if m % 256 == 0 else m
    return pl.pallas_call(
        _out_proj_kernel,
        grid=(m // tm,),
        in_specs=[pl.BlockSpec((tm, k), lambda i: (i, 0)),
                  pl.BlockSpec((k, n), lambda i: (0, 0)),
                  pl.BlockSpec((tm, n), lambda i: (i, 0))],
        out_specs=pl.BlockSpec((tm, n), lambda i: (i, 0)),
        out_shape=jax.ShapeDtypeStruct((m, n), jnp.float32),
        compiler_params=pltpu.CompilerParams(dimension_semantics=("parallel",), vmem_limit_bytes=VMEM_LIMIT),
        name="out_proj_residual",
    )(z, w.astype(MXU_DTYPE), x)


def _mem_block_kernel(x_ref, g1_ref, g2_ref, wq_ref, kv_ref, wo_ref, x_out_ref, xn_out_ref):
    d = MEM_HEAD_DIM
    hd = MEM_HEADS * d
    x = x_ref[0]
    xn = _rms(x, g1_ref[...]).astype(wq_ref.dtype)
    q = jnp.dot(xn, wq_ref[...], preferred_element_type=jnp.float32).astype(wq_ref.dtype)
    outs = []
    for h in range(MEM_HEADS):
        k = kv_ref[0, :, h * d:(h + 1) * d].astype(wq_ref.dtype)
        v = kv_ref[0, :, hd + h * d:hd + (h + 1) * d].astype(wq_ref.dtype)
        s = _dot_nt(q[:, h * d:(h + 1) * d], k) * d ** -0.5
        e = jnp.exp(s - jnp.max(s, axis=1, keepdims=True))
        p = e / jnp.sum(e, axis=1, keepdims=True)
        outs.append(jnp.dot(p.astype(v.dtype), v, preferred_element_type=jnp.float32))
    o = jnp.concatenate(outs, axis=1).astype(wo_ref.dtype)
    x2 = x + jnp.dot(o, wo_ref[...], preferred_element_type=jnp.float32)
    x_out_ref[0] = x2
    xn_out_ref[0] = _rms(x2, g2_ref[...]).astype(xn_out_ref.dtype)


def mem_block(x, g1, g2, w_q, kv, w_o):
    b, t, dm = x.shape
    mt, kvw = kv.shape[1:]
    hd = w_q.shape[1]
    tm = 256 if t % 256 == 0 else t
    row_spec = pl.BlockSpec((1, tm, dm), lambda bi, i: (bi, i, 0))
    g_spec = pl.BlockSpec((1, dm), lambda bi, i: (0, 0))
    return pl.pallas_call(
        _mem_block_kernel,
        grid=(b, t // tm),
        in_specs=[row_spec, g_spec, g_spec,
                  pl.BlockSpec((dm, hd), lambda bi, i: (0, 0)),
                  pl.BlockSpec((1, mt, kvw), lambda bi, i: (bi, 0, 0)),
                  pl.BlockSpec((hd, dm), lambda bi, i: (0, 0))],
        out_specs=[row_spec, row_spec],
        out_shape=[jax.ShapeDtypeStruct((b, t, dm), jnp.float32), jax.ShapeDtypeStruct((b, t, dm), MXU_DTYPE)],
        compiler_params=pltpu.CompilerParams(
            dimension_semantics=("parallel", "parallel"), vmem_limit_bytes=VMEM_LIMIT),
        name="mem_block",
    )(x, g1.reshape(1, dm).astype(jnp.float32), g2.reshape(1, dm).astype(jnp.float32), w_q.astype(MXU_DTYPE),
      kv, w_o.astype(MXU_DTYPE))


def _ffn_up_kernel(x_ref, halo_ref, wg_ref, wu_ref, cwg_ref, cwu_ref, cbg_ref, cbu_ref, pg_ref, pu_ref,
                   h_ref, sg_ref, su_ref, *, nb, tm, t_real):
    i = pl.program_id(2)
    n_halo = halo_ref.shape[1]
    tn = wg_ref.shape[1]
    rows = nb * tm
    x = x_ref[...].reshape(rows, x_ref.shape[2])
    if nb == 1:
        x = jnp.concatenate([halo_ref[0], x], axis=0)
    row = _iota((nb, tm, 1), 1).reshape(rows, 1)
    last = (t_real - 1) // tm
    r_last = (t_real - 1) % tm
    spread = lambda p: jnp.broadcast_to(p, (nb, tm, tn)).reshape(rows, tn)

    def branch(w_ref, cw_ref, cb_ref, p_ref, s_ref):
        u = jnp.dot(x, w_ref[...], preferred_element_type=jnp.float32)
        p0, p1 = p_ref[:, 0:1, :], p_ref[:, 1:2, :]
        if nb == 1:
            uh, u = u[:n_halo], u[n_halo:]
            p0 = jnp.where(i == 0, p0, uh[n_halo - 2:n_halo - 1][None])
            p1 = jnp.where(i == 0, p1, uh[n_halo - 1:n_halo][None])
        p0, p1 = spread(p0), spread(p1)
        u1 = jnp.where(row == 0, p1, pltpu.roll(u, 1, 0))
        u2 = jnp.where(row == 0, p0, jnp.where(row == 1, p1, pltpu.roll(u, 2, 0)))

        @pl.when(i == last)
        def _():
            s_ref[...] = u.reshape(nb, tm, tn)[:, r_last - 1:r_last + 1, :]

        return cb_ref[...] + u2 * cw_ref[0:1, :] + u1 * cw_ref[1:2, :] + u * cw_ref[2:3, :]

    gate = branch(wg_ref, cwg_ref, cbg_ref, pg_ref, sg_ref)
    up = branch(wu_ref, cwu_ref, cbu_ref, pu_ref, su_ref)
    h_ref[...] = (jax.nn.silu(gate) * up).reshape(nb, tm, tn).astype(h_ref.dtype)


def ffn_up(xn, prev, w_up, conv_w, conv_b, t_real):
    b, t, dm = xn.shape
    f2 = w_up.shape[1]
    f = f2 // 2
    tm = 512 if t % 512 == 0 else t
    nb = 1 if t > tm else max(1, min(b, 512 // tm))
    while b % nb:
        nb -= 1
    tn = _tile(f, 512)
    nf = f // tn
    halo = min(BF16_ROWS, tm)
    hpt = tm // halo
    assert t_real >= 2 and (t_real - 1) % tm >= 1 and tm % SUBLANE == 0
    w_up = w_up.astype(MXU_DTYPE)
    conv_b = conv_b.reshape(1, f2)
    col = lambda off: (lambda j, bi, i: (0, j + off))
    st = lambda off: (lambda j, bi, i: (bi, 0, j + off))
    specs = [pl.BlockSpec((nb, tm, dm), lambda j, bi, i: (bi, i, 0)),
             pl.BlockSpec((1, halo, dm), lambda j, bi, i: (bi * nb, jnp.maximum(i * hpt - 1, 0), 0)),
             pl.BlockSpec((dm, tn), col(0)), pl.BlockSpec((dm, tn), col(nf)),
             pl.BlockSpec((CONV_WIDTH, tn), col(0)), pl.BlockSpec((CONV_WIDTH, tn), col(nf)),
             pl.BlockSpec((1, tn), col(0)), pl.BlockSpec((1, tn), col(nf)),
             pl.BlockSpec((nb, 2, tn), st(0)), pl.BlockSpec((nb, 2, tn), st(nf))]
    h, sg, su = pl.pallas_call(
        functools.partial(_ffn_up_kernel, nb=nb, tm=tm, t_real=t_real),
        grid=(nf, b // nb, t // tm),
        in_specs=specs,
        out_specs=[pl.BlockSpec((nb, tm, tn), lambda j, bi, i: (bi, i, j)),
                   pl.BlockSpec((nb, 2, tn), st(0)), pl.BlockSpec((nb, 2, tn), st(0))],
        out_shape=[jax.ShapeDtypeStruct((b, t, f), MXU_DTYPE), jax.ShapeDtypeStruct((b, 2, f), jnp.float32),
                   jax.ShapeDtypeStruct((b, 2, f), jnp.float32)],
        compiler_params=pltpu.CompilerParams(
            dimension_semantics=("parallel", "parallel", "arbitrary"), vmem_limit_bytes=VMEM_LIMIT),
        name="ffn_up",
    )(xn, xn, w_up, w_up, conv_w, conv_w, conv_b, conv_b, prev, prev)
    return h, jnp.concatenate([sg, su], axis=-1)


def _ffn_down_kernel(h_ref, w_ref, x_ref, g_ref, o_ref, acc_ref):
    @pl.when(pl.program_id(1) == 0)
    def _():
        acc_ref[...] = x_ref[...]

    acc_ref[...] += jnp.dot(h_ref[...], w_ref[...], preferred_element_type=jnp.float32)

    @pl.when(pl.program_id(1) == pl.num_programs(1) - 1)
    def _():
        o_ref[...] = _rms(acc_ref[...], g_ref[...])


def ffn_down_norm(h, w_down, x, g):
    m, f = h.shape
    dm = w_down.shape[1]
    tm = 512 if m % 512 == 0 else m
    tk = _tile(f, 2048)
    return pl.pallas_call(
        _ffn_down_kernel,
        grid=(m // tm, f // tk),
        in_specs=[pl.BlockSpec((tm, tk), lambda i, l: (i, l)),
                  pl.BlockSpec((tk, dm), lambda i, l: (l, 0)),
                  pl.BlockSpec((tm, dm), lambda i, l: (i, 0)),
                  pl.BlockSpec((1, dm), lambda i, l: (0, 0))],
        out_specs=pl.BlockSpec((tm, dm), lambda i, l: (i, 0)),
        out_shape=jax.ShapeDtypeStruct((m, dm), jnp.float32),
        scratch_shapes=[pltpu.VMEM((tm, dm), jnp.float32)],
        compiler_params=pltpu.CompilerParams(
            dimension_semantics=("parallel", "arbitrary"), vmem_limit_bytes=VMEM_LIMIT),
        name="ffn_down_norm",
    )(h, w_down.astype(MXU_DTYPE), x, g.reshape(1, dm).astype(jnp.float32))


def dense_tail(x, o_nsa, o_dsa, gm, kv_mem, prev_u, t_real, g_mem, g_ffn, g_final, w_oa, w_ob, w_o, w_mq, w_mo,
               w_up, conv_w, conv_b, w_down):
    b, t, dm = x.shape
    rows = lambda a: a.reshape(b * t, a.shape[-1])
    z = gated_merge(rows(o_nsa), rows(o_dsa), rows(gm), w_oa, w_ob)
    x1 = out_proj_residual(z, w_o, rows(x)).reshape(b, t, dm)
    x2, xn2 = mem_block(x1, g_mem, g_ffn, w_mq, kv_mem, w_mo)
    h, state = ffn_up(xn2, prev_u, w_up, conv_w, conv_b, t_real)
    y = ffn_down_norm(rows(h), w_down, rows(x2), g_final)
    return y.reshape(b, t, dm), state


def _dot_nt(a, b):
    return lax.dot_general(a, b, (((1,), (1,)), ((), ())), preferred_element_type=jnp.float32)


def _iota(shape, dim):
    return lax.broadcasted_iota(jnp.int32, shape, dim)


def _flash_init(rows, d):
    return (jnp.full((rows, 1), NEG_INF, jnp.float32), jnp.zeros((rows, 1), jnp.float32),
            jnp.zeros((rows, d), jnp.float32))


def _flash_step(carry, q, k, v, madd, nh, scale=None, kv_t=False):
    m, l, acc = carry
    s = jnp.dot(q, k, preferred_element_type=jnp.float32) if kv_t else _dot_nt(q, k)
    if scale is not None:
        s = s * scale
    r, kb = s.shape
    s = (s.reshape(nh, r // nh, kb) + madd[None]).reshape(r, kb)
    m_new = jnp.maximum(m, jnp.max(s, axis=1, keepdims=True))
    m_safe = jnp.where(m_new == NEG_INF, 0.0, m_new)
    p = jnp.exp(s - m_safe)
    alpha = jnp.exp(m - m_safe)
    l = alpha * l + jnp.sum(p, axis=1, keepdims=True)
    pv = _dot_nt(p.astype(v.dtype), v) if kv_t else jnp.dot(p.astype(v.dtype), v, preferred_element_type=jnp.float32)
    return m_new, l, alpha * acc + pv


def _flash_finish(carry):
    _, l, acc = carry
    return acc / jnp.maximum(l, 1e-30)


def _split_dot(x, m01):
    hi = x.astype(jnp.bfloat16)
    r1 = x - hi.astype(jnp.float32)
    mid = r1.astype(jnp.bfloat16)
    lo = (r1 - mid.astype(jnp.float32)).astype(jnp.bfloat16)
    dot = functools.partial(jnp.dot, preferred_element_type=jnp.float32)
    return dot(hi, m01) + dot(mid, m01) + dot(lo, m01)


def _nsa_group(g, i, qc_ref, qr_ref, gate_ref, kc_ref, vc_ref, ks_ref, vs_ref, kw_ref, vw_ref, mask_ref,
               *, tq, kb, qpos0, wbase, ns, n_sel, kv_t):
    nh, d = HPG_A, HEAD_DIM_A
    rows = nh * tq
    nc = kc_ref.shape[3]
    nsp = _round_up(ns, LANE)
    t0 = qpos0 + i * tq
    nj = (t0 + tq - 1) // kb + 1
    cols = slice(g * d, (g + 1) * d)

    def stack_heads(q_ref):
        return jnp.concatenate([q_ref[0, :, (g * nh + h) * d:(g * nh + h + 1) * d] for h in range(nh)], axis=0)

    qc = stack_heads(qc_ref)
    qr = stack_heads(qr_ref)
    tpos = t0 + _iota((tq, 1), 0)

    blk_last = _iota((1, nc), 1) * CMP_STRIDE + (CMP_BLOCK - 1)
    madd_c = jnp.where(blk_last <= tpos, 0.0, NEG_INF)
    s = _dot_nt(qc, kc_ref[0, 0, g]).reshape(nh, tq, nc) + madd_c[None]
    m = jnp.max(s, axis=2, keepdims=True)
    m = jnp.where(m == NEG_INF, 0.0, m)
    e = jnp.exp(s - m)
    p = e / jnp.maximum(jnp.sum(e, axis=2, keepdims=True), 1e-30)
    o_cmp = jnp.dot(p.reshape(rows, nc).astype(vc_ref.dtype), vc_ref[0, 0, g], preferred_element_type=jnp.float32)

    imp = jnp.sum(p, axis=0)
    c_id = _iota((nc, nsp), 0)
    m_id = _iota((nc, nsp), 1)
    overlap = (jnp.right_shift(c_id, CMP_PER_SLC_SHIFT) == m_id) | (c_id == m_id * CMP_PER_SLC - 1)
    score = _split_dot(imp, overlap.astype(jnp.bfloat16))
    blk = _iota((1, nsp), 1)
    cur = jnp.right_shift(tpos, SLC_SHIFT)
    forced = (blk == 0) | (blk == cur) | (blk == cur - 1)
    sc = jnp.where(forced, POS_INF, jnp.where(blk * SLC_BLOCK <= tpos, score, NEG_INF))
    if tq % LANE == 0 and nsp % LANE == 0:
        sc_t = sc.T
        blk_t = _iota((nsp, 1), 0)
        rank_t = jnp.zeros((nsp, tq), jnp.float32)
        for mp in range(ns):
            ref = sc_t[mp:mp + 1, :]
            beats = (ref > sc_t) | ((ref == sc_t) & (blk_t > mp))
            rank_t = rank_t + jnp.where(beats, 1.0, 0.0)
        sel = jnp.where(rank_t < n_sel, 1.0, 0.0).T.astype(jnp.bfloat16)
    else:
        rank = jnp.zeros((tq, nsp), jnp.float32)
        for mp in range(ns):
            col = sc[:, mp:mp + 1]
            beats = (col > sc) | ((col == sc) & (blk > mp))
            rank = rank + jnp.where(beats, 1.0, 0.0)
        sel = jnp.where(rank < n_sel, 1.0, 0.0).astype(jnp.bfloat16)

    def make_mask(j, _):
        kpos = j * kb + _iota((1, kb), 1)
        expand = (jnp.right_shift(j * kb + _iota((nsp, kb), 1), SLC_SHIFT) == _iota((nsp, kb), 0))
        hit = jnp.dot(sel, expand.astype(jnp.bfloat16), preferred_element_type=jnp.float32)
        mask_ref[j] = jnp.where((hit > 0.5) & (kpos <= tpos), 0.0, NEG_INF)
        return 0

    lax.fori_loop(0, nj, make_mask, 0)

    def slc_body(j, carry):
        off = pl.multiple_of(j * kb, kb)
        return _flash_step(carry, qr, ks_ref[0, pl.ds(off, kb), cols], vs_ref[0, pl.ds(off, kb), cols],
                           mask_ref[j], nh)

    if kv_t:
        o_slc = _flash_finish(_flash_step(_flash_init(rows, d), qr, ks_ref[0, cols, :], vs_ref[0, cols, :],
                                          mask_ref[0], nh, kv_t=True))
    else:
        o_slc = _flash_finish(lax.fori_loop(0, nj, slc_body, _flash_init(rows, d)))

    wk = min(_round_up(WINDOW + tq, WIN_CHUNK), kw_ref.shape[1])
    first = jnp.maximum(t0 - (WINDOW - 1) - wbase, 0) // WIN_CHUNK * WIN_CHUNK
    off = pl.multiple_of(jnp.minimum(first, kw_ref.shape[1] - wk), WIN_CHUNK)
    dist = tpos - (wbase + off + _iota((1, wk), 1))
    madd_w = jnp.where((dist >= 0) & (dist < WINDOW), 0.0, NEG_INF)
    o_win = _flash_finish(_flash_step(_flash_init(rows, d), qr, kw_ref[0, pl.ds(off, wk), cols],
                                      vw_ref[0, pl.ds(off, wk), cols], madd_w, nh))

    gates = gate_ref[0]
    outs = []
    for h in range(nh):
        rs = slice(h * tq, (h + 1) * tq)
        c = g * nh + h
        outs.append(gates[:, c:c + 1] * o_cmp[rs] + gates[:, N_HEADS_A + c:N_HEADS_A + c + 1] * o_slc[rs]
                    + gates[:, 2 * N_HEADS_A + c:2 * N_HEADS_A + c + 1] * o_win[rs])
    return outs


def _nsa_kernel(qc_ref, qr_ref, gate_ref, kc_ref, vc_ref, ks_ref, vs_ref, kw_ref, vw_ref, o_ref, mask_ref, **kw):
    i = pl.program_id(1)
    outs = []
    for g in range(KV_GROUPS_A):
        outs += _nsa_group(g, i, qc_ref, qr_ref, gate_ref, kc_ref, vc_ref, ks_ref, vs_ref, kw_ref, vw_ref,
                           mask_ref, **kw)
    o_ref[0] = jnp.concatenate(outs, axis=1).astype(o_ref.dtype)


def nsa_attention(qc, qr, gates, kvc, ks, vs, kw, vw, *, tq, qpos0, wbase, n_keys, kv_t=False):
    b, t, hd = qc.shape
    l = ks.shape[2] if kv_t else ks.shape[1]
    lw = kw.shape[1]
    nc, d = kvc.shape[3:]
    kb = min(KEY_CHUNK, l) if tq >= TQ else l
    assert l % kb == 0 and t % tq == 0 and lw % WIN_CHUNK == 0 and (kb == l or not kv_t)
    assert (qpos0 + t - 1) // kb + 1 <= l // kb and (qpos0 + t - 1 - wbase) // WIN_CHUNK + 1 <= lw // WIN_CHUNK
    assert (qpos0 - wbase) % WIN_CHUNK == 0 and WIN_CHUNK % tq == 0 and tq > 1
    ns = l // SLC_BLOCK
    n_sel = min(N_SELECT, -(-n_keys // SLC_BLOCK))
    q_spec = pl.BlockSpec((1, tq, hd), lambda bi, i: (bi, i, 0))
    kc_spec = pl.BlockSpec((1, 1, KV_GROUPS_A, nc, d), lambda bi, i: (0, bi, 0, 0, 0))
    vc_spec = pl.BlockSpec((1, 1, KV_GROUPS_A, nc, d), lambda bi, i: (1, bi, 0, 0, 0))
    k_spec = pl.BlockSpec((1, KV_A, l) if kv_t else (1, l, KV_A), lambda bi, i: (bi, 0, 0))
    w_spec = pl.BlockSpec((1, lw, KV_A), lambda bi, i: (bi, 0, 0))
    return pl.pallas_call(
        functools.partial(_nsa_kernel, tq=tq, kb=kb, qpos0=qpos0, wbase=wbase, ns=ns, n_sel=n_sel, kv_t=kv_t),
        grid=(b, t // tq),
        in_specs=[q_spec, q_spec, pl.BlockSpec((1, tq, 3 * N_HEADS_A), lambda bi, i: (bi, i, 0)),
                  kc_spec, vc_spec, k_spec, k_spec, w_spec, w_spec],
        out_specs=q_spec,
        out_shape=jax.ShapeDtypeStruct((b, t, hd), MXU_DTYPE),
        scratch_shapes=[pltpu.VMEM((l // kb, tq, kb), jnp.float32)],
        compiler_params=pltpu.CompilerParams(
            dimension_semantics=("parallel", "arbitrary"), vmem_limit_bytes=VMEM_LIMIT),
        name="nsa_attention",
    )(qc, qr, gates, kvc, kvc, ks, vs, kw, vw)


def _dsa_kernel(qb_ref, qi_ref, wi_ref, kb_ref, vb_ref, ki_ref, o_ref, score_ref, *, tq, kb, qpos0, n_keep, kv_t):
    i = pl.program_id(1)
    nh, d = N_HEADS_B, HEAD_DIM_B
    t0 = qpos0 + i * tq
    nj = (t0 + tq - 1) // kb + 1
    tpos = t0 + _iota((tq, 1), 0)
    w = wi_ref[0]

    def idx_body(j, carry):
        lo, hi = carry
        off = pl.multiple_of(j * kb, kb)
        kidx = ki_ref[0] if kv_t else ki_ref[0, pl.ds(off, kb), :]
        acc = jnp.zeros((tq, kb), jnp.float32)
        heads = [qi_ref[0, :, h * IDX_DIM:(h + 1) * IDX_DIM] for h in range(IDX_HEADS)]
        if kv_t:
            dots_all = jnp.dot(jnp.concatenate(heads, axis=0), kidx, preferred_element_type=jnp.float32)
        for h in range(IDX_HEADS):
            dots = dots_all[h * tq:(h + 1) * tq] if kv_t else _dot_nt(heads[h], kidx)
            acc = acc + w[:, h:h + 1] * jnp.maximum(dots, 0.0)
        vis = (off + _iota((1, kb), 1)) <= tpos
        score_ref[j] = jnp.where(vis, acc, NEG_INF)
        lo = jnp.minimum(lo, jnp.min(jnp.where(vis, acc, POS_INF), axis=1, keepdims=True))
        hi = jnp.maximum(hi, jnp.max(jnp.where(vis, acc, NEG_INF), axis=1, keepdims=True))
        return lo, hi

    lo, hi = lax.fori_loop(0, nj, idx_body, (jnp.full((tq, 1), POS_INF, jnp.float32),
                                             jnp.full((tq, 1), NEG_INF, jnp.float32)))

    k = float(n_keep)
    n_vis = (tpos + 1).astype(jnp.float32)

    def unsettled(cnt_lo):
        return jnp.max(jnp.where((cnt_lo != k) & (n_vis > k), 1.0, 0.0)) > 0.0

    def bisect(state):
        it, lo, hi, cnt_lo = state
        mid = 0.5 * (lo + hi)
        mid_b = jnp.broadcast_to(mid, (tq, LANE))

        def count(j, acc):
            sc = score_ref[j]
            for c in range(kb // LANE):
                acc = acc + jnp.where(sc[:, c * LANE:(c + 1) * LANE] >= mid_b, 1.0, 0.0)
            return acc

        cnt = jnp.sum(lax.fori_loop(0, nj, count, jnp.zeros((tq, LANE), jnp.float32)), axis=1, keepdims=True)
        ge = cnt >= k
        return it + 1, jnp.where(ge, mid, lo), jnp.where(ge, hi, mid), jnp.where(ge, cnt, cnt_lo)

    _, thr, _, _ = lax.while_loop(lambda st: (st[0] < BISECT_ITERS) & unsettled(st[3]), bisect,
                                  (jnp.int32(0), lo, hi, n_vis))

    q = jnp.concatenate([qb_ref[0, :, h * d:(h + 1) * d] for h in range(nh)], axis=0)

    def att_body(j, carry):
        off = pl.multiple_of(j * kb, kb)
        madd = jnp.where(score_ref[j] >= thr, 0.0, NEG_INF)
        if kv_t:
            return _flash_step(carry, q, kb_ref[0], vb_ref[0], madd, nh, scale=d ** -0.5, kv_t=True)
        return _flash_step(carry, q, kb_ref[0, pl.ds(off, kb), :], vb_ref[0, pl.ds(off, kb), :], madd, nh,
                           scale=d ** -0.5)

    o = _flash_finish(lax.fori_loop(0, nj, att_body, _flash_init(nh * tq, d)))
    o_ref[0] = jnp.concatenate([o[h * tq:(h + 1) * tq] for h in range(nh)], axis=1).astype(o_ref.dtype)


def dsa_attention(qb, qi, wi, kb_, vb, ki, *, tq, qpos0, n_keep, kv_t=False):
    b, t, hd = qb.shape
    l = kb_.shape[2] if kv_t else kb_.shape[1]
    kb = min(KEY_CHUNK, l) if tq >= TQ else l
    assert l % kb == 0 and t % tq == 0 and (qpos0 + t - 1) // kb + 1 <= l // kb and (kb == l or not kv_t)
    kv_spec = lambda width: pl.BlockSpec((1, width, l) if kv_t else (1, l, width), lambda bi, i: (bi, 0, 0))
    return pl.pallas_call(
        functools.partial(_dsa_kernel, tq=tq, kb=kb, qpos0=qpos0, n_keep=n_keep, kv_t=kv_t),
        grid=(b, t // tq),
        in_specs=[pl.BlockSpec((1, tq, hd), lambda bi, i: (bi, i, 0)),
                  pl.BlockSpec((1, tq, IDX_HEADS * IDX_DIM), lambda bi, i: (bi, i, 0)),
                  pl.BlockSpec((1, tq, IDX_HEADS), lambda bi, i: (bi, i, 0)),
                  kv_spec(HEAD_DIM_B), kv_spec(HEAD_DIM_B), kv_spec(IDX_DIM)],
        out_specs=pl.BlockSpec((1, tq, hd), lambda bi, i: (bi, i, 0)),
        out_shape=jax.ShapeDtypeStruct((b, t, hd), MXU_DTYPE),
        scratch_shapes=[pltpu.VMEM((l // kb, tq, kb), jnp.float32)],
        compiler_params=pltpu.CompilerParams(
            dimension_semantics=("parallel", "arbitrary"), vmem_limit_bytes=VMEM_LIMIT),
        name="dsa_attention",
    )(qb, qi, wi, kb_, vb, ki)


def _page_maps(n_pages, pp):
    n_steps = n_pages // pp

    def page_map(r):
        return lambda b, s, pt: (pt[b * n_pages + jnp.minimum(s, n_steps - 1) * pp + r], 0, 0)

    return n_steps, page_map


def _chunk_rows(tok_ref, sec, row0, n):
    d = HEAD_DIM_A
    first_half = _iota((n, KV_A), 1) < d
    pieces = [[] for _ in range(KV_GROUPS_A)]
    for j in range(0, CMP_STRIDE, 2):
        a, b = [tok_ref[sec, pl.ds(row0 + jj, n, stride=CMP_STRIDE), :] for jj in (j, j + 1)]
        pieces[0].append(jnp.where(first_half, a, pltpu.roll(b, d, 1)))
        pieces[1].append(jnp.where(first_half, pltpu.roll(a, d, 1), b))
    return [jnp.concatenate(p, axis=1) for p in pieces]


def _nsa_gather_kernel(pt_ref, *refs, pp, n_steps, rows):
    del pt_ref
    pages, (tail_tok_ref, tail_t_ref, pe_ref) = refs[:pp], refs[pp:pp + 3]
    zt_ref, zb_ref, ks_ref, vs_ref, tok_ref = refs[pp + 3:]
    is_tail = pl.program_id(1) == n_steps
    z = [[[] for _ in range(KV_GROUPS_A)] for _ in range(2)]
    for r in range(pp):
        cs = slice(r * rows, (r + 1) * rows)
        ks_ref[0, :, cs] = jnp.where(is_tail, tail_t_ref[0, :KV_A, cs],
                                     pages[r][0, 2 * KV_A:3 * KV_A, :]).astype(ks_ref.dtype)
        vs_ref[0, :, cs] = jnp.where(is_tail, tail_t_ref[0, KV_A:, cs],
                                     pages[r][0, 3 * KV_A:, :]).astype(vs_ref.dtype)
        for sec in range(2):
            tok_ref[sec, cs, :] = jnp.where(is_tail, tail_tok_ref[0, sec, cs, :],
                                            pages[r][0, sec * KV_A:(sec + 1) * KV_A, :].T)
            for g, zg in enumerate(_chunk_rows(tok_ref, sec, r * rows, rows // CMP_STRIDE)):
                z[sec][g].append(zg)
    for sec in range(2):
        for g in range(KV_GROUPS_A):
            zf = jnp.concatenate(z[sec][g], axis=0)
            zt_ref[sec, 0, g] = (zf + pe_ref[sec, 0]).astype(zt_ref.dtype)
            zb_ref[sec, 0, g] = (zf + pe_ref[sec, 1]).astype(zb_ref.dtype)


def nsa_gather(cache_t, page_table, new_rows, cmp_pe):
    db, n_pages = page_table.shape
    width, rows = cache_t.shape[1:]
    pp = PAGES_PER_STEP
    n_steps, page_map = _page_maps(n_pages, pp)
    l = (n_steps + 1) * pp * rows
    cps = pp * rows // CMP_STRIDE
    flat = CMP_STRIDE * HEAD_DIM_A
    pe = cmp_pe.reshape(2, 2, 1, flat).astype(jnp.float32)
    tail = pad_rows(new_rows, pp * rows)
    tail_tok = tail[:, :, :2 * KV_A].reshape(db, pp * rows, 2, KV_A).swapaxes(1, 2)
    tail_t = tail[:, :, 2 * KV_A:].swapaxes(1, 2)
    z_spec = pl.BlockSpec((2, 1, KV_GROUPS_A, cps, flat), lambda b, s, pt: (0, b, 0, s, 0))
    r_spec = pl.BlockSpec((1, KV_A, pp * rows), lambda b, s, pt: (b, 0, s))
    z_shape = jax.ShapeDtypeStruct((2, db, KV_GROUPS_A, l // CMP_STRIDE, flat), MXU_DTYPE)
    r_shape = jax.ShapeDtypeStruct((db, KV_A, l), MXU_DTYPE)
    return pl.pallas_call(
        functools.partial(_nsa_gather_kernel, pp=pp, n_steps=n_steps, rows=rows),
        grid_spec=pltpu.PrefetchScalarGridSpec(
            num_scalar_prefetch=1,
            grid=(db, n_steps + 1),
            in_specs=[pl.BlockSpec((1, width, rows), page_map(r)) for r in range(pp)]
            + [pl.BlockSpec((1, 2, pp * rows, KV_A), lambda b, s, pt: (b, 0, 0, 0)),
               pl.BlockSpec((1, 2 * KV_A, pp * rows), lambda b, s, pt: (b, 0, 0)),
               pl.BlockSpec((2, 2, 1, flat), lambda b, s, pt: (0, 0, 0, 0))],
            out_specs=[z_spec, z_spec, r_spec, r_spec],
            scratch_shapes=[pltpu.VMEM((2, pp * rows, KV_A), jnp.float32)]),
        out_shape=[z_shape, z_shape, r_shape, r_shape],
        compiler_params=pltpu.CompilerParams(
            dimension_semantics=("parallel", "arbitrary"), vmem_limit_bytes=VMEM_LIMIT),
        name="nsa_gather",
    )(page_table.reshape(-1), *([cache_t] * pp), tail_tok, tail_t, pe)


def _dsa_gather_kernel(pt_ref, *refs, pp, n_steps, rows):
    del pt_ref
    pages, tail_ref, (k_ref, v_ref, i_ref) = refs[:pp], refs[pp], refs[pp + 1:]
    is_tail = pl.program_id(1) == n_steps
    d = HEAD_DIM_B
    for r in range(pp):
        cs = slice(r * rows, (r + 1) * rows)
        x = jnp.where(is_tail, tail_ref[0, :, cs], pages[r][0])
        k_ref[0, :, cs] = x[:d].astype(k_ref.dtype)
        v_ref[0, :, cs] = x[d:2 * d].astype(v_ref.dtype)
        i_ref[0, :, cs] = x[2 * d:].astype(i_ref.dtype)


def dsa_gather(cache_t, page_table, new_rows):
    db, n_pages = page_table.shape
    width, rows = cache_t.shape[1:]
    pp = PAGES_PER_STEP
    n_steps, page_map = _page_maps(n_pages, pp)
    l = (n_steps + 1) * pp * rows
    tail_t = pad_rows(new_rows, pp * rows).swapaxes(1, 2)
    out_spec = lambda w: pl.BlockSpec((1, w, pp * rows), lambda b, s, pt: (b, 0, s))
    widths = (HEAD_DIM_B, HEAD_DIM_B, IDX_DIM)
    return pl.pallas_call(
        functools.partial(_dsa_gather_kernel, pp=pp, n_steps=n_steps, rows=rows),
        grid_spec=pltpu.PrefetchScalarGridSpec(
            num_scalar_prefetch=1,
            grid=(db, n_steps + 1),
            in_specs=[pl.BlockSpec((1, width, rows), page_map(r)) for r in range(pp)]
            + [pl.BlockSpec((1, width, pp * rows), lambda b, s, pt: (b, 0, 0))],
            out_specs=[out_spec(w) for w in widths]),
        out_shape=[jax.ShapeDtypeStruct((db, w, l), MXU_DTYPE) for w in widths],
        compiler_params=pltpu.CompilerParams(
            dimension_semantics=("parallel", "arbitrary"), vmem_limit_bytes=VMEM_LIMIT),
        name="dsa_gather",
    )(page_table.reshape(-1), *([cache_t] * pp), tail_t)


def _compress_kernel(zt_ref, zb_ref, pe_ref, w1t_ref, w1b_ref, b1_ref, w2_ref, o_ref, ab_ref, *, ncp):
    dot = functools.partial(jnp.dot, preferred_element_type=jnp.float32)
    ch = zb_ref.shape[2]
    at = dot(zt_ref[0, 0], w1t_ref[0])
    ab_ref[:ch] = dot(zb_ref[0, 0], w1b_ref[0])
    pe_rows = jnp.broadcast_to(pe_ref[0, 1], (SUBLANE, pe_ref.shape[3])).astype(zb_ref.dtype)
    ab_ref[ch:] = dot(pe_rows, w1b_ref[0])
    h = jax.nn.gelu(at[:ncp] + ab_ref[pl.ds(1, ncp), :] + b1_ref[0])
    o_ref[0, 0, :ncp] = dot(h.astype(w2_ref.dtype), w2_ref[0]).astype(o_ref.dtype)
    if o_ref.shape[2] > ncp:
        o_ref[0, 0, ncp:] = jnp.zeros((o_ref.shape[2] - ncp, o_ref.shape[3]), o_ref.dtype)


def compress(zt, zb, cmp_pe, w1, b1, w2, n_keys):
    ncp = _round_up(-(-n_keys // CMP_STRIDE), BF16_ROWS)
    ncl = ncp if ncp <= LANE else _round_up(ncp, LANE)
    _, nb, ch, kdim = zt.shape
    hid = w1.shape[-1]
    d = w2.shape[-1]
    assert ch + SUBLANE >= ncp + 1 and ch % SUBLANE == 0
    w1 = w1.astype(MXU_DTYPE)
    pe = cmp_pe.reshape(2, 2, 1, kdim).astype(jnp.float32)
    z_spec = pl.BlockSpec((1, 1, ch, kdim), lambda s, n: (s, n, 0, 0))
    return pl.pallas_call(
        functools.partial(_compress_kernel, ncp=ncp),
        grid=(2, nb),
        in_specs=[z_spec, z_spec,
                  pl.BlockSpec((1, 2, 1, kdim), lambda s, n: (s, 0, 0, 0)),
                  pl.BlockSpec((1, kdim, hid), lambda s, n: (s, 0, 0)),
                  pl.BlockSpec((1, kdim, hid), lambda s, n: (s, 1, 0)),
                  pl.BlockSpec((1, 1, hid), lambda s, n: (s, 0, 0)),
                  pl.BlockSpec((1, hid, d), lambda s, n: (s, 0, 0))],
        out_specs=pl.BlockSpec((1, 1, ncl, d), lambda s, n: (s, n, 0, 0)),
        out_shape=jax.ShapeDtypeStruct((2, nb, ncl, d), MXU_DTYPE),
        scratch_shapes=[pltpu.VMEM((ch + SUBLANE, hid), jnp.float32)],
        compiler_params=pltpu.CompilerParams(
            dimension_semantics=("parallel", "parallel"), vmem_limit_bytes=VMEM_LIMIT),
        name="compress",
    )(zt, zb, pe, w1, w1, b1.reshape(2, 1, hid).astype(jnp.float32), w2.astype(MXU_DTYPE))


def pad_rows(a, n):
    return jnp.pad(a, [(0, 0), (0, n - a.shape[1])] + [(0, 0)] * (a.ndim - 2))


def _rope_tables(pos, d):
    half = d // ROT_FRACTION // 2
    inv = ROPE_THETA ** (-jnp.arange(half, dtype=jnp.float32) / half)
    ang = pos.astype(jnp.float32)[:, None] * inv[None, :]
    lane = jnp.arange(LANE) % d
    cos = jnp.cos(ang)[:, lane % half]
    sin = jnp.sin(ang)[:, lane % half]
    one, zero = jnp.ones_like(cos), jnp.zeros_like(cos)
    c = jnp.where(lane < 2 * half, cos, one)
    sa = jnp.where((lane >= half) & (lane < 2 * half), sin, zero)
    sb = jnp.where(lane < half, -sin, zero)
    return jnp.stack([c, sa, sb])


def _rope(x, t_ref, half):
    c, sa, sb = t_ref[0], t_ref[1], t_ref[2]
    outs = []
    for j in range(x.shape[1] // LANE):
        xs = x[:, j * LANE:(j + 1) * LANE]
        outs.append(xs * c + pltpu.roll(xs, half, 1) * sa + pltpu.roll(xs, LANE - half, 1) * sb)
    return outs[0] if len(outs) == 1 else jnp.concatenate(outs, axis=1)


_QA = N_HEADS_A * HEAD_DIM_A
_KVA = 6 * KV_A
_QB = N_HEADS_B * HEAD_DIM_B
_KVB = 2 * HEAD_DIM_B
_QI = IDX_HEADS * IDX_DIM
_MISC = IDX_DIM + IDX_HEADS + 3 * N_HEADS_A
PREP_WIDTH = _QA + _KVA + _QB + _KVB + _QI + _MISC
assert _MISC == LANE


def _prep_kernel(x_ref, t64_ref, t128_ref, pe_ref, qc_ref, qr_ref, qb_ref, qi_ref, nsa_ref, win_ref, dsa_ref,
                 ks_ref, vs_ref, kw_ref, vw_ref, kb_ref, vb_ref, ki_ref, wi_ref, gate_ref, *z_refs, tm):
    x = x_ref[...]
    o = 0
    qa = x[:, o:o + _QA]; o += _QA
    kva = x[:, o:o + _KVA]; o += _KVA
    qb = x[:, o:o + _QB]; o += _QB
    kvb = x[:, o:o + _KVB]; o += _KVB
    qi = x[:, o:o + _QI]; o += _QI
    misc = x[:, o:o + _MISC]
    h64, h128 = HEAD_DIM_A // ROT_FRACTION // 2, HEAD_DIM_B // ROT_FRACTION // 2
    sec = lambda i: kva[:, i * KV_A:(i + 1) * KV_A]
    mx = lambda a: a.astype(qc_ref.dtype)
    qc_ref[...] = mx(qa * HEAD_DIM_A ** -0.5)
    qr_ref[...] = mx(_rope(qa, t64_ref, h64) * HEAD_DIM_A ** -0.5)
    k_slc, k_win = _rope(sec(2), t64_ref, h64), _rope(sec(4), t64_ref, h64)
    nsa_ref[:, :2 * KV_A] = kva[:, :2 * KV_A]
    nsa_ref[:, 2 * KV_A:3 * KV_A] = k_slc
    nsa_ref[:, 3 * KV_A:] = sec(3)
    win_ref[:, :KV_A] = k_win
    win_ref[:, KV_A:] = sec(5)
    ks_ref[...], vs_ref[...], kw_ref[...], vw_ref[...] = mx(k_slc), mx(sec(3)), mx(k_win), mx(sec(5))
    qb_ref[...] = mx(_rope(qb, t128_ref, h128))
    k_b, v_b = _rope(kvb[:, :HEAD_DIM_B], t128_ref, h128), kvb[:, HEAD_DIM_B:]
    k_idx = _rope(misc, t64_ref, h64)[:, :IDX_DIM]
    dsa_ref[:, :HEAD_DIM_B] = k_b
    dsa_ref[:, HEAD_DIM_B:2 * HEAD_DIM_B] = v_b
    dsa_ref[:, 2 * HEAD_DIM_B:] = k_idx
    kb_ref[...], vb_ref[...], ki_ref[...] = mx(k_b), mx(v_b), mx(k_idx)
    qi_ref[...] = mx(_rope(qi, t64_ref, h64) * IDX_DIM ** -0.5)
    wi_ref[...] = misc[:, IDX_DIM:IDX_DIM + IDX_HEADS] * IDX_HEADS ** -0.5
    gate_ref[...] = jax.nn.sigmoid(misc[:, IDX_DIM + IDX_HEADS:])
    if z_refs:
        zt_ref, zb_ref, tok_ref = z_refs
        for s_ in range(2):
            tok_ref[s_] = sec(s_)
            for g, zg in enumerate(_chunk_rows(tok_ref, s_, 0, tm // CMP_STRIDE)):
                zt_ref[s_, 0, g] = (zg + pe_ref[s_, 0]).astype(zt_ref.dtype)
                zb_ref[s_, 0, g] = (zg + pe_ref[s_, 1]).astype(zb_ref.dtype)


def prep(proj, pos, cmp_pe, with_chunks):
    b, t, width = proj.shape
    assert width == PREP_WIDTH
    m = b * t
    tm = 256 if t % 256 == 0 else m
    assert m % tm == 0 and t % tm in (0, t)
    nt = max(t // tm, 1)
    flat = CMP_STRIDE * HEAD_DIM_A
    names = ['qc', 'qr', 'qb', 'qi', 'nsa_rows', 'win_rows', 'dsa_rows', 'ks', 'vs', 'kw', 'vw', 'kb', 'vb', 'ki',
             'wi', 'gates']
    widths = [_QA, _QA, _QB, _QI, NSA_SECTIONS * KV_A, 2 * KV_A, DSA_ROW, KV_A, KV_A, KV_A, KV_A, HEAD_DIM_B,
              HEAD_DIM_B, IDX_DIM, IDX_HEADS, 3 * N_HEADS_A]
    dtypes = [MXU_DTYPE] * 4 + [jnp.float32] * 3 + [MXU_DTYPE] * 7 + [jnp.float32] * 2
    row = lambda w: pl.BlockSpec((tm, w), lambda i: (i, 0))
    out_specs = [row(w) for w in widths]
    out_shape = [jax.ShapeDtypeStruct((m, w), dt) for w, dt in zip(widths, dtypes)]
    scratch = []
    if with_chunks:
        assert tm % CMP_STRIDE == 0 and t % tm == 0
        z_spec = pl.BlockSpec((2, 1, KV_GROUPS_A, tm // CMP_STRIDE, flat), lambda i: (0, i // nt, 0, i % nt, 0))
        z_shape = jax.ShapeDtypeStruct((2, b, KV_GROUPS_A, t // CMP_STRIDE, flat), MXU_DTYPE)
        out_specs += [z_spec, z_spec]
        out_shape += [z_shape, z_shape]
        names += ['zt', 'zb']
        scratch = [pltpu.VMEM((2, tm, KV_A), jnp.float32)]
    t_spec = pl.BlockSpec((3, tm, LANE), lambda i: (0, i, 0))
    outs = pl.pallas_call(
        functools.partial(_prep_kernel, tm=tm),
        grid=(m // tm,),
        in_specs=[row(width), t_spec, t_spec, pl.BlockSpec((2, 2, 1, flat), lambda i: (0, 0, 0, 0))],
        out_specs=out_specs,
        out_shape=out_shape,
        scratch_shapes=scratch,
        compiler_params=pltpu.CompilerParams(dimension_semantics=("parallel",), vmem_limit_bytes=VMEM_LIMIT),
        name="prep",
    )(proj.reshape(m, width), _rope_tables(pos, HEAD_DIM_A), _rope_tables(pos, HEAD_DIM_B),
      cmp_pe.reshape(2, 2, 1, flat).astype(jnp.float32))
    return {n: (o if o.ndim > 2 else o.reshape(b, t, o.shape[-1])) for n, o in zip(names, outs)}


def project(x, g, w_in, sizes):
    starts = [sum(sizes[:i]) for i in range(len(sizes))]
    qa, kva, ga, qb, kvb, qi, ki, wi, gm = [slice(o, o + n) for o, n in zip(starts, sizes)]
    w_bf = w_in.astype(MXU_DTYPE)
    w_prep = jnp.concatenate([w_bf[:, c] for c in (qa, kva, qb, kvb, qi, ki, wi, ga)], axis=1)
    return norm_matmul(x, g, w_prep), norm_matmul(x, g, w_bf[:, gm])


def mixer_prompt(x, g, w_in, cmp_pe, cmp_w1, cmp_b1, cmp_w2, sizes):
    b, s, _ = x.shape
    proj, gm = project(x, g, w_in, sizes)
    p = prep(proj, jnp.tile(jnp.arange(s, dtype=jnp.int32), b), cmp_pe, True)
    merge_bg = lambda a: a.reshape((2, b * KV_GROUPS_A) + a.shape[3:])
    kvc = compress(merge_bg(p['zt']), merge_bg(p['zb']), cmp_pe, cmp_w1, cmp_b1, cmp_w2, s)
    kvc = kvc.reshape(2, b, KV_GROUPS_A, kvc.shape[2], HEAD_DIM_A)
    o_nsa = nsa_attention(p['qc'], p['qr'], p['gates'], kvc, p['ks'], p['vs'], p['kw'], p['vw'],
                          tq=min(TQ, s), qpos0=0, wbase=0, n_keys=s)
    o_dsa = dsa_attention(p['qb'], p['qi'], p['wi'], p['kb'], p['vb'], p['ki'],
                          tq=min(TQ, s), qpos0=0, n_keep=min(DSA_TOPK, s // 4))
    nsa_rows = p['nsa_rows'].reshape(b, s, NSA_SECTIONS, KV_GROUPS_A, HEAD_DIM_A)
    win_state = p['win_rows'].reshape(b, s, 2, KV_GROUPS_A, HEAD_DIM_A)[:, -min(WINDOW, s):]
    return (o_nsa, o_dsa, gm), nsa_rows, win_state, p['dsa_rows']


def mixer_sample(x, g, cache_nsa, win_buf, cache_dsa, page_table, w_in, cmp_pe, cmp_w1, cmp_b1, cmp_w2, sizes):
    b, t, _ = x.shape
    page = cache_nsa.shape[1]
    past_len = page_table.shape[1] * page
    n_keys = past_len + t
    proj, gm = project(x, g, w_in, sizes)
    p = prep(proj, jnp.tile(past_len + jnp.arange(t, dtype=jnp.int32), b), cmp_pe, False)
    pad = lambda a: pad_rows(a, TQ_STEP)

    cache_nsa_t = jnp.transpose(cache_nsa, (0, 2, 3, 4, 1)).reshape(cache_nsa.shape[0], NSA_SECTIONS * KV_A, page)
    zt, zb, ks, vs = nsa_gather(cache_nsa_t, page_table, p['nsa_rows'], cmp_pe)
    merge_bg = lambda a: a.reshape((2, b * KV_GROUPS_A) + a.shape[3:])
    kvc = compress(merge_bg(zt), merge_bg(zb), cmp_pe, cmp_w1, cmp_b1, cmp_w2, n_keys)
    kvc = kvc.reshape(2, b, KV_GROUPS_A, kvc.shape[2], HEAD_DIM_A)
    w_len = win_buf.shape[1]
    win_new = p['win_rows'].reshape(b, t, 2, KV_GROUPS_A, HEAD_DIM_A)
    win_all = jnp.concatenate([win_buf, win_new], axis=1)
    win_pad = pad_rows(win_all, w_len + WIN_CHUNK).astype(MXU_DTYPE)
    kw = win_pad[:, :, 0].reshape(b, w_len + WIN_CHUNK, KV_A)
    vw = win_pad[:, :, 1].reshape(b, w_len + WIN_CHUNK, KV_A)
    o_nsa = nsa_attention(pad(p['qc']), pad(p['qr']), pad(p['gates']), kvc, ks, vs, kw, vw,
                          tq=TQ_STEP, qpos0=past_len, wbase=past_len - w_len, n_keys=n_keys, kv_t=True)

    kb_, vb, ki = dsa_gather(jnp.swapaxes(cache_dsa, 1, 2), page_table, p['dsa_rows'])
    o_dsa = dsa_attention(pad(p['qb']), pad(p['qi']), pad(p['wi']), kb_, vb, ki, tq=TQ_STEP, qpos0=past_len,
                          n_keep=min(DSA_TOPK, n_keys // 4), kv_t=True)
    nsa_rows = p['nsa_rows'].reshape(b, t, NSA_SECTIONS, KV_GROUPS_A, HEAD_DIM_A)
    return (o_nsa, o_dsa, pad(gm)), nsa_rows, win_all[:, -w_len:], p['dsa_rows']


def kernel(x_prompt, x_sample, mem_prompt, cache_nsa_kv, state_nsa_win, cache_dsa_kv, cache_mem_kv, state_conv,
           page_table, norm_g, w_in, cmp_pe, cmp_w1, cmp_b1, cmp_w2, w_out_a, w_out_b, w_out, w_mem_q, w_mem_kv,
           w_mem_out, w_up, conv_w, conv_b, w_down, final_g):
    depth = w_in.shape[0]
    d_model = x_prompt.shape[-1]
    d_ff = w_down.shape[1]
    assert CONV_WIDTH == 3
    sizes = (N_HEADS_A * HEAD_DIM_A, 6 * KV_A, 3 * N_HEADS_A, N_HEADS_B * HEAD_DIM_B, 2 * HEAD_DIM_B,
             IDX_HEADS * IDX_DIM, IDX_DIM, IDX_HEADS, 2 * d_model)
    xp, xs = x_prompt, pad_rows(x_sample, TQ_STEP)
    t_step = x_sample.shape[1]
    nsa_p, nsa_s, win_p, win_s, dsa_p, dsa_s, mem_p, conv_p, conv_s = [], [], [], [], [], [], [], [], []
    for l in range(depth):
        assert l == depth - 1, "the fused FFN epilogue applies the final norm"
        branches_p, a, bwin, c = mixer_prompt(xp, norm_g[l, 0], w_in[l], cmp_pe[l], cmp_w1[l], cmp_b1[l], cmp_w2[l],
                                              sizes)
        nsa_p.append(a); win_p.append(bwin); dsa_p.append(c)
        branches_s, a, bwin, c = mixer_sample(xs[:, :t_step], norm_g[l, 0], cache_nsa_kv[l], state_nsa_win[l],
                                              cache_dsa_kv[l], page_table, w_in[l], cmp_pe[l], cmp_w1[l], cmp_b1[l],
                                              cmp_w2[l], sizes)
        nsa_s.append(a); win_s.append(bwin); dsa_s.append(c)
        kv_p = norm_matmul(mem_prompt, norm_g[l, 2], w_mem_kv[l])
        mem_p.append(kv_p.reshape(kv_p.shape[:2] + (2, MEM_HEADS, MEM_HEAD_DIM)))
        kv_s = cache_mem_kv[l].reshape(cache_mem_kv.shape[1:3] + (-1,))
        weights = (norm_g[l, 1], norm_g[l, 3], final_g, w_out_a[l], w_out_b[l], w_out[l], w_mem_q[l], w_mem_out[l],
                   w_up[l], conv_w[l], conv_b[l], w_down[l])
        xp, cp = dense_tail(xp, *branches_p, kv_p, jnp.zeros((xp.shape[0], CONV_WIDTH - 1, 2 * d_ff), xp.dtype),
                            xp.shape[1], *weights)
        xs, cs = dense_tail(xs, *branches_s, kv_s, state_conv[l], t_step, *weights)
        conv_p.append(cp); conv_s.append(cs)
    y_prompt, y_sample = xp, xs[:, :t_step]
    return (y_prompt, y_sample, jnp.stack(nsa_p), jnp.stack(nsa_s), jnp.stack(win_p), jnp.stack(win_s),
            jnp.stack(dsa_p), jnp.stack(dsa_s), jnp.stack(mem_p), jnp.stack(conv_p), jnp.stack(conv_s))
```

```python
import functools

import jax
import jax.numpy as jnp
from jax import lax
from jax.experimental import pallas as pl
from jax.experimental.pallas import tpu as pltpu

N_HEADS_A = 16
HEAD_DIM_A = 64
KV_GROUPS_A = 2
CMP_BLOCK = 32
CMP_STRIDE = 16
SLC_BLOCK = 64
N_SELECT = 16
WINDOW = 512
N_HEADS_B = 8
HEAD_DIM_B = 128
IDX_HEADS = 16
IDX_DIM = 64
DSA_TOPK = 256
MEM_HEADS = 4
MEM_HEAD_DIM = 128
CONV_WIDTH = 3
ROPE_THETA = 500000.0
ROT_FRACTION = 4
EPS = 1e-6
KV_A = KV_GROUPS_A * HEAD_DIM_A
CMP_PER_SLC = SLC_BLOCK // CMP_STRIDE
HPG_A = N_HEADS_A // KV_GROUPS_A
DSA_ROW = 2 * HEAD_DIM_B + IDX_DIM
NSA_SECTIONS = 4

LANE = 128
SUBLANE = 8
BF16_ROWS = 16
VMEM_LIMIT = 48 * 1024 * 1024

NEG_INF = float('-inf')
POS_INF = float('inf')
MXU_DTYPE = jnp.bfloat16
TQ = 128
TQ_STEP = BF16_ROWS
KEY_CHUNK = 512
WIN_CHUNK = 128
PAGES_PER_STEP = 8
BISECT_ITERS = 40
TIE_STRIP_ITERS = 64
SLC_SHIFT = SLC_BLOCK.bit_length() - 1
CMP_PER_SLC_SHIFT = CMP_PER_SLC.bit_length() - 1


def _round_up(n, m):
    return -(-n // m) * m


def _tile(n, cap):
    if n <= cap:
        return n
    best = None
    for t in range(LANE, cap + 1, LANE):
        if n % t == 0:
            best = t
    assert best is not None, (n, cap)
    return best


def _mm_kernel(x_ref, w_ref, o_ref, acc_ref):
    @pl.when(pl.program_id(2) == 0)
    def _():
        acc_ref[...] = jnp.zeros_like(acc_ref)

    acc_ref[...] += jnp.dot(x_ref[...], w_ref[...], preferred_element_type=jnp.float32)

    @pl.when(pl.program_id(2) == pl.num_programs(2) - 1)
    def _():
        o_ref[...] = acc_ref[...].astype(o_ref.dtype)


def matmul(x, w, out_dtype=jnp.float32):
    lead = x.shape[:-1]
    k = x.shape[-1]
    n = w.shape[-1]
    x2 = x.reshape(-1, k).astype(MXU_DTYPE)
    w2 = w.astype(MXU_DTYPE)
    m0 = x2.shape[0]
    tm = 512 if m0 >= 512 else _round_up(m0, BF16_ROWS)
    m = _round_up(m0, tm)
    if m != m0:
        x2 = jnp.pad(x2, ((0, m - m0), (0, 0)))
    tn = _tile(n, 1024)
    tk = _tile(k, 2048)
    out = pl.pallas_call(
        _mm_kernel,
        grid=(m // tm, n // tn, k // tk),
        in_specs=[pl.BlockSpec((tm, tk), lambda i, j, l: (i, l)),
                  pl.BlockSpec((tk, tn), lambda i, j, l: (l, j))],
        out_specs=pl.BlockSpec((tm, tn), lambda i, j, l: (i, j)),
        out_shape=jax.ShapeDtypeStruct((m, n), out_dtype),
        scratch_shapes=[pltpu.VMEM((tm, tn), jnp.float32)],
        compiler_params=pltpu.CompilerParams(
            dimension_semantics=("parallel", "parallel", "arbitrary"),
            vmem_limit_bytes=VMEM_LIMIT),
        name="matmul",
    )(x2, w2)
    return out[:m0].reshape(lead + (n,))


def _rms(x, g):
    return x * lax.rsqrt(jnp.mean(x * x, axis=-1, keepdims=True) + EPS) * g


def _norm_mm_kernel(x_ref, g_ref, w_ref, o_ref, xn_ref):
    @pl.when(pl.program_id(1) == 0)
    def _():
        xn_ref[...] = _rms(x_ref[...], g_ref[...]).astype(xn_ref.dtype)

    o_ref[...] = jnp.dot(xn_ref[...], w_ref[...], preferred_element_type=jnp.float32).astype(o_ref.dtype)


def norm_matmul(x, g, w):
    lead = x.shape[:-1]
    k = x.shape[-1]
    n = w.shape[-1]
    x2 = x.reshape(-1, k)
    m = x2.shape[0]
    tm = 512 if m % 512 == 0 else m
    tn = _tile(n, 1536)
    out = pl.pallas_call(
        _norm_mm_kernel,
        grid=(m // tm, n // tn),
        in_specs=[pl.BlockSpec((tm, k), lambda i, j: (i, 0)),
                  pl.BlockSpec((1, k), lambda i, j: (0, 0)),
                  pl.BlockSpec((k, tn), lambda i, j: (0, j))],
        out_specs=pl.BlockSpec((tm, tn), lambda i, j: (i, j)),
        out_shape=jax.ShapeDtypeStruct((m, n), jnp.float32),
        scratch_shapes=[pltpu.VMEM((tm, k), MXU_DTYPE)],
        compiler_params=pltpu.CompilerParams(
            dimension_semantics=("parallel", "arbitrary"), vmem_limit_bytes=VMEM_LIMIT),
        name="norm_matmul",
    )(x2, g.reshape(1, k).astype(jnp.float32), w.astype(MXU_DTYPE))
    return out.reshape(lead + (n,))


def _gated_merge_kernel(oa_ref, ob_ref, wa_ref, wb_ref, ga_ref, gb_ref, o_ref):
    dot = functools.partial(jnp.dot, preferred_element_type=jnp.float32)
    ya = dot(oa_ref[...], wa_ref[...])
    yb = dot(ob_ref[...], wb_ref[...])
    o_ref[...] = (jax.nn.sigmoid(ga_ref[...]) * ya + jax.nn.sigmoid(gb_ref[...]) * yb).astype(o_ref.dtype)


def gated_merge(o_nsa, o_dsa, gm, w_oa, w_ob):
    m, ka = o_nsa.shape
    kb = o_dsa.shape[1]
    n = w_oa.shape[1]
    tm = 512 if m % 512 == 0 else m
    tn = _tile(n, 1024)
    nb = n // tn
    return pl.pallas_call(
        _gated_merge_kernel,
        grid=(m // tm, nb),
        in_specs=[pl.BlockSpec((tm, ka), lambda i, j: (i, 0)),
                  pl.BlockSpec((tm, kb), lambda i, j: (i, 0)),
                  pl.BlockSpec((ka, tn), lambda i, j: (0, j)),
                  pl.BlockSpec((kb, tn), lambda i, j: (0, j)),
                  pl.BlockSpec((tm, tn), lambda i, j: (i, j)),
                  pl.BlockSpec((tm, tn), lambda i, j: (i, j + nb))],
        out_specs=pl.BlockSpec((tm, tn), lambda i, j: (i, j)),
        out_shape=jax.ShapeDtypeStruct((m, n), MXU_DTYPE),
        compiler_params=pltpu.CompilerParams(
            dimension_semantics=("parallel", "parallel"), vmem_limit_bytes=VMEM_LIMIT),
        name="gated_merge",
    )(o_nsa, o_dsa, w_oa.astype(MXU_DTYPE), w_ob.astype(MXU_DTYPE), gm, gm)


def _out_proj_kernel(z_ref, w_ref, x_ref, o_ref):
    o_ref[...] = x_ref[...] + jnp.dot(z_ref[...], w_ref[...], preferred_element_type=jnp.float32)


def out_proj_residual(z, w, x):
    m, k = z.shape
    n = w.shape[1]
    tm = 256 if m % 256 == 0 else m
    return pl.pallas_call(
        _out_proj_kernel,
        grid=(m // tm,),
        in_specs=[pl.BlockSpec((tm, k), lambda i: (i, 0)),
                  pl.BlockSpec((k, n), lambda i: (0, 0)),
                  pl.BlockSpec((tm, n), lambda i: (i, 0))],
        out_specs=pl.BlockSpec((tm, n), lambda i: (i, 0)),
        out_shape=jax.ShapeDtypeStruct((m, n), jnp.float32),
        compiler_params=pltpu.CompilerParams(dimension_semantics=("parallel",), vmem_limit_bytes=VMEM_LIMIT),
        name="out_proj_residual",
    )(z, w.astype(MXU_DTYPE), x)


def _mem_block_kernel(x_ref, g1_ref, g2_ref, wq_ref, kv_ref, wo_ref, x_out_ref, xn_out_ref):
    d = MEM_HEAD_DIM
    hd = MEM_HEADS * d
    x = x_ref[0]
    xn = _rms(x, g1_ref[...]).astype(wq_ref.dtype)
    q = jnp.dot(xn, wq_ref[...], preferred_element_type=jnp.float32).astype(wq_ref.dtype)
    outs = []
    for h in range(MEM_HEADS):
        k = kv_ref[0, :, h * d:(h + 1) * d].astype(wq_ref.dtype)
        v = kv_ref[0, :, hd + h * d:hd + (h + 1) * d].astype(wq_ref.dtype)
        s = _dot_nt(q[:, h * d:(h + 1) * d], k) * d ** -0.5
        e = jnp.exp(s - jnp.max(s, axis=1, keepdims=True))
        p = e / jnp.sum(e, axis=1, keepdims=True)
        outs.append(jnp.dot(p.astype(v.dtype), v, preferred_element_type=jnp.float32))
    o = jnp.concatenate(outs, axis=1).astype(wo_ref.dtype)
    x2 = x + jnp.dot(o, wo_ref[...], preferred_element_type=jnp.float32)
    x_out_ref[0] = x2
    xn_out_ref[0] = _rms(x2, g2_ref[...]).astype(xn_out_ref.dtype)


def mem_block(x, g1, g2, w_q, kv, w_o):
    b, t, dm = x.shape
    mt, kvw = kv.shape[1:]
    hd = w_q.shape[1]
    tm = 256 if t % 256 == 0 else t
    row_spec = pl.BlockSpec((1, tm, dm), lambda bi, i: (bi, i, 0))
    g_spec = pl.BlockSpec((1, dm), lambda bi, i: (0, 0))
    return pl.pallas_call(
        _mem_block_kernel,
        grid=(b, t // tm),
        in_specs=[row_spec, g_spec, g_spec,
                  pl.BlockSpec((dm, hd), lambda bi, i: (0, 0)),
                  pl.BlockSpec((1, mt, kvw), lambda bi, i: (bi, 0, 0)),
                  pl.BlockSpec((hd, dm), lambda bi, i: (0, 0))],
        out_specs=[row_spec, row_spec],
        out_shape=[jax.ShapeDtypeStruct((b, t, dm), jnp.float32), jax.ShapeDtypeStruct((b, t, dm), MXU_DTYPE)],
        compiler_params=pltpu.CompilerParams(
            dimension_semantics=("parallel", "parallel"), vmem_limit_bytes=VMEM_LIMIT),
        name="mem_block",
    )(x, g1.reshape(1, dm).astype(jnp.float32), g2.reshape(1, dm).astype(jnp.float32), w_q.astype(MXU_DTYPE),
      kv, w_o.astype(MXU_DTYPE))


def _ffn_up_kernel(x_ref, halo_ref, wg_ref, wu_ref, cwg_ref, cwu_ref, cbg_ref, cbu_ref, pg_ref, pu_ref,
                   h_ref, sg_ref, su_ref, *, nb, tm, t_real):
    i = pl.program_id(2)
    n_halo = halo_ref.shape[1]
    tn = wg_ref.shape[1]
    rows = nb * tm
    x = x_ref[...].reshape(rows, x_ref.shape[2])
    if nb == 1:
        x = jnp.concatenate([halo_ref[0], x], axis=0)
    row = _iota((nb, tm, 1), 1).reshape(rows, 1)
    last = (t_real - 1) // tm
    r_last = (t_real - 1) % tm
    spread = lambda p: jnp.broadcast_to(p, (nb, tm, tn)).reshape(rows, tn)

    def branch(w_ref, cw_ref, cb_ref, p_ref, s_ref):
        u = jnp.dot(x, w_ref[...], preferred_element_type=jnp.float32)
        p0, p1 = p_ref[:, 0:1, :], p_ref[:, 1:2, :]
        if nb == 1:
            uh, u = u[:n_halo], u[n_halo:]
            p0 = jnp.where(i == 0, p0, uh[n_halo - 2:n_halo - 1][None])
            p1 = jnp.where(i == 0, p1, uh[n_halo - 1:n_halo][None])
        p0, p1 = spread(p0), spread(p1)
        u1 = jnp.where(row == 0, p1, pltpu.roll(u, 1, 0))
        u2 = jnp.where(row == 0, p0, jnp.where(row == 1, p1, pltpu.roll(u, 2, 0)))

        @pl.when(i == last)
        def _():
            s_ref[...] = u.reshape(nb, tm, tn)[:, r_last - 1:r_last + 1, :]

        return cb_ref[...] + u2 * cw_ref[0:1, :] + u1 * cw_ref[1:2, :] + u * cw_ref[2:3, :]

    gate = branch(wg_ref, cwg_ref, cbg_ref, pg_ref, sg_ref)
    up = branch(wu_ref, cwu_ref, cbu_ref, pu_ref, su_ref)
    h_ref[...] = (jax.nn.silu(gate) * up).reshape(nb, tm, tn).astype(h_ref.dtype)


def ffn_up(xn, prev, w_up, conv_w, conv_b, t_real):
    b, t, dm = xn.shape
    f2 = w_up.shape[1]
    f = f2 // 2
    tm = 512 if t % 512 == 0 else t
    nb = 1 if t > tm else max(1, min(b, 512 // tm))
    while b % nb:
        nb -= 1
    tn = _tile(f, 512)
    nf = f // tn
    halo = min(BF16_ROWS, tm)
    hpt = tm // halo
    assert t_real >= 2 and (t_real - 1) % tm >= 1 and tm % SUBLANE == 0
    w_up = w_up.astype(MXU_DTYPE)
    conv_b = conv_b.reshape(1, f2)
    col = lambda off: (lambda j, bi, i: (0, j + off))
    st = lambda off: (lambda j, bi, i: (bi, 0, j + off))
    specs = [pl.BlockSpec((nb, tm, dm), lambda j, bi, i: (bi, i, 0)),
             pl.BlockSpec((1, halo, dm), lambda j, bi, i: (bi * nb, jnp.maximum(i * hpt - 1, 0), 0)),
             pl.BlockSpec((dm, tn), col(0)), pl.BlockSpec((dm, tn), col(nf)),
             pl.BlockSpec((CONV_WIDTH, tn), col(0)), pl.BlockSpec((CONV_WIDTH, tn), col(nf)),
             pl.BlockSpec((1, tn), col(0)), pl.BlockSpec((1, tn), col(nf)),
             pl.BlockSpec((nb, 2, tn), st(0)), pl.BlockSpec((nb, 2, tn), st(nf))]
    h, sg, su = pl.pallas_call(
        functools.partial(_ffn_up_kernel, nb=nb, tm=tm, t_real=t_real),
        grid=(nf, b // nb, t // tm),
        in_specs=specs,
        out_specs=[pl.BlockSpec((nb, tm, tn), lambda j, bi, i: (bi, i, j)),
                   pl.BlockSpec((nb, 2, tn), st(0)), pl.BlockSpec((nb, 2, tn), st(0))],
        out_shape=[jax.ShapeDtypeStruct((b, t, f), MXU_DTYPE), jax.ShapeDtypeStruct((b, 2, f), jnp.float32),
                   jax.ShapeDtypeStruct((b, 2, f), jnp.float32)],
        compiler_params=pltpu.CompilerParams(
            dimension_semantics=("parallel", "parallel", "arbitrary"), vmem_limit_bytes=VMEM_LIMIT),
        name="ffn_up",
    )(xn, xn, w_up, w_up, conv_w, conv_w, conv_b, conv_b, prev, prev)
    return h, jnp.concatenate([sg, su], axis=-1)


def _ffn_down_kernel(h_ref, w_ref, x_ref, g_ref, o_ref, acc_ref):
    @pl.when(pl.program_id(1) == 0)
    def _():
        acc_ref[...] = x_ref[...]

    acc_ref[...] += jnp.dot(h_ref[...], w_ref[...], preferred_element_type=jnp.float32)

    @pl.when(pl.program_id(1) == pl.num_programs(1) - 1)
    def _():
        o_ref[...] = _rms(acc_ref[...], g_ref[...])


def ffn_down_norm(h, w_down, x, g):
    m, f = h.shape
    dm = w_down.shape[1]
    tm = 512 if m % 512 == 0 else m
    tk = _tile(f, 2048)
    return pl.pallas_call(
        _ffn_down_kernel,
        grid=(m // tm, f // tk),
        in_specs=[pl.BlockSpec((tm, tk), lambda i, l: (i, l)),
                  pl.BlockSpec((tk, dm), lambda i, l: (l, 0)),
                  pl.BlockSpec((tm, dm), lambda i, l: (i, 0)),
                  pl.BlockSpec((1, dm), lambda i, l: (0, 0))],
        out_specs=pl.BlockSpec((tm, dm), lambda i, l: (i, 0)),
        out_shape=jax.ShapeDtypeStruct((m, dm), jnp.float32),
        scratch_shapes=[pltpu.VMEM((tm, dm), jnp.float32)],
        compiler_params=pltpu.CompilerParams(
            dimension_semantics=("parallel", "arbitrary"), vmem_limit_bytes=VMEM_LIMIT),
        name="ffn_down_norm",
    )(h, w_down.astype(MXU_DTYPE), x, g.reshape(1, dm).astype(jnp.float32))


def dense_tail(x, o_nsa, o_dsa, gm, kv_mem, prev_u, t_real, g_mem, g_ffn, g_final, w_oa, w_ob, w_o, w_mq, w_mo,
               w_up, conv_w, conv_b, w_down):
    b, t, dm = x.shape
    rows = lambda a: a.reshape(b * t, a.shape[-1])
    z = gated_merge(rows(o_nsa), rows(o_dsa), rows(gm), w_oa, w_ob)
    x1 = out_proj_residual(z, w_o, rows(x)).reshape(b, t, dm)
    x2, xn2 = mem_block(x1, g_mem, g_ffn, w_mq, kv_mem, w_mo)
    h, state = ffn_up(xn2, prev_u, w_up, conv_w, conv_b, t_real)
    y = ffn_down_norm(rows(h), w_down, rows(x2), g_final)
    return y.reshape(b, t, dm), state


def _dot_nt(a, b):
    return lax.dot_general(a, b, (((1,), (1,)), ((), ())), preferred_element_type=jnp.float32)


def _iota(shape, dim):
    return lax.broadcasted_iota(jnp.int32, shape, dim)


def _flash_init(rows, d):
    return (jnp.full((rows, 1), NEG_INF, jnp.float32), jnp.zeros((rows, 1), jnp.float32),
            jnp.zeros((rows, d), jnp.float32))


def _flash_step(carry, q, k, v, madd, nh, scale=None, kv_t=False):
    m, l, acc = carry
    s = jnp.dot(q, k, preferred_element_type=jnp.float32) if kv_t else _dot_nt(q, k)
    if scale is not None:
        s = s * scale
    r, kb = s.shape
    s = (s.reshape(nh, r // nh, kb) + madd[None]).reshape(r, kb)
    m_new = jnp.maximum(m, jnp.max(s, axis=1, keepdims=True))
    m_safe = jnp.where(m_new == NEG_INF, 0.0, m_new)
    p = jnp.exp(s - m_safe)
    alpha = jnp.exp(m - m_safe)
    l = alpha * l + jnp.sum(p, axis=1, keepdims=True)
    pv = _dot_nt(p.astype(v.dtype), v) if kv_t else jnp.dot(p.astype(v.dtype), v, preferred_element_type=jnp.float32)
    return m_new, l, alpha * acc + pv


def _flash_finish(carry):
    _, l, acc = carry
    return acc / jnp.maximum(l, 1e-30)


def _split_dot(x, m01):
    hi = x.astype(jnp.bfloat16)
    r1 = x - hi.astype(jnp.float32)
    mid = r1.astype(jnp.bfloat16)
    lo = (r1 - mid.astype(jnp.float32)).astype(jnp.bfloat16)
    dot = functools.partial(jnp.dot, preferred_element_type=jnp.float32)
    return dot(hi, m01) + dot(mid, m01) + dot(lo, m01)


def _nsa_group(g, i, qc_ref, qr_ref, gate_ref, kc_ref, vc_ref, ks_ref, vs_ref, kw_ref, vw_ref, mask_ref,
               *, tq, kb, qpos0, wbase, ns, n_sel, kv_t):
    nh, d = HPG_A, HEAD_DIM_A
    rows = nh * tq
    nc = kc_ref.shape[3]
    nsp = _round_up(ns, LANE)
    t0 = qpos0 + i * tq
    nj = (t0 + tq - 1) // kb + 1
    cols = slice(g * d, (g + 1) * d)

    def stack_heads(q_ref):
        return jnp.concatenate([q_ref[0, :, (g * nh + h) * d:(g * nh + h + 1) * d] for h in range(nh)], axis=0)

    qc = stack_heads(qc_ref)
    qr = stack_heads(qr_ref)
    tpos = t0 + _iota((tq, 1), 0)

    blk_last = _iota((1, nc), 1) * CMP_STRIDE + (CMP_BLOCK - 1)
    madd_c = jnp.where(blk_last <= tpos, 0.0, NEG_INF)
    s = _dot_nt(qc, kc_ref[0, 0, g]).reshape(nh, tq, nc) + madd_c[None]
    m = jnp.max(s, axis=2, keepdims=True)
    m = jnp.where(m == NEG_INF, 0.0, m)
    e = jnp.exp(s - m)
    p = e / jnp.maximum(jnp.sum(e, axis=2, keepdims=True), 1e-30)
    o_cmp = jnp.dot(p.reshape(rows, nc).astype(vc_ref.dtype), vc_ref[0, 0, g], preferred_element_type=jnp.float32)

    imp = jnp.sum(p, axis=0)
    c_id = _iota((nc, nsp), 0)
    m_id = _iota((nc, nsp), 1)
    overlap = (jnp.right_shift(c_id, CMP_PER_SLC_SHIFT) == m_id) | (c_id == m_id * CMP_PER_SLC - 1)
    score = _split_dot(imp, overlap.astype(jnp.bfloat16))
    blk = _iota((1, nsp), 1)
    cur = jnp.right_shift(tpos, SLC_SHIFT)
    forced = (blk == 0) | (blk == cur) | (blk == cur - 1)
    sc = jnp.where(forced, POS_INF, jnp.where(blk * SLC_BLOCK <= tpos, score, NEG_INF))
    if tq % LANE == 0 and nsp % LANE == 0:
        sc_t = sc.T
        blk_t = _iota((nsp, 1), 0)
        rank_t = jnp.zeros((nsp, tq), jnp.float32)
        for mp in range(ns):
            ref = sc_t[mp:mp + 1, :]
            beats = (ref > sc_t) | ((ref == sc_t) & (blk_t > mp))
            rank_t = rank_t + jnp.where(beats, 1.0, 0.0)
        sel = jnp.where(rank_t < n_sel, 1.0, 0.0).T.astype(jnp.bfloat16)
    else:
        rank = jnp.zeros((tq, nsp), jnp.float32)
        for mp in range(ns):
            col = sc[:, mp:mp + 1]
            beats = (col > sc) | ((col == sc) & (blk > mp))
            rank = rank + jnp.where(beats, 1.0, 0.0)
        sel = jnp.where(rank < n_sel, 1.0, 0.0).astype(jnp.bfloat16)

    def make_mask(j, _):
        kpos = j * kb + _iota((1, kb), 1)
        expand = (jnp.right_shift(j * kb + _iota((nsp, kb), 1), SLC_SHIFT) == _iota((nsp, kb), 0))
        hit = jnp.dot(sel, expand.astype(jnp.bfloat16), preferred_element_type=jnp.float32)
        mask_ref[j] = jnp.where((hit > 0.5) & (kpos <= tpos), 0.0, NEG_INF)
        return 0

    lax.fori_loop(0, nj, make_mask, 0)

    def slc_body(j, carry):
        off = pl.multiple_of(j * kb, kb)
        return _flash_step(carry, qr, ks_ref[0, pl.ds(off, kb), cols], vs_ref[0, pl.ds(off, kb), cols],
                           mask_ref[j], nh)

    if kv_t:
        o_slc = _flash_finish(_flash_step(_flash_init(rows, d), qr, ks_ref[0, cols, :], vs_ref[0, cols, :],
                                          mask_ref[0], nh, kv_t=True))
    else:
        o_slc = _flash_finish(lax.fori_loop(0, nj, slc_body, _flash_init(rows, d)))

    wk = min(_round_up(WINDOW + tq, WIN_CHUNK), kw_ref.shape[1])
    first = jnp.maximum(t0 - (WINDOW - 1) - wbase, 0) // WIN_CHUNK * WIN_CHUNK
    off = pl.multiple_of(jnp.minimum(first, kw_ref.shape[1] - wk), WIN_CHUNK)
    dist = tpos - (wbase + off + _iota((1, wk), 1))
    madd_w = jnp.where((dist >= 0) & (dist < WINDOW), 0.0, NEG_INF)
    o_win = _flash_finish(_flash_step(_flash_init(rows, d), qr, kw_ref[0, pl.ds(off, wk), cols],
                                      vw_ref[0, pl.ds(off, wk), cols], madd_w, nh))

    gates = gate_ref[0]
    outs = []
    for h in range(nh):
        rs = slice(h * tq, (h + 1) * tq)
        c = g * nh + h
        outs.append(gates[:, c:c + 1] * o_cmp[rs] + gates[:, N_HEADS_A + c:N_HEADS_A + c + 1] * o_slc[rs]
                    + gates[:, 2 * N_HEADS_A + c:2 * N_HEADS_A + c + 1] * o_win[rs])
    return outs


def _nsa_kernel(qc_ref, qr_ref, gate_ref, kc_ref, vc_ref, ks_ref, vs_ref, kw_ref, vw_ref, o_ref, mask_ref, **kw):
    i = pl.program_id(1)
    outs = []
    for g in range(KV_GROUPS_A):
        outs += _nsa_group(g, i, qc_ref, qr_ref, gate_ref, kc_ref, vc_ref, ks_ref, vs_ref, kw_ref, vw_ref,
                           mask_ref, **kw)
    o_ref[0] = jnp.concatenate(outs, axis=1).astype(o_ref.dtype)


def nsa_attention(qc, qr, gates, kvc, ks, vs, kw, vw, *, tq, qpos0, wbase, n_keys, kv_t=False):
    b, t, hd = qc.shape
    l = ks.shape[2] if kv_t else ks.shape[1]
    lw = kw.shape[1]
    nc, d = kvc.shape[3:]
    kb = min(KEY_CHUNK, l) if tq >= TQ else l
    assert l % kb == 0 and t % tq == 0 and lw % WIN_CHUNK == 0 and (kb == l or not kv_t)
    assert (qpos0 + t - 1) // kb + 1 <= l // kb and (qpos0 + t - 1 - wbase) // WIN_CHUNK + 1 <= lw // WIN_CHUNK
    assert (qpos0 - wbase) % WIN_CHUNK == 0 and WIN_CHUNK % tq == 0 and tq > 1
    ns = l // SLC_BLOCK
    n_sel = min(N_SELECT, -(-n_keys // SLC_BLOCK))
    q_spec = pl.BlockSpec((1, tq, hd), lambda bi, i: (bi, i, 0))
    kc_spec = pl.BlockSpec((1, 1, KV_GROUPS_A, nc, d), lambda bi, i: (0, bi, 0, 0, 0))
    vc_spec = pl.BlockSpec((1, 1, KV_GROUPS_A, nc, d), lambda bi, i: (1, bi, 0, 0, 0))
    k_spec = pl.BlockSpec((1, KV_A, l) if kv_t else (1, l, KV_A), lambda bi, i: (bi, 0, 0))
    w_spec = pl.BlockSpec((1, lw, KV_A), lambda bi, i: (bi, 0, 0))
    return pl.pallas_call(
        functools.partial(_nsa_kernel, tq=tq, kb=kb, qpos0=qpos0, wbase=wbase, ns=ns, n_sel=n_sel, kv_t=kv_t),
        grid=(b, t // tq),
        in_specs=[q_spec, q_spec, pl.BlockSpec((1, tq, 3 * N_HEADS_A), lambda bi, i: (bi, i, 0)),
                  kc_spec, vc_spec, k_spec, k_spec, w_spec, w_spec],
        out_specs=q_spec,
        out_shape=jax.ShapeDtypeStruct((b, t, hd), MXU_DTYPE),
        scratch_shapes=[pltpu.VMEM((l // kb, tq, kb), jnp.float32)],
        compiler_params=pltpu.CompilerParams(
            dimension_semantics=("parallel", "arbitrary"), vmem_limit_bytes=VMEM_LIMIT),
        name="nsa_attention",
    )(qc, qr, gates, kvc, kvc, ks, vs, kw, vw)


def _dsa_kernel(qb_ref, qi_ref, wi_ref, kb_ref, vb_ref, ki_ref, o_ref, score_ref, *, tq, kb, qpos0, n_keep, kv_t):
    i = pl.program_id(1)
    nh, d = N_HEADS_B, HEAD_DIM_B
    t0 = qpos0 + i * tq
    nj = (t0 + tq - 1) // kb + 1
    tpos = t0 + _iota((tq, 1), 0)
    w = wi_ref[0]

    def idx_body(j, carry):
        lo, hi = carry
        off = pl.multiple_of(j * kb, kb)
        kidx = ki_ref[0] if kv_t else ki_ref[0, pl.ds(off, kb), :]
        acc = jnp.zeros((tq, kb), jnp.float32)
        heads = [qi_ref[0, :, h * IDX_DIM:(h + 1) * IDX_DIM] for h in range(IDX_HEADS)]
        if kv_t:
            dots_all = jnp.dot(jnp.concatenate(heads, axis=0), kidx, preferred_element_type=jnp.float32)
        for h in range(IDX_HEADS):
            dots = dots_all[h * tq:(h + 1) * tq] if kv_t else _dot_nt(heads[h], kidx)
            acc = acc + w[:, h:h + 1] * jnp.maximum(dots, 0.0)
        vis = (off + _iota((1, kb), 1)) <= tpos
        score_ref[j] = jnp.where(vis, acc, NEG_INF)
        lo = jnp.minimum(lo, jnp.min(jnp.where(vis, acc, POS_INF), axis=1, keepdims=True))
        hi = jnp.maximum(hi, jnp.max(jnp.where(vis, acc, NEG_INF), axis=1, keepdims=True))
        return lo, hi

    lo, hi = lax.fori_loop(0, nj, idx_body, (jnp.full((tq, 1), POS_INF, jnp.float32),
                                             jnp.full((tq, 1), NEG_INF, jnp.float32)))

    k = float(n_keep)
    n_vis = (tpos + 1).astype(jnp.float32)
    n_keys = score_ref.shape[0] * kb

    def reduce_scores(pred, pick, init, combine, lane_reduce):
        def body(j, acc):
            sc = score_ref[j]
            for c in range(kb // LANE):
                kpos = (j * kb + c * LANE + _iota((1, LANE), 1)).astype(jnp.float32)
                piece = sc[:, c * LANE:(c + 1) * LANE]
                acc = combine(acc, pick(pred(piece, kpos), piece))
            return acc

        return lane_reduce(lax.fori_loop(0, nj, body, jnp.full((tq, LANE), init, jnp.float32)),
                           axis=1, keepdims=True)

    def count_where(pred):
        return reduce_scores(pred, lambda m, _: jnp.where(m, 1.0, 0.0), 0.0, jnp.add, jnp.sum)

    def min_where(pred):
        return reduce_scores(pred, lambda m, x: jnp.where(m, x, POS_INF), POS_INF, jnp.minimum, jnp.min)

    def any_row(flag):
        return jnp.max(jnp.where(flag, 1.0, 0.0)) > 0.0

    def bisect(state):
        it, lo, hi, cnt_lo = state
        mid = 0.5 * (lo + hi)
        mid_b = jnp.broadcast_to(mid, (tq, LANE))
        cnt = count_where(lambda x, _: x >= mid_b)
        ge = cnt >= k
        return it + 1, jnp.where(ge, mid, lo), jnp.where(ge, hi, mid), jnp.where(ge, cnt, cnt_lo)

    _, thr, _, cnt_lo = lax.while_loop(
        lambda st: (st[0] < BISECT_ITERS) & any_row((st[3] != k) & (n_vis > k)), bisect,
        (jnp.int32(0), lo, hi, n_vis))

    def break_ties():
        def above(v):
            v_b = jnp.broadcast_to(v, (tq, LANE))
            return count_where(lambda x, _: x > v_b)

        def strip(state):
            it, v, c_gt = state
            v_b = jnp.broadcast_to(v, (tq, LANE))
            v_next = jnp.where(c_gt >= k, min_where(lambda x, _: x > v_b), v)
            return it + 1, v_next, above(v_next)

        thr_b = jnp.broadcast_to(thr, (tq, LANE))
        v0 = min_where(lambda x, _: x >= thr_b)
        _, v, c_gt = lax.while_loop(lambda st: (st[0] < TIE_STRIP_ITERS) & any_row(st[2] >= k), strip,
                                    (jnp.int32(0), v0, above(v0)))
        need = k - c_gt
        v_b = jnp.broadcast_to(v, (tq, LANE))

        def narrow(_, bounds):
            j_lo, j_hi = bounds
            mid = jnp.floor(0.5 * (j_lo + j_hi))
            ge = count_where(lambda x, kpos: (x == v_b) & (kpos <= mid)) >= need
            return jnp.where(ge, j_lo, mid), jnp.where(ge, mid, j_hi)

        _, j_max = lax.fori_loop(0, n_keys.bit_length(), narrow,
                                 (jnp.full((tq, 1), -1.0, jnp.float32), jnp.full((tq, 1), n_keys - 1.0, jnp.float32)))
        return v, j_max

    thr, j_max = lax.cond(any_row((cnt_lo > k) & (n_vis > k)), break_ties,
                          lambda: (thr, jnp.full((tq, 1), float(n_keys), jnp.float32)))

    q = jnp.concatenate([qb_ref[0, :, h * d:(h + 1) * d] for h in range(nh)], axis=0)

    def att_body(j, carry):
        off = pl.multiple_of(j * kb, kb)
        sc = score_ref[j]
        kpos = (off + _iota((1, kb), 1)).astype(jnp.float32)
        madd = jnp.where((sc > thr) | ((sc == thr) & (kpos <= j_max)), 0.0, NEG_INF)
        if kv_t:
            return _flash_step(carry, q, kb_ref[0], vb_ref[0], madd, nh, scale=d ** -0.5, kv_t=True)
        return _flash_step(carry, q, kb_ref[0, pl.ds(off, kb), :], vb_ref[0, pl.ds(off, kb), :], madd, nh,
                           scale=d ** -0.5)

    o = _flash_finish(lax.fori_loop(0, nj, att_body, _flash_init(nh * tq, d)))
    o_ref[0] = jnp.concatenate([o[h * tq:(h + 1) * tq] for h in range(nh)], axis=1).astype(o_ref.dtype)


def dsa_attention(qb, qi, wi, kb_, vb, ki, *, tq, qpos0, n_keep, kv_t=False):
    b, t, hd = qb.shape
    l = kb_.shape[2] if kv_t else kb_.shape[1]
    kb = min(KEY_CHUNK, l) if tq >= TQ else l
    assert l % kb == 0 and t % tq == 0 and (qpos0 + t - 1) // kb + 1 <= l // kb and (kb == l or not kv_t)
    kv_spec = lambda width: pl.BlockSpec((1, width, l) if kv_t else (1, l, width), lambda bi, i: (bi, 0, 0))
    return pl.pallas_call(
        functools.partial(_dsa_kernel, tq=tq, kb=kb, qpos0=qpos0, n_keep=n_keep, kv_t=kv_t),
        grid=(b, t // tq),
        in_specs=[pl.BlockSpec((1, tq, hd), lambda bi, i: (bi, i, 0)),
                  pl.BlockSpec((1, tq, IDX_HEADS * IDX_DIM), lambda bi, i: (bi, i, 0)),
                  pl.BlockSpec((1, tq, IDX_HEADS), lambda bi, i: (bi, i, 0)),
                  kv_spec(HEAD_DIM_B), kv_spec(HEAD_DIM_B), kv_spec(IDX_DIM)],
        out_specs=pl.BlockSpec((1, tq, hd), lambda bi, i: (bi, i, 0)),
        out_shape=jax.ShapeDtypeStruct((b, t, hd), MXU_DTYPE),
        scratch_shapes=[pltpu.VMEM((l // kb, tq, kb), jnp.float32)],
        compiler_params=pltpu.CompilerParams(
            dimension_semantics=("parallel", "arbitrary"), vmem_limit_bytes=VMEM_LIMIT),
        name="dsa_attention",
    )(qb, qi, wi, kb_, vb, ki)


def _page_maps(n_pages, pp):
    n_steps = n_pages // pp

    def page_map(r):
        return lambda b, s, pt: (pt[b * n_pages + jnp.minimum(s, n_steps - 1) * pp + r], 0, 0)

    return n_steps, page_map


def _chunk_rows(tok_ref, sec, row0, n):
    d = HEAD_DIM_A
    first_half = _iota((n, KV_A), 1) < d
    pieces = [[] for _ in range(KV_GROUPS_A)]
    for j in range(0, CMP_STRIDE, 2):
        a, b = [tok_ref[sec, pl.ds(row0 + jj, n, stride=CMP_STRIDE), :] for jj in (j, j + 1)]
        pieces[0].append(jnp.where(first_half, a, pltpu.roll(b, d, 1)))
        pieces[1].append(jnp.where(first_half, pltpu.roll(a, d, 1), b))
    return [jnp.concatenate(p, axis=1) for p in pieces]


def _nsa_gather_kernel(pt_ref, *refs, pp, n_steps, rows):
    del pt_ref
    pages, (tail_tok_ref, tail_t_ref, pe_ref) = refs[:pp], refs[pp:pp + 3]
    zt_ref, zb_ref, ks_ref, vs_ref, tok_ref = refs[pp + 3:]
    is_tail = pl.program_id(1) == n_steps
    z = [[[] for _ in range(KV_GROUPS_A)] for _ in range(2)]
    for r in range(pp):
        cs = slice(r * rows, (r + 1) * rows)
        ks_ref[0, :, cs] = jnp.where(is_tail, tail_t_ref[0, :KV_A, cs],
                                     pages[r][0, 2 * KV_A:3 * KV_A, :]).astype(ks_ref.dtype)
        vs_ref[0, :, cs] = jnp.where(is_tail, tail_t_ref[0, KV_A:, cs],
                                     pages[r][0, 3 * KV_A:, :]).astype(vs_ref.dtype)
        for sec in range(2):
            tok_ref[sec, cs, :] = jnp.where(is_tail, tail_tok_ref[0, sec, cs, :],
                                            pages[r][0, sec * KV_A:(sec + 1) * KV_A, :].T)
            for g, zg in enumerate(_chunk_rows(tok_ref, sec, r * rows, rows // CMP_STRIDE)):
                z[sec][g].append(zg)
    for sec in range(2):
        for g in range(KV_GROUPS_A):
            zf = jnp.concatenate(z[sec][g], axis=0)
            zt_ref[sec, 0, g] = (zf + pe_ref[sec, 0]).astype(zt_ref.dtype)
            zb_ref[sec, 0, g] = (zf + pe_ref[sec, 1]).astype(zb_ref.dtype)


def nsa_gather(cache_t, page_table, new_rows, cmp_pe):
    db, n_pages = page_table.shape
    width, rows = cache_t.shape[1:]
    pp = PAGES_PER_STEP
    n_steps, page_map = _page_maps(n_pages, pp)
    l = (n_steps + 1) * pp * rows
    cps = pp * rows // CMP_STRIDE
    flat = CMP_STRIDE * HEAD_DIM_A
    pe = cmp_pe.reshape(2, 2, 1, flat).astype(jnp.float32)
    tail = pad_rows(new_rows, pp * rows)
    tail_tok = tail[:, :, :2 * KV_A].reshape(db, pp * rows, 2, KV_A).swapaxes(1, 2)
    tail_t = tail[:, :, 2 * KV_A:].swapaxes(1, 2)
    z_spec = pl.BlockSpec((2, 1, KV_GROUPS_A, cps, flat), lambda b, s, pt: (0, b, 0, s, 0))
    r_spec = pl.BlockSpec((1, KV_A, pp * rows), lambda b, s, pt: (b, 0, s))
    z_shape = jax.ShapeDtypeStruct((2, db, KV_GROUPS_A, l // CMP_STRIDE, flat), MXU_DTYPE)
    r_shape = jax.ShapeDtypeStruct((db, KV_A, l), MXU_DTYPE)
    return pl.pallas_call(
        functools.partial(_nsa_gather_kernel, pp=pp, n_steps=n_steps, rows=rows),
        grid_spec=pltpu.PrefetchScalarGridSpec(
            num_scalar_prefetch=1,
            grid=(db, n_steps + 1),
            in_specs=[pl.BlockSpec((1, width, rows), page_map(r)) for r in range(pp)]
            + [pl.BlockSpec((1, 2, pp * rows, KV_A), lambda b, s, pt: (b, 0, 0, 0)),
               pl.BlockSpec((1, 2 * KV_A, pp * rows), lambda b, s, pt: (b, 0, 0)),
               pl.BlockSpec((2, 2, 1, flat), lambda b, s, pt: (0, 0, 0, 0))],
            out_specs=[z_spec, z_spec, r_spec, r_spec],
            scratch_shapes=[pltpu.VMEM((2, pp * rows, KV_A), jnp.float32)]),
        out_shape=[z_shape, z_shape, r_shape, r_shape],
        compiler_params=pltpu.CompilerParams(
            dimension_semantics=("parallel", "arbitrary"), vmem_limit_bytes=VMEM_LIMIT),
        name="nsa_gather",
    )(page_table.reshape(-1), *([cache_t] * pp), tail_tok, tail_t, pe)


def _dsa_gather_kernel(pt_ref, *refs, pp, n_steps, rows):
    del pt_ref
    pages, tail_ref, (k_ref, v_ref, i_ref) = refs[:pp], refs[pp], refs[pp + 1:]
    is_tail = pl.program_id(1) == n_steps
    d = HEAD_DIM_B
    for r in range(pp):
        cs = slice(r * rows, (r + 1) * rows)
        x = jnp.where(is_tail, tail_ref[0, :, cs], pages[r][0])
        k_ref[0, :, cs] = x[:d].astype(k_ref.dtype)
        v_ref[0, :, cs] = x[d:2 * d].astype(v_ref.dtype)
        i_ref[0, :, cs] = x[2 * d:].astype(i_ref.dtype)


def dsa_gather(cache_t, page_table, new_rows):
    db, n_pages = page_table.shape
    width, rows = cache_t.shape[1:]
    pp = PAGES_PER_STEP
    n_steps, page_map = _page_maps(n_pages, pp)
    l = (n_steps + 1) * pp * rows
    tail_t = pad_rows(new_rows, pp * rows).swapaxes(1, 2)
    out_spec = lambda w: pl.BlockSpec((1, w, pp * rows), lambda b, s, pt: (b, 0, s))
    widths = (HEAD_DIM_B, HEAD_DIM_B, IDX_DIM)
    return pl.pallas_call(
        functools.partial(_dsa_gather_kernel, pp=pp, n_steps=n_steps, rows=rows),
        grid_spec=pltpu.PrefetchScalarGridSpec(
            num_scalar_prefetch=1,
            grid=(db, n_steps + 1),
            in_specs=[pl.BlockSpec((1, width, rows), page_map(r)) for r in range(pp)]
            + [pl.BlockSpec((1, width, pp * rows), lambda b, s, pt: (b, 0, 0))],
            out_specs=[out_spec(w) for w in widths]),
        out_shape=[jax.ShapeDtypeStruct((db, w, l), MXU_DTYPE) for w in widths],
        compiler_params=pltpu.CompilerParams(
            dimension_semantics=("parallel", "arbitrary"), vmem_limit_bytes=VMEM_LIMIT),
        name="dsa_gather",
    )(page_table.reshape(-1), *([cache_t] * pp), tail_t)


def _compress_kernel(zt_ref, zb_ref, pe_ref, w1t_ref, w1b_ref, b1_ref, w2_ref, o_ref, ab_ref, *, ncp):
    dot = functools.partial(jnp.dot, preferred_element_type=jnp.float32)
    ch = zb_ref.shape[2]
    at = dot(zt_ref[0, 0], w1t_ref[0])
    ab_ref[:ch] = dot(zb_ref[0, 0], w1b_ref[0])
    pe_rows = jnp.broadcast_to(pe_ref[0, 1], (SUBLANE, pe_ref.shape[3])).astype(zb_ref.dtype)
    ab_ref[ch:] = dot(pe_rows, w1b_ref[0])
    h = jax.nn.gelu(at[:ncp] + ab_ref[pl.ds(1, ncp), :] + b1_ref[0])
    o_ref[0, 0, :ncp] = dot(h.astype(w2_ref.dtype), w2_ref[0]).astype(o_ref.dtype)
    if o_ref.shape[2] > ncp:
        o_ref[0, 0, ncp:] = jnp.zeros((o_ref.shape[2] - ncp, o_ref.shape[3]), o_ref.dtype)


def compress(zt, zb, cmp_pe, w1, b1, w2, n_keys):
    ncp = _round_up(-(-n_keys // CMP_STRIDE), BF16_ROWS)
    ncl = ncp if ncp <= LANE else _round_up(ncp, LANE)
    _, nb, ch, kdim = zt.shape
    hid = w1.shape[-1]
    d = w2.shape[-1]
    assert ch + SUBLANE >= ncp + 1 and ch % SUBLANE == 0
    w1 = w1.astype(MXU_DTYPE)
    pe = cmp_pe.reshape(2, 2, 1, kdim).astype(jnp.float32)
    z_spec = pl.BlockSpec((1, 1, ch, kdim), lambda s, n: (s, n, 0, 0))
    return pl.pallas_call(
        functools.partial(_compress_kernel, ncp=ncp),
        grid=(2, nb),
        in_specs=[z_spec, z_spec,
                  pl.BlockSpec((1, 2, 1, kdim), lambda s, n: (s, 0, 0, 0)),
                  pl.BlockSpec((1, kdim, hid), lambda s, n: (s, 0, 0)),
                  pl.BlockSpec((1, kdim, hid), lambda s, n: (s, 1, 0)),
                  pl.BlockSpec((1, 1, hid), lambda s, n: (s, 0, 0)),
                  pl.BlockSpec((1, hid, d), lambda s, n: (s, 0, 0))],
        out_specs=pl.BlockSpec((1, 1, ncl, d), lambda s, n: (s, n, 0, 0)),
        out_shape=jax.ShapeDtypeStruct((2, nb, ncl, d), MXU_DTYPE),
        scratch_shapes=[pltpu.VMEM((ch + SUBLANE, hid), jnp.float32)],
        compiler_params=pltpu.CompilerParams(
            dimension_semantics=("parallel", "parallel"), vmem_limit_bytes=VMEM_LIMIT),
        name="compress",
    )(zt, zb, pe, w1, w1, b1.reshape(2, 1, hid).astype(jnp.float32), w2.astype(MXU_DTYPE))


def pad_rows(a, n):
    return jnp.pad(a, [(0, 0), (0, n - a.shape[1])] + [(0, 0)] * (a.ndim - 2))


def _rope_tables(pos, d):
    half = d // ROT_FRACTION // 2
    inv = ROPE_THETA ** (-jnp.arange(half, dtype=jnp.float32) / half)
    ang = pos.astype(jnp.float32)[:, None] * inv[None, :]
    lane = jnp.arange(LANE) % d
    cos = jnp.cos(ang)[:, lane % half]
    sin = jnp.sin(ang)[:, lane % half]
    one, zero = jnp.ones_like(cos), jnp.zeros_like(cos)
    c = jnp.where(lane < 2 * half, cos, one)
    sa = jnp.where((lane >= half) & (lane < 2 * half), sin, zero)
    sb = jnp.where(lane < half, -sin, zero)
    return jnp.stack([c, sa, sb])


def _rope(x, t_ref, half):
    c, sa, sb = t_ref[0], t_ref[1], t_ref[2]
    outs = []
    for j in range(x.shape[1] // LANE):
        xs = x[:, j * LANE:(j + 1) * LANE]
        outs.append(xs * c + pltpu.roll(xs, half, 1) * sa + pltpu.roll(xs, LANE - half, 1) * sb)
    return outs[0] if len(outs) == 1 else jnp.concatenate(outs, axis=1)


_QA = N_HEADS_A * HEAD_DIM_A
_KVA = 6 * KV_A
_QB = N_HEADS_B * HEAD_DIM_B
_KVB = 2 * HEAD_DIM_B
_QI = IDX_HEADS * IDX_DIM
_MISC = IDX_DIM + IDX_HEADS + 3 * N_HEADS_A
PREP_WIDTH = _QA + _KVA + _QB + _KVB + _QI + _MISC
assert _MISC == LANE


def _prep_kernel(x_ref, t64_ref, t128_ref, pe_ref, qc_ref, qr_ref, qb_ref, qi_ref, nsa_ref, win_ref, dsa_ref,
                 ks_ref, vs_ref, kw_ref, vw_ref, kb_ref, vb_ref, ki_ref, wi_ref, gate_ref, *z_refs, tm):
    x = x_ref[...]
    o = 0
    qa = x[:, o:o + _QA]; o += _QA
    kva = x[:, o:o + _KVA]; o += _KVA
    qb = x[:, o:o + _QB]; o += _QB
    kvb = x[:, o:o + _KVB]; o += _KVB
    qi = x[:, o:o + _QI]; o += _QI
    misc = x[:, o:o + _MISC]
    h64, h128 = HEAD_DIM_A // ROT_FRACTION // 2, HEAD_DIM_B // ROT_FRACTION // 2
    sec = lambda i: kva[:, i * KV_A:(i + 1) * KV_A]
    mx = lambda a: a.astype(qc_ref.dtype)
    qc_ref[...] = mx(qa * HEAD_DIM_A ** -0.5)
    qr_ref[...] = mx(_rope(qa, t64_ref, h64) * HEAD_DIM_A ** -0.5)
    k_slc, k_win = _rope(sec(2), t64_ref, h64), _rope(sec(4), t64_ref, h64)
    nsa_ref[:, :2 * KV_A] = kva[:, :2 * KV_A]
    nsa_ref[:, 2 * KV_A:3 * KV_A] = k_slc
    nsa_ref[:, 3 * KV_A:] = sec(3)
    win_ref[:, :KV_A] = k_win
    win_ref[:, KV_A:] = sec(5)
    ks_ref[...], vs_ref[...], kw_ref[...], vw_ref[...] = mx(k_slc), mx(sec(3)), mx(k_win), mx(sec(5))
    qb_ref[...] = mx(_rope(qb, t128_ref, h128))
    k_b, v_b = _rope(kvb[:, :HEAD_DIM_B], t128_ref, h128), kvb[:, HEAD_DIM_B:]
    k_idx = _rope(misc, t64_ref, h64)[:, :IDX_DIM]
    dsa_ref[:, :HEAD_DIM_B] = k_b
    dsa_ref[:, HEAD_DIM_B:2 * HEAD_DIM_B] = v_b
    dsa_ref[:, 2 * HEAD_DIM_B:] = k_idx
    kb_ref[...], vb_ref[...], ki_ref[...] = mx(k_b), mx(v_b), mx(k_idx)
    qi_ref[...] = mx(_rope(qi, t64_ref, h64) * IDX_DIM ** -0.5)
    wi_ref[...] = misc[:, IDX_DIM:IDX_DIM + IDX_HEADS] * IDX_HEADS ** -0.5
    gate_ref[...] = jax.nn.sigmoid(misc[:, IDX_DIM + IDX_HEADS:])
    if z_refs:
        zt_ref, zb_ref, tok_ref = z_refs
        for s_ in range(2):
            tok_ref[s_] = sec(s_)
            for g, zg in enumerate(_chunk_rows(tok_ref, s_, 0, tm // CMP_STRIDE)):
                zt_ref[s_, 0, g] = (zg + pe_ref[s_, 0]).astype(zt_ref.dtype)
                zb_ref[s_, 0, g] = (zg + pe_ref[s_, 1]).astype(zb_ref.dtype)


def prep(proj, pos, cmp_pe, with_chunks):
    b, t, width = proj.shape
    assert width == PREP_WIDTH
    m = b * t
    tm = 256 if t % 256 == 0 else m
    assert m % tm == 0 and t % tm in (0, t)
    nt = max(t // tm, 1)
    flat = CMP_STRIDE * HEAD_DIM_A
    names = ['qc', 'qr', 'qb', 'qi', 'nsa_rows', 'win_rows', 'dsa_rows', 'ks', 'vs', 'kw', 'vw', 'kb', 'vb', 'ki',
             'wi', 'gates']
    widths = [_QA, _QA, _QB, _QI, NSA_SECTIONS * KV_A, 2 * KV_A, DSA_ROW, KV_A, KV_A, KV_A, KV_A, HEAD_DIM_B,
              HEAD_DIM_B, IDX_DIM, IDX_HEADS, 3 * N_HEADS_A]
    dtypes = [MXU_DTYPE] * 4 + [jnp.float32] * 3 + [MXU_DTYPE] * 7 + [jnp.float32] * 2
    row = lambda w: pl.BlockSpec((tm, w), lambda i: (i, 0))
    out_specs = [row(w) for w in widths]
    out_shape = [jax.ShapeDtypeStruct((m, w), dt) for w, dt in zip(widths, dtypes)]
    scratch = []
    if with_chunks:
        assert tm % CMP_STRIDE == 0 and t % tm == 0
        z_spec = pl.BlockSpec((2, 1, KV_GROUPS_A, tm // CMP_STRIDE, flat), lambda i: (0, i // nt, 0, i % nt, 0))
        z_shape = jax.ShapeDtypeStruct((2, b, KV_GROUPS_A, t // CMP_STRIDE, flat), MXU_DTYPE)
        out_specs += [z_spec, z_spec]
        out_shape += [z_shape, z_shape]
        names += ['zt', 'zb']
        scratch = [pltpu.VMEM((2, tm, KV_A), jnp.float32)]
    t_spec = pl.BlockSpec((3, tm, LANE), lambda i: (0, i, 0))
    outs = pl.pallas_call(
        functools.partial(_prep_kernel, tm=tm),
        grid=(m // tm,),
        in_specs=[row(width), t_spec, t_spec, pl.BlockSpec((2, 2, 1, flat), lambda i: (0, 0, 0, 0))],
        out_specs=out_specs,
        out_shape=out_shape,
        scratch_shapes=scratch,
        compiler_params=pltpu.CompilerParams(dimension_semantics=("parallel",), vmem_limit_bytes=VMEM_LIMIT),
        name="prep",
    )(proj.reshape(m, width), _rope_tables(pos, HEAD_DIM_A), _rope_tables(pos, HEAD_DIM_B),
      cmp_pe.reshape(2, 2, 1, flat).astype(jnp.float32))
    return {n: (o if o.ndim > 2 else o.reshape(b, t, o.shape[-1])) for n, o in zip(names, outs)}


def project(x, g, w_in, sizes):
    starts = [sum(sizes[:i]) for i in range(len(sizes))]
    qa, kva, ga, qb, kvb, qi, ki, wi, gm = [slice(o, o + n) for o, n in zip(starts, sizes)]
    w_bf = w_in.astype(MXU_DTYPE)
    w_prep = jnp.concatenate([w_bf[:, c] for c in (qa, kva, qb, kvb, qi, ki, wi, ga)], axis=1)
    return norm_matmul(x, g, w_prep), norm_matmul(x, g, w_bf[:, gm])


def mixer_prompt(x, g, w_in, cmp_pe, cmp_w1, cmp_b1, cmp_w2, sizes):
    b, s, _ = x.shape
    proj, gm = project(x, g, w_in, sizes)
    p = prep(proj, jnp.tile(jnp.arange(s, dtype=jnp.int32), b), cmp_pe, True)
    merge_bg = lambda a: a.reshape((2, b * KV_GROUPS_A) + a.shape[3:])
    kvc = compress(merge_bg(p['zt']), merge_bg(p['zb']), cmp_pe, cmp_w1, cmp_b1, cmp_w2, s)
    kvc = kvc.reshape(2, b, KV_GROUPS_A, kvc.shape[2], HEAD_DIM_A)
    o_nsa = nsa_attention(p['qc'], p['qr'], p['gates'], kvc, p['ks'], p['vs'], p['kw'], p['vw'],
                          tq=min(TQ, s), qpos0=0, wbase=0, n_keys=s)
    o_dsa = dsa_attention(p['qb'], p['qi'], p['wi'], p['kb'], p['vb'], p['ki'],
                          tq=min(TQ, s), qpos0=0, n_keep=min(DSA_TOPK, s // 4))
    nsa_rows = p['nsa_rows'].reshape(b, s, NSA_SECTIONS, KV_GROUPS_A, HEAD_DIM_A)
    win_state = p['win_rows'].reshape(b, s, 2, KV_GROUPS_A, HEAD_DIM_A)[:, -min(WINDOW, s):]
    return (o_nsa, o_dsa, gm), nsa_rows, win_state, p['dsa_rows']


def mixer_sample(x, g, cache_nsa, win_buf, cache_dsa, page_table, w_in, cmp_pe, cmp_w1, cmp_b1, cmp_w2, sizes):
    b, t, _ = x.shape
    page = cache_nsa.shape[1]
    past_len = page_table.shape[1] * page
    n_keys = past_len + t
    proj, gm = project(x, g, w_in, sizes)
    p = prep(proj, jnp.tile(past_len + jnp.arange(t, dtype=jnp.int32), b), cmp_pe, False)
    pad = lambda a: pad_rows(a, TQ_STEP)

    cache_nsa_t = jnp.transpose(cache_nsa, (0, 2, 3, 4, 1)).reshape(cache_nsa.shape[0], NSA_SECTIONS * KV_A, page)
    zt, zb, ks, vs = nsa_gather(cache_nsa_t, page_table, p['nsa_rows'], cmp_pe)
    merge_bg = lambda a: a.reshape((2, b * KV_GROUPS_A) + a.shape[3:])
    kvc = compress(merge_bg(zt), merge_bg(zb), cmp_pe, cmp_w1, cmp_b1, cmp_w2, n_keys)
    kvc = kvc.reshape(2, b, KV_GROUPS_A, kvc.shape[2], HEAD_DIM_A)
    w_len = win_buf.shape[1]
    win_new = p['win_rows'].reshape(b, t, 2, KV_GROUPS_A, HEAD_DIM_A)
    win_all = jnp.concatenate([win_buf, win_new], axis=1)
    win_pad = pad_rows(win_all, w_len + WIN_CHUNK).astype(MXU_DTYPE)
    kw = win_pad[:, :, 0].reshape(b, w_len + WIN_CHUNK, KV_A)
    vw = win_pad[:, :, 1].reshape(b, w_len + WIN_CHUNK, KV_A)
    o_nsa = nsa_attention(pad(p['qc']), pad(p['qr']), pad(p['gates']), kvc, ks, vs, kw, vw,
                          tq=TQ_STEP, qpos0=past_len, wbase=past_len - w_len, n_keys=n_keys, kv_t=True)

    kb_, vb, ki = dsa_gather(jnp.swapaxes(cache_dsa, 1, 2), page_table, p['dsa_rows'])
    o_dsa = dsa_attention(pad(p['qb']), pad(p['qi']), pad(p['wi']), kb_, vb, ki, tq=TQ_STEP, qpos0=past_len,
                          n_keep=min(DSA_TOPK, n_keys // 4), kv_t=True)
    nsa_rows = p['nsa_rows'].reshape(b, t, NSA_SECTIONS, KV_GROUPS_A, HEAD_DIM_A)
    return (o_nsa, o_dsa, pad(gm)), nsa_rows, win_all[:, -w_len:], p['dsa_rows']


def kernel(x_prompt, x_sample, mem_prompt, cache_nsa_kv, state_nsa_win, cache_dsa_kv, cache_mem_kv, state_conv,
           page_table, norm_g, w_in, cmp_pe, cmp_w1, cmp_b1, cmp_w2, w_out_a, w_out_b, w_out, w_mem_q, w_mem_kv,
           w_mem_out, w_up, conv_w, conv_b, w_down, final_g):
    depth = w_in.shape[0]
    d_model = x_prompt.shape[-1]
    d_ff = w_down.shape[1]
    assert CONV_WIDTH == 3
    sizes = (N_HEADS_A * HEAD_DIM_A, 6 * KV_A, 3 * N_HEADS_A, N_HEADS_B * HEAD_DIM_B, 2 * HEAD_DIM_B,
             IDX_HEADS * IDX_DIM, IDX_DIM, IDX_HEADS, 2 * d_model)
    xp, xs = x_prompt, pad_rows(x_sample, TQ_STEP)
    t_step = x_sample.shape[1]
    nsa_p, nsa_s, win_p, win_s, dsa_p, dsa_s, mem_p, conv_p, conv_s = [], [], [], [], [], [], [], [], []
    for l in range(depth):
        assert l == depth - 1, "the fused FFN epilogue applies the final norm"
        branches_p, a, bwin, c = mixer_prompt(xp, norm_g[l, 0], w_in[l], cmp_pe[l], cmp_w1[l], cmp_b1[l], cmp_w2[l],
                                              sizes)
        nsa_p.append(a); win_p.append(bwin); dsa_p.append(c)
        branches_s, a, bwin, c = mixer_sample(xs[:, :t_step], norm_g[l, 0], cache_nsa_kv[l], state_nsa_win[l],
                                              cache_dsa_kv[l], page_table, w_in[l], cmp_pe[l], cmp_w1[l], cmp_b1[l],
                                              cmp_w2[l], sizes)
        nsa_s.append(a); win_s.append(bwin); dsa_s.append(c)
        kv_p = norm_matmul(mem_prompt, norm_g[l, 2], w_mem_kv[l])
        mem_p.append(kv_p.reshape(kv_p.shape[:2] + (2, MEM_HEADS, MEM_HEAD_DIM)))
        kv_s = cache_mem_kv[l].reshape(cache_mem_kv.shape[1:3] + (-1,))
        weights = (norm_g[l, 1], norm_g[l, 3], final_g, w_out_a[l], w_out_b[l], w_out[l], w_mem_q[l], w_mem_out[l],
                   w_up[l], conv_w[l], conv_b[l], w_down[l])
        xp, cp = dense_tail(xp, *branches_p, kv_p, jnp.zeros((xp.shape[0], CONV_WIDTH - 1, 2 * d_ff), xp.dtype),
                            xp.shape[1], *weights)
        xs, cs = dense_tail(xs, *branches_s, kv_s, state_conv[l], t_step, *weights)
        conv_p.append(cp); conv_s.append(cs)
    y_prompt, y_sample = xp, xs[:, :t_step]
    return (y_prompt, y_sample, jnp.stack(nsa_p), jnp.stack(nsa_s), jnp.stack(win_p), jnp.stack(win_s),
            jnp.stack(dsa_p), jnp.stack(dsa_s), jnp.stack(mem_p), jnp.stack(conv_p), jnp.stack(conv_s))
```

```python
import functools

import jax
import jax.numpy as jnp
from jax import lax
from jax.experimental import pallas as pl
from jax.experimental.pallas import tpu as pltpu

N_HEADS_A = 16
HEAD_DIM_A = 64
KV_GROUPS_A = 2
CMP_BLOCK = 32
CMP_STRIDE = 16
SLC_BLOCK = 64
N_SELECT = 16
WINDOW = 512
N_HEADS_B = 8
HEAD_DIM_B = 128
IDX_HEADS = 16
IDX_DIM = 64
DSA_TOPK = 256
MEM_HEADS = 4
MEM_HEAD_DIM = 128
CONV_WIDTH = 3
ROPE_THETA = 500000.0
ROT_FRACTION = 4
EPS = 1e-6
KV_A = KV_GROUPS_A * HEAD_DIM_A
CMP_PER_SLC = SLC_BLOCK // CMP_STRIDE
HPG_A = N_HEADS_A // KV_GROUPS_A
DSA_ROW = 2 * HEAD_DIM_B + IDX_DIM
NSA_SECTIONS = 4

LANE = 128
SUBLANE = 8
BF16_ROWS = 16
VMEM_LIMIT = 48 * 1024 * 1024

NEG_INF = float('-inf')
POS_INF = float('inf')
MXU_DTYPE = jnp.bfloat16
TQ = 128
TQ_STEP = BF16_ROWS
KEY_CHUNK = 512
WIN_CHUNK = 128
PAGES_PER_STEP = 8
BISECT_ITERS = 40
TIE_STRIP_ITERS = 64
SLC_SHIFT = SLC_BLOCK.bit_length() - 1
CMP_PER_SLC_SHIFT = CMP_PER_SLC.bit_length() - 1


def _round_up(n, m):
    return -(-n // m) * m


def _tile(n, cap):
    if n <= cap:
        return n
    best = None
    for t in range(LANE, cap + 1, LANE):
        if n % t == 0:
            best = t
    assert best is not None, (n, cap)
    return best


def _mm_kernel(x_ref, w_ref, o_ref, acc_ref):
    @pl.when(pl.program_id(2) == 0)
    def _():
        acc_ref[...] = jnp.zeros_like(acc_ref)

    acc_ref[...] += jnp.dot(x_ref[...], w_ref[...], preferred_element_type=jnp.float32)

    @pl.when(pl.program_id(2) == pl.num_programs(2) - 1)
    def _():
        o_ref[...] = acc_ref[...].astype(o_ref.dtype)


def matmul(x, w, out_dtype=jnp.float32):
    lead = x.shape[:-1]
    k = x.shape[-1]
    n = w.shape[-1]
    x2 = x.reshape(-1, k).astype(MXU_DTYPE)
    w2 = w.astype(MXU_DTYPE)
    m0 = x2.shape[0]
    tm = 512 if m0 >= 512 else _round_up(m0, BF16_ROWS)
    m = _round_up(m0, tm)
    if m != m0:
        x2 = jnp.pad(x2, ((0, m - m0), (0, 0)))
    tn = _tile(n, 1024)
    tk = _tile(k, 2048)
    out = pl.pallas_call(
        _mm_kernel,
        grid=(m // tm, n // tn, k // tk),
        in_specs=[pl.BlockSpec((tm, tk), lambda i, j, l: (i, l)),
                  pl.BlockSpec((tk, tn), lambda i, j, l: (l, j))],
        out_specs=pl.BlockSpec((tm, tn), lambda i, j, l: (i, j)),
        out_shape=jax.ShapeDtypeStruct((m, n), out_dtype),
        scratch_shapes=[pltpu.VMEM((tm, tn), jnp.float32)],
        compiler_params=pltpu.CompilerParams(
            dimension_semantics=("parallel", "parallel", "arbitrary"),
            vmem_limit_bytes=VMEM_LIMIT),
        name="matmul",
    )(x2, w2)
    return out[:m0].reshape(lead + (n,))


def _rms(x, g):
    return x * lax.rsqrt(jnp.mean(x * x, axis=-1, keepdims=True) + EPS) * g


def _norm_mm_kernel(x_ref, g_ref, w_ref, o_ref, xn_ref):
    @pl.when(pl.program_id(1) == 0)
    def _():
        xn_ref[...] = _rms(x_ref[...], g_ref[...]).astype(xn_ref.dtype)

    o_ref[...] = jnp.dot(xn_ref[...], w_ref[...], preferred_element_type=jnp.float32).astype(o_ref.dtype)


def norm_matmul(x, g, w):
    lead = x.shape[:-1]
    k = x.shape[-1]
    n = w.shape[-1]
    x2 = x.reshape(-1, k)
    m = x2.shape[0]
    tm = 512 if m % 512 == 0 else m
    tn = _tile(n, 1536)
    out = pl.pallas_call(
        _norm_mm_kernel,
        grid=(m // tm, n // tn),
        in_specs=[pl.BlockSpec((tm, k), lambda i, j: (i, 0)),
                  pl.BlockSpec((1, k), lambda i, j: (0, 0)),
                  pl.BlockSpec((k, tn), lambda i, j: (0, j))],
        out_specs=pl.BlockSpec((tm, tn), lambda i, j: (i, j)),
        out_shape=jax.ShapeDtypeStruct((m, n), jnp.float32),
        scratch_shapes=[pltpu.VMEM((tm, k), MXU_DTYPE)],
        compiler_params=pltpu.CompilerParams(
            dimension_semantics=("parallel", "arbitrary"), vmem_limit_bytes=VMEM_LIMIT),
        name="norm_matmul",
    )(x2, g.reshape(1, k).astype(jnp.float32), w.astype(MXU_DTYPE))
    return out.reshape(lead + (n,))


def _gated_merge_kernel(oa_ref, ob_ref, wa_ref, wb_ref, ga_ref, gb_ref, o_ref):
    dot = functools.partial(jnp.dot, preferred_element_type=jnp.float32)
    ya = dot(oa_ref[...], wa_ref[...])
    yb = dot(ob_ref[...], wb_ref[...])
    o_ref[...] = (jax.nn.sigmoid(ga_ref[...]) * ya + jax.nn.sigmoid(gb_ref[...]) * yb).astype(o_ref.dtype)


def gated_merge(o_nsa, o_dsa, gm, w_oa, w_ob):
    m, ka = o_nsa.shape
    kb = o_dsa.shape[1]
    n = w_oa.shape[1]
    tm = 512 if m % 512 == 0 else m
    tn = _tile(n, 1024)
    nb = n // tn
    return pl.pallas_call(
        _gated_merge_kernel,
        grid=(m // tm, nb),
        in_specs=[pl.BlockSpec((tm, ka), lambda i, j: (i, 0)),
                  pl.BlockSpec((tm, kb), lambda i, j: (i, 0)),
                  pl.BlockSpec((ka, tn), lambda i, j: (0, j)),
                  pl.BlockSpec((kb, tn), lambda i, j: (0, j)),
                  pl.BlockSpec((tm, tn), lambda i, j: (i, j)),
                  pl.BlockSpec((tm, tn), lambda i, j: (i, j + nb))],
        out_specs=pl.BlockSpec((tm, tn), lambda i, j: (i, j)),
        out_shape=jax.ShapeDtypeStruct((m, n), MXU_DTYPE),
        compiler_params=pltpu.CompilerParams(
            dimension_semantics=("parallel", "parallel"), vmem_limit_bytes=VMEM_LIMIT),
        name="gated_merge",
    )(o_nsa, o_dsa, w_oa.astype(MXU_DTYPE), w_ob.astype(MXU_DTYPE), gm, gm)


def _out_proj_kernel(z_ref, w_ref, x_ref, o_ref):
    o_ref[...] = x_ref[...] + jnp.dot(z_ref[...], w_ref[...], preferred_element_type=jnp.float32)


def out_proj_residual(z, w, x):
    m, k = z.shape
    n = w.shape[1]
    tm = 256 if m % 256 == 0 else m
    return pl.pallas_call(
        _out_proj_kernel,
        grid=(m // tm,),
        in_specs=[pl.BlockSpec((tm, k), lambda i: (i, 0)),
                  pl.BlockSpec((k, n), lambda i: (0, 0)),
                  pl.BlockSpec((tm, n), lambda i: (i, 0))],
        out_specs=pl.BlockSpec((tm, n), lambda i: (i, 0)),
        out_shape=jax.ShapeDtypeStruct((m, n), jnp.float32),
        compiler_params=pltpu.CompilerParams(dimension_semantics=("parallel",), vmem_limit_bytes=VMEM_LIMIT),
        name="out_proj_residual",
    )(z, w.astype(MXU_DTYPE), x)


def _mem_block_kernel(x_ref, g1_ref, g2_ref, wq_ref, kv_ref, wo_ref, x_out_ref, xn_out_ref):
    d = MEM_HEAD_DIM
    hd = MEM_HEADS * d
    x = x_ref[0]
    xn = _rms(x, g1_ref[...]).astype(wq_ref.dtype)
    q = jnp.dot(xn, wq_ref[...], preferred_element_type=jnp.float32).astype(wq_ref.dtype)
    outs = []
    for h in range(MEM_HEADS):
        k = kv_ref[0, :, h * d:(h + 1) * d].astype(wq_ref.dtype)
        v = kv_ref[0, :, hd + h * d:hd + (h + 1) * d].astype(wq_ref.dtype)
        s = _dot_nt(q[:, h * d:(h + 1) * d], k) * d ** -0.5
        e = jnp.exp(s - jnp.max(s, axis=1, keepdims=True))
        p = e / jnp.sum(e, axis=1, keepdims=True)
        outs.append(jnp.dot(p.astype(v.dtype), v, preferred_element_type=jnp.float32))
    o = jnp.concatenate(outs, axis=1).astype(wo_ref.dtype)
    x2 = x + jnp.dot(o, wo_ref[...], preferred_element_type=jnp.float32)
    x_out_ref[0] = x2
    xn_out_ref[0] = _rms(x2, g2_ref[...]).astype(xn_out_ref.dtype)


def mem_block(x, g1, g2, w_q, kv, w_o):
    b, t, dm = x.shape
    mt, kvw = kv.shape[1:]
    hd = w_q.shape[1]
    tm = 256 if t % 256 == 0 else t
    row_spec = pl.BlockSpec((1, tm, dm), lambda bi, i: (bi, i, 0))
    g_spec = pl.BlockSpec((1, dm), lambda bi, i: (0, 0))
    return pl.pallas_call(
        _mem_block_kernel,
        grid=(b, t // tm),
        in_specs=[row_spec, g_spec, g_spec,
                  pl.BlockSpec((dm, hd), lambda bi, i: (0, 0)),
                  pl.BlockSpec((1, mt, kvw), lambda bi, i: (bi, 0, 0)),
                  pl.BlockSpec((hd, dm), lambda bi, i: (0, 0))],
        out_specs=[row_spec, row_spec],
        out_shape=[jax.ShapeDtypeStruct((b, t, dm), jnp.float32), jax.ShapeDtypeStruct((b, t, dm), MXU_DTYPE)],
        compiler_params=pltpu.CompilerParams(
            dimension_semantics=("parallel", "parallel"), vmem_limit_bytes=VMEM_LIMIT),
        name="mem_block",
    )(x, g1.reshape(1, dm).astype(jnp.float32), g2.reshape(1, dm).astype(jnp.float32), w_q.astype(MXU_DTYPE),
      kv, w_o.astype(MXU_DTYPE))


def _ffn_up_kernel(x_ref, halo_ref, wg_ref, wu_ref, cwg_ref, cwu_ref, cbg_ref, cbu_ref, pg_ref, pu_ref,
                   h_ref, sg_ref, su_ref, *, nb, tm, t_real):
    i = pl.program_id(2)
    n_halo = halo_ref.shape[1]
    tn = wg_ref.shape[1]
    rows = nb * tm
    x = x_ref[...].reshape(rows, x_ref.shape[2])
    if nb == 1:
        x = jnp.concatenate([halo_ref[0], x], axis=0)
    row = _iota((nb, tm, 1), 1).reshape(rows, 1)
    last = (t_real - 1) // tm
    r_last = (t_real - 1) % tm
    spread = lambda p: jnp.broadcast_to(p, (nb, tm, tn)).reshape(rows, tn)

    def branch(w_ref, cw_ref, cb_ref, p_ref, s_ref):
        u = jnp.dot(x, w_ref[...], preferred_element_type=jnp.float32)
        p0, p1 = p_ref[:, 0:1, :], p_ref[:, 1:2, :]
        if nb == 1:
            uh, u = u[:n_halo], u[n_halo:]
            p0 = jnp.where(i == 0, p0, uh[n_halo - 2:n_halo - 1][None])
            p1 = jnp.where(i == 0, p1, uh[n_halo - 1:n_halo][None])
        p0, p1 = spread(p0), spread(p1)
        u1 = jnp.where(row == 0, p1, pltpu.roll(u, 1, 0))
        u2 = jnp.where(row == 0, p0, jnp.where(row == 1, p1, pltpu.roll(u, 2, 0)))

        @pl.when(i == last)
        def _():
            s_ref[...] = u.reshape(nb, tm, tn)[:, r_last - 1:r_last + 1, :]

        return cb_ref[...] + u2 * cw_ref[0:1, :] + u1 * cw_ref[1:2, :] + u * cw_ref[2:3, :]

    gate = branch(wg_ref, cwg_ref, cbg_ref, pg_ref, sg_ref)
    up = branch(wu_ref, cwu_ref, cbu_ref, pu_ref, su_ref)
    h_ref[...] = (jax.nn.silu(gate) * up).reshape(nb, tm, tn).astype(h_ref.dtype)


def ffn_up(xn, prev, w_up, conv_w, conv_b, t_real):
    b, t, dm = xn.shape
    f2 = w_up.shape[1]
    f = f2 // 2
    tm = 512 if t % 512 == 0 else t
    nb = 1 if t > tm else max(1, min(b, 512 // tm))
    while b % nb:
        nb -= 1
    tn = _tile(f, 512)
    nf = f // tn
    halo = min(BF16_ROWS, tm)
    hpt = tm // halo
    assert t_real >= 2 and (t_real - 1) % tm >= 1 and tm % SUBLANE == 0
    w_up = w_up.astype(MXU_DTYPE)
    conv_b = conv_b.reshape(1, f2)
    col = lambda off: (lambda j, bi, i: (0, j + off))
    st = lambda off: (lambda j, bi, i: (bi, 0, j + off))
    specs = [pl.BlockSpec((nb, tm, dm), lambda j, bi, i: (bi, i, 0)),
             pl.BlockSpec((1, halo, dm), lambda j, bi, i: (bi * nb, jnp.maximum(i * hpt - 1, 0), 0)),
             pl.BlockSpec((dm, tn), col(0)), pl.BlockSpec((dm, tn), col(nf)),
             pl.BlockSpec((CONV_WIDTH, tn), col(0)), pl.BlockSpec((CONV_WIDTH, tn), col(nf)),
             pl.BlockSpec((1, tn), col(0)), pl.BlockSpec((1, tn), col(nf)),
             pl.BlockSpec((nb, 2, tn), st(0)), pl.BlockSpec((nb, 2, tn), st(nf))]
    h, sg, su = pl.pallas_call(
        functools.partial(_ffn_up_kernel, nb=nb, tm=tm, t_real=t_real),
        grid=(nf, b // nb, t // tm),
        in_specs=specs,
        out_specs=[pl.BlockSpec((nb, tm, tn), lambda j, bi, i: (bi, i, j)),
                   pl.BlockSpec((nb, 2, tn), st(0)), pl.BlockSpec((nb, 2, tn), st(0))],
        out_shape=[jax.ShapeDtypeStruct((b, t, f), MXU_DTYPE), jax.ShapeDtypeStruct((b, 2, f), jnp.float32),
                   jax.ShapeDtypeStruct((b, 2, f), jnp.float32)],
        compiler_params=pltpu.CompilerParams(
            dimension_semantics=("parallel", "parallel", "arbitrary"), vmem_limit_bytes=VMEM_LIMIT),
        name="ffn_up",
    )(xn, xn, w_up, w_up, conv_w, conv_w, conv_b, conv_b, prev, prev)
    return h, jnp.concatenate([sg, su], axis=-1)


def _ffn_down_kernel(h_ref, w_ref, x_ref, g_ref, o_ref, acc_ref):
    @pl.when(pl.program_id(1) == 0)
    def _():
        acc_ref[...] = x_ref[...]

    acc_ref[...] += jnp.dot(h_ref[...], w_ref[...], preferred_element_type=jnp.float32)

    @pl.when(pl.program_id(1) == pl.num_programs(1) - 1)
    def _():
        o_ref[...] = _rms(acc_ref[...], g_ref[...])


def ffn_down_norm(h, w_down, x, g):
    m, f = h.shape
    dm = w_down.shape[1]
    tm = 512 if m % 512 == 0 else m
    tk = _tile(f, 2048)
    return pl.pallas_call(
        _ffn_down_kernel,
        grid=(m // tm, f // tk),
        in_specs=[pl.BlockSpec((tm, tk), lambda i, l: (i, l)),
                  pl.BlockSpec((tk, dm), lambda i, l: (l, 0)),
                  pl.BlockSpec((tm, dm), lambda i, l: (i, 0)),
                  pl.BlockSpec((1, dm), lambda i, l: (0, 0))],
        out_specs=pl.BlockSpec((tm, dm), lambda i, l: (i, 0)),
        out_shape=jax.ShapeDtypeStruct((m, dm), jnp.float32),
        scratch_shapes=[pltpu.VMEM((tm, dm), jnp.float32)],
        compiler_params=pltpu.CompilerParams(
            dimension_semantics=("parallel", "arbitrary"), vmem_limit_bytes=VMEM_LIMIT),
        name="ffn_down_norm",
    )(h, w_down.astype(MXU_DTYPE), x, g.reshape(1, dm).astype(jnp.float32))


def dense_tail(x, o_nsa, o_dsa, gm, kv_mem, prev_u, t_real, g_mem, g_ffn, g_final, w_oa, w_ob, w_o, w_mq, w_mo,
               w_up, conv_w, conv_b, w_down):
    b, t, dm = x.shape
    rows = lambda a: a.reshape(b * t, a.shape[-1])
    z = gated_merge(rows(o_nsa), rows(o_dsa), rows(gm), w_oa, w_ob)
    x1 = out_proj_residual(z, w_o, rows(x)).reshape(b, t, dm)
    x2, xn2 = mem_block(x1, g_mem, g_ffn, w_mq, kv_mem, w_mo)
    h, state = ffn_up(xn2, prev_u, w_up, conv_w, conv_b, t_real)
    y = ffn_down_norm(rows(h), w_down, rows(x2), g_final)
    return y.reshape(b, t, dm), state


def _dot_nt(a, b):
    return lax.dot_general(a, b, (((1,), (1,)), ((), ())), preferred_element_type=jnp.float32)


def _iota(shape, dim):
    return lax.broadcasted_iota(jnp.int32, shape, dim)


def _flash_init(rows, d):
    return (jnp.full((rows, 1), NEG_INF, jnp.float32), jnp.zeros((rows, 1), jnp.float32),
            jnp.zeros((rows, d), jnp.float32))


def _flash_step(carry, q, k, v, madd, nh, scale=None, kv_t=False):
    m, l, acc = carry
    s = jnp.dot(q, k, preferred_element_type=jnp.float32) if kv_t else _dot_nt(q, k)
    if scale is not None:
        s = s * scale
    r, kb = s.shape
    s = (s.reshape(nh, r // nh, kb) + madd[None]).reshape(r, kb)
    m_new = jnp.maximum(m, jnp.max(s, axis=1, keepdims=True))
    m_safe = jnp.where(m_new == NEG_INF, 0.0, m_new)
    p = jnp.exp(s - m_safe)
    alpha = jnp.exp(m - m_safe)
    l = alpha * l + jnp.sum(p, axis=1, keepdims=True)
    pv = _dot_nt(p.astype(v.dtype), v) if kv_t else jnp.dot(p.astype(v.dtype), v, preferred_element_type=jnp.float32)
    return m_new, l, alpha * acc + pv


def _flash_finish(carry):
    _, l, acc = carry
    return acc / jnp.maximum(l, 1e-30)


def _split_dot(x, m01):
    hi = x.astype(jnp.bfloat16)
    r1 = x - hi.astype(jnp.float32)
    mid = r1.astype(jnp.bfloat16)
    lo = (r1 - mid.astype(jnp.float32)).astype(jnp.bfloat16)
    dot = functools.partial(jnp.dot, preferred_element_type=jnp.float32)
    return dot(hi, m01) + dot(mid, m01) + dot(lo, m01)


def _nsa_group(g, i, qc_ref, qr_ref, gate_ref, kc_ref, vc_ref, ks_ref, vs_ref, kw_ref, vw_ref, mask_ref,
               *, tq, kb, qpos0, wbase, ns, n_sel, kv_t):
    nh, d = HPG_A, HEAD_DIM_A
    rows = nh * tq
    nc = kc_ref.shape[3]
    nsp = _round_up(ns, LANE)
    t0 = qpos0 + i * tq
    nj = (t0 + tq - 1) // kb + 1
    cols = slice(g * d, (g + 1) * d)

    def stack_heads(q_ref):
        return jnp.concatenate([q_ref[0, :, (g * nh + h) * d:(g * nh + h + 1) * d] for h in range(nh)], axis=0)

    qc = stack_heads(qc_ref)
    qr = stack_heads(qr_ref)
    tpos = t0 + _iota((tq, 1), 0)

    blk_last = _iota((1, nc), 1) * CMP_STRIDE + (CMP_BLOCK - 1)
    madd_c = jnp.where(blk_last <= tpos, 0.0, NEG_INF)
    s = _dot_nt(qc, kc_ref[0, 0, g]).reshape(nh, tq, nc) + madd_c[None]
    m = jnp.max(s, axis=2, keepdims=True)
    m = jnp.where(m == NEG_INF, 0.0, m)
    e = jnp.exp(s - m)
    p = e / jnp.maximum(jnp.sum(e, axis=2, keepdims=True), 1e-30)
    o_cmp = jnp.dot(p.reshape(rows, nc).astype(vc_ref.dtype), vc_ref[0, 0, g], preferred_element_type=jnp.float32)

    imp = jnp.sum(p, axis=0)
    c_id = _iota((nc, nsp), 0)
    m_id = _iota((nc, nsp), 1)
    overlap = (jnp.right_shift(c_id, CMP_PER_SLC_SHIFT) == m_id) | (c_id == m_id * CMP_PER_SLC - 1)
    score = _split_dot(imp, overlap.astype(jnp.bfloat16))
    blk = _iota((1, nsp), 1)
    cur = jnp.right_shift(tpos, SLC_SHIFT)
    forced = (blk == 0) | (blk == cur) | (blk == cur - 1)
    sc = jnp.where(forced, POS_INF, jnp.where(blk * SLC_BLOCK <= tpos, score, NEG_INF))
    if tq % LANE == 0 and nsp % LANE == 0:
        sc_t = sc.T
        blk_t = _iota((nsp, 1), 0)
        rank_t = jnp.zeros((nsp, tq), jnp.float32)
        for mp in range(ns):
            ref = sc_t[mp:mp + 1, :]
            beats = (ref > sc_t) | ((ref == sc_t) & (blk_t > mp))
            rank_t = rank_t + jnp.where(beats, 1.0, 0.0)
        sel = jnp.where(rank_t < n_sel, 1.0, 0.0).T.astype(jnp.bfloat16)
    else:
        rank = jnp.zeros((tq, nsp), jnp.float32)
        for mp in range(ns):
            col = sc[:, mp:mp + 1]
            beats = (col > sc) | ((col == sc) & (blk > mp))
            rank = rank + jnp.where(beats, 1.0, 0.0)
        sel = jnp.where(rank < n_sel, 1.0, 0.0).astype(jnp.bfloat16)

    def make_mask(j, _):
        kpos = j * kb + _iota((1, kb), 1)
        expand = (jnp.right_shift(j * kb + _iota((nsp, kb), 1), SLC_SHIFT) == _iota((nsp, kb), 0))
        hit = jnp.dot(sel, expand.astype(jnp.bfloat16), preferred_element_type=jnp.float32)
        mask_ref[j] = jnp.where((hit > 0.5) & (kpos <= tpos), 0.0, NEG_INF)
        return 0

    lax.fori_loop(0, nj, make_mask, 0)

    def slc_body(j, carry):
        off = pl.multiple_of(j * kb, kb)
        return _flash_step(carry, qr, ks_ref[0, pl.ds(off, kb), cols], vs_ref[0, pl.ds(off, kb), cols],
                           mask_ref[j], nh)

    if kv_t:
        o_slc = _flash_finish(_flash_step(_flash_init(rows, d), qr, ks_ref[0, cols, :], vs_ref[0, cols, :],
                                          mask_ref[0], nh, kv_t=True))
    else:
        o_slc = _flash_finish(lax.fori_loop(0, nj, slc_body, _flash_init(rows, d)))

    wk = min(_round_up(WINDOW + tq, WIN_CHUNK), kw_ref.shape[1])
    first = jnp.maximum(t0 - (WINDOW - 1) - wbase, 0) // WIN_CHUNK * WIN_CHUNK
    off = pl.multiple_of(jnp.minimum(first, kw_ref.shape[1] - wk), WIN_CHUNK)
    dist = tpos - (wbase + off + _iota((1, wk), 1))
    madd_w = jnp.where((dist >= 0) & (dist < WINDOW), 0.0, NEG_INF)
    o_win = _flash_finish(_flash_step(_flash_init(rows, d), qr, kw_ref[0, pl.ds(off, wk), cols],
                                      vw_ref[0, pl.ds(off, wk), cols], madd_w, nh))

    gates = gate_ref[0]
    outs = []
    for h in range(nh):
        rs = slice(h * tq, (h + 1) * tq)
        c = g * nh + h
        outs.append(gates[:, c:c + 1] * o_cmp[rs] + gates[:, N_HEADS_A + c:N_HEADS_A + c + 1] * o_slc[rs]
                    + gates[:, 2 * N_HEADS_A + c:2 * N_HEADS_A + c + 1] * o_win[rs])
    return outs


def _nsa_kernel(qc_ref, qr_ref, gate_ref, kc_ref, vc_ref, ks_ref, vs_ref, kw_ref, vw_ref, o_ref, mask_ref, **kw):
    i = pl.program_id(1)
    outs = []
    for g in range(KV_GROUPS_A):
        outs += _nsa_group(g, i, qc_ref, qr_ref, gate_ref, kc_ref, vc_ref, ks_ref, vs_ref, kw_ref, vw_ref,
                           mask_ref, **kw)
    o_ref[0] = jnp.concatenate(outs, axis=1).astype(o_ref.dtype)


def nsa_attention(qc, qr, gates, kvc, ks, vs, kw, vw, *, tq, qpos0, wbase, n_keys, kv_t=False):
    b, t, hd = qc.shape
    l = ks.shape[2] if kv_t else ks.shape[1]
    lw = kw.shape[1]
    nc, d = kvc.shape[3:]
    kb = min(KEY_CHUNK, l) if tq >= TQ else l
    assert l % kb == 0 and t % tq == 0 and lw % WIN_CHUNK == 0 and (kb == l or not kv_t)
    assert (qpos0 + t - 1) // kb + 1 <= l // kb and (qpos0 + t - 1 - wbase) // WIN_CHUNK + 1 <= lw // WIN_CHUNK
    assert (qpos0 - wbase) % WIN_CHUNK == 0 and WIN_CHUNK % tq == 0 and tq > 1
    ns = l // SLC_BLOCK
    n_sel = min(N_SELECT, -(-n_keys // SLC_BLOCK))
    q_spec = pl.BlockSpec((1, tq, hd), lambda bi, i: (bi, i, 0))
    kc_spec = pl.BlockSpec((1, 1, KV_GROUPS_A, nc, d), lambda bi, i: (0, bi, 0, 0, 0))
    vc_spec = pl.BlockSpec((1, 1, KV_GROUPS_A, nc, d), lambda bi, i: (1, bi, 0, 0, 0))
    k_spec = pl.BlockSpec((1, KV_A, l) if kv_t else (1, l, KV_A), lambda bi, i: (bi, 0, 0))
    w_spec = pl.BlockSpec((1, lw, KV_A), lambda bi, i: (bi, 0, 0))
    return pl.pallas_call(
        functools.partial(_nsa_kernel, tq=tq, kb=kb, qpos0=qpos0, wbase=wbase, ns=ns, n_sel=n_sel, kv_t=kv_t),
        grid=(b, t // tq),
        in_specs=[q_spec, q_spec, pl.BlockSpec((1, tq, 3 * N_HEADS_A), lambda bi, i: (bi, i, 0)),
                  kc_spec, vc_spec, k_spec, k_spec, w_spec, w_spec],
        out_specs=q_spec,
        out_shape=jax.ShapeDtypeStruct((b, t, hd), MXU_DTYPE),
        scratch_shapes=[pltpu.VMEM((l // kb, tq, kb), jnp.float32)],
        compiler_params=pltpu.CompilerParams(
            dimension_semantics=("parallel", "arbitrary"), vmem_limit_bytes=VMEM_LIMIT),
        name="nsa_attention",
    )(qc, qr, gates, kvc, kvc, ks, vs, kw, vw)


def _dsa_kernel(qb_ref, qi_ref, wi_ref, kb_ref, vb_ref, ki_ref, o_ref, score_ref,
                *, tq, tq_real, kb, qpos0, n_keep, kv_t):
    i = pl.program_id(1)
    nh, d = N_HEADS_B, HEAD_DIM_B
    t0 = qpos0 + i * tq
    nj = (t0 + tq - 1) // kb + 1
    tpos = t0 + _iota((tq, 1), 0)
    w = wi_ref[0]

    def idx_body(j, carry):
        lo, hi = carry
        off = pl.multiple_of(j * kb, kb)
        kidx = ki_ref[0] if kv_t else ki_ref[0, pl.ds(off, kb), :]
        acc = jnp.zeros((tq, kb), jnp.float32)
        heads = [qi_ref[0, :, h * IDX_DIM:(h + 1) * IDX_DIM] for h in range(IDX_HEADS)]
        if kv_t:
            dots_all = jnp.dot(jnp.concatenate(heads, axis=0), kidx, preferred_element_type=jnp.float32)
        for h in range(IDX_HEADS):
            dots = dots_all[h * tq:(h + 1) * tq] if kv_t else _dot_nt(heads[h], kidx)
            acc = acc + w[:, h:h + 1] * jnp.maximum(dots, 0.0)
        vis = (off + _iota((1, kb), 1)) <= tpos
        score_ref[j] = jnp.where(vis, acc, NEG_INF)
        lo = jnp.minimum(lo, jnp.min(jnp.where(vis, acc, POS_INF), axis=1, keepdims=True))
        hi = jnp.maximum(hi, jnp.max(jnp.where(vis, acc, NEG_INF), axis=1, keepdims=True))
        return lo, hi

    lo, hi = lax.fori_loop(0, nj, idx_body, (jnp.full((tq, 1), POS_INF, jnp.float32),
                                             jnp.full((tq, 1), NEG_INF, jnp.float32)))

    k = float(n_keep)
    n_vis = (tpos + 1).astype(jnp.float32)
    n_keys = score_ref.shape[0] * kb

    def reduce_scores(pred, pick, init, combine, lane_reduce):
        def body(j, acc):
            sc = score_ref[j]
            for c in range(kb // LANE):
                kpos = (j * kb + c * LANE + _iota((1, LANE), 1)).astype(jnp.float32)
                piece = sc[:, c * LANE:(c + 1) * LANE]
                acc = combine(acc, pick(pred(piece, kpos), piece))
            return acc

        return lane_reduce(lax.fori_loop(0, nj, body, jnp.full((tq, LANE), init, jnp.float32)),
                           axis=1, keepdims=True)

    def count_where(pred):
        return reduce_scores(pred, lambda m, _: jnp.where(m, 1.0, 0.0), 0.0, jnp.add, jnp.sum)

    def min_where(pred):
        return reduce_scores(pred, lambda m, x: jnp.where(m, x, POS_INF), POS_INF, jnp.minimum, jnp.min)

    real = _iota((tq, 1), 0) < tq_real

    def any_row(flag):
        return jnp.max(jnp.where(real & flag, 1.0, 0.0)) > 0.0

    def bisect(state):
        it, lo, hi, cnt_lo = state
        mid = 0.5 * (lo + hi)
        mid_b = jnp.broadcast_to(mid, (tq, LANE))
        cnt = count_where(lambda x, _: x >= mid_b)
        ge = cnt >= k
        return it + 1, jnp.where(ge, mid, lo), jnp.where(ge, hi, mid), jnp.where(ge, cnt, cnt_lo)

    _, thr, _, cnt_lo = lax.while_loop(
        lambda st: (st[0] < BISECT_ITERS) & any_row((st[3] != k) & (n_vis > k)), bisect,
        (jnp.int32(0), lo, hi, n_vis))

    def break_ties():
        def above(v):
            v_b = jnp.broadcast_to(v, (tq, LANE))
            return count_where(lambda x, _: x > v_b)

        def strip(state):
            it, v, c_gt = state
            v_b = jnp.broadcast_to(v, (tq, LANE))
            v_next = jnp.where(c_gt >= k, min_where(lambda x, _: x > v_b), v)
            return it + 1, v_next, above(v_next)

        thr_b = jnp.broadcast_to(thr, (tq, LANE))
        v0 = min_where(lambda x, _: x >= thr_b)
        _, v, c_gt = lax.while_loop(lambda st: (st[0] < TIE_STRIP_ITERS) & any_row(st[2] >= k), strip,
                                    (jnp.int32(0), v0, above(v0)))
        need = k - c_gt
        v_b = jnp.broadcast_to(v, (tq, LANE))

        def narrow(_, bounds):
            j_lo, j_hi = bounds
            mid = jnp.floor(0.5 * (j_lo + j_hi))
            ge = count_where(lambda x, kpos: (x == v_b) & (kpos <= mid)) >= need
            return jnp.where(ge, j_lo, mid), jnp.where(ge, mid, j_hi)

        _, j_max = lax.fori_loop(0, n_keys.bit_length(), narrow,
                                 (jnp.full((tq, 1), -1.0, jnp.float32), jnp.full((tq, 1), n_keys - 1.0, jnp.float32)))
        return v, j_max

    thr, j_max = lax.cond(any_row((cnt_lo > k) & (n_vis > k)), break_ties,
                          lambda: (thr, jnp.full((tq, 1), float(n_keys), jnp.float32)))

    q = jnp.concatenate([qb_ref[0, :, h * d:(h + 1) * d] for h in range(nh)], axis=0)

    def att_body(j, carry):
        off = pl.multiple_of(j * kb, kb)
        sc = score_ref[j]
        kpos = (off + _iota((1, kb), 1)).astype(jnp.float32)
        madd = jnp.where((sc > thr) | ((sc == thr) & (kpos <= j_max)), 0.0, NEG_INF)
        if kv_t:
            return _flash_step(carry, q, kb_ref[0], vb_ref[0], madd, nh, scale=d ** -0.5, kv_t=True)
        return _flash_step(carry, q, kb_ref[0, pl.ds(off, kb), :], vb_ref[0, pl.ds(off, kb), :], madd, nh,
                           scale=d ** -0.5)

    o = _flash_finish(lax.fori_loop(0, nj, att_body, _flash_init(nh * tq, d)))
    o_ref[0] = jnp.concatenate([o[h * tq:(h + 1) * tq] for h in range(nh)], axis=1).astype(o_ref.dtype)


def dsa_attention(qb, qi, wi, kb_, vb, ki, *, tq, qpos0, n_keep, kv_t=False, tq_real=None):
    b, t, hd = qb.shape
    l = kb_.shape[2] if kv_t else kb_.shape[1]
    kb = min(KEY_CHUNK, l) if tq >= TQ else l
    assert l % kb == 0 and t % tq == 0 and (qpos0 + t - 1) // kb + 1 <= l // kb and (kb == l or not kv_t)
    kv_spec = lambda width: pl.BlockSpec((1, width, l) if kv_t else (1, l, width), lambda bi, i: (bi, 0, 0))
    return pl.pallas_call(
        functools.partial(_dsa_kernel, tq=tq, tq_real=tq if tq_real is None else tq_real, kb=kb, qpos0=qpos0,
                          n_keep=n_keep, kv_t=kv_t),
        grid=(b, t // tq),
        in_specs=[pl.BlockSpec((1, tq, hd), lambda bi, i: (bi, i, 0)),
                  pl.BlockSpec((1, tq, IDX_HEADS * IDX_DIM), lambda bi, i: (bi, i, 0)),
                  pl.BlockSpec((1, tq, IDX_HEADS), lambda bi, i: (bi, i, 0)),
                  kv_spec(HEAD_DIM_B), kv_spec(HEAD_DIM_B), kv_spec(IDX_DIM)],
        out_specs=pl.BlockSpec((1, tq, hd), lambda bi, i: (bi, i, 0)),
        out_shape=jax.ShapeDtypeStruct((b, t, hd), MXU_DTYPE),
        scratch_shapes=[pltpu.VMEM((l // kb, tq, kb), jnp.float32)],
        compiler_params=pltpu.CompilerParams(
            dimension_semantics=("parallel", "arbitrary"), vmem_limit_bytes=VMEM_LIMIT),
        name="dsa_attention",
    )(qb, qi, wi, kb_, vb, ki)


def _page_maps(n_pages, pp):
    n_steps = n_pages // pp

    def page_map(r):
        return lambda b, s, pt: (pt[b * n_pages + jnp.minimum(s, n_steps - 1) * pp + r], 0, 0)

    return n_steps, page_map


def _chunk_rows(tok_ref, sec, row0, n):
    d = HEAD_DIM_A
    first_half = _iota((n, KV_A), 1) < d
    pieces = [[] for _ in range(KV_GROUPS_A)]
    for j in range(0, CMP_STRIDE, 2):
        a, b = [tok_ref[sec, pl.ds(row0 + jj, n, stride=CMP_STRIDE), :] for jj in (j, j + 1)]
        pieces[0].append(jnp.where(first_half, a, pltpu.roll(b, d, 1)))
        pieces[1].append(jnp.where(first_half, pltpu.roll(a, d, 1), b))
    return [jnp.concatenate(p, axis=1) for p in pieces]


def _nsa_gather_kernel(pt_ref, *refs, pp, n_steps, rows):
    del pt_ref
    pages, (tail_tok_ref, tail_t_ref, pe_ref) = refs[:pp], refs[pp:pp + 3]
    zt_ref, zb_ref, ks_ref, vs_ref, tok_ref = refs[pp + 3:]
    is_tail = pl.program_id(1) == n_steps
    z = [[[] for _ in range(KV_GROUPS_A)] for _ in range(2)]
    for r in range(pp):
        cs = slice(r * rows, (r + 1) * rows)
        ks_ref[0, :, cs] = jnp.where(is_tail, tail_t_ref[0, :KV_A, cs],
                                     pages[r][0, 2 * KV_A:3 * KV_A, :]).astype(ks_ref.dtype)
        vs_ref[0, :, cs] = jnp.where(is_tail, tail_t_ref[0, KV_A:, cs],
                                     pages[r][0, 3 * KV_A:, :]).astype(vs_ref.dtype)
        for sec in range(2):
            tok_ref[sec, cs, :] = jnp.where(is_tail, tail_tok_ref[0, sec, cs, :],
                                            pages[r][0, sec * KV_A:(sec + 1) * KV_A, :].T)
            for g, zg in enumerate(_chunk_rows(tok_ref, sec, r * rows, rows // CMP_STRIDE)):
                z[sec][g].append(zg)
    for sec in range(2):
        for g in range(KV_GROUPS_A):
            zf = jnp.concatenate(z[sec][g], axis=0)
            zt_ref[sec, 0, g] = (zf + pe_ref[sec, 0]).astype(zt_ref.dtype)
            zb_ref[sec, 0, g] = (zf + pe_ref[sec, 1]).astype(zb_ref.dtype)


def nsa_gather(cache_t, page_table, new_rows, cmp_pe):
    db, n_pages = page_table.shape
    width, rows = cache_t.shape[1:]
    pp = PAGES_PER_STEP
    n_steps, page_map = _page_maps(n_pages, pp)
    l = (n_steps + 1) * pp * rows
    cps = pp * rows // CMP_STRIDE
    flat = CMP_STRIDE * HEAD_DIM_A
    pe = cmp_pe.reshape(2, 2, 1, flat).astype(jnp.float32)
    tail = pad_rows(new_rows, pp * rows)
    tail_tok = tail[:, :, :2 * KV_A].reshape(db, pp * rows, 2, KV_A).swapaxes(1, 2)
    tail_t = tail[:, :, 2 * KV_A:].swapaxes(1, 2)
    z_spec = pl.BlockSpec((2, 1, KV_GROUPS_A, cps, flat), lambda b, s, pt: (0, b, 0, s, 0))
    r_spec = pl.BlockSpec((1, KV_A, pp * rows), lambda b, s, pt: (b, 0, s))
    z_shape = jax.ShapeDtypeStruct((2, db, KV_GROUPS_A, l // CMP_STRIDE, flat), MXU_DTYPE)
    r_shape = jax.ShapeDtypeStruct((db, KV_A, l), MXU_DTYPE)
    return pl.pallas_call(
        functools.partial(_nsa_gather_kernel, pp=pp, n_steps=n_steps, rows=rows),
        grid_spec=pltpu.PrefetchScalarGridSpec(
            num_scalar_prefetch=1,
            grid=(db, n_steps + 1),
            in_specs=[pl.BlockSpec((1, width, rows), page_map(r)) for r in range(pp)]
            + [pl.BlockSpec((1, 2, pp * rows, KV_A), lambda b, s, pt: (b, 0, 0, 0)),
               pl.BlockSpec((1, 2 * KV_A, pp * rows), lambda b, s, pt: (b, 0, 0)),
               pl.BlockSpec((2, 2, 1, flat), lambda b, s, pt: (0, 0, 0, 0))],
            out_specs=[z_spec, z_spec, r_spec, r_spec],
            scratch_shapes=[pltpu.VMEM((2, pp * rows, KV_A), jnp.float32)]),
        out_shape=[z_shape, z_shape, r_shape, r_shape],
        compiler_params=pltpu.CompilerParams(
            dimension_semantics=("parallel", "arbitrary"), vmem_limit_bytes=VMEM_LIMIT),
        name="nsa_gather",
    )(page_table.reshape(-1), *([cache_t] * pp), tail_tok, tail_t, pe)


def _dsa_gather_kernel(pt_ref, *refs, pp, n_steps, rows):
    del pt_ref
    pages, tail_ref, (k_ref, v_ref, i_ref) = refs[:pp], refs[pp], refs[pp + 1:]
    is_tail = pl.program_id(1) == n_steps
    d = HEAD_DIM_B
    for r in range(pp):
        cs = slice(r * rows, (r + 1) * rows)
        x = jnp.where(is_tail, tail_ref[0, :, cs], pages[r][0])
        k_ref[0, :, cs] = x[:d].astype(k_ref.dtype)
        v_ref[0, :, cs] = x[d:2 * d].astype(v_ref.dtype)
        i_ref[0, :, cs] = x[2 * d:].astype(i_ref.dtype)


def dsa_gather(cache_t, page_table, new_rows):
    db, n_pages = page_table.shape
    width, rows = cache_t.shape[1:]
    pp = PAGES_PER_STEP
    n_steps, page_map = _page_maps(n_pages, pp)
    l = (n_steps + 1) * pp * rows
    tail_t = pad_rows(new_rows, pp * rows).swapaxes(1, 2)
    out_spec = lambda w: pl.BlockSpec((1, w, pp * rows), lambda b, s, pt: (b, 0, s))
    widths = (HEAD_DIM_B, HEAD_DIM_B, IDX_DIM)
    return pl.pallas_call(
        functools.partial(_dsa_gather_kernel, pp=pp, n_steps=n_steps, rows=rows),
        grid_spec=pltpu.PrefetchScalarGridSpec(
            num_scalar_prefetch=1,
            grid=(db, n_steps + 1),
            in_specs=[pl.BlockSpec((1, width, rows), page_map(r)) for r in range(pp)]
            + [pl.BlockSpec((1, width, pp * rows), lambda b, s, pt: (b, 0, 0))],
            out_specs=[out_spec(w) for w in widths]),
        out_shape=[jax.ShapeDtypeStruct((db, w, l), MXU_DTYPE) for w in widths],
        compiler_params=pltpu.CompilerParams(
            dimension_semantics=("parallel", "arbitrary"), vmem_limit_bytes=VMEM_LIMIT),
        name="dsa_gather",
    )(page_table.reshape(-1), *([cache_t] * pp), tail_t)


def _compress_kernel(zt_ref, zb_ref, pe_ref, w1t_ref, w1b_ref, b1_ref, w2_ref, o_ref, ab_ref, *, ncp):
    dot = functools.partial(jnp.dot, preferred_element_type=jnp.float32)
    ch = zb_ref.shape[2]
    at = dot(zt_ref[0, 0], w1t_ref[0])
    ab_ref[:ch] = dot(zb_ref[0, 0], w1b_ref[0])
    pe_rows = jnp.broadcast_to(pe_ref[0, 1], (SUBLANE, pe_ref.shape[3])).astype(zb_ref.dtype)
    ab_ref[ch:] = dot(pe_rows, w1b_ref[0])
    h = jax.nn.gelu(at[:ncp] + ab_ref[pl.ds(1, ncp), :] + b1_ref[0])
    o_ref[0, 0, :ncp] = dot(h.astype(w2_ref.dtype), w2_ref[0]).astype(o_ref.dtype)
    if o_ref.shape[2] > ncp:
        o_ref[0, 0, ncp:] = jnp.zeros((o_ref.shape[2] - ncp, o_ref.shape[3]), o_ref.dtype)


def compress(zt, zb, cmp_pe, w1, b1, w2, n_keys):
    ncp = _round_up(-(-n_keys // CMP_STRIDE), BF16_ROWS)
    ncl = ncp if ncp <= LANE else _round_up(ncp, LANE)
    _, nb, ch, kdim = zt.shape
    hid = w1.shape[-1]
    d = w2.shape[-1]
    assert ch + SUBLANE >= ncp + 1 and ch % SUBLANE == 0
    w1 = w1.astype(MXU_DTYPE)
    pe = cmp_pe.reshape(2, 2, 1, kdim).astype(jnp.float32)
    z_spec = pl.BlockSpec((1, 1, ch, kdim), lambda s, n: (s, n, 0, 0))
    return pl.pallas_call(
        functools.partial(_compress_kernel, ncp=ncp),
        grid=(2, nb),
        in_specs=[z_spec, z_spec,
                  pl.BlockSpec((1, 2, 1, kdim), lambda s, n: (s, 0, 0, 0)),
                  pl.BlockSpec((1, kdim, hid), lambda s, n: (s, 0, 0)),
                  pl.BlockSpec((1, kdim, hid), lambda s, n: (s, 1, 0)),
                  pl.BlockSpec((1, 1, hid), lambda s, n: (s, 0, 0)),
                  pl.BlockSpec((1, hid, d), lambda s, n: (s, 0, 0))],
        out_specs=pl.BlockSpec((1, 1, ncl, d), lambda s, n: (s, n, 0, 0)),
        out_shape=jax.ShapeDtypeStruct((2, nb, ncl, d), MXU_DTYPE),
        scratch_shapes=[pltpu.VMEM((ch + SUBLANE, hid), jnp.float32)],
        compiler_params=pltpu.CompilerParams(
            dimension_semantics=("parallel", "parallel"), vmem_limit_bytes=VMEM_LIMIT),
        name="compress",
    )(zt, zb, pe, w1, w1, b1.reshape(2, 1, hid).astype(jnp.float32), w2.astype(MXU_DTYPE))


def pad_rows(a, n):
    return jnp.pad(a, [(0, 0), (0, n - a.shape[1])] + [(0, 0)] * (a.ndim - 2))


def _rope_tables(pos, d):
    half = d // ROT_FRACTION // 2
    inv = ROPE_THETA ** (-jnp.arange(half, dtype=jnp.float32) / half)
    ang = pos.astype(jnp.float32)[:, None] * inv[None, :]
    lane = jnp.arange(LANE) % d
    cos = jnp.cos(ang)[:, lane % half]
    sin = jnp.sin(ang)[:, lane % half]
    one, zero = jnp.ones_like(cos), jnp.zeros_like(cos)
    c = jnp.where(lane < 2 * half, cos, one)
    sa = jnp.where((lane >= half) & (lane < 2 * half), sin, zero)
    sb = jnp.where(lane < half, -sin, zero)
    return jnp.stack([c, sa, sb])


def _rope(x, t_ref, half):
    c, sa, sb = t_ref[0], t_ref[1], t_ref[2]
    outs = []
    for j in range(x.shape[1] // LANE):
        xs = x[:, j * LANE:(j + 1) * LANE]
        outs.append(xs * c + pltpu.roll(xs, half, 1) * sa + pltpu.roll(xs, LANE - half, 1) * sb)
    return outs[0] if len(outs) == 1 else jnp.concatenate(outs, axis=1)


_QA = N_HEADS_A * HEAD_DIM_A
_KVA = 6 * KV_A
_QB = N_HEADS_B * HEAD_DIM_B
_KVB = 2 * HEAD_DIM_B
_QI = IDX_HEADS * IDX_DIM
_MISC = IDX_DIM + IDX_HEADS + 3 * N_HEADS_A
PREP_WIDTH = _QA + _KVA + _QB + _KVB + _QI + _MISC
assert _MISC == LANE


def _prep_kernel(x_ref, t64_ref, t128_ref, pe_ref, qc_ref, qr_ref, qb_ref, qi_ref, nsa_ref, win_ref, dsa_ref,
                 ks_ref, vs_ref, kw_ref, vw_ref, kb_ref, vb_ref, ki_ref, wi_ref, gate_ref, *z_refs, tm):
    x = x_ref[...]
    o = 0
    qa = x[:, o:o + _QA]; o += _QA
    kva = x[:, o:o + _KVA]; o += _KVA
    qb = x[:, o:o + _QB]; o += _QB
    kvb = x[:, o:o + _KVB]; o += _KVB
    qi = x[:, o:o + _QI]; o += _QI
    misc = x[:, o:o + _MISC]
    h64, h128 = HEAD_DIM_A // ROT_FRACTION // 2, HEAD_DIM_B // ROT_FRACTION // 2
    sec = lambda i: kva[:, i * KV_A:(i + 1) * KV_A]
    mx = lambda a: a.astype(qc_ref.dtype)
    qc_ref[...] = mx(qa * HEAD_DIM_A ** -0.5)
    qr_ref[...] = mx(_rope(qa, t64_ref, h64) * HEAD_DIM_A ** -0.5)
    k_slc, k_win = _rope(sec(2), t64_ref, h64), _rope(sec(4), t64_ref, h64)
    nsa_ref[:, :2 * KV_A] = kva[:, :2 * KV_A]
    nsa_ref[:, 2 * KV_A:3 * KV_A] = k_slc
    nsa_ref[:, 3 * KV_A:] = sec(3)
    win_ref[:, :KV_A] = k_win
    win_ref[:, KV_A:] = sec(5)
    ks_ref[...], vs_ref[...], kw_ref[...], vw_ref[...] = mx(k_slc), mx(sec(3)), mx(k_win), mx(sec(5))
    qb_ref[...] = mx(_rope(qb, t128_ref, h128))
    k_b, v_b = _rope(kvb[:, :HEAD_DIM_B], t128_ref, h128), kvb[:, HEAD_DIM_B:]
    k_idx = _rope(misc, t64_ref, h64)[:, :IDX_DIM]
    dsa_ref[:, :HEAD_DIM_B] = k_b
    dsa_ref[:, HEAD_DIM_B:2 * HEAD_DIM_B] = v_b
    dsa_ref[:, 2 * HEAD_DIM_B:] = k_idx
    kb_ref[...], vb_ref[...], ki_ref[...] = mx(k_b), mx(v_b), mx(k_idx)
    qi_ref[...] = mx(_rope(qi, t64_ref, h64) * IDX_DIM ** -0.5)
    wi_ref[...] = misc[:, IDX_DIM:IDX_DIM + IDX_HEADS] * IDX_HEADS ** -0.5
    gate_ref[...] = jax.nn.sigmoid(misc[:, IDX_DIM + IDX_HEADS:])
    if z_refs:
        zt_ref, zb_ref, tok_ref = z_refs
        for s_ in range(2):
            tok_ref[s_] = sec(s_)
            for g, zg in enumerate(_chunk_rows(tok_ref, s_, 0, tm // CMP_STRIDE)):
                zt_ref[s_, 0, g] = (zg + pe_ref[s_, 0]).astype(zt_ref.dtype)
                zb_ref[s_, 0, g] = (zg + pe_ref[s_, 1]).astype(zb_ref.dtype)


def prep(proj, pos, cmp_pe, with_chunks):
    b, t, width = proj.shape
    assert width == PREP_WIDTH
    m = b * t
    tm = 256 if t % 256 == 0 else m
    assert m % tm == 0 and t % tm in (0, t)
    nt = max(t // tm, 1)
    flat = CMP_STRIDE * HEAD_DIM_A
    names = ['qc', 'qr', 'qb', 'qi', 'nsa_rows', 'win_rows', 'dsa_rows', 'ks', 'vs', 'kw', 'vw', 'kb', 'vb', 'ki',
             'wi', 'gates']
    widths = [_QA, _QA, _QB, _QI, NSA_SECTIONS * KV_A, 2 * KV_A, DSA_ROW, KV_A, KV_A, KV_A, KV_A, HEAD_DIM_B,
              HEAD_DIM_B, IDX_DIM, IDX_HEADS, 3 * N_HEADS_A]
    dtypes = [MXU_DTYPE] * 4 + [jnp.float32] * 3 + [MXU_DTYPE] * 7 + [jnp.float32] * 2
    row = lambda w: pl.BlockSpec((tm, w), lambda i: (i, 0))
    out_specs = [row(w) for w in widths]
    out_shape = [jax.ShapeDtypeStruct((m, w), dt) for w, dt in zip(widths, dtypes)]
    scratch = []
    if with_chunks:
        assert tm % CMP_STRIDE == 0 and t % tm == 0
        z_spec = pl.BlockSpec((2, 1, KV_GROUPS_A, tm // CMP_STRIDE, flat), lambda i: (0, i // nt, 0, i % nt, 0))
        z_shape = jax.ShapeDtypeStruct((2, b, KV_GROUPS_A, t // CMP_STRIDE, flat), MXU_DTYPE)
        out_specs += [z_spec, z_spec]
        out_shape += [z_shape, z_shape]
        names += ['zt', 'zb']
        scratch = [pltpu.VMEM((2, tm, KV_A), jnp.float32)]
    t_spec = pl.BlockSpec((3, tm, LANE), lambda i: (0, i, 0))
    outs = pl.pallas_call(
        functools.partial(_prep_kernel, tm=tm),
        grid=(m // tm,),
        in_specs=[row(width), t_spec, t_spec, pl.BlockSpec((2, 2, 1, flat), lambda i: (0, 0, 0, 0))],
        out_specs=out_specs,
        out_shape=out_shape,
        scratch_shapes=scratch,
        compiler_params=pltpu.CompilerParams(dimension_semantics=("parallel",), vmem_limit_bytes=VMEM_LIMIT),
        name="prep",
    )(proj.reshape(m, width), _rope_tables(pos, HEAD_DIM_A), _rope_tables(pos, HEAD_DIM_B),
      cmp_pe.reshape(2, 2, 1, flat).astype(jnp.float32))
    return {n: (o if o.ndim > 2 else o.reshape(b, t, o.shape[-1])) for n, o in zip(names, outs)}


def project(x, g, w_in, sizes):
    starts = [sum(sizes[:i]) for i in range(len(sizes))]
    qa, kva, ga, qb, kvb, qi, ki, wi, gm = [slice(o, o + n) for o, n in zip(starts, sizes)]
    w_bf = w_in.astype(MXU_DTYPE)
    w_prep = jnp.concatenate([w_bf[:, c] for c in (qa, kva, qb, kvb, qi, ki, wi, ga)], axis=1)
    return norm_matmul(x, g, w_prep), norm_matmul(x, g, w_bf[:, gm])


def mixer_prompt(x, g, w_in, cmp_pe, cmp_w1, cmp_b1, cmp_w2, sizes):
    b, s, _ = x.shape
    proj, gm = project(x, g, w_in, sizes)
    p = prep(proj, jnp.tile(jnp.arange(s, dtype=jnp.int32), b), cmp_pe, True)
    merge_bg = lambda a: a.reshape((2, b * KV_GROUPS_A) + a.shape[3:])
    kvc = compress(merge_bg(p['zt']), merge_bg(p['zb']), cmp_pe, cmp_w1, cmp_b1, cmp_w2, s)
    kvc = kvc.reshape(2, b, KV_GROUPS_A, kvc.shape[2], HEAD_DIM_A)
    o_nsa = nsa_attention(p['qc'], p['qr'], p['gates'], kvc, p['ks'], p['vs'], p['kw'], p['vw'],
                          tq=min(TQ, s), qpos0=0, wbase=0, n_keys=s)
    o_dsa = dsa_attention(p['qb'], p['qi'], p['wi'], p['kb'], p['vb'], p['ki'],
                          tq=min(TQ, s), qpos0=0, n_keep=min(DSA_TOPK, s // 4))
    nsa_rows = p['nsa_rows'].reshape(b, s, NSA_SECTIONS, KV_GROUPS_A, HEAD_DIM_A)
    win_state = p['win_rows'].reshape(b, s, 2, KV_GROUPS_A, HEAD_DIM_A)[:, -min(WINDOW, s):]
    return (o_nsa, o_dsa, gm), nsa_rows, win_state, p['dsa_rows']


def mixer_sample(x, g, cache_nsa, win_buf, cache_dsa, page_table, w_in, cmp_pe, cmp_w1, cmp_b1, cmp_w2, sizes):
    b, t, _ = x.shape
    page = cache_nsa.shape[1]
    past_len = page_table.shape[1] * page
    n_keys = past_len + t
    proj, gm = project(x, g, w_in, sizes)
    p = prep(proj, jnp.tile(past_len + jnp.arange(t, dtype=jnp.int32), b), cmp_pe, False)
    pad = lambda a: pad_rows(a, TQ_STEP)

    cache_nsa_t = jnp.transpose(cache_nsa, (0, 2, 3, 4, 1)).reshape(cache_nsa.shape[0], NSA_SECTIONS * KV_A, page)
    zt, zb, ks, vs = nsa_gather(cache_nsa_t, page_table, p['nsa_rows'], cmp_pe)
    merge_bg = lambda a: a.reshape((2, b * KV_GROUPS_A) + a.shape[3:])
    kvc = compress(merge_bg(zt), merge_bg(zb), cmp_pe, cmp_w1, cmp_b1, cmp_w2, n_keys)
    kvc = kvc.reshape(2, b, KV_GROUPS_A, kvc.shape[2], HEAD_DIM_A)
    w_len = win_buf.shape[1]
    win_new = p['win_rows'].reshape(b, t, 2, KV_GROUPS_A, HEAD_DIM_A)
    win_all = jnp.concatenate([win_buf, win_new], axis=1)
    win_pad = pad_rows(win_all, w_len + WIN_CHUNK).astype(MXU_DTYPE)
    kw = win_pad[:, :, 0].reshape(b, w_len + WIN_CHUNK, KV_A)
    vw = win_pad[:, :, 1].reshape(b, w_len + WIN_CHUNK, KV_A)
    o_nsa = nsa_attention(pad(p['qc']), pad(p['qr']), pad(p['gates']), kvc, ks, vs, kw, vw,
                          tq=TQ_STEP, qpos0=past_len, wbase=past_len - w_len, n_keys=n_keys, kv_t=True)

    kb_, vb, ki = dsa_gather(jnp.swapaxes(cache_dsa, 1, 2), page_table, p['dsa_rows'])
    o_dsa = dsa_attention(pad(p['qb']), pad(p['qi']), pad(p['wi']), kb_, vb, ki, tq=TQ_STEP, qpos0=past_len,
                          n_keep=min(DSA_TOPK, n_keys // 4), kv_t=True, tq_real=t)
    nsa_rows = p['nsa_rows'].reshape(b, t, NSA_SECTIONS, KV_GROUPS_A, HEAD_DIM_A)
    return (o_nsa, o_dsa, pad(gm)), nsa_rows, win_all[:, -w_len:], p['dsa_rows']


def kernel(x_prompt, x_sample, mem_prompt, cache_nsa_kv, state_nsa_win, cache_dsa_kv, cache_mem_kv, state_conv,
           page_table, norm_g, w_in, cmp_pe, cmp_w1, cmp_b1, cmp_w2, w_out_a, w_out_b, w_out, w_mem_q, w_mem_kv,
           w_mem_out, w_up, conv_w, conv_b, w_down, final_g):
    depth = w_in.shape[0]
    d_model = x_prompt.shape[-1]
    d_ff = w_down.shape[1]
    assert CONV_WIDTH == 3
    sizes = (N_HEADS_A * HEAD_DIM_A, 6 * KV_A, 3 * N_HEADS_A, N_HEADS_B * HEAD_DIM_B, 2 * HEAD_DIM_B,
             IDX_HEADS * IDX_DIM, IDX_DIM, IDX_HEADS, 2 * d_model)
    xp, xs = x_prompt, pad_rows(x_sample, TQ_STEP)
    t_step = x_sample.shape[1]
    nsa_p, nsa_s, win_p, win_s, dsa_p, dsa_s, mem_p, conv_p, conv_s = [], [], [], [], [], [], [], [], []
    for l in range(depth):
        assert l == depth - 1, "the fused FFN epilogue applies the final norm"
        branches_p, a, bwin, c = mixer_prompt(xp, norm_g[l, 0], w_in[l], cmp_pe[l], cmp_w1[l], cmp_b1[l], cmp_w2[l],
                                              sizes)
        nsa_p.append(a); win_p.append(bwin); dsa_p.append(c)
        branches_s, a, bwin, c = mixer_sample(xs[:, :t_step], norm_g[l, 0], cache_nsa_kv[l], state_nsa_win[l],
                                              cache_dsa_kv[l], page_table, w_in[l], cmp_pe[l], cmp_w1[l], cmp_b1[l],
                                              cmp_w2[l], sizes)
        nsa_s.append(a); win_s.append(bwin); dsa_s.append(c)
        kv_p = norm_matmul(mem_prompt, norm_g[l, 2], w_mem_kv[l])
        mem_p.append(kv_p.reshape(kv_p.shape[:2] + (2, MEM_HEADS, MEM_HEAD_DIM)))
        kv_s = cache_mem_kv[l].reshape(cache_mem_kv.shape[1:3] + (-1,))
        weights = (norm_g[l, 1], norm_g[l, 3], final_g, w_out_a[l], w_out_b[l], w_out[l], w_mem_q[l], w_mem_out[l],
                   w_up[l], conv_w[l], conv_b[l], w_down[l])
        xp, cp = dense_tail(xp, *branches_p, kv_p, jnp.zeros((xp.shape[0], CONV_WIDTH - 1, 2 * d_ff), xp.dtype),
                            xp.shape[1], *weights)
        xs, cs = dense_tail(xs, *branches_s, kv_s, state_conv[l], t_step, *weights)
        conv_p.append(cp); conv_s.append(cs)
    y_prompt, y_sample = xp, xs[:, :t_step]
    return (y_prompt, y_sample, jnp.stack(nsa_p), jnp.stack(nsa_s), jnp.stack(win_p), jnp.stack(win_s),
            jnp.stack(dsa_p), jnp.stack(dsa_s), jnp.stack(mem_p), jnp.stack(conv_p), jnp.stack(conv_s))
```

```python
import functools

import jax
import jax.numpy as jnp
from jax import lax
from jax.experimental import pallas as pl
from jax.experimental.pallas import tpu as pltpu

N_HEADS_A = 16
HEAD_DIM_A = 64
KV_GROUPS_A = 2
CMP_BLOCK = 32
CMP_STRIDE = 16
SLC_BLOCK = 64
N_SELECT = 16
WINDOW = 512
N_HEADS_B = 8
HEAD_DIM_B = 128
IDX_HEADS = 16
IDX_DIM = 64
DSA_TOPK = 256
MEM_HEADS = 4
MEM_HEAD_DIM = 128
CONV_WIDTH = 3
ROPE_THETA = 500000.0
ROT_FRACTION = 4
EPS = 1e-6
KV_A = KV_GROUPS_A * HEAD_DIM_A
CMP_PER_SLC = SLC_BLOCK // CMP_STRIDE
HPG_A = N_HEADS_A // KV_GROUPS_A
DSA_ROW = 2 * HEAD_DIM_B + IDX_DIM
NSA_SECTIONS = 4

LANE = 128
SUBLANE = 8
BF16_ROWS = 16
VMEM_LIMIT = 48 * 1024 * 1024

NEG_INF = float('-inf')
POS_INF = float('inf')
MXU_DTYPE = jnp.bfloat16
TQ = 128
TQ_STEP = BF16_ROWS
KEY_CHUNK = 512
WIN_CHUNK = 128
PAGES_PER_STEP = 8
BISECT_ITERS = 40
TIE_STRIP_ITERS = 64
SLC_SHIFT = SLC_BLOCK.bit_length() - 1
CMP_PER_SLC_SHIFT = CMP_PER_SLC.bit_length() - 1


def _round_up(n, m):
    return -(-n // m) * m


def _tile(n, cap):
    if n <= cap:
        return n
    best = None
    for t in range(LANE, cap + 1, LANE):
        if n % t == 0:
            best = t
    assert best is not None, (n, cap)
    return best


def _mm_kernel(x_ref, w_ref, o_ref, acc_ref):
    @pl.when(pl.program_id(2) == 0)
    def _():
        acc_ref[...] = jnp.zeros_like(acc_ref)

    acc_ref[...] += jnp.dot(x_ref[...], w_ref[...], preferred_element_type=jnp.float32)

    @pl.when(pl.program_id(2) == pl.num_programs(2) - 1)
    def _():
        o_ref[...] = acc_ref[...].astype(o_ref.dtype)


def matmul(x, w, out_dtype=jnp.float32):
    lead = x.shape[:-1]
    k = x.shape[-1]
    n = w.shape[-1]
    x2 = x.reshape(-1, k).astype(MXU_DTYPE)
    w2 = w.astype(MXU_DTYPE)
    m0 = x2.shape[0]
    tm = 512 if m0 >= 512 else _round_up(m0, BF16_ROWS)
    m = _round_up(m0, tm)
    if m != m0:
        x2 = jnp.pad(x2, ((0, m - m0), (0, 0)))
    tn = _tile(n, 1024)
    tk = _tile(k, 2048)
    out = pl.pallas_call(
        _mm_kernel,
        grid=(m // tm, n // tn, k // tk),
        in_specs=[pl.BlockSpec((tm, tk), lambda i, j, l: (i, l)),
                  pl.BlockSpec((tk, tn), lambda i, j, l: (l, j))],
        out_specs=pl.BlockSpec((tm, tn), lambda i, j, l: (i, j)),
        out_shape=jax.ShapeDtypeStruct((m, n), out_dtype),
        scratch_shapes=[pltpu.VMEM((tm, tn), jnp.float32)],
        compiler_params=pltpu.CompilerParams(
            dimension_semantics=("parallel", "parallel", "arbitrary"),
            vmem_limit_bytes=VMEM_LIMIT),
        name="matmul",
    )(x2, w2)
    return out[:m0].reshape(lead + (n,))


def _rms(x, g):
    return x * lax.rsqrt(jnp.mean(x * x, axis=-1, keepdims=True) + EPS) * g


def _norm_mm_kernel(x_ref, g_ref, w_ref, o_ref, xn_ref):
    @pl.when(pl.program_id(1) == 0)
    def _():
        xn_ref[...] = _rms(x_ref[...], g_ref[...]).astype(xn_ref.dtype)

    o_ref[...] = jnp.dot(xn_ref[...], w_ref[...], preferred_element_type=jnp.float32).astype(o_ref.dtype)


def norm_matmul(x, g, w):
    lead = x.shape[:-1]
    k = x.shape[-1]
    n = w.shape[-1]
    x2 = x.reshape(-1, k)
    m = x2.shape[0]
    tm = 512 if m % 512 == 0 else m
    tn = _tile(n, 1536)
    out = pl.pallas_call(
        _norm_mm_kernel,
        grid=(m // tm, n // tn),
        in_specs=[pl.BlockSpec((tm, k), lambda i, j: (i, 0)),
                  pl.BlockSpec((1, k), lambda i, j: (0, 0)),
                  pl.BlockSpec((k, tn), lambda i, j: (0, j))],
        out_specs=pl.BlockSpec((tm, tn), lambda i, j: (i, j)),
        out_shape=jax.ShapeDtypeStruct((m, n), jnp.float32),
        scratch_shapes=[pltpu.VMEM((tm, k), MXU_DTYPE)],
        compiler_params=pltpu.CompilerParams(
            dimension_semantics=("parallel", "arbitrary"), vmem_limit_bytes=VMEM_LIMIT),
        name="norm_matmul",
    )(x2, g.reshape(1, k).astype(jnp.float32), w.astype(MXU_DTYPE))
    return out.reshape(lead + (n,))


def _gated_merge_kernel(oa_ref, ob_ref, wa_ref, wb_ref, ga_ref, gb_ref, o_ref):
    dot = functools.partial(jnp.dot, preferred_element_type=jnp.float32)
    ya = dot(oa_ref[...], wa_ref[...])
    yb = dot(ob_ref[...], wb_ref[...])
    o_ref[...] = (jax.nn.sigmoid(ga_ref[...]) * ya + jax.nn.sigmoid(gb_ref[...]) * yb).astype(o_ref.dtype)


def gated_merge(o_nsa, o_dsa, gm, w_oa, w_ob):
    m, ka = o_nsa.shape
    kb = o_dsa.shape[1]
    n = w_oa.shape[1]
    tm = 512 if m % 512 == 0 else m
    tn = _tile(n, 1024)
    nb = n // tn
    return pl.pallas_call(
        _gated_merge_kernel,
        grid=(m // tm, nb),
        in_specs=[pl.BlockSpec((tm, ka), lambda i, j: (i, 0)),
                  pl.BlockSpec((tm, kb), lambda i, j: (i, 0)),
                  pl.BlockSpec((ka, tn), lambda i, j: (0, j)),
                  pl.BlockSpec((kb, tn), lambda i, j: (0, j)),
                  pl.BlockSpec((tm, tn), lambda i, j: (i, j)),
                  pl.BlockSpec((tm, tn), lambda i, j: (i, j + nb))],
        out_specs=pl.BlockSpec((tm, tn), lambda i, j: (i, j)),
        out_shape=jax.ShapeDtypeStruct((m, n), MXU_DTYPE),
        compiler_params=pltpu.CompilerParams(
            dimension_semantics=("parallel", "parallel"), vmem_limit_bytes=VMEM_LIMIT),
        name="gated_merge",
    )(o_nsa, o_dsa, w_oa.astype(MXU_DTYPE), w_ob.astype(MXU_DTYPE), gm, gm)


def _out_proj_kernel(z_ref, w_ref, x_ref, o_ref):
    o_ref[...] = x_ref[...] + jnp.dot(z_ref[...], w_ref[...], preferred_element_type=jnp.float32)


def out_proj_residual(z, w, x):
    m, k = z.shape
    n = w.shape[1]
    tm = 256 if m % 256 == 0 else m
    return pl.pallas_call(
        _out_proj_kernel,
        grid=(m // tm,),
        in_specs=[pl.BlockSpec((tm, k), lambda i: (i, 0)),
                  pl.BlockSpec((k, n), lambda i: (0, 0)),
                  pl.BlockSpec((tm, n), lambda i: (i, 0))],
        out_specs=pl.BlockSpec((tm, n), lambda i: (i, 0)),
        out_shape=jax.ShapeDtypeStruct((m, n), jnp.float32),
        compiler_params=pltpu.CompilerParams(dimension_semantics=("parallel",), vmem_limit_bytes=VMEM_LIMIT),
        name="out_proj_residual",
    )(z, w.astype(MXU_DTYPE), x)


def _mem_block_kernel(x_ref, g1_ref, g2_ref, wq_ref, kv_ref, wo_ref, x_out_ref, xn_out_ref):
    d = MEM_HEAD_DIM
    hd = MEM_HEADS * d
    x = x_ref[0]
    xn = _rms(x, g1_ref[...]).astype(wq_ref.dtype)
    q = jnp.dot(xn, wq_ref[...], preferred_element_type=jnp.float32).astype(wq_ref.dtype)
    outs = []
    for h in range(MEM_HEADS):
        k = kv_ref[0, :, h * d:(h + 1) * d].astype(wq_ref.dtype)
        v = kv_ref[0, :, hd + h * d:hd + (h + 1) * d].astype(wq_ref.dtype)
        s = _dot_nt(q[:, h * d:(h + 1) * d], k) * d ** -0.5
        e = jnp.exp(s - jnp.max(s, axis=1, keepdims=True))
        p = e / jnp.sum(e, axis=1, keepdims=True)
        outs.append(jnp.dot(p.astype(v.dtype), v, preferred_element_type=jnp.float32))
    o = jnp.concatenate(outs, axis=1).astype(wo_ref.dtype)
    x2 = x + jnp.dot(o, wo_ref[...], preferred_element_type=jnp.float32)
    x_out_ref[0] = x2
    xn_out_ref[0] = _rms(x2, g2_ref[...]).astype(xn_out_ref.dtype)


def mem_block(x, g1, g2, w_q, kv, w_o):
    b, t, dm = x.shape
    mt, kvw = kv.shape[1:]
    hd = w_q.shape[1]
    tm = 256 if t % 256 == 0 else t
    row_spec = pl.BlockSpec((1, tm, dm), lambda bi, i: (bi, i, 0))
    g_spec = pl.BlockSpec((1, dm), lambda bi, i: (0, 0))
    return pl.pallas_call(
        _mem_block_kernel,
        grid=(b, t // tm),
        in_specs=[row_spec, g_spec, g_spec,
                  pl.BlockSpec((dm, hd), lambda bi, i: (0, 0)),
                  pl.BlockSpec((1, mt, kvw), lambda bi, i: (bi, 0, 0)),
                  pl.BlockSpec((hd, dm), lambda bi, i: (0, 0))],
        out_specs=[row_spec, row_spec],
        out_shape=[jax.ShapeDtypeStruct((b, t, dm), jnp.float32), jax.ShapeDtypeStruct((b, t, dm), MXU_DTYPE)],
        compiler_params=pltpu.CompilerParams(
            dimension_semantics=("parallel", "parallel"), vmem_limit_bytes=VMEM_LIMIT),
        name="mem_block",
    )(x, g1.reshape(1, dm).astype(jnp.float32), g2.reshape(1, dm).astype(jnp.float32), w_q.astype(MXU_DTYPE),
      kv, w_o.astype(MXU_DTYPE))


def _ffn_up_kernel(x_ref, halo_ref, wg_ref, wu_ref, cwg_ref, cwu_ref, cbg_ref, cbu_ref, pg_ref, pu_ref,
                   h_ref, sg_ref, su_ref, *, nb, tm, t_real):
    i = pl.program_id(2)
    n_halo = halo_ref.shape[1]
    tn = wg_ref.shape[1]
    rows = nb * tm
    x = x_ref[...].reshape(rows, x_ref.shape[2])
    if nb == 1:
        x = jnp.concatenate([halo_ref[0], x], axis=0)
    row = _iota((nb, tm, 1), 1).reshape(rows, 1)
    last = (t_real - 1) // tm
    r_last = (t_real - 1) % tm
    spread = lambda p: jnp.broadcast_to(p, (nb, tm, tn)).reshape(rows, tn)

    def branch(w_ref, cw_ref, cb_ref, p_ref, s_ref):
        u = jnp.dot(x, w_ref[...], preferred_element_type=jnp.float32)
        p0, p1 = p_ref[:, 0:1, :], p_ref[:, 1:2, :]
        if nb == 1:
            uh, u = u[:n_halo], u[n_halo:]
            p0 = jnp.where(i == 0, p0, uh[n_halo - 2:n_halo - 1][None])
            p1 = jnp.where(i == 0, p1, uh[n_halo - 1:n_halo][None])
        p0, p1 = spread(p0), spread(p1)
        u1 = jnp.where(row == 0, p1, pltpu.roll(u, 1, 0))
        u2 = jnp.where(row == 0, p0, jnp.where(row == 1, p1, pltpu.roll(u, 2, 0)))

        @pl.when(i == last)
        def _():
            s_ref[...] = u.reshape(nb, tm, tn)[:, r_last - 1:r_last + 1, :]

        return cb_ref[...] + u2 * cw_ref[0:1, :] + u1 * cw_ref[1:2, :] + u * cw_ref[2:3, :]

    gate = branch(wg_ref, cwg_ref, cbg_ref, pg_ref, sg_ref)
    up = branch(wu_ref, cwu_ref, cbu_ref, pu_ref, su_ref)
    h_ref[...] = (jax.nn.silu(gate) * up).reshape(nb, tm, tn).astype(h_ref.dtype)


def ffn_up(xn, prev, w_up, conv_w, conv_b, t_real):
    b, t, dm = xn.shape
    f2 = w_up.shape[1]
    f = f2 // 2
    tm = 512 if t % 512 == 0 else t
    nb = 1 if t > tm else max(1, min(b, 512 // tm))
    while b % nb:
        nb -= 1
    tn = _tile(f, 512)
    nf = f // tn
    halo = min(BF16_ROWS, tm)
    hpt = tm // halo
    assert t_real >= 2 and (t_real - 1) % tm >= 1 and tm % SUBLANE == 0
    w_up = w_up.astype(MXU_DTYPE)
    conv_b = conv_b.reshape(1, f2)
    col = lambda off: (lambda j, bi, i: (0, j + off))
    st = lambda off: (lambda j, bi, i: (bi, 0, j + off))
    specs = [pl.BlockSpec((nb, tm, dm), lambda j, bi, i: (bi, i, 0)),
             pl.BlockSpec((1, halo, dm), lambda j, bi, i: (bi * nb, jnp.maximum(i * hpt - 1, 0), 0)),
             pl.BlockSpec((dm, tn), col(0)), pl.BlockSpec((dm, tn), col(nf)),
             pl.BlockSpec((CONV_WIDTH, tn), col(0)), pl.BlockSpec((CONV_WIDTH, tn), col(nf)),
             pl.BlockSpec((1, tn), col(0)), pl.BlockSpec((1, tn), col(nf)),
             pl.BlockSpec((nb, 2, tn), st(0)), pl.BlockSpec((nb, 2, tn), st(nf))]
    h, sg, su = pl.pallas_call(
        functools.partial(_ffn_up_kernel, nb=nb, tm=tm, t_real=t_real),
        grid=(nf, b // nb, t // tm),
        in_specs=specs,
        out_specs=[pl.BlockSpec((nb, tm, tn), lambda j, bi, i: (bi, i, j)),
                   pl.BlockSpec((nb, 2, tn), st(0)), pl.BlockSpec((nb, 2, tn), st(0))],
        out_shape=[jax.ShapeDtypeStruct((b, t, f), MXU_DTYPE), jax.ShapeDtypeStruct((b, 2, f), jnp.float32),
                   jax.ShapeDtypeStruct((b, 2, f), jnp.float32)],
        compiler_params=pltpu.CompilerParams(
            dimension_semantics=("parallel", "parallel", "arbitrary"), vmem_limit_bytes=VMEM_LIMIT),
        name="ffn_up",
    )(xn, xn, w_up, w_up, conv_w, conv_w, conv_b, conv_b, prev, prev)
    return h, jnp.concatenate([sg, su], axis=-1)


def _ffn_down_kernel(h_ref, w_ref, x_ref, g_ref, o_ref, acc_ref):
    @pl.when(pl.program_id(1) == 0)
    def _():
        acc_ref[...] = x_ref[...]

    acc_ref[...] += jnp.dot(h_ref[...], w_ref[...], preferred_element_type=jnp.float32)

    @pl.when(pl.program_id(1) == pl.num_programs(1) - 1)
    def _():
        o_ref[...] = _rms(acc_ref[...], g_ref[...])


def ffn_down_norm(h, w_down, x, g):
    m, f = h.shape
    dm = w_down.shape[1]
    tm = 512 if m % 512 == 0 else m
    tk = _tile(f, 2048)
    return pl.pallas_call(
        _ffn_down_kernel,
        grid=(m // tm, f // tk),
        in_specs=[pl.BlockSpec((tm, tk), lambda i, l: (i, l)),
                  pl.BlockSpec((tk, dm), lambda i, l: (l, 0)),
                  pl.BlockSpec((tm, dm), lambda i, l: (i, 0)),
                  pl.BlockSpec((1, dm), lambda i, l: (0, 0))],
        out_specs=pl.BlockSpec((tm, dm), lambda i, l: (i, 0)),
        out_shape=jax.ShapeDtypeStruct((m, dm), jnp.float32),
        scratch_shapes=[pltpu.VMEM((tm, dm), jnp.float32)],
        compiler_params=pltpu.CompilerParams(
            dimension_semantics=("parallel", "arbitrary"), vmem_limit_bytes=VMEM_LIMIT),
        name="ffn_down_norm",
    )(h, w_down.astype(MXU_DTYPE), x, g.reshape(1, dm).astype(jnp.float32))


def dense_tail(x, o_nsa, o_dsa, gm, kv_mem, prev_u, t_real, g_mem, g_ffn, g_final, w_oa, w_ob, w_o, w_mq, w_mo,
               w_up, conv_w, conv_b, w_down):
    b, t, dm = x.shape
    rows = lambda a: a.reshape(b * t, a.shape[-1])
    z = gated_merge(rows(o_nsa), rows(o_dsa), rows(gm), w_oa, w_ob)
    x1 = out_proj_residual(z, w_o, rows(x)).reshape(b, t, dm)
    x2, xn2 = mem_block(x1, g_mem, g_ffn, w_mq, kv_mem, w_mo)
    h, state = ffn_up(xn2, prev_u, w_up, conv_w, conv_b, t_real)
    y = ffn_down_norm(rows(h), w_down, rows(x2), g_final)
    return y.reshape(b, t, dm), state


def _dot_nt(a, b):
    return lax.dot_general(a, b, (((1,), (1,)), ((), ())), preferred_element_type=jnp.float32)


def _iota(shape, dim):
    return lax.broadcasted_iota(jnp.int32, shape, dim)


def _flash_init(rows, d):
    return (jnp.full((rows, 1), NEG_INF, jnp.float32), jnp.zeros((rows, 1), jnp.float32),
            jnp.zeros((rows, d), jnp.float32))


def _flash_step(carry, q, k, v, madd, nh, scale=None, kv_t=False):
    m, l, acc = carry
    s = jnp.dot(q, k, preferred_element_type=jnp.float32) if kv_t else _dot_nt(q, k)
    if scale is not None:
        s = s * scale
    r, kb = s.shape
    s = (s.reshape(nh, r // nh, kb) + madd[None]).reshape(r, kb)
    m_new = jnp.maximum(m, jnp.max(s, axis=1, keepdims=True))
    m_safe = jnp.where(m_new == NEG_INF, 0.0, m_new)
    p = jnp.exp(s - m_safe)
    alpha = jnp.exp(m - m_safe)
    l = alpha * l + jnp.sum(p, axis=1, keepdims=True)
    pv = _dot_nt(p.astype(v.dtype), v) if kv_t else jnp.dot(p.astype(v.dtype), v, preferred_element_type=jnp.float32)
    return m_new, l, alpha * acc + pv


def _flash_finish(carry):
    _, l, acc = carry
    return acc / jnp.maximum(l, 1e-30)


def _split_dot(x, m01):
    hi = x.astype(jnp.bfloat16)
    r1 = x - hi.astype(jnp.float32)
    mid = r1.astype(jnp.bfloat16)
    lo = (r1 - mid.astype(jnp.float32)).astype(jnp.bfloat16)
    dot = functools.partial(jnp.dot, preferred_element_type=jnp.float32)
    return dot(hi, m01) + dot(mid, m01) + dot(lo, m01)


def _nsa_group(g, i, qc_ref, qr_ref, gate_ref, kc_ref, vc_ref, ks_ref, vs_ref, kw_ref, vw_ref, mask_ref,
               *, tq, kb, qpos0, wbase, ns, n_sel, kv_t):
    nh, d = HPG_A, HEAD_DIM_A
    rows = nh * tq
    nc = kc_ref.shape[3]
    nsp = _round_up(ns, LANE)
    t0 = qpos0 + i * tq
    nj = (t0 + tq - 1) // kb + 1
    cols = slice(g * d, (g + 1) * d)

    def stack_heads(q_ref):
        return jnp.concatenate([q_ref[0, :, (g * nh + h) * d:(g * nh + h + 1) * d] for h in range(nh)], axis=0)

    qc = stack_heads(qc_ref)
    qr = stack_heads(qr_ref)
    tpos = t0 + _iota((tq, 1), 0)

    blk_last = _iota((1, nc), 1) * CMP_STRIDE + (CMP_BLOCK - 1)
    madd_c = jnp.where(blk_last <= tpos, 0.0, NEG_INF)
    s = _dot_nt(qc, kc_ref[0, 0, g]).reshape(nh, tq, nc) + madd_c[None]
    m = jnp.max(s, axis=2, keepdims=True)
    m = jnp.where(m == NEG_INF, 0.0, m)
    e = jnp.exp(s - m)
    p = e / jnp.maximum(jnp.sum(e, axis=2, keepdims=True), 1e-30)
    o_cmp = jnp.dot(p.reshape(rows, nc).astype(vc_ref.dtype), vc_ref[0, 0, g], preferred_element_type=jnp.float32)

    imp = jnp.sum(p, axis=0)
    c_id = _iota((nc, nsp), 0)
    m_id = _iota((nc, nsp), 1)
    overlap = (jnp.right_shift(c_id, CMP_PER_SLC_SHIFT) == m_id) | (c_id == m_id * CMP_PER_SLC - 1)
    score = _split_dot(imp, overlap.astype(jnp.bfloat16))
    blk = _iota((1, nsp), 1)
    cur = jnp.right_shift(tpos, SLC_SHIFT)
    forced = (blk == 0) | (blk == cur) | (blk == cur - 1)
    sc = jnp.where(forced, POS_INF, jnp.where(blk * SLC_BLOCK <= tpos, score, NEG_INF))
    if tq % LANE == 0 and nsp % LANE == 0:
        sc_t = sc.T
        blk_t = _iota((nsp, 1), 0)
        rank_t = jnp.zeros((nsp, tq), jnp.float32)
        for mp in range(ns):
            ref = sc_t[mp:mp + 1, :]
            beats = (ref > sc_t) | ((ref == sc_t) & (blk_t > mp))
            rank_t = rank_t + jnp.where(beats, 1.0, 0.0)
        sel = jnp.where(rank_t < n_sel, 1.0, 0.0).T.astype(jnp.bfloat16)
    else:
        rank = jnp.zeros((tq, nsp), jnp.float32)
        for mp in range(ns):
            col = sc[:, mp:mp + 1]
            beats = (col > sc) | ((col == sc) & (blk > mp))
            rank = rank + jnp.where(beats, 1.0, 0.0)
        sel = jnp.where(rank < n_sel, 1.0, 0.0).astype(jnp.bfloat16)

    def make_mask(j, _):
        kpos = j * kb + _iota((1, kb), 1)
        expand = (jnp.right_shift(j * kb + _iota((nsp, kb), 1), SLC_SHIFT) == _iota((nsp, kb), 0))
        hit = jnp.dot(sel, expand.astype(jnp.bfloat16), preferred_element_type=jnp.float32)
        mask_ref[j] = jnp.where((hit > 0.5) & (kpos <= tpos), 0.0, NEG_INF)
        return 0

    def slc_body(j, carry):
        off = pl.multiple_of(j * kb, kb)
        return _flash_step(carry, qr, ks_ref[0, pl.ds(off, kb), cols], vs_ref[0, pl.ds(off, kb), cols],
                           mask_ref[j], nh)

    if kv_t:
        sel_f = sel.astype(jnp.float32)
        low_half = _iota((tq, LANE), 1) < SLC_BLOCK
        pieces = []
        for v in range(kb // LANE):
            hit = jnp.where(low_half, sel_f[:, 2 * v:2 * v + 1], sel_f[:, 2 * v + 1:2 * v + 2])
            kpos = v * LANE + _iota((1, LANE), 1)
            pieces.append(jnp.where((hit > 0.5) & (kpos <= tpos), 0.0, NEG_INF))
        o_slc = _flash_finish(_flash_step(_flash_init(rows, d), qr, ks_ref[0, cols, :], vs_ref[0, cols, :],
                                          jnp.concatenate(pieces, axis=1), nh, kv_t=True))
    else:
        lax.fori_loop(0, nj, make_mask, 0)
        o_slc = _flash_finish(lax.fori_loop(0, nj, slc_body, _flash_init(rows, d)))

    wk = min(_round_up(WINDOW + tq, WIN_CHUNK), kw_ref.shape[1])
    first = jnp.maximum(t0 - (WINDOW - 1) - wbase, 0) // WIN_CHUNK * WIN_CHUNK
    off = pl.multiple_of(jnp.minimum(first, kw_ref.shape[1] - wk), WIN_CHUNK)
    dist = tpos - (wbase + off + _iota((1, wk), 1))
    madd_w = jnp.where((dist >= 0) & (dist < WINDOW), 0.0, NEG_INF)
    o_win = _flash_finish(_flash_step(_flash_init(rows, d), qr, kw_ref[0, pl.ds(off, wk), cols],
                                      vw_ref[0, pl.ds(off, wk), cols], madd_w, nh))

    gates = gate_ref[0]
    outs = []
    for h in range(nh):
        rs = slice(h * tq, (h + 1) * tq)
        c = g * nh + h
        outs.append(gates[:, c:c + 1] * o_cmp[rs] + gates[:, N_HEADS_A + c:N_HEADS_A + c + 1] * o_slc[rs]
                    + gates[:, 2 * N_HEADS_A + c:2 * N_HEADS_A + c + 1] * o_win[rs])
    return outs


def _nsa_kernel(qc_ref, qr_ref, gate_ref, kc_ref, vc_ref, ks_ref, vs_ref, kw_ref, vw_ref, o_ref, mask_ref, **kw):
    i = pl.program_id(1)
    outs = []
    for g in range(KV_GROUPS_A):
        outs += _nsa_group(g, i, qc_ref, qr_ref, gate_ref, kc_ref, vc_ref, ks_ref, vs_ref, kw_ref, vw_ref,
                           mask_ref, **kw)
    o_ref[0] = jnp.concatenate(outs, axis=1).astype(o_ref.dtype)


def nsa_attention(qc, qr, gates, kvc, ks, vs, kw, vw, *, tq, qpos0, wbase, n_keys, kv_t=False):
    b, t, hd = qc.shape
    l = ks.shape[2] if kv_t else ks.shape[1]
    lw = kw.shape[1]
    nc, d = kvc.shape[3:]
    kb = min(KEY_CHUNK, l) if tq >= TQ else l
    assert l % kb == 0 and t % tq == 0 and lw % WIN_CHUNK == 0 and (kb == l or not kv_t)
    assert (qpos0 + t - 1) // kb + 1 <= l // kb and (qpos0 + t - 1 - wbase) // WIN_CHUNK + 1 <= lw // WIN_CHUNK
    assert (qpos0 - wbase) % WIN_CHUNK == 0 and WIN_CHUNK % tq == 0 and tq > 1
    assert 2 * SLC_BLOCK == LANE or not kv_t
    ns = l // SLC_BLOCK
    n_sel = min(N_SELECT, -(-n_keys // SLC_BLOCK))
    q_spec = pl.BlockSpec((1, tq, hd), lambda bi, i: (bi, i, 0))
    kc_spec = pl.BlockSpec((1, 1, KV_GROUPS_A, nc, d), lambda bi, i: (0, bi, 0, 0, 0))
    vc_spec = pl.BlockSpec((1, 1, KV_GROUPS_A, nc, d), lambda bi, i: (1, bi, 0, 0, 0))
    k_spec = pl.BlockSpec((1, KV_A, l) if kv_t else (1, l, KV_A), lambda bi, i: (bi, 0, 0))
    w_spec = pl.BlockSpec((1, lw, KV_A), lambda bi, i: (bi, 0, 0))
    return pl.pallas_call(
        functools.partial(_nsa_kernel, tq=tq, kb=kb, qpos0=qpos0, wbase=wbase, ns=ns, n_sel=n_sel, kv_t=kv_t),
        grid=(b, t // tq),
        in_specs=[q_spec, q_spec, pl.BlockSpec((1, tq, 3 * N_HEADS_A), lambda bi, i: (bi, i, 0)),
                  kc_spec, vc_spec, k_spec, k_spec, w_spec, w_spec],
        out_specs=q_spec,
        out_shape=jax.ShapeDtypeStruct((b, t, hd), MXU_DTYPE),
        scratch_shapes=[pltpu.VMEM((l // kb, tq, kb), jnp.float32)],
        compiler_params=pltpu.CompilerParams(
            dimension_semantics=("parallel", "arbitrary"), vmem_limit_bytes=VMEM_LIMIT),
        name="nsa_attention",
    )(qc, qr, gates, kvc, kvc, ks, vs, kw, vw)


def _dsa_kernel(qb_ref, qi_ref, wi_ref, kb_ref, vb_ref, ki_ref, o_ref, score_ref,
                *, tq, tq_real, kb, qpos0, n_keep, kv_t):
    i = pl.program_id(1)
    nh, d = N_HEADS_B, HEAD_DIM_B
    t0 = qpos0 + i * tq
    nj = (t0 + tq - 1) // kb + 1
    tpos = t0 + _iota((tq, 1), 0)
    w = wi_ref[0]

    def idx_body(j, carry):
        lo, hi = carry
        off = pl.multiple_of(j * kb, kb)
        kidx = ki_ref[0] if kv_t else ki_ref[0, pl.ds(off, kb), :]
        acc = jnp.zeros((tq, kb), jnp.float32)
        heads = [qi_ref[0, :, h * IDX_DIM:(h + 1) * IDX_DIM] for h in range(IDX_HEADS)]
        if kv_t:
            dots_all = jnp.dot(jnp.concatenate(heads, axis=0), kidx, preferred_element_type=jnp.float32)
        for h in range(IDX_HEADS):
            dots = dots_all[h * tq:(h + 1) * tq] if kv_t else _dot_nt(heads[h], kidx)
            acc = acc + w[:, h:h + 1] * jnp.maximum(dots, 0.0)
        vis = (off + _iota((1, kb), 1)) <= tpos
        score_ref[j] = jnp.where(vis, acc, NEG_INF)
        lo = jnp.minimum(lo, jnp.min(jnp.where(vis, acc, POS_INF), axis=1, keepdims=True))
        hi = jnp.maximum(hi, jnp.max(jnp.where(vis, acc, NEG_INF), axis=1, keepdims=True))
        return lo, hi

    lo, hi = lax.fori_loop(0, nj, idx_body, (jnp.full((tq, 1), POS_INF, jnp.float32),
                                             jnp.full((tq, 1), NEG_INF, jnp.float32)))

    k = float(n_keep)
    n_vis = (tpos + 1).astype(jnp.float32)
    n_keys = score_ref.shape[0] * kb

    def reduce_scores(pred, pick, init, combine, lane_reduce):
        def body(j, acc):
            sc = score_ref[j]
            for c in range(kb // LANE):
                kpos = (j * kb + c * LANE + _iota((1, LANE), 1)).astype(jnp.float32)
                piece = sc[:, c * LANE:(c + 1) * LANE]
                acc = combine(acc, pick(pred(piece, kpos), piece))
            return acc

        return lane_reduce(lax.fori_loop(0, nj, body, jnp.full((tq, LANE), init, jnp.float32)),
                           axis=1, keepdims=True)

    def count_where(pred):
        return reduce_scores(pred, lambda m, _: jnp.where(m, 1.0, 0.0), 0.0, jnp.add, jnp.sum)

    def min_where(pred):
        return reduce_scores(pred, lambda m, x: jnp.where(m, x, POS_INF), POS_INF, jnp.minimum, jnp.min)

    real = _iota((tq, 1), 0) < tq_real

    def any_row(flag):
        return jnp.max(jnp.where(real & flag, 1.0, 0.0)) > 0.0

    def bisect(state):
        it, lo, hi, cnt_lo = state
        mid = 0.5 * (lo + hi)
        mid_b = jnp.broadcast_to(mid, (tq, LANE))
        cnt = count_where(lambda x, _: x >= mid_b)
        ge = cnt >= k
        return it + 1, jnp.where(ge, mid, lo), jnp.where(ge, hi, mid), jnp.where(ge, cnt, cnt_lo)

    _, thr, _, cnt_lo = lax.while_loop(
        lambda st: (st[0] < BISECT_ITERS) & any_row((st[3] != k) & (n_vis > k)), bisect,
        (jnp.int32(0), lo, hi, n_vis))

    def break_ties():
        def above(v):
            v_b = jnp.broadcast_to(v, (tq, LANE))
            return count_where(lambda x, _: x > v_b)

        def strip(state):
            it, v, c_gt = state
            v_b = jnp.broadcast_to(v, (tq, LANE))
            v_next = jnp.where(c_gt >= k, min_where(lambda x, _: x > v_b), v)
            return it + 1, v_next, above(v_next)

        thr_b = jnp.broadcast_to(thr, (tq, LANE))
        v0 = min_where(lambda x, _: x >= thr_b)
        _, v, c_gt = lax.while_loop(lambda st: (st[0] < TIE_STRIP_ITERS) & any_row(st[2] >= k), strip,
                                    (jnp.int32(0), v0, above(v0)))
        need = k - c_gt
        v_b = jnp.broadcast_to(v, (tq, LANE))

        def narrow(_, bounds):
            j_lo, j_hi = bounds
            mid = jnp.floor(0.5 * (j_lo + j_hi))
            ge = count_where(lambda x, kpos: (x == v_b) & (kpos <= mid)) >= need
            return jnp.where(ge, j_lo, mid), jnp.where(ge, mid, j_hi)

        _, j_max = lax.fori_loop(0, n_keys.bit_length(), narrow,
                                 (jnp.full((tq, 1), -1.0, jnp.float32), jnp.full((tq, 1), n_keys - 1.0, jnp.float32)))
        return v, j_max

    thr, j_max = lax.cond(any_row((cnt_lo > k) & (n_vis > k)), break_ties,
                          lambda: (thr, jnp.full((tq, 1), float(n_keys), jnp.float32)))

    q = jnp.concatenate([qb_ref[0, :, h * d:(h + 1) * d] for h in range(nh)], axis=0)

    def att_body(j, carry):
        off = pl.multiple_of(j * kb, kb)
        sc = score_ref[j]
        kpos = (off + _iota((1, kb), 1)).astype(jnp.float32)
        madd = jnp.where((sc > thr) | ((sc == thr) & (kpos <= j_max)), 0.0, NEG_INF)
        if kv_t:
            return _flash_step(carry, q, kb_ref[0], vb_ref[0], madd, nh, scale=d ** -0.5, kv_t=True)
        return _flash_step(carry, q, kb_ref[0, pl.ds(off, kb), :], vb_ref[0, pl.ds(off, kb), :], madd, nh,
                           scale=d ** -0.5)

    o = _flash_finish(lax.fori_loop(0, nj, att_body, _flash_init(nh * tq, d)))
    o_ref[0] = jnp.concatenate([o[h * tq:(h + 1) * tq] for h in range(nh)], axis=1).astype(o_ref.dtype)


def dsa_attention(qb, qi, wi, kb_, vb, ki, *, tq, qpos0, n_keep, kv_t=False, tq_real=None):
    b, t, hd = qb.shape
    l = kb_.shape[2] if kv_t else kb_.shape[1]
    kb = min(KEY_CHUNK, l) if tq >= TQ else l
    assert l % kb == 0 and t % tq == 0 and (qpos0 + t - 1) // kb + 1 <= l // kb and (kb == l or not kv_t)
    kv_spec = lambda width: pl.BlockSpec((1, width, l) if kv_t else (1, l, width), lambda bi, i: (bi, 0, 0))
    return pl.pallas_call(
        functools.partial(_dsa_kernel, tq=tq, tq_real=tq if tq_real is None else tq_real, kb=kb, qpos0=qpos0,
                          n_keep=n_keep, kv_t=kv_t),
        grid=(b, t // tq),
        in_specs=[pl.BlockSpec((1, tq, hd), lambda bi, i: (bi, i, 0)),
                  pl.BlockSpec((1, tq, IDX_HEADS * IDX_DIM), lambda bi, i: (bi, i, 0)),
                  pl.BlockSpec((1, tq, IDX_HEADS), lambda bi, i: (bi, i, 0)),
                  kv_spec(HEAD_DIM_B), kv_spec(HEAD_DIM_B), kv_spec(IDX_DIM)],
        out_specs=pl.BlockSpec((1, tq, hd), lambda bi, i: (bi, i, 0)),
        out_shape=jax.ShapeDtypeStruct((b, t, hd), MXU_DTYPE),
        scratch_shapes=[pltpu.VMEM((l // kb, tq, kb), jnp.float32)],
        compiler_params=pltpu.CompilerParams(
            dimension_semantics=("parallel", "arbitrary"), vmem_limit_bytes=VMEM_LIMIT),
        name="dsa_attention",
    )(qb, qi, wi, kb_, vb, ki)


def _page_maps(n_pages, pp):
    n_steps = n_pages // pp

    def page_map(r):
        return lambda b, s, pt: (pt[b * n_pages + jnp.minimum(s, n_steps - 1) * pp + r], 0, 0)

    return n_steps, page_map


def _chunk_rows(tok_ref, sec, row0, n):
    d = HEAD_DIM_A
    first_half = _iota((n, KV_A), 1) < d
    pieces = [[] for _ in range(KV_GROUPS_A)]
    for j in range(0, CMP_STRIDE, 2):
        a, b = [tok_ref[sec, pl.ds(row0 + jj, n, stride=CMP_STRIDE), :] for jj in (j, j + 1)]
        pieces[0].append(jnp.where(first_half, a, pltpu.roll(b, d, 1)))
        pieces[1].append(jnp.where(first_half, pltpu.roll(a, d, 1), b))
    return [jnp.concatenate(p, axis=1) for p in pieces]


def _nsa_gather_kernel(pt_ref, *refs, pp, n_steps, rows):
    del pt_ref
    pages, (tail_tok_ref, tail_t_ref, pe_ref) = refs[:pp], refs[pp:pp + 3]
    zt_ref, zb_ref, ks_ref, vs_ref, tok_ref = refs[pp + 3:]
    is_tail = pl.program_id(1) == n_steps
    z = [[[] for _ in range(KV_GROUPS_A)] for _ in range(2)]
    for r in range(pp):
        cs = slice(r * rows, (r + 1) * rows)
        ks_ref[0, :, cs] = jnp.where(is_tail, tail_t_ref[0, :KV_A, cs],
                                     pages[r][0, 2 * KV_A:3 * KV_A, :]).astype(ks_ref.dtype)
        vs_ref[0, :, cs] = jnp.where(is_tail, tail_t_ref[0, KV_A:, cs],
                                     pages[r][0, 3 * KV_A:, :]).astype(vs_ref.dtype)
        for sec in range(2):
            tok_ref[sec, cs, :] = jnp.where(is_tail, tail_tok_ref[0, sec, cs, :],
                                            pages[r][0, sec * KV_A:(sec + 1) * KV_A, :].T)
            for g, zg in enumerate(_chunk_rows(tok_ref, sec, r * rows, rows // CMP_STRIDE)):
                z[sec][g].append(zg)
    for sec in range(2):
        for g in range(KV_GROUPS_A):
            zf = jnp.concatenate(z[sec][g], axis=0)
            zt_ref[sec, 0, g] = (zf + pe_ref[sec, 0]).astype(zt_ref.dtype)
            zb_ref[sec, 0, g] = (zf + pe_ref[sec, 1]).astype(zb_ref.dtype)


def nsa_gather(cache_t, page_table, new_rows, cmp_pe):
    db, n_pages = page_table.shape
    width, rows = cache_t.shape[1:]
    pp = PAGES_PER_STEP
    n_steps, page_map = _page_maps(n_pages, pp)
    l = (n_steps + 1) * pp * rows
    cps = pp * rows // CMP_STRIDE
    flat = CMP_STRIDE * HEAD_DIM_A
    pe = cmp_pe.reshape(2, 2, 1, flat).astype(jnp.float32)
    tail = pad_rows(new_rows, pp * rows)
    tail_tok = tail[:, :, :2 * KV_A].reshape(db, pp * rows, 2, KV_A).swapaxes(1, 2)
    tail_t = tail[:, :, 2 * KV_A:].swapaxes(1, 2)
    z_spec = pl.BlockSpec((2, 1, KV_GROUPS_A, cps, flat), lambda b, s, pt: (0, b, 0, s, 0))
    r_spec = pl.BlockSpec((1, KV_A, pp * rows), lambda b, s, pt: (b, 0, s))
    z_shape = jax.ShapeDtypeStruct((2, db, KV_GROUPS_A, l // CMP_STRIDE, flat), MXU_DTYPE)
    r_shape = jax.ShapeDtypeStruct((db, KV_A, l), MXU_DTYPE)
    return pl.pallas_call(
        functools.partial(_nsa_gather_kernel, pp=pp, n_steps=n_steps, rows=rows),
        grid_spec=pltpu.PrefetchScalarGridSpec(
            num_scalar_prefetch=1,
            grid=(db, n_steps + 1),
            in_specs=[pl.BlockSpec((1, width, rows), page_map(r)) for r in range(pp)]
            + [pl.BlockSpec((1, 2, pp * rows, KV_A), lambda b, s, pt: (b, 0, 0, 0)),
               pl.BlockSpec((1, 2 * KV_A, pp * rows), lambda b, s, pt: (b, 0, 0)),
               pl.BlockSpec((2, 2, 1, flat), lambda b, s, pt: (0, 0, 0, 0))],
            out_specs=[z_spec, z_spec, r_spec, r_spec],
            scratch_shapes=[pltpu.VMEM((2, pp * rows, KV_A), jnp.float32)]),
        out_shape=[z_shape, z_shape, r_shape, r_shape],
        compiler_params=pltpu.CompilerParams(
            dimension_semantics=("parallel", "arbitrary"), vmem_limit_bytes=VMEM_LIMIT),
        name="nsa_gather",
    )(page_table.reshape(-1), *([cache_t] * pp), tail_tok, tail_t, pe)


def _dsa_gather_kernel(pt_ref, *refs, pp, n_steps, rows):
    del pt_ref
    pages, tail_ref, (k_ref, v_ref, i_ref) = refs[:pp], refs[pp], refs[pp + 1:]
    is_tail = pl.program_id(1) == n_steps
    d = HEAD_DIM_B
    for r in range(pp):
        cs = slice(r * rows, (r + 1) * rows)
        x = jnp.where(is_tail, tail_ref[0, :, cs], pages[r][0])
        k_ref[0, :, cs] = x[:d].astype(k_ref.dtype)
        v_ref[0, :, cs] = x[d:2 * d].astype(v_ref.dtype)
        i_ref[0, :, cs] = x[2 * d:].astype(i_ref.dtype)


def dsa_gather(cache_t, page_table, new_rows):
    db, n_pages = page_table.shape
    width, rows = cache_t.shape[1:]
    pp = PAGES_PER_STEP
    n_steps, page_map = _page_maps(n_pages, pp)
    l = (n_steps + 1) * pp * rows
    tail_t = pad_rows(new_rows, pp * rows).swapaxes(1, 2)
    out_spec = lambda w: pl.BlockSpec((1, w, pp * rows), lambda b, s, pt: (b, 0, s))
    widths = (HEAD_DIM_B, HEAD_DIM_B, IDX_DIM)
    return pl.pallas_call(
        functools.partial(_dsa_gather_kernel, pp=pp, n_steps=n_steps, rows=rows),
        grid_spec=pltpu.PrefetchScalarGridSpec(
            num_scalar_prefetch=1,
            grid=(db, n_steps + 1),
            in_specs=[pl.BlockSpec((1, width, rows), page_map(r)) for r in range(pp)]
            + [pl.BlockSpec((1, width, pp * rows), lambda b, s, pt: (b, 0, 0))],
            out_specs=[out_spec(w) for w in widths]),
        out_shape=[jax.ShapeDtypeStruct((db, w, l), MXU_DTYPE) for w in widths],
        compiler_params=pltpu.CompilerParams(
            dimension_semantics=("parallel", "arbitrary"), vmem_limit_bytes=VMEM_LIMIT),
        name="dsa_gather",
    )(page_table.reshape(-1), *([cache_t] * pp), tail_t)


def _compress_kernel(zt_ref, zb_ref, pe_ref, w1t_ref, w1b_ref, b1_ref, w2_ref, o_ref, ab_ref, *, ncp):
    dot = functools.partial(jnp.dot, preferred_element_type=jnp.float32)
    ch = zb_ref.shape[2]
    at = dot(zt_ref[0, 0], w1t_ref[0])
    ab_ref[:ch] = dot(zb_ref[0, 0], w1b_ref[0])
    pe_rows = jnp.broadcast_to(pe_ref[0, 1], (SUBLANE, pe_ref.shape[3])).astype(zb_ref.dtype)
    ab_ref[ch:] = dot(pe_rows, w1b_ref[0])
    h = jax.nn.gelu(at[:ncp] + ab_ref[pl.ds(1, ncp), :] + b1_ref[0])
    o_ref[0, 0, :ncp] = dot(h.astype(w2_ref.dtype), w2_ref[0]).astype(o_ref.dtype)
    if o_ref.shape[2] > ncp:
        o_ref[0, 0, ncp:] = jnp.zeros((o_ref.shape[2] - ncp, o_ref.shape[3]), o_ref.dtype)


def compress(zt, zb, cmp_pe, w1, b1, w2, n_keys):
    ncp = _round_up(-(-n_keys // CMP_STRIDE), BF16_ROWS)
    ncl = ncp if ncp <= LANE else _round_up(ncp, LANE)
    _, nb, ch, kdim = zt.shape
    hid = w1.shape[-1]
    d = w2.shape[-1]
    assert ch + SUBLANE >= ncp + 1 and ch % SUBLANE == 0
    w1 = w1.astype(MXU_DTYPE)
    pe = cmp_pe.reshape(2, 2, 1, kdim).astype(jnp.float32)
    z_spec = pl.BlockSpec((1, 1, ch, kdim), lambda s, n: (s, n, 0, 0))
    return pl.pallas_call(
        functools.partial(_compress_kernel, ncp=ncp),
        grid=(2, nb),
        in_specs=[z_spec, z_spec,
                  pl.BlockSpec((1, 2, 1, kdim), lambda s, n: (s, 0, 0, 0)),
                  pl.BlockSpec((1, kdim, hid), lambda s, n: (s, 0, 0)),
                  pl.BlockSpec((1, kdim, hid), lambda s, n: (s, 1, 0)),
                  pl.BlockSpec((1, 1, hid), lambda s, n: (s, 0, 0)),
                  pl.BlockSpec((1, hid, d), lambda s, n: (s, 0, 0))],
        out_specs=pl.BlockSpec((1, 1, ncl, d), lambda s, n: (s, n, 0, 0)),
        out_shape=jax.ShapeDtypeStruct((2, nb, ncl, d), MXU_DTYPE),
        scratch_shapes=[pltpu.VMEM((ch + SUBLANE, hid), jnp.float32)],
        compiler_params=pltpu.CompilerParams(
            dimension_semantics=("parallel", "parallel"), vmem_limit_bytes=VMEM_LIMIT),
        name="compress",
    )(zt, zb, pe, w1, w1, b1.reshape(2, 1, hid).astype(jnp.float32), w2.astype(MXU_DTYPE))


def pad_rows(a, n):
    return jnp.pad(a, [(0, 0), (0, n - a.shape[1])] + [(0, 0)] * (a.ndim - 2))


def _rope_tables(pos, d):
    half = d // ROT_FRACTION // 2
    inv = ROPE_THETA ** (-jnp.arange(half, dtype=jnp.float32) / half)
    ang = pos.astype(jnp.float32)[:, None] * inv[None, :]
    lane = jnp.arange(LANE) % d
    cos = jnp.cos(ang)[:, lane % half]
    sin = jnp.sin(ang)[:, lane % half]
    one, zero = jnp.ones_like(cos), jnp.zeros_like(cos)
    c = jnp.where(lane < 2 * half, cos, one)
    sa = jnp.where((lane >= half) & (lane < 2 * half), sin, zero)
    sb = jnp.where(lane < half, -sin, zero)
    return jnp.stack([c, sa, sb])


def _rope(x, t_ref, half):
    c, sa, sb = t_ref[0], t_ref[1], t_ref[2]
    outs = []
    for j in range(x.shape[1] // LANE):
        xs = x[:, j * LANE:(j + 1) * LANE]
        outs.append(xs * c + pltpu.roll(xs, half, 1) * sa + pltpu.roll(xs, LANE - half, 1) * sb)
    return outs[0] if len(outs) == 1 else jnp.concatenate(outs, axis=1)


_QA = N_HEADS_A * HEAD_DIM_A
_KVA = 6 * KV_A
_QB = N_HEADS_B * HEAD_DIM_B
_KVB = 2 * HEAD_DIM_B
_QI = IDX_HEADS * IDX_DIM
_MISC = IDX_DIM + IDX_HEADS + 3 * N_HEADS_A
PREP_WIDTH = _QA + _KVA + _QB + _KVB + _QI + _MISC
assert _MISC == LANE


def _prep_kernel(x_ref, t64_ref, t128_ref, pe_ref, qc_ref, qr_ref, qb_ref, qi_ref, nsa_ref, win_ref, dsa_ref,
                 ks_ref, vs_ref, kw_ref, vw_ref, kb_ref, vb_ref, ki_ref, wi_ref, gate_ref, *z_refs, tm):
    x = x_ref[...]
    o = 0
    qa = x[:, o:o + _QA]; o += _QA
    kva = x[:, o:o + _KVA]; o += _KVA
    qb = x[:, o:o + _QB]; o += _QB
    kvb = x[:, o:o + _KVB]; o += _KVB
    qi = x[:, o:o + _QI]; o += _QI
    misc = x[:, o:o + _MISC]
    h64, h128 = HEAD_DIM_A // ROT_FRACTION // 2, HEAD_DIM_B // ROT_FRACTION // 2
    sec = lambda i: kva[:, i * KV_A:(i + 1) * KV_A]
    mx = lambda a: a.astype(qc_ref.dtype)
    qc_ref[...] = mx(qa * HEAD_DIM_A ** -0.5)
    qr_ref[...] = mx(_rope(qa, t64_ref, h64) * HEAD_DIM_A ** -0.5)
    k_slc, k_win = _rope(sec(2), t64_ref, h64), _rope(sec(4), t64_ref, h64)
    nsa_ref[:, :2 * KV_A] = kva[:, :2 * KV_A]
    nsa_ref[:, 2 * KV_A:3 * KV_A] = k_slc
    nsa_ref[:, 3 * KV_A:] = sec(3)
    win_ref[:, :KV_A] = k_win
    win_ref[:, KV_A:] = sec(5)
    ks_ref[...], vs_ref[...], kw_ref[...], vw_ref[...] = mx(k_slc), mx(sec(3)), mx(k_win), mx(sec(5))
    qb_ref[...] = mx(_rope(qb, t128_ref, h128))
    k_b, v_b = _rope(kvb[:, :HEAD_DIM_B], t128_ref, h128), kvb[:, HEAD_DIM_B:]
    k_idx = _rope(misc, t64_ref, h64)[:, :IDX_DIM]
    dsa_ref[:, :HEAD_DIM_B] = k_b
    dsa_ref[:, HEAD_DIM_B:2 * HEAD_DIM_B] = v_b
    dsa_ref[:, 2 * HEAD_DIM_B:] = k_idx
    kb_ref[...], vb_ref[...], ki_ref[...] = mx(k_b), mx(v_b), mx(k_idx)
    qi_ref[...] = mx(_rope(qi, t64_ref, h64) * IDX_DIM ** -0.5)
    wi_ref[...] = misc[:, IDX_DIM:IDX_DIM + IDX_HEADS] * IDX_HEADS ** -0.5
    gate_ref[...] = jax.nn.sigmoid(misc[:, IDX_DIM + IDX_HEADS:])
    if z_refs:
        zt_ref, zb_ref, tok_ref = z_refs
        for s_ in range(2):
            tok_ref[s_] = sec(s_)
            for g, zg in enumerate(_chunk_rows(tok_ref, s_, 0, tm // CMP_STRIDE)):
                zt_ref[s_, 0, g] = (zg + pe_ref[s_, 0]).astype(zt_ref.dtype)
                zb_ref[s_, 0, g] = (zg + pe_ref[s_, 1]).astype(zb_ref.dtype)


def prep(proj, pos, cmp_pe, with_chunks):
    b, t, width = proj.shape
    assert width == PREP_WIDTH
    m = b * t
    tm = 256 if t % 256 == 0 else m
    assert m % tm == 0 and t % tm in (0, t)
    nt = max(t // tm, 1)
    flat = CMP_STRIDE * HEAD_DIM_A
    names = ['qc', 'qr', 'qb', 'qi', 'nsa_rows', 'win_rows', 'dsa_rows', 'ks', 'vs', 'kw', 'vw', 'kb', 'vb', 'ki',
             'wi', 'gates']
    widths = [_QA, _QA, _QB, _QI, NSA_SECTIONS * KV_A, 2 * KV_A, DSA_ROW, KV_A, KV_A, KV_A, KV_A, HEAD_DIM_B,
              HEAD_DIM_B, IDX_DIM, IDX_HEADS, 3 * N_HEADS_A]
    dtypes = [MXU_DTYPE] * 4 + [jnp.float32] * 3 + [MXU_DTYPE] * 7 + [jnp.float32] * 2
    row = lambda w: pl.BlockSpec((tm, w), lambda i: (i, 0))
    out_specs = [row(w) for w in widths]
    out_shape = [jax.ShapeDtypeStruct((m, w), dt) for w, dt in zip(widths, dtypes)]
    scratch = []
    if with_chunks:
        assert tm % CMP_STRIDE == 0 and t % tm == 0
        z_spec = pl.BlockSpec((2, 1, KV_GROUPS_A, tm // CMP_STRIDE, flat), lambda i: (0, i // nt, 0, i % nt, 0))
        z_shape = jax.ShapeDtypeStruct((2, b, KV_GROUPS_A, t // CMP_STRIDE, flat), MXU_DTYPE)
        out_specs += [z_spec, z_spec]
        out_shape += [z_shape, z_shape]
        names += ['zt', 'zb']
        scratch = [pltpu.VMEM((2, tm, KV_A), jnp.float32)]
    t_spec = pl.BlockSpec((3, tm, LANE), lambda i: (0, i, 0))
    outs = pl.pallas_call(
        functools.partial(_prep_kernel, tm=tm),
        grid=(m // tm,),
        in_specs=[row(width), t_spec, t_spec, pl.BlockSpec((2, 2, 1, flat), lambda i: (0, 0, 0, 0))],
        out_specs=out_specs,
        out_shape=out_shape,
        scratch_shapes=scratch,
        compiler_params=pltpu.CompilerParams(dimension_semantics=("parallel",), vmem_limit_bytes=VMEM_LIMIT),
        name="prep",
    )(proj.reshape(m, width), _rope_tables(pos, HEAD_DIM_A), _rope_tables(pos, HEAD_DIM_B),
      cmp_pe.reshape(2, 2, 1, flat).astype(jnp.float32))
    return {n: (o if o.ndim > 2 else o.reshape(b, t, o.shape[-1])) for n, o in zip(names, outs)}


def project(x, g, w_in, sizes):
    starts = [sum(sizes[:i]) for i in range(len(sizes))]
    qa, kva, ga, qb, kvb, qi, ki, wi, gm = [slice(o, o + n) for o, n in zip(starts, sizes)]
    w_bf = w_in.astype(MXU_DTYPE)
    w_prep = jnp.concatenate([w_bf[:, c] for c in (qa, kva, qb, kvb, qi, ki, wi, ga)], axis=1)
    return norm_matmul(x, g, w_prep), norm_matmul(x, g, w_bf[:, gm])


def mixer_prompt(x, g, w_in, cmp_pe, cmp_w1, cmp_b1, cmp_w2, sizes):
    b, s, _ = x.shape
    proj, gm = project(x, g, w_in, sizes)
    p = prep(proj, jnp.tile(jnp.arange(s, dtype=jnp.int32), b), cmp_pe, True)
    merge_bg = lambda a: a.reshape((2, b * KV_GROUPS_A) + a.shape[3:])
    kvc = compress(merge_bg(p['zt']), merge_bg(p['zb']), cmp_pe, cmp_w1, cmp_b1, cmp_w2, s)
    kvc = kvc.reshape(2, b, KV_GROUPS_A, kvc.shape[2], HEAD_DIM_A)
    o_nsa = nsa_attention(p['qc'], p['qr'], p['gates'], kvc, p['ks'], p['vs'], p['kw'], p['vw'],
                          tq=min(TQ, s), qpos0=0, wbase=0, n_keys=s)
    o_dsa = dsa_attention(p['qb'], p['qi'], p['wi'], p['kb'], p['vb'], p['ki'],
                          tq=min(TQ, s), qpos0=0, n_keep=min(DSA_TOPK, s // 4))
    nsa_rows = p['nsa_rows'].reshape(b, s, NSA_SECTIONS, KV_GROUPS_A, HEAD_DIM_A)
    win_state = p['win_rows'].reshape(b, s, 2, KV_GROUPS_A, HEAD_DIM_A)[:, -min(WINDOW, s):]
    return (o_nsa, o_dsa, gm), nsa_rows, win_state, p['dsa_rows']


def mixer_sample(x, g, cache_nsa, win_buf, cache_dsa, page_table, w_in, cmp_pe, cmp_w1, cmp_b1, cmp_w2, sizes):
    b, t, _ = x.shape
    page = cache_nsa.shape[1]
    past_len = page_table.shape[1] * page
    n_keys = past_len + t
    proj, gm = project(x, g, w_in, sizes)
    p = prep(proj, jnp.tile(past_len + jnp.arange(t, dtype=jnp.int32), b), cmp_pe, False)
    pad = lambda a: pad_rows(a, TQ_STEP)

    cache_nsa_t = jnp.transpose(cache_nsa, (0, 2, 3, 4, 1)).reshape(cache_nsa.shape[0], NSA_SECTIONS * KV_A, page)
    zt, zb, ks, vs = nsa_gather(cache_nsa_t, page_table, p['nsa_rows'], cmp_pe)
    merge_bg = lambda a: a.reshape((2, b * KV_GROUPS_A) + a.shape[3:])
    kvc = compress(merge_bg(zt), merge_bg(zb), cmp_pe, cmp_w1, cmp_b1, cmp_w2, n_keys)
    kvc = kvc.reshape(2, b, KV_GROUPS_A, kvc.shape[2], HEAD_DIM_A)
    w_len = win_buf.shape[1]
    win_new = p['win_rows'].reshape(b, t, 2, KV_GROUPS_A, HEAD_DIM_A)
    win_all = jnp.concatenate([win_buf, win_new], axis=1)
    win_pad = pad_rows(win_all, w_len + WIN_CHUNK).astype(MXU_DTYPE)
    kw = win_pad[:, :, 0].reshape(b, w_len + WIN_CHUNK, KV_A)
    vw = win_pad[:, :, 1].reshape(b, w_len + WIN_CHUNK, KV_A)
    o_nsa = nsa_attention(pad(p['qc']), pad(p['qr']), pad(p['gates']), kvc, ks, vs, kw, vw,
                          tq=TQ_STEP, qpos0=past_len, wbase=past_len - w_len, n_keys=n_keys, kv_t=True)

    kb_, vb, ki = dsa_gather(jnp.swapaxes(cache_dsa, 1, 2), page_table, p['dsa_rows'])
    o_dsa = dsa_attention(pad(p['qb']), pad(p['qi']), pad(p['wi']), kb_, vb, ki, tq=TQ_STEP, qpos0=past_len,
                          n_keep=min(DSA_TOPK, n_keys // 4), kv_t=True, tq_real=t)
    nsa_rows = p['nsa_rows'].reshape(b, t, NSA_SECTIONS, KV_GROUPS_A, HEAD_DIM_A)
    return (o_nsa, o_dsa, pad(gm)), nsa_rows, win_all[:, -w_len:], p['dsa_rows']


def kernel(x_prompt, x_sample, mem_prompt, cache_nsa_kv, state_nsa_win, cache_dsa_kv, cache_mem_kv, state_conv,
           page_table, norm_g, w_in, cmp_pe, cmp_w1, cmp_b1, cmp_w2, w_out_a, w_out_b, w_out, w_mem_q, w_mem_kv,
           w_mem_out, w_up, conv_w, conv_b, w_down, final_g):
    depth = w_in.shape[0]
    d_model = x_prompt.shape[-1]
    d_ff = w_down.shape[1]
    assert CONV_WIDTH == 3
    sizes = (N_HEADS_A * HEAD_DIM_A, 6 * KV_A, 3 * N_HEADS_A, N_HEADS_B * HEAD_DIM_B, 2 * HEAD_DIM_B,
             IDX_HEADS * IDX_DIM, IDX_DIM, IDX_HEADS, 2 * d_model)
    xp, xs = x_prompt, pad_rows(x_sample, TQ_STEP)
    t_step = x_sample.shape[1]
    nsa_p, nsa_s, win_p, win_s, dsa_p, dsa_s, mem_p, conv_p, conv_s = [], [], [], [], [], [], [], [], []
    for l in range(depth):
        assert l == depth - 1, "the fused FFN epilogue applies the final norm"
        branches_p, a, bwin, c = mixer_prompt(xp, norm_g[l, 0], w_in[l], cmp_pe[l], cmp_w1[l], cmp_b1[l], cmp_w2[l],
                                              sizes)
        nsa_p.append(a); win_p.append(bwin); dsa_p.append(c)
        branches_s, a, bwin, c = mixer_sample(xs[:, :t_step], norm_g[l, 0], cache_nsa_kv[l], state_nsa_win[l],
                                              cache_dsa_kv[l], page_table, w_in[l], cmp_pe[l], cmp_w1[l], cmp_b1[l],
                                              cmp_w2[l], sizes)
        nsa_s.append(a); win_s.append(bwin); dsa_s.append(c)
        kv_p = norm_matmul(mem_prompt, norm_g[l, 2], w_mem_kv[l])
        mem_p.append(kv_p.reshape(kv_p.shape[:2] + (2, MEM_HEADS, MEM_HEAD_DIM)))
        kv_s = cache_mem_kv[l].reshape(cache_mem_kv.shape[1:3] + (-1,))
        weights = (norm_g[l, 1], norm_g[l, 3], final_g, w_out_a[l], w_out_b[l], w_out[l], w_mem_q[l], w_mem_out[l],
                   w_up[l], conv_w[l], conv_b[l], w_down[l])
        xp, cp = dense_tail(xp, *branches_p, kv_p, jnp.zeros((xp.shape[0], CONV_WIDTH - 1, 2 * d_ff), xp.dtype),
                            xp.shape[1], *weights)
        xs, cs = dense_tail(xs, *branches_s, kv_s, state_conv[l], t_step, *weights)
        conv_p.append(cp); conv_s.append(cs)
    y_prompt, y_sample = xp, xs[:, :t_step]
    return (y_prompt, y_sample, jnp.stack(nsa_p), jnp.stack(nsa_s), jnp.stack(win_p), jnp.stack(win_s),
            jnp.stack(dsa_p), jnp.stack(dsa_s), jnp.stack(mem_p), jnp.stack(conv_p), jnp.stack(conv_s))
```

```python
import functools

import jax
import jax.numpy as jnp
from jax import lax
from jax.experimental import pallas as pl
from jax.experimental.pallas import tpu as pltpu

N_HEADS_A = 16
HEAD_DIM_A = 64
KV_GROUPS_A = 2
CMP_BLOCK = 32
CMP_STRIDE = 16
SLC_BLOCK = 64
N_SELECT = 16
WINDOW = 512
N_HEADS_B = 8
HEAD_DIM_B = 128
IDX_HEADS = 16
IDX_DIM = 64
DSA_TOPK = 256
MEM_HEADS = 4
MEM_HEAD_DIM = 128
CONV_WIDTH = 3
ROPE_THETA = 500000.0
ROT_FRACTION = 4
EPS = 1e-6
KV_A = KV_GROUPS_A * HEAD_DIM_A
CMP_PER_SLC = SLC_BLOCK // CMP_STRIDE
HPG_A = N_HEADS_A // KV_GROUPS_A
DSA_ROW = 2 * HEAD_DIM_B + IDX_DIM
NSA_SECTIONS = 4

LANE = 128
SUBLANE = 8
BF16_ROWS = 16
VMEM_LIMIT = 48 * 1024 * 1024

NEG_INF = float('-inf')
POS_INF = float('inf')
MXU_DTYPE = jnp.bfloat16
TQ = 128
TQ_STEP = BF16_ROWS
KEY_CHUNK = 512
WIN_CHUNK = 128
PAGES_PER_STEP = 16
BISECT_ITERS = 40
TIE_STRIP_ITERS = 64
SLC_SHIFT = SLC_BLOCK.bit_length() - 1
CMP_PER_SLC_SHIFT = CMP_PER_SLC.bit_length() - 1


def _round_up(n, m):
    return -(-n // m) * m


def _tile(n, cap):
    if n <= cap:
        return n
    best = None
    for t in range(LANE, cap + 1, LANE):
        if n % t == 0:
            best = t
    assert best is not None, (n, cap)
    return best


def _rms(x, g):
    return x * lax.rsqrt(jnp.mean(x * x, axis=-1, keepdims=True) + EPS) * g


def _norm_mm_kernel(x_ref, g_ref, w_ref, o_ref, xn_ref):
    @pl.when(pl.program_id(1) == 0)
    def _():
        xn_ref[...] = _rms(x_ref[...], g_ref[...]).astype(xn_ref.dtype)

    o_ref[...] = jnp.dot(xn_ref[...], w_ref[...], preferred_element_type=jnp.float32).astype(o_ref.dtype)


def norm_matmul(x, g, w):
    lead = x.shape[:-1]
    k = x.shape[-1]
    n = w.shape[-1]
    x2 = x.reshape(-1, k)
    m = x2.shape[0]
    tm = 512 if m % 512 == 0 else m
    tn = _tile(n, 1536)
    out = pl.pallas_call(
        _norm_mm_kernel,
        grid=(m // tm, n // tn),
        in_specs=[pl.BlockSpec((tm, k), lambda i, j: (i, 0)),
                  pl.BlockSpec((1, k), lambda i, j: (0, 0)),
                  pl.BlockSpec((k, tn), lambda i, j: (0, j))],
        out_specs=pl.BlockSpec((tm, tn), lambda i, j: (i, j)),
        out_shape=jax.ShapeDtypeStruct((m, n), jnp.float32),
        scratch_shapes=[pltpu.VMEM((tm, k), MXU_DTYPE)],
        compiler_params=pltpu.CompilerParams(
            dimension_semantics=("parallel", "arbitrary"), vmem_limit_bytes=VMEM_LIMIT),
        name="norm_matmul",
    )(x2, g.reshape(1, k).astype(jnp.float32), w.astype(MXU_DTYPE))
    return out.reshape(lead + (n,))


def _gated_merge_kernel(oa_ref, ob_ref, wa_ref, wb_ref, ga_ref, gb_ref, o_ref):
    dot = functools.partial(jnp.dot, preferred_element_type=jnp.float32)
    ya = dot(oa_ref[...], wa_ref[...])
    yb = dot(ob_ref[...], wb_ref[...])
    o_ref[...] = (jax.nn.sigmoid(ga_ref[...]) * ya + jax.nn.sigmoid(gb_ref[...]) * yb).astype(o_ref.dtype)


def gated_merge(o_nsa, o_dsa, gm, w_oa, w_ob):
    m, ka = o_nsa.shape
    kb = o_dsa.shape[1]
    n = w_oa.shape[1]
    tm = 512 if m % 512 == 0 else m
    tn = _tile(n, 1024)
    nb = n // tn
    return pl.pallas_call(
        _gated_merge_kernel,
        grid=(m // tm, nb),
        in_specs=[pl.BlockSpec((tm, ka), lambda i, j: (i, 0)),
                  pl.BlockSpec((tm, kb), lambda i, j: (i, 0)),
                  pl.BlockSpec((ka, tn), lambda i, j: (0, j)),
                  pl.BlockSpec((kb, tn), lambda i, j: (0, j)),
                  pl.BlockSpec((tm, tn), lambda i, j: (i, j)),
                  pl.BlockSpec((tm, tn), lambda i, j: (i, j + nb))],
        out_specs=pl.BlockSpec((tm, tn), lambda i, j: (i, j)),
        out_shape=jax.ShapeDtypeStruct((m, n), MXU_DTYPE),
        compiler_params=pltpu.CompilerParams(
            dimension_semantics=("parallel", "parallel"), vmem_limit_bytes=VMEM_LIMIT),
        name="gated_merge",
    )(o_nsa, o_dsa, w_oa.astype(MXU_DTYPE), w_ob.astype(MXU_DTYPE), gm, gm)


def _out_proj_kernel(z_ref, w_ref, x_ref, o_ref):
    o_ref[...] = x_ref[...] + jnp.dot(z_ref[...], w_ref[...], preferred_element_type=jnp.float32)


def out_proj_residual(z, w, x):
    m, k = z.shape
    n = w.shape[1]
    tm = 256 if m % 256 == 0 else m
    return pl.pallas_call(
        _out_proj_kernel,
        grid=(m // tm,),
        in_specs=[pl.BlockSpec((tm, k), lambda i: (i, 0)),
                  pl.BlockSpec((k, n), lambda i: (0, 0)),
                  pl.BlockSpec((tm, n), lambda i: (i, 0))],
        out_specs=pl.BlockSpec((tm, n), lambda i: (i, 0)),
        out_shape=jax.ShapeDtypeStruct((m, n), jnp.float32),
        compiler_params=pltpu.CompilerParams(dimension_semantics=("parallel",), vmem_limit_bytes=VMEM_LIMIT),
        name="out_proj_residual",
    )(z, w.astype(MXU_DTYPE), x)


def _mem_block_kernel(x_ref, g1_ref, g2_ref, wq_ref, kv_ref, wo_ref, x_out_ref, xn_out_ref):
    d = MEM_HEAD_DIM
    hd = MEM_HEADS * d
    x = x_ref[0]
    xn = _rms(x, g1_ref[...]).astype(wq_ref.dtype)
    q = jnp.dot(xn, wq_ref[...], preferred_element_type=jnp.float32).astype(wq_ref.dtype)
    outs = []
    for h in range(MEM_HEADS):
        k = kv_ref[0, :, h * d:(h + 1) * d].astype(wq_ref.dtype)
        v = kv_ref[0, :, hd + h * d:hd + (h + 1) * d].astype(wq_ref.dtype)
        s = _dot_nt(q[:, h * d:(h + 1) * d], k) * d ** -0.5
        e = jnp.exp(s - jnp.max(s, axis=1, keepdims=True))
        p = e / jnp.sum(e, axis=1, keepdims=True)
        outs.append(jnp.dot(p.astype(v.dtype), v, preferred_element_type=jnp.float32))
    o = jnp.concatenate(outs, axis=1).astype(wo_ref.dtype)
    x2 = x + jnp.dot(o, wo_ref[...], preferred_element_type=jnp.float32)
    x_out_ref[0] = x2
    xn_out_ref[0] = _rms(x2, g2_ref[...]).astype(xn_out_ref.dtype)


def mem_block(x, g1, g2, w_q, kv, w_o):
    b, t, dm = x.shape
    mt, kvw = kv.shape[1:]
    hd = w_q.shape[1]
    tm = 256 if t % 256 == 0 else t
    row_spec = pl.BlockSpec((1, tm, dm), lambda bi, i: (bi, i, 0))
    g_spec = pl.BlockSpec((1, dm), lambda bi, i: (0, 0))
    return pl.pallas_call(
        _mem_block_kernel,
        grid=(b, t // tm),
        in_specs=[row_spec, g_spec, g_spec,
                  pl.BlockSpec((dm, hd), lambda bi, i: (0, 0)),
                  pl.BlockSpec((1, mt, kvw), lambda bi, i: (bi, 0, 0)),
                  pl.BlockSpec((hd, dm), lambda bi, i: (0, 0))],
        out_specs=[row_spec, row_spec],
        out_shape=[jax.ShapeDtypeStruct((b, t, dm), jnp.float32), jax.ShapeDtypeStruct((b, t, dm), MXU_DTYPE)],
        compiler_params=pltpu.CompilerParams(
            dimension_semantics=("parallel", "parallel"), vmem_limit_bytes=VMEM_LIMIT),
        name="mem_block",
    )(x, g1.reshape(1, dm).astype(jnp.float32), g2.reshape(1, dm).astype(jnp.float32), w_q.astype(MXU_DTYPE),
      kv, w_o.astype(MXU_DTYPE))


def _ffn_up_kernel(x_ref, halo_ref, wg_ref, wu_ref, cwg_ref, cwu_ref, cbg_ref, cbu_ref, pg_ref, pu_ref,
                   h_ref, sg_ref, su_ref, *, nb, tm, t_real):
    i = pl.program_id(2)
    n_halo = halo_ref.shape[1]
    tn = wg_ref.shape[1]
    rows = nb * tm
    x = x_ref[...].reshape(rows, x_ref.shape[2])
    if nb == 1:
        x = jnp.concatenate([halo_ref[0], x], axis=0)
    row = _iota((nb, tm, 1), 1).reshape(rows, 1)
    last = (t_real - 1) // tm
    r_last = (t_real - 1) % tm
    spread = lambda p: jnp.broadcast_to(p, (nb, tm, tn)).reshape(rows, tn)

    def branch(w_ref, cw_ref, cb_ref, p_ref, s_ref):
        u = jnp.dot(x, w_ref[...], preferred_element_type=jnp.float32)
        p0, p1 = p_ref[:, 0:1, :], p_ref[:, 1:2, :]
        if nb == 1:
            uh, u = u[:n_halo], u[n_halo:]
            p0 = jnp.where(i == 0, p0, uh[n_halo - 2:n_halo - 1][None])
            p1 = jnp.where(i == 0, p1, uh[n_halo - 1:n_halo][None])
        p0, p1 = spread(p0), spread(p1)
        u1 = jnp.where(row == 0, p1, pltpu.roll(u, 1, 0))
        u2 = jnp.where(row == 0, p0, jnp.where(row == 1, p1, pltpu.roll(u, 2, 0)))

        @pl.when(i == last)
        def _():
            s_ref[...] = u.reshape(nb, tm, tn)[:, r_last - 1:r_last + 1, :]

        return cb_ref[...] + u2 * cw_ref[0:1, :] + u1 * cw_ref[1:2, :] + u * cw_ref[2:3, :]

    gate = branch(wg_ref, cwg_ref, cbg_ref, pg_ref, sg_ref)
    up = branch(wu_ref, cwu_ref, cbu_ref, pu_ref, su_ref)
    h_ref[...] = (jax.nn.silu(gate) * up).reshape(nb, tm, tn).astype(h_ref.dtype)


def ffn_up(xn, prev, w_up, conv_w, conv_b, t_real):
    b, t, dm = xn.shape
    f2 = w_up.shape[1]
    f = f2 // 2
    tm = 512 if t % 512 == 0 else t
    nb = 1 if t > tm else max(1, min(b, 512 // tm))
    while b % nb:
        nb -= 1
    tn = _tile(f, 512)
    nf = f // tn
    halo = min(BF16_ROWS, tm)
    hpt = tm // halo
    assert t_real >= 2 and (t_real - 1) % tm >= 1 and tm % SUBLANE == 0
    w_up = w_up.astype(MXU_DTYPE)
    conv_b = conv_b.reshape(1, f2)
    col = lambda off: (lambda j, bi, i: (0, j + off))
    st = lambda off: (lambda j, bi, i: (bi, 0, j + off))
    specs = [pl.BlockSpec((nb, tm, dm), lambda j, bi, i: (bi, i, 0)),
             pl.BlockSpec((1, halo, dm), lambda j, bi, i: (bi * nb, jnp.maximum(i * hpt - 1, 0), 0)),
             pl.BlockSpec((dm, tn), col(0)), pl.BlockSpec((dm, tn), col(nf)),
             pl.BlockSpec((CONV_WIDTH, tn), col(0)), pl.BlockSpec((CONV_WIDTH, tn), col(nf)),
             pl.BlockSpec((1, tn), col(0)), pl.BlockSpec((1, tn), col(nf)),
             pl.BlockSpec((nb, 2, tn), st(0)), pl.BlockSpec((nb, 2, tn), st(nf))]
    h, sg, su = pl.pallas_call(
        functools.partial(_ffn_up_kernel, nb=nb, tm=tm, t_real=t_real),
        grid=(nf, b // nb, t // tm),
        in_specs=specs,
        out_specs=[pl.BlockSpec((nb, tm, tn), lambda j, bi, i: (bi, i, j)),
                   pl.BlockSpec((nb, 2, tn), st(0)), pl.BlockSpec((nb, 2, tn), st(0))],
        out_shape=[jax.ShapeDtypeStruct((b, t, f), MXU_DTYPE), jax.ShapeDtypeStruct((b, 2, f), jnp.float32),
                   jax.ShapeDtypeStruct((b, 2, f), jnp.float32)],
        compiler_params=pltpu.CompilerParams(
            dimension_semantics=("parallel", "parallel", "arbitrary"), vmem_limit_bytes=VMEM_LIMIT),
        name="ffn_up",
    )(xn, xn, w_up, w_up, conv_w, conv_w, conv_b, conv_b, prev, prev)
    return h, jnp.concatenate([sg, su], axis=-1)


def _ffn_down_kernel(h_ref, w_ref, x_ref, g_ref, o_ref, acc_ref):
    @pl.when(pl.program_id(1) == 0)
    def _():
        acc_ref[...] = x_ref[...]

    acc_ref[...] += jnp.dot(h_ref[...], w_ref[...], preferred_element_type=jnp.float32)

    @pl.when(pl.program_id(1) == pl.num_programs(1) - 1)
    def _():
        o_ref[...] = _rms(acc_ref[...], g_ref[...])


def ffn_down_norm(h, w_down, x, g):
    m, f = h.shape
    dm = w_down.shape[1]
    tm = 512 if m % 512 == 0 else m
    tk = _tile(f, 2048)
    return pl.pallas_call(
        _ffn_down_kernel,
        grid=(m // tm, f // tk),
        in_specs=[pl.BlockSpec((tm, tk), lambda i, l: (i, l)),
                  pl.BlockSpec((tk, dm), lambda i, l: (l, 0)),
                  pl.BlockSpec((tm, dm), lambda i, l: (i, 0)),
                  pl.BlockSpec((1, dm), lambda i, l: (0, 0))],
        out_specs=pl.BlockSpec((tm, dm), lambda i, l: (i, 0)),
        out_shape=jax.ShapeDtypeStruct((m, dm), jnp.float32),
        scratch_shapes=[pltpu.VMEM((tm, dm), jnp.float32)],
        compiler_params=pltpu.CompilerParams(
            dimension_semantics=("parallel", "arbitrary"), vmem_limit_bytes=VMEM_LIMIT),
        name="ffn_down_norm",
    )(h, w_down.astype(MXU_DTYPE), x, g.reshape(1, dm).astype(jnp.float32))


def dense_tail(x, o_nsa, o_dsa, gm, kv_mem, prev_u, t_real, g_mem, g_ffn, g_final, w_oa, w_ob, w_o, w_mq, w_mo,
               w_up, conv_w, conv_b, w_down):
    b, t, dm = x.shape
    rows = lambda a: a.reshape(b * t, a.shape[-1])
    z = gated_merge(rows(o_nsa), rows(o_dsa), rows(gm), w_oa, w_ob)
    x1 = out_proj_residual(z, w_o, rows(x)).reshape(b, t, dm)
    x2, xn2 = mem_block(x1, g_mem, g_ffn, w_mq, kv_mem, w_mo)
    h, state = ffn_up(xn2, prev_u, w_up, conv_w, conv_b, t_real)
    y = ffn_down_norm(rows(h), w_down, rows(x2), g_final)
    return y.reshape(b, t, dm), state


def _dot_nt(a, b):
    return lax.dot_general(a, b, (((1,), (1,)), ((), ())), preferred_element_type=jnp.float32)


def _iota(shape, dim):
    return lax.broadcasted_iota(jnp.int32, shape, dim)


def _flash_init(rows, d):
    return (jnp.full((rows, 1), NEG_INF, jnp.float32), jnp.zeros((rows, 1), jnp.float32),
            jnp.zeros((rows, d), jnp.float32))


def _flash_step(carry, q, k, v, madd, nh, scale=None, kv_t=False):
    m, l, acc = carry
    s = jnp.dot(q, k, preferred_element_type=jnp.float32) if kv_t else _dot_nt(q, k)
    if scale is not None:
        s = s * scale
    r, kb = s.shape
    s = (s.reshape(nh, r // nh, kb) + madd[None]).reshape(r, kb)
    m_new = jnp.maximum(m, jnp.max(s, axis=1, keepdims=True))
    m_safe = jnp.where(m_new == NEG_INF, 0.0, m_new)
    p = jnp.exp(s - m_safe)
    alpha = jnp.exp(m - m_safe)
    l = alpha * l + jnp.sum(p, axis=1, keepdims=True)
    pv = _dot_nt(p.astype(v.dtype), v) if kv_t else jnp.dot(p.astype(v.dtype), v, preferred_element_type=jnp.float32)
    return m_new, l, alpha * acc + pv


def _flash_finish(carry):
    _, l, acc = carry
    return acc / jnp.maximum(l, 1e-30)


def _split_dot(x, m01):
    hi = x.astype(jnp.bfloat16)
    r1 = x - hi.astype(jnp.float32)
    mid = r1.astype(jnp.bfloat16)
    lo = (r1 - mid.astype(jnp.float32)).astype(jnp.bfloat16)
    dot = functools.partial(jnp.dot, preferred_element_type=jnp.float32)
    return dot(hi, m01) + dot(mid, m01) + dot(lo, m01)


def _nsa_group(g, i, qc_ref, qr_ref, gate_ref, kc_ref, vc_ref, ks_ref, vs_ref, kw_ref, vw_ref, mask_ref,
               *, tq, kb, qpos0, wbase, ns, n_sel, kv_t):
    nh, d = HPG_A, HEAD_DIM_A
    rows = nh * tq
    nc = kc_ref.shape[3]
    nsp = _round_up(ns, LANE)
    t0 = qpos0 + i * tq
    nj = (t0 + tq - 1) // kb + 1
    cols = slice(g * d, (g + 1) * d)

    def stack_heads(q_ref):
        return jnp.concatenate([q_ref[0, :, (g * nh + h) * d:(g * nh + h + 1) * d] for h in range(nh)], axis=0)

    qc = stack_heads(qc_ref)
    qr = stack_heads(qr_ref)
    tpos = t0 + _iota((tq, 1), 0)

    blk_last = _iota((1, nc), 1) * CMP_STRIDE + (CMP_BLOCK - 1)
    madd_c = jnp.where(blk_last <= tpos, 0.0, NEG_INF)
    s = _dot_nt(qc, kc_ref[0, 0, g]).reshape(nh, tq, nc) + madd_c[None]
    m = jnp.max(s, axis=2, keepdims=True)
    m = jnp.where(m == NEG_INF, 0.0, m)
    e = jnp.exp(s - m)
    p = e / jnp.maximum(jnp.sum(e, axis=2, keepdims=True), 1e-30)
    o_cmp = jnp.dot(p.reshape(rows, nc).astype(vc_ref.dtype), vc_ref[0, 0, g], preferred_element_type=jnp.float32)

    imp = jnp.sum(p, axis=0)
    c_id = _iota((nc, nsp), 0)
    m_id = _iota((nc, nsp), 1)
    overlap = (jnp.right_shift(c_id, CMP_PER_SLC_SHIFT) == m_id) | (c_id == m_id * CMP_PER_SLC - 1)
    score = _split_dot(imp, overlap.astype(jnp.bfloat16))
    blk = _iota((1, nsp), 1)
    cur = jnp.right_shift(tpos, SLC_SHIFT)
    forced = (blk == 0) | (blk == cur) | (blk == cur - 1)
    sc = jnp.where(forced, POS_INF, jnp.where(blk * SLC_BLOCK <= tpos, score, NEG_INF))
    if tq % LANE == 0 and nsp % LANE == 0:
        sc_t = sc.T
        blk_t = _iota((nsp, 1), 0)
        rank_t = jnp.zeros((nsp, tq), jnp.float32)
        for mp in range(ns):
            ref = sc_t[mp:mp + 1, :]
            beats = (ref > sc_t) | ((ref == sc_t) & (blk_t > mp))
            rank_t = rank_t + jnp.where(beats, 1.0, 0.0)
        sel = jnp.where(rank_t < n_sel, 1.0, 0.0).T.astype(jnp.bfloat16)
    else:
        rank = jnp.zeros((tq, nsp), jnp.float32)
        for mp in range(ns):
            col = sc[:, mp:mp + 1]
            beats = (col > sc) | ((col == sc) & (blk > mp))
            rank = rank + jnp.where(beats, 1.0, 0.0)
        sel = jnp.where(rank < n_sel, 1.0, 0.0).astype(jnp.bfloat16)

    def make_mask(j, _):
        kpos = j * kb + _iota((1, kb), 1)
        expand = (jnp.right_shift(j * kb + _iota((nsp, kb), 1), SLC_SHIFT) == _iota((nsp, kb), 0))
        hit = jnp.dot(sel, expand.astype(jnp.bfloat16), preferred_element_type=jnp.float32)
        mask_ref[j] = jnp.where((hit > 0.5) & (kpos <= tpos), 0.0, NEG_INF)
        return 0

    def slc_body(j, carry):
        off = pl.multiple_of(j * kb, kb)
        return _flash_step(carry, qr, ks_ref[0, pl.ds(off, kb), cols], vs_ref[0, pl.ds(off, kb), cols],
                           mask_ref[j], nh)

    if kv_t:
        sel_f = sel.astype(jnp.float32)
        low_half = _iota((tq, LANE), 1) < SLC_BLOCK
        pieces = []
        for v in range(kb // LANE):
            hit = jnp.where(low_half, sel_f[:, 2 * v:2 * v + 1], sel_f[:, 2 * v + 1:2 * v + 2])
            kpos = v * LANE + _iota((1, LANE), 1)
            pieces.append(jnp.where((hit > 0.5) & (kpos <= tpos), 0.0, NEG_INF))
        o_slc = _flash_finish(_flash_step(_flash_init(rows, d), qr, ks_ref[0, cols, :], vs_ref[0, cols, :],
                                          jnp.concatenate(pieces, axis=1), nh, kv_t=True))
    else:
        lax.fori_loop(0, nj, make_mask, 0)
        o_slc = _flash_finish(lax.fori_loop(0, nj, slc_body, _flash_init(rows, d)))

    wk = min(_round_up(WINDOW + tq, WIN_CHUNK), kw_ref.shape[1])
    first = jnp.maximum(t0 - (WINDOW - 1) - wbase, 0) // WIN_CHUNK * WIN_CHUNK
    off = pl.multiple_of(jnp.minimum(first, kw_ref.shape[1] - wk), WIN_CHUNK)
    dist = tpos - (wbase + off + _iota((1, wk), 1))
    madd_w = jnp.where((dist >= 0) & (dist < WINDOW), 0.0, NEG_INF)
    o_win = _flash_finish(_flash_step(_flash_init(rows, d), qr, kw_ref[0, pl.ds(off, wk), cols],
                                      vw_ref[0, pl.ds(off, wk), cols], madd_w, nh))

    gates = gate_ref[0]
    outs = []
    for h in range(nh):
        rs = slice(h * tq, (h + 1) * tq)
        c = g * nh + h
        outs.append(gates[:, c:c + 1] * o_cmp[rs] + gates[:, N_HEADS_A + c:N_HEADS_A + c + 1] * o_slc[rs]
                    + gates[:, 2 * N_HEADS_A + c:2 * N_HEADS_A + c + 1] * o_win[rs])
    return outs


def _nsa_kernel(qc_ref, qr_ref, gate_ref, kc_ref, vc_ref, ks_ref, vs_ref, kw_ref, vw_ref, o_ref, mask_ref, **kw):
    i = pl.program_id(1)
    outs = []
    for g in range(KV_GROUPS_A):
        outs += _nsa_group(g, i, qc_ref, qr_ref, gate_ref, kc_ref, vc_ref, ks_ref, vs_ref, kw_ref, vw_ref,
                           mask_ref, **kw)
    o_ref[0] = jnp.concatenate(outs, axis=1).astype(o_ref.dtype)


def nsa_attention(qc, qr, gates, kvc, ks, vs, kw, vw, *, tq, qpos0, wbase, n_keys, kv_t=False):
    b, t, hd = qc.shape
    l = ks.shape[2] if kv_t else ks.shape[1]
    lw = kw.shape[1]
    nc, d = kvc.shape[3:]
    kb = min(KEY_CHUNK, l) if tq >= TQ else l
    assert l % kb == 0 and t % tq == 0 and lw % WIN_CHUNK == 0 and (kb == l or not kv_t)
    assert (qpos0 + t - 1) // kb + 1 <= l // kb and (qpos0 + t - 1 - wbase) // WIN_CHUNK + 1 <= lw // WIN_CHUNK
    assert (qpos0 - wbase) % WIN_CHUNK == 0 and WIN_CHUNK % tq == 0 and tq > 1
    assert 2 * SLC_BLOCK == LANE or not kv_t
    ns = l // SLC_BLOCK
    n_sel = min(N_SELECT, -(-n_keys // SLC_BLOCK))
    q_spec = pl.BlockSpec((1, tq, hd), lambda bi, i: (bi, i, 0))
    kc_spec = pl.BlockSpec((1, 1, KV_GROUPS_A, nc, d), lambda bi, i: (0, bi, 0, 0, 0))
    vc_spec = pl.BlockSpec((1, 1, KV_GROUPS_A, nc, d), lambda bi, i: (1, bi, 0, 0, 0))
    k_spec = pl.BlockSpec((1, KV_A, l) if kv_t else (1, l, KV_A), lambda bi, i: (bi, 0, 0))
    w_spec = pl.BlockSpec((1, lw, KV_A), lambda bi, i: (bi, 0, 0))
    return pl.pallas_call(
        functools.partial(_nsa_kernel, tq=tq, kb=kb, qpos0=qpos0, wbase=wbase, ns=ns, n_sel=n_sel, kv_t=kv_t),
        grid=(b, t // tq),
        in_specs=[q_spec, q_spec, pl.BlockSpec((1, tq, 3 * N_HEADS_A), lambda bi, i: (bi, i, 0)),
                  kc_spec, vc_spec, k_spec, k_spec, w_spec, w_spec],
        out_specs=q_spec,
        out_shape=jax.ShapeDtypeStruct((b, t, hd), MXU_DTYPE),
        scratch_shapes=[pltpu.VMEM((l // kb, tq, kb), jnp.float32)],
        compiler_params=pltpu.CompilerParams(
            dimension_semantics=("parallel", "arbitrary"), vmem_limit_bytes=VMEM_LIMIT),
        name="nsa_attention",
    )(qc, qr, gates, kvc, kvc, ks, vs, kw, vw)


def _dsa_kernel(qb_ref, qi_ref, wi_ref, kb_ref, vb_ref, ki_ref, o_ref, score_ref,
                *, tq, tq_real, kb, qpos0, n_keep, kv_t):
    i = pl.program_id(1)
    nh, d = N_HEADS_B, HEAD_DIM_B
    t0 = qpos0 + i * tq
    nj = (t0 + tq - 1) // kb + 1
    tpos = t0 + _iota((tq, 1), 0)
    w = wi_ref[0]

    def idx_body(j, carry):
        lo, hi = carry
        off = pl.multiple_of(j * kb, kb)
        kidx = ki_ref[0] if kv_t else ki_ref[0, pl.ds(off, kb), :]
        acc = jnp.zeros((tq, kb), jnp.float32)
        heads = [qi_ref[0, :, h * IDX_DIM:(h + 1) * IDX_DIM] for h in range(IDX_HEADS)]
        if kv_t:
            dots_all = jnp.dot(jnp.concatenate(heads, axis=0), kidx, preferred_element_type=jnp.float32)
        for h in range(IDX_HEADS):
            dots = dots_all[h * tq:(h + 1) * tq] if kv_t else _dot_nt(heads[h], kidx)
            acc = acc + w[:, h:h + 1] * jnp.maximum(dots, 0.0)
        vis = (off + _iota((1, kb), 1)) <= tpos
        score_ref[j] = jnp.where(vis, acc, NEG_INF)
        lo = jnp.minimum(lo, jnp.min(jnp.where(vis, acc, POS_INF), axis=1, keepdims=True))
        hi = jnp.maximum(hi, jnp.max(jnp.where(vis, acc, NEG_INF), axis=1, keepdims=True))
        return lo, hi

    lo, hi = lax.fori_loop(0, nj, idx_body, (jnp.full((tq, 1), POS_INF, jnp.float32),
                                             jnp.full((tq, 1), NEG_INF, jnp.float32)))

    k = float(n_keep)
    n_vis = (tpos + 1).astype(jnp.float32)
    n_keys = score_ref.shape[0] * kb

    def reduce_scores(pred, pick, init, combine, lane_reduce):
        def body(j, acc):
            sc = score_ref[j]
            for c in range(kb // LANE):
                kpos = (j * kb + c * LANE + _iota((1, LANE), 1)).astype(jnp.float32)
                piece = sc[:, c * LANE:(c + 1) * LANE]
                acc = combine(acc, pick(pred(piece, kpos), piece))
            return acc

        return lane_reduce(lax.fori_loop(0, nj, body, jnp.full((tq, LANE), init, jnp.float32)),
                           axis=1, keepdims=True)

    def count_where(pred):
        return reduce_scores(pred, lambda m, _: jnp.where(m, 1.0, 0.0), 0.0, jnp.add, jnp.sum)

    def min_where(pred):
        return reduce_scores(pred, lambda m, x: jnp.where(m, x, POS_INF), POS_INF, jnp.minimum, jnp.min)

    real = _iota((tq, 1), 0) < tq_real

    def any_row(flag):
        return jnp.max(jnp.where(real & flag, 1.0, 0.0)) > 0.0

    def bisect(state):
        it, lo, hi, cnt_lo = state
        mid = 0.5 * (lo + hi)
        mid_b = jnp.broadcast_to(mid, (tq, LANE))
        cnt = count_where(lambda x, _: x >= mid_b)
        ge = cnt >= k
        return it + 1, jnp.where(ge, mid, lo), jnp.where(ge, hi, mid), jnp.where(ge, cnt, cnt_lo)

    _, thr, _, cnt_lo = lax.while_loop(
        lambda st: (st[0] < BISECT_ITERS) & any_row((st[3] != k) & (n_vis > k)), bisect,
        (jnp.int32(0), lo, hi, n_vis))

    def break_ties():
        def above(v):
            v_b = jnp.broadcast_to(v, (tq, LANE))
            return count_where(lambda x, _: x > v_b)

        def strip(state):
            it, v, c_gt = state
            v_b = jnp.broadcast_to(v, (tq, LANE))
            v_next = jnp.where(c_gt >= k, min_where(lambda x, _: x > v_b), v)
            return it + 1, v_next, above(v_next)

        thr_b = jnp.broadcast_to(thr, (tq, LANE))
        v0 = min_where(lambda x, _: x >= thr_b)
        _, v, c_gt = lax.while_loop(lambda st: (st[0] < TIE_STRIP_ITERS) & any_row(st[2] >= k), strip,
                                    (jnp.int32(0), v0, above(v0)))
        need = k - c_gt
        v_b = jnp.broadcast_to(v, (tq, LANE))

        def narrow(_, bounds):
            j_lo, j_hi = bounds
            mid = jnp.floor(0.5 * (j_lo + j_hi))
            ge = count_where(lambda x, kpos: (x == v_b) & (kpos <= mid)) >= need
            return jnp.where(ge, j_lo, mid), jnp.where(ge, mid, j_hi)

        _, j_max = lax.fori_loop(0, n_keys.bit_length(), narrow,
                                 (jnp.full((tq, 1), -1.0, jnp.float32), jnp.full((tq, 1), n_keys - 1.0, jnp.float32)))
        return v, j_max

    thr, j_max = lax.cond(any_row((cnt_lo > k) & (n_vis > k)), break_ties,
                          lambda: (thr, jnp.full((tq, 1), float(n_keys), jnp.float32)))

    q = jnp.concatenate([qb_ref[0, :, h * d:(h + 1) * d] for h in range(nh)], axis=0)

    def att_body(j, carry):
        off = pl.multiple_of(j * kb, kb)
        sc = score_ref[j]
        kpos = (off + _iota((1, kb), 1)).astype(jnp.float32)
        madd = jnp.where((sc > thr) | ((sc == thr) & (kpos <= j_max)), 0.0, NEG_INF)
        if kv_t:
            return _flash_step(carry, q, kb_ref[0], vb_ref[0], madd, nh, scale=d ** -0.5, kv_t=True)
        return _flash_step(carry, q, kb_ref[0, pl.ds(off, kb), :], vb_ref[0, pl.ds(off, kb), :], madd, nh,
                           scale=d ** -0.5)

    o = _flash_finish(lax.fori_loop(0, nj, att_body, _flash_init(nh * tq, d)))
    o_ref[0] = jnp.concatenate([o[h * tq:(h + 1) * tq] for h in range(nh)], axis=1).astype(o_ref.dtype)


def dsa_attention(qb, qi, wi, kb_, vb, ki, *, tq, qpos0, n_keep, kv_t=False, tq_real=None):
    b, t, hd = qb.shape
    l = kb_.shape[2] if kv_t else kb_.shape[1]
    kb = min(KEY_CHUNK, l) if tq >= TQ else l
    assert l % kb == 0 and t % tq == 0 and (qpos0 + t - 1) // kb + 1 <= l // kb and (kb == l or not kv_t)
    kv_spec = lambda width: pl.BlockSpec((1, width, l) if kv_t else (1, l, width), lambda bi, i: (bi, 0, 0))
    return pl.pallas_call(
        functools.partial(_dsa_kernel, tq=tq, tq_real=tq if tq_real is None else tq_real, kb=kb, qpos0=qpos0,
                          n_keep=n_keep, kv_t=kv_t),
        grid=(b, t // tq),
        in_specs=[pl.BlockSpec((1, tq, hd), lambda bi, i: (bi, i, 0)),
                  pl.BlockSpec((1, tq, IDX_HEADS * IDX_DIM), lambda bi, i: (bi, i, 0)),
                  pl.BlockSpec((1, tq, IDX_HEADS), lambda bi, i: (bi, i, 0)),
                  kv_spec(HEAD_DIM_B), kv_spec(HEAD_DIM_B), kv_spec(IDX_DIM)],
        out_specs=pl.BlockSpec((1, tq, hd), lambda bi, i: (bi, i, 0)),
        out_shape=jax.ShapeDtypeStruct((b, t, hd), MXU_DTYPE),
        scratch_shapes=[pltpu.VMEM((l // kb, tq, kb), jnp.float32)],
        compiler_params=pltpu.CompilerParams(
            dimension_semantics=("parallel", "arbitrary"), vmem_limit_bytes=VMEM_LIMIT),
        name="dsa_attention",
    )(qb, qi, wi, kb_, vb, ki)


def _page_maps(n_pages, pp):
    n_steps = n_pages // pp

    def page_map(r):
        return lambda b, s, pt: (pt[b * n_pages + jnp.minimum(s, n_steps - 1) * pp + r], 0, 0)

    return n_steps, page_map


def _chunk_rows(tok_ref, sec, row0, n):
    d = HEAD_DIM_A
    first_half = _iota((n, KV_A), 1) < d
    pieces = [[] for _ in range(KV_GROUPS_A)]
    for j in range(0, CMP_STRIDE, 2):
        a, b = [tok_ref[sec, pl.ds(row0 + jj, n, stride=CMP_STRIDE), :] for jj in (j, j + 1)]
        pieces[0].append(jnp.where(first_half, a, pltpu.roll(b, d, 1)))
        pieces[1].append(jnp.where(first_half, pltpu.roll(a, d, 1), b))
    return [jnp.concatenate(p, axis=1) for p in pieces]


def _nsa_gather_kernel(pt_ref, *refs, pp, n_steps, rows):
    del pt_ref
    pages, (tail_tok_ref, tail_t_ref, pe_ref) = refs[:pp], refs[pp:pp + 3]
    zt_ref, zb_ref, ks_ref, vs_ref, tok_ref = refs[pp + 3:]
    is_tail = pl.program_id(1) == n_steps
    z = [[[] for _ in range(KV_GROUPS_A)] for _ in range(2)]
    for r in range(pp):
        cs = slice(r * rows, (r + 1) * rows)
        ks_ref[0, :, cs] = jnp.where(is_tail, tail_t_ref[0, :KV_A, cs],
                                     pages[r][0, 2 * KV_A:3 * KV_A, :]).astype(ks_ref.dtype)
        vs_ref[0, :, cs] = jnp.where(is_tail, tail_t_ref[0, KV_A:, cs],
                                     pages[r][0, 3 * KV_A:, :]).astype(vs_ref.dtype)
        for sec in range(2):
            tok_ref[sec, cs, :] = jnp.where(is_tail, tail_tok_ref[0, sec, cs, :],
                                            pages[r][0, sec * KV_A:(sec + 1) * KV_A, :].T)
            for g, zg in enumerate(_chunk_rows(tok_ref, sec, r * rows, rows // CMP_STRIDE)):
                z[sec][g].append(zg)
    for sec in range(2):
        for g in range(KV_GROUPS_A):
            zf = jnp.concatenate(z[sec][g], axis=0)
            zt_ref[sec, 0, g] = (zf + pe_ref[sec, 0]).astype(zt_ref.dtype)
            zb_ref[sec, 0, g] = (zf + pe_ref[sec, 1]).astype(zb_ref.dtype)


def nsa_gather(cache_t, page_table, new_rows, cmp_pe):
    db, n_pages = page_table.shape
    width, rows = cache_t.shape[1:]
    pp = PAGES_PER_STEP
    n_steps, page_map = _page_maps(n_pages, pp)
    l = (n_steps + 1) * pp * rows
    cps = pp * rows // CMP_STRIDE
    flat = CMP_STRIDE * HEAD_DIM_A
    pe = cmp_pe.reshape(2, 2, 1, flat).astype(jnp.float32)
    tail = pad_rows(new_rows, pp * rows)
    tail_tok = tail[:, :, :2 * KV_A].reshape(db, pp * rows, 2, KV_A).swapaxes(1, 2)
    tail_t = tail[:, :, 2 * KV_A:].swapaxes(1, 2)
    z_spec = pl.BlockSpec((2, 1, KV_GROUPS_A, cps, flat), lambda b, s, pt: (0, b, 0, s, 0))
    r_spec = pl.BlockSpec((1, KV_A, pp * rows), lambda b, s, pt: (b, 0, s))
    z_shape = jax.ShapeDtypeStruct((2, db, KV_GROUPS_A, l // CMP_STRIDE, flat), MXU_DTYPE)
    r_shape = jax.ShapeDtypeStruct((db, KV_A, l), MXU_DTYPE)
    return pl.pallas_call(
        functools.partial(_nsa_gather_kernel, pp=pp, n_steps=n_steps, rows=rows),
        grid_spec=pltpu.PrefetchScalarGridSpec(
            num_scalar_prefetch=1,
            grid=(db, n_steps + 1),
            in_specs=[pl.BlockSpec((1, width, rows), page_map(r)) for r in range(pp)]
            + [pl.BlockSpec((1, 2, pp * rows, KV_A), lambda b, s, pt: (b, 0, 0, 0)),
               pl.BlockSpec((1, 2 * KV_A, pp * rows), lambda b, s, pt: (b, 0, 0)),
               pl.BlockSpec((2, 2, 1, flat), lambda b, s, pt: (0, 0, 0, 0))],
            out_specs=[z_spec, z_spec, r_spec, r_spec],
            scratch_shapes=[pltpu.VMEM((2, pp * rows, KV_A), jnp.float32)]),
        out_shape=[z_shape, z_shape, r_shape, r_shape],
        compiler_params=pltpu.CompilerParams(
            dimension_semantics=("parallel", "arbitrary"), vmem_limit_bytes=VMEM_LIMIT),
        name="nsa_gather",
    )(page_table.reshape(-1), *([cache_t] * pp), tail_tok, tail_t, pe)


def _dsa_gather_kernel(pt_ref, *refs, pp, n_steps, rows):
    del pt_ref
    pages, tail_ref, (k_ref, v_ref, i_ref) = refs[:pp], refs[pp], refs[pp + 1:]
    is_tail = pl.program_id(1) == n_steps
    d = HEAD_DIM_B
    for r in range(pp):
        cs = slice(r * rows, (r + 1) * rows)
        x = jnp.where(is_tail, tail_ref[0, :, cs], pages[r][0])
        k_ref[0, :, cs] = x[:d].astype(k_ref.dtype)
        v_ref[0, :, cs] = x[d:2 * d].astype(v_ref.dtype)
        i_ref[0, :, cs] = x[2 * d:].astype(i_ref.dtype)


def dsa_gather(cache_t, page_table, new_rows):
    db, n_pages = page_table.shape
    width, rows = cache_t.shape[1:]
    pp = PAGES_PER_STEP
    n_steps, page_map = _page_maps(n_pages, pp)
    l = (n_steps + 1) * pp * rows
    tail_t = pad_rows(new_rows, pp * rows).swapaxes(1, 2)
    out_spec = lambda w: pl.BlockSpec((1, w, pp * rows), lambda b, s, pt: (b, 0, s))
    widths = (HEAD_DIM_B, HEAD_DIM_B, IDX_DIM)
    return pl.pallas_call(
        functools.partial(_dsa_gather_kernel, pp=pp, n_steps=n_steps, rows=rows),
        grid_spec=pltpu.PrefetchScalarGridSpec(
            num_scalar_prefetch=1,
            grid=(db, n_steps + 1),
            in_specs=[pl.BlockSpec((1, width, rows), page_map(r)) for r in range(pp)]
            + [pl.BlockSpec((1, width, pp * rows), lambda b, s, pt: (b, 0, 0))],
            out_specs=[out_spec(w) for w in widths]),
        out_shape=[jax.ShapeDtypeStruct((db, w, l), MXU_DTYPE) for w in widths],
        compiler_params=pltpu.CompilerParams(
            dimension_semantics=("parallel", "arbitrary"), vmem_limit_bytes=VMEM_LIMIT),
        name="dsa_gather",
    )(page_table.reshape(-1), *([cache_t] * pp), tail_t)


def _compress_kernel(zt_ref, zb_ref, pe_ref, w1t_ref, w1b_ref, b1_ref, w2_ref, o_ref, ab_ref, *, ncp):
    dot = functools.partial(jnp.dot, preferred_element_type=jnp.float32)
    ch = zb_ref.shape[2]
    at = dot(zt_ref[0, 0], w1t_ref[0])
    ab_ref[:ch] = dot(zb_ref[0, 0], w1b_ref[0])
    pe_rows = jnp.broadcast_to(pe_ref[0, 1], (SUBLANE, pe_ref.shape[3])).astype(zb_ref.dtype)
    ab_ref[ch:] = dot(pe_rows, w1b_ref[0])
    h = jax.nn.gelu(at[:ncp] + ab_ref[pl.ds(1, ncp), :] + b1_ref[0])
    o_ref[0, 0, :ncp] = dot(h.astype(w2_ref.dtype), w2_ref[0]).astype(o_ref.dtype)
    if o_ref.shape[2] > ncp:
        o_ref[0, 0, ncp:] = jnp.zeros((o_ref.shape[2] - ncp, o_ref.shape[3]), o_ref.dtype)


def compress(zt, zb, cmp_pe, w1, b1, w2, n_keys):
    ncp = _round_up(-(-n_keys // CMP_STRIDE), BF16_ROWS)
    ncl = ncp if ncp <= LANE else _round_up(ncp, LANE)
    _, nb, ch, kdim = zt.shape
    hid = w1.shape[-1]
    d = w2.shape[-1]
    assert ch + SUBLANE >= ncp + 1 and ch % SUBLANE == 0
    w1 = w1.astype(MXU_DTYPE)
    pe = cmp_pe.reshape(2, 2, 1, kdim).astype(jnp.float32)
    z_spec = pl.BlockSpec((1, 1, ch, kdim), lambda s, n: (s, n, 0, 0))
    return pl.pallas_call(
        functools.partial(_compress_kernel, ncp=ncp),
        grid=(2, nb),
        in_specs=[z_spec, z_spec,
                  pl.BlockSpec((1, 2, 1, kdim), lambda s, n: (s, 0, 0, 0)),
                  pl.BlockSpec((1, kdim, hid), lambda s, n: (s, 0, 0)),
                  pl.BlockSpec((1, kdim, hid), lambda s, n: (s, 1, 0)),
                  pl.BlockSpec((1, 1, hid), lambda s, n: (s, 0, 0)),
                  pl.BlockSpec((1, hid, d), lambda s, n: (s, 0, 0))],
        out_specs=pl.BlockSpec((1, 1, ncl, d), lambda s, n: (s, n, 0, 0)),
        out_shape=jax.ShapeDtypeStruct((2, nb, ncl, d), MXU_DTYPE),
        scratch_shapes=[pltpu.VMEM((ch + SUBLANE, hid), jnp.float32)],
        compiler_params=pltpu.CompilerParams(
            dimension_semantics=("parallel", "parallel"), vmem_limit_bytes=VMEM_LIMIT),
        name="compress",
    )(zt, zb, pe, w1, w1, b1.reshape(2, 1, hid).astype(jnp.float32), w2.astype(MXU_DTYPE))


def pad_rows(a, n):
    return jnp.pad(a, [(0, 0), (0, n - a.shape[1])] + [(0, 0)] * (a.ndim - 2))


def _rope_tables(pos, d):
    half = d // ROT_FRACTION // 2
    inv = ROPE_THETA ** (-jnp.arange(half, dtype=jnp.float32) / half)
    ang = pos.astype(jnp.float32)[:, None] * inv[None, :]
    lane = jnp.arange(LANE) % d
    cos = jnp.cos(ang)[:, lane % half]
    sin = jnp.sin(ang)[:, lane % half]
    one, zero = jnp.ones_like(cos), jnp.zeros_like(cos)
    c = jnp.where(lane < 2 * half, cos, one)
    sa = jnp.where((lane >= half) & (lane < 2 * half), sin, zero)
    sb = jnp.where(lane < half, -sin, zero)
    return jnp.stack([c, sa, sb])


def _rope(x, t_ref, half):
    c, sa, sb = t_ref[0], t_ref[1], t_ref[2]
    outs = []
    for j in range(x.shape[1] // LANE):
        xs = x[:, j * LANE:(j + 1) * LANE]
        outs.append(xs * c + pltpu.roll(xs, half, 1) * sa + pltpu.roll(xs, LANE - half, 1) * sb)
    return outs[0] if len(outs) == 1 else jnp.concatenate(outs, axis=1)


_QA = N_HEADS_A * HEAD_DIM_A
_KVA = 6 * KV_A
_QB = N_HEADS_B * HEAD_DIM_B
_KVB = 2 * HEAD_DIM_B
_QI = IDX_HEADS * IDX_DIM
_MISC = IDX_DIM + IDX_HEADS + 3 * N_HEADS_A
PREP_WIDTH = _QA + _KVA + _QB + _KVB + _QI + _MISC
assert _MISC == LANE


def _prep_kernel(x_ref, t64_ref, t128_ref, pe_ref, qc_ref, qr_ref, qb_ref, qi_ref, nsa_ref, win_ref, dsa_ref,
                 ks_ref, vs_ref, kw_ref, vw_ref, kb_ref, vb_ref, ki_ref, wi_ref, gate_ref, *z_refs, tm):
    x = x_ref[...]
    o = 0
    qa = x[:, o:o + _QA]; o += _QA
    kva = x[:, o:o + _KVA]; o += _KVA
    qb = x[:, o:o + _QB]; o += _QB
    kvb = x[:, o:o + _KVB]; o += _KVB
    qi = x[:, o:o + _QI]; o += _QI
    misc = x[:, o:o + _MISC]
    h64, h128 = HEAD_DIM_A // ROT_FRACTION // 2, HEAD_DIM_B // ROT_FRACTION // 2
    sec = lambda i: kva[:, i * KV_A:(i + 1) * KV_A]
    mx = lambda a: a.astype(qc_ref.dtype)
    qc_ref[...] = mx(qa * HEAD_DIM_A ** -0.5)
    qr_ref[...] = mx(_rope(qa, t64_ref, h64) * HEAD_DIM_A ** -0.5)
    k_slc, k_win = _rope(sec(2), t64_ref, h64), _rope(sec(4), t64_ref, h64)
    nsa_ref[:, :2 * KV_A] = kva[:, :2 * KV_A]
    nsa_ref[:, 2 * KV_A:3 * KV_A] = k_slc
    nsa_ref[:, 3 * KV_A:] = sec(3)
    win_ref[:, :KV_A] = k_win
    win_ref[:, KV_A:] = sec(5)
    ks_ref[...], vs_ref[...], kw_ref[...], vw_ref[...] = mx(k_slc), mx(sec(3)), mx(k_win), mx(sec(5))
    qb_ref[...] = mx(_rope(qb, t128_ref, h128))
    k_b, v_b = _rope(kvb[:, :HEAD_DIM_B], t128_ref, h128), kvb[:, HEAD_DIM_B:]
    k_idx = _rope(misc, t64_ref, h64)[:, :IDX_DIM]
    dsa_ref[:, :HEAD_DIM_B] = k_b
    dsa_ref[:, HEAD_DIM_B:2 * HEAD_DIM_B] = v_b
    dsa_ref[:, 2 * HEAD_DIM_B:] = k_idx
    kb_ref[...], vb_ref[...], ki_ref[...] = mx(k_b), mx(v_b), mx(k_idx)
    qi_ref[...] = mx(_rope(qi, t64_ref, h64) * IDX_DIM ** -0.5)
    wi_ref[...] = misc[:, IDX_DIM:IDX_DIM + IDX_HEADS] * IDX_HEADS ** -0.5
    gate_ref[...] = jax.nn.sigmoid(misc[:, IDX_DIM + IDX_HEADS:])
    if z_refs:
        zt_ref, zb_ref, tok_ref = z_refs
        for s_ in range(2):
            tok_ref[s_] = sec(s_)
            for g, zg in enumerate(_chunk_rows(tok_ref, s_, 0, tm // CMP_STRIDE)):
                zt_ref[s_, 0, g] = (zg + pe_ref[s_, 0]).astype(zt_ref.dtype)
                zb_ref[s_, 0, g] = (zg + pe_ref[s_, 1]).astype(zb_ref.dtype)


def prep(proj, pos, cmp_pe, with_chunks):
    b, t, width = proj.shape
    assert width == PREP_WIDTH
    m = b * t
    tm = 256 if t % 256 == 0 else m
    assert m % tm == 0 and t % tm in (0, t)
    nt = max(t // tm, 1)
    flat = CMP_STRIDE * HEAD_DIM_A
    names = ['qc', 'qr', 'qb', 'qi', 'nsa_rows', 'win_rows', 'dsa_rows', 'ks', 'vs', 'kw', 'vw', 'kb', 'vb', 'ki',
             'wi', 'gates']
    widths = [_QA, _QA, _QB, _QI, NSA_SECTIONS * KV_A, 2 * KV_A, DSA_ROW, KV_A, KV_A, KV_A, KV_A, HEAD_DIM_B,
              HEAD_DIM_B, IDX_DIM, IDX_HEADS, 3 * N_HEADS_A]
    dtypes = [MXU_DTYPE] * 4 + [jnp.float32] * 3 + [MXU_DTYPE] * 7 + [jnp.float32] * 2
    row = lambda w: pl.BlockSpec((tm, w), lambda i: (i, 0))
    out_specs = [row(w) for w in widths]
    out_shape = [jax.ShapeDtypeStruct((m, w), dt) for w, dt in zip(widths, dtypes)]
    scratch = []
    if with_chunks:
        assert tm % CMP_STRIDE == 0 and t % tm == 0
        z_spec = pl.BlockSpec((2, 1, KV_GROUPS_A, tm // CMP_STRIDE, flat), lambda i: (0, i // nt, 0, i % nt, 0))
        z_shape = jax.ShapeDtypeStruct((2, b, KV_GROUPS_A, t // CMP_STRIDE, flat), MXU_DTYPE)
        out_specs += [z_spec, z_spec]
        out_shape += [z_shape, z_shape]
        names += ['zt', 'zb']
        scratch = [pltpu.VMEM((2, tm, KV_A), jnp.float32)]
    t_spec = pl.BlockSpec((3, tm, LANE), lambda i: (0, i, 0))
    outs = pl.pallas_call(
        functools.partial(_prep_kernel, tm=tm),
        grid=(m // tm,),
        in_specs=[row(width), t_spec, t_spec, pl.BlockSpec((2, 2, 1, flat), lambda i: (0, 0, 0, 0))],
        out_specs=out_specs,
        out_shape=out_shape,
        scratch_shapes=scratch,
        compiler_params=pltpu.CompilerParams(dimension_semantics=("parallel",), vmem_limit_bytes=VMEM_LIMIT),
        name="prep",
    )(proj.reshape(m, width), _rope_tables(pos, HEAD_DIM_A), _rope_tables(pos, HEAD_DIM_B),
      cmp_pe.reshape(2, 2, 1, flat).astype(jnp.float32))
    return {n: (o if o.ndim > 2 else o.reshape(b, t, o.shape[-1])) for n, o in zip(names, outs)}


def project(x, g, w_in, sizes):
    starts = [sum(sizes[:i]) for i in range(len(sizes))]
    qa, kva, ga, qb, kvb, qi, ki, wi, gm = [slice(o, o + n) for o, n in zip(starts, sizes)]
    w_bf = w_in.astype(MXU_DTYPE)
    w_prep = jnp.concatenate([w_bf[:, c] for c in (qa, kva, qb, kvb, qi, ki, wi, ga)], axis=1)
    return norm_matmul(x, g, w_prep), norm_matmul(x, g, w_bf[:, gm])


def mixer_prompt(x, g, w_in, cmp_pe, cmp_w1, cmp_b1, cmp_w2, sizes):
    b, s, _ = x.shape
    proj, gm = project(x, g, w_in, sizes)
    p = prep(proj, jnp.tile(jnp.arange(s, dtype=jnp.int32), b), cmp_pe, True)
    merge_bg = lambda a: a.reshape((2, b * KV_GROUPS_A) + a.shape[3:])
    kvc = compress(merge_bg(p['zt']), merge_bg(p['zb']), cmp_pe, cmp_w1, cmp_b1, cmp_w2, s)
    kvc = kvc.reshape(2, b, KV_GROUPS_A, kvc.shape[2], HEAD_DIM_A)
    o_nsa = nsa_attention(p['qc'], p['qr'], p['gates'], kvc, p['ks'], p['vs'], p['kw'], p['vw'],
                          tq=min(TQ, s), qpos0=0, wbase=0, n_keys=s)
    o_dsa = dsa_attention(p['qb'], p['qi'], p['wi'], p['kb'], p['vb'], p['ki'],
                          tq=min(TQ, s), qpos0=0, n_keep=min(DSA_TOPK, s // 4))
    nsa_rows = p['nsa_rows'].reshape(b, s, NSA_SECTIONS, KV_GROUPS_A, HEAD_DIM_A)
    win_state = p['win_rows'].reshape(b, s, 2, KV_GROUPS_A, HEAD_DIM_A)[:, -min(WINDOW, s):]
    return (o_nsa, o_dsa, gm), nsa_rows, win_state, p['dsa_rows']


def mixer_sample(x, g, cache_nsa, win_buf, cache_dsa, page_table, w_in, cmp_pe, cmp_w1, cmp_b1, cmp_w2, sizes):
    b, t, _ = x.shape
    page = cache_nsa.shape[1]
    past_len = page_table.shape[1] * page
    n_keys = past_len + t
    proj, gm = project(x, g, w_in, sizes)
    p = prep(proj, jnp.tile(past_len + jnp.arange(t, dtype=jnp.int32), b), cmp_pe, False)
    pad = lambda a: pad_rows(a, TQ_STEP)

    cache_nsa_t = jnp.transpose(cache_nsa, (0, 2, 3, 4, 1)).reshape(cache_nsa.shape[0], NSA_SECTIONS * KV_A, page)
    zt, zb, ks, vs = nsa_gather(cache_nsa_t, page_table, p['nsa_rows'], cmp_pe)
    merge_bg = lambda a: a.reshape((2, b * KV_GROUPS_A) + a.shape[3:])
    kvc = compress(merge_bg(zt), merge_bg(zb), cmp_pe, cmp_w1, cmp_b1, cmp_w2, n_keys)
    kvc = kvc.reshape(2, b, KV_GROUPS_A, kvc.shape[2], HEAD_DIM_A)
    w_len = win_buf.shape[1]
    win_new = p['win_rows'].reshape(b, t, 2, KV_GROUPS_A, HEAD_DIM_A)
    win_all = jnp.concatenate([win_buf, win_new], axis=1)
    win_pad = pad_rows(win_all, w_len + WIN_CHUNK).astype(MXU_DTYPE)
    kw = win_pad[:, :, 0].reshape(b, w_len + WIN_CHUNK, KV_A)
    vw = win_pad[:, :, 1].reshape(b, w_len + WIN_CHUNK, KV_A)
    o_nsa = nsa_attention(pad(p['qc']), pad(p['qr']), pad(p['gates']), kvc, ks, vs, kw, vw,
                          tq=TQ_STEP, qpos0=past_len, wbase=past_len - w_len, n_keys=n_keys, kv_t=True)

    kb_, vb, ki = dsa_gather(jnp.swapaxes(cache_dsa, 1, 2), page_table, p['dsa_rows'])
    o_dsa = dsa_attention(pad(p['qb']), pad(p['qi']), pad(p['wi']), kb_, vb, ki, tq=TQ_STEP, qpos0=past_len,
                          n_keep=min(DSA_TOPK, n_keys // 4), kv_t=True, tq_real=t)
    nsa_rows = p['nsa_rows'].reshape(b, t, NSA_SECTIONS, KV_GROUPS_A, HEAD_DIM_A)
    return (o_nsa, o_dsa, pad(gm)), nsa_rows, win_all[:, -w_len:], p['dsa_rows']


def kernel(x_prompt, x_sample, mem_prompt, cache_nsa_kv, state_nsa_win, cache_dsa_kv, cache_mem_kv, state_conv,
           page_table, norm_g, w_in, cmp_pe, cmp_w1, cmp_b1, cmp_w2, w_out_a, w_out_b, w_out, w_mem_q, w_mem_kv,
           w_mem_out, w_up, conv_w, conv_b, w_down, final_g):
    depth = w_in.shape[0]
    d_model = x_prompt.shape[-1]
    d_ff = w_down.shape[1]
    assert CONV_WIDTH == 3
    sizes = (N_HEADS_A * HEAD_DIM_A, 6 * KV_A, 3 * N_HEADS_A, N_HEADS_B * HEAD_DIM_B, 2 * HEAD_DIM_B,
             IDX_HEADS * IDX_DIM, IDX_DIM, IDX_HEADS, 2 * d_model)
    xp, xs = x_prompt, pad_rows(x_sample, TQ_STEP)
    t_step = x_sample.shape[1]
    nsa_p, nsa_s, win_p, win_s, dsa_p, dsa_s, mem_p, conv_p, conv_s = [], [], [], [], [], [], [], [], []
    for l in range(depth):
        assert l == depth - 1, "the fused FFN epilogue applies the final norm"
        branches_p, a, bwin, c = mixer_prompt(xp, norm_g[l, 0], w_in[l], cmp_pe[l], cmp_w1[l], cmp_b1[l], cmp_w2[l],
                                              sizes)
        nsa_p.append(a); win_p.append(bwin); dsa_p.append(c)
        branches_s, a, bwin, c = mixer_sample(xs[:, :t_step], norm_g[l, 0], cache_nsa_kv[l], state_nsa_win[l],
                                              cache_dsa_kv[l], page_table, w_in[l], cmp_pe[l], cmp_w1[l], cmp_b1[l],
                                              cmp_w2[l], sizes)
        nsa_s.append(a); win_s.append(bwin); dsa_s.append(c)
        kv_p = norm_matmul(mem_prompt, norm_g[l, 2], w_mem_kv[l])
        mem_p.append(kv_p.reshape(kv_p.shape[:2] + (2, MEM_HEADS, MEM_HEAD_DIM)))
        kv_s = cache_mem_kv[l].reshape(cache_mem_kv.shape[1:3] + (-1,))
        weights = (norm_g[l, 1], norm_g[l, 3], final_g, w_out_a[l], w_out_b[l], w_out[l], w_mem_q[l], w_mem_out[l],
                   w_up[l], conv_w[l], conv_b[l], w_down[l])
        xp, cp = dense_tail(xp, *branches_p, kv_p, jnp.zeros((xp.shape[0], CONV_WIDTH - 1, 2 * d_ff), xp.dtype),
                            xp.shape[1], *weights)
        xs, cs = dense_tail(xs, *branches_s, kv_s, state_conv[l], t_step, *weights)
        conv_p.append(cp); conv_s.append(cs)
    y_prompt, y_sample = xp, xs[:, :t_step]
    return (y_prompt, y_sample, jnp.stack(nsa_p), jnp.stack(nsa_s), jnp.stack(win_p), jnp.stack(win_s),
            jnp.stack(dsa_p), jnp.stack(dsa_s), jnp.stack(mem_p), jnp.stack(conv_p), jnp.stack(conv_s))
```

```python
import functools

import jax
import jax.numpy as jnp
from jax import lax
from jax.experimental import pallas as pl
from jax.experimental.pallas import tpu as pltpu

N_HEADS_A = 16
HEAD_DIM_A = 64
KV_GROUPS_A = 2
CMP_BLOCK = 32
CMP_STRIDE = 16
SLC_BLOCK = 64
N_SELECT = 16
WINDOW = 512
N_HEADS_B = 8
HEAD_DIM_B = 128
IDX_HEADS = 16
IDX_DIM = 64
DSA_TOPK = 256
MEM_HEADS = 4
MEM_HEAD_DIM = 128
CONV_WIDTH = 3
ROPE_THETA = 500000.0
ROT_FRACTION = 4
EPS = 1e-6
KV_A = KV_GROUPS_A * HEAD_DIM_A
CMP_PER_SLC = SLC_BLOCK // CMP_STRIDE
HPG_A = N_HEADS_A // KV_GROUPS_A
DSA_ROW = 2 * HEAD_DIM_B + IDX_DIM
NSA_SECTIONS = 4

LANE = 128
SUBLANE = 8
BF16_ROWS = 16
VMEM_LIMIT = 48 * 1024 * 1024

NEG_INF = float('-inf')
POS_INF = float('inf')
MXU_DTYPE = jnp.bfloat16
TQ = 128
TQ_STEP = BF16_ROWS
KEY_CHUNK = 512
WIN_CHUNK = 128
PAGES_PER_STEP = 8
FFN_ROWS = 1024
BISECT_ITERS = 40
TIE_STRIP_ITERS = 64
SLC_SHIFT = SLC_BLOCK.bit_length() - 1
CMP_PER_SLC_SHIFT = CMP_PER_SLC.bit_length() - 1


def _round_up(n, m):
    return -(-n // m) * m


def _tile(n, cap):
    if n <= cap:
        return n
    best = None
    for t in range(LANE, cap + 1, LANE):
        if n % t == 0:
            best = t
    assert best is not None, (n, cap)
    return best


def _rms(x, g):
    return x * lax.rsqrt(jnp.mean(x * x, axis=-1, keepdims=True) + EPS) * g


def _norm_mm_kernel(x_ref, g_ref, w_ref, o_ref, xn_ref):
    @pl.when(pl.program_id(1) == 0)
    def _():
        xn_ref[...] = _rms(x_ref[...], g_ref[...]).astype(xn_ref.dtype)

    o_ref[...] = jnp.dot(xn_ref[...], w_ref[...], preferred_element_type=jnp.float32).astype(o_ref.dtype)


def norm_matmul(x, g, w):
    lead = x.shape[:-1]
    k = x.shape[-1]
    n = w.shape[-1]
    x2 = x.reshape(-1, k)
    m = x2.shape[0]
    tm = 512 if m % 512 == 0 else m
    tn = _tile(n, 1536)
    out = pl.pallas_call(
        _norm_mm_kernel,
        grid=(m // tm, n // tn),
        in_specs=[pl.BlockSpec((tm, k), lambda i, j: (i, 0)),
                  pl.BlockSpec((1, k), lambda i, j: (0, 0)),
                  pl.BlockSpec((k, tn), lambda i, j: (0, j))],
        out_specs=pl.BlockSpec((tm, tn), lambda i, j: (i, j)),
        out_shape=jax.ShapeDtypeStruct((m, n), jnp.float32),
        scratch_shapes=[pltpu.VMEM((tm, k), MXU_DTYPE)],
        compiler_params=pltpu.CompilerParams(
            dimension_semantics=("parallel", "arbitrary"), vmem_limit_bytes=VMEM_LIMIT),
        name="norm_matmul",
    )(x2, g.reshape(1, k).astype(jnp.float32), w.astype(MXU_DTYPE))
    return out.reshape(lead + (n,))


def _gated_merge_kernel(oa_ref, ob_ref, wa_ref, wb_ref, ga_ref, gb_ref, o_ref):
    dot = functools.partial(jnp.dot, preferred_element_type=jnp.float32)
    ya = dot(oa_ref[...], wa_ref[...])
    yb = dot(ob_ref[...], wb_ref[...])
    o_ref[...] = (jax.nn.sigmoid(ga_ref[...]) * ya + jax.nn.sigmoid(gb_ref[...]) * yb).astype(o_ref.dtype)


def gated_merge(o_nsa, o_dsa, gm, w_oa, w_ob):
    m, ka = o_nsa.shape
    kb = o_dsa.shape[1]
    n = w_oa.shape[1]
    tm = 512 if m % 512 == 0 else m
    tn = _tile(n, 1024)
    nb = n // tn
    return pl.pallas_call(
        _gated_merge_kernel,
        grid=(m // tm, nb),
        in_specs=[pl.BlockSpec((tm, ka), lambda i, j: (i, 0)),
                  pl.BlockSpec((tm, kb), lambda i, j: (i, 0)),
                  pl.BlockSpec((ka, tn), lambda i, j: (0, j)),
                  pl.BlockSpec((kb, tn), lambda i, j: (0, j)),
                  pl.BlockSpec((tm, tn), lambda i, j: (i, j)),
                  pl.BlockSpec((tm, tn), lambda i, j: (i, j + nb))],
        out_specs=pl.BlockSpec((tm, tn), lambda i, j: (i, j)),
        out_shape=jax.ShapeDtypeStruct((m, n), MXU_DTYPE),
        compiler_params=pltpu.CompilerParams(
            dimension_semantics=("parallel", "parallel"), vmem_limit_bytes=VMEM_LIMIT),
        name="gated_merge",
    )(o_nsa, o_dsa, w_oa.astype(MXU_DTYPE), w_ob.astype(MXU_DTYPE), gm, gm)


def _out_proj_kernel(z_ref, w_ref, x_ref, o_ref):
    o_ref[...] = x_ref[...] + jnp.dot(z_ref[...], w_ref[...], preferred_element_type=jnp.float32)


def out_proj_residual(z, w, x):
    m, k = z.shape
    n = w.shape[1]
    tm = 256 if m % 256 == 0 else m
    return pl.pallas_call(
        _out_proj_kernel,
        grid=(m // tm,),
        in_specs=[pl.BlockSpec((tm, k), lambda i: (i, 0)),
                  pl.BlockSpec((k, n), lambda i: (0, 0)),
                  pl.BlockSpec((tm, n), lambda i: (i, 0))],
        out_specs=pl.BlockSpec((tm, n), lambda i: (i, 0)),
        out_shape=jax.ShapeDtypeStruct((m, n), jnp.float32),
        compiler_params=pltpu.CompilerParams(dimension_semantics=("parallel",), vmem_limit_bytes=VMEM_LIMIT),
        name="out_proj_residual",
    )(z, w.astype(MXU_DTYPE), x)


def _mem_block_kernel(x_ref, g1_ref, g2_ref, wq_ref, kv_ref, wo_ref, x_out_ref, xn_out_ref):
    d = MEM_HEAD_DIM
    hd = MEM_HEADS * d
    x = x_ref[0]
    xn = _rms(x, g1_ref[...]).astype(wq_ref.dtype)
    q = jnp.dot(xn, wq_ref[...], preferred_element_type=jnp.float32).astype(wq_ref.dtype)
    outs = []
    for h in range(MEM_HEADS):
        k = kv_ref[0, :, h * d:(h + 1) * d].astype(wq_ref.dtype)
        v = kv_ref[0, :, hd + h * d:hd + (h + 1) * d].astype(wq_ref.dtype)
        s = _dot_nt(q[:, h * d:(h + 1) * d], k) * d ** -0.5
        e = jnp.exp(s - jnp.max(s, axis=1, keepdims=True))
        p = e / jnp.sum(e, axis=1, keepdims=True)
        outs.append(jnp.dot(p.astype(v.dtype), v, preferred_element_type=jnp.float32))
    o = jnp.concatenate(outs, axis=1).astype(wo_ref.dtype)
    x2 = x + jnp.dot(o, wo_ref[...], preferred_element_type=jnp.float32)
    x_out_ref[0] = x2
    xn_out_ref[0] = _rms(x2, g2_ref[...]).astype(xn_out_ref.dtype)


def mem_block(x, g1, g2, w_q, kv, w_o):
    b, t, dm = x.shape
    mt, kvw = kv.shape[1:]
    hd = w_q.shape[1]
    tm = 256 if t % 256 == 0 else t
    row_spec = pl.BlockSpec((1, tm, dm), lambda bi, i: (bi, i, 0))
    g_spec = pl.BlockSpec((1, dm), lambda bi, i: (0, 0))
    return pl.pallas_call(
        _mem_block_kernel,
        grid=(b, t // tm),
        in_specs=[row_spec, g_spec, g_spec,
                  pl.BlockSpec((dm, hd), lambda bi, i: (0, 0)),
                  pl.BlockSpec((1, mt, kvw), lambda bi, i: (bi, 0, 0)),
                  pl.BlockSpec((hd, dm), lambda bi, i: (0, 0))],
        out_specs=[row_spec, row_spec],
        out_shape=[jax.ShapeDtypeStruct((b, t, dm), jnp.float32), jax.ShapeDtypeStruct((b, t, dm), MXU_DTYPE)],
        compiler_params=pltpu.CompilerParams(
            dimension_semantics=("parallel", "parallel"), vmem_limit_bytes=VMEM_LIMIT),
        name="mem_block",
    )(x, g1.reshape(1, dm).astype(jnp.float32), g2.reshape(1, dm).astype(jnp.float32), w_q.astype(MXU_DTYPE),
      kv, w_o.astype(MXU_DTYPE))


def _ffn_up_kernel(x_ref, halo_ref, wg_ref, wu_ref, cwg_ref, cwu_ref, cbg_ref, cbu_ref, pg_ref, pu_ref,
                   h_ref, sg_ref, su_ref, *, nb, tm, t_real):
    i = pl.program_id(2)
    n_halo = halo_ref.shape[1]
    tn = wg_ref.shape[1]
    rows = nb * tm
    x = x_ref[...].reshape(rows, x_ref.shape[2])
    if nb == 1:
        x = jnp.concatenate([halo_ref[0], x], axis=0)
    row = _iota((nb, tm, 1), 1).reshape(rows, 1)
    last = (t_real - 1) // tm
    r_last = (t_real - 1) % tm
    spread = lambda p: jnp.broadcast_to(p, (nb, tm, tn)).reshape(rows, tn)

    def branch(w_ref, cw_ref, cb_ref, p_ref, s_ref):
        u = jnp.dot(x, w_ref[...], preferred_element_type=jnp.float32)
        p0, p1 = p_ref[:, 0:1, :], p_ref[:, 1:2, :]
        if nb == 1:
            uh, u = u[:n_halo], u[n_halo:]
            p0 = jnp.where(i == 0, p0, uh[n_halo - 2:n_halo - 1][None])
            p1 = jnp.where(i == 0, p1, uh[n_halo - 1:n_halo][None])
        p0, p1 = spread(p0), spread(p1)
        u1 = jnp.where(row == 0, p1, pltpu.roll(u, 1, 0))
        u2 = jnp.where(row == 0, p0, jnp.where(row == 1, p1, pltpu.roll(u, 2, 0)))

        @pl.when(i == last)
        def _():
            s_ref[...] = u.reshape(nb, tm, tn)[:, r_last - 1:r_last + 1, :]

        return cb_ref[...] + u2 * cw_ref[0:1, :] + u1 * cw_ref[1:2, :] + u * cw_ref[2:3, :]

    gate = branch(wg_ref, cwg_ref, cbg_ref, pg_ref, sg_ref)
    up = branch(wu_ref, cwu_ref, cbu_ref, pu_ref, su_ref)
    h_ref[...] = (jax.nn.silu(gate) * up).reshape(nb, tm, tn).astype(h_ref.dtype)


def ffn_up(xn, prev, w_up, conv_w, conv_b, t_real):
    b, t, dm = xn.shape
    f2 = w_up.shape[1]
    f = f2 // 2
    tm = FFN_ROWS if t % FFN_ROWS == 0 else t
    nb = 1 if t > tm else max(1, min(b, 512 // tm))
    while b % nb:
        nb -= 1
    tn = _tile(f, 512)
    nf = f // tn
    halo = min(BF16_ROWS, tm)
    hpt = tm // halo
    assert t_real >= 2 and (t_real - 1) % tm >= 1 and tm % SUBLANE == 0
    w_up = w_up.astype(MXU_DTYPE)
    conv_b = conv_b.reshape(1, f2)
    col = lambda off: (lambda j, bi, i: (0, j + off))
    st = lambda off: (lambda j, bi, i: (bi, 0, j + off))
    specs = [pl.BlockSpec((nb, tm, dm), lambda j, bi, i: (bi, i, 0)),
             pl.BlockSpec((1, halo, dm), lambda j, bi, i: (bi * nb, jnp.maximum(i * hpt - 1, 0), 0)),
             pl.BlockSpec((dm, tn), col(0)), pl.BlockSpec((dm, tn), col(nf)),
             pl.BlockSpec((CONV_WIDTH, tn), col(0)), pl.BlockSpec((CONV_WIDTH, tn), col(nf)),
             pl.BlockSpec((1, tn), col(0)), pl.BlockSpec((1, tn), col(nf)),
             pl.BlockSpec((nb, 2, tn), st(0)), pl.BlockSpec((nb, 2, tn), st(nf))]
    h, sg, su = pl.pallas_call(
        functools.partial(_ffn_up_kernel, nb=nb, tm=tm, t_real=t_real),
        grid=(nf, b // nb, t // tm),
        in_specs=specs,
        out_specs=[pl.BlockSpec((nb, tm, tn), lambda j, bi, i: (bi, i, j)),
                   pl.BlockSpec((nb, 2, tn), st(0)), pl.BlockSpec((nb, 2, tn), st(0))],
        out_shape=[jax.ShapeDtypeStruct((b, t, f), MXU_DTYPE), jax.ShapeDtypeStruct((b, 2, f), jnp.float32),
                   jax.ShapeDtypeStruct((b, 2, f), jnp.float32)],
        compiler_params=pltpu.CompilerParams(
            dimension_semantics=("parallel", "parallel", "arbitrary"), vmem_limit_bytes=VMEM_LIMIT),
        name="ffn_up",
    )(xn, xn, w_up, w_up, conv_w, conv_w, conv_b, conv_b, prev, prev)
    return h, jnp.concatenate([sg, su], axis=-1)


def _ffn_down_kernel(h_ref, w_ref, x_ref, g_ref, o_ref, acc_ref):
    @pl.when(pl.program_id(1) == 0)
    def _():
        acc_ref[...] = x_ref[...]

    acc_ref[...] += jnp.dot(h_ref[...], w_ref[...], preferred_element_type=jnp.float32)

    @pl.when(pl.program_id(1) == pl.num_programs(1) - 1)
    def _():
        o_ref[...] = _rms(acc_ref[...], g_ref[...])


def ffn_down_norm(h, w_down, x, g):
    m, f = h.shape
    dm = w_down.shape[1]
    tm = 512 if m % 512 == 0 else m
    tk = _tile(f, 2048)
    return pl.pallas_call(
        _ffn_down_kernel,
        grid=(m // tm, f // tk),
        in_specs=[pl.BlockSpec((tm, tk), lambda i, l: (i, l)),
                  pl.BlockSpec((tk, dm), lambda i, l: (l, 0)),
                  pl.BlockSpec((tm, dm), lambda i, l: (i, 0)),
                  pl.BlockSpec((1, dm), lambda i, l: (0, 0))],
        out_specs=pl.BlockSpec((tm, dm), lambda i, l: (i, 0)),
        out_shape=jax.ShapeDtypeStruct((m, dm), jnp.float32),
        scratch_shapes=[pltpu.VMEM((tm, dm), jnp.float32)],
        compiler_params=pltpu.CompilerParams(
            dimension_semantics=("parallel", "arbitrary"), vmem_limit_bytes=VMEM_LIMIT),
        name="ffn_down_norm",
    )(h, w_down.astype(MXU_DTYPE), x, g.reshape(1, dm).astype(jnp.float32))


def dense_tail(x, o_nsa, o_dsa, gm, kv_mem, prev_u, t_real, g_mem, g_ffn, g_final, w_oa, w_ob, w_o, w_mq, w_mo,
               w_up, conv_w, conv_b, w_down):
    b, t, dm = x.shape
    rows = lambda a: a.reshape(b * t, a.shape[-1])
    z = gated_merge(rows(o_nsa), rows(o_dsa), rows(gm), w_oa, w_ob)
    x1 = out_proj_residual(z, w_o, rows(x)).reshape(b, t, dm)
    x2, xn2 = mem_block(x1, g_mem, g_ffn, w_mq, kv_mem, w_mo)
    h, state = ffn_up(xn2, prev_u, w_up, conv_w, conv_b, t_real)
    y = ffn_down_norm(rows(h), w_down, rows(x2), g_final)
    return y.reshape(b, t, dm), state


def _dot_nt(a, b):
    return lax.dot_general(a, b, (((1,), (1,)), ((), ())), preferred_element_type=jnp.float32)


def _iota(shape, dim):
    return lax.broadcasted_iota(jnp.int32, shape, dim)


def _flash_init(rows, d):
    return (jnp.full((rows, 1), NEG_INF, jnp.float32), jnp.zeros((rows, 1), jnp.float32),
            jnp.zeros((rows, d), jnp.float32))


def _flash_step(carry, q, k, v, madd, nh, scale=None, kv_t=False):
    m, l, acc = carry
    s = jnp.dot(q, k, preferred_element_type=jnp.float32) if kv_t else _dot_nt(q, k)
    if scale is not None:
        s = s * scale
    r, kb = s.shape
    s = (s.reshape(nh, r // nh, kb) + madd[None]).reshape(r, kb)
    m_new = jnp.maximum(m, jnp.max(s, axis=1, keepdims=True))
    m_safe = jnp.where(m_new == NEG_INF, 0.0, m_new)
    p = jnp.exp(s - m_safe)
    alpha = jnp.exp(m - m_safe)
    l = alpha * l + jnp.sum(p, axis=1, keepdims=True)
    pv = _dot_nt(p.astype(v.dtype), v) if kv_t else jnp.dot(p.astype(v.dtype), v, preferred_element_type=jnp.float32)
    return m_new, l, alpha * acc + pv


def _flash_finish(carry):
    _, l, acc = carry
    return acc / jnp.maximum(l, 1e-30)


def _split_dot(x, m01):
    hi = x.astype(jnp.bfloat16)
    r1 = x - hi.astype(jnp.float32)
    mid = r1.astype(jnp.bfloat16)
    lo = (r1 - mid.astype(jnp.float32)).astype(jnp.bfloat16)
    dot = functools.partial(jnp.dot, preferred_element_type=jnp.float32)
    return dot(hi, m01) + dot(mid, m01) + dot(lo, m01)


def _nsa_group(g, i, qc_ref, qr_ref, gate_ref, kc_ref, vc_ref, ks_ref, vs_ref, kw_ref, vw_ref, mask_ref,
               *, tq, kb, qpos0, wbase, ns, n_sel, kv_t):
    nh, d = HPG_A, HEAD_DIM_A
    rows = nh * tq
    nc = kc_ref.shape[3]
    nsp = _round_up(ns, LANE)
    t0 = qpos0 + i * tq
    nj = (t0 + tq - 1) // kb + 1
    cols = slice(g * d, (g + 1) * d)

    def stack_heads(q_ref):
        return jnp.concatenate([q_ref[0, :, (g * nh + h) * d:(g * nh + h + 1) * d] for h in range(nh)], axis=0)

    qc = stack_heads(qc_ref)
    qr = stack_heads(qr_ref)
    tpos = t0 + _iota((tq, 1), 0)

    blk_last = _iota((1, nc), 1) * CMP_STRIDE + (CMP_BLOCK - 1)
    madd_c = jnp.where(blk_last <= tpos, 0.0, NEG_INF)
    s = _dot_nt(qc, kc_ref[0, 0, g]).reshape(nh, tq, nc) + madd_c[None]
    m = jnp.max(s, axis=2, keepdims=True)
    m = jnp.where(m == NEG_INF, 0.0, m)
    e = jnp.exp(s - m)
    p = e / jnp.maximum(jnp.sum(e, axis=2, keepdims=True), 1e-30)
    o_cmp = jnp.dot(p.reshape(rows, nc).astype(vc_ref.dtype), vc_ref[0, 0, g], preferred_element_type=jnp.float32)

    imp = jnp.sum(p, axis=0)
    c_id = _iota((nc, nsp), 0)
    m_id = _iota((nc, nsp), 1)
    overlap = (jnp.right_shift(c_id, CMP_PER_SLC_SHIFT) == m_id) | (c_id == m_id * CMP_PER_SLC - 1)
    score = _split_dot(imp, overlap.astype(jnp.bfloat16))
    blk = _iota((1, nsp), 1)
    cur = jnp.right_shift(tpos, SLC_SHIFT)
    forced = (blk == 0) | (blk == cur) | (blk == cur - 1)
    sc = jnp.where(forced, POS_INF, jnp.where(blk * SLC_BLOCK <= tpos, score, NEG_INF))
    if tq % LANE == 0 and nsp % LANE == 0:
        sc_t = sc.T
        blk_t = _iota((nsp, 1), 0)
        rank_t = jnp.zeros((nsp, tq), jnp.float32)
        for mp in range(ns):
            ref = sc_t[mp:mp + 1, :]
            beats = (ref > sc_t) | ((ref == sc_t) & (blk_t > mp))
            rank_t = rank_t + jnp.where(beats, 1.0, 0.0)
        sel = jnp.where(rank_t < n_sel, 1.0, 0.0).T.astype(jnp.bfloat16)
    else:
        rank = jnp.zeros((tq, nsp), jnp.float32)
        for mp in range(ns):
            col = sc[:, mp:mp + 1]
            beats = (col > sc) | ((col == sc) & (blk > mp))
            rank = rank + jnp.where(beats, 1.0, 0.0)
        sel = jnp.where(rank < n_sel, 1.0, 0.0).astype(jnp.bfloat16)

    def make_mask(j, _):
        kpos = j * kb + _iota((1, kb), 1)
        expand = (jnp.right_shift(j * kb + _iota((nsp, kb), 1), SLC_SHIFT) == _iota((nsp, kb), 0))
        hit = jnp.dot(sel, expand.astype(jnp.bfloat16), preferred_element_type=jnp.float32)
        mask_ref[j] = jnp.where((hit > 0.5) & (kpos <= tpos), 0.0, NEG_INF)
        return 0

    def slc_body(j, carry):
        off = pl.multiple_of(j * kb, kb)
        return _flash_step(carry, qr, ks_ref[0, pl.ds(off, kb), cols], vs_ref[0, pl.ds(off, kb), cols],
                           mask_ref[j], nh)

    if kv_t:
        sel_f = sel.astype(jnp.float32)
        low_half = _iota((tq, LANE), 1) < SLC_BLOCK
        pieces = []
        for v in range(kb // LANE):
            hit = jnp.where(low_half, sel_f[:, 2 * v:2 * v + 1], sel_f[:, 2 * v + 1:2 * v + 2])
            kpos = v * LANE + _iota((1, LANE), 1)
            pieces.append(jnp.where((hit > 0.5) & (kpos <= tpos), 0.0, NEG_INF))
        o_slc = _flash_finish(_flash_step(_flash_init(rows, d), qr, ks_ref[0, cols, :], vs_ref[0, cols, :],
                                          jnp.concatenate(pieces, axis=1), nh, kv_t=True))
    else:
        lax.fori_loop(0, nj, make_mask, 0)
        o_slc = _flash_finish(lax.fori_loop(0, nj, slc_body, _flash_init(rows, d)))

    wk = min(_round_up(WINDOW + tq, WIN_CHUNK), kw_ref.shape[1])
    first = jnp.maximum(t0 - (WINDOW - 1) - wbase, 0) // WIN_CHUNK * WIN_CHUNK
    off = pl.multiple_of(jnp.minimum(first, kw_ref.shape[1] - wk), WIN_CHUNK)
    dist = tpos - (wbase + off + _iota((1, wk), 1))
    madd_w = jnp.where((dist >= 0) & (dist < WINDOW), 0.0, NEG_INF)
    o_win = _flash_finish(_flash_step(_flash_init(rows, d), qr, kw_ref[0, pl.ds(off, wk), cols],
                                      vw_ref[0, pl.ds(off, wk), cols], madd_w, nh))

    gates = gate_ref[0]
    outs = []
    for h in range(nh):
        rs = slice(h * tq, (h + 1) * tq)
        c = g * nh + h
        outs.append(gates[:, c:c + 1] * o_cmp[rs] + gates[:, N_HEADS_A + c:N_HEADS_A + c + 1] * o_slc[rs]
                    + gates[:, 2 * N_HEADS_A + c:2 * N_HEADS_A + c + 1] * o_win[rs])
    return outs


def _nsa_kernel(qc_ref, qr_ref, gate_ref, kc_ref, vc_ref, ks_ref, vs_ref, kw_ref, vw_ref, o_ref, mask_ref, **kw):
    i = pl.program_id(1)
    outs = []
    for g in range(KV_GROUPS_A):
        outs += _nsa_group(g, i, qc_ref, qr_ref, gate_ref, kc_ref, vc_ref, ks_ref, vs_ref, kw_ref, vw_ref,
                           mask_ref, **kw)
    o_ref[0] = jnp.concatenate(outs, axis=1).astype(o_ref.dtype)


def nsa_attention(qc, qr, gates, kvc, ks, vs, kw, vw, *, tq, qpos0, wbase, n_keys, kv_t=False):
    b, t, hd = qc.shape
    l = ks.shape[2] if kv_t else ks.shape[1]
    lw = kw.shape[1]
    nc, d = kvc.shape[3:]
    kb = min(KEY_CHUNK, l) if tq >= TQ else l
    assert l % kb == 0 and t % tq == 0 and lw % WIN_CHUNK == 0 and (kb == l or not kv_t)
    assert (qpos0 + t - 1) // kb + 1 <= l // kb and (qpos0 + t - 1 - wbase) // WIN_CHUNK + 1 <= lw // WIN_CHUNK
    assert (qpos0 - wbase) % WIN_CHUNK == 0 and WIN_CHUNK % tq == 0 and tq > 1
    assert 2 * SLC_BLOCK == LANE or not kv_t
    ns = l // SLC_BLOCK
    n_sel = min(N_SELECT, -(-n_keys // SLC_BLOCK))
    q_spec = pl.BlockSpec((1, tq, hd), lambda bi, i: (bi, i, 0))
    kc_spec = pl.BlockSpec((1, 1, KV_GROUPS_A, nc, d), lambda bi, i: (0, bi, 0, 0, 0))
    vc_spec = pl.BlockSpec((1, 1, KV_GROUPS_A, nc, d), lambda bi, i: (1, bi, 0, 0, 0))
    k_spec = pl.BlockSpec((1, KV_A, l) if kv_t else (1, l, KV_A), lambda bi, i: (bi, 0, 0))
    w_spec = pl.BlockSpec((1, lw, KV_A), lambda bi, i: (bi, 0, 0))
    return pl.pallas_call(
        functools.partial(_nsa_kernel, tq=tq, kb=kb, qpos0=qpos0, wbase=wbase, ns=ns, n_sel=n_sel, kv_t=kv_t),
        grid=(b, t // tq),
        in_specs=[q_spec, q_spec, pl.BlockSpec((1, tq, 3 * N_HEADS_A), lambda bi, i: (bi, i, 0)),
                  kc_spec, vc_spec, k_spec, k_spec, w_spec, w_spec],
        out_specs=q_spec,
        out_shape=jax.ShapeDtypeStruct((b, t, hd), MXU_DTYPE),
        scratch_shapes=[pltpu.VMEM((l // kb, tq, kb), jnp.float32)],
        compiler_params=pltpu.CompilerParams(
            dimension_semantics=("parallel", "arbitrary"), vmem_limit_bytes=VMEM_LIMIT),
        name="nsa_attention",
    )(qc, qr, gates, kvc, kvc, ks, vs, kw, vw)


def _dsa_kernel(qb_ref, qi_ref, wi_ref, kb_ref, vb_ref, ki_ref, o_ref, score_ref,
                *, tq, tq_real, kb, qpos0, n_keep, kv_t):
    i = pl.program_id(1)
    nh, d = N_HEADS_B, HEAD_DIM_B
    t0 = qpos0 + i * tq
    nj = (t0 + tq - 1) // kb + 1
    tpos = t0 + _iota((tq, 1), 0)
    w = wi_ref[0]

    def idx_body(j, carry):
        lo, hi = carry
        off = pl.multiple_of(j * kb, kb)
        kidx = ki_ref[0] if kv_t else ki_ref[0, pl.ds(off, kb), :]
        acc = jnp.zeros((tq, kb), jnp.float32)
        heads = [qi_ref[0, :, h * IDX_DIM:(h + 1) * IDX_DIM] for h in range(IDX_HEADS)]
        if kv_t:
            dots_all = jnp.dot(jnp.concatenate(heads, axis=0), kidx, preferred_element_type=jnp.float32)
        for h in range(IDX_HEADS):
            dots = dots_all[h * tq:(h + 1) * tq] if kv_t else _dot_nt(heads[h], kidx)
            acc = acc + w[:, h:h + 1] * jnp.maximum(dots, 0.0)
        vis = (off + _iota((1, kb), 1)) <= tpos
        score_ref[j] = jnp.where(vis, acc, NEG_INF)
        lo = jnp.minimum(lo, jnp.min(jnp.where(vis, acc, POS_INF), axis=1, keepdims=True))
        hi = jnp.maximum(hi, jnp.max(jnp.where(vis, acc, NEG_INF), axis=1, keepdims=True))
        return lo, hi

    lo, hi = lax.fori_loop(0, nj, idx_body, (jnp.full((tq, 1), POS_INF, jnp.float32),
                                             jnp.full((tq, 1), NEG_INF, jnp.float32)))

    k = float(n_keep)
    n_vis = (tpos + 1).astype(jnp.float32)
    n_keys = score_ref.shape[0] * kb

    def reduce_scores(pred, pick, init, combine, lane_reduce):
        def body(j, acc):
            sc = score_ref[j]
            for c in range(kb // LANE):
                kpos = (j * kb + c * LANE + _iota((1, LANE), 1)).astype(jnp.float32)
                piece = sc[:, c * LANE:(c + 1) * LANE]
                acc = combine(acc, pick(pred(piece, kpos), piece))
            return acc

        return lane_reduce(lax.fori_loop(0, nj, body, jnp.full((tq, LANE), init, jnp.float32)),
                           axis=1, keepdims=True)

    def count_where(pred):
        return reduce_scores(pred, lambda m, _: jnp.where(m, 1.0, 0.0), 0.0, jnp.add, jnp.sum)

    def min_where(pred):
        return reduce_scores(pred, lambda m, x: jnp.where(m, x, POS_INF), POS_INF, jnp.minimum, jnp.min)

    real = _iota((tq, 1), 0) < tq_real

    def any_row(flag):
        return jnp.max(jnp.where(real & flag, 1.0, 0.0)) > 0.0

    def bisect(state):
        it, lo, hi, cnt_lo = state
        mid = 0.5 * (lo + hi)
        mid_b = jnp.broadcast_to(mid, (tq, LANE))
        cnt = count_where(lambda x, _: x >= mid_b)
        ge = cnt >= k
        return it + 1, jnp.where(ge, mid, lo), jnp.where(ge, hi, mid), jnp.where(ge, cnt, cnt_lo)

    _, thr, _, cnt_lo = lax.while_loop(
        lambda st: (st[0] < BISECT_ITERS) & any_row((st[3] != k) & (n_vis > k)), bisect,
        (jnp.int32(0), lo, hi, n_vis))

    def break_ties():
        def above(v):
            v_b = jnp.broadcast_to(v, (tq, LANE))
            return count_where(lambda x, _: x > v_b)

        def strip(state):
            it, v, c_gt = state
            v_b = jnp.broadcast_to(v, (tq, LANE))
            v_next = jnp.where(c_gt >= k, min_where(lambda x, _: x > v_b), v)
            return it + 1, v_next, above(v_next)

        thr_b = jnp.broadcast_to(thr, (tq, LANE))
        v0 = min_where(lambda x, _: x >= thr_b)
        _, v, c_gt = lax.while_loop(lambda st: (st[0] < TIE_STRIP_ITERS) & any_row(st[2] >= k), strip,
                                    (jnp.int32(0), v0, above(v0)))
        need = k - c_gt
        v_b = jnp.broadcast_to(v, (tq, LANE))

        def narrow(_, bounds):
            j_lo, j_hi = bounds
            mid = jnp.floor(0.5 * (j_lo + j_hi))
            ge = count_where(lambda x, kpos: (x == v_b) & (kpos <= mid)) >= need
            return jnp.where(ge, j_lo, mid), jnp.where(ge, mid, j_hi)

        _, j_max = lax.fori_loop(0, n_keys.bit_length(), narrow,
                                 (jnp.full((tq, 1), -1.0, jnp.float32), jnp.full((tq, 1), n_keys - 1.0, jnp.float32)))
        return v, j_max

    thr, j_max = lax.cond(any_row((cnt_lo > k) & (n_vis > k)), break_ties,
                          lambda: (thr, jnp.full((tq, 1), float(n_keys), jnp.float32)))

    q = jnp.concatenate([qb_ref[0, :, h * d:(h + 1) * d] for h in range(nh)], axis=0)

    def att_body(j, carry):
        off = pl.multiple_of(j * kb, kb)
        sc = score_ref[j]
        kpos = (off + _iota((1, kb), 1)).astype(jnp.float32)
        madd = jnp.where((sc > thr) | ((sc == thr) & (kpos <= j_max)), 0.0, NEG_INF)
        if kv_t:
            return _flash_step(carry, q, kb_ref[0], vb_ref[0], madd, nh, scale=d ** -0.5, kv_t=True)
        return _flash_step(carry, q, kb_ref[0, pl.ds(off, kb), :], vb_ref[0, pl.ds(off, kb), :], madd, nh,
                           scale=d ** -0.5)

    o = _flash_finish(lax.fori_loop(0, nj, att_body, _flash_init(nh * tq, d)))
    o_ref[0] = jnp.concatenate([o[h * tq:(h + 1) * tq] for h in range(nh)], axis=1).astype(o_ref.dtype)


def dsa_attention(qb, qi, wi, kb_, vb, ki, *, tq, qpos0, n_keep, kv_t=False, tq_real=None):
    b, t, hd = qb.shape
    l = kb_.shape[2] if kv_t else kb_.shape[1]
    kb = min(KEY_CHUNK, l) if tq >= TQ else l
    assert l % kb == 0 and t % tq == 0 and (qpos0 + t - 1) // kb + 1 <= l // kb and (kb == l or not kv_t)
    kv_spec = lambda width: pl.BlockSpec((1, width, l) if kv_t else (1, l, width), lambda bi, i: (bi, 0, 0))
    return pl.pallas_call(
        functools.partial(_dsa_kernel, tq=tq, tq_real=tq if tq_real is None else tq_real, kb=kb, qpos0=qpos0,
                          n_keep=n_keep, kv_t=kv_t),
        grid=(b, t // tq),
        in_specs=[pl.BlockSpec((1, tq, hd), lambda bi, i: (bi, i, 0)),
                  pl.BlockSpec((1, tq, IDX_HEADS * IDX_DIM), lambda bi, i: (bi, i, 0)),
                  pl.BlockSpec((1, tq, IDX_HEADS), lambda bi, i: (bi, i, 0)),
                  kv_spec(HEAD_DIM_B), kv_spec(HEAD_DIM_B), kv_spec(IDX_DIM)],
        out_specs=pl.BlockSpec((1, tq, hd), lambda bi, i: (bi, i, 0)),
        out_shape=jax.ShapeDtypeStruct((b, t, hd), MXU_DTYPE),
        scratch_shapes=[pltpu.VMEM((l // kb, tq, kb), jnp.float32)],
        compiler_params=pltpu.CompilerParams(
            dimension_semantics=("parallel", "arbitrary"), vmem_limit_bytes=VMEM_LIMIT),
        name="dsa_attention",
    )(qb, qi, wi, kb_, vb, ki)


def _page_maps(n_pages, pp):
    n_steps = n_pages // pp

    def page_map(r):
        return lambda b, s, pt: (pt[b * n_pages + jnp.minimum(s, n_steps - 1) * pp + r], 0, 0)

    return n_steps, page_map


def _chunk_rows(tok_ref, sec, row0, n):
    d = HEAD_DIM_A
    first_half = _iota((n, KV_A), 1) < d
    pieces = [[] for _ in range(KV_GROUPS_A)]
    for j in range(0, CMP_STRIDE, 2):
        a, b = [tok_ref[sec, pl.ds(row0 + jj, n, stride=CMP_STRIDE), :] for jj in (j, j + 1)]
        pieces[0].append(jnp.where(first_half, a, pltpu.roll(b, d, 1)))
        pieces[1].append(jnp.where(first_half, pltpu.roll(a, d, 1), b))
    return [jnp.concatenate(p, axis=1) for p in pieces]


def _nsa_gather_kernel(pt_ref, *refs, pp, n_steps, rows):
    del pt_ref
    pages, (tail_tok_ref, tail_t_ref, pe_ref) = refs[:pp], refs[pp:pp + 3]
    zt_ref, zb_ref, ks_ref, vs_ref, tok_ref = refs[pp + 3:]
    is_tail = pl.program_id(1) == n_steps
    z = [[[] for _ in range(KV_GROUPS_A)] for _ in range(2)]
    for r in range(pp):
        cs = slice(r * rows, (r + 1) * rows)
        ks_ref[0, :, cs] = jnp.where(is_tail, tail_t_ref[0, :KV_A, cs],
                                     pages[r][0, 2 * KV_A:3 * KV_A, :]).astype(ks_ref.dtype)
        vs_ref[0, :, cs] = jnp.where(is_tail, tail_t_ref[0, KV_A:, cs],
                                     pages[r][0, 3 * KV_A:, :]).astype(vs_ref.dtype)
        for sec in range(2):
            tok_ref[sec, cs, :] = jnp.where(is_tail, tail_tok_ref[0, sec, cs, :],
                                            pages[r][0, sec * KV_A:(sec + 1) * KV_A, :].T)
            for g, zg in enumerate(_chunk_rows(tok_ref, sec, r * rows, rows // CMP_STRIDE)):
                z[sec][g].append(zg)
    for sec in range(2):
        for g in range(KV_GROUPS_A):
            zf = jnp.concatenate(z[sec][g], axis=0)
            zt_ref[sec, 0, g] = (zf + pe_ref[sec, 0]).astype(zt_ref.dtype)
            zb_ref[sec, 0, g] = (zf + pe_ref[sec, 1]).astype(zb_ref.dtype)


def nsa_gather(cache_t, page_table, new_rows, cmp_pe):
    db, n_pages = page_table.shape
    width, rows = cache_t.shape[1:]
    pp = PAGES_PER_STEP
    n_steps, page_map = _page_maps(n_pages, pp)
    l = (n_steps + 1) * pp * rows
    cps = pp * rows // CMP_STRIDE
    flat = CMP_STRIDE * HEAD_DIM_A
    pe = cmp_pe.reshape(2, 2, 1, flat).astype(jnp.float32)
    tail = pad_rows(new_rows, pp * rows)
    tail_tok = tail[:, :, :2 * KV_A].reshape(db, pp * rows, 2, KV_A).swapaxes(1, 2)
    tail_t = tail[:, :, 2 * KV_A:].swapaxes(1, 2)
    z_spec = pl.BlockSpec((2, 1, KV_GROUPS_A, cps, flat), lambda b, s, pt: (0, b, 0, s, 0))
    r_spec = pl.BlockSpec((1, KV_A, pp * rows), lambda b, s, pt: (b, 0, s))
    z_shape = jax.ShapeDtypeStruct((2, db, KV_GROUPS_A, l // CMP_STRIDE, flat), MXU_DTYPE)
    r_shape = jax.ShapeDtypeStruct((db, KV_A, l), MXU_DTYPE)
    return pl.pallas_call(
        functools.partial(_nsa_gather_kernel, pp=pp, n_steps=n_steps, rows=rows),
        grid_spec=pltpu.PrefetchScalarGridSpec(
            num_scalar_prefetch=1,
            grid=(db, n_steps + 1),
            in_specs=[pl.BlockSpec((1, width, rows), page_map(r)) for r in range(pp)]
            + [pl.BlockSpec((1, 2, pp * rows, KV_A), lambda b, s, pt: (b, 0, 0, 0)),
               pl.BlockSpec((1, 2 * KV_A, pp * rows), lambda b, s, pt: (b, 0, 0)),
               pl.BlockSpec((2, 2, 1, flat), lambda b, s, pt: (0, 0, 0, 0))],
            out_specs=[z_spec, z_spec, r_spec, r_spec],
            scratch_shapes=[pltpu.VMEM((2, pp * rows, KV_A), jnp.float32)]),
        out_shape=[z_shape, z_shape, r_shape, r_shape],
        compiler_params=pltpu.CompilerParams(
            dimension_semantics=("parallel", "arbitrary"), vmem_limit_bytes=VMEM_LIMIT),
        name="nsa_gather",
    )(page_table.reshape(-1), *([cache_t] * pp), tail_tok, tail_t, pe)


def _dsa_gather_kernel(pt_ref, *refs, pp, n_steps, rows):
    del pt_ref
    pages, tail_ref, (k_ref, v_ref, i_ref) = refs[:pp], refs[pp], refs[pp + 1:]
    is_tail = pl.program_id(1) == n_steps
    d = HEAD_DIM_B
    for r in range(pp):
        cs = slice(r * rows, (r + 1) * rows)
        x = jnp.where(is_tail, tail_ref[0, :, cs], pages[r][0])
        k_ref[0, :, cs] = x[:d].astype(k_ref.dtype)
        v_ref[0, :, cs] = x[d:2 * d].astype(v_ref.dtype)
        i_ref[0, :, cs] = x[2 * d:].astype(i_ref.dtype)


def dsa_gather(cache_t, page_table, new_rows):
    db, n_pages = page_table.shape
    width, rows = cache_t.shape[1:]
    pp = PAGES_PER_STEP
    n_steps, page_map = _page_maps(n_pages, pp)
    l = (n_steps + 1) * pp * rows
    tail_t = pad_rows(new_rows, pp * rows).swapaxes(1, 2)
    out_spec = lambda w: pl.BlockSpec((1, w, pp * rows), lambda b, s, pt: (b, 0, s))
    widths = (HEAD_DIM_B, HEAD_DIM_B, IDX_DIM)
    return pl.pallas_call(
        functools.partial(_dsa_gather_kernel, pp=pp, n_steps=n_steps, rows=rows),
        grid_spec=pltpu.PrefetchScalarGridSpec(
            num_scalar_prefetch=1,
            grid=(db, n_steps + 1),
            in_specs=[pl.BlockSpec((1, width, rows), page_map(r)) for r in range(pp)]
            + [pl.BlockSpec((1, width, pp * rows), lambda b, s, pt: (b, 0, 0))],
            out_specs=[out_spec(w) for w in widths]),
        out_shape=[jax.ShapeDtypeStruct((db, w, l), MXU_DTYPE) for w in widths],
        compiler_params=pltpu.CompilerParams(
            dimension_semantics=("parallel", "arbitrary"), vmem_limit_bytes=VMEM_LIMIT),
        name="dsa_gather",
    )(page_table.reshape(-1), *([cache_t] * pp), tail_t)


def _compress_kernel(zt_ref, zb_ref, pe_ref, w1t_ref, w1b_ref, b1_ref, w2_ref, o_ref, ab_ref, *, ncp):
    dot = functools.partial(jnp.dot, preferred_element_type=jnp.float32)
    ch = zb_ref.shape[2]
    at = dot(zt_ref[0, 0], w1t_ref[0])
    ab_ref[:ch] = dot(zb_ref[0, 0], w1b_ref[0])
    pe_rows = jnp.broadcast_to(pe_ref[0, 1], (SUBLANE, pe_ref.shape[3])).astype(zb_ref.dtype)
    ab_ref[ch:] = dot(pe_rows, w1b_ref[0])
    h = jax.nn.gelu(at[:ncp] + ab_ref[pl.ds(1, ncp), :] + b1_ref[0])
    o_ref[0, 0, :ncp] = dot(h.astype(w2_ref.dtype), w2_ref[0]).astype(o_ref.dtype)
    if o_ref.shape[2] > ncp:
        o_ref[0, 0, ncp:] = jnp.zeros((o_ref.shape[2] - ncp, o_ref.shape[3]), o_ref.dtype)


def compress(zt, zb, cmp_pe, w1, b1, w2, n_keys):
    ncp = _round_up(-(-n_keys // CMP_STRIDE), BF16_ROWS)
    ncl = ncp if ncp <= LANE else _round_up(ncp, LANE)
    _, nb, ch, kdim = zt.shape
    hid = w1.shape[-1]
    d = w2.shape[-1]
    assert ch + SUBLANE >= ncp + 1 and ch % SUBLANE == 0
    w1 = w1.astype(MXU_DTYPE)
    pe = cmp_pe.reshape(2, 2, 1, kdim).astype(jnp.float32)
    z_spec = pl.BlockSpec((1, 1, ch, kdim), lambda s, n: (s, n, 0, 0))
    return pl.pallas_call(
        functools.partial(_compress_kernel, ncp=ncp),
        grid=(2, nb),
        in_specs=[z_spec, z_spec,
                  pl.BlockSpec((1, 2, 1, kdim), lambda s, n: (s, 0, 0, 0)),
                  pl.BlockSpec((1, kdim, hid), lambda s, n: (s, 0, 0)),
                  pl.BlockSpec((1, kdim, hid), lambda s, n: (s, 1, 0)),
                  pl.BlockSpec((1, 1, hid), lambda s, n: (s, 0, 0)),
                  pl.BlockSpec((1, hid, d), lambda s, n: (s, 0, 0))],
        out_specs=pl.BlockSpec((1, 1, ncl, d), lambda s, n: (s, n, 0, 0)),
        out_shape=jax.ShapeDtypeStruct((2, nb, ncl, d), MXU_DTYPE),
        scratch_shapes=[pltpu.VMEM((ch + SUBLANE, hid), jnp.float32)],
        compiler_params=pltpu.CompilerParams(
            dimension_semantics=("parallel", "parallel"), vmem_limit_bytes=VMEM_LIMIT),
        name="compress",
    )(zt, zb, pe, w1, w1, b1.reshape(2, 1, hid).astype(jnp.float32), w2.astype(MXU_DTYPE))


def pad_rows(a, n):
    return jnp.pad(a, [(0, 0), (0, n - a.shape[1])] + [(0, 0)] * (a.ndim - 2))


def _rope_tables(pos, d):
    half = d // ROT_FRACTION // 2
    inv = ROPE_THETA ** (-jnp.arange(half, dtype=jnp.float32) / half)
    ang = pos.astype(jnp.float32)[:, None] * inv[None, :]
    lane = jnp.arange(LANE) % d
    cos = jnp.cos(ang)[:, lane % half]
    sin = jnp.sin(ang)[:, lane % half]
    one, zero = jnp.ones_like(cos), jnp.zeros_like(cos)
    c = jnp.where(lane < 2 * half, cos, one)
    sa = jnp.where((lane >= half) & (lane < 2 * half), sin, zero)
    sb = jnp.where(lane < half, -sin, zero)
    return jnp.stack([c, sa, sb])


def _rope(x, t_ref, half):
    c, sa, sb = t_ref[0], t_ref[1], t_ref[2]
    outs = []
    for j in range(x.shape[1] // LANE):
        xs = x[:, j * LANE:(j + 1) * LANE]
        outs.append(xs * c + pltpu.roll(xs, half, 1) * sa + pltpu.roll(xs, LANE - half, 1) * sb)
    return outs[0] if len(outs) == 1 else jnp.concatenate(outs, axis=1)


_QA = N_HEADS_A * HEAD_DIM_A
_KVA = 6 * KV_A
_QB = N_HEADS_B * HEAD_DIM_B
_KVB = 2 * HEAD_DIM_B
_QI = IDX_HEADS * IDX_DIM
_MISC = IDX_DIM + IDX_HEADS + 3 * N_HEADS_A
PREP_WIDTH = _QA + _KVA + _QB + _KVB + _QI + _MISC
assert _MISC == LANE


def _prep_kernel(x_ref, t64_ref, t128_ref, pe_ref, qc_ref, qr_ref, qb_ref, qi_ref, nsa_ref, win_ref, dsa_ref,
                 ks_ref, vs_ref, kw_ref, vw_ref, kb_ref, vb_ref, ki_ref, wi_ref, gate_ref, *z_refs, tm):
    x = x_ref[...]
    o = 0
    qa = x[:, o:o + _QA]; o += _QA
    kva = x[:, o:o + _KVA]; o += _KVA
    qb = x[:, o:o + _QB]; o += _QB
    kvb = x[:, o:o + _KVB]; o += _KVB
    qi = x[:, o:o + _QI]; o += _QI
    misc = x[:, o:o + _MISC]
    h64, h128 = HEAD_DIM_A // ROT_FRACTION // 2, HEAD_DIM_B // ROT_FRACTION // 2
    sec = lambda i: kva[:, i * KV_A:(i + 1) * KV_A]
    mx = lambda a: a.astype(qc_ref.dtype)
    qc_ref[...] = mx(qa * HEAD_DIM_A ** -0.5)
    qr_ref[...] = mx(_rope(qa, t64_ref, h64) * HEAD_DIM_A ** -0.5)
    k_slc, k_win = _rope(sec(2), t64_ref, h64), _rope(sec(4), t64_ref, h64)
    nsa_ref[:, :2 * KV_A] = kva[:, :2 * KV_A]
    nsa_ref[:, 2 * KV_A:3 * KV_A] = k_slc
    nsa_ref[:, 3 * KV_A:] = sec(3)
    win_ref[:, :KV_A] = k_win
    win_ref[:, KV_A:] = sec(5)
    ks_ref[...], vs_ref[...], kw_ref[...], vw_ref[...] = mx(k_slc), mx(sec(3)), mx(k_win), mx(sec(5))
    qb_ref[...] = mx(_rope(qb, t128_ref, h128))
    k_b, v_b = _rope(kvb[:, :HEAD_DIM_B], t128_ref, h128), kvb[:, HEAD_DIM_B:]
    k_idx = _rope(misc, t64_ref, h64)[:, :IDX_DIM]
    dsa_ref[:, :HEAD_DIM_B] = k_b
    dsa_ref[:, HEAD_DIM_B:2 * HEAD_DIM_B] = v_b
    dsa_ref[:, 2 * HEAD_DIM_B:] = k_idx
    kb_ref[...], vb_ref[...], ki_ref[...] = mx(k_b), mx(v_b), mx(k_idx)
    qi_ref[...] = mx(_rope(qi, t64_ref, h64) * IDX_DIM ** -0.5)
    wi_ref[...] = misc[:, IDX_DIM:IDX_DIM + IDX_HEADS] * IDX_HEADS ** -0.5
    gate_ref[...] = jax.nn.sigmoid(misc[:, IDX_DIM + IDX_HEADS:])
    if z_refs:
        zt_ref, zb_ref, tok_ref = z_refs
        for s_ in range(2):
            tok_ref[s_] = sec(s_)
            for g, zg in enumerate(_chunk_rows(tok_ref, s_, 0, tm // CMP_STRIDE)):
                zt_ref[s_, 0, g] = (zg + pe_ref[s_, 0]).astype(zt_ref.dtype)
                zb_ref[s_, 0, g] = (zg + pe_ref[s_, 1]).astype(zb_ref.dtype)


def prep(proj, pos, cmp_pe, with_chunks):
    b, t, width = proj.shape
    assert width == PREP_WIDTH
    m = b * t
    tm = 256 if t % 256 == 0 else m
    assert m % tm == 0 and t % tm in (0, t)
    nt = max(t // tm, 1)
    flat = CMP_STRIDE * HEAD_DIM_A
    names = ['qc', 'qr', 'qb', 'qi', 'nsa_rows', 'win_rows', 'dsa_rows', 'ks', 'vs', 'kw', 'vw', 'kb', 'vb', 'ki',
             'wi', 'gates']
    widths = [_QA, _QA, _QB, _QI, NSA_SECTIONS * KV_A, 2 * KV_A, DSA_ROW, KV_A, KV_A, KV_A, KV_A, HEAD_DIM_B,
              HEAD_DIM_B, IDX_DIM, IDX_HEADS, 3 * N_HEADS_A]
    dtypes = [MXU_DTYPE] * 4 + [jnp.float32] * 3 + [MXU_DTYPE] * 7 + [jnp.float32] * 2
    row = lambda w: pl.BlockSpec((tm, w), lambda i: (i, 0))
    out_specs = [row(w) for w in widths]
    out_shape = [jax.ShapeDtypeStruct((m, w), dt) for w, dt in zip(widths, dtypes)]
    scratch = []
    if with_chunks:
        assert tm % CMP_STRIDE == 0 and t % tm == 0
        z_spec = pl.BlockSpec((2, 1, KV_GROUPS_A, tm // CMP_STRIDE, flat), lambda i: (0, i // nt, 0, i % nt, 0))
        z_shape = jax.ShapeDtypeStruct((2, b, KV_GROUPS_A, t // CMP_STRIDE, flat), MXU_DTYPE)
        out_specs += [z_spec, z_spec]
        out_shape += [z_shape, z_shape]
        names += ['zt', 'zb']
        scratch = [pltpu.VMEM((2, tm, KV_A), jnp.float32)]
    t_spec = pl.BlockSpec((3, tm, LANE), lambda i: (0, i, 0))
    outs = pl.pallas_call(
        functools.partial(_prep_kernel, tm=tm),
        grid=(m // tm,),
        in_specs=[row(width), t_spec, t_spec, pl.BlockSpec((2, 2, 1, flat), lambda i: (0, 0, 0, 0))],
        out_specs=out_specs,
        out_shape=out_shape,
        scratch_shapes=scratch,
        compiler_params=pltpu.CompilerParams(dimension_semantics=("parallel",), vmem_limit_bytes=VMEM_LIMIT),
        name="prep",
    )(proj.reshape(m, width), _rope_tables(pos, HEAD_DIM_A), _rope_tables(pos, HEAD_DIM_B),
      cmp_pe.reshape(2, 2, 1, flat).astype(jnp.float32))
    return {n: (o if o.ndim > 2 else o.reshape(b, t, o.shape[-1])) for n, o in zip(names, outs)}


def project(x, g, w_in, sizes):
    starts = [sum(sizes[:i]) for i in range(len(sizes))]
    qa, kva, ga, qb, kvb, qi, ki, wi, gm = [slice(o, o + n) for o, n in zip(starts, sizes)]
    w_bf = w_in.astype(MXU_DTYPE)
    w_prep = jnp.concatenate([w_bf[:, c] for c in (qa, kva, qb, kvb, qi, ki, wi, ga)], axis=1)
    return norm_matmul(x, g, w_prep), norm_matmul(x, g, w_bf[:, gm])


def mixer_prompt(x, g, w_in, cmp_pe, cmp_w1, cmp_b1, cmp_w2, sizes):
    b, s, _ = x.shape
    proj, gm = project(x, g, w_in, sizes)
    p = prep(proj, jnp.tile(jnp.arange(s, dtype=jnp.int32), b), cmp_pe, True)
    merge_bg = lambda a: a.reshape((2, b * KV_GROUPS_A) + a.shape[3:])
    kvc = compress(merge_bg(p['zt']), merge_bg(p['zb']), cmp_pe, cmp_w1, cmp_b1, cmp_w2, s)
    kvc = kvc.reshape(2, b, KV_GROUPS_A, kvc.shape[2], HEAD_DIM_A)
    o_nsa = nsa_attention(p['qc'], p['qr'], p['gates'], kvc, p['ks'], p['vs'], p['kw'], p['vw'],
                          tq=min(TQ, s), qpos0=0, wbase=0, n_keys=s)
    o_dsa = dsa_attention(p['qb'], p['qi'], p['wi'], p['kb'], p['vb'], p['ki'],
                          tq=min(TQ, s), qpos0=0, n_keep=min(DSA_TOPK, s // 4))
    nsa_rows = p['nsa_rows'].reshape(b, s, NSA_SECTIONS, KV_GROUPS_A, HEAD_DIM_A)
    win_state = p['win_rows'].reshape(b, s, 2, KV_GROUPS_A, HEAD_DIM_A)[:, -min(WINDOW, s):]
    return (o_nsa, o_dsa, gm), nsa_rows, win_state, p['dsa_rows']


def mixer_sample(x, g, cache_nsa, win_buf, cache_dsa, page_table, w_in, cmp_pe, cmp_w1, cmp_b1, cmp_w2, sizes):
    b, t, _ = x.shape
    page = cache_nsa.shape[1]
    past_len = page_table.shape[1] * page
    n_keys = past_len + t
    proj, gm = project(x, g, w_in, sizes)
    p = prep(proj, jnp.tile(past_len + jnp.arange(t, dtype=jnp.int32), b), cmp_pe, False)
    pad = lambda a: pad_rows(a, TQ_STEP)

    cache_nsa_t = jnp.transpose(cache_nsa, (0, 2, 3, 4, 1)).reshape(cache_nsa.shape[0], NSA_SECTIONS * KV_A, page)
    zt, zb, ks, vs = nsa_gather(cache_nsa_t, page_table, p['nsa_rows'], cmp_pe)
    merge_bg = lambda a: a.reshape((2, b * KV_GROUPS_A) + a.shape[3:])
    kvc = compress(merge_bg(zt), merge_bg(zb), cmp_pe, cmp_w1, cmp_b1, cmp_w2, n_keys)
    kvc = kvc.reshape(2, b, KV_GROUPS_A, kvc.shape[2], HEAD_DIM_A)
    w_len = win_buf.shape[1]
    win_new = p['win_rows'].reshape(b, t, 2, KV_GROUPS_A, HEAD_DIM_A)
    win_all = jnp.concatenate([win_buf, win_new], axis=1)
    win_pad = pad_rows(win_all, w_len + WIN_CHUNK).astype(MXU_DTYPE)
    kw = win_pad[:, :, 0].reshape(b, w_len + WIN_CHUNK, KV_A)
    vw = win_pad[:, :, 1].reshape(b, w_len + WIN_CHUNK, KV_A)
    o_nsa = nsa_attention(pad(p['qc']), pad(p['qr']), pad(p['gates']), kvc, ks, vs, kw, vw,
                          tq=TQ_STEP, qpos0=past_len, wbase=past_len - w_len, n_keys=n_keys, kv_t=True)

    kb_, vb, ki = dsa_gather(jnp.swapaxes(cache_dsa, 1, 2), page_table, p['dsa_rows'])
    o_dsa = dsa_attention(pad(p['qb']), pad(p['qi']), pad(p['wi']), kb_, vb, ki, tq=TQ_STEP, qpos0=past_len,
                          n_keep=min(DSA_TOPK, n_keys // 4), kv_t=True, tq_real=t)
    nsa_rows = p['nsa_rows'].reshape(b, t, NSA_SECTIONS, KV_GROUPS_A, HEAD_DIM_A)
    return (o_nsa, o_dsa, pad(gm)), nsa_rows, win_all[:, -w_len:], p['dsa_rows']


def kernel(x_prompt, x_sample, mem_prompt, cache_nsa_kv, state_nsa_win, cache_dsa_kv, cache_mem_kv, state_conv,
           page_table, norm_g, w_in, cmp_pe, cmp_w1, cmp_b1, cmp_w2, w_out_a, w_out_b, w_out, w_mem_q, w_mem_kv,
           w_mem_out, w_up, conv_w, conv_b, w_down, final_g):
    depth = w_in.shape[0]
    d_model = x_prompt.shape[-1]
    d_ff = w_down.shape[1]
    assert CONV_WIDTH == 3
    sizes = (N_HEADS_A * HEAD_DIM_A, 6 * KV_A, 3 * N_HEADS_A, N_HEADS_B * HEAD_DIM_B, 2 * HEAD_DIM_B,
             IDX_HEADS * IDX_DIM, IDX_DIM, IDX_HEADS, 2 * d_model)
    xp, xs = x_prompt, pad_rows(x_sample, TQ_STEP)
    t_step = x_sample.shape[1]
    nsa_p, nsa_s, win_p, win_s, dsa_p, dsa_s, mem_p, conv_p, conv_s = [], [], [], [], [], [], [], [], []
    for l in range(depth):
        assert l == depth - 1, "the fused FFN epilogue applies the final norm"
        branches_p, a, bwin, c = mixer_prompt(xp, norm_g[l, 0], w_in[l], cmp_pe[l], cmp_w1[l], cmp_b1[l], cmp_w2[l],
                                              sizes)
        nsa_p.append(a); win_p.append(bwin); dsa_p.append(c)
        branches_s, a, bwin, c = mixer_sample(xs[:, :t_step], norm_g[l, 0], cache_nsa_kv[l], state_nsa_win[l],
                                              cache_dsa_kv[l], page_table, w_in[l], cmp_pe[l], cmp_w1[l], cmp_b1[l],
                                              cmp_w2[l], sizes)
        nsa_s.append(a); win_s.append(bwin); dsa_s.append(c)
        kv_p = norm_matmul(mem_prompt, norm_g[l, 2], w_mem_kv[l])
        mem_p.append(kv_p.reshape(kv_p.shape[:2] + (2, MEM_HEADS, MEM_HEAD_DIM)))
        kv_s = cache_mem_kv[l].reshape(cache_mem_kv.shape[1:3] + (-1,))
        weights = (norm_g[l, 1], norm_g[l, 3], final_g, w_out_a[l], w_out_b[l], w_out[l], w_mem_q[l], w_mem_out[l],
                   w_up[l], conv_w[l], conv_b[l], w_down[l])
        xp, cp = dense_tail(xp, *branches_p, kv_p, jnp.zeros((xp.shape[0], CONV_WIDTH - 1, 2 * d_ff), xp.dtype),
                            xp.shape[1], *weights)
        xs, cs = dense_tail(xs, *branches_s, kv_s, state_conv[l], t_step, *weights)
        conv_p.append(cp); conv_s.append(cs)
    y_prompt, y_sample = xp, xs[:, :t_step]
    return (y_prompt, y_sample, jnp.stack(nsa_p), jnp.stack(nsa_s), jnp.stack(win_p), jnp.stack(win_s),
            jnp.stack(dsa_p), jnp.stack(dsa_s), jnp.stack(mem_p), jnp.stack(conv_p), jnp.stack(conv_s))
```

```python
import functools

import jax
import jax.numpy as jnp
from jax import lax
from jax.experimental import pallas as pl
from jax.experimental.pallas import tpu as pltpu

N_HEADS_A = 16
HEAD_DIM_A = 64
KV_GROUPS_A = 2
CMP_BLOCK = 32
CMP_STRIDE = 16
SLC_BLOCK = 64
N_SELECT = 16
WINDOW = 512
N_HEADS_B = 8
HEAD_DIM_B = 128
IDX_HEADS = 16
IDX_DIM = 64
DSA_TOPK = 256
MEM_HEADS = 4
MEM_HEAD_DIM = 128
CONV_WIDTH = 3
ROPE_THETA = 500000.0
ROT_FRACTION = 4
EPS = 1e-6
KV_A = KV_GROUPS_A * HEAD_DIM_A
CMP_PER_SLC = SLC_BLOCK // CMP_STRIDE
HPG_A = N_HEADS_A // KV_GROUPS_A
DSA_ROW = 2 * HEAD_DIM_B + IDX_DIM
NSA_SECTIONS = 4

LANE = 128
SUBLANE = 8
BF16_ROWS = 16
VMEM_LIMIT = 48 * 1024 * 1024

NEG_INF = float('-inf')
POS_INF = float('inf')
MXU_DTYPE = jnp.bfloat16
TQ = 128
TQ_STEP = BF16_ROWS
KEY_CHUNK = 512
WIN_CHUNK = 128
PAGES_PER_STEP = 8
FFN_ROWS = 1024
BISECT_ITERS = 40
TIE_STRIP_ITERS = 64
SLC_SHIFT = SLC_BLOCK.bit_length() - 1
CMP_PER_SLC_SHIFT = CMP_PER_SLC.bit_length() - 1


def _round_up(n, m):
    return -(-n // m) * m


def _tile(n, cap):
    if n <= cap:
        return n
    best = None
    for t in range(LANE, cap + 1, LANE):
        if n % t == 0:
            best = t
    assert best is not None, (n, cap)
    return best


def _rms(x, g):
    return x * lax.rsqrt(jnp.mean(x * x, axis=-1, keepdims=True) + EPS) * g


def _norm_mm_kernel(x_ref, g_ref, w_ref, o_ref, xn_ref):
    @pl.when(pl.program_id(1) == 0)
    def _():
        xn_ref[...] = _rms(x_ref[...], g_ref[...]).astype(xn_ref.dtype)

    o_ref[...] = jnp.dot(xn_ref[...], w_ref[...], preferred_element_type=jnp.float32).astype(o_ref.dtype)


def norm_matmul(x, g, w):
    lead = x.shape[:-1]
    k = x.shape[-1]
    n = w.shape[-1]
    x2 = x.reshape(-1, k)
    m = x2.shape[0]
    tm = 512 if m % 512 == 0 else m
    tn = _tile(n, 1536)
    out = pl.pallas_call(
        _norm_mm_kernel,
        grid=(m // tm, n // tn),
        in_specs=[pl.BlockSpec((tm, k), lambda i, j: (i, 0)),
                  pl.BlockSpec((1, k), lambda i, j: (0, 0)),
                  pl.BlockSpec((k, tn), lambda i, j: (0, j))],
        out_specs=pl.BlockSpec((tm, tn), lambda i, j: (i, j)),
        out_shape=jax.ShapeDtypeStruct((m, n), jnp.float32),
        scratch_shapes=[pltpu.VMEM((tm, k), MXU_DTYPE)],
        compiler_params=pltpu.CompilerParams(
            dimension_semantics=("parallel", "arbitrary"), vmem_limit_bytes=VMEM_LIMIT),
        name="norm_matmul",
    )(x2, g.reshape(1, k).astype(jnp.float32), w.astype(MXU_DTYPE))
    return out.reshape(lead + (n,))


def _gated_merge_kernel(oa_ref, ob_ref, wa_ref, wb_ref, ga_ref, gb_ref, o_ref):
    dot = functools.partial(jnp.dot, preferred_element_type=jnp.float32)
    ya = dot(oa_ref[...], wa_ref[...])
    yb = dot(ob_ref[...], wb_ref[...])
    o_ref[...] = (jax.nn.sigmoid(ga_ref[...]) * ya + jax.nn.sigmoid(gb_ref[...]) * yb).astype(o_ref.dtype)


def gated_merge(o_nsa, o_dsa, gm, w_oa, w_ob):
    m, ka = o_nsa.shape
    kb = o_dsa.shape[1]
    n = w_oa.shape[1]
    tm = 512 if m % 512 == 0 else m
    tn = _tile(n, 1024)
    nb = n // tn
    return pl.pallas_call(
        _gated_merge_kernel,
        grid=(m // tm, nb),
        in_specs=[pl.BlockSpec((tm, ka), lambda i, j: (i, 0)),
                  pl.BlockSpec((tm, kb), lambda i, j: (i, 0)),
                  pl.BlockSpec((ka, tn), lambda i, j: (0, j)),
                  pl.BlockSpec((kb, tn), lambda i, j: (0, j)),
                  pl.BlockSpec((tm, tn), lambda i, j: (i, j)),
                  pl.BlockSpec((tm, tn), lambda i, j: (i, j + nb))],
        out_specs=pl.BlockSpec((tm, tn), lambda i, j: (i, j)),
        out_shape=jax.ShapeDtypeStruct((m, n), MXU_DTYPE),
        compiler_params=pltpu.CompilerParams(
            dimension_semantics=("parallel", "parallel"), vmem_limit_bytes=VMEM_LIMIT),
        name="gated_merge",
    )(o_nsa, o_dsa, w_oa.astype(MXU_DTYPE), w_ob.astype(MXU_DTYPE), gm, gm)


def _out_proj_kernel(z_ref, w_ref, x_ref, o_ref):
    o_ref[...] = x_ref[...] + jnp.dot(z_ref[...], w_ref[...], preferred_element_type=jnp.float32)


def out_proj_residual(z, w, x):
    m, k = z.shape
    n = w.shape[1]
    tm = 256 if m % 256 == 0 else m
    return pl.pallas_call(
        _out_proj_kernel,
        grid=(m // tm,),
        in_specs=[pl.BlockSpec((tm, k), lambda i: (i, 0)),
                  pl.BlockSpec((k, n), lambda i: (0, 0)),
                  pl.BlockSpec((tm, n), lambda i: (i, 0))],
        out_specs=pl.BlockSpec((tm, n), lambda i: (i, 0)),
        out_shape=jax.ShapeDtypeStruct((m, n), jnp.float32),
        compiler_params=pltpu.CompilerParams(dimension_semantics=("parallel",), vmem_limit_bytes=VMEM_LIMIT),
        name="out_proj_residual",
    )(z, w.astype(MXU_DTYPE), x)


def _mem_block_kernel(x_ref, g1_ref, g2_ref, wq_ref, kv_ref, wo_ref, x_out_ref, xn_out_ref):
    d = MEM_HEAD_DIM
    hd = MEM_HEADS * d
    x = x_ref[0]
    xn = _rms(x, g1_ref[...]).astype(wq_ref.dtype)
    q = jnp.dot(xn, wq_ref[...], preferred_element_type=jnp.float32).astype(wq_ref.dtype)
    outs = []
    for h in range(MEM_HEADS):
        k = kv_ref[0, :, h * d:(h + 1) * d].astype(wq_ref.dtype)
        v = kv_ref[0, :, hd + h * d:hd + (h + 1) * d].astype(wq_ref.dtype)
        s = _dot_nt(q[:, h * d:(h + 1) * d], k) * d ** -0.5
        e = jnp.exp(s - jnp.max(s, axis=1, keepdims=True))
        p = e / jnp.sum(e, axis=1, keepdims=True)
        outs.append(jnp.dot(p.astype(v.dtype), v, preferred_element_type=jnp.float32))
    o = jnp.concatenate(outs, axis=1).astype(wo_ref.dtype)
    x2 = x + jnp.dot(o, wo_ref[...], preferred_element_type=jnp.float32)
    x_out_ref[0] = x2
    xn_out_ref[0] = _rms(x2, g2_ref[...]).astype(xn_out_ref.dtype)


def mem_block(x, g1, g2, w_q, kv, w_o):
    b, t, dm = x.shape
    mt, kvw = kv.shape[1:]
    hd = w_q.shape[1]
    tm = 256 if t % 256 == 0 else t
    row_spec = pl.BlockSpec((1, tm, dm), lambda bi, i: (bi, i, 0))
    g_spec = pl.BlockSpec((1, dm), lambda bi, i: (0, 0))
    return pl.pallas_call(
        _mem_block_kernel,
        grid=(b, t // tm),
        in_specs=[row_spec, g_spec, g_spec,
                  pl.BlockSpec((dm, hd), lambda bi, i: (0, 0)),
                  pl.BlockSpec((1, mt, kvw), lambda bi, i: (bi, 0, 0)),
                  pl.BlockSpec((hd, dm), lambda bi, i: (0, 0))],
        out_specs=[row_spec, row_spec],
        out_shape=[jax.ShapeDtypeStruct((b, t, dm), jnp.float32), jax.ShapeDtypeStruct((b, t, dm), MXU_DTYPE)],
        compiler_params=pltpu.CompilerParams(
            dimension_semantics=("parallel", "parallel"), vmem_limit_bytes=VMEM_LIMIT),
        name="mem_block",
    )(x, g1.reshape(1, dm).astype(jnp.float32), g2.reshape(1, dm).astype(jnp.float32), w_q.astype(MXU_DTYPE),
      kv, w_o.astype(MXU_DTYPE))


def _ffn_up_kernel(x_ref, halo_ref, wg_ref, wu_ref, cwg_ref, cwu_ref, cbg_ref, cbu_ref, pg_ref, pu_ref,
                   h_ref, sg_ref, su_ref, *, nb, tm, t_real):
    i = pl.program_id(2)
    n_halo = halo_ref.shape[1]
    tn = wg_ref.shape[1]
    rows = nb * tm
    x = x_ref[...].reshape(rows, x_ref.shape[2])
    if nb == 1:
        x = jnp.concatenate([halo_ref[0], x], axis=0)
    row = _iota((nb, tm, 1), 1).reshape(rows, 1)
    last = (t_real - 1) // tm
    r_last = (t_real - 1) % tm
    spread = lambda p: jnp.broadcast_to(p, (nb, tm, tn)).reshape(rows, tn)

    def branch(w_ref, cw_ref, cb_ref, p_ref, s_ref):
        u = jnp.dot(x, w_ref[...], preferred_element_type=jnp.float32)
        p0, p1 = p_ref[:, 0:1, :], p_ref[:, 1:2, :]
        if nb == 1:
            uh, u = u[:n_halo], u[n_halo:]
            p0 = jnp.where(i == 0, p0, uh[n_halo - 2:n_halo - 1][None])
            p1 = jnp.where(i == 0, p1, uh[n_halo - 1:n_halo][None])
        p0, p1 = spread(p0), spread(p1)
        u1 = jnp.where(row == 0, p1, pltpu.roll(u, 1, 0))
        u2 = jnp.where(row == 0, p0, jnp.where(row == 1, p1, pltpu.roll(u, 2, 0)))

        @pl.when(i == last)
        def _():
            s_ref[...] = u.reshape(nb, tm, tn)[:, r_last - 1:r_last + 1, :]

        return cb_ref[...] + u2 * cw_ref[0:1, :] + u1 * cw_ref[1:2, :] + u * cw_ref[2:3, :]

    gate = branch(wg_ref, cwg_ref, cbg_ref, pg_ref, sg_ref)
    up = branch(wu_ref, cwu_ref, cbu_ref, pu_ref, su_ref)
    h_ref[...] = (jax.nn.silu(gate) * up).reshape(nb, tm, tn).astype(h_ref.dtype)


def ffn_up(xn, prev, w_up, conv_w, conv_b, t_real):
    b, t, dm = xn.shape
    f2 = w_up.shape[1]
    f = f2 // 2
    tm = FFN_ROWS if t % FFN_ROWS == 0 else t
    nb = 1 if t > tm else max(1, min(b, 512 // tm))
    while b % nb:
        nb -= 1
    tn = _tile(f, 512)
    nf = f // tn
    halo = min(BF16_ROWS, tm)
    hpt = tm // halo
    assert t_real >= 2 and (t_real - 1) % tm >= 1 and tm % SUBLANE == 0
    w_up = w_up.astype(MXU_DTYPE)
    conv_b = conv_b.reshape(1, f2)
    col = lambda off: (lambda j, bi, i: (0, j + off))
    st = lambda off: (lambda j, bi, i: (bi, 0, j + off))
    specs = [pl.BlockSpec((nb, tm, dm), lambda j, bi, i: (bi, i, 0)),
             pl.BlockSpec((1, halo, dm), lambda j, bi, i: (bi * nb, jnp.maximum(i * hpt - 1, 0), 0)),
             pl.BlockSpec((dm, tn), col(0)), pl.BlockSpec((dm, tn), col(nf)),
             pl.BlockSpec((CONV_WIDTH, tn), col(0)), pl.BlockSpec((CONV_WIDTH, tn), col(nf)),
             pl.BlockSpec((1, tn), col(0)), pl.BlockSpec((1, tn), col(nf)),
             pl.BlockSpec((nb, 2, tn), st(0)), pl.BlockSpec((nb, 2, tn), st(nf))]
    h, sg, su = pl.pallas_call(
        functools.partial(_ffn_up_kernel, nb=nb, tm=tm, t_real=t_real),
        grid=(nf, b // nb, t // tm),
        in_specs=specs,
        out_specs=[pl.BlockSpec((nb, tm, tn), lambda j, bi, i: (bi, i, j)),
                   pl.BlockSpec((nb, 2, tn), st(0)), pl.BlockSpec((nb, 2, tn), st(0))],
        out_shape=[jax.ShapeDtypeStruct((b, t, f), MXU_DTYPE), jax.ShapeDtypeStruct((b, 2, f), jnp.float32),
                   jax.ShapeDtypeStruct((b, 2, f), jnp.float32)],
        compiler_params=pltpu.CompilerParams(
            dimension_semantics=("parallel", "parallel", "arbitrary"), vmem_limit_bytes=VMEM_LIMIT),
        name="ffn_up",
    )(xn, xn, w_up, w_up, conv_w, conv_w, conv_b, conv_b, prev, prev)
    return h, jnp.concatenate([sg, su], axis=-1)


def _ffn_down_kernel(h_ref, w_ref, x_ref, g_ref, o_ref, acc_ref):
    @pl.when(pl.program_id(1) == 0)
    def _():
        acc_ref[...] = x_ref[...]

    acc_ref[...] += jnp.dot(h_ref[...], w_ref[...], preferred_element_type=jnp.float32)

    @pl.when(pl.program_id(1) == pl.num_programs(1) - 1)
    def _():
        o_ref[...] = _rms(acc_ref[...], g_ref[...])


def ffn_down_norm(h, w_down, x, g):
    m, f = h.shape
    dm = w_down.shape[1]
    tm = 512 if m % 512 == 0 else m
    tk = _tile(f, 2048)
    return pl.pallas_call(
        _ffn_down_kernel,
        grid=(m // tm, f // tk),
        in_specs=[pl.BlockSpec((tm, tk), lambda i, l: (i, l)),
                  pl.BlockSpec((tk, dm), lambda i, l: (l, 0)),
                  pl.BlockSpec((tm, dm), lambda i, l: (i, 0)),
                  pl.BlockSpec((1, dm), lambda i, l: (0, 0))],
        out_specs=pl.BlockSpec((tm, dm), lambda i, l: (i, 0)),
        out_shape=jax.ShapeDtypeStruct((m, dm), jnp.float32),
        scratch_shapes=[pltpu.VMEM((tm, dm), jnp.float32)],
        compiler_params=pltpu.CompilerParams(
            dimension_semantics=("parallel", "arbitrary"), vmem_limit_bytes=VMEM_LIMIT),
        name="ffn_down_norm",
    )(h, w_down.astype(MXU_DTYPE), x, g.reshape(1, dm).astype(jnp.float32))


def dense_tail(x, o_nsa, o_dsa, gm, kv_mem, prev_u, t_real, g_mem, g_ffn, g_final, w_oa, w_ob, w_o, w_mq, w_mo,
               w_up, conv_w, conv_b, w_down):
    b, t, dm = x.shape
    rows = lambda a: a.reshape(b * t, a.shape[-1])
    z = gated_merge(rows(o_nsa), rows(o_dsa), rows(gm), w_oa, w_ob)
    x1 = out_proj_residual(z, w_o, rows(x)).reshape(b, t, dm)
    x2, xn2 = mem_block(x1, g_mem, g_ffn, w_mq, kv_mem, w_mo)
    h, state = ffn_up(xn2, prev_u, w_up, conv_w, conv_b, t_real)
    y = ffn_down_norm(rows(h), w_down, rows(x2), g_final)
    return y.reshape(b, t, dm), state


def _dot_nt(a, b):
    return lax.dot_general(a, b, (((1,), (1,)), ((), ())), preferred_element_type=jnp.float32)


def _iota(shape, dim):
    return lax.broadcasted_iota(jnp.int32, shape, dim)


def _flash_init(rows, d):
    return (jnp.full((rows, 1), NEG_INF, jnp.float32), jnp.zeros((rows, 1), jnp.float32),
            jnp.zeros((rows, d), jnp.float32))


def _flash_step(carry, q, k, v, madd, nh, scale=None, kv_t=False):
    m, l, acc = carry
    s = jnp.dot(q, k, preferred_element_type=jnp.float32) if kv_t else _dot_nt(q, k)
    if scale is not None:
        s = s * scale
    r, kb = s.shape
    s = (s.reshape(nh, r // nh, kb) + madd[None]).reshape(r, kb)
    m_new = jnp.maximum(m, jnp.max(s, axis=1, keepdims=True))
    m_safe = jnp.where(m_new == NEG_INF, 0.0, m_new)
    p = jnp.exp(s - m_safe)
    alpha = jnp.exp(m - m_safe)
    l = alpha * l + jnp.sum(p, axis=1, keepdims=True)
    pv = _dot_nt(p.astype(v.dtype), v) if kv_t else jnp.dot(p.astype(v.dtype), v, preferred_element_type=jnp.float32)
    return m_new, l, alpha * acc + pv


def _flash_finish(carry):
    _, l, acc = carry
    return acc / jnp.maximum(l, 1e-30)


def _split_dot(x, m01):
    hi = x.astype(jnp.bfloat16)
    r1 = x - hi.astype(jnp.float32)
    mid = r1.astype(jnp.bfloat16)
    lo = (r1 - mid.astype(jnp.float32)).astype(jnp.bfloat16)
    dot = functools.partial(jnp.dot, preferred_element_type=jnp.float32)
    return dot(hi, m01) + dot(mid, m01) + dot(lo, m01)


def _nsa_group(g, i, qc_ref, qr_ref, gate_ref, kc_ref, vc_ref, ks_ref, vs_ref, kw_ref, vw_ref, mask_ref,
               *, tq, kb, qpos0, wbase, ns, n_sel, kv_t):
    nh, d = HPG_A, HEAD_DIM_A
    rows = nh * tq
    nc = kc_ref.shape[3]
    nsp = _round_up(ns, LANE)
    t0 = qpos0 + i * tq
    nj = (t0 + tq - 1) // kb + 1
    cols = slice(g * d, (g + 1) * d)

    def stack_heads(q_ref):
        return jnp.concatenate([q_ref[0, :, (g * nh + h) * d:(g * nh + h + 1) * d] for h in range(nh)], axis=0)

    qc = stack_heads(qc_ref)
    qr = stack_heads(qr_ref)
    tpos = t0 + _iota((tq, 1), 0)

    blk_last = _iota((1, nc), 1) * CMP_STRIDE + (CMP_BLOCK - 1)
    madd_c = jnp.where(blk_last <= tpos, 0.0, NEG_INF)
    s = _dot_nt(qc, kc_ref[0, 0, g]).reshape(nh, tq, nc) + madd_c[None]
    m = jnp.max(s, axis=2, keepdims=True)
    m = jnp.where(m == NEG_INF, 0.0, m)
    e = jnp.exp(s - m)
    p = e / jnp.maximum(jnp.sum(e, axis=2, keepdims=True), 1e-30)
    o_cmp = jnp.dot(p.reshape(rows, nc).astype(vc_ref.dtype), vc_ref[0, 0, g], preferred_element_type=jnp.float32)

    imp = jnp.sum(p, axis=0)
    c_id = _iota((nc, nsp), 0)
    m_id = _iota((nc, nsp), 1)
    overlap = (jnp.right_shift(c_id, CMP_PER_SLC_SHIFT) == m_id) | (c_id == m_id * CMP_PER_SLC - 1)
    score = _split_dot(imp, overlap.astype(jnp.bfloat16))
    blk = _iota((1, nsp), 1)
    cur = jnp.right_shift(tpos, SLC_SHIFT)
    forced = (blk == 0) | (blk == cur) | (blk == cur - 1)
    sc = jnp.where(forced, POS_INF, jnp.where(blk * SLC_BLOCK <= tpos, score, NEG_INF))
    if tq % LANE == 0 and nsp % LANE == 0:
        sc_t = sc.T
        blk_t = _iota((nsp, 1), 0)
        rank_t = jnp.zeros((nsp, tq), jnp.float32)
        for mp in range(ns):
            ref = sc_t[mp:mp + 1, :]
            beats = (ref > sc_t) | ((ref == sc_t) & (blk_t > mp))
            rank_t = rank_t + jnp.where(beats, 1.0, 0.0)
        sel = jnp.where(rank_t < n_sel, 1.0, 0.0).T.astype(jnp.bfloat16)
    else:
        rank = jnp.zeros((tq, nsp), jnp.float32)
        for mp in range(ns):
            col = sc[:, mp:mp + 1]
            beats = (col > sc) | ((col == sc) & (blk > mp))
            rank = rank + jnp.where(beats, 1.0, 0.0)
        sel = jnp.where(rank < n_sel, 1.0, 0.0).astype(jnp.bfloat16)

    def make_mask(j, _):
        kpos = j * kb + _iota((1, kb), 1)
        expand = (jnp.right_shift(j * kb + _iota((nsp, kb), 1), SLC_SHIFT) == _iota((nsp, kb), 0))
        hit = jnp.dot(sel, expand.astype(jnp.bfloat16), preferred_element_type=jnp.float32)
        mask_ref[j] = jnp.where((hit > 0.5) & (kpos <= tpos), 0.0, NEG_INF)
        return 0

    def slc_body(j, carry):
        off = pl.multiple_of(j * kb, kb)
        return _flash_step(carry, qr, ks_ref[0, pl.ds(off, kb), cols], vs_ref[0, pl.ds(off, kb), cols],
                           mask_ref[j], nh)

    if kv_t:
        sel_f = sel.astype(jnp.float32)
        low_half = _iota((tq, LANE), 1) < SLC_BLOCK
        pieces = []
        for v in range(kb // LANE):
            hit = jnp.where(low_half, sel_f[:, 2 * v:2 * v + 1], sel_f[:, 2 * v + 1:2 * v + 2])
            kpos = v * LANE + _iota((1, LANE), 1)
            pieces.append(jnp.where((hit > 0.5) & (kpos <= tpos), 0.0, NEG_INF))
        o_slc = _flash_finish(_flash_step(_flash_init(rows, d), qr, ks_ref[0, cols, :], vs_ref[0, cols, :],
                                          jnp.concatenate(pieces, axis=1), nh, kv_t=True))
    else:
        lax.fori_loop(0, nj, make_mask, 0)
        o_slc = _flash_finish(lax.fori_loop(0, nj, slc_body, _flash_init(rows, d)))

    wk = min(_round_up(WINDOW + tq, WIN_CHUNK), kw_ref.shape[1])
    first = jnp.maximum(t0 - (WINDOW - 1) - wbase, 0) // WIN_CHUNK * WIN_CHUNK
    off = pl.multiple_of(jnp.minimum(first, kw_ref.shape[1] - wk), WIN_CHUNK)
    dist = tpos - (wbase + off + _iota((1, wk), 1))
    madd_w = jnp.where((dist >= 0) & (dist < WINDOW), 0.0, NEG_INF)
    o_win = _flash_finish(_flash_step(_flash_init(rows, d), qr, kw_ref[0, pl.ds(off, wk), cols],
                                      vw_ref[0, pl.ds(off, wk), cols], madd_w, nh))

    gates = gate_ref[0]
    outs = []
    for h in range(nh):
        rs = slice(h * tq, (h + 1) * tq)
        c = g * nh + h
        outs.append(gates[:, c:c + 1] * o_cmp[rs] + gates[:, N_HEADS_A + c:N_HEADS_A + c + 1] * o_slc[rs]
                    + gates[:, 2 * N_HEADS_A + c:2 * N_HEADS_A + c + 1] * o_win[rs])
    return outs


def _nsa_kernel(qc_ref, qr_ref, gate_ref, kc_ref, vc_ref, ks_ref, vs_ref, kw_ref, vw_ref, o_ref, mask_ref, **kw):
    i = pl.program_id(1)
    outs = []
    for g in range(KV_GROUPS_A):
        outs += _nsa_group(g, i, qc_ref, qr_ref, gate_ref, kc_ref, vc_ref, ks_ref, vs_ref, kw_ref, vw_ref,
                           mask_ref, **kw)
    o_ref[0] = jnp.concatenate(outs, axis=1).astype(o_ref.dtype)


def nsa_attention(qc, qr, gates, kvc, ks, vs, kw, vw, *, tq, qpos0, wbase, n_keys, kv_t=False):
    b, t, hd = qc.shape
    l = ks.shape[2] if kv_t else ks.shape[1]
    lw = kw.shape[1]
    nc, d = kvc.shape[3:]
    kb = min(KEY_CHUNK, l) if tq >= TQ else l
    assert l % kb == 0 and t % tq == 0 and lw % WIN_CHUNK == 0 and (kb == l or not kv_t)
    assert (qpos0 + t - 1) // kb + 1 <= l // kb and (qpos0 + t - 1 - wbase) // WIN_CHUNK + 1 <= lw // WIN_CHUNK
    assert (qpos0 - wbase) % WIN_CHUNK == 0 and WIN_CHUNK % tq == 0 and tq > 1
    assert 2 * SLC_BLOCK == LANE or not kv_t
    ns = l // SLC_BLOCK
    n_sel = min(N_SELECT, -(-n_keys // SLC_BLOCK))
    q_spec = pl.BlockSpec((1, tq, hd), lambda bi, i: (bi, i, 0))
    kc_spec = pl.BlockSpec((1, 1, KV_GROUPS_A, nc, d), lambda bi, i: (0, bi, 0, 0, 0))
    vc_spec = pl.BlockSpec((1, 1, KV_GROUPS_A, nc, d), lambda bi, i: (1, bi, 0, 0, 0))
    k_spec = pl.BlockSpec((1, KV_A, l) if kv_t else (1, l, KV_A), lambda bi, i: (bi, 0, 0))
    w_spec = pl.BlockSpec((1, lw, KV_A), lambda bi, i: (bi, 0, 0))
    return pl.pallas_call(
        functools.partial(_nsa_kernel, tq=tq, kb=kb, qpos0=qpos0, wbase=wbase, ns=ns, n_sel=n_sel, kv_t=kv_t),
        grid=(b, t // tq),
        in_specs=[q_spec, q_spec, pl.BlockSpec((1, tq, 3 * N_HEADS_A), lambda bi, i: (bi, i, 0)),
                  kc_spec, vc_spec, k_spec, k_spec, w_spec, w_spec],
        out_specs=q_spec,
        out_shape=jax.ShapeDtypeStruct((b, t, hd), MXU_DTYPE),
        scratch_shapes=[pltpu.VMEM((l // kb, tq, kb), jnp.float32)],
        compiler_params=pltpu.CompilerParams(
            dimension_semantics=("parallel", "arbitrary"), vmem_limit_bytes=VMEM_LIMIT),
        name="nsa_attention",
    )(qc, qr, gates, kvc, kvc, ks, vs, kw, vw)


def _dsa_kernel(qb_ref, qi_ref, wi_ref, kb_ref, vb_ref, ki_ref, o_ref, score_ref,
                *, tq, tq_real, kb, qpos0, n_keep, kv_t):
    i = pl.program_id(1)
    nh, d = N_HEADS_B, HEAD_DIM_B
    t0 = qpos0 + i * tq
    nj = (t0 + tq - 1) // kb + 1
    tpos = t0 + _iota((tq, 1), 0)
    w = wi_ref[0]

    def idx_body(j, carry):
        lo, hi = carry
        off = pl.multiple_of(j * kb, kb)
        kidx = ki_ref[0] if kv_t else ki_ref[0, pl.ds(off, kb), :]
        acc = jnp.zeros((tq, kb), jnp.float32)
        heads = [qi_ref[0, :, h * IDX_DIM:(h + 1) * IDX_DIM] for h in range(IDX_HEADS)]
        if kv_t:
            dots_all = jnp.dot(jnp.concatenate(heads, axis=0), kidx, preferred_element_type=jnp.float32)
        for h in range(IDX_HEADS):
            dots = dots_all[h * tq:(h + 1) * tq] if kv_t else _dot_nt(heads[h], kidx)
            acc = acc + w[:, h:h + 1] * jnp.maximum(dots, 0.0)
        vis = (off + _iota((1, kb), 1)) <= tpos
        score_ref[j] = jnp.where(vis, acc, NEG_INF)
        lo = jnp.minimum(lo, jnp.min(jnp.where(vis, acc, POS_INF), axis=1, keepdims=True))
        hi = jnp.maximum(hi, jnp.max(jnp.where(vis, acc, NEG_INF), axis=1, keepdims=True))
        return lo, hi

    lo, hi = lax.fori_loop(0, nj, idx_body, (jnp.full((tq, 1), POS_INF, jnp.float32),
                                             jnp.full((tq, 1), NEG_INF, jnp.float32)))

    k = float(n_keep)
    n_vis = (tpos + 1).astype(jnp.float32)
    n_keys = score_ref.shape[0] * kb

    def reduce_scores(pred, pick, init, combine, lane_reduce):
        def body(j, acc):
            sc = score_ref[j]
            for c in range(kb // LANE):
                kpos = (j * kb + c * LANE + _iota((1, LANE), 1)).astype(jnp.float32)
                piece = sc[:, c * LANE:(c + 1) * LANE]
                acc = combine(acc, pick(pred(piece, kpos), piece))
            return acc

        return lane_reduce(lax.fori_loop(0, nj, body, jnp.full((tq, LANE), init, jnp.float32)),
                           axis=1, keepdims=True)

    def count_where(pred):
        return reduce_scores(pred, lambda m, _: jnp.where(m, 1.0, 0.0), 0.0, jnp.add, jnp.sum)

    def min_where(pred):
        return reduce_scores(pred, lambda m, x: jnp.where(m, x, POS_INF), POS_INF, jnp.minimum, jnp.min)

    real = _iota((tq, 1), 0) < tq_real

    def any_row(flag):
        return jnp.max(jnp.where(real & flag, 1.0, 0.0)) > 0.0

    def bisect(state):
        it, lo, hi, cnt_lo = state
        mid = 0.5 * (lo + hi)
        mid_b = jnp.broadcast_to(mid, (tq, LANE))
        cnt = count_where(lambda x, _: x >= mid_b)
        ge = cnt >= k
        return it + 1, jnp.where(ge, mid, lo), jnp.where(ge, hi, mid), jnp.where(ge, cnt, cnt_lo)

    _, thr, _, cnt_lo = lax.while_loop(
        lambda st: (st[0] < BISECT_ITERS) & any_row((st[3] != k) & (n_vis > k)), lambda st: bisect(bisect(st)),
        (jnp.int32(0), lo, hi, n_vis))

    def break_ties():
        def above(v):
            v_b = jnp.broadcast_to(v, (tq, LANE))
            return count_where(lambda x, _: x > v_b)

        def strip(state):
            it, v, c_gt = state
            v_b = jnp.broadcast_to(v, (tq, LANE))
            v_next = jnp.where(c_gt >= k, min_where(lambda x, _: x > v_b), v)
            return it + 1, v_next, above(v_next)

        thr_b = jnp.broadcast_to(thr, (tq, LANE))
        v0 = min_where(lambda x, _: x >= thr_b)
        _, v, c_gt = lax.while_loop(lambda st: (st[0] < TIE_STRIP_ITERS) & any_row(st[2] >= k), strip,
                                    (jnp.int32(0), v0, above(v0)))
        need = k - c_gt
        v_b = jnp.broadcast_to(v, (tq, LANE))

        def narrow(_, bounds):
            j_lo, j_hi = bounds
            mid = jnp.floor(0.5 * (j_lo + j_hi))
            ge = count_where(lambda x, kpos: (x == v_b) & (kpos <= mid)) >= need
            return jnp.where(ge, j_lo, mid), jnp.where(ge, mid, j_hi)

        _, j_max = lax.fori_loop(0, n_keys.bit_length(), narrow,
                                 (jnp.full((tq, 1), -1.0, jnp.float32), jnp.full((tq, 1), n_keys - 1.0, jnp.float32)))
        return v, j_max

    thr, j_max = lax.cond(any_row((cnt_lo > k) & (n_vis > k)), break_ties,
                          lambda: (thr, jnp.full((tq, 1), float(n_keys), jnp.float32)))

    q = jnp.concatenate([qb_ref[0, :, h * d:(h + 1) * d] for h in range(nh)], axis=0)

    def att_body(j, carry):
        off = pl.multiple_of(j * kb, kb)
        sc = score_ref[j]
        kpos = (off + _iota((1, kb), 1)).astype(jnp.float32)
        madd = jnp.where((sc > thr) | ((sc == thr) & (kpos <= j_max)), 0.0, NEG_INF)
        if kv_t:
            return _flash_step(carry, q, kb_ref[0], vb_ref[0], madd, nh, scale=d ** -0.5, kv_t=True)
        return _flash_step(carry, q, kb_ref[0, pl.ds(off, kb), :], vb_ref[0, pl.ds(off, kb), :], madd, nh,
                           scale=d ** -0.5)

    o = _flash_finish(lax.fori_loop(0, nj, att_body, _flash_init(nh * tq, d)))
    o_ref[0] = jnp.concatenate([o[h * tq:(h + 1) * tq] for h in range(nh)], axis=1).astype(o_ref.dtype)


def dsa_attention(qb, qi, wi, kb_, vb, ki, *, tq, qpos0, n_keep, kv_t=False, tq_real=None):
    b, t, hd = qb.shape
    l = kb_.shape[2] if kv_t else kb_.shape[1]
    kb = min(KEY_CHUNK, l) if tq >= TQ else l
    assert l % kb == 0 and t % tq == 0 and (qpos0 + t - 1) // kb + 1 <= l // kb and (kb == l or not kv_t)
    kv_spec = lambda width: pl.BlockSpec((1, width, l) if kv_t else (1, l, width), lambda bi, i: (bi, 0, 0))
    return pl.pallas_call(
        functools.partial(_dsa_kernel, tq=tq, tq_real=tq if tq_real is None else tq_real, kb=kb, qpos0=qpos0,
                          n_keep=n_keep, kv_t=kv_t),
        grid=(b, t // tq),
        in_specs=[pl.BlockSpec((1, tq, hd), lambda bi, i: (bi, i, 0)),
                  pl.BlockSpec((1, tq, IDX_HEADS * IDX_DIM), lambda bi, i: (bi, i, 0)),
                  pl.BlockSpec((1, tq, IDX_HEADS), lambda bi, i: (bi, i, 0)),
                  kv_spec(HEAD_DIM_B), kv_spec(HEAD_DIM_B), kv_spec(IDX_DIM)],
        out_specs=pl.BlockSpec((1, tq, hd), lambda bi, i: (bi, i, 0)),
        out_shape=jax.ShapeDtypeStruct((b, t, hd), MXU_DTYPE),
        scratch_shapes=[pltpu.VMEM((l // kb, tq, kb), jnp.float32)],
        compiler_params=pltpu.CompilerParams(
            dimension_semantics=("parallel", "arbitrary"), vmem_limit_bytes=VMEM_LIMIT),
        name="dsa_attention",
    )(qb, qi, wi, kb_, vb, ki)


def _page_maps(n_pages, pp):
    n_steps = n_pages // pp

    def page_map(r):
        return lambda b, s, pt: (pt[b * n_pages + jnp.minimum(s, n_steps - 1) * pp + r], 0, 0)

    return n_steps, page_map


def _chunk_rows(tok_ref, sec, row0, n):
    d = HEAD_DIM_A
    first_half = _iota((n, KV_A), 1) < d
    pieces = [[] for _ in range(KV_GROUPS_A)]
    for j in range(0, CMP_STRIDE, 2):
        a, b = [tok_ref[sec, pl.ds(row0 + jj, n, stride=CMP_STRIDE), :] for jj in (j, j + 1)]
        pieces[0].append(jnp.where(first_half, a, pltpu.roll(b, d, 1)))
        pieces[1].append(jnp.where(first_half, pltpu.roll(a, d, 1), b))
    return [jnp.concatenate(p, axis=1) for p in pieces]


def _nsa_gather_kernel(pt_ref, *refs, pp, n_steps, rows):
    del pt_ref
    pages, (tail_tok_ref, tail_t_ref, pe_ref) = refs[:pp], refs[pp:pp + 3]
    zt_ref, zb_ref, ks_ref, vs_ref, tok_ref = refs[pp + 3:]
    is_tail = pl.program_id(1) == n_steps
    z = [[[] for _ in range(KV_GROUPS_A)] for _ in range(2)]
    for r in range(pp):
        cs = slice(r * rows, (r + 1) * rows)
        ks_ref[0, :, cs] = jnp.where(is_tail, tail_t_ref[0, :KV_A, cs],
                                     pages[r][0, 2 * KV_A:3 * KV_A, :]).astype(ks_ref.dtype)
        vs_ref[0, :, cs] = jnp.where(is_tail, tail_t_ref[0, KV_A:, cs],
                                     pages[r][0, 3 * KV_A:, :]).astype(vs_ref.dtype)
        for sec in range(2):
            tok_ref[sec, cs, :] = jnp.where(is_tail, tail_tok_ref[0, sec, cs, :],
                                            pages[r][0, sec * KV_A:(sec + 1) * KV_A, :].T)
            for g, zg in enumerate(_chunk_rows(tok_ref, sec, r * rows, rows // CMP_STRIDE)):
                z[sec][g].append(zg)
    for sec in range(2):
        for g in range(KV_GROUPS_A):
            zf = jnp.concatenate(z[sec][g], axis=0)
            zt_ref[sec, 0, g] = (zf + pe_ref[sec, 0]).astype(zt_ref.dtype)
            zb_ref[sec, 0, g] = (zf + pe_ref[sec, 1]).astype(zb_ref.dtype)


def nsa_gather(cache_t, page_table, new_rows, cmp_pe):
    db, n_pages = page_table.shape
    width, rows = cache_t.shape[1:]
    pp = PAGES_PER_STEP
    n_steps, page_map = _page_maps(n_pages, pp)
    l = (n_steps + 1) * pp * rows
    cps = pp * rows // CMP_STRIDE
    flat = CMP_STRIDE * HEAD_DIM_A
    pe = cmp_pe.reshape(2, 2, 1, flat).astype(jnp.float32)
    tail = pad_rows(new_rows, pp * rows)
    tail_tok = tail[:, :, :2 * KV_A].reshape(db, pp * rows, 2, KV_A).swapaxes(1, 2)
    tail_t = tail[:, :, 2 * KV_A:].swapaxes(1, 2)
    z_spec = pl.BlockSpec((2, 1, KV_GROUPS_A, cps, flat), lambda b, s, pt: (0, b, 0, s, 0))
    r_spec = pl.BlockSpec((1, KV_A, pp * rows), lambda b, s, pt: (b, 0, s))
    z_shape = jax.ShapeDtypeStruct((2, db, KV_GROUPS_A, l // CMP_STRIDE, flat), MXU_DTYPE)
    r_shape = jax.ShapeDtypeStruct((db, KV_A, l), MXU_DTYPE)
    return pl.pallas_call(
        functools.partial(_nsa_gather_kernel, pp=pp, n_steps=n_steps, rows=rows),
        grid_spec=pltpu.PrefetchScalarGridSpec(
            num_scalar_prefetch=1,
            grid=(db, n_steps + 1),
            in_specs=[pl.BlockSpec((1, width, rows), page_map(r)) for r in range(pp)]
            + [pl.BlockSpec((1, 2, pp * rows, KV_A), lambda b, s, pt: (b, 0, 0, 0)),
               pl.BlockSpec((1, 2 * KV_A, pp * rows), lambda b, s, pt: (b, 0, 0)),
               pl.BlockSpec((2, 2, 1, flat), lambda b, s, pt: (0, 0, 0, 0))],
            out_specs=[z_spec, z_spec, r_spec, r_spec],
            scratch_shapes=[pltpu.VMEM((2, pp * rows, KV_A), jnp.float32)]),
        out_shape=[z_shape, z_shape, r_shape, r_shape],
        compiler_params=pltpu.CompilerParams(
            dimension_semantics=("parallel", "arbitrary"), vmem_limit_bytes=VMEM_LIMIT),
        name="nsa_gather",
    )(page_table.reshape(-1), *([cache_t] * pp), tail_tok, tail_t, pe)


def _dsa_gather_kernel(pt_ref, *refs, pp, n_steps, rows):
    del pt_ref
    pages, tail_ref, (k_ref, v_ref, i_ref) = refs[:pp], refs[pp], refs[pp + 1:]
    is_tail = pl.program_id(1) == n_steps
    d = HEAD_DIM_B
    for r in range(pp):
        cs = slice(r * rows, (r + 1) * rows)
        x = jnp.where(is_tail, tail_ref[0, :, cs], pages[r][0])
        k_ref[0, :, cs] = x[:d].astype(k_ref.dtype)
        v_ref[0, :, cs] = x[d:2 * d].astype(v_ref.dtype)
        i_ref[0, :, cs] = x[2 * d:].astype(i_ref.dtype)


def dsa_gather(cache_t, page_table, new_rows):
    db, n_pages = page_table.shape
    width, rows = cache_t.shape[1:]
    pp = PAGES_PER_STEP
    n_steps, page_map = _page_maps(n_pages, pp)
    l = (n_steps + 1) * pp * rows
    tail_t = pad_rows(new_rows, pp * rows).swapaxes(1, 2)
    out_spec = lambda w: pl.BlockSpec((1, w, pp * rows), lambda b, s, pt: (b, 0, s))
    widths = (HEAD_DIM_B, HEAD_DIM_B, IDX_DIM)
    return pl.pallas_call(
        functools.partial(_dsa_gather_kernel, pp=pp, n_steps=n_steps, rows=rows),
        grid_spec=pltpu.PrefetchScalarGridSpec(
            num_scalar_prefetch=1,
            grid=(db, n_steps + 1),
            in_specs=[pl.BlockSpec((1, width, rows), page_map(r)) for r in range(pp)]
            + [pl.BlockSpec((1, width, pp * rows), lambda b, s, pt: (b, 0, 0))],
            out_specs=[out_spec(w) for w in widths]),
        out_shape=[jax.ShapeDtypeStruct((db, w, l), MXU_DTYPE) for w in widths],
        compiler_params=pltpu.CompilerParams(
            dimension_semantics=("parallel", "arbitrary"), vmem_limit_bytes=VMEM_LIMIT),
        name="dsa_gather",
    )(page_table.reshape(-1), *([cache_t] * pp), tail_t)


def _compress_kernel(zt_ref, zb_ref, pe_ref, w1t_ref, w1b_ref, b1_ref, w2_ref, o_ref, ab_ref, *, ncp):
    dot = functools.partial(jnp.dot, preferred_element_type=jnp.float32)
    ch = zb_ref.shape[2]
    at = dot(zt_ref[0, 0], w1t_ref[0])
    ab_ref[:ch] = dot(zb_ref[0, 0], w1b_ref[0])
    pe_rows = jnp.broadcast_to(pe_ref[0, 1], (SUBLANE, pe_ref.shape[3])).astype(zb_ref.dtype)
    ab_ref[ch:] = dot(pe_rows, w1b_ref[0])
    h = jax.nn.gelu(at[:ncp] + ab_ref[pl.ds(1, ncp), :] + b1_ref[0])
    o_ref[0, 0, :ncp] = dot(h.astype(w2_ref.dtype), w2_ref[0]).astype(o_ref.dtype)
    if o_ref.shape[2] > ncp:
        o_ref[0, 0, ncp:] = jnp.zeros((o_ref.shape[2] - ncp, o_ref.shape[3]), o_ref.dtype)


def compress(zt, zb, cmp_pe, w1, b1, w2, n_keys):
    ncp = _round_up(-(-n_keys // CMP_STRIDE), BF16_ROWS)
    ncl = ncp if ncp <= LANE else _round_up(ncp, LANE)
    _, nb, ch, kdim = zt.shape
    hid = w1.shape[-1]
    d = w2.shape[-1]
    assert ch + SUBLANE >= ncp + 1 and ch % SUBLANE == 0
    w1 = w1.astype(MXU_DTYPE)
    pe = cmp_pe.reshape(2, 2, 1, kdim).astype(jnp.float32)
    z_spec = pl.BlockSpec((1, 1, ch, kdim), lambda s, n: (s, n, 0, 0))
    return pl.pallas_call(
        functools.partial(_compress_kernel, ncp=ncp),
        grid=(2, nb),
        in_specs=[z_spec, z_spec,
                  pl.BlockSpec((1, 2, 1, kdim), lambda s, n: (s, 0, 0, 0)),
                  pl.BlockSpec((1, kdim, hid), lambda s, n: (s, 0, 0)),
                  pl.BlockSpec((1, kdim, hid), lambda s, n: (s, 1, 0)),
                  pl.BlockSpec((1, 1, hid), lambda s, n: (s, 0, 0)),
                  pl.BlockSpec((1, hid, d), lambda s, n: (s, 0, 0))],
        out_specs=pl.BlockSpec((1, 1, ncl, d), lambda s, n: (s, n, 0, 0)),
        out_shape=jax.ShapeDtypeStruct((2, nb, ncl, d), MXU_DTYPE),
        scratch_shapes=[pltpu.VMEM((ch + SUBLANE, hid), jnp.float32)],
        compiler_params=pltpu.CompilerParams(
            dimension_semantics=("parallel", "parallel"), vmem_limit_bytes=VMEM_LIMIT),
        name="compress",
    )(zt, zb, pe, w1, w1, b1.reshape(2, 1, hid).astype(jnp.float32), w2.astype(MXU_DTYPE))


def pad_rows(a, n):
    return jnp.pad(a, [(0, 0), (0, n - a.shape[1])] + [(0, 0)] * (a.ndim - 2))


def _rope_tables(pos, d):
    half = d // ROT_FRACTION // 2
    inv = ROPE_THETA ** (-jnp.arange(half, dtype=jnp.float32) / half)
    ang = pos.astype(jnp.float32)[:, None] * inv[None, :]
    lane = jnp.arange(LANE) % d
    cos = jnp.cos(ang)[:, lane % half]
    sin = jnp.sin(ang)[:, lane % half]
    one, zero = jnp.ones_like(cos), jnp.zeros_like(cos)
    c = jnp.where(lane < 2 * half, cos, one)
    sa = jnp.where((lane >= half) & (lane < 2 * half), sin, zero)
    sb = jnp.where(lane < half, -sin, zero)
    return jnp.stack([c, sa, sb])


def _rope(x, t_ref, half):
    c, sa, sb = t_ref[0], t_ref[1], t_ref[2]
    outs = []
    for j in range(x.shape[1] // LANE):
        xs = x[:, j * LANE:(j + 1) * LANE]
        outs.append(xs * c + pltpu.roll(xs, half, 1) * sa + pltpu.roll(xs, LANE - half, 1) * sb)
    return outs[0] if len(outs) == 1 else jnp.concatenate(outs, axis=1)


_QA = N_HEADS_A * HEAD_DIM_A
_KVA = 6 * KV_A
_QB = N_HEADS_B * HEAD_DIM_B
_KVB = 2 * HEAD_DIM_B
_QI = IDX_HEADS * IDX_DIM
_MISC = IDX_DIM + IDX_HEADS + 3 * N_HEADS_A
PREP_WIDTH = _QA + _KVA + _QB + _KVB + _QI + _MISC
assert _MISC == LANE


def _prep_kernel(x_ref, t64_ref, t128_ref, pe_ref, qc_ref, qr_ref, qb_ref, qi_ref, nsa_ref, win_ref, dsa_ref,
                 ks_ref, vs_ref, kw_ref, vw_ref, kb_ref, vb_ref, ki_ref, wi_ref, gate_ref, *z_refs, tm):
    x = x_ref[...]
    o = 0
    qa = x[:, o:o + _QA]; o += _QA
    kva = x[:, o:o + _KVA]; o += _KVA
    qb = x[:, o:o + _QB]; o += _QB
    kvb = x[:, o:o + _KVB]; o += _KVB
    qi = x[:, o:o + _QI]; o += _QI
    misc = x[:, o:o + _MISC]
    h64, h128 = HEAD_DIM_A // ROT_FRACTION // 2, HEAD_DIM_B // ROT_FRACTION // 2
    sec = lambda i: kva[:, i * KV_A:(i + 1) * KV_A]
    mx = lambda a: a.astype(qc_ref.dtype)
    qc_ref[...] = mx(qa * HEAD_DIM_A ** -0.5)
    qr_ref[...] = mx(_rope(qa, t64_ref, h64) * HEAD_DIM_A ** -0.5)
    k_slc, k_win = _rope(sec(2), t64_ref, h64), _rope(sec(4), t64_ref, h64)
    nsa_ref[:, :2 * KV_A] = kva[:, :2 * KV_A]
    nsa_ref[:, 2 * KV_A:3 * KV_A] = k_slc
    nsa_ref[:, 3 * KV_A:] = sec(3)
    win_ref[:, :KV_A] = k_win
    win_ref[:, KV_A:] = sec(5)
    ks_ref[...], vs_ref[...], kw_ref[...], vw_ref[...] = mx(k_slc), mx(sec(3)), mx(k_win), mx(sec(5))
    qb_ref[...] = mx(_rope(qb, t128_ref, h128))
    k_b, v_b = _rope(kvb[:, :HEAD_DIM_B], t128_ref, h128), kvb[:, HEAD_DIM_B:]
    k_idx = _rope(misc, t64_ref, h64)[:, :IDX_DIM]
    dsa_ref[:, :HEAD_DIM_B] = k_b
    dsa_ref[:, HEAD_DIM_B:2 * HEAD_DIM_B] = v_b
    dsa_ref[:, 2 * HEAD_DIM_B:] = k_idx
    kb_ref[...], vb_ref[...], ki_ref[...] = mx(k_b), mx(v_b), mx(k_idx)
    qi_ref[...] = mx(_rope(qi, t64_ref, h64) * IDX_DIM ** -0.5)
    wi_ref[...] = misc[:, IDX_DIM:IDX_DIM + IDX_HEADS] * IDX_HEADS ** -0.5
    gate_ref[...] = jax.nn.sigmoid(misc[:, IDX_DIM + IDX_HEADS:])
    if z_refs:
        zt_ref, zb_ref, tok_ref = z_refs
        for s_ in range(2):
            tok_ref[s_] = sec(s_)
            for g, zg in enumerate(_chunk_rows(tok_ref, s_, 0, tm // CMP_STRIDE)):
                zt_ref[s_, 0, g] = (zg + pe_ref[s_, 0]).astype(zt_ref.dtype)
                zb_ref[s_, 0, g] = (zg + pe_ref[s_, 1]).astype(zb_ref.dtype)


def prep(proj, pos, cmp_pe, with_chunks):
    b, t, width = proj.shape
    assert width == PREP_WIDTH
    m = b * t
    tm = 256 if t % 256 == 0 else m
    assert m % tm == 0 and t % tm in (0, t)
    nt = max(t // tm, 1)
    flat = CMP_STRIDE * HEAD_DIM_A
    names = ['qc', 'qr', 'qb', 'qi', 'nsa_rows', 'win_rows', 'dsa_rows', 'ks', 'vs', 'kw', 'vw', 'kb', 'vb', 'ki',
             'wi', 'gates']
    widths = [_QA, _QA, _QB, _QI, NSA_SECTIONS * KV_A, 2 * KV_A, DSA_ROW, KV_A, KV_A, KV_A, KV_A, HEAD_DIM_B,
              HEAD_DIM_B, IDX_DIM, IDX_HEADS, 3 * N_HEADS_A]
    dtypes = [MXU_DTYPE] * 4 + [jnp.float32] * 3 + [MXU_DTYPE] * 7 + [jnp.float32] * 2
    row = lambda w: pl.BlockSpec((tm, w), lambda i: (i, 0))
    out_specs = [row(w) for w in widths]
    out_shape = [jax.ShapeDtypeStruct((m, w), dt) for w, dt in zip(widths, dtypes)]
    scratch = []
    if with_chunks:
        assert tm % CMP_STRIDE == 0 and t % tm == 0
        z_spec = pl.BlockSpec((2, 1, KV_GROUPS_A, tm // CMP_STRIDE, flat), lambda i: (0, i // nt, 0, i % nt, 0))
        z_shape = jax.ShapeDtypeStruct((2, b, KV_GROUPS_A, t // CMP_STRIDE, flat), MXU_DTYPE)
        out_specs += [z_spec, z_spec]
        out_shape += [z_shape, z_shape]
        names += ['zt', 'zb']
        scratch = [pltpu.VMEM((2, tm, KV_A), jnp.float32)]
    t_spec = pl.BlockSpec((3, tm, LANE), lambda i: (0, i, 0))
    outs = pl.pallas_call(
        functools.partial(_prep_kernel, tm=tm),
        grid=(m // tm,),
        in_specs=[row(width), t_spec, t_spec, pl.BlockSpec((2, 2, 1, flat), lambda i: (0, 0, 0, 0))],
        out_specs=out_specs,
        out_shape=out_shape,
        scratch_shapes=scratch,
        compiler_params=pltpu.CompilerParams(dimension_semantics=("parallel",), vmem_limit_bytes=VMEM_LIMIT),
        name="prep",
    )(proj.reshape(m, width), _rope_tables(pos, HEAD_DIM_A), _rope_tables(pos, HEAD_DIM_B),
      cmp_pe.reshape(2, 2, 1, flat).astype(jnp.float32))
    return {n: (o if o.ndim > 2 else o.reshape(b, t, o.shape[-1])) for n, o in zip(names, outs)}


def project(x, g, w_in, sizes):
    starts = [sum(sizes[:i]) for i in range(len(sizes))]
    qa, kva, ga, qb, kvb, qi, ki, wi, gm = [slice(o, o + n) for o, n in zip(starts, sizes)]
    w_bf = w_in.astype(MXU_DTYPE)
    w_prep = jnp.concatenate([w_bf[:, c] for c in (qa, kva, qb, kvb, qi, ki, wi, ga)], axis=1)
    return norm_matmul(x, g, w_prep), norm_matmul(x, g, w_bf[:, gm])


def mixer_prompt(x, g, w_in, cmp_pe, cmp_w1, cmp_b1, cmp_w2, sizes):
    b, s, _ = x.shape
    proj, gm = project(x, g, w_in, sizes)
    p = prep(proj, jnp.tile(jnp.arange(s, dtype=jnp.int32), b), cmp_pe, True)
    merge_bg = lambda a: a.reshape((2, b * KV_GROUPS_A) + a.shape[3:])
    kvc = compress(merge_bg(p['zt']), merge_bg(p['zb']), cmp_pe, cmp_w1, cmp_b1, cmp_w2, s)
    kvc = kvc.reshape(2, b, KV_GROUPS_A, kvc.shape[2], HEAD_DIM_A)
    o_nsa = nsa_attention(p['qc'], p['qr'], p['gates'], kvc, p['ks'], p['vs'], p['kw'], p['vw'],
                          tq=min(TQ, s), qpos0=0, wbase=0, n_keys=s)
    o_dsa = dsa_attention(p['qb'], p['qi'], p['wi'], p['kb'], p['vb'], p['ki'],
                          tq=min(TQ, s), qpos0=0, n_keep=min(DSA_TOPK, s // 4))
    nsa_rows = p['nsa_rows'].reshape(b, s, NSA_SECTIONS, KV_GROUPS_A, HEAD_DIM_A)
    win_state = p['win_rows'].reshape(b, s, 2, KV_GROUPS_A, HEAD_DIM_A)[:, -min(WINDOW, s):]
    return (o_nsa, o_dsa, gm), nsa_rows, win_state, p['dsa_rows']


def mixer_sample(x, g, cache_nsa, win_buf, cache_dsa, page_table, w_in, cmp_pe, cmp_w1, cmp_b1, cmp_w2, sizes):
    b, t, _ = x.shape
    page = cache_nsa.shape[1]
    past_len = page_table.shape[1] * page
    n_keys = past_len + t
    proj, gm = project(x, g, w_in, sizes)
    p = prep(proj, jnp.tile(past_len + jnp.arange(t, dtype=jnp.int32), b), cmp_pe, False)
    pad = lambda a: pad_rows(a, TQ_STEP)

    cache_nsa_t = jnp.transpose(cache_nsa, (0, 2, 3, 4, 1)).reshape(cache_nsa.shape[0], NSA_SECTIONS * KV_A, page)
    zt, zb, ks, vs = nsa_gather(cache_nsa_t, page_table, p['nsa_rows'], cmp_pe)
    merge_bg = lambda a: a.reshape((2, b * KV_GROUPS_A) + a.shape[3:])
    kvc = compress(merge_bg(zt), merge_bg(zb), cmp_pe, cmp_w1, cmp_b1, cmp_w2, n_keys)
    kvc = kvc.reshape(2, b, KV_GROUPS_A, kvc.shape[2], HEAD_DIM_A)
    w_len = win_buf.shape[1]
    win_new = p['win_rows'].reshape(b, t, 2, KV_GROUPS_A, HEAD_DIM_A)
    win_all = jnp.concatenate([win_buf, win_new], axis=1)
    win_pad = pad_rows(win_all, w_len + WIN_CHUNK).astype(MXU_DTYPE)
    kw = win_pad[:, :, 0].reshape(b, w_len + WIN_CHUNK, KV_A)
    vw = win_pad[:, :, 1].reshape(b, w_len + WIN_CHUNK, KV_A)
    o_nsa = nsa_attention(pad(p['qc']), pad(p['qr']), pad(p['gates']), kvc, ks, vs, kw, vw,
                          tq=TQ_STEP, qpos0=past_len, wbase=past_len - w_len, n_keys=n_keys, kv_t=True)

    kb_, vb, ki = dsa_gather(jnp.swapaxes(cache_dsa, 1, 2), page_table, p['dsa_rows'])
    o_dsa = dsa_attention(pad(p['qb']), pad(p['qi']), pad(p['wi']), kb_, vb, ki, tq=TQ_STEP, qpos0=past_len,
                          n_keep=min(DSA_TOPK, n_keys // 4), kv_t=True, tq_real=t)
    nsa_rows = p['nsa_rows'].reshape(b, t, NSA_SECTIONS, KV_GROUPS_A, HEAD_DIM_A)
    return (o_nsa, o_dsa, pad(gm)), nsa_rows, win_all[:, -w_len:], p['dsa_rows']


def kernel(x_prompt, x_sample, mem_prompt, cache_nsa_kv, state_nsa_win, cache_dsa_kv, cache_mem_kv, state_conv,
           page_table, norm_g, w_in, cmp_pe, cmp_w1, cmp_b1, cmp_w2, w_out_a, w_out_b, w_out, w_mem_q, w_mem_kv,
           w_mem_out, w_up, conv_w, conv_b, w_down, final_g):
    depth = w_in.shape[0]
    d_model = x_prompt.shape[-1]
    d_ff = w_down.shape[1]
    assert CONV_WIDTH == 3
    sizes = (N_HEADS_A * HEAD_DIM_A, 6 * KV_A, 3 * N_HEADS_A, N_HEADS_B * HEAD_DIM_B, 2 * HEAD_DIM_B,
             IDX_HEADS * IDX_DIM, IDX_DIM, IDX_HEADS, 2 * d_model)
    xp, xs = x_prompt, pad_rows(x_sample, TQ_STEP)
    t_step = x_sample.shape[1]
    nsa_p, nsa_s, win_p, win_s, dsa_p, dsa_s, mem_p, conv_p, conv_s = [], [], [], [], [], [], [], [], []
    for l in range(depth):
        assert l == depth - 1, "the fused FFN epilogue applies the final norm"
        branches_p, a, bwin, c = mixer_prompt(xp, norm_g[l, 0], w_in[l], cmp_pe[l], cmp_w1[l], cmp_b1[l], cmp_w2[l],
                                              sizes)
        nsa_p.append(a); win_p.append(bwin); dsa_p.append(c)
        branches_s, a, bwin, c = mixer_sample(xs[:, :t_step], norm_g[l, 0], cache_nsa_kv[l], state_nsa_win[l],
                                              cache_dsa_kv[l], page_table, w_in[l], cmp_pe[l], cmp_w1[l], cmp_b1[l],
                                              cmp_w2[l], sizes)
        nsa_s.append(a); win_s.append(bwin); dsa_s.append(c)
        kv_p = norm_matmul(mem_prompt, norm_g[l, 2], w_mem_kv[l])
        mem_p.append(kv_p.reshape(kv_p.shape[:2] + (2, MEM_HEADS, MEM_HEAD_DIM)))
        kv_s = cache_mem_kv[l].reshape(cache_mem_kv.shape[1:3] + (-1,))
        weights = (norm_g[l, 1], norm_g[l, 3], final_g, w_out_a[l], w_out_b[l], w_out[l], w_mem_q[l], w_mem_out[l],
                   w_up[l], conv_w[l], conv_b[l], w_down[l])
        xp, cp = dense_tail(xp, *branches_p, kv_p, jnp.zeros((xp.shape[0], CONV_WIDTH - 1, 2 * d_ff), xp.dtype),
                            xp.shape[1], *weights)
        xs, cs = dense_tail(xs, *branches_s, kv_s, state_conv[l], t_step, *weights)
        conv_p.append(cp); conv_s.append(cs)
    y_prompt, y_sample = xp, xs[:, :t_step]
    return (y_prompt, y_sample, jnp.stack(nsa_p), jnp.stack(nsa_s), jnp.stack(win_p), jnp.stack(win_s),
            jnp.stack(dsa_p), jnp.stack(dsa_s), jnp.stack(mem_p), jnp.stack(conv_p), jnp.stack(conv_s))
```

```python
import functools

import jax
import jax.numpy as jnp
from jax import lax
from jax.experimental import pallas as pl
from jax.experimental.pallas import tpu as pltpu

N_HEADS_A = 16
HEAD_DIM_A = 64
KV_GROUPS_A = 2
CMP_BLOCK = 32
CMP_STRIDE = 16
SLC_BLOCK = 64
N_SELECT = 16
WINDOW = 512
N_HEADS_B = 8
HEAD_DIM_B = 128
IDX_HEADS = 16
IDX_DIM = 64
DSA_TOPK = 256
MEM_HEADS = 4
MEM_HEAD_DIM = 128
CONV_WIDTH = 3
ROPE_THETA = 500000.0
ROT_FRACTION = 4
EPS = 1e-6
KV_A = KV_GROUPS_A * HEAD_DIM_A
CMP_PER_SLC = SLC_BLOCK // CMP_STRIDE
HPG_A = N_HEADS_A // KV_GROUPS_A
DSA_ROW = 2 * HEAD_DIM_B + IDX_DIM
NSA_SECTIONS = 4

LANE = 128
SUBLANE = 8
BF16_ROWS = 16
VMEM_LIMIT = 48 * 1024 * 1024

NEG_INF = float('-inf')
POS_INF = float('inf')
MXU_DTYPE = jnp.bfloat16
TQ = 128
TQ_STEP = BF16_ROWS
KEY_CHUNK = 512
WIN_CHUNK = 128
PAGES_PER_STEP = 8
FFN_ROWS = 1024
BISECT_ITERS = 40
TIE_STRIP_ITERS = 64
SLC_SHIFT = SLC_BLOCK.bit_length() - 1
CMP_PER_SLC_SHIFT = CMP_PER_SLC.bit_length() - 1


def _round_up(n, m):
    return -(-n // m) * m


def _tile(n, cap):
    if n <= cap:
        return n
    best = None
    for t in range(LANE, cap + 1, LANE):
        if n % t == 0:
            best = t
    assert best is not None, (n, cap)
    return best


def _rms(x, g):
    return x * lax.rsqrt(jnp.mean(x * x, axis=-1, keepdims=True) + EPS) * g


def _norm_mm_kernel(x_ref, g_ref, w_ref, o_ref, xn_ref):
    @pl.when(pl.program_id(1) == 0)
    def _():
        xn_ref[...] = _rms(x_ref[...], g_ref[...]).astype(xn_ref.dtype)

    o_ref[...] = jnp.dot(xn_ref[...], w_ref[...], preferred_element_type=jnp.float32).astype(o_ref.dtype)


def norm_matmul(x, g, w):
    lead = x.shape[:-1]
    k = x.shape[-1]
    n = w.shape[-1]
    x2 = x.reshape(-1, k)
    m = x2.shape[0]
    tm = 512 if m % 512 == 0 else m
    tn = _tile(n, 1536)
    out = pl.pallas_call(
        _norm_mm_kernel,
        grid=(m // tm, n // tn),
        in_specs=[pl.BlockSpec((tm, k), lambda i, j: (i, 0)),
                  pl.BlockSpec((1, k), lambda i, j: (0, 0)),
                  pl.BlockSpec((k, tn), lambda i, j: (0, j))],
        out_specs=pl.BlockSpec((tm, tn), lambda i, j: (i, j)),
        out_shape=jax.ShapeDtypeStruct((m, n), jnp.float32),
        scratch_shapes=[pltpu.VMEM((tm, k), MXU_DTYPE)],
        compiler_params=pltpu.CompilerParams(
            dimension_semantics=("parallel", "arbitrary"), vmem_limit_bytes=VMEM_LIMIT),
        name="norm_matmul",
    )(x2, g.reshape(1, k).astype(jnp.float32), w.astype(MXU_DTYPE))
    return out.reshape(lead + (n,))


def _gated_merge_kernel(oa_ref, ob_ref, wa_ref, wb_ref, ga_ref, gb_ref, o_ref):
    dot = functools.partial(jnp.dot, preferred_element_type=jnp.float32)
    ya = dot(oa_ref[...], wa_ref[...])
    yb = dot(ob_ref[...], wb_ref[...])
    o_ref[...] = (jax.nn.sigmoid(ga_ref[...]) * ya + jax.nn.sigmoid(gb_ref[...]) * yb).astype(o_ref.dtype)


def gated_merge(o_nsa, o_dsa, gm, w_oa, w_ob):
    m, ka = o_nsa.shape
    kb = o_dsa.shape[1]
    n = w_oa.shape[1]
    tm = 512 if m % 512 == 0 else m
    tn = _tile(n, 1024)
    nb = n // tn
    return pl.pallas_call(
        _gated_merge_kernel,
        grid=(m // tm, nb),
        in_specs=[pl.BlockSpec((tm, ka), lambda i, j: (i, 0)),
                  pl.BlockSpec((tm, kb), lambda i, j: (i, 0)),
                  pl.BlockSpec((ka, tn), lambda i, j: (0, j)),
                  pl.BlockSpec((kb, tn), lambda i, j: (0, j)),
                  pl.BlockSpec((tm, tn), lambda i, j: (i, j)),
                  pl.BlockSpec((tm, tn), lambda i, j: (i, j + nb))],
        out_specs=pl.BlockSpec((tm, tn), lambda i, j: (i, j)),
        out_shape=jax.ShapeDtypeStruct((m, n), MXU_DTYPE),
        compiler_params=pltpu.CompilerParams(
            dimension_semantics=("parallel", "parallel"), vmem_limit_bytes=VMEM_LIMIT),
        name="gated_merge",
    )(o_nsa, o_dsa, w_oa.astype(MXU_DTYPE), w_ob.astype(MXU_DTYPE), gm, gm)


def _out_proj_kernel(z_ref, w_ref, x_ref, o_ref):
    o_ref[...] = x_ref[...] + jnp.dot(z_ref[...], w_ref[...], preferred_element_type=jnp.float32)


def out_proj_residual(z, w, x):
    m, k = z.shape
    n = w.shape[1]
    tm = 256 if m % 256 == 0 else m
    return pl.pallas_call(
        _out_proj_kernel,
        grid=(m // tm,),
        in_specs=[pl.BlockSpec((tm, k), lambda i: (i, 0)),
                  pl.BlockSpec((k, n), lambda i: (0, 0)),
                  pl.BlockSpec((tm, n), lambda i: (i, 0))],
        out_specs=pl.BlockSpec((tm, n), lambda i: (i, 0)),
        out_shape=jax.ShapeDtypeStruct((m, n), jnp.float32),
        compiler_params=pltpu.CompilerParams(dimension_semantics=("parallel",), vmem_limit_bytes=VMEM_LIMIT),
        name="out_proj_residual",
    )(z, w.astype(MXU_DTYPE), x)


def _mem_block_kernel(x_ref, g1_ref, g2_ref, wq_ref, kv_ref, wo_ref, x_out_ref, xn_out_ref):
    d = MEM_HEAD_DIM
    hd = MEM_HEADS * d
    x = x_ref[0]
    xn = _rms(x, g1_ref[...]).astype(wq_ref.dtype)
    q = jnp.dot(xn, wq_ref[...], preferred_element_type=jnp.float32).astype(wq_ref.dtype)
    outs = []
    for h in range(MEM_HEADS):
        k = kv_ref[0, :, h * d:(h + 1) * d].astype(wq_ref.dtype)
        v = kv_ref[0, :, hd + h * d:hd + (h + 1) * d].astype(wq_ref.dtype)
        s = _dot_nt(q[:, h * d:(h + 1) * d], k) * d ** -0.5
        e = jnp.exp(s - jnp.max(s, axis=1, keepdims=True))
        p = e / jnp.sum(e, axis=1, keepdims=True)
        outs.append(jnp.dot(p.astype(v.dtype), v, preferred_element_type=jnp.float32))
    o = jnp.concatenate(outs, axis=1).astype(wo_ref.dtype)
    x2 = x + jnp.dot(o, wo_ref[...], preferred_element_type=jnp.float32)
    x_out_ref[0] = x2
    xn_out_ref[0] = _rms(x2, g2_ref[...]).astype(xn_out_ref.dtype)


def mem_block(x, g1, g2, w_q, kv, w_o):
    b, t, dm = x.shape
    mt, kvw = kv.shape[1:]
    hd = w_q.shape[1]
    tm = 256 if t % 256 == 0 else t
    row_spec = pl.BlockSpec((1, tm, dm), lambda bi, i: (bi, i, 0))
    g_spec = pl.BlockSpec((1, dm), lambda bi, i: (0, 0))
    return pl.pallas_call(
        _mem_block_kernel,
        grid=(b, t // tm),
        in_specs=[row_spec, g_spec, g_spec,
                  pl.BlockSpec((dm, hd), lambda bi, i: (0, 0)),
                  pl.BlockSpec((1, mt, kvw), lambda bi, i: (bi, 0, 0)),
                  pl.BlockSpec((hd, dm), lambda bi, i: (0, 0))],
        out_specs=[row_spec, row_spec],
        out_shape=[jax.ShapeDtypeStruct((b, t, dm), jnp.float32), jax.ShapeDtypeStruct((b, t, dm), MXU_DTYPE)],
        compiler_params=pltpu.CompilerParams(
            dimension_semantics=("parallel", "parallel"), vmem_limit_bytes=VMEM_LIMIT),
        name="mem_block",
    )(x, g1.reshape(1, dm).astype(jnp.float32), g2.reshape(1, dm).astype(jnp.float32), w_q.astype(MXU_DTYPE),
      kv, w_o.astype(MXU_DTYPE))


def _ffn_up_kernel(x_ref, halo_ref, wg_ref, wu_ref, cwg_ref, cwu_ref, cbg_ref, cbu_ref, pg_ref, pu_ref,
                   h_ref, sg_ref, su_ref, *, nb, tm, t_real):
    i = pl.program_id(2)
    n_halo = halo_ref.shape[1]
    tn = wg_ref.shape[1]
    rows = nb * tm
    x = x_ref[...].reshape(rows, x_ref.shape[2])
    if nb == 1:
        x = jnp.concatenate([halo_ref[0], x], axis=0)
    row = _iota((nb, tm, 1), 1).reshape(rows, 1)
    last = (t_real - 1) // tm
    r_last = (t_real - 1) % tm
    spread = lambda p: jnp.broadcast_to(p, (nb, tm, tn)).reshape(rows, tn)

    def branch(w_ref, cw_ref, cb_ref, p_ref, s_ref):
        u = jnp.dot(x, w_ref[...], preferred_element_type=jnp.float32)
        p0, p1 = p_ref[:, 0:1, :], p_ref[:, 1:2, :]
        if nb == 1:
            uh, u = u[:n_halo], u[n_halo:]
            p0 = jnp.where(i == 0, p0, uh[n_halo - 2:n_halo - 1][None])
            p1 = jnp.where(i == 0, p1, uh[n_halo - 1:n_halo][None])
        p0, p1 = spread(p0), spread(p1)
        u1 = jnp.where(row == 0, p1, pltpu.roll(u, 1, 0))
        u2 = jnp.where(row == 0, p0, jnp.where(row == 1, p1, pltpu.roll(u, 2, 0)))

        @pl.when(i == last)
        def _():
            s_ref[...] = u.reshape(nb, tm, tn)[:, r_last - 1:r_last + 1, :]

        return cb_ref[...] + u2 * cw_ref[0:1, :] + u1 * cw_ref[1:2, :] + u * cw_ref[2:3, :]

    gate = branch(wg_ref, cwg_ref, cbg_ref, pg_ref, sg_ref)
    up = branch(wu_ref, cwu_ref, cbu_ref, pu_ref, su_ref)
    h_ref[...] = (jax.nn.silu(gate) * up).reshape(nb, tm, tn).astype(h_ref.dtype)


def ffn_up(xn, prev, w_up, conv_w, conv_b, t_real):
    b, t, dm = xn.shape
    f2 = w_up.shape[1]
    f = f2 // 2
    tm = FFN_ROWS if t % FFN_ROWS == 0 else t
    nb = 1 if t > tm else max(1, min(b, 512 // tm))
    while b % nb:
        nb -= 1
    tn = _tile(f, 512)
    nf = f // tn
    halo = min(BF16_ROWS, tm)
    hpt = tm // halo
    assert t_real >= 2 and (t_real - 1) % tm >= 1 and tm % SUBLANE == 0
    w_up = w_up.astype(MXU_DTYPE)
    conv_b = conv_b.reshape(1, f2)
    col = lambda off: (lambda j, bi, i: (0, j + off))
    st = lambda off: (lambda j, bi, i: (bi, 0, j + off))
    specs = [pl.BlockSpec((nb, tm, dm), lambda j, bi, i: (bi, i, 0)),
             pl.BlockSpec((1, halo, dm), lambda j, bi, i: (bi * nb, jnp.maximum(i * hpt - 1, 0), 0)),
             pl.BlockSpec((dm, tn), col(0)), pl.BlockSpec((dm, tn), col(nf)),
             pl.BlockSpec((CONV_WIDTH, tn), col(0)), pl.BlockSpec((CONV_WIDTH, tn), col(nf)),
             pl.BlockSpec((1, tn), col(0)), pl.BlockSpec((1, tn), col(nf)),
             pl.BlockSpec((nb, 2, tn), st(0)), pl.BlockSpec((nb, 2, tn), st(nf))]
    h, sg, su = pl.pallas_call(
        functools.partial(_ffn_up_kernel, nb=nb, tm=tm, t_real=t_real),
        grid=(nf, b // nb, t // tm),
        in_specs=specs,
        out_specs=[pl.BlockSpec((nb, tm, tn), lambda j, bi, i: (bi, i, j)),
                   pl.BlockSpec((nb, 2, tn), st(0)), pl.BlockSpec((nb, 2, tn), st(0))],
        out_shape=[jax.ShapeDtypeStruct((b, t, f), MXU_DTYPE), jax.ShapeDtypeStruct((b, 2, f), jnp.float32),
                   jax.ShapeDtypeStruct((b, 2, f), jnp.float32)],
        compiler_params=pltpu.CompilerParams(
            dimension_semantics=("parallel", "parallel", "arbitrary"), vmem_limit_bytes=VMEM_LIMIT),
        name="ffn_up",
    )(xn, xn, w_up, w_up, conv_w, conv_w, conv_b, conv_b, prev, prev)
    return h, jnp.concatenate([sg, su], axis=-1)


def _ffn_down_kernel(h_ref, w_ref, x_ref, g_ref, o_ref, acc_ref):
    @pl.when(pl.program_id(1) == 0)
    def _():
        acc_ref[...] = x_ref[...]

    acc_ref[...] += jnp.dot(h_ref[...], w_ref[...], preferred_element_type=jnp.float32)

    @pl.when(pl.program_id(1) == pl.num_programs(1) - 1)
    def _():
        o_ref[...] = _rms(acc_ref[...], g_ref[...])


def ffn_down_norm(h, w_down, x, g):
    m, f = h.shape
    dm = w_down.shape[1]
    tm = 512 if m % 512 == 0 else m
    tk = _tile(f, 2048)
    return pl.pallas_call(
        _ffn_down_kernel,
        grid=(m // tm, f // tk),
        in_specs=[pl.BlockSpec((tm, tk), lambda i, l: (i, l)),
                  pl.BlockSpec((tk, dm), lambda i, l: (l, 0)),
                  pl.BlockSpec((tm, dm), lambda i, l: (i, 0)),
                  pl.BlockSpec((1, dm), lambda i, l: (0, 0))],
        out_specs=pl.BlockSpec((tm, dm), lambda i, l: (i, 0)),
        out_shape=jax.ShapeDtypeStruct((m, dm), jnp.float32),
        scratch_shapes=[pltpu.VMEM((tm, dm), jnp.float32)],
        compiler_params=pltpu.CompilerParams(
            dimension_semantics=("parallel", "arbitrary"), vmem_limit_bytes=VMEM_LIMIT),
        name="ffn_down_norm",
    )(h, w_down.astype(MXU_DTYPE), x, g.reshape(1, dm).astype(jnp.float32))


def dense_tail(x, o_nsa, o_dsa, gm, kv_mem, prev_u, t_real, g_mem, g_ffn, g_final, w_oa, w_ob, w_o, w_mq, w_mo,
               w_up, conv_w, conv_b, w_down):
    b, t, dm = x.shape
    rows = lambda a: a.reshape(b * t, a.shape[-1])
    z = gated_merge(rows(o_nsa), rows(o_dsa), rows(gm), w_oa, w_ob)
    x1 = out_proj_residual(z, w_o, rows(x)).reshape(b, t, dm)
    x2, xn2 = mem_block(x1, g_mem, g_ffn, w_mq, kv_mem, w_mo)
    h, state = ffn_up(xn2, prev_u, w_up, conv_w, conv_b, t_real)
    y = ffn_down_norm(rows(h), w_down, rows(x2), g_final)
    return y.reshape(b, t, dm), state


def _dot_nt(a, b):
    return lax.dot_general(a, b, (((1,), (1,)), ((), ())), preferred_element_type=jnp.float32)


def _iota(shape, dim):
    return lax.broadcasted_iota(jnp.int32, shape, dim)


def _flash_init(rows, d):
    return (jnp.full((rows, 1), NEG_INF, jnp.float32), jnp.zeros((rows, 1), jnp.float32),
            jnp.zeros((rows, d), jnp.float32))


def _flash_step(carry, q, k, v, madd, nh, scale=None, kv_t=False):
    m, l, acc = carry
    s = jnp.dot(q, k, preferred_element_type=jnp.float32) if kv_t else _dot_nt(q, k)
    if scale is not None:
        s = s * scale
    r, kb = s.shape
    s = (s.reshape(nh, r // nh, kb) + madd[None]).reshape(r, kb)
    m_new = jnp.maximum(m, jnp.max(s, axis=1, keepdims=True))
    m_safe = jnp.where(m_new == NEG_INF, 0.0, m_new)
    p = jnp.exp(s - m_safe)
    alpha = jnp.exp(m - m_safe)
    l = alpha * l + jnp.sum(p, axis=1, keepdims=True)
    pv = _dot_nt(p.astype(v.dtype), v) if kv_t else jnp.dot(p.astype(v.dtype), v, preferred_element_type=jnp.float32)
    return m_new, l, alpha * acc + pv


def _flash_finish(carry):
    _, l, acc = carry
    return acc / jnp.maximum(l, 1e-30)


def _split_dot(x, m01):
    hi = x.astype(jnp.bfloat16)
    r1 = x - hi.astype(jnp.float32)
    mid = r1.astype(jnp.bfloat16)
    lo = (r1 - mid.astype(jnp.float32)).astype(jnp.bfloat16)
    dot = functools.partial(jnp.dot, preferred_element_type=jnp.float32)
    return dot(hi, m01) + dot(mid, m01) + dot(lo, m01)


def _nsa_group(g, i, qc_ref, qr_ref, gate_ref, kc_ref, vc_ref, ks_ref, vs_ref, kw_ref, vw_ref, mask_ref,
               *, tq, kb, qpos0, wbase, ns, n_sel, kv_t):
    nh, d = HPG_A, HEAD_DIM_A
    rows = nh * tq
    nc = kc_ref.shape[3]
    nsp = _round_up(ns, LANE)
    t0 = qpos0 + i * tq
    nj = (t0 + tq - 1) // kb + 1
    cols = slice(g * d, (g + 1) * d)

    def stack_heads(q_ref):
        return jnp.concatenate([q_ref[0, :, (g * nh + h) * d:(g * nh + h + 1) * d] for h in range(nh)], axis=0)

    qc = stack_heads(qc_ref)
    qr = stack_heads(qr_ref)
    tpos = t0 + _iota((tq, 1), 0)

    blk_last = _iota((1, nc), 1) * CMP_STRIDE + (CMP_BLOCK - 1)
    madd_c = jnp.where(blk_last <= tpos, 0.0, NEG_INF)
    s = _dot_nt(qc, kc_ref[0, 0, g]).reshape(nh, tq, nc) + madd_c[None]
    m = jnp.max(s, axis=2, keepdims=True)
    m = jnp.where(m == NEG_INF, 0.0, m)
    e = jnp.exp(s - m)
    p = e / jnp.maximum(jnp.sum(e, axis=2, keepdims=True), 1e-30)
    o_cmp = jnp.dot(p.reshape(rows, nc).astype(vc_ref.dtype), vc_ref[0, 0, g], preferred_element_type=jnp.float32)

    imp = jnp.sum(p, axis=0)
    c_id = _iota((nc, nsp), 0)
    m_id = _iota((nc, nsp), 1)
    overlap = (jnp.right_shift(c_id, CMP_PER_SLC_SHIFT) == m_id) | (c_id == m_id * CMP_PER_SLC - 1)
    score = _split_dot(imp, overlap.astype(jnp.bfloat16))
    blk = _iota((1, nsp), 1)
    cur = jnp.right_shift(tpos, SLC_SHIFT)
    forced = (blk == 0) | (blk == cur) | (blk == cur - 1)
    sc = jnp.where(forced, POS_INF, jnp.where(blk * SLC_BLOCK <= tpos, score, NEG_INF))
    if tq % LANE == 0 and nsp % LANE == 0:
        sc_t = sc.T
        blk_t = _iota((nsp, 1), 0)
        rank_t = jnp.zeros((nsp, tq), jnp.float32)
        for mp in range(ns):
            ref = sc_t[mp:mp + 1, :]
            beats = (ref > sc_t) | ((ref == sc_t) & (blk_t > mp))
            rank_t = rank_t + jnp.where(beats, 1.0, 0.0)
        sel = jnp.where(rank_t < n_sel, 1.0, 0.0).T.astype(jnp.bfloat16)
    else:
        rank = jnp.zeros((tq, nsp), jnp.float32)
        for mp in range(ns):
            col = sc[:, mp:mp + 1]
            beats = (col > sc) | ((col == sc) & (blk > mp))
            rank = rank + jnp.where(beats, 1.0, 0.0)
        sel = jnp.where(rank < n_sel, 1.0, 0.0).astype(jnp.bfloat16)

    def make_mask(j, _):
        kpos = j * kb + _iota((1, kb), 1)
        expand = (jnp.right_shift(j * kb + _iota((nsp, kb), 1), SLC_SHIFT) == _iota((nsp, kb), 0))
        hit = jnp.dot(sel, expand.astype(jnp.bfloat16), preferred_element_type=jnp.float32)
        mask_ref[j] = jnp.where((hit > 0.5) & (kpos <= tpos), 0.0, NEG_INF)
        return 0

    def slc_body(j, carry):
        off = pl.multiple_of(j * kb, kb)
        return _flash_step(carry, qr, ks_ref[0, pl.ds(off, kb), cols], vs_ref[0, pl.ds(off, kb), cols],
                           mask_ref[j], nh)

    if kv_t:
        sel_f = sel.astype(jnp.float32)
        low_half = _iota((tq, LANE), 1) < SLC_BLOCK
        pieces = []
        for v in range(kb // LANE):
            hit = jnp.where(low_half, sel_f[:, 2 * v:2 * v + 1], sel_f[:, 2 * v + 1:2 * v + 2])
            kpos = v * LANE + _iota((1, LANE), 1)
            pieces.append(jnp.where((hit > 0.5) & (kpos <= tpos), 0.0, NEG_INF))
        wide = lambda ref: jnp.concatenate([ref[0, c, cols, :] for c in range(ref.shape[1])], axis=1)
        o_slc = _flash_finish(_flash_step(_flash_init(rows, d), qr, wide(ks_ref), wide(vs_ref),
                                          jnp.concatenate(pieces, axis=1), nh, kv_t=True))
    else:
        lax.fori_loop(0, nj, make_mask, 0)
        o_slc = _flash_finish(lax.fori_loop(0, nj, slc_body, _flash_init(rows, d)))

    wk = min(_round_up(WINDOW + tq, WIN_CHUNK), kw_ref.shape[1])
    first = jnp.maximum(t0 - (WINDOW - 1) - wbase, 0) // WIN_CHUNK * WIN_CHUNK
    off = pl.multiple_of(jnp.minimum(first, kw_ref.shape[1] - wk), WIN_CHUNK)
    dist = tpos - (wbase + off + _iota((1, wk), 1))
    madd_w = jnp.where((dist >= 0) & (dist < WINDOW), 0.0, NEG_INF)
    o_win = _flash_finish(_flash_step(_flash_init(rows, d), qr, kw_ref[0, pl.ds(off, wk), cols],
                                      vw_ref[0, pl.ds(off, wk), cols], madd_w, nh))

    gates = gate_ref[0]
    outs = []
    for h in range(nh):
        rs = slice(h * tq, (h + 1) * tq)
        c = g * nh + h
        outs.append(gates[:, c:c + 1] * o_cmp[rs] + gates[:, N_HEADS_A + c:N_HEADS_A + c + 1] * o_slc[rs]
                    + gates[:, 2 * N_HEADS_A + c:2 * N_HEADS_A + c + 1] * o_win[rs])
    return outs


def _nsa_kernel(qc_ref, qr_ref, gate_ref, kc_ref, vc_ref, ks_ref, vs_ref, kw_ref, vw_ref, o_ref, mask_ref, **kw):
    i = pl.program_id(1)
    outs = []
    for g in range(KV_GROUPS_A):
        outs += _nsa_group(g, i, qc_ref, qr_ref, gate_ref, kc_ref, vc_ref, ks_ref, vs_ref, kw_ref, vw_ref,
                           mask_ref, **kw)
    o_ref[0] = jnp.concatenate(outs, axis=1).astype(o_ref.dtype)


def nsa_attention(qc, qr, gates, kvc, ks, vs, kw, vw, *, tq, qpos0, wbase, n_keys, kv_t=False):
    b, t, hd = qc.shape
    l = ks.shape[1] * ks.shape[3] if kv_t else ks.shape[1]
    lw = kw.shape[1]
    nc, d = kvc.shape[3:]
    kb = min(KEY_CHUNK, l) if tq >= TQ else l
    assert l % kb == 0 and t % tq == 0 and lw % WIN_CHUNK == 0 and (kb == l or not kv_t)
    assert (qpos0 + t - 1) // kb + 1 <= l // kb and (qpos0 + t - 1 - wbase) // WIN_CHUNK + 1 <= lw // WIN_CHUNK
    assert (qpos0 - wbase) % WIN_CHUNK == 0 and WIN_CHUNK % tq == 0 and tq > 1
    assert 2 * SLC_BLOCK == LANE or not kv_t
    ns = l // SLC_BLOCK
    n_sel = min(N_SELECT, -(-n_keys // SLC_BLOCK))
    q_spec = pl.BlockSpec((1, tq, hd), lambda bi, i: (bi, i, 0))
    kc_spec = pl.BlockSpec((1, 1, KV_GROUPS_A, nc, d), lambda bi, i: (0, bi, 0, 0, 0))
    vc_spec = pl.BlockSpec((1, 1, KV_GROUPS_A, nc, d), lambda bi, i: (1, bi, 0, 0, 0))
    k_spec = (pl.BlockSpec((1,) + ks.shape[1:], lambda bi, i: (bi, 0, 0, 0)) if kv_t
              else pl.BlockSpec((1, l, KV_A), lambda bi, i: (bi, 0, 0)))
    w_spec = pl.BlockSpec((1, lw, KV_A), lambda bi, i: (bi, 0, 0))
    return pl.pallas_call(
        functools.partial(_nsa_kernel, tq=tq, kb=kb, qpos0=qpos0, wbase=wbase, ns=ns, n_sel=n_sel, kv_t=kv_t),
        grid=(b, t // tq),
        in_specs=[q_spec, q_spec, pl.BlockSpec((1, tq, 3 * N_HEADS_A), lambda bi, i: (bi, i, 0)),
                  kc_spec, vc_spec, k_spec, k_spec, w_spec, w_spec],
        out_specs=q_spec,
        out_shape=jax.ShapeDtypeStruct((b, t, hd), MXU_DTYPE),
        scratch_shapes=[pltpu.VMEM((l // kb, tq, kb), jnp.float32)],
        compiler_params=pltpu.CompilerParams(
            dimension_semantics=("parallel", "arbitrary"), vmem_limit_bytes=VMEM_LIMIT),
        name="nsa_attention",
    )(qc, qr, gates, kvc, kvc, ks, vs, kw, vw)


def _dsa_kernel(qb_ref, qi_ref, wi_ref, kb_ref, vb_ref, ki_ref, o_ref, score_ref,
                *, tq, tq_real, kb, qpos0, n_keep, kv_t):
    i = pl.program_id(1)
    nh, d = N_HEADS_B, HEAD_DIM_B
    t0 = qpos0 + i * tq
    nj = (t0 + tq - 1) // kb + 1
    tpos = t0 + _iota((tq, 1), 0)
    w = wi_ref[0]
    wide = lambda ref: jnp.concatenate([ref[0, c] for c in range(ref.shape[1])], axis=1)

    def idx_body(j, carry):
        lo, hi = carry
        off = pl.multiple_of(j * kb, kb)
        kidx = wide(ki_ref) if kv_t else ki_ref[0, pl.ds(off, kb), :]
        acc = jnp.zeros((tq, kb), jnp.float32)
        heads = [qi_ref[0, :, h * IDX_DIM:(h + 1) * IDX_DIM] for h in range(IDX_HEADS)]
        if kv_t:
            dots_all = jnp.dot(jnp.concatenate(heads, axis=0), kidx, preferred_element_type=jnp.float32)
        for h in range(IDX_HEADS):
            dots = dots_all[h * tq:(h + 1) * tq] if kv_t else _dot_nt(heads[h], kidx)
            acc = acc + w[:, h:h + 1] * jnp.maximum(dots, 0.0)
        vis = (off + _iota((1, kb), 1)) <= tpos
        score_ref[j] = jnp.where(vis, acc, NEG_INF)
        lo = jnp.minimum(lo, jnp.min(jnp.where(vis, acc, POS_INF), axis=1, keepdims=True))
        hi = jnp.maximum(hi, jnp.max(jnp.where(vis, acc, NEG_INF), axis=1, keepdims=True))
        return lo, hi

    lo, hi = lax.fori_loop(0, nj, idx_body, (jnp.full((tq, 1), POS_INF, jnp.float32),
                                             jnp.full((tq, 1), NEG_INF, jnp.float32)))

    k = float(n_keep)
    n_vis = (tpos + 1).astype(jnp.float32)
    n_keys = score_ref.shape[0] * kb

    def reduce_scores(pred, pick, init, combine, lane_reduce):
        def body(j, acc):
            sc = score_ref[j]
            for c in range(kb // LANE):
                kpos = (j * kb + c * LANE + _iota((1, LANE), 1)).astype(jnp.float32)
                piece = sc[:, c * LANE:(c + 1) * LANE]
                acc = combine(acc, pick(pred(piece, kpos), piece))
            return acc

        return lane_reduce(lax.fori_loop(0, nj, body, jnp.full((tq, LANE), init, jnp.float32)),
                           axis=1, keepdims=True)

    def count_where(pred):
        return reduce_scores(pred, lambda m, _: jnp.where(m, 1.0, 0.0), 0.0, jnp.add, jnp.sum)

    def min_where(pred):
        return reduce_scores(pred, lambda m, x: jnp.where(m, x, POS_INF), POS_INF, jnp.minimum, jnp.min)

    real = _iota((tq, 1), 0) < tq_real

    def any_row(flag):
        return jnp.max(jnp.where(real & flag, 1.0, 0.0)) > 0.0

    def bisect(state):
        it, lo, hi, cnt_lo = state
        mid = 0.5 * (lo + hi)
        mid_b = jnp.broadcast_to(mid, (tq, LANE))
        cnt = count_where(lambda x, _: x >= mid_b)
        ge = cnt >= k
        return it + 1, jnp.where(ge, mid, lo), jnp.where(ge, hi, mid), jnp.where(ge, cnt, cnt_lo)

    _, thr, _, cnt_lo = lax.while_loop(
        lambda st: (st[0] < BISECT_ITERS) & any_row((st[3] != k) & (n_vis > k)), lambda st: bisect(bisect(st)),
        (jnp.int32(0), lo, hi, n_vis))

    def break_ties():
        def above(v):
            v_b = jnp.broadcast_to(v, (tq, LANE))
            return count_where(lambda x, _: x > v_b)

        def strip(state):
            it, v, c_gt = state
            v_b = jnp.broadcast_to(v, (tq, LANE))
            v_next = jnp.where(c_gt >= k, min_where(lambda x, _: x > v_b), v)
            return it + 1, v_next, above(v_next)

        thr_b = jnp.broadcast_to(thr, (tq, LANE))
        v0 = min_where(lambda x, _: x >= thr_b)
        _, v, c_gt = lax.while_loop(lambda st: (st[0] < TIE_STRIP_ITERS) & any_row(st[2] >= k), strip,
                                    (jnp.int32(0), v0, above(v0)))
        need = k - c_gt
        v_b = jnp.broadcast_to(v, (tq, LANE))

        def narrow(_, bounds):
            j_lo, j_hi = bounds
            mid = jnp.floor(0.5 * (j_lo + j_hi))
            ge = count_where(lambda x, kpos: (x == v_b) & (kpos <= mid)) >= need
            return jnp.where(ge, j_lo, mid), jnp.where(ge, mid, j_hi)

        _, j_max = lax.fori_loop(0, n_keys.bit_length(), narrow,
                                 (jnp.full((tq, 1), -1.0, jnp.float32), jnp.full((tq, 1), n_keys - 1.0, jnp.float32)))
        return v, j_max

    thr, j_max = lax.cond(any_row((cnt_lo > k) & (n_vis > k)), break_ties,
                          lambda: (thr, jnp.full((tq, 1), float(n_keys), jnp.float32)))

    q = jnp.concatenate([qb_ref[0, :, h * d:(h + 1) * d] for h in range(nh)], axis=0)

    def att_body(j, carry):
        off = pl.multiple_of(j * kb, kb)
        sc = score_ref[j]
        kpos = (off + _iota((1, kb), 1)).astype(jnp.float32)
        madd = jnp.where((sc > thr) | ((sc == thr) & (kpos <= j_max)), 0.0, NEG_INF)
        if kv_t:
            return _flash_step(carry, q, wide(kb_ref), wide(vb_ref), madd, nh, scale=d ** -0.5, kv_t=True)
        return _flash_step(carry, q, kb_ref[0, pl.ds(off, kb), :], vb_ref[0, pl.ds(off, kb), :], madd, nh,
                           scale=d ** -0.5)

    o = _flash_finish(lax.fori_loop(0, nj, att_body, _flash_init(nh * tq, d)))
    o_ref[0] = jnp.concatenate([o[h * tq:(h + 1) * tq] for h in range(nh)], axis=1).astype(o_ref.dtype)


def dsa_attention(qb, qi, wi, kb_, vb, ki, *, tq, qpos0, n_keep, kv_t=False, tq_real=None):
    b, t, hd = qb.shape
    l = kb_.shape[1] * kb_.shape[3] if kv_t else kb_.shape[1]
    kb = min(KEY_CHUNK, l) if tq >= TQ else l
    assert l % kb == 0 and t % tq == 0 and (qpos0 + t - 1) // kb + 1 <= l // kb and (kb == l or not kv_t)
    steps = kb_.shape[1]
    kv_spec = lambda width: (pl.BlockSpec((1, steps, width, l // steps), lambda bi, i: (bi, 0, 0, 0)) if kv_t
                             else pl.BlockSpec((1, l, width), lambda bi, i: (bi, 0, 0)))
    return pl.pallas_call(
        functools.partial(_dsa_kernel, tq=tq, tq_real=tq if tq_real is None else tq_real, kb=kb, qpos0=qpos0,
                          n_keep=n_keep, kv_t=kv_t),
        grid=(b, t // tq),
        in_specs=[pl.BlockSpec((1, tq, hd), lambda bi, i: (bi, i, 0)),
                  pl.BlockSpec((1, tq, IDX_HEADS * IDX_DIM), lambda bi, i: (bi, i, 0)),
                  pl.BlockSpec((1, tq, IDX_HEADS), lambda bi, i: (bi, i, 0)),
                  kv_spec(HEAD_DIM_B), kv_spec(HEAD_DIM_B), kv_spec(IDX_DIM)],
        out_specs=pl.BlockSpec((1, tq, hd), lambda bi, i: (bi, i, 0)),
        out_shape=jax.ShapeDtypeStruct((b, t, hd), MXU_DTYPE),
        scratch_shapes=[pltpu.VMEM((l // kb, tq, kb), jnp.float32)],
        compiler_params=pltpu.CompilerParams(
            dimension_semantics=("parallel", "arbitrary"), vmem_limit_bytes=VMEM_LIMIT),
        name="dsa_attention",
    )(qb, qi, wi, kb_, vb, ki)


def _page_maps(n_pages, pp):
    n_steps = n_pages // pp

    def page_map(r):
        return lambda b, s, pt: (pt[b * n_pages + jnp.minimum(s, n_steps - 1) * pp + r], 0, 0)

    return n_steps, page_map


def _chunk_rows(tok_ref, sec, row0, n):
    d = HEAD_DIM_A
    first_half = _iota((n, KV_A), 1) < d
    pieces = [[] for _ in range(KV_GROUPS_A)]
    for j in range(0, CMP_STRIDE, 2):
        a, b = [tok_ref[sec, pl.ds(row0 + jj, n, stride=CMP_STRIDE), :] for jj in (j, j + 1)]
        pieces[0].append(jnp.where(first_half, a, pltpu.roll(b, d, 1)))
        pieces[1].append(jnp.where(first_half, pltpu.roll(a, d, 1), b))
    return [jnp.concatenate(p, axis=1) for p in pieces]


def _nsa_gather_kernel(pt_ref, *refs, pp, n_steps, rows):
    del pt_ref
    pages, (tail_tok_ref, tail_t_ref, pe_ref) = refs[:pp], refs[pp:pp + 3]
    zt_ref, zb_ref, ks_ref, vs_ref, tok_ref = refs[pp + 3:]
    is_tail = pl.program_id(1) == n_steps
    z = [[[] for _ in range(KV_GROUPS_A)] for _ in range(2)]
    for r in range(pp):
        cs = slice(r * rows, (r + 1) * rows)
        ks_ref[0, 0, :, cs] = jnp.where(is_tail, tail_t_ref[0, :KV_A, cs],
                                     pages[r][0, 2 * KV_A:3 * KV_A, :]).astype(ks_ref.dtype)
        vs_ref[0, 0, :, cs] = jnp.where(is_tail, tail_t_ref[0, KV_A:, cs],
                                     pages[r][0, 3 * KV_A:, :]).astype(vs_ref.dtype)
        for sec in range(2):
            tok_ref[sec, cs, :] = jnp.where(is_tail, tail_tok_ref[0, sec, cs, :],
                                            pages[r][0, sec * KV_A:(sec + 1) * KV_A, :].T)
            for g, zg in enumerate(_chunk_rows(tok_ref, sec, r * rows, rows // CMP_STRIDE)):
                z[sec][g].append(zg)
    for sec in range(2):
        for g in range(KV_GROUPS_A):
            zf = jnp.concatenate(z[sec][g], axis=0)
            zt_ref[sec, 0, g] = (zf + pe_ref[sec, 0]).astype(zt_ref.dtype)
            zb_ref[sec, 0, g] = (zf + pe_ref[sec, 1]).astype(zb_ref.dtype)


def nsa_gather(cache_t, page_table, new_rows, cmp_pe):
    db, n_pages = page_table.shape
    width, rows = cache_t.shape[1:]
    pp = PAGES_PER_STEP
    n_steps, page_map = _page_maps(n_pages, pp)
    l = (n_steps + 1) * pp * rows
    cps = pp * rows // CMP_STRIDE
    flat = CMP_STRIDE * HEAD_DIM_A
    pe = cmp_pe.reshape(2, 2, 1, flat).astype(jnp.float32)
    tail = pad_rows(new_rows, pp * rows)
    tail_tok = tail[:, :, :2 * KV_A].reshape(db, pp * rows, 2, KV_A).swapaxes(1, 2)
    tail_t = tail[:, :, 2 * KV_A:].swapaxes(1, 2)
    z_spec = pl.BlockSpec((2, 1, KV_GROUPS_A, cps, flat), lambda b, s, pt: (0, b, 0, s, 0))
    r_spec = pl.BlockSpec((1, 1, KV_A, pp * rows), lambda b, s, pt: (b, s, 0, 0))
    z_shape = jax.ShapeDtypeStruct((2, db, KV_GROUPS_A, l // CMP_STRIDE, flat), MXU_DTYPE)
    r_shape = jax.ShapeDtypeStruct((db, n_steps + 1, KV_A, pp * rows), MXU_DTYPE)
    return pl.pallas_call(
        functools.partial(_nsa_gather_kernel, pp=pp, n_steps=n_steps, rows=rows),
        grid_spec=pltpu.PrefetchScalarGridSpec(
            num_scalar_prefetch=1,
            grid=(db, n_steps + 1),
            in_specs=[pl.BlockSpec((1, width, rows), page_map(r)) for r in range(pp)]
            + [pl.BlockSpec((1, 2, pp * rows, KV_A), lambda b, s, pt: (b, 0, 0, 0)),
               pl.BlockSpec((1, 2 * KV_A, pp * rows), lambda b, s, pt: (b, 0, 0)),
               pl.BlockSpec((2, 2, 1, flat), lambda b, s, pt: (0, 0, 0, 0))],
            out_specs=[z_spec, z_spec, r_spec, r_spec],
            scratch_shapes=[pltpu.VMEM((2, pp * rows, KV_A), jnp.float32)]),
        out_shape=[z_shape, z_shape, r_shape, r_shape],
        compiler_params=pltpu.CompilerParams(
            dimension_semantics=("parallel", "arbitrary"), vmem_limit_bytes=VMEM_LIMIT),
        name="nsa_gather",
    )(page_table.reshape(-1), *([cache_t] * pp), tail_tok, tail_t, pe)


def _dsa_gather_kernel(pt_ref, *refs, pp, n_steps, rows):
    del pt_ref
    pages, tail_ref, (k_ref, v_ref, i_ref) = refs[:pp], refs[pp], refs[pp + 1:]
    is_tail = pl.program_id(1) == n_steps
    d = HEAD_DIM_B
    for r in range(pp):
        cs = slice(r * rows, (r + 1) * rows)
        x = jnp.where(is_tail, tail_ref[0, :, cs], pages[r][0])
        k_ref[0, 0, :, cs] = x[:d].astype(k_ref.dtype)
        v_ref[0, 0, :, cs] = x[d:2 * d].astype(v_ref.dtype)
        i_ref[0, 0, :, cs] = x[2 * d:].astype(i_ref.dtype)


def dsa_gather(cache_t, page_table, new_rows):
    db, n_pages = page_table.shape
    width, rows = cache_t.shape[1:]
    pp = PAGES_PER_STEP
    n_steps, page_map = _page_maps(n_pages, pp)
    tail_t = pad_rows(new_rows, pp * rows).swapaxes(1, 2)
    out_spec = lambda w: pl.BlockSpec((1, 1, w, pp * rows), lambda b, s, pt: (b, s, 0, 0))
    widths = (HEAD_DIM_B, HEAD_DIM_B, IDX_DIM)
    return pl.pallas_call(
        functools.partial(_dsa_gather_kernel, pp=pp, n_steps=n_steps, rows=rows),
        grid_spec=pltpu.PrefetchScalarGridSpec(
            num_scalar_prefetch=1,
            grid=(db, n_steps + 1),
            in_specs=[pl.BlockSpec((1, width, rows), page_map(r)) for r in range(pp)]
            + [pl.BlockSpec((1, width, pp * rows), lambda b, s, pt: (b, 0, 0))],
            out_specs=[out_spec(w) for w in widths]),
        out_shape=[jax.ShapeDtypeStruct((db, n_steps + 1, w, pp * rows), MXU_DTYPE) for w in widths],
        compiler_params=pltpu.CompilerParams(
            dimension_semantics=("parallel", "arbitrary"), vmem_limit_bytes=VMEM_LIMIT),
        name="dsa_gather",
    )(page_table.reshape(-1), *([cache_t] * pp), tail_t)


def _compress_kernel(zt_ref, zb_ref, pe_ref, w1t_ref, w1b_ref, b1_ref, w2_ref, o_ref, ab_ref, *, ncp):
    dot = functools.partial(jnp.dot, preferred_element_type=jnp.float32)
    ch = zb_ref.shape[2]
    at = dot(zt_ref[0, 0], w1t_ref[0])
    ab_ref[:ch] = dot(zb_ref[0, 0], w1b_ref[0])
    pe_rows = jnp.broadcast_to(pe_ref[0, 1], (SUBLANE, pe_ref.shape[3])).astype(zb_ref.dtype)
    ab_ref[ch:] = dot(pe_rows, w1b_ref[0])
    h = jax.nn.gelu(at[:ncp] + ab_ref[pl.ds(1, ncp), :] + b1_ref[0])
    o_ref[0, 0, :ncp] = dot(h.astype(w2_ref.dtype), w2_ref[0]).astype(o_ref.dtype)
    if o_ref.shape[2] > ncp:
        o_ref[0, 0, ncp:] = jnp.zeros((o_ref.shape[2] - ncp, o_ref.shape[3]), o_ref.dtype)


def compress(zt, zb, cmp_pe, w1, b1, w2, n_keys):
    ncp = _round_up(-(-n_keys // CMP_STRIDE), BF16_ROWS)
    ncl = ncp if ncp <= LANE else _round_up(ncp, LANE)
    _, nb, ch, kdim = zt.shape
    hid = w1.shape[-1]
    d = w2.shape[-1]
    assert ch + SUBLANE >= ncp + 1 and ch % SUBLANE == 0
    w1 = w1.astype(MXU_DTYPE)
    pe = cmp_pe.reshape(2, 2, 1, kdim).astype(jnp.float32)
    z_spec = pl.BlockSpec((1, 1, ch, kdim), lambda s, n: (s, n, 0, 0))
    return pl.pallas_call(
        functools.partial(_compress_kernel, ncp=ncp),
        grid=(2, nb),
        in_specs=[z_spec, z_spec,
                  pl.BlockSpec((1, 2, 1, kdim), lambda s, n: (s, 0, 0, 0)),
                  pl.BlockSpec((1, kdim, hid), lambda s, n: (s, 0, 0)),
                  pl.BlockSpec((1, kdim, hid), lambda s, n: (s, 1, 0)),
                  pl.BlockSpec((1, 1, hid), lambda s, n: (s, 0, 0)),
                  pl.BlockSpec((1, hid, d), lambda s, n: (s, 0, 0))],
        out_specs=pl.BlockSpec((1, 1, ncl, d), lambda s, n: (s, n, 0, 0)),
        out_shape=jax.ShapeDtypeStruct((2, nb, ncl, d), MXU_DTYPE),
        scratch_shapes=[pltpu.VMEM((ch + SUBLANE, hid), jnp.float32)],
        compiler_params=pltpu.CompilerParams(
            dimension_semantics=("parallel", "parallel"), vmem_limit_bytes=VMEM_LIMIT),
        name="compress",
    )(zt, zb, pe, w1, w1, b1.reshape(2, 1, hid).astype(jnp.float32), w2.astype(MXU_DTYPE))


def pad_rows(a, n):
    return jnp.pad(a, [(0, 0), (0, n - a.shape[1])] + [(0, 0)] * (a.ndim - 2))


def _rope_tables(pos, d):
    half = d // ROT_FRACTION // 2
    inv = ROPE_THETA ** (-jnp.arange(half, dtype=jnp.float32) / half)
    ang = pos.astype(jnp.float32)[:, None] * inv[None, :]
    lane = jnp.arange(LANE) % d
    cos = jnp.cos(ang)[:, lane % half]
    sin = jnp.sin(ang)[:, lane % half]
    one, zero = jnp.ones_like(cos), jnp.zeros_like(cos)
    c = jnp.where(lane < 2 * half, cos, one)
    sa = jnp.where((lane >= half) & (lane < 2 * half), sin, zero)
    sb = jnp.where(lane < half, -sin, zero)
    return jnp.stack([c, sa, sb])


def _rope(x, t_ref, half):
    c, sa, sb = t_ref[0], t_ref[1], t_ref[2]
    outs = []
    for j in range(x.shape[1] // LANE):
        xs = x[:, j * LANE:(j + 1) * LANE]
        outs.append(xs * c + pltpu.roll(xs, half, 1) * sa + pltpu.roll(xs, LANE - half, 1) * sb)
    return outs[0] if len(outs) == 1 else jnp.concatenate(outs, axis=1)


_QA = N_HEADS_A * HEAD_DIM_A
_KVA = 6 * KV_A
_QB = N_HEADS_B * HEAD_DIM_B
_KVB = 2 * HEAD_DIM_B
_QI = IDX_HEADS * IDX_DIM
_MISC = IDX_DIM + IDX_HEADS + 3 * N_HEADS_A
PREP_WIDTH = _QA + _KVA + _QB + _KVB + _QI + _MISC
assert _MISC == LANE


def _prep_kernel(x_ref, t64_ref, t128_ref, pe_ref, qc_ref, qr_ref, qb_ref, qi_ref, nsa_ref, win_ref, dsa_ref,
                 ks_ref, vs_ref, kw_ref, vw_ref, kb_ref, vb_ref, ki_ref, wi_ref, gate_ref, *z_refs, tm):
    x = x_ref[...]
    o = 0
    qa = x[:, o:o + _QA]; o += _QA
    kva = x[:, o:o + _KVA]; o += _KVA
    qb = x[:, o:o + _QB]; o += _QB
    kvb = x[:, o:o + _KVB]; o += _KVB
    qi = x[:, o:o + _QI]; o += _QI
    misc = x[:, o:o + _MISC]
    h64, h128 = HEAD_DIM_A // ROT_FRACTION // 2, HEAD_DIM_B // ROT_FRACTION // 2
    sec = lambda i: kva[:, i * KV_A:(i + 1) * KV_A]
    mx = lambda a: a.astype(qc_ref.dtype)
    qc_ref[...] = mx(qa * HEAD_DIM_A ** -0.5)
    qr_ref[...] = mx(_rope(qa, t64_ref, h64) * HEAD_DIM_A ** -0.5)
    k_slc, k_win = _rope(sec(2), t64_ref, h64), _rope(sec(4), t64_ref, h64)
    nsa_ref[:, :2 * KV_A] = kva[:, :2 * KV_A]
    nsa_ref[:, 2 * KV_A:3 * KV_A] = k_slc
    nsa_ref[:, 3 * KV_A:] = sec(3)
    win_ref[:, :KV_A] = k_win
    win_ref[:, KV_A:] = sec(5)
    ks_ref[...], vs_ref[...], kw_ref[...], vw_ref[...] = mx(k_slc), mx(sec(3)), mx(k_win), mx(sec(5))
    qb_ref[...] = mx(_rope(qb, t128_ref, h128))
    k_b, v_b = _rope(kvb[:, :HEAD_DIM_B], t128_ref, h128), kvb[:, HEAD_DIM_B:]
    k_idx = _rope(misc, t64_ref, h64)[:, :IDX_DIM]
    dsa_ref[:, :HEAD_DIM_B] = k_b
    dsa_ref[:, HEAD_DIM_B:2 * HEAD_DIM_B] = v_b
    dsa_ref[:, 2 * HEAD_DIM_B:] = k_idx
    kb_ref[...], vb_ref[...], ki_ref[...] = mx(k_b), mx(v_b), mx(k_idx)
    qi_ref[...] = mx(_rope(qi, t64_ref, h64) * IDX_DIM ** -0.5)
    wi_ref[...] = misc[:, IDX_DIM:IDX_DIM + IDX_HEADS] * IDX_HEADS ** -0.5
    gate_ref[...] = jax.nn.sigmoid(misc[:, IDX_DIM + IDX_HEADS:])
    if z_refs:
        zt_ref, zb_ref, tok_ref = z_refs
        for s_ in range(2):
            tok_ref[s_] = sec(s_)
            for g, zg in enumerate(_chunk_rows(tok_ref, s_, 0, tm // CMP_STRIDE)):
                zt_ref[s_, 0, g] = (zg + pe_ref[s_, 0]).astype(zt_ref.dtype)
                zb_ref[s_, 0, g] = (zg + pe_ref[s_, 1]).astype(zb_ref.dtype)


def prep(proj, pos, cmp_pe, with_chunks):
    b, t, width = proj.shape
    assert width == PREP_WIDTH
    m = b * t
    tm = 256 if t % 256 == 0 else m
    assert m % tm == 0 and t % tm in (0, t)
    nt = max(t // tm, 1)
    flat = CMP_STRIDE * HEAD_DIM_A
    names = ['qc', 'qr', 'qb', 'qi', 'nsa_rows', 'win_rows', 'dsa_rows', 'ks', 'vs', 'kw', 'vw', 'kb', 'vb', 'ki',
             'wi', 'gates']
    widths = [_QA, _QA, _QB, _QI, NSA_SECTIONS * KV_A, 2 * KV_A, DSA_ROW, KV_A, KV_A, KV_A, KV_A, HEAD_DIM_B,
              HEAD_DIM_B, IDX_DIM, IDX_HEADS, 3 * N_HEADS_A]
    dtypes = [MXU_DTYPE] * 4 + [jnp.float32] * 3 + [MXU_DTYPE] * 7 + [jnp.float32] * 2
    row = lambda w: pl.BlockSpec((tm, w), lambda i: (i, 0))
    out_specs = [row(w) for w in widths]
    out_shape = [jax.ShapeDtypeStruct((m, w), dt) for w, dt in zip(widths, dtypes)]
    scratch = []
    if with_chunks:
        assert tm % CMP_STRIDE == 0 and t % tm == 0
        z_spec = pl.BlockSpec((2, 1, KV_GROUPS_A, tm // CMP_STRIDE, flat), lambda i: (0, i // nt, 0, i % nt, 0))
        z_shape = jax.ShapeDtypeStruct((2, b, KV_GROUPS_A, t // CMP_STRIDE, flat), MXU_DTYPE)
        out_specs += [z_spec, z_spec]
        out_shape += [z_shape, z_shape]
        names += ['zt', 'zb']
        scratch = [pltpu.VMEM((2, tm, KV_A), jnp.float32)]
    t_spec = pl.BlockSpec((3, tm, LANE), lambda i: (0, i, 0))
    outs = pl.pallas_call(
        functools.partial(_prep_kernel, tm=tm),
        grid=(m // tm,),
        in_specs=[row(width), t_spec, t_spec, pl.BlockSpec((2, 2, 1, flat), lambda i: (0, 0, 0, 0))],
        out_specs=out_specs,
        out_shape=out_shape,
        scratch_shapes=scratch,
        compiler_params=pltpu.CompilerParams(dimension_semantics=("parallel",), vmem_limit_bytes=VMEM_LIMIT),
        name="prep",
    )(proj.reshape(m, width), _rope_tables(pos, HEAD_DIM_A), _rope_tables(pos, HEAD_DIM_B),
      cmp_pe.reshape(2, 2, 1, flat).astype(jnp.float32))
    return {n: (o if o.ndim > 2 else o.reshape(b, t, o.shape[-1])) for n, o in zip(names, outs)}


def project(x, g, w_in, sizes):
    starts = [sum(sizes[:i]) for i in range(len(sizes))]
    qa, kva, ga, qb, kvb, qi, ki, wi, gm = [slice(o, o + n) for o, n in zip(starts, sizes)]
    w_bf = w_in.astype(MXU_DTYPE)
    w_prep = jnp.concatenate([w_bf[:, c] for c in (qa, kva, qb, kvb, qi, ki, wi, ga)], axis=1)
    return norm_matmul(x, g, w_prep), norm_matmul(x, g, w_bf[:, gm])


def mixer_prompt(x, g, w_in, cmp_pe, cmp_w1, cmp_b1, cmp_w2, sizes):
    b, s, _ = x.shape
    proj, gm = project(x, g, w_in, sizes)
    p = prep(proj, jnp.tile(jnp.arange(s, dtype=jnp.int32), b), cmp_pe, True)
    merge_bg = lambda a: a.reshape((2, b * KV_GROUPS_A) + a.shape[3:])
    kvc = compress(merge_bg(p['zt']), merge_bg(p['zb']), cmp_pe, cmp_w1, cmp_b1, cmp_w2, s)
    kvc = kvc.reshape(2, b, KV_GROUPS_A, kvc.shape[2], HEAD_DIM_A)
    o_nsa = nsa_attention(p['qc'], p['qr'], p['gates'], kvc, p['ks'], p['vs'], p['kw'], p['vw'],
                          tq=min(TQ, s), qpos0=0, wbase=0, n_keys=s)
    o_dsa = dsa_attention(p['qb'], p['qi'], p['wi'], p['kb'], p['vb'], p['ki'],
                          tq=min(TQ, s), qpos0=0, n_keep=min(DSA_TOPK, s // 4))
    nsa_rows = p['nsa_rows'].reshape(b, s, NSA_SECTIONS, KV_GROUPS_A, HEAD_DIM_A)
    win_state = p['win_rows'].reshape(b, s, 2, KV_GROUPS_A, HEAD_DIM_A)[:, -min(WINDOW, s):]
    return (o_nsa, o_dsa, gm), nsa_rows, win_state, p['dsa_rows']


def mixer_sample(x, g, cache_nsa, win_buf, cache_dsa, page_table, w_in, cmp_pe, cmp_w1, cmp_b1, cmp_w2, sizes):
    b, t, _ = x.shape
    page = cache_nsa.shape[1]
    past_len = page_table.shape[1] * page
    n_keys = past_len + t
    proj, gm = project(x, g, w_in, sizes)
    p = prep(proj, jnp.tile(past_len + jnp.arange(t, dtype=jnp.int32), b), cmp_pe, False)
    pad = lambda a: pad_rows(a, TQ_STEP)

    cache_nsa_t = jnp.transpose(cache_nsa, (0, 2, 3, 4, 1)).reshape(cache_nsa.shape[0], NSA_SECTIONS * KV_A, page)
    zt, zb, ks, vs = nsa_gather(cache_nsa_t, page_table, p['nsa_rows'], cmp_pe)
    merge_bg = lambda a: a.reshape((2, b * KV_GROUPS_A) + a.shape[3:])
    kvc = compress(merge_bg(zt), merge_bg(zb), cmp_pe, cmp_w1, cmp_b1, cmp_w2, n_keys)
    kvc = kvc.reshape(2, b, KV_GROUPS_A, kvc.shape[2], HEAD_DIM_A)
    w_len = win_buf.shape[1]
    win_new = p['win_rows'].reshape(b, t, 2, KV_GROUPS_A, HEAD_DIM_A)
    win_all = jnp.concatenate([win_buf, win_new], axis=1)
    win_pad = pad_rows(win_all, w_len + WIN_CHUNK).astype(MXU_DTYPE)
    kw = win_pad[:, :, 0].reshape(b, w_len + WIN_CHUNK, KV_A)
    vw = win_pad[:, :, 1].reshape(b, w_len + WIN_CHUNK, KV_A)
    o_nsa = nsa_attention(pad(p['qc']), pad(p['qr']), pad(p['gates']), kvc, ks, vs, kw, vw,
                          tq=TQ_STEP, qpos0=past_len, wbase=past_len - w_len, n_keys=n_keys, kv_t=True)

    kb_, vb, ki = dsa_gather(jnp.swapaxes(cache_dsa, 1, 2), page_table, p['dsa_rows'])
    o_dsa = dsa_attention(pad(p['qb']), pad(p['qi']), pad(p['wi']), kb_, vb, ki, tq=TQ_STEP, qpos0=past_len,
                          n_keep=min(DSA_TOPK, n_keys // 4), kv_t=True, tq_real=t)
    nsa_rows = p['nsa_rows'].reshape(b, t, NSA_SECTIONS, KV_GROUPS_A, HEAD_DIM_A)
    return (o_nsa, o_dsa, pad(gm)), nsa_rows, win_all[:, -w_len:], p['dsa_rows']


def kernel(x_prompt, x_sample, mem_prompt, cache_nsa_kv, state_nsa_win, cache_dsa_kv, cache_mem_kv, state_conv,
           page_table, norm_g, w_in, cmp_pe, cmp_w1, cmp_b1, cmp_w2, w_out_a, w_out_b, w_out, w_mem_q, w_mem_kv,
           w_mem_out, w_up, conv_w, conv_b, w_down, final_g):
    depth = w_in.shape[0]
    d_model = x_prompt.shape[-1]
    d_ff = w_down.shape[1]
    assert CONV_WIDTH == 3
    sizes = (N_HEADS_A * HEAD_DIM_A, 6 * KV_A, 3 * N_HEADS_A, N_HEADS_B * HEAD_DIM_B, 2 * HEAD_DIM_B,
             IDX_HEADS * IDX_DIM, IDX_DIM, IDX_HEADS, 2 * d_model)
    xp, xs = x_prompt, pad_rows(x_sample, TQ_STEP)
    t_step = x_sample.shape[1]
    nsa_p, nsa_s, win_p, win_s, dsa_p, dsa_s, mem_p, conv_p, conv_s = [], [], [], [], [], [], [], [], []
    for l in range(depth):
        assert l == depth - 1, "the fused FFN epilogue applies the final norm"
        branches_p, a, bwin, c = mixer_prompt(xp, norm_g[l, 0], w_in[l], cmp_pe[l], cmp_w1[l], cmp_b1[l], cmp_w2[l],
                                              sizes)
        nsa_p.append(a); win_p.append(bwin); dsa_p.append(c)
        branches_s, a, bwin, c = mixer_sample(xs[:, :t_step], norm_g[l, 0], cache_nsa_kv[l], state_nsa_win[l],
                                              cache_dsa_kv[l], page_table, w_in[l], cmp_pe[l], cmp_w1[l], cmp_b1[l],
                                              cmp_w2[l], sizes)
        nsa_s.append(a); win_s.append(bwin); dsa_s.append(c)
        kv_p = norm_matmul(mem_prompt, norm_g[l, 2], w_mem_kv[l])
        mem_p.append(kv_p.reshape(kv_p.shape[:2] + (2, MEM_HEADS, MEM_HEAD_DIM)))
        kv_s = cache_mem_kv[l].reshape(cache_mem_kv.shape[1:3] + (-1,))
        weights = (norm_g[l, 1], norm_g[l, 3], final_g, w_out_a[l], w_out_b[l], w_out[l], w_mem_q[l], w_mem_out[l],
                   w_up[l], conv_w[l], conv_b[l], w_down[l])
        xp, cp = dense_tail(xp, *branches_p, kv_p, jnp.zeros((xp.shape[0], CONV_WIDTH - 1, 2 * d_ff), xp.dtype),
                            xp.shape[1], *weights)
        xs, cs = dense_tail(xs, *branches_s, kv_s, state_conv[l], t_step, *weights)
        conv_p.append(cp); conv_s.append(cs)
    y_prompt, y_sample = xp, xs[:, :t_step]
    return (y_prompt, y_sample, jnp.stack(nsa_p), jnp.stack(nsa_s), jnp.stack(win_p), jnp.stack(win_s),
            jnp.stack(dsa_p), jnp.stack(dsa_s), jnp.stack(mem_p), jnp.stack(conv_p), jnp.stack(conv_s))
```

```python
import functools

import jax
import jax.numpy as jnp
from jax import lax
from jax.experimental import pallas as pl
from jax.experimental.pallas import tpu as pltpu

N_HEADS_A = 16
HEAD_DIM_A = 64
KV_GROUPS_A = 2
CMP_BLOCK = 32
CMP_STRIDE = 16
SLC_BLOCK = 64
N_SELECT = 16
WINDOW = 512
N_HEADS_B = 8
HEAD_DIM_B = 128
IDX_HEADS = 16
IDX_DIM = 64
DSA_TOPK = 256
MEM_HEADS = 4
MEM_HEAD_DIM = 128
CONV_WIDTH = 3
ROPE_THETA = 500000.0
ROT_FRACTION = 4
EPS = 1e-6
KV_A = KV_GROUPS_A * HEAD_DIM_A
CMP_PER_SLC = SLC_BLOCK // CMP_STRIDE
HPG_A = N_HEADS_A // KV_GROUPS_A
DSA_ROW = 2 * HEAD_DIM_B + IDX_DIM
NSA_SECTIONS = 4

LANE = 128
SUBLANE = 8
BF16_ROWS = 16
VMEM_LIMIT = 48 * 1024 * 1024

NEG_INF = float('-inf')
POS_INF = float('inf')
MXU_DTYPE = jnp.bfloat16
TQ = 128
TQ_STEP = BF16_ROWS
KEY_CHUNK = 512
WIN_CHUNK = 128
PAGES_PER_STEP = 8
GATHER_SLOTS = 3
FFN_ROWS = 1024
BISECT_ITERS = 40
TIE_STRIP_ITERS = 64
SLC_SHIFT = SLC_BLOCK.bit_length() - 1
CMP_PER_SLC_SHIFT = CMP_PER_SLC.bit_length() - 1


def _round_up(n, m):
    return -(-n // m) * m


def _tile(n, cap):
    if n <= cap:
        return n
    best = None
    for t in range(LANE, cap + 1, LANE):
        if n % t == 0:
            best = t
    assert best is not None, (n, cap)
    return best


def _rms(x, g):
    return x * lax.rsqrt(jnp.mean(x * x, axis=-1, keepdims=True) + EPS) * g


def _norm_mm_kernel(x_ref, g_ref, w_ref, o_ref, xn_ref):
    @pl.when(pl.program_id(1) == 0)
    def _():
        xn_ref[...] = _rms(x_ref[...], g_ref[...]).astype(xn_ref.dtype)

    o_ref[...] = jnp.dot(xn_ref[...], w_ref[...], preferred_element_type=jnp.float32).astype(o_ref.dtype)


def norm_matmul(x, g, w):
    lead = x.shape[:-1]
    k = x.shape[-1]
    n = w.shape[-1]
    x2 = x.reshape(-1, k)
    m = x2.shape[0]
    tm = 512 if m % 512 == 0 else m
    tn = _tile(n, 1536)
    out = pl.pallas_call(
        _norm_mm_kernel,
        grid=(m // tm, n // tn),
        in_specs=[pl.BlockSpec((tm, k), lambda i, j: (i, 0)),
                  pl.BlockSpec((1, k), lambda i, j: (0, 0)),
                  pl.BlockSpec((k, tn), lambda i, j: (0, j))],
        out_specs=pl.BlockSpec((tm, tn), lambda i, j: (i, j)),
        out_shape=jax.ShapeDtypeStruct((m, n), jnp.float32),
        scratch_shapes=[pltpu.VMEM((tm, k), MXU_DTYPE)],
        compiler_params=pltpu.CompilerParams(
            dimension_semantics=("parallel", "arbitrary"), vmem_limit_bytes=VMEM_LIMIT),
        name="norm_matmul",
    )(x2, g.reshape(1, k).astype(jnp.float32), w.astype(MXU_DTYPE))
    return out.reshape(lead + (n,))


def _gated_merge_kernel(oa_ref, ob_ref, wa_ref, wb_ref, ga_ref, gb_ref, o_ref):
    dot = functools.partial(jnp.dot, preferred_element_type=jnp.float32)
    ya = dot(oa_ref[...], wa_ref[...])
    yb = dot(ob_ref[...], wb_ref[...])
    o_ref[...] = (jax.nn.sigmoid(ga_ref[...]) * ya + jax.nn.sigmoid(gb_ref[...]) * yb).astype(o_ref.dtype)


def gated_merge(o_nsa, o_dsa, gm, w_oa, w_ob):
    m, ka = o_nsa.shape
    kb = o_dsa.shape[1]
    n = w_oa.shape[1]
    tm = 512 if m % 512 == 0 else m
    tn = _tile(n, 1024)
    nb = n // tn
    return pl.pallas_call(
        _gated_merge_kernel,
        grid=(m // tm, nb),
        in_specs=[pl.BlockSpec((tm, ka), lambda i, j: (i, 0)),
                  pl.BlockSpec((tm, kb), lambda i, j: (i, 0)),
                  pl.BlockSpec((ka, tn), lambda i, j: (0, j)),
                  pl.BlockSpec((kb, tn), lambda i, j: (0, j)),
                  pl.BlockSpec((tm, tn), lambda i, j: (i, j)),
                  pl.BlockSpec((tm, tn), lambda i, j: (i, j + nb))],
        out_specs=pl.BlockSpec((tm, tn), lambda i, j: (i, j)),
        out_shape=jax.ShapeDtypeStruct((m, n), MXU_DTYPE),
        compiler_params=pltpu.CompilerParams(
            dimension_semantics=("parallel", "parallel"), vmem_limit_bytes=VMEM_LIMIT),
        name="gated_merge",
    )(o_nsa, o_dsa, w_oa.astype(MXU_DTYPE), w_ob.astype(MXU_DTYPE), gm, gm)


def _out_proj_kernel(z_ref, w_ref, x_ref, o_ref):
    o_ref[...] = x_ref[...] + jnp.dot(z_ref[...], w_ref[...], preferred_element_type=jnp.float32)


def out_proj_residual(z, w, x):
    m, k = z.shape
    n = w.shape[1]
    tm = 256 if m % 256 == 0 else m
    return pl.pallas_call(
        _out_proj_kernel,
        grid=(m // tm,),
        in_specs=[pl.BlockSpec((tm, k), lambda i: (i, 0)),
                  pl.BlockSpec((k, n), lambda i: (0, 0)),
                  pl.BlockSpec((tm, n), lambda i: (i, 0))],
        out_specs=pl.BlockSpec((tm, n), lambda i: (i, 0)),
        out_shape=jax.ShapeDtypeStruct((m, n), jnp.float32),
        compiler_params=pltpu.CompilerParams(dimension_semantics=("parallel",), vmem_limit_bytes=VMEM_LIMIT),
        name="out_proj_residual",
    )(z, w.astype(MXU_DTYPE), x)


def _mem_block_kernel(x_ref, g1_ref, g2_ref, wq_ref, kv_ref, wo_ref, x_out_ref, xn_out_ref):
    d = MEM_HEAD_DIM
    hd = MEM_HEADS * d
    x = x_ref[0]
    xn = _rms(x, g1_ref[...]).astype(wq_ref.dtype)
    q = jnp.dot(xn, wq_ref[...], preferred_element_type=jnp.float32).astype(wq_ref.dtype)
    outs = []
    for h in range(MEM_HEADS):
        k = kv_ref[0, :, h * d:(h + 1) * d].astype(wq_ref.dtype)
        v = kv_ref[0, :, hd + h * d:hd + (h + 1) * d].astype(wq_ref.dtype)
        s = _dot_nt(q[:, h * d:(h + 1) * d], k) * d ** -0.5
        e = jnp.exp(s - jnp.max(s, axis=1, keepdims=True))
        p = e / jnp.sum(e, axis=1, keepdims=True)
        outs.append(jnp.dot(p.astype(v.dtype), v, preferred_element_type=jnp.float32))
    o = jnp.concatenate(outs, axis=1).astype(wo_ref.dtype)
    x2 = x + jnp.dot(o, wo_ref[...], preferred_element_type=jnp.float32)
    x_out_ref[0] = x2
    xn_out_ref[0] = _rms(x2, g2_ref[...]).astype(xn_out_ref.dtype)


def mem_block(x, g1, g2, w_q, kv, w_o):
    b, t, dm = x.shape
    mt, kvw = kv.shape[1:]
    hd = w_q.shape[1]
    tm = 256 if t % 256 == 0 else t
    row_spec = pl.BlockSpec((1, tm, dm), lambda bi, i: (bi, i, 0))
    g_spec = pl.BlockSpec((1, dm), lambda bi, i: (0, 0))
    return pl.pallas_call(
        _mem_block_kernel,
        grid=(b, t // tm),
        in_specs=[row_spec, g_spec, g_spec,
                  pl.BlockSpec((dm, hd), lambda bi, i: (0, 0)),
                  pl.BlockSpec((1, mt, kvw), lambda bi, i: (bi, 0, 0)),
                  pl.BlockSpec((hd, dm), lambda bi, i: (0, 0))],
        out_specs=[row_spec, row_spec],
        out_shape=[jax.ShapeDtypeStruct((b, t, dm), jnp.float32), jax.ShapeDtypeStruct((b, t, dm), MXU_DTYPE)],
        compiler_params=pltpu.CompilerParams(
            dimension_semantics=("parallel", "parallel"), vmem_limit_bytes=VMEM_LIMIT),
        name="mem_block",
    )(x, g1.reshape(1, dm).astype(jnp.float32), g2.reshape(1, dm).astype(jnp.float32), w_q.astype(MXU_DTYPE),
      kv, w_o.astype(MXU_DTYPE))


def _ffn_up_kernel(x_ref, halo_ref, wg_ref, wu_ref, cwg_ref, cwu_ref, cbg_ref, cbu_ref, pg_ref, pu_ref,
                   h_ref, sg_ref, su_ref, *, nb, tm, t_real):
    i = pl.program_id(2)
    n_halo = halo_ref.shape[1]
    tn = wg_ref.shape[1]
    rows = nb * tm
    x = x_ref[...].reshape(rows, x_ref.shape[2])
    if nb == 1:
        x = jnp.concatenate([halo_ref[0], x], axis=0)
    row = _iota((nb, tm, 1), 1).reshape(rows, 1)
    last = (t_real - 1) // tm
    r_last = (t_real - 1) % tm
    spread = lambda p: jnp.broadcast_to(p, (nb, tm, tn)).reshape(rows, tn)

    def branch(w_ref, cw_ref, cb_ref, p_ref, s_ref):
        u = jnp.dot(x, w_ref[...], preferred_element_type=jnp.float32)
        p0, p1 = p_ref[:, 0:1, :], p_ref[:, 1:2, :]
        if nb == 1:
            uh, u = u[:n_halo], u[n_halo:]
            p0 = jnp.where(i == 0, p0, uh[n_halo - 2:n_halo - 1][None])
            p1 = jnp.where(i == 0, p1, uh[n_halo - 1:n_halo][None])
        p0, p1 = spread(p0), spread(p1)
        u1 = jnp.where(row == 0, p1, pltpu.roll(u, 1, 0))
        u2 = jnp.where(row == 0, p0, jnp.where(row == 1, p1, pltpu.roll(u, 2, 0)))

        @pl.when(i == last)
        def _():
            s_ref[...] = u.reshape(nb, tm, tn)[:, r_last - 1:r_last + 1, :]

        return cb_ref[...] + u2 * cw_ref[0:1, :] + u1 * cw_ref[1:2, :] + u * cw_ref[2:3, :]

    gate = branch(wg_ref, cwg_ref, cbg_ref, pg_ref, sg_ref)
    up = branch(wu_ref, cwu_ref, cbu_ref, pu_ref, su_ref)
    h_ref[...] = (jax.nn.silu(gate) * up).reshape(nb, tm, tn).astype(h_ref.dtype)


def ffn_up(xn, prev, w_up, conv_w, conv_b, t_real):
    b, t, dm = xn.shape
    f2 = w_up.shape[1]
    f = f2 // 2
    tm = FFN_ROWS if t % FFN_ROWS == 0 else t
    nb = 1 if t > tm else max(1, min(b, 512 // tm))
    while b % nb:
        nb -= 1
    tn = _tile(f, 512)
    nf = f // tn
    halo = min(BF16_ROWS, tm)
    hpt = tm // halo
    assert t_real >= 2 and (t_real - 1) % tm >= 1 and tm % SUBLANE == 0
    w_up = w_up.astype(MXU_DTYPE)
    conv_b = conv_b.reshape(1, f2)
    col = lambda off: (lambda j, bi, i: (0, j + off))
    st = lambda off: (lambda j, bi, i: (bi, 0, j + off))
    specs = [pl.BlockSpec((nb, tm, dm), lambda j, bi, i: (bi, i, 0)),
             pl.BlockSpec((1, halo, dm), lambda j, bi, i: (bi * nb, jnp.maximum(i * hpt - 1, 0), 0)),
             pl.BlockSpec((dm, tn), col(0)), pl.BlockSpec((dm, tn), col(nf)),
             pl.BlockSpec((CONV_WIDTH, tn), col(0)), pl.BlockSpec((CONV_WIDTH, tn), col(nf)),
             pl.BlockSpec((1, tn), col(0)), pl.BlockSpec((1, tn), col(nf)),
             pl.BlockSpec((nb, 2, tn), st(0)), pl.BlockSpec((nb, 2, tn), st(nf))]
    h, sg, su = pl.pallas_call(
        functools.partial(_ffn_up_kernel, nb=nb, tm=tm, t_real=t_real),
        grid=(nf, b // nb, t // tm),
        in_specs=specs,
        out_specs=[pl.BlockSpec((nb, tm, tn), lambda j, bi, i: (bi, i, j)),
                   pl.BlockSpec((nb, 2, tn), st(0)), pl.BlockSpec((nb, 2, tn), st(0))],
        out_shape=[jax.ShapeDtypeStruct((b, t, f), MXU_DTYPE), jax.ShapeDtypeStruct((b, 2, f), jnp.float32),
                   jax.ShapeDtypeStruct((b, 2, f), jnp.float32)],
        compiler_params=pltpu.CompilerParams(
            dimension_semantics=("parallel", "parallel", "arbitrary"), vmem_limit_bytes=VMEM_LIMIT),
        name="ffn_up",
    )(xn, xn, w_up, w_up, conv_w, conv_w, conv_b, conv_b, prev, prev)
    return h, jnp.concatenate([sg, su], axis=-1)


def _ffn_down_kernel(h_ref, w_ref, x_ref, g_ref, o_ref, acc_ref):
    @pl.when(pl.program_id(1) == 0)
    def _():
        acc_ref[...] = x_ref[...]

    acc_ref[...] += jnp.dot(h_ref[...], w_ref[...], preferred_element_type=jnp.float32)

    @pl.when(pl.program_id(1) == pl.num_programs(1) - 1)
    def _():
        o_ref[...] = _rms(acc_ref[...], g_ref[...])


def ffn_down_norm(h, w_down, x, g):
    m, f = h.shape
    dm = w_down.shape[1]
    tm = 512 if m % 512 == 0 else m
    tk = _tile(f, 2048)
    return pl.pallas_call(
        _ffn_down_kernel,
        grid=(m // tm, f // tk),
        in_specs=[pl.BlockSpec((tm, tk), lambda i, l: (i, l)),
                  pl.BlockSpec((tk, dm), lambda i, l: (l, 0)),
                  pl.BlockSpec((tm, dm), lambda i, l: (i, 0)),
                  pl.BlockSpec((1, dm), lambda i, l: (0, 0))],
        out_specs=pl.BlockSpec((tm, dm), lambda i, l: (i, 0)),
        out_shape=jax.ShapeDtypeStruct((m, dm), jnp.float32),
        scratch_shapes=[pltpu.VMEM((tm, dm), jnp.float32)],
        compiler_params=pltpu.CompilerParams(
            dimension_semantics=("parallel", "arbitrary"), vmem_limit_bytes=VMEM_LIMIT),
        name="ffn_down_norm",
    )(h, w_down.astype(MXU_DTYPE), x, g.reshape(1, dm).astype(jnp.float32))


def dense_tail(x, o_nsa, o_dsa, gm, kv_mem, prev_u, t_real, g_mem, g_ffn, g_final, w_oa, w_ob, w_o, w_mq, w_mo,
               w_up, conv_w, conv_b, w_down):
    b, t, dm = x.shape
    rows = lambda a: a.reshape(b * t, a.shape[-1])
    z = gated_merge(rows(o_nsa), rows(o_dsa), rows(gm), w_oa, w_ob)
    x1 = out_proj_residual(z, w_o, rows(x)).reshape(b, t, dm)
    x2, xn2 = mem_block(x1, g_mem, g_ffn, w_mq, kv_mem, w_mo)
    h, state = ffn_up(xn2, prev_u, w_up, conv_w, conv_b, t_real)
    y = ffn_down_norm(rows(h), w_down, rows(x2), g_final)
    return y.reshape(b, t, dm), state


def _dot_nt(a, b):
    return lax.dot_general(a, b, (((1,), (1,)), ((), ())), preferred_element_type=jnp.float32)


def _iota(shape, dim):
    return lax.broadcasted_iota(jnp.int32, shape, dim)


def _flash_init(rows, d):
    return (jnp.full((rows, 1), NEG_INF, jnp.float32), jnp.zeros((rows, 1), jnp.float32),
            jnp.zeros((rows, d), jnp.float32))


def _flash_step(carry, q, k, v, madd, nh, scale=None, kv_t=False):
    m, l, acc = carry
    s = jnp.dot(q, k, preferred_element_type=jnp.float32) if kv_t else _dot_nt(q, k)
    if scale is not None:
        s = s * scale
    r, kb = s.shape
    s = (s.reshape(nh, r // nh, kb) + madd[None]).reshape(r, kb)
    m_new = jnp.maximum(m, jnp.max(s, axis=1, keepdims=True))
    m_safe = jnp.where(m_new == NEG_INF, 0.0, m_new)
    p = jnp.exp(s - m_safe)
    alpha = jnp.exp(m - m_safe)
    l = alpha * l + jnp.sum(p, axis=1, keepdims=True)
    pv = _dot_nt(p.astype(v.dtype), v) if kv_t else jnp.dot(p.astype(v.dtype), v, preferred_element_type=jnp.float32)
    return m_new, l, alpha * acc + pv


def _flash_finish(carry):
    _, l, acc = carry
    return acc / jnp.maximum(l, 1e-30)


def _split_dot(x, m01):
    hi = x.astype(jnp.bfloat16)
    r1 = x - hi.astype(jnp.float32)
    mid = r1.astype(jnp.bfloat16)
    lo = (r1 - mid.astype(jnp.float32)).astype(jnp.bfloat16)
    dot = functools.partial(jnp.dot, preferred_element_type=jnp.float32)
    return dot(hi, m01) + dot(mid, m01) + dot(lo, m01)


def _nsa_group(g, i, qc_ref, qr_ref, gate_ref, kc_ref, vc_ref, ks_ref, vs_ref, kw_ref, vw_ref, mask_ref,
               *, tq, kb, qpos0, wbase, ns, n_sel, kv_t):
    nh, d = HPG_A, HEAD_DIM_A
    rows = nh * tq
    nc = kc_ref.shape[3]
    nsp = _round_up(ns, LANE)
    t0 = qpos0 + i * tq
    nj = (t0 + tq - 1) // kb + 1
    cols = slice(g * d, (g + 1) * d)

    def stack_heads(q_ref):
        return jnp.concatenate([q_ref[0, :, (g * nh + h) * d:(g * nh + h + 1) * d] for h in range(nh)], axis=0)

    qc = stack_heads(qc_ref)
    qr = stack_heads(qr_ref)
    tpos = t0 + _iota((tq, 1), 0)

    blk_last = _iota((1, nc), 1) * CMP_STRIDE + (CMP_BLOCK - 1)
    madd_c = jnp.where(blk_last <= tpos, 0.0, NEG_INF)
    s = _dot_nt(qc, kc_ref[0, 0, g]).reshape(nh, tq, nc) + madd_c[None]
    m = jnp.max(s, axis=2, keepdims=True)
    m = jnp.where(m == NEG_INF, 0.0, m)
    e = jnp.exp(s - m)
    p = e / jnp.maximum(jnp.sum(e, axis=2, keepdims=True), 1e-30)
    o_cmp = jnp.dot(p.reshape(rows, nc).astype(vc_ref.dtype), vc_ref[0, 0, g], preferred_element_type=jnp.float32)

    imp = jnp.sum(p, axis=0)
    c_id = _iota((nc, nsp), 0)
    m_id = _iota((nc, nsp), 1)
    overlap = (jnp.right_shift(c_id, CMP_PER_SLC_SHIFT) == m_id) | (c_id == m_id * CMP_PER_SLC - 1)
    score = _split_dot(imp, overlap.astype(jnp.bfloat16))
    blk = _iota((1, nsp), 1)
    cur = jnp.right_shift(tpos, SLC_SHIFT)
    forced = (blk == 0) | (blk == cur) | (blk == cur - 1)
    sc = jnp.where(forced, POS_INF, jnp.where(blk * SLC_BLOCK <= tpos, score, NEG_INF))
    if tq % LANE == 0 and nsp % LANE == 0:
        sc_t = sc.T
        blk_t = _iota((nsp, 1), 0)
        rank_t = jnp.zeros((nsp, tq), jnp.float32)
        for mp in range(ns):
            ref = sc_t[mp:mp + 1, :]
            beats = (ref > sc_t) | ((ref == sc_t) & (blk_t > mp))
            rank_t = rank_t + jnp.where(beats, 1.0, 0.0)
        sel = jnp.where(rank_t < n_sel, 1.0, 0.0).T.astype(jnp.bfloat16)
    else:
        rank = jnp.zeros((tq, nsp), jnp.float32)
        for mp in range(ns):
            col = sc[:, mp:mp + 1]
            beats = (col > sc) | ((col == sc) & (blk > mp))
            rank = rank + jnp.where(beats, 1.0, 0.0)
        sel = jnp.where(rank < n_sel, 1.0, 0.0).astype(jnp.bfloat16)

    def make_mask(j, _):
        kpos = j * kb + _iota((1, kb), 1)
        expand = (jnp.right_shift(j * kb + _iota((nsp, kb), 1), SLC_SHIFT) == _iota((nsp, kb), 0))
        hit = jnp.dot(sel, expand.astype(jnp.bfloat16), preferred_element_type=jnp.float32)
        mask_ref[j] = jnp.where((hit > 0.5) & (kpos <= tpos), 0.0, NEG_INF)
        return 0

    def slc_body(j, carry):
        off = pl.multiple_of(j * kb, kb)
        return _flash_step(carry, qr, ks_ref[0, pl.ds(off, kb), cols], vs_ref[0, pl.ds(off, kb), cols],
                           mask_ref[j], nh)

    if kv_t:
        sel_f = sel.astype(jnp.float32)
        low_half = _iota((tq, LANE), 1) < SLC_BLOCK
        pieces = []
        for v in range(kb // LANE):
            hit = jnp.where(low_half, sel_f[:, 2 * v:2 * v + 1], sel_f[:, 2 * v + 1:2 * v + 2])
            kpos = v * LANE + _iota((1, LANE), 1)
            pieces.append(jnp.where((hit > 0.5) & (kpos <= tpos), 0.0, NEG_INF))
        wide = lambda ref: jnp.concatenate([ref[0, c, cols, :] for c in range(ref.shape[1])], axis=1)
        o_slc = _flash_finish(_flash_step(_flash_init(rows, d), qr, wide(ks_ref), wide(vs_ref),
                                          jnp.concatenate(pieces, axis=1), nh, kv_t=True))
    else:
        lax.fori_loop(0, nj, make_mask, 0)
        o_slc = _flash_finish(lax.fori_loop(0, nj, slc_body, _flash_init(rows, d)))

    wk = min(_round_up(WINDOW + tq, WIN_CHUNK), kw_ref.shape[1])
    first = jnp.maximum(t0 - (WINDOW - 1) - wbase, 0) // WIN_CHUNK * WIN_CHUNK
    off = pl.multiple_of(jnp.minimum(first, kw_ref.shape[1] - wk), WIN_CHUNK)
    dist = tpos - (wbase + off + _iota((1, wk), 1))
    madd_w = jnp.where((dist >= 0) & (dist < WINDOW), 0.0, NEG_INF)
    o_win = _flash_finish(_flash_step(_flash_init(rows, d), qr, kw_ref[0, pl.ds(off, wk), cols],
                                      vw_ref[0, pl.ds(off, wk), cols], madd_w, nh))

    gates = gate_ref[0]
    outs = []
    for h in range(nh):
        rs = slice(h * tq, (h + 1) * tq)
        c = g * nh + h
        outs.append(gates[:, c:c + 1] * o_cmp[rs] + gates[:, N_HEADS_A + c:N_HEADS_A + c + 1] * o_slc[rs]
                    + gates[:, 2 * N_HEADS_A + c:2 * N_HEADS_A + c + 1] * o_win[rs])
    return outs


def _nsa_kernel(qc_ref, qr_ref, gate_ref, kc_ref, vc_ref, ks_ref, vs_ref, kw_ref, vw_ref, o_ref, mask_ref, **kw):
    i = pl.program_id(1)
    outs = []
    for g in range(KV_GROUPS_A):
        outs += _nsa_group(g, i, qc_ref, qr_ref, gate_ref, kc_ref, vc_ref, ks_ref, vs_ref, kw_ref, vw_ref,
                           mask_ref, **kw)
    o_ref[0] = jnp.concatenate(outs, axis=1).astype(o_ref.dtype)


def nsa_attention(qc, qr, gates, kvc, ks, vs, kw, vw, *, tq, qpos0, wbase, n_keys, kv_t=False):
    b, t, hd = qc.shape
    l = ks.shape[1] * ks.shape[3] if kv_t else ks.shape[1]
    lw = kw.shape[1]
    nc, d = kvc.shape[3:]
    kb = min(KEY_CHUNK, l) if tq >= TQ else l
    assert l % kb == 0 and t % tq == 0 and lw % WIN_CHUNK == 0 and (kb == l or not kv_t)
    assert (qpos0 + t - 1) // kb + 1 <= l // kb and (qpos0 + t - 1 - wbase) // WIN_CHUNK + 1 <= lw // WIN_CHUNK
    assert (qpos0 - wbase) % WIN_CHUNK == 0 and WIN_CHUNK % tq == 0 and tq > 1
    assert 2 * SLC_BLOCK == LANE or not kv_t
    ns = l // SLC_BLOCK
    n_sel = min(N_SELECT, -(-n_keys // SLC_BLOCK))
    q_spec = pl.BlockSpec((1, tq, hd), lambda bi, i: (bi, i, 0))
    kc_spec = pl.BlockSpec((1, 1, KV_GROUPS_A, nc, d), lambda bi, i: (0, bi, 0, 0, 0))
    vc_spec = pl.BlockSpec((1, 1, KV_GROUPS_A, nc, d), lambda bi, i: (1, bi, 0, 0, 0))
    k_spec = (pl.BlockSpec((1,) + ks.shape[1:], lambda bi, i: (bi, 0, 0, 0)) if kv_t
              else pl.BlockSpec((1, l, KV_A), lambda bi, i: (bi, 0, 0)))
    w_spec = pl.BlockSpec((1, lw, KV_A), lambda bi, i: (bi, 0, 0))
    return pl.pallas_call(
        functools.partial(_nsa_kernel, tq=tq, kb=kb, qpos0=qpos0, wbase=wbase, ns=ns, n_sel=n_sel, kv_t=kv_t),
        grid=(b, t // tq),
        in_specs=[q_spec, q_spec, pl.BlockSpec((1, tq, 3 * N_HEADS_A), lambda bi, i: (bi, i, 0)),
                  kc_spec, vc_spec, k_spec, k_spec, w_spec, w_spec],
        out_specs=q_spec,
        out_shape=jax.ShapeDtypeStruct((b, t, hd), MXU_DTYPE),
        scratch_shapes=[pltpu.VMEM((l // kb, tq, kb), jnp.float32)],
        compiler_params=pltpu.CompilerParams(
            dimension_semantics=("parallel", "arbitrary"), vmem_limit_bytes=VMEM_LIMIT),
        name="nsa_attention",
    )(qc, qr, gates, kvc, kvc, ks, vs, kw, vw)


def _dsa_kernel(qb_ref, qi_ref, wi_ref, kb_ref, vb_ref, ki_ref, o_ref, score_ref,
                *, tq, tq_real, kb, qpos0, n_keep, kv_t):
    i = pl.program_id(1)
    nh, d = N_HEADS_B, HEAD_DIM_B
    t0 = qpos0 + i * tq
    nj = (t0 + tq - 1) // kb + 1
    tpos = t0 + _iota((tq, 1), 0)
    w = wi_ref[0]
    wide = lambda ref: jnp.concatenate([ref[0, c] for c in range(ref.shape[1])], axis=1)

    def idx_body(j, carry):
        lo, hi = carry
        off = pl.multiple_of(j * kb, kb)
        kidx = wide(ki_ref) if kv_t else ki_ref[0, pl.ds(off, kb), :]
        acc = jnp.zeros((tq, kb), jnp.float32)
        heads = [qi_ref[0, :, h * IDX_DIM:(h + 1) * IDX_DIM] for h in range(IDX_HEADS)]
        if kv_t:
            dots_all = jnp.dot(jnp.concatenate(heads, axis=0), kidx, preferred_element_type=jnp.float32)
        for h in range(IDX_HEADS):
            dots = dots_all[h * tq:(h + 1) * tq] if kv_t else _dot_nt(heads[h], kidx)
            acc = acc + w[:, h:h + 1] * jnp.maximum(dots, 0.0)
        vis = (off + _iota((1, kb), 1)) <= tpos
        score_ref[j] = jnp.where(vis, acc, NEG_INF)
        lo = jnp.minimum(lo, jnp.min(jnp.where(vis, acc, POS_INF), axis=1, keepdims=True))
        hi = jnp.maximum(hi, jnp.max(jnp.where(vis, acc, NEG_INF), axis=1, keepdims=True))
        return lo, hi

    lo, hi = lax.fori_loop(0, nj, idx_body, (jnp.full((tq, 1), POS_INF, jnp.float32),
                                             jnp.full((tq, 1), NEG_INF, jnp.float32)))

    k = float(n_keep)
    n_vis = (tpos + 1).astype(jnp.float32)
    n_keys = score_ref.shape[0] * kb

    def reduce_scores(pred, pick, init, combine, lane_reduce):
        def body(j, acc):
            sc = score_ref[j]
            for c in range(kb // LANE):
                kpos = (j * kb + c * LANE + _iota((1, LANE), 1)).astype(jnp.float32)
                piece = sc[:, c * LANE:(c + 1) * LANE]
                acc = combine(acc, pick(pred(piece, kpos), piece))
            return acc

        return lane_reduce(lax.fori_loop(0, nj, body, jnp.full((tq, LANE), init, jnp.float32)),
                           axis=1, keepdims=True)

    def count_where(pred):
        return reduce_scores(pred, lambda m, _: jnp.where(m, 1.0, 0.0), 0.0, jnp.add, jnp.sum)

    def min_where(pred):
        return reduce_scores(pred, lambda m, x: jnp.where(m, x, POS_INF), POS_INF, jnp.minimum, jnp.min)

    real = _iota((tq, 1), 0) < tq_real

    def any_row(flag):
        return jnp.max(jnp.where(real & flag, 1.0, 0.0)) > 0.0

    def bisect(state):
        it, lo, hi, cnt_lo = state
        mid = 0.5 * (lo + hi)
        mid_b = jnp.broadcast_to(mid, (tq, LANE))
        cnt = count_where(lambda x, _: x >= mid_b)
        ge = cnt >= k
        return it + 1, jnp.where(ge, mid, lo), jnp.where(ge, hi, mid), jnp.where(ge, cnt, cnt_lo)

    _, thr, _, cnt_lo = lax.while_loop(
        lambda st: (st[0] < BISECT_ITERS) & any_row((st[3] != k) & (n_vis > k)), lambda st: bisect(bisect(st)),
        (jnp.int32(0), lo, hi, n_vis))

    def break_ties():
        def above(v):
            v_b = jnp.broadcast_to(v, (tq, LANE))
            return count_where(lambda x, _: x > v_b)

        def strip(state):
            it, v, c_gt = state
            v_b = jnp.broadcast_to(v, (tq, LANE))
            v_next = jnp.where(c_gt >= k, min_where(lambda x, _: x > v_b), v)
            return it + 1, v_next, above(v_next)

        thr_b = jnp.broadcast_to(thr, (tq, LANE))
        v0 = min_where(lambda x, _: x >= thr_b)
        _, v, c_gt = lax.while_loop(lambda st: (st[0] < TIE_STRIP_ITERS) & any_row(st[2] >= k), strip,
                                    (jnp.int32(0), v0, above(v0)))
        need = k - c_gt
        v_b = jnp.broadcast_to(v, (tq, LANE))

        def narrow(_, bounds):
            j_lo, j_hi = bounds
            mid = jnp.floor(0.5 * (j_lo + j_hi))
            ge = count_where(lambda x, kpos: (x == v_b) & (kpos <= mid)) >= need
            return jnp.where(ge, j_lo, mid), jnp.where(ge, mid, j_hi)

        _, j_max = lax.fori_loop(0, n_keys.bit_length(), narrow,
                                 (jnp.full((tq, 1), -1.0, jnp.float32), jnp.full((tq, 1), n_keys - 1.0, jnp.float32)))
        return v, j_max

    thr, j_max = lax.cond(any_row((cnt_lo > k) & (n_vis > k)), break_ties,
                          lambda: (thr, jnp.full((tq, 1), float(n_keys), jnp.float32)))

    q = jnp.concatenate([qb_ref[0, :, h * d:(h + 1) * d] for h in range(nh)], axis=0)

    def att_body(j, carry):
        off = pl.multiple_of(j * kb, kb)
        sc = score_ref[j]
        kpos = (off + _iota((1, kb), 1)).astype(jnp.float32)
        madd = jnp.where((sc > thr) | ((sc == thr) & (kpos <= j_max)), 0.0, NEG_INF)
        if kv_t:
            return _flash_step(carry, q, wide(kb_ref), wide(vb_ref), madd, nh, scale=d ** -0.5, kv_t=True)
        return _flash_step(carry, q, kb_ref[0, pl.ds(off, kb), :], vb_ref[0, pl.ds(off, kb), :], madd, nh,
                           scale=d ** -0.5)

    o = _flash_finish(lax.fori_loop(0, nj, att_body, _flash_init(nh * tq, d)))
    o_ref[0] = jnp.concatenate([o[h * tq:(h + 1) * tq] for h in range(nh)], axis=1).astype(o_ref.dtype)


def dsa_attention(qb, qi, wi, kb_, vb, ki, *, tq, qpos0, n_keep, kv_t=False, tq_real=None):
    b, t, hd = qb.shape
    l = kb_.shape[1] * kb_.shape[3] if kv_t else kb_.shape[1]
    kb = min(KEY_CHUNK, l) if tq >= TQ else l
    assert l % kb == 0 and t % tq == 0 and (qpos0 + t - 1) // kb + 1 <= l // kb and (kb == l or not kv_t)
    steps = kb_.shape[1]
    kv_spec = lambda width: (pl.BlockSpec((1, steps, width, l // steps), lambda bi, i: (bi, 0, 0, 0)) if kv_t
                             else pl.BlockSpec((1, l, width), lambda bi, i: (bi, 0, 0)))
    return pl.pallas_call(
        functools.partial(_dsa_kernel, tq=tq, tq_real=tq if tq_real is None else tq_real, kb=kb, qpos0=qpos0,
                          n_keep=n_keep, kv_t=kv_t),
        grid=(b, t // tq),
        in_specs=[pl.BlockSpec((1, tq, hd), lambda bi, i: (bi, i, 0)),
                  pl.BlockSpec((1, tq, IDX_HEADS * IDX_DIM), lambda bi, i: (bi, i, 0)),
                  pl.BlockSpec((1, tq, IDX_HEADS), lambda bi, i: (bi, i, 0)),
                  kv_spec(HEAD_DIM_B), kv_spec(HEAD_DIM_B), kv_spec(IDX_DIM)],
        out_specs=pl.BlockSpec((1, tq, hd), lambda bi, i: (bi, i, 0)),
        out_shape=jax.ShapeDtypeStruct((b, t, hd), MXU_DTYPE),
        scratch_shapes=[pltpu.VMEM((l // kb, tq, kb), jnp.float32)],
        compiler_params=pltpu.CompilerParams(
            dimension_semantics=("parallel", "arbitrary"), vmem_limit_bytes=VMEM_LIMIT),
        name="dsa_attention",
    )(qb, qi, wi, kb_, vb, ki)


def _page_maps(n_pages, pp):
    n_steps = n_pages // pp

    def page_map(r):
        return lambda b, s, pt: (pt[b * n_pages + jnp.minimum(s, n_steps - 1) * pp + r], 0, 0)

    return n_steps, page_map


def _chunk_rows(tok_ref, sec, row0, n):
    d = HEAD_DIM_A
    first_half = _iota((n, KV_A), 1) < d
    pieces = [[] for _ in range(KV_GROUPS_A)]
    for j in range(0, CMP_STRIDE, 2):
        a, b = [tok_ref[sec, pl.ds(row0 + jj, n, stride=CMP_STRIDE), :] for jj in (j, j + 1)]
        pieces[0].append(jnp.where(first_half, a, pltpu.roll(b, d, 1)))
        pieces[1].append(jnp.where(first_half, pltpu.roll(a, d, 1), b))
    return [jnp.concatenate(p, axis=1) for p in pieces]


def _nsa_gather_kernel(pt_ref, *refs, pp, n_steps, rows):
    del pt_ref
    pages, (tail_tok_ref, tail_t_ref, pe_ref) = refs[:pp], refs[pp:pp + 3]
    zt_ref, zb_ref, ks_ref, vs_ref, tok_ref = refs[pp + 3:]
    is_tail = pl.program_id(1) == n_steps
    z = [[[] for _ in range(KV_GROUPS_A)] for _ in range(2)]
    for r in range(pp):
        cs = slice(r * rows, (r + 1) * rows)
        ks_ref[0, 0, :, cs] = jnp.where(is_tail, tail_t_ref[0, :KV_A, cs],
                                     pages[r][0, 2 * KV_A:3 * KV_A, :]).astype(ks_ref.dtype)
        vs_ref[0, 0, :, cs] = jnp.where(is_tail, tail_t_ref[0, KV_A:, cs],
                                     pages[r][0, 3 * KV_A:, :]).astype(vs_ref.dtype)
        for sec in range(2):
            tok_ref[sec, cs, :] = jnp.where(is_tail, tail_tok_ref[0, sec, cs, :],
                                            pages[r][0, sec * KV_A:(sec + 1) * KV_A, :].T)
            for g, zg in enumerate(_chunk_rows(tok_ref, sec, r * rows, rows // CMP_STRIDE)):
                z[sec][g].append(zg)
    for sec in range(2):
        for g in range(KV_GROUPS_A):
            zf = jnp.concatenate(z[sec][g], axis=0)
            zt_ref[sec, 0, g] = (zf + pe_ref[sec, 0]).astype(zt_ref.dtype)
            zb_ref[sec, 0, g] = (zf + pe_ref[sec, 1]).astype(zb_ref.dtype)


def nsa_gather(cache_t, page_table, new_rows, cmp_pe):
    db, n_pages = page_table.shape
    width, rows = cache_t.shape[1:]
    pp = PAGES_PER_STEP
    n_steps, page_map = _page_maps(n_pages, pp)
    l = (n_steps + 1) * pp * rows
    cps = pp * rows // CMP_STRIDE
    flat = CMP_STRIDE * HEAD_DIM_A
    pe = cmp_pe.reshape(2, 2, 1, flat).astype(jnp.float32)
    tail = pad_rows(new_rows, pp * rows)
    tail_tok = tail[:, :, :2 * KV_A].reshape(db, pp * rows, 2, KV_A).swapaxes(1, 2)
    tail_t = tail[:, :, 2 * KV_A:].swapaxes(1, 2)
    z_spec = pl.BlockSpec((2, 1, KV_GROUPS_A, cps, flat), lambda b, s, pt: (0, b, 0, s, 0))
    r_spec = pl.BlockSpec((1, 1, KV_A, pp * rows), lambda b, s, pt: (b, s, 0, 0))
    z_shape = jax.ShapeDtypeStruct((2, db, KV_GROUPS_A, l // CMP_STRIDE, flat), MXU_DTYPE)
    r_shape = jax.ShapeDtypeStruct((db, n_steps + 1, KV_A, pp * rows), MXU_DTYPE)
    return pl.pallas_call(
        functools.partial(_nsa_gather_kernel, pp=pp, n_steps=n_steps, rows=rows),
        grid_spec=pltpu.PrefetchScalarGridSpec(
            num_scalar_prefetch=1,
            grid=(db, n_steps + 1),
            in_specs=[pl.BlockSpec((1, width, rows), page_map(r)) for r in range(pp)]
            + [pl.BlockSpec((1, 2, pp * rows, KV_A), lambda b, s, pt: (b, 0, 0, 0)),
               pl.BlockSpec((1, 2 * KV_A, pp * rows), lambda b, s, pt: (b, 0, 0)),
               pl.BlockSpec((2, 2, 1, flat), lambda b, s, pt: (0, 0, 0, 0))],
            out_specs=[z_spec, z_spec, r_spec, r_spec],
            scratch_shapes=[pltpu.VMEM((2, pp * rows, KV_A), jnp.float32)]),
        out_shape=[z_shape, z_shape, r_shape, r_shape],
        compiler_params=pltpu.CompilerParams(
            dimension_semantics=("parallel", "arbitrary"), vmem_limit_bytes=VMEM_LIMIT),
        name="nsa_gather",
    )(page_table.reshape(-1), *([cache_t] * pp), tail_tok, tail_t, pe)


def _dsa_gather_kernel(pt_ref, cache_ref, tail_ref, k_ref, v_ref, i_ref, buf_ref, sem_ref,
                       *, pp, n_steps, n_pages, rows):
    per_seq = n_steps + 1
    total = pl.num_programs(0) * per_seq
    g = pl.program_id(0) * per_seq + pl.program_id(1)
    is_tail = pl.program_id(1) == n_steps
    d = HEAD_DIM_B

    def has_pages(gi):
        return (gi < total) & (gi % per_seq < n_steps)

    def page_copies(gi):
        b, s, slot = gi // per_seq, gi % per_seq, gi % GATHER_SLOTS
        return [pltpu.make_async_copy(cache_ref.at[pt_ref[b * n_pages + s * pp + r]], buf_ref.at[slot, r],
                                      sem_ref.at[slot, r]) for r in range(pp)]

    def start(gi):
        @pl.when(has_pages(gi))
        def _():
            for c in page_copies(gi):
                c.start()

    @pl.when(g == 0)
    def _():
        for ahead in range(GATHER_SLOTS - 1):
            start(g + ahead)

    start(g + GATHER_SLOTS - 1)

    def emit(r, x):
        cs = slice(r * rows, (r + 1) * rows)
        k_ref[0, 0, :, cs] = x[:d].astype(k_ref.dtype)
        v_ref[0, 0, :, cs] = x[d:2 * d].astype(v_ref.dtype)
        i_ref[0, 0, :, cs] = x[2 * d:].astype(i_ref.dtype)

    @pl.when(is_tail)
    def _():
        for r in range(pp):
            emit(r, tail_ref[0, :, r * rows:(r + 1) * rows])

    @pl.when(jnp.logical_not(is_tail))
    def _():
        for c in page_copies(g):
            c.wait()
        for r in range(pp):
            emit(r, buf_ref[g % GATHER_SLOTS, r])


def dsa_gather(cache_t, page_table, new_rows):
    db, n_pages = page_table.shape
    width, rows = cache_t.shape[1:]
    pp = PAGES_PER_STEP
    n_steps = n_pages // pp
    tail_t = pad_rows(new_rows, pp * rows).swapaxes(1, 2)
    out_spec = lambda w: pl.BlockSpec((1, 1, w, pp * rows), lambda b, s, pt: (b, s, 0, 0))
    widths = (HEAD_DIM_B, HEAD_DIM_B, IDX_DIM)
    return pl.pallas_call(
        functools.partial(_dsa_gather_kernel, pp=pp, n_steps=n_steps, n_pages=n_pages, rows=rows),
        grid_spec=pltpu.PrefetchScalarGridSpec(
            num_scalar_prefetch=1,
            grid=(db, n_steps + 1),
            in_specs=[pl.BlockSpec(memory_space=pl.ANY),
                      pl.BlockSpec((1, width, pp * rows), lambda b, s, pt: (b, 0, 0))],
            out_specs=[out_spec(w) for w in widths],
            scratch_shapes=[pltpu.VMEM((GATHER_SLOTS, pp, width, rows), jnp.float32),
                            pltpu.SemaphoreType.DMA((GATHER_SLOTS, pp))]),
        out_shape=[jax.ShapeDtypeStruct((db, n_steps + 1, w, pp * rows), MXU_DTYPE) for w in widths],
        compiler_params=pltpu.CompilerParams(
            dimension_semantics=("arbitrary", "arbitrary"), vmem_limit_bytes=VMEM_LIMIT),
        name="dsa_gather",
    )(page_table.reshape(-1), cache_t, tail_t)


def _compress_kernel(zt_ref, zb_ref, pe_ref, w1t_ref, w1b_ref, b1_ref, w2_ref, o_ref, ab_ref, *, ncp):
    dot = functools.partial(jnp.dot, preferred_element_type=jnp.float32)
    ch = zb_ref.shape[2]
    at = dot(zt_ref[0, 0], w1t_ref[0])
    ab_ref[:ch] = dot(zb_ref[0, 0], w1b_ref[0])
    pe_rows = jnp.broadcast_to(pe_ref[0, 1], (SUBLANE, pe_ref.shape[3])).astype(zb_ref.dtype)
    ab_ref[ch:] = dot(pe_rows, w1b_ref[0])
    h = jax.nn.gelu(at[:ncp] + ab_ref[pl.ds(1, ncp), :] + b1_ref[0])
    o_ref[0, 0, :ncp] = dot(h.astype(w2_ref.dtype), w2_ref[0]).astype(o_ref.dtype)
    if o_ref.shape[2] > ncp:
        o_ref[0, 0, ncp:] = jnp.zeros((o_ref.shape[2] - ncp, o_ref.shape[3]), o_ref.dtype)


def compress(zt, zb, cmp_pe, w1, b1, w2, n_keys):
    ncp = _round_up(-(-n_keys // CMP_STRIDE), BF16_ROWS)
    ncl = ncp if ncp <= LANE else _round_up(ncp, LANE)
    _, nb, ch, kdim = zt.shape
    hid = w1.shape[-1]
    d = w2.shape[-1]
    assert ch + SUBLANE >= ncp + 1 and ch % SUBLANE == 0
    w1 = w1.astype(MXU_DTYPE)
    pe = cmp_pe.reshape(2, 2, 1, kdim).astype(jnp.float32)
    z_spec = pl.BlockSpec((1, 1, ch, kdim), lambda s, n: (s, n, 0, 0))
    return pl.pallas_call(
        functools.partial(_compress_kernel, ncp=ncp),
        grid=(2, nb),
        in_specs=[z_spec, z_spec,
                  pl.BlockSpec((1, 2, 1, kdim), lambda s, n: (s, 0, 0, 0)),
                  pl.BlockSpec((1, kdim, hid), lambda s, n: (s, 0, 0)),
                  pl.BlockSpec((1, kdim, hid), lambda s, n: (s, 1, 0)),
                  pl.BlockSpec((1, 1, hid), lambda s, n: (s, 0, 0)),
                  pl.BlockSpec((1, hid, d), lambda s, n: (s, 0, 0))],
        out_specs=pl.BlockSpec((1, 1, ncl, d), lambda s, n: (s, n, 0, 0)),
        out_shape=jax.ShapeDtypeStruct((2, nb, ncl, d), MXU_DTYPE),
        scratch_shapes=[pltpu.VMEM((ch + SUBLANE, hid), jnp.float32)],
        compiler_params=pltpu.CompilerParams(
            dimension_semantics=("parallel", "parallel"), vmem_limit_bytes=VMEM_LIMIT),
        name="compress",
    )(zt, zb, pe, w1, w1, b1.reshape(2, 1, hid).astype(jnp.float32), w2.astype(MXU_DTYPE))


def pad_rows(a, n):
    return jnp.pad(a, [(0, 0), (0, n - a.shape[1])] + [(0, 0)] * (a.ndim - 2))


def _rope_tables(pos, d):
    half = d // ROT_FRACTION // 2
    inv = ROPE_THETA ** (-jnp.arange(half, dtype=jnp.float32) / half)
    ang = pos.astype(jnp.float32)[:, None] * inv[None, :]
    lane = jnp.arange(LANE) % d
    cos = jnp.cos(ang)[:, lane % half]
    sin = jnp.sin(ang)[:, lane % half]
    one, zero = jnp.ones_like(cos), jnp.zeros_like(cos)
    c = jnp.where(lane < 2 * half, cos, one)
    sa = jnp.where((lane >= half) & (lane < 2 * half), sin, zero)
    sb = jnp.where(lane < half, -sin, zero)
    return jnp.stack([c, sa, sb])


def _rope(x, t_ref, half):
    c, sa, sb = t_ref[0], t_ref[1], t_ref[2]
    outs = []
    for j in range(x.shape[1] // LANE):
        xs = x[:, j * LANE:(j + 1) * LANE]
        outs.append(xs * c + pltpu.roll(xs, half, 1) * sa + pltpu.roll(xs, LANE - half, 1) * sb)
    return outs[0] if len(outs) == 1 else jnp.concatenate(outs, axis=1)


_QA = N_HEADS_A * HEAD_DIM_A
_KVA = 6 * KV_A
_QB = N_HEADS_B * HEAD_DIM_B
_KVB = 2 * HEAD_DIM_B
_QI = IDX_HEADS * IDX_DIM
_MISC = IDX_DIM + IDX_HEADS + 3 * N_HEADS_A
PREP_WIDTH = _QA + _KVA + _QB + _KVB + _QI + _MISC
assert _MISC == LANE


def _prep_kernel(x_ref, t64_ref, t128_ref, pe_ref, qc_ref, qr_ref, qb_ref, qi_ref, nsa_ref, win_ref, dsa_ref,
                 ks_ref, vs_ref, kw_ref, vw_ref, kb_ref, vb_ref, ki_ref, wi_ref, gate_ref, *z_refs, tm):
    x = x_ref[...]
    o = 0
    qa = x[:, o:o + _QA]; o += _QA
    kva = x[:, o:o + _KVA]; o += _KVA
    qb = x[:, o:o + _QB]; o += _QB
    kvb = x[:, o:o + _KVB]; o += _KVB
    qi = x[:, o:o + _QI]; o += _QI
    misc = x[:, o:o + _MISC]
    h64, h128 = HEAD_DIM_A // ROT_FRACTION // 2, HEAD_DIM_B // ROT_FRACTION // 2
    sec = lambda i: kva[:, i * KV_A:(i + 1) * KV_A]
    mx = lambda a: a.astype(qc_ref.dtype)
    qc_ref[...] = mx(qa * HEAD_DIM_A ** -0.5)
    qr_ref[...] = mx(_rope(qa, t64_ref, h64) * HEAD_DIM_A ** -0.5)
    k_slc, k_win = _rope(sec(2), t64_ref, h64), _rope(sec(4), t64_ref, h64)
    nsa_ref[:, :2 * KV_A] = kva[:, :2 * KV_A]
    nsa_ref[:, 2 * KV_A:3 * KV_A] = k_slc
    nsa_ref[:, 3 * KV_A:] = sec(3)
    win_ref[:, :KV_A] = k_win
    win_ref[:, KV_A:] = sec(5)
    ks_ref[...], vs_ref[...], kw_ref[...], vw_ref[...] = mx(k_slc), mx(sec(3)), mx(k_win), mx(sec(5))
    qb_ref[...] = mx(_rope(qb, t128_ref, h128))
    k_b, v_b = _rope(kvb[:, :HEAD_DIM_B], t128_ref, h128), kvb[:, HEAD_DIM_B:]
    k_idx = _rope(misc, t64_ref, h64)[:, :IDX_DIM]
    dsa_ref[:, :HEAD_DIM_B] = k_b
    dsa_ref[:, HEAD_DIM_B:2 * HEAD_DIM_B] = v_b
    dsa_ref[:, 2 * HEAD_DIM_B:] = k_idx
    kb_ref[...], vb_ref[...], ki_ref[...] = mx(k_b), mx(v_b), mx(k_idx)
    qi_ref[...] = mx(_rope(qi, t64_ref, h64) * IDX_DIM ** -0.5)
    wi_ref[...] = misc[:, IDX_DIM:IDX_DIM + IDX_HEADS] * IDX_HEADS ** -0.5
    gate_ref[...] = jax.nn.sigmoid(misc[:, IDX_DIM + IDX_HEADS:])
    if z_refs:
        zt_ref, zb_ref, tok_ref = z_refs
        for s_ in range(2):
            tok_ref[s_] = sec(s_)
            for g, zg in enumerate(_chunk_rows(tok_ref, s_, 0, tm // CMP_STRIDE)):
                zt_ref[s_, 0, g] = (zg + pe_ref[s_, 0]).astype(zt_ref.dtype)
                zb_ref[s_, 0, g] = (zg + pe_ref[s_, 1]).astype(zb_ref.dtype)


def prep(proj, pos, cmp_pe, with_chunks):
    b, t, width = proj.shape
    assert width == PREP_WIDTH
    m = b * t
    tm = 256 if t % 256 == 0 else m
    assert m % tm == 0 and t % tm in (0, t)
    nt = max(t // tm, 1)
    flat = CMP_STRIDE * HEAD_DIM_A
    names = ['qc', 'qr', 'qb', 'qi', 'nsa_rows', 'win_rows', 'dsa_rows', 'ks', 'vs', 'kw', 'vw', 'kb', 'vb', 'ki',
             'wi', 'gates']
    widths = [_QA, _QA, _QB, _QI, NSA_SECTIONS * KV_A, 2 * KV_A, DSA_ROW, KV_A, KV_A, KV_A, KV_A, HEAD_DIM_B,
              HEAD_DIM_B, IDX_DIM, IDX_HEADS, 3 * N_HEADS_A]
    dtypes = [MXU_DTYPE] * 4 + [jnp.float32] * 3 + [MXU_DTYPE] * 7 + [jnp.float32] * 2
    row = lambda w: pl.BlockSpec((tm, w), lambda i: (i, 0))
    out_specs = [row(w) for w in widths]
    out_shape = [jax.ShapeDtypeStruct((m, w), dt) for w, dt in zip(widths, dtypes)]
    scratch = []
    if with_chunks:
        assert tm % CMP_STRIDE == 0 and t % tm == 0
        z_spec = pl.BlockSpec((2, 1, KV_GROUPS_A, tm // CMP_STRIDE, flat), lambda i: (0, i // nt, 0, i % nt, 0))
        z_shape = jax.ShapeDtypeStruct((2, b, KV_GROUPS_A, t // CMP_STRIDE, flat), MXU_DTYPE)
        out_specs += [z_spec, z_spec]
        out_shape += [z_shape, z_shape]
        names += ['zt', 'zb']
        scratch = [pltpu.VMEM((2, tm, KV_A), jnp.float32)]
    t_spec = pl.BlockSpec((3, tm, LANE), lambda i: (0, i, 0))
    outs = pl.pallas_call(
        functools.partial(_prep_kernel, tm=tm),
        grid=(m // tm,),
        in_specs=[row(width), t_spec, t_spec, pl.BlockSpec((2, 2, 1, flat), lambda i: (0, 0, 0, 0))],
        out_specs=out_specs,
        out_shape=out_shape,
        scratch_shapes=scratch,
        compiler_params=pltpu.CompilerParams(dimension_semantics=("parallel",), vmem_limit_bytes=VMEM_LIMIT),
        name="prep",
    )(proj.reshape(m, width), _rope_tables(pos, HEAD_DIM_A), _rope_tables(pos, HEAD_DIM_B),
      cmp_pe.reshape(2, 2, 1, flat).astype(jnp.float32))
    return {n: (o if o.ndim > 2 else o.reshape(b, t, o.shape[-1])) for n, o in zip(names, outs)}


def project(x, g, w_in, sizes):
    starts = [sum(sizes[:i]) for i in range(len(sizes))]
    qa, kva, ga, qb, kvb, qi, ki, wi, gm = [slice(o, o + n) for o, n in zip(starts, sizes)]
    w_bf = w_in.astype(MXU_DTYPE)
    w_prep = jnp.concatenate([w_bf[:, c] for c in (qa, kva, qb, kvb, qi, ki, wi, ga)], axis=1)
    return norm_matmul(x, g, w_prep), norm_matmul(x, g, w_bf[:, gm])


def mixer_prompt(x, g, w_in, cmp_pe, cmp_w1, cmp_b1, cmp_w2, sizes):
    b, s, _ = x.shape
    proj, gm = project(x, g, w_in, sizes)
    p = prep(proj, jnp.tile(jnp.arange(s, dtype=jnp.int32), b), cmp_pe, True)
    merge_bg = lambda a: a.reshape((2, b * KV_GROUPS_A) + a.shape[3:])
    kvc = compress(merge_bg(p['zt']), merge_bg(p['zb']), cmp_pe, cmp_w1, cmp_b1, cmp_w2, s)
    kvc = kvc.reshape(2, b, KV_GROUPS_A, kvc.shape[2], HEAD_DIM_A)
    o_nsa = nsa_attention(p['qc'], p['qr'], p['gates'], kvc, p['ks'], p['vs'], p['kw'], p['vw'],
                          tq=min(TQ, s), qpos0=0, wbase=0, n_keys=s)
    o_dsa = dsa_attention(p['qb'], p['qi'], p['wi'], p['kb'], p['vb'], p['ki'],
                          tq=min(TQ, s), qpos0=0, n_keep=min(DSA_TOPK, s // 4))
    nsa_rows = p['nsa_rows'].reshape(b, s, NSA_SECTIONS, KV_GROUPS_A, HEAD_DIM_A)
    win_state = p['win_rows'].reshape(b, s, 2, KV_GROUPS_A, HEAD_DIM_A)[:, -min(WINDOW, s):]
    return (o_nsa, o_dsa, gm), nsa_rows, win_state, p['dsa_rows']


def mixer_sample(x, g, cache_nsa, win_buf, cache_dsa, page_table, w_in, cmp_pe, cmp_w1, cmp_b1, cmp_w2, sizes):
    b, t, _ = x.shape
    page = cache_nsa.shape[1]
    past_len = page_table.shape[1] * page
    n_keys = past_len + t
    proj, gm = project(x, g, w_in, sizes)
    p = prep(proj, jnp.tile(past_len + jnp.arange(t, dtype=jnp.int32), b), cmp_pe, False)
    pad = lambda a: pad_rows(a, TQ_STEP)

    cache_nsa_t = jnp.transpose(cache_nsa, (0, 2, 3, 4, 1)).reshape(cache_nsa.shape[0], NSA_SECTIONS * KV_A, page)
    zt, zb, ks, vs = nsa_gather(cache_nsa_t, page_table, p['nsa_rows'], cmp_pe)
    merge_bg = lambda a: a.reshape((2, b * KV_GROUPS_A) + a.shape[3:])
    kvc = compress(merge_bg(zt), merge_bg(zb), cmp_pe, cmp_w1, cmp_b1, cmp_w2, n_keys)
    kvc = kvc.reshape(2, b, KV_GROUPS_A, kvc.shape[2], HEAD_DIM_A)
    w_len = win_buf.shape[1]
    win_new = p['win_rows'].reshape(b, t, 2, KV_GROUPS_A, HEAD_DIM_A)
    win_all = jnp.concatenate([win_buf, win_new], axis=1)
    win_pad = pad_rows(win_all, w_len + WIN_CHUNK).astype(MXU_DTYPE)
    kw = win_pad[:, :, 0].reshape(b, w_len + WIN_CHUNK, KV_A)
    vw = win_pad[:, :, 1].reshape(b, w_len + WIN_CHUNK, KV_A)
    o_nsa = nsa_attention(pad(p['qc']), pad(p['qr']), pad(p['gates']), kvc, ks, vs, kw, vw,
                          tq=TQ_STEP, qpos0=past_len, wbase=past_len - w_len, n_keys=n_keys, kv_t=True)

    kb_, vb, ki = dsa_gather(jnp.swapaxes(cache_dsa, 1, 2), page_table, p['dsa_rows'])
    o_dsa = dsa_attention(pad(p['qb']), pad(p['qi']), pad(p['wi']), kb_, vb, ki, tq=TQ_STEP, qpos0=past_len,
                          n_keep=min(DSA_TOPK, n_keys // 4), kv_t=True, tq_real=t)
    nsa_rows = p['nsa_rows'].reshape(b, t, NSA_SECTIONS, KV_GROUPS_A, HEAD_DIM_A)
    return (o_nsa, o_dsa, pad(gm)), nsa_rows, win_all[:, -w_len:], p['dsa_rows']


def kernel(x_prompt, x_sample, mem_prompt, cache_nsa_kv, state_nsa_win, cache_dsa_kv, cache_mem_kv, state_conv,
           page_table, norm_g, w_in, cmp_pe, cmp_w1, cmp_b1, cmp_w2, w_out_a, w_out_b, w_out, w_mem_q, w_mem_kv,
           w_mem_out, w_up, conv_w, conv_b, w_down, final_g):
    depth = w_in.shape[0]
    d_model = x_prompt.shape[-1]
    d_ff = w_down.shape[1]
    assert CONV_WIDTH == 3
    sizes = (N_HEADS_A * HEAD_DIM_A, 6 * KV_A, 3 * N_HEADS_A, N_HEADS_B * HEAD_DIM_B, 2 * HEAD_DIM_B,
             IDX_HEADS * IDX_DIM, IDX_DIM, IDX_HEADS, 2 * d_model)
    xp, xs = x_prompt, pad_rows(x_sample, TQ_STEP)
    t_step = x_sample.shape[1]
    nsa_p, nsa_s, win_p, win_s, dsa_p, dsa_s, mem_p, conv_p, conv_s = [], [], [], [], [], [], [], [], []
    for l in range(depth):
        assert l == depth - 1, "the fused FFN epilogue applies the final norm"
        branches_p, a, bwin, c = mixer_prompt(xp, norm_g[l, 0], w_in[l], cmp_pe[l], cmp_w1[l], cmp_b1[l], cmp_w2[l],
                                              sizes)
        nsa_p.append(a); win_p.append(bwin); dsa_p.append(c)
        branches_s, a, bwin, c = mixer_sample(xs[:, :t_step], norm_g[l, 0], cache_nsa_kv[l], state_nsa_win[l],
                                              cache_dsa_kv[l], page_table, w_in[l], cmp_pe[l], cmp_w1[l], cmp_b1[l],
                                              cmp_w2[l], sizes)
        nsa_s.append(a); win_s.append(bwin); dsa_s.append(c)
        kv_p = norm_matmul(mem_prompt, norm_g[l, 2], w_mem_kv[l])
        mem_p.append(kv_p.reshape(kv_p.shape[:2] + (2, MEM_HEADS, MEM_HEAD_DIM)))
        kv_s = cache_mem_kv[l].reshape(cache_mem_kv.shape[1:3] + (-1,))
        weights = (norm_g[l, 1], norm_g[l, 3], final_g, w_out_a[l], w_out_b[l], w_out[l], w_mem_q[l], w_mem_out[l],
                   w_up[l], conv_w[l], conv_b[l], w_down[l])
        xp, cp = dense_tail(xp, *branches_p, kv_p, jnp.zeros((xp.shape[0], CONV_WIDTH - 1, 2 * d_ff), xp.dtype),
                            xp.shape[1], *weights)
        xs, cs = dense_tail(xs, *branches_s, kv_s, state_conv[l], t_step, *weights)
        conv_p.append(cp); conv_s.append(cs)
    y_prompt, y_sample = xp, xs[:, :t_step]
    return (y_prompt, y_sample, jnp.stack(nsa_p), jnp.stack(nsa_s), jnp.stack(win_p), jnp.stack(win_s),
            jnp.stack(dsa_p), jnp.stack(dsa_s), jnp.stack(mem_p), jnp.stack(conv_p), jnp.stack(conv_s))
```

```python
import functools

import jax
import jax.numpy as jnp
from jax import lax
from jax.experimental import pallas as pl
from jax.experimental.pallas import tpu as pltpu

N_HEADS_A = 16
HEAD_DIM_A = 64
KV_GROUPS_A = 2
CMP_BLOCK = 32
CMP_STRIDE = 16
SLC_BLOCK = 64
N_SELECT = 16
WINDOW = 512
N_HEADS_B = 8
HEAD_DIM_B = 128
IDX_HEADS = 16
IDX_DIM = 64
DSA_TOPK = 256
MEM_HEADS = 4
MEM_HEAD_DIM = 128
CONV_WIDTH = 3
ROPE_THETA = 500000.0
ROT_FRACTION = 4
EPS = 1e-6
KV_A = KV_GROUPS_A * HEAD_DIM_A
CMP_PER_SLC = SLC_BLOCK // CMP_STRIDE
HPG_A = N_HEADS_A // KV_GROUPS_A
DSA_ROW = 2 * HEAD_DIM_B + IDX_DIM
NSA_SECTIONS = 4

LANE = 128
SUBLANE = 8
BF16_ROWS = 16
VMEM_LIMIT = 48 * 1024 * 1024

NEG_INF = float('-inf')
POS_INF = float('inf')
MXU_DTYPE = jnp.bfloat16
TQ = 128
TQ_STEP = BF16_ROWS
KEY_CHUNK = 512
WIN_CHUNK = 128
PAGES_PER_STEP = 8
GATHER_SLOTS = 3
FFN_ROWS = 1024
BISECT_ITERS = 40
TIE_STRIP_ITERS = 64
SLC_SHIFT = SLC_BLOCK.bit_length() - 1
CMP_PER_SLC_SHIFT = CMP_PER_SLC.bit_length() - 1


def _round_up(n, m):
    return -(-n // m) * m


def _tile(n, cap):
    if n <= cap:
        return n
    best = None
    for t in range(LANE, cap + 1, LANE):
        if n % t == 0:
            best = t
    assert best is not None, (n, cap)
    return best


def _rms(x, g):
    return x * lax.rsqrt(jnp.mean(x * x, axis=-1, keepdims=True) + EPS) * g


def _norm_mm_kernel(x_ref, g_ref, w_ref, o_ref, xn_ref):
    @pl.when(pl.program_id(1) == 0)
    def _():
        xn_ref[...] = _rms(x_ref[...], g_ref[...]).astype(xn_ref.dtype)

    o_ref[...] = jnp.dot(xn_ref[...], w_ref[...], preferred_element_type=jnp.float32).astype(o_ref.dtype)


def norm_matmul(x, g, w):
    lead = x.shape[:-1]
    k = x.shape[-1]
    n = w.shape[-1]
    x2 = x.reshape(-1, k)
    m = x2.shape[0]
    tm = 512 if m % 512 == 0 else m
    tn = _tile(n, 1536)
    out = pl.pallas_call(
        _norm_mm_kernel,
        grid=(m // tm, n // tn),
        in_specs=[pl.BlockSpec((tm, k), lambda i, j: (i, 0)),
                  pl.BlockSpec((1, k), lambda i, j: (0, 0)),
                  pl.BlockSpec((k, tn), lambda i, j: (0, j))],
        out_specs=pl.BlockSpec((tm, tn), lambda i, j: (i, j)),
        out_shape=jax.ShapeDtypeStruct((m, n), jnp.float32),
        scratch_shapes=[pltpu.VMEM((tm, k), MXU_DTYPE)],
        compiler_params=pltpu.CompilerParams(
            dimension_semantics=("parallel", "arbitrary"), vmem_limit_bytes=VMEM_LIMIT),
        name="norm_matmul",
    )(x2, g.reshape(1, k).astype(jnp.float32), w.astype(MXU_DTYPE))
    return out.reshape(lead + (n,))


def _gated_merge_kernel(oa_ref, ob_ref, wa_ref, wb_ref, ga_ref, gb_ref, o_ref):
    dot = functools.partial(jnp.dot, preferred_element_type=jnp.float32)
    ya = dot(oa_ref[...], wa_ref[...])
    yb = dot(ob_ref[...], wb_ref[...])
    o_ref[...] = (jax.nn.sigmoid(ga_ref[...]) * ya + jax.nn.sigmoid(gb_ref[...]) * yb).astype(o_ref.dtype)


def gated_merge(o_nsa, o_dsa, gm, w_oa, w_ob):
    m, ka = o_nsa.shape
    kb = o_dsa.shape[1]
    n = w_oa.shape[1]
    tm = 512 if m % 512 == 0 else m
    tn = _tile(n, 1024)
    nb = n // tn
    return pl.pallas_call(
        _gated_merge_kernel,
        grid=(m // tm, nb),
        in_specs=[pl.BlockSpec((tm, ka), lambda i, j: (i, 0)),
                  pl.BlockSpec((tm, kb), lambda i, j: (i, 0)),
                  pl.BlockSpec((ka, tn), lambda i, j: (0, j)),
                  pl.BlockSpec((kb, tn), lambda i, j: (0, j)),
                  pl.BlockSpec((tm, tn), lambda i, j: (i, j)),
                  pl.BlockSpec((tm, tn), lambda i, j: (i, j + nb))],
        out_specs=pl.BlockSpec((tm, tn), lambda i, j: (i, j)),
        out_shape=jax.ShapeDtypeStruct((m, n), MXU_DTYPE),
        compiler_params=pltpu.CompilerParams(
            dimension_semantics=("parallel", "parallel"), vmem_limit_bytes=VMEM_LIMIT),
        name="gated_merge",
    )(o_nsa, o_dsa, w_oa.astype(MXU_DTYPE), w_ob.astype(MXU_DTYPE), gm, gm)


def _out_proj_kernel(z_ref, w_ref, x_ref, o_ref):
    o_ref[...] = x_ref[...] + jnp.dot(z_ref[...], w_ref[...], preferred_element_type=jnp.float32)


def out_proj_residual(z, w, x):
    m, k = z.shape
    n = w.shape[1]
    tm = 256 if m % 256 == 0 else m
    return pl.pallas_call(
        _out_proj_kernel,
        grid=(m // tm,),
        in_specs=[pl.BlockSpec((tm, k), lambda i: (i, 0)),
                  pl.BlockSpec((k, n), lambda i: (0, 0)),
                  pl.BlockSpec((tm, n), lambda i: (i, 0))],
        out_specs=pl.BlockSpec((tm, n), lambda i: (i, 0)),
        out_shape=jax.ShapeDtypeStruct((m, n), jnp.float32),
        compiler_params=pltpu.CompilerParams(dimension_semantics=("parallel",), vmem_limit_bytes=VMEM_LIMIT),
        name="out_proj_residual",
    )(z, w.astype(MXU_DTYPE), x)


def _mem_block_kernel(x_ref, g1_ref, g2_ref, wq_ref, kv_ref, wo_ref, x_out_ref, xn_out_ref):
    d = MEM_HEAD_DIM
    hd = MEM_HEADS * d
    x = x_ref[0]
    xn = _rms(x, g1_ref[...]).astype(wq_ref.dtype)
    q = jnp.dot(xn, wq_ref[...], preferred_element_type=jnp.float32).astype(wq_ref.dtype)
    outs = []
    for h in range(MEM_HEADS):
        k = kv_ref[0, :, h * d:(h + 1) * d].astype(wq_ref.dtype)
        v = kv_ref[0, :, hd + h * d:hd + (h + 1) * d].astype(wq_ref.dtype)
        s = _dot_nt(q[:, h * d:(h + 1) * d], k) * d ** -0.5
        e = jnp.exp(s - jnp.max(s, axis=1, keepdims=True))
        p = e / jnp.sum(e, axis=1, keepdims=True)
        outs.append(jnp.dot(p.astype(v.dtype), v, preferred_element_type=jnp.float32))
    o = jnp.concatenate(outs, axis=1).astype(wo_ref.dtype)
    x2 = x + jnp.dot(o, wo_ref[...], preferred_element_type=jnp.float32)
    x_out_ref[0] = x2
    xn_out_ref[0] = _rms(x2, g2_ref[...]).astype(xn_out_ref.dtype)


def mem_block(x, g1, g2, w_q, kv, w_o):
    b, t, dm = x.shape
    mt, kvw = kv.shape[1:]
    hd = w_q.shape[1]
    tm = 256 if t % 256 == 0 else t
    row_spec = pl.BlockSpec((1, tm, dm), lambda bi, i: (bi, i, 0))
    g_spec = pl.BlockSpec((1, dm), lambda bi, i: (0, 0))
    return pl.pallas_call(
        _mem_block_kernel,
        grid=(b, t // tm),
        in_specs=[row_spec, g_spec, g_spec,
                  pl.BlockSpec((dm, hd), lambda bi, i: (0, 0)),
                  pl.BlockSpec((1, mt, kvw), lambda bi, i: (bi, 0, 0)),
                  pl.BlockSpec((hd, dm), lambda bi, i: (0, 0))],
        out_specs=[row_spec, row_spec],
        out_shape=[jax.ShapeDtypeStruct((b, t, dm), jnp.float32), jax.ShapeDtypeStruct((b, t, dm), MXU_DTYPE)],
        compiler_params=pltpu.CompilerParams(
            dimension_semantics=("parallel", "parallel"), vmem_limit_bytes=VMEM_LIMIT),
        name="mem_block",
    )(x, g1.reshape(1, dm).astype(jnp.float32), g2.reshape(1, dm).astype(jnp.float32), w_q.astype(MXU_DTYPE),
      kv, w_o.astype(MXU_DTYPE))


def _ffn_up_kernel(x_ref, halo_ref, wg_ref, wu_ref, cwg_ref, cwu_ref, cbg_ref, cbu_ref, pg_ref, pu_ref,
                   h_ref, sg_ref, su_ref, *, nb, tm, t_real):
    i = pl.program_id(2)
    n_halo = halo_ref.shape[1]
    tn = wg_ref.shape[1]
    rows = nb * tm
    x = x_ref[...].reshape(rows, x_ref.shape[2])
    if nb == 1:
        x = jnp.concatenate([halo_ref[0], x], axis=0)
    row = _iota((nb, tm, 1), 1).reshape(rows, 1)
    last = (t_real - 1) // tm
    r_last = (t_real - 1) % tm
    spread = lambda p: jnp.broadcast_to(p, (nb, tm, tn)).reshape(rows, tn)

    def branch(w_ref, cw_ref, cb_ref, p_ref, s_ref):
        u = jnp.dot(x, w_ref[...], preferred_element_type=jnp.float32)
        p0, p1 = p_ref[:, 0:1, :], p_ref[:, 1:2, :]
        if nb == 1:
            uh, u = u[:n_halo], u[n_halo:]
            p0 = jnp.where(i == 0, p0, uh[n_halo - 2:n_halo - 1][None])
            p1 = jnp.where(i == 0, p1, uh[n_halo - 1:n_halo][None])
        p0, p1 = spread(p0), spread(p1)
        u1 = jnp.where(row == 0, p1, pltpu.roll(u, 1, 0))
        u2 = jnp.where(row == 0, p0, jnp.where(row == 1, p1, pltpu.roll(u, 2, 0)))

        @pl.when(i == last)
        def _():
            s_ref[...] = u.reshape(nb, tm, tn)[:, r_last - 1:r_last + 1, :]

        return cb_ref[...] + u2 * cw_ref[0:1, :] + u1 * cw_ref[1:2, :] + u * cw_ref[2:3, :]

    gate = branch(wg_ref, cwg_ref, cbg_ref, pg_ref, sg_ref)
    up = branch(wu_ref, cwu_ref, cbu_ref, pu_ref, su_ref)
    h_ref[...] = (jax.nn.silu(gate) * up).reshape(nb, tm, tn).astype(h_ref.dtype)


def ffn_up(xn, prev, w_up, conv_w, conv_b, t_real):
    b, t, dm = xn.shape
    f2 = w_up.shape[1]
    f = f2 // 2
    tm = FFN_ROWS if t % FFN_ROWS == 0 else t
    nb = 1 if t > tm else max(1, min(b, 512 // tm))
    while b % nb:
        nb -= 1
    tn = _tile(f, 512)
    nf = f // tn
    halo = min(BF16_ROWS, tm)
    hpt = tm // halo
    assert t_real >= 2 and (t_real - 1) % tm >= 1 and tm % SUBLANE == 0
    w_up = w_up.astype(MXU_DTYPE)
    conv_b = conv_b.reshape(1, f2)
    col = lambda off: (lambda j, bi, i: (0, j + off))
    st = lambda off: (lambda j, bi, i: (bi, 0, j + off))
    specs = [pl.BlockSpec((nb, tm, dm), lambda j, bi, i: (bi, i, 0)),
             pl.BlockSpec((1, halo, dm), lambda j, bi, i: (bi * nb, jnp.maximum(i * hpt - 1, 0), 0)),
             pl.BlockSpec((dm, tn), col(0)), pl.BlockSpec((dm, tn), col(nf)),
             pl.BlockSpec((CONV_WIDTH, tn), col(0)), pl.BlockSpec((CONV_WIDTH, tn), col(nf)),
             pl.BlockSpec((1, tn), col(0)), pl.BlockSpec((1, tn), col(nf)),
             pl.BlockSpec((nb, 2, tn), st(0)), pl.BlockSpec((nb, 2, tn), st(nf))]
    h, sg, su = pl.pallas_call(
        functools.partial(_ffn_up_kernel, nb=nb, tm=tm, t_real=t_real),
        grid=(nf, b // nb, t // tm),
        in_specs=specs,
        out_specs=[pl.BlockSpec((nb, tm, tn), lambda j, bi, i: (bi, i, j)),
                   pl.BlockSpec((nb, 2, tn), st(0)), pl.BlockSpec((nb, 2, tn), st(0))],
        out_shape=[jax.ShapeDtypeStruct((b, t, f), MXU_DTYPE), jax.ShapeDtypeStruct((b, 2, f), jnp.float32),
                   jax.ShapeDtypeStruct((b, 2, f), jnp.float32)],
        compiler_params=pltpu.CompilerParams(
            dimension_semantics=("parallel", "parallel", "arbitrary"), vmem_limit_bytes=VMEM_LIMIT),
        name="ffn_up",
    )(xn, xn, w_up, w_up, conv_w, conv_w, conv_b, conv_b, prev, prev)
    return h, jnp.concatenate([sg, su], axis=-1)


def _ffn_down_kernel(h_ref, w_ref, x_ref, g_ref, o_ref, acc_ref):
    @pl.when(pl.program_id(1) == 0)
    def _():
        acc_ref[...] = x_ref[...]

    acc_ref[...] += jnp.dot(h_ref[...], w_ref[...], preferred_element_type=jnp.float32)

    @pl.when(pl.program_id(1) == pl.num_programs(1) - 1)
    def _():
        o_ref[...] = _rms(acc_ref[...], g_ref[...])


def ffn_down_norm(h, w_down, x, g):
    m, f = h.shape
    dm = w_down.shape[1]
    tm = 512 if m % 512 == 0 else m
    tk = _tile(f, 2048)
    return pl.pallas_call(
        _ffn_down_kernel,
        grid=(m // tm, f // tk),
        in_specs=[pl.BlockSpec((tm, tk), lambda i, l: (i, l)),
                  pl.BlockSpec((tk, dm), lambda i, l: (l, 0)),
                  pl.BlockSpec((tm, dm), lambda i, l: (i, 0)),
                  pl.BlockSpec((1, dm), lambda i, l: (0, 0))],
        out_specs=pl.BlockSpec((tm, dm), lambda i, l: (i, 0)),
        out_shape=jax.ShapeDtypeStruct((m, dm), jnp.float32),
        scratch_shapes=[pltpu.VMEM((tm, dm), jnp.float32)],
        compiler_params=pltpu.CompilerParams(
            dimension_semantics=("parallel", "arbitrary"), vmem_limit_bytes=VMEM_LIMIT),
        name="ffn_down_norm",
    )(h, w_down.astype(MXU_DTYPE), x, g.reshape(1, dm).astype(jnp.float32))


def dense_tail(x, o_nsa, o_dsa, gm, kv_mem, prev_u, t_real, g_mem, g_ffn, g_final, w_oa, w_ob, w_o, w_mq, w_mo,
               w_up, conv_w, conv_b, w_down):
    b, t, dm = x.shape
    rows = lambda a: a.reshape(b * t, a.shape[-1])
    z = gated_merge(rows(o_nsa), rows(o_dsa), rows(gm), w_oa, w_ob)
    x1 = out_proj_residual(z, w_o, rows(x)).reshape(b, t, dm)
    x2, xn2 = mem_block(x1, g_mem, g_ffn, w_mq, kv_mem, w_mo)
    h, state = ffn_up(xn2, prev_u, w_up, conv_w, conv_b, t_real)
    y = ffn_down_norm(rows(h), w_down, rows(x2), g_final)
    return y.reshape(b, t, dm), state


def _dot_nt(a, b):
    return lax.dot_general(a, b, (((1,), (1,)), ((), ())), preferred_element_type=jnp.float32)


def _iota(shape, dim):
    return lax.broadcasted_iota(jnp.int32, shape, dim)


def _flash_init(rows, d):
    return (jnp.full((rows, 1), NEG_INF, jnp.float32), jnp.zeros((rows, 1), jnp.float32),
            jnp.zeros((rows, d), jnp.float32))


def _flash_step(carry, q, k, v, madd, nh, scale=None, kv_t=False):
    m, l, acc = carry
    s = jnp.dot(q, k, preferred_element_type=jnp.float32) if kv_t else _dot_nt(q, k)
    if scale is not None:
        s = s * scale
    r, kb = s.shape
    s = (s.reshape(nh, r // nh, kb) + madd[None]).reshape(r, kb)
    m_new = jnp.maximum(m, jnp.max(s, axis=1, keepdims=True))
    m_safe = jnp.where(m_new == NEG_INF, 0.0, m_new)
    p = jnp.exp(s - m_safe)
    alpha = jnp.exp(m - m_safe)
    l = alpha * l + jnp.sum(p, axis=1, keepdims=True)
    pv = _dot_nt(p.astype(v.dtype), v) if kv_t else jnp.dot(p.astype(v.dtype), v, preferred_element_type=jnp.float32)
    return m_new, l, alpha * acc + pv


def _flash_finish(carry):
    _, l, acc = carry
    return acc / jnp.maximum(l, 1e-30)


def _split_dot(x, m01):
    hi = x.astype(jnp.bfloat16)
    r1 = x - hi.astype(jnp.float32)
    mid = r1.astype(jnp.bfloat16)
    lo = (r1 - mid.astype(jnp.float32)).astype(jnp.bfloat16)
    dot = functools.partial(jnp.dot, preferred_element_type=jnp.float32)
    return dot(hi, m01) + dot(mid, m01) + dot(lo, m01)


def _nsa_group(g, i, qc_ref, qr_ref, gate_ref, kc_ref, vc_ref, ks_ref, vs_ref, kw_ref, vw_ref, mask_ref,
               *, tq, kb, qpos0, wbase, ns, n_sel, kv_t):
    nh, d = HPG_A, HEAD_DIM_A
    rows = nh * tq
    nc = kc_ref.shape[3]
    nsp = _round_up(ns, LANE)
    t0 = qpos0 + i * tq
    nj = (t0 + tq - 1) // kb + 1
    cols = slice(g * d, (g + 1) * d)

    def stack_heads(q_ref):
        return jnp.concatenate([q_ref[0, :, (g * nh + h) * d:(g * nh + h + 1) * d] for h in range(nh)], axis=0)

    qc = stack_heads(qc_ref)
    qr = stack_heads(qr_ref)
    tpos = t0 + _iota((tq, 1), 0)

    blk_last = _iota((1, nc), 1) * CMP_STRIDE + (CMP_BLOCK - 1)
    madd_c = jnp.where(blk_last <= tpos, 0.0, NEG_INF)
    s = _dot_nt(qc, kc_ref[0, 0, g]).reshape(nh, tq, nc) + madd_c[None]
    m = jnp.max(s, axis=2, keepdims=True)
    m = jnp.where(m == NEG_INF, 0.0, m)
    e = jnp.exp(s - m)
    p = e / jnp.maximum(jnp.sum(e, axis=2, keepdims=True), 1e-30)
    o_cmp = jnp.dot(p.reshape(rows, nc).astype(vc_ref.dtype), vc_ref[0, 0, g], preferred_element_type=jnp.float32)

    imp = jnp.sum(p, axis=0)
    c_id = _iota((nc, nsp), 0)
    m_id = _iota((nc, nsp), 1)
    overlap = (jnp.right_shift(c_id, CMP_PER_SLC_SHIFT) == m_id) | (c_id == m_id * CMP_PER_SLC - 1)
    score = _split_dot(imp, overlap.astype(jnp.bfloat16))
    blk = _iota((1, nsp), 1)
    cur = jnp.right_shift(tpos, SLC_SHIFT)
    forced = (blk == 0) | (blk == cur) | (blk == cur - 1)
    sc = jnp.where(forced, POS_INF, jnp.where(blk * SLC_BLOCK <= tpos, score, NEG_INF))
    if tq % LANE == 0 and nsp % LANE == 0:
        sc_t = sc.T
        blk_t = _iota((nsp, 1), 0)
        rank_t = jnp.zeros((nsp, tq), jnp.float32)
        for mp in range(ns):
            ref = sc_t[mp:mp + 1, :]
            beats = (ref > sc_t) | ((ref == sc_t) & (blk_t > mp))
            rank_t = rank_t + jnp.where(beats, 1.0, 0.0)
        sel = jnp.where(rank_t < n_sel, 1.0, 0.0).T.astype(jnp.bfloat16)
    else:
        rank = jnp.zeros((tq, nsp), jnp.float32)
        for mp in range(ns):
            col = sc[:, mp:mp + 1]
            beats = (col > sc) | ((col == sc) & (blk > mp))
            rank = rank + jnp.where(beats, 1.0, 0.0)
        sel = jnp.where(rank < n_sel, 1.0, 0.0).astype(jnp.bfloat16)

    def make_mask(j, _):
        kpos = j * kb + _iota((1, kb), 1)
        expand = (jnp.right_shift(j * kb + _iota((nsp, kb), 1), SLC_SHIFT) == _iota((nsp, kb), 0))
        hit = jnp.dot(sel, expand.astype(jnp.bfloat16), preferred_element_type=jnp.float32)
        mask_ref[j] = jnp.where((hit > 0.5) & (kpos <= tpos), 0.0, NEG_INF)
        return 0

    def slc_body(j, carry):
        off = pl.multiple_of(j * kb, kb)
        return _flash_step(carry, qr, ks_ref[0, pl.ds(off, kb), cols], vs_ref[0, pl.ds(off, kb), cols],
                           mask_ref[j], nh)

    if kv_t:
        sel_f = sel.astype(jnp.float32)
        low_half = _iota((tq, LANE), 1) < SLC_BLOCK
        pieces = []
        for v in range(kb // LANE):
            hit = jnp.where(low_half, sel_f[:, 2 * v:2 * v + 1], sel_f[:, 2 * v + 1:2 * v + 2])
            kpos = v * LANE + _iota((1, LANE), 1)
            pieces.append(jnp.where((hit > 0.5) & (kpos <= tpos), 0.0, NEG_INF))
        wide = lambda ref: jnp.concatenate([ref[0, c, cols, :] for c in range(ref.shape[1])], axis=1)
        o_slc = _flash_finish(_flash_step(_flash_init(rows, d), qr, wide(ks_ref), wide(vs_ref),
                                          jnp.concatenate(pieces, axis=1), nh, kv_t=True))
    else:
        lax.fori_loop(0, nj, make_mask, 0)
        o_slc = _flash_finish(lax.fori_loop(0, nj, slc_body, _flash_init(rows, d)))

    wk = min(_round_up(WINDOW + tq, WIN_CHUNK), kw_ref.shape[1])
    first = jnp.maximum(t0 - (WINDOW - 1) - wbase, 0) // WIN_CHUNK * WIN_CHUNK
    off = pl.multiple_of(jnp.minimum(first, kw_ref.shape[1] - wk), WIN_CHUNK)
    dist = tpos - (wbase + off + _iota((1, wk), 1))
    madd_w = jnp.where((dist >= 0) & (dist < WINDOW), 0.0, NEG_INF)
    o_win = _flash_finish(_flash_step(_flash_init(rows, d), qr, kw_ref[0, pl.ds(off, wk), cols],
                                      vw_ref[0, pl.ds(off, wk), cols], madd_w, nh))

    gates = gate_ref[0]
    outs = []
    for h in range(nh):
        rs = slice(h * tq, (h + 1) * tq)
        c = g * nh + h
        outs.append(gates[:, c:c + 1] * o_cmp[rs] + gates[:, N_HEADS_A + c:N_HEADS_A + c + 1] * o_slc[rs]
                    + gates[:, 2 * N_HEADS_A + c:2 * N_HEADS_A + c + 1] * o_win[rs])
    return outs


def _nsa_kernel(qc_ref, qr_ref, gate_ref, kc_ref, vc_ref, ks_ref, vs_ref, kw_ref, vw_ref, o_ref, mask_ref, **kw):
    i = pl.program_id(1)
    outs = []
    for g in range(KV_GROUPS_A):
        outs += _nsa_group(g, i, qc_ref, qr_ref, gate_ref, kc_ref, vc_ref, ks_ref, vs_ref, kw_ref, vw_ref,
                           mask_ref, **kw)
    o_ref[0] = jnp.concatenate(outs, axis=1).astype(o_ref.dtype)


def nsa_attention(qc, qr, gates, kvc, ks, vs, kw, vw, *, tq, qpos0, wbase, n_keys, kv_t=False):
    b, t, hd = qc.shape
    l = ks.shape[1] * ks.shape[3] if kv_t else ks.shape[1]
    lw = kw.shape[1]
    nc, d = kvc.shape[3:]
    kb = min(KEY_CHUNK, l) if tq >= TQ else l
    assert l % kb == 0 and t % tq == 0 and lw % WIN_CHUNK == 0 and (kb == l or not kv_t)
    assert (qpos0 + t - 1) // kb + 1 <= l // kb and (qpos0 + t - 1 - wbase) // WIN_CHUNK + 1 <= lw // WIN_CHUNK
    assert (qpos0 - wbase) % WIN_CHUNK == 0 and WIN_CHUNK % tq == 0 and tq > 1
    assert 2 * SLC_BLOCK == LANE or not kv_t
    ns = l // SLC_BLOCK
    n_sel = min(N_SELECT, -(-n_keys // SLC_BLOCK))
    q_spec = pl.BlockSpec((1, tq, hd), lambda bi, i: (bi, i, 0))
    kc_spec = pl.BlockSpec((1, 1, KV_GROUPS_A, nc, d), lambda bi, i: (0, bi, 0, 0, 0))
    vc_spec = pl.BlockSpec((1, 1, KV_GROUPS_A, nc, d), lambda bi, i: (1, bi, 0, 0, 0))
    k_spec = (pl.BlockSpec((1,) + ks.shape[1:], lambda bi, i: (bi, 0, 0, 0)) if kv_t
              else pl.BlockSpec((1, l, KV_A), lambda bi, i: (bi, 0, 0)))
    w_spec = pl.BlockSpec((1, lw, KV_A), lambda bi, i: (bi, 0, 0))
    return pl.pallas_call(
        functools.partial(_nsa_kernel, tq=tq, kb=kb, qpos0=qpos0, wbase=wbase, ns=ns, n_sel=n_sel, kv_t=kv_t),
        grid=(b, t // tq),
        in_specs=[q_spec, q_spec, pl.BlockSpec((1, tq, 3 * N_HEADS_A), lambda bi, i: (bi, i, 0)),
                  kc_spec, vc_spec, k_spec, k_spec, w_spec, w_spec],
        out_specs=q_spec,
        out_shape=jax.ShapeDtypeStruct((b, t, hd), MXU_DTYPE),
        scratch_shapes=[pltpu.VMEM((l // kb, tq, kb), jnp.float32)],
        compiler_params=pltpu.CompilerParams(
            dimension_semantics=("parallel", "arbitrary"), vmem_limit_bytes=VMEM_LIMIT),
        name="nsa_attention",
    )(qc, qr, gates, kvc, kvc, ks, vs, kw, vw)


def _dsa_kernel(qb_ref, qi_ref, wi_ref, kb_ref, vb_ref, ki_ref, o_ref, score_ref,
                *, tq, tq_real, kb, qpos0, n_keep, kv_t):
    i = pl.program_id(1)
    nh, d = N_HEADS_B, HEAD_DIM_B
    t0 = qpos0 + i * tq
    nj = (t0 + tq - 1) // kb + 1
    tpos = t0 + _iota((tq, 1), 0)
    w = wi_ref[0]
    wide = lambda ref: jnp.concatenate([ref[0, c] for c in range(ref.shape[1])], axis=1)

    def idx_body(j, carry):
        lo, hi = carry
        off = pl.multiple_of(j * kb, kb)
        kidx = wide(ki_ref) if kv_t else ki_ref[0, pl.ds(off, kb), :]
        acc = jnp.zeros((tq, kb), jnp.float32)
        heads = [qi_ref[0, :, h * IDX_DIM:(h + 1) * IDX_DIM] for h in range(IDX_HEADS)]
        if kv_t:
            dots_all = jnp.dot(jnp.concatenate(heads, axis=0), kidx, preferred_element_type=jnp.float32)
        for h in range(IDX_HEADS):
            dots = dots_all[h * tq:(h + 1) * tq] if kv_t else _dot_nt(heads[h], kidx)
            acc = acc + w[:, h:h + 1] * jnp.maximum(dots, 0.0)
        vis = (off + _iota((1, kb), 1)) <= tpos
        score_ref[j] = jnp.where(vis, acc, NEG_INF)
        lo = jnp.minimum(lo, jnp.min(jnp.where(vis, acc, POS_INF), axis=1, keepdims=True))
        hi = jnp.maximum(hi, jnp.max(jnp.where(vis, acc, NEG_INF), axis=1, keepdims=True))
        return lo, hi

    lo, hi = lax.fori_loop(0, nj, idx_body, (jnp.full((tq, 1), POS_INF, jnp.float32),
                                             jnp.full((tq, 1), NEG_INF, jnp.float32)))

    k = float(n_keep)
    n_vis = (tpos + 1).astype(jnp.float32)
    n_keys = score_ref.shape[0] * kb

    def reduce_scores(pred, pick, init, combine, lane_reduce):
        def body(j, acc):
            sc = score_ref[j]
            for c in range(kb // LANE):
                kpos = (j * kb + c * LANE + _iota((1, LANE), 1)).astype(jnp.float32)
                piece = sc[:, c * LANE:(c + 1) * LANE]
                acc = combine(acc, pick(pred(piece, kpos), piece))
            return acc

        return lane_reduce(lax.fori_loop(0, nj, body, jnp.full((tq, LANE), init, jnp.float32)),
                           axis=1, keepdims=True)

    def count_where(pred):
        return reduce_scores(pred, lambda m, _: jnp.where(m, 1.0, 0.0), 0.0, jnp.add, jnp.sum)

    def min_where(pred):
        return reduce_scores(pred, lambda m, x: jnp.where(m, x, POS_INF), POS_INF, jnp.minimum, jnp.min)

    real = _iota((tq, 1), 0) < tq_real

    def any_row(flag):
        return jnp.max(jnp.where(real & flag, 1.0, 0.0)) > 0.0

    def bisect(state):
        it, lo, hi, cnt_lo = state
        mid = 0.5 * (lo + hi)
        mid_b = jnp.broadcast_to(mid, (tq, LANE))
        cnt = count_where(lambda x, _: x >= mid_b)
        ge = cnt >= k
        return it + 1, jnp.where(ge, mid, lo), jnp.where(ge, hi, mid), jnp.where(ge, cnt, cnt_lo)

    _, thr, _, cnt_lo = lax.while_loop(
        lambda st: (st[0] < BISECT_ITERS) & any_row((st[3] != k) & (n_vis > k)), lambda st: bisect(bisect(st)),
        (jnp.int32(0), lo, hi, n_vis))

    def break_ties():
        def above(v):
            v_b = jnp.broadcast_to(v, (tq, LANE))
            return count_where(lambda x, _: x > v_b)

        def strip(state):
            it, v, c_gt = state
            v_b = jnp.broadcast_to(v, (tq, LANE))
            v_next = jnp.where(c_gt >= k, min_where(lambda x, _: x > v_b), v)
            return it + 1, v_next, above(v_next)

        thr_b = jnp.broadcast_to(thr, (tq, LANE))
        v0 = min_where(lambda x, _: x >= thr_b)
        _, v, c_gt = lax.while_loop(lambda st: (st[0] < TIE_STRIP_ITERS) & any_row(st[2] >= k), strip,
                                    (jnp.int32(0), v0, above(v0)))
        need = k - c_gt
        v_b = jnp.broadcast_to(v, (tq, LANE))

        def narrow(_, bounds):
            j_lo, j_hi = bounds
            mid = jnp.floor(0.5 * (j_lo + j_hi))
            ge = count_where(lambda x, kpos: (x == v_b) & (kpos <= mid)) >= need
            return jnp.where(ge, j_lo, mid), jnp.where(ge, mid, j_hi)

        _, j_max = lax.fori_loop(0, n_keys.bit_length(), narrow,
                                 (jnp.full((tq, 1), -1.0, jnp.float32), jnp.full((tq, 1), n_keys - 1.0, jnp.float32)))
        return v, j_max

    thr, j_max = lax.cond(any_row((cnt_lo > k) & (n_vis > k)), break_ties,
                          lambda: (thr, jnp.full((tq, 1), float(n_keys), jnp.float32)))

    q = jnp.concatenate([qb_ref[0, :, h * d:(h + 1) * d] for h in range(nh)], axis=0)

    def att_body(j, carry):
        off = pl.multiple_of(j * kb, kb)
        sc = score_ref[j]
        kpos = (off + _iota((1, kb), 1)).astype(jnp.float32)
        madd = jnp.where((sc > thr) | ((sc == thr) & (kpos <= j_max)), 0.0, NEG_INF)
        if kv_t:
            return _flash_step(carry, q, wide(kb_ref), wide(vb_ref), madd, nh, scale=d ** -0.5, kv_t=True)
        return _flash_step(carry, q, kb_ref[0, pl.ds(off, kb), :], vb_ref[0, pl.ds(off, kb), :], madd, nh,
                           scale=d ** -0.5)

    o = _flash_finish(lax.fori_loop(0, nj, att_body, _flash_init(nh * tq, d)))
    o_ref[0] = jnp.concatenate([o[h * tq:(h + 1) * tq] for h in range(nh)], axis=1).astype(o_ref.dtype)


def dsa_attention(qb, qi, wi, kb_, vb, ki, *, tq, qpos0, n_keep, kv_t=False, tq_real=None):
    b, t, hd = qb.shape
    l = kb_.shape[1] * kb_.shape[3] if kv_t else kb_.shape[1]
    kb = min(KEY_CHUNK, l) if tq >= TQ else l
    assert l % kb == 0 and t % tq == 0 and (qpos0 + t - 1) // kb + 1 <= l // kb and (kb == l or not kv_t)
    steps = kb_.shape[1]
    kv_spec = lambda width: (pl.BlockSpec((1, steps, width, l // steps), lambda bi, i: (bi, 0, 0, 0)) if kv_t
                             else pl.BlockSpec((1, l, width), lambda bi, i: (bi, 0, 0)))
    return pl.pallas_call(
        functools.partial(_dsa_kernel, tq=tq, tq_real=tq if tq_real is None else tq_real, kb=kb, qpos0=qpos0,
                          n_keep=n_keep, kv_t=kv_t),
        grid=(b, t // tq),
        in_specs=[pl.BlockSpec((1, tq, hd), lambda bi, i: (bi, i, 0)),
                  pl.BlockSpec((1, tq, IDX_HEADS * IDX_DIM), lambda bi, i: (bi, i, 0)),
                  pl.BlockSpec((1, tq, IDX_HEADS), lambda bi, i: (bi, i, 0)),
                  kv_spec(HEAD_DIM_B), kv_spec(HEAD_DIM_B), kv_spec(IDX_DIM)],
        out_specs=pl.BlockSpec((1, tq, hd), lambda bi, i: (bi, i, 0)),
        out_shape=jax.ShapeDtypeStruct((b, t, hd), MXU_DTYPE),
        scratch_shapes=[pltpu.VMEM((l // kb, tq, kb), jnp.float32)],
        compiler_params=pltpu.CompilerParams(
            dimension_semantics=("parallel", "arbitrary"), vmem_limit_bytes=VMEM_LIMIT),
        name="dsa_attention",
    )(qb, qi, wi, kb_, vb, ki)


def _chunk_rows(tok_ref, sec, row0, n):
    d = HEAD_DIM_A
    first_half = _iota((n, KV_A), 1) < d
    pieces = [[] for _ in range(KV_GROUPS_A)]
    for j in range(0, CMP_STRIDE, 2):
        a, b = [tok_ref[sec, pl.ds(row0 + jj, n, stride=CMP_STRIDE), :] for jj in (j, j + 1)]
        pieces[0].append(jnp.where(first_half, a, pltpu.roll(b, d, 1)))
        pieces[1].append(jnp.where(first_half, pltpu.roll(a, d, 1), b))
    return [jnp.concatenate(p, axis=1) for p in pieces]


def _page_ring_step(pt_ref, cache_ref, buf_ref, sem_ref, *, pp, n_steps, n_pages):
    per_seq = n_steps + 1
    total = pl.num_programs(0) * per_seq
    g = pl.program_id(0) * per_seq + pl.program_id(1)

    def has_pages(gi):
        return (gi < total) & (gi % per_seq < n_steps)

    def page_copies(gi):
        b, s, slot = gi // per_seq, gi % per_seq, gi % GATHER_SLOTS
        return [pltpu.make_async_copy(cache_ref.at[pt_ref[b * n_pages + s * pp + r]], buf_ref.at[slot, r],
                                      sem_ref.at[slot, r]) for r in range(pp)]

    def start(gi):
        @pl.when(has_pages(gi))
        def _():
            for c in page_copies(gi):
                c.start()

    @pl.when(g == 0)
    def _():
        for ahead in range(GATHER_SLOTS - 1):
            start(g + ahead)

    start(g + GATHER_SLOTS - 1)

    @pl.when(has_pages(g))
    def _():
        for c in page_copies(g):
            c.wait()

    return g % GATHER_SLOTS


def _nsa_gather_kernel(pt_ref, cache_ref, tail_tok_ref, tail_t_ref, pe_ref, zt_ref, zb_ref, ks_ref, vs_ref,
                       tok_ref, buf_ref, sem_ref, *, pp, n_steps, n_pages, rows):
    slot = _page_ring_step(pt_ref, cache_ref, buf_ref, sem_ref, pp=pp, n_steps=n_steps, n_pages=n_pages)
    is_tail = pl.program_id(1) == n_steps
    z = [[[] for _ in range(KV_GROUPS_A)] for _ in range(2)]
    for r in range(pp):
        cs = slice(r * rows, (r + 1) * rows)
        ks_ref[0, 0, :, cs] = jnp.where(is_tail, tail_t_ref[0, :KV_A, cs],
                                     buf_ref[slot, r, 2 * KV_A:3 * KV_A, :]).astype(ks_ref.dtype)
        vs_ref[0, 0, :, cs] = jnp.where(is_tail, tail_t_ref[0, KV_A:, cs],
                                     buf_ref[slot, r, 3 * KV_A:, :]).astype(vs_ref.dtype)
        for sec in range(2):
            tok_ref[sec, cs, :] = jnp.where(is_tail, tail_tok_ref[0, sec, cs, :],
                                            buf_ref[slot, r, sec * KV_A:(sec + 1) * KV_A, :].T)
            for g, zg in enumerate(_chunk_rows(tok_ref, sec, r * rows, rows // CMP_STRIDE)):
                z[sec][g].append(zg)
    for sec in range(2):
        for g in range(KV_GROUPS_A):
            zf = jnp.concatenate(z[sec][g], axis=0)
            zt_ref[sec, 0, g] = (zf + pe_ref[sec, 0]).astype(zt_ref.dtype)
            zb_ref[sec, 0, g] = (zf + pe_ref[sec, 1]).astype(zb_ref.dtype)


def nsa_gather(cache_t, page_table, new_rows, cmp_pe):
    db, n_pages = page_table.shape
    width, rows = cache_t.shape[1:]
    pp = PAGES_PER_STEP
    n_steps = n_pages // pp
    l = (n_steps + 1) * pp * rows
    cps = pp * rows // CMP_STRIDE
    flat = CMP_STRIDE * HEAD_DIM_A
    pe = cmp_pe.reshape(2, 2, 1, flat).astype(jnp.float32)
    tail = pad_rows(new_rows, pp * rows)
    tail_tok = tail[:, :, :2 * KV_A].reshape(db, pp * rows, 2, KV_A).swapaxes(1, 2)
    tail_t = tail[:, :, 2 * KV_A:].swapaxes(1, 2)
    z_spec = pl.BlockSpec((2, 1, KV_GROUPS_A, cps, flat), lambda b, s, pt: (0, b, 0, s, 0))
    r_spec = pl.BlockSpec((1, 1, KV_A, pp * rows), lambda b, s, pt: (b, s, 0, 0))
    z_shape = jax.ShapeDtypeStruct((2, db, KV_GROUPS_A, l // CMP_STRIDE, flat), MXU_DTYPE)
    r_shape = jax.ShapeDtypeStruct((db, n_steps + 1, KV_A, pp * rows), MXU_DTYPE)
    return pl.pallas_call(
        functools.partial(_nsa_gather_kernel, pp=pp, n_steps=n_steps, n_pages=n_pages, rows=rows),
        grid_spec=pltpu.PrefetchScalarGridSpec(
            num_scalar_prefetch=1,
            grid=(db, n_steps + 1),
            in_specs=[pl.BlockSpec(memory_space=pl.ANY),
                      pl.BlockSpec((1, 2, pp * rows, KV_A), lambda b, s, pt: (b, 0, 0, 0)),
                      pl.BlockSpec((1, 2 * KV_A, pp * rows), lambda b, s, pt: (b, 0, 0)),
                      pl.BlockSpec((2, 2, 1, flat), lambda b, s, pt: (0, 0, 0, 0))],
            out_specs=[z_spec, z_spec, r_spec, r_spec],
            scratch_shapes=[pltpu.VMEM((2, pp * rows, KV_A), jnp.float32),
                            pltpu.VMEM((GATHER_SLOTS, pp, width, rows), jnp.float32),
                            pltpu.SemaphoreType.DMA((GATHER_SLOTS, pp))]),
        out_shape=[z_shape, z_shape, r_shape, r_shape],
        compiler_params=pltpu.CompilerParams(
            dimension_semantics=("arbitrary", "arbitrary"), vmem_limit_bytes=VMEM_LIMIT),
        name="nsa_gather",
    )(page_table.reshape(-1), cache_t, tail_tok, tail_t, pe)


def _dsa_gather_kernel(pt_ref, cache_ref, tail_ref, k_ref, v_ref, i_ref, buf_ref, sem_ref,
                       *, pp, n_steps, n_pages, rows):
    slot = _page_ring_step(pt_ref, cache_ref, buf_ref, sem_ref, pp=pp, n_steps=n_steps, n_pages=n_pages)
    is_tail = pl.program_id(1) == n_steps
    d = HEAD_DIM_B

    def emit(r, x):
        cs = slice(r * rows, (r + 1) * rows)
        k_ref[0, 0, :, cs] = x[:d].astype(k_ref.dtype)
        v_ref[0, 0, :, cs] = x[d:2 * d].astype(v_ref.dtype)
        i_ref[0, 0, :, cs] = x[2 * d:].astype(i_ref.dtype)

    @pl.when(is_tail)
    def _():
        for r in range(pp):
            emit(r, tail_ref[0, :, r * rows:(r + 1) * rows])

    @pl.when(jnp.logical_not(is_tail))
    def _():
        for r in range(pp):
            emit(r, buf_ref[slot, r])


def dsa_gather(cache_t, page_table, new_rows):
    db, n_pages = page_table.shape
    width, rows = cache_t.shape[1:]
    pp = PAGES_PER_STEP
    n_steps = n_pages // pp
    tail_t = pad_rows(new_rows, pp * rows).swapaxes(1, 2)
    out_spec = lambda w: pl.BlockSpec((1, 1, w, pp * rows), lambda b, s, pt: (b, s, 0, 0))
    widths = (HEAD_DIM_B, HEAD_DIM_B, IDX_DIM)
    return pl.pallas_call(
        functools.partial(_dsa_gather_kernel, pp=pp, n_steps=n_steps, n_pages=n_pages, rows=rows),
        grid_spec=pltpu.PrefetchScalarGridSpec(
            num_scalar_prefetch=1,
            grid=(db, n_steps + 1),
            in_specs=[pl.BlockSpec(memory_space=pl.ANY),
                      pl.BlockSpec((1, width, pp * rows), lambda b, s, pt: (b, 0, 0))],
            out_specs=[out_spec(w) for w in widths],
            scratch_shapes=[pltpu.VMEM((GATHER_SLOTS, pp, width, rows), jnp.float32),
                            pltpu.SemaphoreType.DMA((GATHER_SLOTS, pp))]),
        out_shape=[jax.ShapeDtypeStruct((db, n_steps + 1, w, pp * rows), MXU_DTYPE) for w in widths],
        compiler_params=pltpu.CompilerParams(
            dimension_semantics=("arbitrary", "arbitrary"), vmem_limit_bytes=VMEM_LIMIT),
        name="dsa_gather",
    )(page_table.reshape(-1), cache_t, tail_t)


def _compress_kernel(zt_ref, zb_ref, pe_ref, w1t_ref, w1b_ref, b1_ref, w2_ref, o_ref, ab_ref, *, ncp):
    dot = functools.partial(jnp.dot, preferred_element_type=jnp.float32)
    ch = zb_ref.shape[2]
    at = dot(zt_ref[0, 0], w1t_ref[0])
    ab_ref[:ch] = dot(zb_ref[0, 0], w1b_ref[0])
    pe_rows = jnp.broadcast_to(pe_ref[0, 1], (SUBLANE, pe_ref.shape[3])).astype(zb_ref.dtype)
    ab_ref[ch:] = dot(pe_rows, w1b_ref[0])
    h = jax.nn.gelu(at[:ncp] + ab_ref[pl.ds(1, ncp), :] + b1_ref[0])
    o_ref[0, 0, :ncp] = dot(h.astype(w2_ref.dtype), w2_ref[0]).astype(o_ref.dtype)
    if o_ref.shape[2] > ncp:
        o_ref[0, 0, ncp:] = jnp.zeros((o_ref.shape[2] - ncp, o_ref.shape[3]), o_ref.dtype)


def compress(zt, zb, cmp_pe, w1, b1, w2, n_keys):
    ncp = _round_up(-(-n_keys // CMP_STRIDE), BF16_ROWS)
    ncl = ncp if ncp <= LANE else _round_up(ncp, LANE)
    _, nb, ch, kdim = zt.shape
    hid = w1.shape[-1]
    d = w2.shape[-1]
    assert ch + SUBLANE >= ncp + 1 and ch % SUBLANE == 0
    w1 = w1.astype(MXU_DTYPE)
    pe = cmp_pe.reshape(2, 2, 1, kdim).astype(jnp.float32)
    z_spec = pl.BlockSpec((1, 1, ch, kdim), lambda s, n: (s, n, 0, 0))
    return pl.pallas_call(
        functools.partial(_compress_kernel, ncp=ncp),
        grid=(2, nb),
        in_specs=[z_spec, z_spec,
                  pl.BlockSpec((1, 2, 1, kdim), lambda s, n: (s, 0, 0, 0)),
                  pl.BlockSpec((1, kdim, hid), lambda s, n: (s, 0, 0)),
                  pl.BlockSpec((1, kdim, hid), lambda s, n: (s, 1, 0)),
                  pl.BlockSpec((1, 1, hid), lambda s, n: (s, 0, 0)),
                  pl.BlockSpec((1, hid, d), lambda s, n: (s, 0, 0))],
        out_specs=pl.BlockSpec((1, 1, ncl, d), lambda s, n: (s, n, 0, 0)),
        out_shape=jax.ShapeDtypeStruct((2, nb, ncl, d), MXU_DTYPE),
        scratch_shapes=[pltpu.VMEM((ch + SUBLANE, hid), jnp.float32)],
        compiler_params=pltpu.CompilerParams(
            dimension_semantics=("parallel", "parallel"), vmem_limit_bytes=VMEM_LIMIT),
        name="compress",
    )(zt, zb, pe, w1, w1, b1.reshape(2, 1, hid).astype(jnp.float32), w2.astype(MXU_DTYPE))


def pad_rows(a, n):
    return jnp.pad(a, [(0, 0), (0, n - a.shape[1])] + [(0, 0)] * (a.ndim - 2))


def _rope_tables(pos, d):
    half = d // ROT_FRACTION // 2
    inv = ROPE_THETA ** (-jnp.arange(half, dtype=jnp.float32) / half)
    ang = pos.astype(jnp.float32)[:, None] * inv[None, :]
    lane = jnp.arange(LANE) % d
    cos = jnp.cos(ang)[:, lane % half]
    sin = jnp.sin(ang)[:, lane % half]
    one, zero = jnp.ones_like(cos), jnp.zeros_like(cos)
    c = jnp.where(lane < 2 * half, cos, one)
    sa = jnp.where((lane >= half) & (lane < 2 * half), sin, zero)
    sb = jnp.where(lane < half, -sin, zero)
    return jnp.stack([c, sa, sb])


def _rope(x, t_ref, half):
    c, sa, sb = t_ref[0], t_ref[1], t_ref[2]
    outs = []
    for j in range(x.shape[1] // LANE):
        xs = x[:, j * LANE:(j + 1) * LANE]
        outs.append(xs * c + pltpu.roll(xs, half, 1) * sa + pltpu.roll(xs, LANE - half, 1) * sb)
    return outs[0] if len(outs) == 1 else jnp.concatenate(outs, axis=1)


_QA = N_HEADS_A * HEAD_DIM_A
_KVA = 6 * KV_A
_QB = N_HEADS_B * HEAD_DIM_B
_KVB = 2 * HEAD_DIM_B
_QI = IDX_HEADS * IDX_DIM
_MISC = IDX_DIM + IDX_HEADS + 3 * N_HEADS_A
PREP_WIDTH = _QA + _KVA + _QB + _KVB + _QI + _MISC
assert _MISC == LANE


def _prep_kernel(x_ref, t64_ref, t128_ref, pe_ref, qc_ref, qr_ref, qb_ref, qi_ref, nsa_ref, win_ref, dsa_ref,
                 ks_ref, vs_ref, kw_ref, vw_ref, kb_ref, vb_ref, ki_ref, wi_ref, gate_ref, *z_refs, tm):
    x = x_ref[...]
    o = 0
    qa = x[:, o:o + _QA]; o += _QA
    kva = x[:, o:o + _KVA]; o += _KVA
    qb = x[:, o:o + _QB]; o += _QB
    kvb = x[:, o:o + _KVB]; o += _KVB
    qi = x[:, o:o + _QI]; o += _QI
    misc = x[:, o:o + _MISC]
    h64, h128 = HEAD_DIM_A // ROT_FRACTION // 2, HEAD_DIM_B // ROT_FRACTION // 2
    sec = lambda i: kva[:, i * KV_A:(i + 1) * KV_A]
    mx = lambda a: a.astype(qc_ref.dtype)
    qc_ref[...] = mx(qa * HEAD_DIM_A ** -0.5)
    qr_ref[...] = mx(_rope(qa, t64_ref, h64) * HEAD_DIM_A ** -0.5)
    k_slc, k_win = _rope(sec(2), t64_ref, h64), _rope(sec(4), t64_ref, h64)
    nsa_ref[:, :2 * KV_A] = kva[:, :2 * KV_A]
    nsa_ref[:, 2 * KV_A:3 * KV_A] = k_slc
    nsa_ref[:, 3 * KV_A:] = sec(3)
    win_ref[:, :KV_A] = k_win
    win_ref[:, KV_A:] = sec(5)
    ks_ref[...], vs_ref[...], kw_ref[...], vw_ref[...] = mx(k_slc), mx(sec(3)), mx(k_win), mx(sec(5))
    qb_ref[...] = mx(_rope(qb, t128_ref, h128))
    k_b, v_b = _rope(kvb[:, :HEAD_DIM_B], t128_ref, h128), kvb[:, HEAD_DIM_B:]
    k_idx = _rope(misc, t64_ref, h64)[:, :IDX_DIM]
    dsa_ref[:, :HEAD_DIM_B] = k_b
    dsa_ref[:, HEAD_DIM_B:2 * HEAD_DIM_B] = v_b
    dsa_ref[:, 2 * HEAD_DIM_B:] = k_idx
    kb_ref[...], vb_ref[...], ki_ref[...] = mx(k_b), mx(v_b), mx(k_idx)
    qi_ref[...] = mx(_rope(qi, t64_ref, h64) * IDX_DIM ** -0.5)
    wi_ref[...] = misc[:, IDX_DIM:IDX_DIM + IDX_HEADS] * IDX_HEADS ** -0.5
    gate_ref[...] = jax.nn.sigmoid(misc[:, IDX_DIM + IDX_HEADS:])
    if z_refs:
        zt_ref, zb_ref, tok_ref = z_refs
        for s_ in range(2):
            tok_ref[s_] = sec(s_)
            for g, zg in enumerate(_chunk_rows(tok_ref, s_, 0, tm // CMP_STRIDE)):
                zt_ref[s_, 0, g] = (zg + pe_ref[s_, 0]).astype(zt_ref.dtype)
                zb_ref[s_, 0, g] = (zg + pe_ref[s_, 1]).astype(zb_ref.dtype)


def prep(proj, pos, cmp_pe, with_chunks):
    b, t, width = proj.shape
    assert width == PREP_WIDTH
    m = b * t
    tm = 256 if t % 256 == 0 else m
    assert m % tm == 0 and t % tm in (0, t)
    nt = max(t // tm, 1)
    flat = CMP_STRIDE * HEAD_DIM_A
    names = ['qc', 'qr', 'qb', 'qi', 'nsa_rows', 'win_rows', 'dsa_rows', 'ks', 'vs', 'kw', 'vw', 'kb', 'vb', 'ki',
             'wi', 'gates']
    widths = [_QA, _QA, _QB, _QI, NSA_SECTIONS * KV_A, 2 * KV_A, DSA_ROW, KV_A, KV_A, KV_A, KV_A, HEAD_DIM_B,
              HEAD_DIM_B, IDX_DIM, IDX_HEADS, 3 * N_HEADS_A]
    dtypes = [MXU_DTYPE] * 4 + [jnp.float32] * 3 + [MXU_DTYPE] * 7 + [jnp.float32] * 2
    row = lambda w: pl.BlockSpec((tm, w), lambda i: (i, 0))
    out_specs = [row(w) for w in widths]
    out_shape = [jax.ShapeDtypeStruct((m, w), dt) for w, dt in zip(widths, dtypes)]
    scratch = []
    if with_chunks:
        assert tm % CMP_STRIDE == 0 and t % tm == 0
        z_spec = pl.BlockSpec((2, 1, KV_GROUPS_A, tm // CMP_STRIDE, flat), lambda i: (0, i // nt, 0, i % nt, 0))
        z_shape = jax.ShapeDtypeStruct((2, b, KV_GROUPS_A, t // CMP_STRIDE, flat), MXU_DTYPE)
        out_specs += [z_spec, z_spec]
        out_shape += [z_shape, z_shape]
        names += ['zt', 'zb']
        scratch = [pltpu.VMEM((2, tm, KV_A), jnp.float32)]
    t_spec = pl.BlockSpec((3, tm, LANE), lambda i: (0, i, 0))
    outs = pl.pallas_call(
        functools.partial(_prep_kernel, tm=tm),
        grid=(m // tm,),
        in_specs=[row(width), t_spec, t_spec, pl.BlockSpec((2, 2, 1, flat), lambda i: (0, 0, 0, 0))],
        out_specs=out_specs,
        out_shape=out_shape,
        scratch_shapes=scratch,
        compiler_params=pltpu.CompilerParams(dimension_semantics=("parallel",), vmem_limit_bytes=VMEM_LIMIT),
        name="prep",
    )(proj.reshape(m, width), _rope_tables(pos, HEAD_DIM_A), _rope_tables(pos, HEAD_DIM_B),
      cmp_pe.reshape(2, 2, 1, flat).astype(jnp.float32))
    return {n: (o if o.ndim > 2 else o.reshape(b, t, o.shape[-1])) for n, o in zip(names, outs)}


def project(x, g, w_in, sizes):
    starts = [sum(sizes[:i]) for i in range(len(sizes))]
    qa, kva, ga, qb, kvb, qi, ki, wi, gm = [slice(o, o + n) for o, n in zip(starts, sizes)]
    w_bf = w_in.astype(MXU_DTYPE)
    w_prep = jnp.concatenate([w_bf[:, c] for c in (qa, kva, qb, kvb, qi, ki, wi, ga)], axis=1)
    return norm_matmul(x, g, w_prep), norm_matmul(x, g, w_bf[:, gm])


def mixer_prompt(x, g, w_in, cmp_pe, cmp_w1, cmp_b1, cmp_w2, sizes):
    b, s, _ = x.shape
    proj, gm = project(x, g, w_in, sizes)
    p = prep(proj, jnp.tile(jnp.arange(s, dtype=jnp.int32), b), cmp_pe, True)
    merge_bg = lambda a: a.reshape((2, b * KV_GROUPS_A) + a.shape[3:])
    kvc = compress(merge_bg(p['zt']), merge_bg(p['zb']), cmp_pe, cmp_w1, cmp_b1, cmp_w2, s)
    kvc = kvc.reshape(2, b, KV_GROUPS_A, kvc.shape[2], HEAD_DIM_A)
    o_nsa = nsa_attention(p['qc'], p['qr'], p['gates'], kvc, p['ks'], p['vs'], p['kw'], p['vw'],
                          tq=min(TQ, s), qpos0=0, wbase=0, n_keys=s)
    o_dsa = dsa_attention(p['qb'], p['qi'], p['wi'], p['kb'], p['vb'], p['ki'],
                          tq=min(TQ, s), qpos0=0, n_keep=min(DSA_TOPK, s // 4))
    nsa_rows = p['nsa_rows'].reshape(b, s, NSA_SECTIONS, KV_GROUPS_A, HEAD_DIM_A)
    win_state = p['win_rows'].reshape(b, s, 2, KV_GROUPS_A, HEAD_DIM_A)[:, -min(WINDOW, s):]
    return (o_nsa, o_dsa, gm), nsa_rows, win_state, p['dsa_rows']


def mixer_sample(x, g, cache_nsa, win_buf, cache_dsa, page_table, w_in, cmp_pe, cmp_w1, cmp_b1, cmp_w2, sizes):
    b, t, _ = x.shape
    page = cache_nsa.shape[1]
    past_len = page_table.shape[1] * page
    n_keys = past_len + t
    proj, gm = project(x, g, w_in, sizes)
    p = prep(proj, jnp.tile(past_len + jnp.arange(t, dtype=jnp.int32), b), cmp_pe, False)
    pad = lambda a: pad_rows(a, TQ_STEP)

    cache_nsa_t = jnp.transpose(cache_nsa, (0, 2, 3, 4, 1)).reshape(cache_nsa.shape[0], NSA_SECTIONS * KV_A, page)
    zt, zb, ks, vs = nsa_gather(cache_nsa_t, page_table, p['nsa_rows'], cmp_pe)
    merge_bg = lambda a: a.reshape((2, b * KV_GROUPS_A) + a.shape[3:])
    kvc = compress(merge_bg(zt), merge_bg(zb), cmp_pe, cmp_w1, cmp_b1, cmp_w2, n_keys)
    kvc = kvc.reshape(2, b, KV_GROUPS_A, kvc.shape[2], HEAD_DIM_A)
    w_len = win_buf.shape[1]
    win_new = p['win_rows'].reshape(b, t, 2, KV_GROUPS_A, HEAD_DIM_A)
    win_all = jnp.concatenate([win_buf, win_new], axis=1)
    win_pad = pad_rows(win_all, w_len + WIN_CHUNK).astype(MXU_DTYPE)
    kw = win_pad[:, :, 0].reshape(b, w_len + WIN_CHUNK, KV_A)
    vw = win_pad[:, :, 1].reshape(b, w_len + WIN_CHUNK, KV_A)
    o_nsa = nsa_attention(pad(p['qc']), pad(p['qr']), pad(p['gates']), kvc, ks, vs, kw, vw,
                          tq=TQ_STEP, qpos0=past_len, wbase=past_len - w_len, n_keys=n_keys, kv_t=True)

    kb_, vb, ki = dsa_gather(jnp.swapaxes(cache_dsa, 1, 2), page_table, p['dsa_rows'])
    o_dsa = dsa_attention(pad(p['qb']), pad(p['qi']), pad(p['wi']), kb_, vb, ki, tq=TQ_STEP, qpos0=past_len,
                          n_keep=min(DSA_TOPK, n_keys // 4), kv_t=True, tq_real=t)
    nsa_rows = p['nsa_rows'].reshape(b, t, NSA_SECTIONS, KV_GROUPS_A, HEAD_DIM_A)
    return (o_nsa, o_dsa, pad(gm)), nsa_rows, win_all[:, -w_len:], p['dsa_rows']


def kernel(x_prompt, x_sample, mem_prompt, cache_nsa_kv, state_nsa_win, cache_dsa_kv, cache_mem_kv, state_conv,
           page_table, norm_g, w_in, cmp_pe, cmp_w1, cmp_b1, cmp_w2, w_out_a, w_out_b, w_out, w_mem_q, w_mem_kv,
           w_mem_out, w_up, conv_w, conv_b, w_down, final_g):
    depth = w_in.shape[0]
    d_model = x_prompt.shape[-1]
    d_ff = w_down.shape[1]
    assert CONV_WIDTH == 3
    sizes = (N_HEADS_A * HEAD_DIM_A, 6 * KV_A, 3 * N_HEADS_A, N_HEADS_B * HEAD_DIM_B, 2 * HEAD_DIM_B,
             IDX_HEADS * IDX_DIM, IDX_DIM, IDX_HEADS, 2 * d_model)
    xp, xs = x_prompt, pad_rows(x_sample, TQ_STEP)
    t_step = x_sample.shape[1]
    nsa_p, nsa_s, win_p, win_s, dsa_p, dsa_s, mem_p, conv_p, conv_s = [], [], [], [], [], [], [], [], []
    for l in range(depth):
        assert l == depth - 1, "the fused FFN epilogue applies the final norm"
        branches_p, a, bwin, c = mixer_prompt(xp, norm_g[l, 0], w_in[l], cmp_pe[l], cmp_w1[l], cmp_b1[l], cmp_w2[l],
                                              sizes)
        nsa_p.append(a); win_p.append(bwin); dsa_p.append(c)
        branches_s, a, bwin, c = mixer_sample(xs[:, :t_step], norm_g[l, 0], cache_nsa_kv[l], state_nsa_win[l],
                                              cache_dsa_kv[l], page_table, w_in[l], cmp_pe[l], cmp_w1[l], cmp_b1[l],
                                              cmp_w2[l], sizes)
        nsa_s.append(a); win_s.append(bwin); dsa_s.append(c)
        kv_p = norm_matmul(mem_prompt, norm_g[l, 2], w_mem_kv[l])
        mem_p.append(kv_p.reshape(kv_p.shape[:2] + (2, MEM_HEADS, MEM_HEAD_DIM)))
        kv_s = cache_mem_kv[l].reshape(cache_mem_kv.shape[1:3] + (-1,))
        weights = (norm_g[l, 1], norm_g[l, 3], final_g, w_out_a[l], w_out_b[l], w_out[l], w_mem_q[l], w_mem_out[l],
                   w_up[l], conv_w[l], conv_b[l], w_down[l])
        xp, cp = dense_tail(xp, *branches_p, kv_p, jnp.zeros((xp.shape[0], CONV_WIDTH - 1, 2 * d_ff), xp.dtype),
                            xp.shape[1], *weights)
        xs, cs = dense_tail(xs, *branches_s, kv_s, state_conv[l], t_step, *weights)
        conv_p.append(cp); conv_s.append(cs)
    y_prompt, y_sample = xp, xs[:, :t_step]
    return (y_prompt, y_sample, jnp.stack(nsa_p), jnp.stack(nsa_s), jnp.stack(win_p), jnp.stack(win_s),
            jnp.stack(dsa_p), jnp.stack(dsa_s), jnp.stack(mem_p), jnp.stack(conv_p), jnp.stack(conv_s))
```
